```python
import jax, jax.numpy as jnp
from jax import lax
import numpy as np

D_MODEL = 2048
BATCH = 8
SEQ = 4096
DEPTH = 1

CONV_WIDTH = 3
CONV_GROUPS = 8
D_CONV = D_MODEL // 2
HGRN_HEADS = 8
HGRN_DK = 128
HGRN_DV = (D_MODEL // 2) // HGRN_HEADS
D_HGRN_K = HGRN_HEADS * HGRN_DK
D_HGRN_V = HGRN_HEADS * HGRN_DV
CHUNK = 64
D_FF = ((8 * D_MODEL // 3 + 255) // 256) * 256
N_MOD = 6
EPS = 1e-6
SPLITS = (D_CONV, D_CONV, D_CONV, D_HGRN_K, D_HGRN_K, D_HGRN_V, D_HGRN_V, D_MODEL, D_MODEL)
D_IN = sum(SPLITS)

kernel_name = "hybrid_conv_hgrn2_gated_merge_block"


def rmsnorm(x, g):
    xf = x.astype(jnp.float32)
    y = xf * lax.rsqrt(jnp.mean(xf * xf, axis=-1, keepdims=True) + EPS)
    return (y * g.astype(jnp.float32)).astype(x.dtype)


def causal_depthwise_conv(u, w):
    s = u.shape[1]
    upad = jnp.pad(u, ((0, 0), (CONV_WIDTH - 1, 0), (0, 0)))
    y = w[0] * upad[:, 0:s]
    for k in range(1, CONV_WIDTH):
        y = y + w[k] * upad[:, k:k + s]
    return y


def to_chunks(t):
    b, s, h, d = t.shape
    return t.reshape(b, s // CHUNK, CHUNK, h, d).transpose(1, 0, 3, 2, 4)


def from_chunks(t):
    n, b, h, c, d = t.shape
    return t.transpose(1, 0, 3, 2, 4).reshape(b, n * c, h, d)


def hgrn2_chunked(q, k, v, log_f):
    qc, kc, vc = to_chunks(q), to_chunks(k), to_chunks(v)
    bc = jnp.cumsum(to_chunks(log_f), axis=3)
    causal = jnp.tril(jnp.ones((CHUNK, CHUNK), dtype=bool))[:, :, None]
    b_, h_ = q.shape[0], q.shape[2]
    s0 = jnp.zeros((b_, h_, HGRN_DK, HGRN_DV), jnp.float32)

    def step(state, xs):
        qi, ki, vi, bi = xs
        b_last = bi[:, :, -1:, :]
        o_inter = jnp.einsum('bhtk,bhkv->bhtv', qi * jnp.exp(bi), state)
        diff = bi[:, :, :, None, :] - bi[:, :, None, :, :]
        decay = jnp.exp(jnp.where(causal, diff, -jnp.inf))
        scores = jnp.einsum('bhtk,bhtsk,bhsk->bhts', qi, decay, ki)
        o_intra = jnp.einsum('bhts,bhsv->bhtv', scores, vi)
        new_state = (jnp.swapaxes(jnp.exp(b_last), -1, -2) * state
                     + jnp.einsum('bhsk,bhsv->bhkv', ki * jnp.exp(b_last - bi), vi))
        return new_state, o_inter + o_intra

    _, o = lax.scan(step, s0, (qc, kc, vc, bc))
    return from_chunks(o)


def _fwd_setup_inputs(seed: int = 0) -> dict:
    key = jax.random.key(seed)
    ks = jax.random.split(key, 20)

    def nrm(k, shape, fan_in):
        return jax.random.normal(k, shape, jnp.float32) * (fan_in ** -0.5)

    def gain(k, shape):
        return 1.0 + 0.02 * jax.random.normal(k, shape, jnp.float32)

    return {
        "x": jax.random.normal(ks[0], (BATCH, SEQ, D_MODEL), jnp.float32),
        "c": jax.random.normal(ks[1], (BATCH, D_MODEL), jnp.float32),
        "w_ada": nrm(ks[2], (DEPTH, D_MODEL, N_MOD * D_MODEL), D_MODEL) * 0.5,
        "b_ada": 0.02 * jax.random.normal(ks[3], (DEPTH, N_MOD * D_MODEL), jnp.float32),
        "norm_mix_g": gain(ks[4], (DEPTH, D_MODEL)),
        "w_in": nrm(ks[5], (DEPTH, D_MODEL, D_IN), D_MODEL),
        "conv_w": nrm(ks[6], (DEPTH, CONV_WIDTH, D_CONV), CONV_WIDTH),
        "lb_param": jax.random.normal(ks[7], (DEPTH + 1, D_HGRN_K), jnp.float32),
        "gnorm_g": gain(ks[8], (DEPTH, HGRN_DV)),
        "w_conv_out": nrm(ks[9], (DEPTH, D_CONV, D_MODEL), D_CONV),
        "w_hgrn_out": nrm(ks[10], (DEPTH, D_HGRN_V, D_MODEL), D_HGRN_V),
        "w_o": nrm(ks[11], (DEPTH, D_MODEL, D_MODEL), D_MODEL),
        "norm_ffn_g": gain(ks[12], (DEPTH, D_MODEL)),
        "w_ffn_gate": nrm(ks[13], (DEPTH, D_MODEL, D_FF), D_MODEL),
        "w_ffn_up": nrm(ks[14], (DEPTH, D_MODEL, D_FF), D_MODEL),
        "w_ffn_down": nrm(ks[15], (DEPTH, D_FF, D_MODEL), D_FF),
        "norm_final_g": gain(ks[16], (D_MODEL,)),
    }


def _fwd_reference(x, c, w_ada, b_ada, norm_mix_g, w_in, conv_w, lb_param, gnorm_g,
              w_conv_out, w_hgrn_out, w_o, norm_ffn_g, w_ffn_gate, w_ffn_up, w_ffn_down,
              norm_final_g):
    b, s, _ = x.shape
    lb_all = jnp.cumsum(jax.nn.softmax(lb_param.astype(jnp.float32), axis=0), axis=0)
    split_idx = list(np.cumsum(SPLITS)[:-1])
    c_act = jax.nn.silu(c)

    for l in range(DEPTH):
        mod = c_act @ w_ada[l] + b_ada[l]
        sh_m, sc_m, gt_m, sh_f, sc_f, gt_f = [m[:, None, :] for m in jnp.split(mod, N_MOD, axis=-1)]

        h = rmsnorm(x, norm_mix_g[l]) * (1.0 + sc_m) + sh_m
        proj = h @ w_in[l]
        a_b, a_c, a_x, q, f_logit, i_in, g_out, gate_a, gate_b = jnp.split(proj, split_idx, axis=-1)

        y_a = (a_b * causal_depthwise_conv(a_c * a_x, conv_w[l])) @ w_conv_out[l]

        lb = lb_all[l]
        f = lb + (1.0 - lb) * jax.nn.sigmoid(f_logit.astype(jnp.float32))
        qh = jax.nn.silu(q.astype(jnp.float32)).reshape(b, s, HGRN_HEADS, HGRN_DK)
        kh = (1.0 - f).reshape(b, s, HGRN_HEADS, HGRN_DK)
        log_f = jnp.log(f).reshape(b, s, HGRN_HEADS, HGRN_DK)
        vh = i_in.astype(jnp.float32).reshape(b, s, HGRN_HEADS, HGRN_DV)
        o = hgrn2_chunked(qh, kh, vh, log_f)
        o = rmsnorm(o, gnorm_g[l]) * jax.nn.silu(g_out.astype(jnp.float32)).reshape(b, s, HGRN_HEADS, HGRN_DV)
        y_b = o.reshape(b, s, D_HGRN_V).astype(x.dtype) @ w_hgrn_out[l]

        merged = jax.nn.sigmoid(gate_a) * y_a + jax.nn.sigmoid(gate_b) * y_b
        x = x + gt_m * (merged @ w_o[l])

        h2 = rmsnorm(x, norm_ffn_g[l]) * (1.0 + sc_f) + sh_f
        ff = (jax.nn.silu(h2 @ w_ffn_gate[l]) * (h2 @ w_ffn_up[l])) @ w_ffn_down[l]
        x = x + gt_f * ff

    return rmsnorm(x, norm_final_g)


import jax as _jax
import jax.numpy as _jnp

TWIN_FORMAT = 'train_step'
FWD_PARAMS = ['x', 'c', 'w_ada', 'b_ada', 'norm_mix_g', 'w_in', 'conv_w', 'lb_param', 'gnorm_g', 'w_conv_out', 'w_hgrn_out', 'w_o', 'norm_ffn_g', 'w_ffn_gate', 'w_ffn_up', 'w_ffn_down', 'norm_final_g']
TWIN_WEIGHTS = ['w_ada', 'b_ada', 'norm_mix_g', 'w_in', 'conv_w', 'lb_param', 'gnorm_g', 'w_conv_out', 'w_hgrn_out', 'w_o', 'norm_ffn_g', 'w_ffn_gate', 'w_ffn_up', 'w_ffn_down', 'norm_final_g']
TWIN_DIFF_INPUT = 'x'
TWIN_INPUTS = ['x', 'c', 'w_ada', 'b_ada', 'norm_mix_g', 'w_in', 'conv_w', 'lb_param', 'gnorm_g', 'w_conv_out', 'w_hgrn_out', 'w_o', 'norm_ffn_g', 'w_ffn_gate', 'w_ffn_up', 'w_ffn_down', 'norm_final_g', 'loss_target', 'm_w_ada', 'm_b_ada', 'm_norm_mix_g', 'm_w_in', 'm_conv_w', 'm_lb_param', 'm_gnorm_g', 'm_w_conv_out', 'm_w_hgrn_out', 'm_w_o', 'm_norm_ffn_g', 'm_w_ffn_gate', 'm_w_ffn_up', 'm_w_ffn_down', 'm_norm_final_g', 'v_w_ada', 'v_b_ada', 'v_norm_mix_g', 'v_w_in', 'v_conv_w', 'v_lb_param', 'v_gnorm_g', 'v_w_conv_out', 'v_w_hgrn_out', 'v_w_o', 'v_norm_ffn_g', 'v_w_ffn_gate', 'v_w_ffn_up', 'v_w_ffn_down', 'v_norm_final_g']
TWIN_OUTPUTS = ['loss', 'grad_x', 'grad_w_ada', 'grad_b_ada', 'grad_norm_mix_g', 'grad_w_in', 'grad_conv_w', 'grad_lb_param', 'grad_gnorm_g', 'grad_w_conv_out', 'grad_w_hgrn_out', 'grad_w_o', 'grad_norm_ffn_g', 'grad_w_ffn_gate', 'grad_w_ffn_up', 'grad_w_ffn_down', 'grad_norm_final_g', 'delta_w_ada', 'delta_b_ada', 'delta_norm_mix_g', 'delta_w_in', 'delta_conv_w', 'delta_lb_param', 'delta_gnorm_g', 'delta_w_conv_out', 'delta_w_hgrn_out', 'delta_w_o', 'delta_norm_ffn_g', 'delta_w_ffn_gate', 'delta_w_ffn_up', 'delta_w_ffn_down', 'delta_norm_final_g', 'new_m_w_ada', 'new_m_b_ada', 'new_m_norm_mix_g', 'new_m_w_in', 'new_m_conv_w', 'new_m_lb_param', 'new_m_gnorm_g', 'new_m_w_conv_out', 'new_m_w_hgrn_out', 'new_m_w_o', 'new_m_norm_ffn_g', 'new_m_w_ffn_gate', 'new_m_w_ffn_up', 'new_m_w_ffn_down', 'new_m_norm_final_g', 'new_v_w_ada', 'new_v_b_ada', 'new_v_norm_mix_g', 'new_v_w_in', 'new_v_conv_w', 'new_v_lb_param', 'new_v_gnorm_g', 'new_v_w_conv_out', 'new_v_w_hgrn_out', 'new_v_w_o', 'new_v_norm_ffn_g', 'new_v_w_ffn_gate', 'new_v_w_ffn_up', 'new_v_w_ffn_down', 'new_v_norm_final_g']
TWIN_LEAF_KINDS = {'loss': 'loss', 'grad_x': 'grad_x', 'grad_w_ada': 'grad_w', 'grad_b_ada': 'grad_w', 'grad_norm_mix_g': 'grad_w', 'grad_w_in': 'grad_w', 'grad_conv_w': 'grad_w', 'grad_lb_param': 'grad_w', 'grad_gnorm_g': 'grad_w', 'grad_w_conv_out': 'grad_w', 'grad_w_hgrn_out': 'grad_w', 'grad_w_o': 'grad_w', 'grad_norm_ffn_g': 'grad_w', 'grad_w_ffn_gate': 'grad_w', 'grad_w_ffn_up': 'grad_w', 'grad_w_ffn_down': 'grad_w', 'grad_norm_final_g': 'grad_w', 'delta_w_ada': 'delta_w', 'delta_b_ada': 'delta_w', 'delta_norm_mix_g': 'delta_w', 'delta_w_in': 'delta_w', 'delta_conv_w': 'delta_w', 'delta_lb_param': 'delta_w', 'delta_gnorm_g': 'delta_w', 'delta_w_conv_out': 'delta_w', 'delta_w_hgrn_out': 'delta_w', 'delta_w_o': 'delta_w', 'delta_norm_ffn_g': 'delta_w', 'delta_w_ffn_gate': 'delta_w', 'delta_w_ffn_up': 'delta_w', 'delta_w_ffn_down': 'delta_w', 'delta_norm_final_g': 'delta_w', 'new_m_w_ada': 'new_m', 'new_m_b_ada': 'new_m', 'new_m_norm_mix_g': 'new_m', 'new_m_w_in': 'new_m', 'new_m_conv_w': 'new_m', 'new_m_lb_param': 'new_m', 'new_m_gnorm_g': 'new_m', 'new_m_w_conv_out': 'new_m', 'new_m_w_hgrn_out': 'new_m', 'new_m_w_o': 'new_m', 'new_m_norm_ffn_g': 'new_m', 'new_m_w_ffn_gate': 'new_m', 'new_m_w_ffn_up': 'new_m', 'new_m_w_ffn_down': 'new_m', 'new_m_norm_final_g': 'new_m', 'new_v_w_ada': 'new_v', 'new_v_b_ada': 'new_v', 'new_v_norm_mix_g': 'new_v', 'new_v_w_in': 'new_v', 'new_v_conv_w': 'new_v', 'new_v_lb_param': 'new_v', 'new_v_gnorm_g': 'new_v', 'new_v_w_conv_out': 'new_v', 'new_v_w_hgrn_out': 'new_v', 'new_v_w_o': 'new_v', 'new_v_norm_ffn_g': 'new_v', 'new_v_w_ffn_gate': 'new_v', 'new_v_w_ffn_up': 'new_v', 'new_v_w_ffn_down': 'new_v', 'new_v_norm_final_g': 'new_v'}


def _forward(args):
    return _fwd_reference(*[args[k] for k in FWD_PARAMS])


def _output_shape():
    def fwd():
        inp = _fwd_setup_inputs(0)
        return _fwd_reference(*[inp[k] for k in FWD_PARAMS])
    out = _jax.eval_shape(fwd)
    return out.shape, out.dtype

N_MICROBATCH = 1
ADAM_LR = 0.001
ADAM_B1 = 0.9
ADAM_B2 = 0.999
ADAM_EPS = 1e-08
ADAM_WD = 0.01
ADAM_STEP = 10
PER_EXAMPLE_BATCH_AXIS = {'x': 0, 'c': 0, 'loss_target': 0}
SHARED_INPUTS = []
_WEIGHT_DTYPES = {'w_ada': _jnp.float32, 'b_ada': _jnp.float32, 'norm_mix_g': _jnp.float32, 'w_in': _jnp.float32, 'conv_w': _jnp.float32, 'lb_param': _jnp.float32, 'gnorm_g': _jnp.float32, 'w_conv_out': _jnp.float32, 'w_hgrn_out': _jnp.float32, 'w_o': _jnp.float32, 'norm_ffn_g': _jnp.float32, 'w_ffn_gate': _jnp.float32, 'w_ffn_up': _jnp.float32, 'w_ffn_down': _jnp.float32, 'norm_final_g': _jnp.float32}
MOMENT_SCALE = {'w_ada': 2.807578e-02, 'b_ada': 4.772431e-02, 'norm_mix_g': 3.383325e-02, 'w_in': 1.522132e-02, 'conv_w': 2.640609e-02, 'lb_param': 8.882683e-04, 'gnorm_g': 5.062107e-02, 'w_conv_out': 1.840453e-02, 'w_hgrn_out': 9.494003e-03, 'w_o': 2.069565e-02, 'norm_ffn_g': 2.510185e-02, 'w_ffn_gate': 1.146622e-02, 'w_ffn_up': 1.109955e-02, 'w_ffn_down': 1.841579e-02, 'norm_final_g': 1.600155e+01}


def _to_microbatches(a, axis):
    t = _jnp.moveaxis(a, axis, 0)
    t = t.reshape((N_MICROBATCH, t.shape[0] // N_MICROBATCH) + t.shape[1:])
    return _jnp.moveaxis(t, 1, axis + 1)


def setup_inputs(seed: int = 0) -> dict:
    inp = _fwd_setup_inputs(seed)
    key = _jax.random.fold_in(_jax.random.key(seed), 7919)
    shape, _ = _output_shape()
    out = dict(inp)
    out["loss_target"] = _jax.random.normal(_jax.random.fold_in(key, 0), shape, _jnp.float32)
    for i, name in enumerate(TWIN_WEIGHTS):
        w = inp[name].astype(_jnp.float32)
        if MOMENT_SCALE is None:
            s = _jnp.sqrt(_jnp.mean(_jnp.square(w)) + 1e-30)
        else:
            s = MOMENT_SCALE[name]
        km, kv = _jax.random.split(_jax.random.fold_in(key, i + 1))
        out[name] = w
        out["m_" + name] = s * _jax.random.normal(km, w.shape, _jnp.float32)
        out["v_" + name] = (s * s) * _jax.random.uniform(kv, w.shape, _jnp.float32, 0.5, 1.5)
    if N_MICROBATCH > 1:
        for name, axis in PER_EXAMPLE_BATCH_AXIS.items():
            out[name] = _to_microbatches(out[name], axis)
    return {'x': out['x'], 'c': out['c'], 'w_ada': out['w_ada'], 'b_ada': out['b_ada'], 'norm_mix_g': out['norm_mix_g'], 'w_in': out['w_in'], 'conv_w': out['conv_w'], 'lb_param': out['lb_param'], 'gnorm_g': out['gnorm_g'], 'w_conv_out': out['w_conv_out'], 'w_hgrn_out': out['w_hgrn_out'], 'w_o': out['w_o'], 'norm_ffn_g': out['norm_ffn_g'], 'w_ffn_gate': out['w_ffn_gate'], 'w_ffn_up': out['w_ffn_up'], 'w_ffn_down': out['w_ffn_down'], 'norm_final_g': out['norm_final_g'], 'loss_target': out['loss_target'], 'm_w_ada': out['m_w_ada'], 'm_b_ada': out['m_b_ada'], 'm_norm_mix_g': out['m_norm_mix_g'], 'm_w_in': out['m_w_in'], 'm_conv_w': out['m_conv_w'], 'm_lb_param': out['m_lb_param'], 'm_gnorm_g': out['m_gnorm_g'], 'm_w_conv_out': out['m_w_conv_out'], 'm_w_hgrn_out': out['m_w_hgrn_out'], 'm_w_o': out['m_w_o'], 'm_norm_ffn_g': out['m_norm_ffn_g'], 'm_w_ffn_gate': out['m_w_ffn_gate'], 'm_w_ffn_up': out['m_w_ffn_up'], 'm_w_ffn_down': out['m_w_ffn_down'], 'm_norm_final_g': out['m_norm_final_g'], 'v_w_ada': out['v_w_ada'], 'v_b_ada': out['v_b_ada'], 'v_norm_mix_g': out['v_norm_mix_g'], 'v_w_in': out['v_w_in'], 'v_conv_w': out['v_conv_w'], 'v_lb_param': out['v_lb_param'], 'v_gnorm_g': out['v_gnorm_g'], 'v_w_conv_out': out['v_w_conv_out'], 'v_w_hgrn_out': out['v_w_hgrn_out'], 'v_w_o': out['v_w_o'], 'v_norm_ffn_g': out['v_norm_ffn_g'], 'v_w_ffn_gate': out['v_w_ffn_gate'], 'v_w_ffn_up': out['v_w_ffn_up'], 'v_w_ffn_down': out['v_w_ffn_down'], 'v_norm_final_g': out['v_norm_final_g']}


def _loss(weights, diff, rest, loss_target):
    with _jax.named_scope("forward"):
        args = {**rest, TWIN_DIFF_INPUT: diff, **{k: w.astype(_WEIGHT_DTYPES[k]) for k, w in weights.items()}}
        y = _forward(args)
    with _jax.named_scope("loss_head"):
        err = _jnp.square(y.astype(_jnp.float32) - loss_target)
        return 0.5 * _jnp.sum(_jnp.mean(err, axis=-1)) if err.ndim else 0.5 * err


def _adamw(w, g, m, v):
    m = ADAM_B1 * m + (1.0 - ADAM_B1) * g
    v = ADAM_B2 * v + (1.0 - ADAM_B2) * _jnp.square(g)
    m_hat = m / (1.0 - ADAM_B1 ** ADAM_STEP)
    v_hat = v / (1.0 - ADAM_B2 ** ADAM_STEP)
    delta = -ADAM_LR * (m_hat / (_jnp.sqrt(v_hat) + ADAM_EPS) + ADAM_WD * w)
    return delta, m, v


def reference(x, c, w_ada, b_ada, norm_mix_g, w_in, conv_w, lb_param, gnorm_g, w_conv_out, w_hgrn_out, w_o, norm_ffn_g, w_ffn_gate, w_ffn_up, w_ffn_down, norm_final_g, loss_target, m_w_ada, m_b_ada, m_norm_mix_g, m_w_in, m_conv_w, m_lb_param, m_gnorm_g, m_w_conv_out, m_w_hgrn_out, m_w_o, m_norm_ffn_g, m_w_ffn_gate, m_w_ffn_up, m_w_ffn_down, m_norm_final_g, v_w_ada, v_b_ada, v_norm_mix_g, v_w_in, v_conv_w, v_lb_param, v_gnorm_g, v_w_conv_out, v_w_hgrn_out, v_w_o, v_norm_ffn_g, v_w_ffn_gate, v_w_ffn_up, v_w_ffn_down, v_norm_final_g):
    given = dict(x=x, c=c, w_ada=w_ada, b_ada=b_ada, norm_mix_g=norm_mix_g, w_in=w_in, conv_w=conv_w, lb_param=lb_param, gnorm_g=gnorm_g, w_conv_out=w_conv_out, w_hgrn_out=w_hgrn_out, w_o=w_o, norm_ffn_g=norm_ffn_g, w_ffn_gate=w_ffn_gate, w_ffn_up=w_ffn_up, w_ffn_down=w_ffn_down, norm_final_g=norm_final_g, loss_target=loss_target, m_w_ada=m_w_ada, m_b_ada=m_b_ada, m_norm_mix_g=m_norm_mix_g, m_w_in=m_w_in, m_conv_w=m_conv_w, m_lb_param=m_lb_param, m_gnorm_g=m_gnorm_g, m_w_conv_out=m_w_conv_out, m_w_hgrn_out=m_w_hgrn_out, m_w_o=m_w_o, m_norm_ffn_g=m_norm_ffn_g, m_w_ffn_gate=m_w_ffn_gate, m_w_ffn_up=m_w_ffn_up, m_w_ffn_down=m_w_ffn_down, m_norm_final_g=m_norm_final_g, v_w_ada=v_w_ada, v_b_ada=v_b_ada, v_norm_mix_g=v_norm_mix_g, v_w_in=v_w_in, v_conv_w=v_conv_w, v_lb_param=v_lb_param, v_gnorm_g=v_gnorm_g, v_w_conv_out=v_w_conv_out, v_w_hgrn_out=v_w_hgrn_out, v_w_o=v_w_o, v_norm_ffn_g=v_norm_ffn_g, v_w_ffn_gate=v_w_ffn_gate, v_w_ffn_up=v_w_ffn_up, v_w_ffn_down=v_w_ffn_down, v_norm_final_g=v_norm_final_g)
    weights = {n: given[n] for n in TWIN_WEIGHTS}
    shared = {n: given[n] for n in SHARED_INPUTS}
    per_example = {n: given[n] for n in ['x', 'c']}
    grad_fn = _jax.value_and_grad(_loss, argnums=(0, 1))

    def one_microbatch(ex, loss_target):
        ex = dict(ex)
        diff = ex.pop(TWIN_DIFF_INPUT)
        return grad_fn(weights, diff, {**shared, **ex}, loss_target)

    if N_MICROBATCH == 1:
        loss, (grad_w, grad_x) = one_microbatch(per_example, given["loss_target"])
    else:
        def body(carry, xs):
            loss_sum, grad_sum = carry
            l_k, (gw_k, gx_k) = one_microbatch(xs[0], xs[1])
            with _jax.named_scope("update"):
                return (loss_sum + l_k, _jax.tree.map(_jnp.add, grad_sum, gw_k)), gx_k

        init = (_jnp.zeros((), _jnp.float32), _jax.tree.map(_jnp.zeros_like, weights))
        (loss, grad_w), grad_x = _jax.lax.scan(body, init, (per_example, given["loss_target"]))
    with _jax.named_scope("update"):
        delta_w, new_m, new_v = {}, {}, {}
        for n in TWIN_WEIGHTS:
            delta_w[n], new_m[n], new_v[n] = _adamw(weights[n], grad_w[n], given["m_" + n], given["v_" + n])
    return (loss, grad_x, *[grad_w[n] for n in TWIN_WEIGHTS], *[delta_w[n] for n in TWIN_WEIGHTS],
            *[new_m[n] for n in TWIN_WEIGHTS], *[new_v[n] for n in TWIN_WEIGHTS])
```

```python
import functools

import numpy as np
import jax
import jax.numpy as jnp
from jax import lax
from jax.experimental import pallas as pl
from jax.experimental.pallas import tpu as pltpu

f32, bf16 = jnp.float32, jnp.bfloat16
SDS = jax.ShapeDtypeStruct

EPS = 1e-6
HEADS, DK, CHUNK = 8, 128, 64
N_MOD = 6
NDEV = 8
ADAM_LR, ADAM_B1, ADAM_B2, ADAM_EPS, ADAM_WD, ADAM_STEP = 0.001, 0.9, 0.999, 1e-08, 0.01, 10
LEVELS = (32, 16, 8, 4, 2, 1)
V7X_VMEM_LIMIT = 56 * 1024 * 1024
HBM = pl.BlockSpec(memory_space=pltpu.HBM)
MESH = pl.DeviceIdType.MESH


def _params(sem):
    return pltpu.CompilerParams(dimension_semantics=sem, vmem_limit_bytes=V7X_VMEM_LIMIT)


def _sigmoid(x):
    return jax.nn.sigmoid(x)


def _dsilu(x, s):
    return s * (1.0 + x * (1.0 - s))


def _mesh_pos():
    x, y, c = lax.axis_index("x"), lax.axis_index("y"), lax.axis_index("c")
    return x, y, c


def _peer(pos, j):
    x, y, c = pos
    return (1 - x if j & 4 else x, 1 - y if j & 2 else y, 1 - c if j & 1 else c)


def _flat(pos):
    return 4 * pos[0] + 2 * pos[1] + pos[2]


def _exchange(name, arrs, all_to_all):
    n = len(arrs)
    out_shapes = [SDS(a.shape if all_to_all else (NDEV,) + a.shape, a.dtype) for a in arrs]

    def body(*refs):
        ins, outs = refs[:n], refs[n:2 * n]
        send_sems, recv_sems, local_sems = refs[2 * n:]
        pos = _mesh_pos()
        me = _flat(pos)

        def src(a, to):
            return ins[a].at[to] if all_to_all else ins[a]

        def copy(a, j, frm, to_pos):
            k = a * (NDEV - 1) + j - 1
            return pltpu.make_async_remote_copy(src_ref=src(a, _flat(to_pos)), dst_ref=outs[a].at[frm], send_sem=send_sems.at[k],
                                                recv_sem=recv_sems.at[k], device_id=to_pos, device_id_type=MESH)

        local = [pltpu.make_async_copy(src(a, me), outs[a].at[me], local_sems.at[a]) for a in range(n)]
        for cp in local:
            cp.start()
        sends = [copy(a, j, me, _peer(pos, j)) for j in range(1, NDEV) for a in range(n)]
        for cp in sends:
            cp.start()
        for j in range(1, NDEV):
            for a in range(n):
                copy(a, j, _flat(_peer(pos, j)), pos).wait_recv()
        for cp in sends:
            cp.wait_send()
        for cp in local:
            cp.wait()

    return pl.pallas_call(
        body, name=name, out_shape=out_shapes, in_specs=[HBM] * n, out_specs=[HBM] * n,
        scratch_shapes=[pltpu.SemaphoreType.DMA((n * (NDEV - 1),)), pltpu.SemaphoreType.DMA((n * (NDEV - 1),)),
                        pltpu.SemaphoreType.DMA((n,))],
    )(*arrs)


def _matmul(name, a, b, out_shape, out_dtype, grid, a_spec, b_spec, o_spec, dims):
    ksteps = grid[2]
    acc_shape = tuple(d for d in o_spec.block_shape if d is not None)

    def body(a_ref, b_ref, o_ref, *acc):
        prod = lax.dot_general(a_ref[...], b_ref[...], (dims, ((), ())), preferred_element_type=f32)
        if ksteps == 1:
            o_ref[...] = prod.astype(o_ref.dtype)
        else:
            k = pl.program_id(2)

            @pl.when(k == 0)
            def _():
                acc[0][...] = prod

            @pl.when(k > 0)
            def _():
                acc[0][...] += prod

            @pl.when(k == ksteps - 1)
            def _():
                o_ref[...] = acc[0][...].astype(o_ref.dtype)

    return pl.pallas_call(
        body, name=name, grid=grid, in_specs=[a_spec, b_spec], out_specs=o_spec, out_shape=SDS(out_shape, out_dtype),
        scratch_shapes=[] if ksteps == 1 else [pltpu.VMEM(acc_shape, f32)],
        compiler_params=_params(("parallel", "parallel", "arbitrary")),
    )(a, b)


NN, NT, TN = ((1,), (0,)), ((1,), (1,)), ((0,), (0,))


def _rowwise(name, fn, n_rows, tm, rows, fulls, row_outs, acc_outs=(), ncol=1):
    assert ncol == 1 or not acc_outs
    nr, nf, no, na = len(rows), len(fulls), len(row_outs), len(acc_outs)
    in_specs = [pl.BlockSpec((tm, w), functools.partial(lambda i, j, cb: (i, cb + j), cb=cb)) for (_, w, cb) in rows]
    in_specs += [pl.BlockSpec(a.shape, functools.partial(lambda i, j, nd: (0,) * nd, nd=a.ndim)) for a in fulls]
    out_shape = [SDS((n_rows, w), dt) for (w, dt) in row_outs] + [SDS(s, f32) for s in acc_outs]
    out_specs = [pl.BlockSpec((tm, w // ncol), lambda i, j: (i, j)) for (w, _) in row_outs]
    out_specs += [pl.BlockSpec(s, functools.partial(lambda i, j, nd: (0,) * nd, nd=len(s))) for s in acc_outs]

    def body(*refs):
        if na:
            @pl.when(pl.program_id(0) == 0)
            def _():
                for r in refs[nr + nf + no:]:
                    r[...] = jnp.zeros(r.shape, r.dtype)
        fn(refs[:nr], refs[nr:nr + nf], refs[nr + nf:nr + nf + no], refs[nr + nf + no:])

    return pl.pallas_call(
        body, name=name, grid=(n_rows // tm, ncol), in_specs=in_specs, out_specs=out_specs, out_shape=out_shape,
        compiler_params=_params(("arbitrary" if na else "parallel", "arbitrary" if na else "parallel")),
    )(*[r[0] for r in rows], *fulls)


def _rms(x):
    r = lax.rsqrt(jnp.mean(x * x, axis=-1, keepdims=True) + EPS)
    return r, x * r


def _colsum(v):
    return jnp.sum(v, axis=0, keepdims=True)


def _shift_down(u, row, k):
    return jnp.where(row >= k, pltpu.roll(u, k, 0), 0.0)


def _shift_up(u, row, k):
    n = u.shape[0]
    return jnp.where(row < n - k, pltpu.roll(u, n - k, 0), 0.0)


def _conv_fwd(proj, conv_w, s, dc):
    nb = dc // 128

    def body(ab_ref, ac_ref, ax_ref, w_ref, z_ref):
        u = ac_ref[...] * ax_ref[...]
        row = lax.broadcasted_iota(jnp.int32, u.shape, 0)
        w = w_ref[...]
        cv = w[0:1] * _shift_down(u, row, 2) + w[1:2] * _shift_down(u, row, 1) + w[2:3] * u
        z_ref[...] = (ab_ref[...] * cv).astype(z_ref.dtype)

    col = lambda off: pl.BlockSpec((s, 128), functools.partial(lambda j, off: (0, off + j), off=off))
    return pl.pallas_call(
        body, name="conv_fwd", grid=(nb,), in_specs=[col(0), col(nb), col(2 * nb), pl.BlockSpec((3, 128), lambda j: (0, j))],
        out_specs=pl.BlockSpec((s, 128), lambda j: (0, j)), out_shape=SDS((s, dc), bf16), compiler_params=_params(("parallel",)),
    )(proj, proj, proj, conv_w)


def _conv_bwd(proj, conv_w, dz, s, dc):
    nb = dc // 128

    def body(ab_ref, ac_ref, ax_ref, w_ref, dz_ref, dab_ref, dac_ref, dax_ref, dw_ref):
        ab, ac, ax, dzv = ab_ref[...], ac_ref[...], ax_ref[...], dz_ref[...]
        u = ac * ax
        row = lax.broadcasted_iota(jnp.int32, u.shape, 0)
        w = w_ref[...]
        u1, u2 = _shift_down(u, row, 1), _shift_down(u, row, 2)
        cv = w[0:1] * u2 + w[1:2] * u1 + w[2:3] * u
        dcv = dzv * ab
        dab_ref[...] = (dzv * cv).astype(dab_ref.dtype)
        du = w[2:3] * dcv + w[1:2] * _shift_up(dcv, row, 1) + w[0:1] * _shift_up(dcv, row, 2)
        dac_ref[...] = (du * ax).astype(dac_ref.dtype)
        dax_ref[...] = (du * ac).astype(dax_ref.dtype)
        dw_ref[0:1, :] = _colsum(dcv * u2)
        dw_ref[1:2, :] = _colsum(dcv * u1)
        dw_ref[2:3, :] = _colsum(dcv * u)

    col = lambda off: pl.BlockSpec((s, 128), functools.partial(lambda j, off: (0, off + j), off=off))
    blk = pl.BlockSpec((s, 128), lambda j: (0, j))
    return pl.pallas_call(
        body, name="conv_bwd", grid=(nb,),
        in_specs=[col(0), col(nb), col(2 * nb), pl.BlockSpec((3, 128), lambda j: (0, j)), blk],
        out_specs=[blk, blk, blk, pl.BlockSpec((3, 128), lambda j: (0, j))],
        out_shape=[SDS((s, dc), bf16)] * 3 + [SDS((3, dc), f32)], compiler_params=_params(("parallel",)),
    )(proj, proj, proj, conv_w, dz)


def _level_masks():
    t = np.arange(CHUNK)[:, None]
    s = np.arange(CHUNK)[None, :]
    m = np.stack([((t & h) != 0) & ((s & h) == 0) & (t // (2 * h) == s // (2 * h)) for h in LEVELS]).astype(np.float32)
    return jnp.asarray(m), jnp.asarray(m.transpose(0, 2, 1))


def _cumsum_rows(x, row):
    for sh in (1, 2, 4, 8, 16, 32):
        x = x + jnp.where(row >= sh, pltpu.roll(x, sh, 0), 0.0)
    return x


def _rev_cumsum_rows(x, row):
    n = x.shape[0]
    for sh in (1, 2, 4, 8, 16, 32):
        x = x + jnp.where(row < n - sh, pltpu.roll(x, n - sh, 0), 0.0)
    return x


def _chunk_terms(qp, fl, lb):
    row = lax.broadcasted_iota(jnp.int32, qp.shape, 0)
    sig = _sigmoid(fl)
    f = lb + (1.0 - lb) * sig
    k = 1.0 - f
    sq = _sigmoid(qp)
    qh = qp * sq
    b = _cumsum_rows(jnp.log(f), row)
    sub = lax.broadcasted_iota(jnp.int32, (CHUNK // 8, 8, DK), 1)
    b8 = b.reshape(CHUNK // 8, 8, DK)
    us, exs, ups = [], [], []
    for m in LEVELS:
        sb = 2 * m
        if sb >= 8:
            b3 = b.reshape(CHUNK // sb, sb, DK)
            bref = jnp.broadcast_to(b3[:, m - 1:m, :], b3.shape).reshape(CHUNK, DK)
        else:
            bref8 = None
            for j in range(8 // sb):
                cand = jnp.broadcast_to(b8[:, j * sb + m - 1:j * sb + m, :], b8.shape)
                bref8 = cand if bref8 is None else jnp.where(sub >= j * sb, cand, bref8)
            bref = bref8.reshape(CHUNK, DK)
        up = (row & m) != 0
        ex = jnp.exp(jnp.where(up, b - bref, bref - b))
        us.append((jnp.where(up, qh, k) * ex).astype(bf16))
        exs.append(ex)
        ups.append(up)
    blast = b[CHUNK - 1:CHUNK, :]
    eb, ebl = jnp.exp(b), jnp.exp(blast - b)
    return dict(sig=sig, f=f, k=k, sq=sq, qh=qh, u=jnp.stack(us), ex=exs, up=ups, eb=eb, ebl=ebl, qt=qh * eb, kt=k * ebl,
                el=jnp.exp(blast), row=row)


def _scores(t, mask):
    pl_ = jnp.einsum("ltk,lsk->lts", t["u"], t["u"], preferred_element_type=f32)
    p = jnp.sum(pl_ * mask, axis=0)
    r = lax.broadcasted_iota(jnp.int32, (CHUNK, CHUNK), 0)
    c = lax.broadcasted_iota(jnp.int32, (CHUNK, CHUNK), 1)
    diag = jnp.sum(t["qh"] * t["k"], axis=-1, keepdims=True)
    return p + jnp.where(r == c, diag, 0.0)


def _hgrn_fwd(proj, lb_param, gnorm, s, dv_total, tb):
    nchunk = tb // CHUNK
    masks, _ = _level_masks()
    q0, f0, v0, g0 = 3 * HEADS, 4 * HEADS, 5 * HEADS, 6 * HEADS

    def body(q_ref, f_ref, v_ref, g_ref, lb_ref, gn_ref, mask_ref, og_ref, o_ref, st_ref, state):
        @pl.when(pl.program_id(1) == 0)
        def _():
            state[...] = jnp.zeros_like(state)

        lbp = lb_ref[...]
        lb = _sigmoid(lbp[0:1, :] - lbp[1:2, :])
        mask = mask_ref[...]
        for i in range(nchunk):
            rs = pl.ds(i * CHUNK, CHUNK)
            t = _chunk_terms(q_ref[rs, :], f_ref[rs, :], lb)
            v = v_ref[rs, :]
            vb = v.astype(bf16)
            st = state[...]
            st_ref[i] = st
            p = _scores(t, mask)
            o = jnp.dot(p.astype(bf16), vb, preferred_element_type=f32)
            o += lax.dot_general(t["qt"].astype(bf16), st.astype(bf16), (NT, ((), ())), preferred_element_type=f32)
            state[...] = st * t["el"] + lax.dot_general(vb, t["kt"].astype(bf16), (TN, ((), ())), preferred_element_type=f32)
            o_ref[rs, :] = o
            r, oh = _rms(o)
            g = g_ref[rs, :]
            og_ref[rs, :] = (oh * gn_ref[...] * (g * _sigmoid(g))).astype(og_ref.dtype)

    col = lambda off: pl.BlockSpec((tb, DK), functools.partial(lambda h, t, off: (t, off + h), off=off))
    blk = pl.BlockSpec((tb, DK), lambda h, t: (t, h))
    return pl.pallas_call(
        body, name="hgrn_fwd", grid=(HEADS, s // tb),
        in_specs=[col(q0), col(f0), col(v0), col(g0), pl.BlockSpec((2, DK), lambda h, t: (0, h)),
                  pl.BlockSpec((1, DK), lambda h, t: (0, 0)), pl.BlockSpec(masks.shape, lambda h, t: (0, 0, 0))],
        out_specs=[blk, blk, pl.BlockSpec((nchunk, None, DK, DK), lambda h, t: (t, h, 0, 0))],
        out_shape=[SDS((s, dv_total), bf16), SDS((s, dv_total), f32), SDS((s // CHUNK, HEADS, DK, DK), f32)],
        scratch_shapes=[pltpu.VMEM((DK, DK), f32)], compiler_params=_params(("parallel", "arbitrary")),
    )(proj, proj, proj, proj, lb_param, gnorm, masks)


def _hgrn_bwd(proj, lb_param, gnorm, o_saved, states, dog, s, dv_total, tb):
    nchunk = tb // CHUNK
    nt = s // tb
    nc_total = s // CHUNK
    masks, masks_t = _level_masks()
    q0, f0, v0, g0 = 3 * HEADS, 4 * HEADS, 5 * HEADS, 6 * HEADS

    def body(q_ref, f_ref, v_ref, g_ref, lb_ref, gn_ref, mask_ref, maskt_ref, o_ref, dog_ref, st_ref, stn_ref,
             dq_ref, df_ref, dv_ref, dg_ref, dlb_ref, dgn_ref, gstate):
        h_id, t_id = pl.program_id(0), pl.program_id(1)

        @pl.when(t_id == 0)
        def _():
            gstate[...] = jnp.zeros_like(gstate)
            dlb_ref[...] = jnp.zeros_like(dlb_ref)

        @pl.when((t_id == 0) & (h_id == 0))
        def _():
            dgn_ref[...] = jnp.zeros_like(dgn_ref)

        lbp = lb_ref[...]
        lb = _sigmoid(lbp[0:1, :] - lbp[1:2, :])
        mask, maskt = mask_ref[...], maskt_ref[...]
        gn = gn_ref[...]
        for i in reversed(range(nchunk)):
            rs = pl.ds(i * CHUNK, CHUNK)
            qp, fl, v, g = q_ref[rs, :], f_ref[rs, :], v_ref[rs, :], g_ref[rs, :]
            t = _chunk_terms(qp, fl, lb)
            vb = v.astype(bf16)
            st0 = st_ref[i]
            st1 = st_ref[i + 1] if i + 1 < nchunk else stn_ref[0]
            gt = gstate[...]
            o = o_ref[rs, :]
            r, oh = _rms(o)
            sg = _sigmoid(g)
            dog_v = dog_ref[rs, :]
            dg_ref[rs, :] = (dog_v * (oh * gn) * _dsilu(g, sg)).astype(dg_ref.dtype)
            don = dog_v * (g * sg)
            dgn_ref[...] += _colsum(don * oh)
            doh = don * gn
            do = r * (doh - oh * jnp.mean(doh * oh, axis=-1, keepdims=True))
            dob = do.astype(bf16)
            d = lax.dot_general(dob, vb, (NT, ((), ())), preferred_element_type=f32)
            dt = lax.dot_general(vb, dob, (NT, ((), ())), preferred_element_type=f32)
            z = (mask * d[None] + maskt * dt[None]).astype(bf16)
            rr = jnp.einsum("lts,lsk->ltk", z, t["u"], preferred_element_type=f32)
            dq = jnp.zeros((CHUNK, DK), f32)
            dk = jnp.zeros((CHUNK, DK), f32)
            qdk = jnp.zeros((CHUNK, DK), f32)
            for li in range(len(LEVELS)):
                du = t["ex"][li] * rr[li]
                dq += jnp.where(t["up"][li], du, 0.0)
                dk += jnp.where(t["up"][li], 0.0, du)
                e = t["u"][li].astype(f32) * rr[li]
                qdk += jnp.where(t["up"][li], e, -e)
            dd = jnp.sum(do * v, axis=-1, keepdims=True)
            dq += dd * t["k"]
            dk += dd * t["qh"]
            gtb = gt.astype(bf16)
            ktb, qtb = t["kt"].astype(bf16), t["qt"].astype(bf16)
            dq_in = jnp.dot(dob, st0.astype(bf16), preferred_element_type=f32)
            dk_in = jnp.dot(vb, gtb, preferred_element_type=f32)
            dq += t["eb"] * dq_in
            dk += t["ebl"] * dk_in
            qdk += qtb.astype(f32) * dq_in - ktb.astype(f32) * dk_in
            p = _scores(t, mask)
            dvv = lax.dot_general(p.astype(bf16), dob, (TN, ((), ())), preferred_element_type=f32)
            dvv += lax.dot_general(ktb, gtb, (NT, ((), ())), preferred_element_type=f32)
            dv_ref[rs, :] = dvv.astype(dv_ref.dtype)
            a_end = _colsum(gtb.astype(f32) * st1)
            dlf = _rev_cumsum_rows(qdk, t["row"]) + a_end
            dfv = dlf / t["f"] - dk
            df_ref[rs, :] = (dfv * (1.0 - lb) * t["sig"] * (1.0 - t["sig"])).astype(df_ref.dtype)
            dlb_ref[...] += _colsum(dfv * (1.0 - t["sig"]))
            dq_ref[rs, :] = (dq * _dsilu(qp, t["sq"])).astype(dq_ref.dtype)
            gstate[...] = gt * t["el"] + lax.dot_general(dob, qtb, (TN, ((), ())), preferred_element_type=f32)

    rev = lambda t: nt - 1 - t
    col = lambda off: pl.BlockSpec((tb, DK), functools.partial(lambda h, t, off: (rev(t), off + h), off=off))
    blk = pl.BlockSpec((tb, DK), lambda h, t: (rev(t), h))
    nxt = lambda h, t: (jnp.minimum((rev(t) + 1) * nchunk, nc_total - 1), h, 0, 0)
    return pl.pallas_call(
        body, name="hgrn_bwd", grid=(HEADS, nt),
        in_specs=[col(q0), col(f0), col(v0), col(g0), pl.BlockSpec((2, DK), lambda h, t: (0, h)),
                  pl.BlockSpec((1, DK), lambda h, t: (0, 0)), pl.BlockSpec(masks.shape, lambda h, t: (0, 0, 0)),
                  pl.BlockSpec(masks.shape, lambda h, t: (0, 0, 0)), blk, blk,
                  pl.BlockSpec((nchunk, None, DK, DK), lambda h, t: (rev(t), h, 0, 0)),
                  pl.BlockSpec((1, None, DK, DK), nxt)],
        out_specs=[blk, blk, blk, blk, pl.BlockSpec((1, DK), lambda h, t: (0, h)), pl.BlockSpec((1, DK), lambda h, t: (0, 0))],
        out_shape=[SDS((s, dv_total), bf16)] * 4 + [SDS((1, HEADS * DK), f32), SDS((1, DK), f32)],
        scratch_shapes=[pltpu.VMEM((DK, DK), f32)], compiler_params=_params(("arbitrary", "arbitrary")),
    )(proj, proj, proj, proj, lb_param, gnorm, masks, masks_t, o_saved, dog, states, states)


def _sum_parts(name, parts, tr):
    _, r, c = parts.shape

    def body(p_ref, o_ref):
        acc = p_ref[0].astype(f32)
        for d in range(1, NDEV):
            acc = acc + p_ref[d].astype(f32)
        o_ref[...] = acc

    return pl.pallas_call(
        body, name=name, grid=(r // tr,), in_specs=[pl.BlockSpec((NDEV, tr, c), lambda i: (0, i, 0))],
        out_specs=pl.BlockSpec((tr, c), lambda i: (i, 0)), out_shape=SDS((r, c), f32), compiler_params=_params(("parallel",)),
    )(parts)


def _adamw(name, g, w, m, v):
    r, c = w.shape
    tr = r
    for cand in (256, 128, 64, 32, 16, 8):
        if r % cand == 0 and r > cand:
            tr = cand
            break

    def body(g_ref, w_ref, m_ref, v_ref, d_ref, mo_ref, vo_ref):
        gv = g_ref[...]
        mn = ADAM_B1 * m_ref[...] + (1.0 - ADAM_B1) * gv
        vn = ADAM_B2 * v_ref[...] + (1.0 - ADAM_B2) * jnp.square(gv)
        m_hat = mn / (1.0 - ADAM_B1 ** ADAM_STEP)
        v_hat = vn / (1.0 - ADAM_B2 ** ADAM_STEP)
        d_ref[...] = -ADAM_LR * (m_hat / (jnp.sqrt(v_hat) + ADAM_EPS) + ADAM_WD * w_ref[...])
        mo_ref[...] = mn
        vo_ref[...] = vn

    blk = pl.BlockSpec((tr, c), lambda i: (i, 0))
    return pl.pallas_call(
        body, name=name, grid=(r // tr,), in_specs=[blk] * 4, out_specs=[blk] * 3, out_shape=[SDS((r, c), f32)] * 3,
        compiler_params=_params(("parallel",)),
    )(g, w, m, v)


def _local_step(x, tgt, mod, g_mix, g_ffn, g_fin, lb_param, gnorm, conv_w, w_in, w_co, w_ho, w_o, w_gt, w_ut, w_d):
    s, d = x.shape
    nsh, _, win_sh = w_in.shape
    dc = w_co.shape[1]
    dff_ = w_d.shape[0]
    tm = min(512, s)
    te = min(256, s)
    tb = min(256, s)
    mt = s // tm
    sh_m, sc_m, gt_m, sh_f, sc_f, gt_f = [mod[i] for i in range(N_MOD)]
    ffb = dff_ // 4
    dh2 = d // 2

    def e1(rows, fulls, outs, accs):
        xv = rows[0][...]
        g, sc, sh = [r[...] for r in fulls]
        _, xh = _rms(xv)
        outs[0][...] = (xh * g * (1.0 + sc) + sh).astype(bf16)

    h, = _rowwise("prenorm_mix", e1, s, te, [(x, d, 0)], [g_mix, sc_m, sh_m], [(d, bf16)])
    proj = _matmul("proj", h, w_in, (s, nsh * win_sh), f32, (nsh, mt, 1), pl.BlockSpec((tm, d), lambda j, i, k: (i, 0)),
                   pl.BlockSpec((None, d, win_sh), lambda j, i, k: (j, 0, 0)), pl.BlockSpec((tm, win_sh), lambda j, i, k: (i, j)), NN)
    z_a = _conv_fwd(proj, conv_w, s, dc)
    og, o_saved, states = _hgrn_fwd(proj, lb_param, gnorm, s, dc, tb)
    tm2 = min(1024, s)

    def out_proj(name, a, w):
        return _matmul(name, a, w, (s, d), f32, (nsh, s // tm2, 1), pl.BlockSpec((tm2, dc), lambda j, i, k: (i, 0)),
                       pl.BlockSpec((None, dc, d // nsh), lambda j, i, k: (j, 0, 0)),
                       pl.BlockSpec((tm2, d // nsh), lambda j, i, k: (i, j)), NN)

    y_a = out_proj("conv_out", z_a, w_co)
    y_b = out_proj("hgrn_out", og, w_ho)
    ga0, gb0 = 7, 9

    def e5(rows, fulls, outs, accs):
        ya, yb, ga, gb = [r[...] for r in rows]
        outs[0][...] = (_sigmoid(ga) * ya + _sigmoid(gb) * yb).astype(bf16)

    merged, = _rowwise("merge", e5, s, te, [(y_a, dh2, 0), (y_b, dh2, 0), (proj, dh2, ga0), (proj, dh2, gb0)], [], [(d, bf16)], ncol=2)
    mo = _matmul("mix_out", merged, w_o, (s, d), f32, (2, mt, 1), pl.BlockSpec((tm, d), lambda j, i, k: (i, 0)),
                 pl.BlockSpec((d, dh2), lambda j, i, k: (0, j)), pl.BlockSpec((tm, dh2), lambda j, i, k: (i, j)), NN)

    def e6(rows, fulls, outs, accs):
        xv, mov = rows[0][...], rows[1][...]
        gt, g, sc, sh = [r[...] for r in fulls]
        x1 = xv + gt * mov
        outs[0][...] = x1
        _, xh = _rms(x1)
        outs[1][...] = (xh * g * (1.0 + sc) + sh).astype(bf16)

    x1, h2 = _rowwise("prenorm_ffn", e6, s, te, [(x, d, 0), (mo, d, 0)], [gt_m, g_ffn, sc_f, sh_f], [(d, f32), (d, bf16)])

    def ffn_in(name, w):
        return _matmul(name, h2, w, (s, dff_), f32, (4, mt, 1), pl.BlockSpec((tm, d), lambda j, i, k: (i, 0)),
                       pl.BlockSpec((ffb, d), lambda j, i, k: (j, 0)), pl.BlockSpec((tm, ffb), lambda j, i, k: (i, j)), NT)

    gg = ffn_in("ffn_gate", w_gt)
    uu = ffn_in("ffn_up", w_ut)

    def e8(rows, fulls, outs, accs):
        gv, uv = rows[0][...], rows[1][...]
        outs[0][...] = (gv * _sigmoid(gv) * uv).astype(bf16)

    act, = _rowwise("swiglu", e8, s, te, [(gg, ffb, 0), (uu, ffb, 0)], [], [(dff_, bf16)], ncol=4)
    ff = _matmul("ffn_down", act, w_d, (s, d), f32, (2, mt, 4), pl.BlockSpec((tm, ffb), lambda j, i, k: (i, k)),
                 pl.BlockSpec((ffb, dh2), lambda j, i, k: (k, j)), pl.BlockSpec((tm, dh2), lambda j, i, k: (i, j)), NN)

    def e9(rows, fulls, outs, accs):
        x1v, ffv, tv = [r[...] for r in rows]
        gt, gf = fulls[0][...], fulls[1][...]
        x2 = x1v + gt * ffv
        r, xh = _rms(x2)
        err = xh * gf - tv
        accs[0][...] += 0.5 * jnp.sum(jnp.mean(err * err, axis=-1, keepdims=True), axis=0, keepdims=True)
        dy = err / d
        accs[1][...] += _colsum(dy * xh)
        dxh = dy * gf
        dx2 = r * (dxh - xh * jnp.mean(dxh * xh, axis=-1, keepdims=True))
        outs[0][...] = dx2
        outs[1][...] = (dx2 * gt).astype(bf16)
        accs[2][...] += _colsum(dx2 * ffv)

    dx2, dff, loss_acc, dg_fin, dgt_f = _rowwise("loss_head", e9, s, te, [(x1, d, 0), (ff, d, 0), (tgt, d, 0)], [gt_f, g_fin],
                                                 [(d, f32), (d, bf16)], [(1, 128), (1, d), (1, d)])
    dact = _matmul("d_ffn_down_in", dff, w_d, (s, dff_), f32, (4, mt, 1), pl.BlockSpec((tm, d), lambda j, i, k: (i, 0)),
                   pl.BlockSpec((ffb, d), lambda j, i, k: (j, 0)), pl.BlockSpec((tm, ffb), lambda j, i, k: (i, j)), NT)

    def wgrad_rows(name, a, b, n_out, kb):
        return _matmul(name, a, b, (n_out, d), bf16, (n_out // kb, 2, mt), pl.BlockSpec((tm, kb), lambda i, j, k: (k, i)),
                       pl.BlockSpec((tm, dh2), lambda i, j, k: (k, j)), pl.BlockSpec((kb, dh2), lambda i, j, k: (i, j)), TN)

    gw_d = wgrad_rows("gw_ffn_down", act, dff, dff_, ffb)

    def b3(rows, fulls, outs, accs):
        da, gv, uv = [r[...] for r in rows]
        sg = _sigmoid(gv)
        outs[0][...] = (da * uv * _dsilu(gv, sg)).astype(bf16)
        outs[1][...] = (da * gv * sg).astype(bf16)

    dgg, duu = _rowwise("d_swiglu", b3, s, te, [(dact, ffb, 0), (gg, ffb, 0), (uu, ffb, 0)], [], [(dff_, bf16), (dff_, bf16)], ncol=4)

    def ffn_in_bwd(name, a, w):
        return _matmul(name, a, w, (s, d), f32, (2, mt, 4), pl.BlockSpec((tm, ffb), lambda j, i, k: (i, k)),
                       pl.BlockSpec((ffb, dh2), lambda j, i, k: (k, j)), pl.BlockSpec((tm, dh2), lambda j, i, k: (i, j)), NN)

    dh2a = ffn_in_bwd("d_ffn_gate_in", dgg, w_gt)
    dh2b = ffn_in_bwd("d_ffn_up_in", duu, w_ut)
    gw_gt = wgrad_rows("gw_ffn_gate", dgg, h2, dff_, ffb)
    gw_ut = wgrad_rows("gw_ffn_up", duu, h2, dff_, ffb)

    def b5(rows, fulls, outs, accs):
        da, db, x1v, dx2v, mov = [r[...] for r in rows]
        sc, g, gt = [r[...] for r in fulls]
        dh = da + db
        r, xh = _rms(x1v)
        accs[0][...] += _colsum(dh)
        accs[1][...] += _colsum(dh * (xh * g))
        dn = dh * (1.0 + sc)
        accs[2][...] += _colsum(dn * xh)
        dxh = dn * g
        dx1 = dx2v + r * (dxh - xh * jnp.mean(dxh * xh, axis=-1, keepdims=True))
        outs[0][...] = dx1
        accs[3][...] += _colsum(dx1 * mov)
        outs[1][...] = (dx1 * gt).astype(bf16)

    dx1, dmo, dsh_f, dsc_f, dg_ffn, dgt_m = _rowwise(
        "d_prenorm_ffn", b5, s, te, [(dh2a, d, 0), (dh2b, d, 0), (x1, d, 0), (dx2, d, 0), (mo, d, 0)], [sc_f, g_ffn, gt_m],
        [(d, f32), (d, bf16)], [(1, d)] * 4)
    dmerged = _matmul("d_mix_out_in", dmo, w_o, (s, d), f32, (2, mt, 1), pl.BlockSpec((tm, d), lambda j, i, k: (i, 0)),
                      pl.BlockSpec((dh2, d), lambda j, i, k: (j, 0)), pl.BlockSpec((tm, dh2), lambda j, i, k: (i, j)), NT)
    gw_o = wgrad_rows("gw_mix_out", merged, dmo, d, dh2)

    def b7(rows, fulls, outs, accs):
        dm, ya, yb, ga, gb = [r[...] for r in rows]
        sa, sb_ = _sigmoid(ga), _sigmoid(gb)
        outs[0][...] = (dm * sa).astype(bf16)
        outs[1][...] = (dm * sb_).astype(bf16)
        outs[2][...] = (dm * ya * sa * (1.0 - sa)).astype(bf16)
        outs[3][...] = (dm * yb * sb_ * (1.0 - sb_)).astype(bf16)

    dya, dyb, dga, dgb = _rowwise("d_merge", b7, s, te, [(dmerged, dh2, 0), (y_a, dh2, 0), (y_b, dh2, 0), (proj, dh2, ga0), (proj, dh2, gb0)],
                                  [], [(d, bf16)] * 4, ncol=2)

    def out_proj_bwd(name, dy, w):
        return _matmul(name, dy, w, (s, dc), f32, (1, s // tm2, nsh), pl.BlockSpec((tm2, d // nsh), lambda j, i, k: (i, k)),
                       pl.BlockSpec((None, dc, d // nsh), lambda j, i, k: (k, 0, 0)), pl.BlockSpec((tm2, dc), lambda j, i, k: (i, 0)), NT)

    def out_proj_wgrad(name, a, dy):
        return _matmul(name, a, dy, (nsh, dc, d // nsh), bf16, (1, nsh, mt), pl.BlockSpec((tm, dc), lambda i, j, k: (k, 0)),
                       pl.BlockSpec((tm, d // nsh), lambda i, j, k: (k, j)), pl.BlockSpec((None, dc, d // nsh), lambda i, j, k: (j, 0, 0)), TN)

    dz_a = out_proj_bwd("d_conv_out_in", dya, w_co)
    dog = out_proj_bwd("d_hgrn_out_in", dyb, w_ho)
    gw_co = out_proj_wgrad("gw_conv_out", z_a, dya)
    gw_ho = out_proj_wgrad("gw_hgrn_out", og, dyb)
    dab, dac, dax, dconv_w = _conv_bwd(proj, conv_w, dz_a, s, dc)
    dq, dfl, dvi, dgo, dlb, dgn = _hgrn_bwd(proj, lb_param, gnorm, o_saved, states, dog, s, dc, tb)
    dproj = jnp.concatenate([dab, dac, dax, dq, dfl, dvi, dgo, dga, dgb], axis=1)
    dh = _matmul("d_proj_in", dproj, w_in, (s, d), f32, (1, mt, nsh), pl.BlockSpec((tm, win_sh), lambda j, i, k: (i, k)),
                 pl.BlockSpec((None, d, win_sh), lambda j, i, k: (k, 0, 0)), pl.BlockSpec((tm, d), lambda j, i, k: (i, 0)), NT)
    gw_in = _matmul("gw_proj", h, dproj, (nsh, d, win_sh), bf16, (2, nsh, mt), pl.BlockSpec((tm, dh2), lambda i, j, k: (k, i)),
                    pl.BlockSpec((tm, win_sh), lambda i, j, k: (k, j)), pl.BlockSpec((None, dh2, win_sh), lambda i, j, k: (j, i, 0)), TN)

    def b12(rows, fulls, outs, accs):
        dhv, xv, dx1v = [r[...] for r in rows]
        sc, g = fulls[0][...], fulls[1][...]
        r, xh = _rms(xv)
        accs[0][...] += _colsum(dhv)
        accs[1][...] += _colsum(dhv * (xh * g))
        dn = dhv * (1.0 + sc)
        accs[2][...] += _colsum(dn * xh)
        dxh = dn * g
        outs[0][...] = dx1v + r * (dxh - xh * jnp.mean(dxh * xh, axis=-1, keepdims=True))

    dx, dsh_m, dsc_m, dg_mix = _rowwise("d_prenorm_mix", b12, s, te, [(dh, d, 0), (x, d, 0), (dx1, d, 0)], [sc_m, g_mix],
                                        [(d, f32)], [(1, d)] * 3)
    dmod = [dsh_m, dsc_m, dgt_m, dsh_f, dsc_f, dgt_f]
    small = dict(loss=loss_acc, g_mix=dg_mix, g_ffn=dg_ffn, g_fin=dg_fin, lb=dlb, gnorm=dgn, conv_w=dconv_w)
    big = dict(w_in=gw_in, w_co=gw_co, w_ho=gw_ho, w_o=gw_o, w_gt=gw_gt, w_ut=gw_ut, w_d=gw_d)
    return dx, dmod, small, big


def _ada_fwd(c_all, w_sh, b_sh):
    def body(c_ref, w_ref, b_ref, o_ref):
        cv = c_ref[...]
        ca = (cv * _sigmoid(cv)).astype(bf16)
        o_ref[...] = jnp.dot(ca, w_ref[...].astype(bf16), preferred_element_type=f32) + b_ref[...]

    return pl.pallas_call(body, name="ada_fwd", out_shape=SDS((c_all.shape[0], w_sh.shape[1]), f32),
                          compiler_params=pltpu.CompilerParams(vmem_limit_bytes=V7X_VMEM_LIMIT))(c_all, w_sh, b_sh)


def _ada_wgrad(c_all, dmod_sh):
    def body(c_ref, d_ref, o_ref):
        cv = c_ref[...]
        ca = (cv * _sigmoid(cv)).astype(bf16)
        o_ref[...] = lax.dot_general(ca, d_ref[...].astype(bf16), (TN, ((), ())), preferred_element_type=f32)

    return pl.pallas_call(body, name="ada_wgrad", out_shape=SDS((c_all.shape[1], dmod_sh.shape[1]), f32),
                          compiler_params=pltpu.CompilerParams(vmem_limit_bytes=V7X_VMEM_LIMIT))(c_all, dmod_sh)


def _lb_grad(lb_param, dlb):
    def body(p_ref, d_ref, o_ref):
        p = p_ref[...]
        lb = _sigmoid(p[0:1, :] - p[1:2, :])
        gl = d_ref[...] * lb * (1.0 - lb)
        o_ref[0:1, :] = gl
        o_ref[1:2, :] = -gl

    return pl.pallas_call(body, name="lb_grad", out_shape=SDS(lb_param.shape, f32))(lb_param, dlb)


def _sum_small(gathered):
    def body(g_ref, o_ref):
        acc = g_ref[0]
        for dd in range(1, NDEV):
            acc = acc + g_ref[dd]
        o_ref[...] = acc

    return pl.pallas_call(body, name="sum_small", out_shape=SDS(gathered.shape[1:], f32))(gathered)


def kernel(x, c, w_ada, b_ada, norm_mix_g, w_in, conv_w, lb_param, gnorm_g, w_conv_out, w_hgrn_out, w_o, norm_ffn_g, w_ffn_gate, w_ffn_up, w_ffn_down, norm_final_g, loss_target, m_w_ada, m_b_ada, m_norm_mix_g, m_w_in, m_conv_w, m_lb_param, m_gnorm_g, m_w_conv_out, m_w_hgrn_out, m_w_o, m_norm_ffn_g, m_w_ffn_gate, m_w_ffn_up, m_w_ffn_down, m_norm_final_g, v_w_ada, v_b_ada, v_norm_mix_g, v_w_in, v_conv_w, v_lb_param, v_gnorm_g, v_w_conv_out, v_w_hgrn_out, v_w_o, v_norm_ffn_g, v_w_ffn_gate, v_w_ffn_up, v_w_ffn_down, v_norm_final_g):
    assert lb_param.shape[0] == 2 and w_ada.shape[0] == 1
    s, d = x.shape[1], x.shape[2]
    me = 4 * lax.axis_index("x") + 2 * lax.axis_index("y") + lax.axis_index("c")
    ada_cols = w_ada.shape[2]

    c_all, cw_all = _exchange("gather_cond", [c, conv_w[0]], all_to_all=False)
    c_all = c_all.reshape(NDEV, d)
    conv_w_full = jnp.transpose(cw_all, (1, 0, 2)).reshape(conv_w.shape[1], -1)
    b_sh = lax.dynamic_slice_in_dim(b_ada, me * ada_cols, ada_cols, axis=1)
    mod_cols = _ada_fwd(c_all, w_ada[0], b_sh)
    mod_all, = _exchange("gather_mod", [mod_cols], all_to_all=False)
    mod = lax.dynamic_index_in_dim(mod_all, me, axis=1, keepdims=False).reshape(N_MOD, 1, d)

    shards = [w_in[0].astype(bf16), w_conv_out[0].astype(bf16), w_hgrn_out[0].astype(bf16), w_o[0].astype(bf16),
              w_ffn_gate[0].T.astype(bf16), w_ffn_up[0].T.astype(bf16), w_ffn_down[0].astype(bf16)]
    g_in, g_co, g_ho, g_o, g_gt, g_ut, g_d = _exchange("gather_weights", shards, all_to_all=False)
    flat = lambda a: a.reshape(a.shape[0] * a.shape[1], a.shape[2])

    dx, dmod, small, big = _local_step(x[0], loss_target[0], mod, norm_mix_g, norm_ffn_g, norm_final_g.reshape(1, d), lb_param,
                                       gnorm_g, conv_w_full, g_in, g_co, g_ho, flat(g_o), flat(g_gt), flat(g_ut), flat(g_d))

    pieces = [*dmod, small["g_mix"], small["g_ffn"], small["g_fin"], small["lb"], small["gnorm"], small["loss"],
              small["conv_w"].reshape(1, -1)]
    widths = [p.shape[1] for p in pieces]
    offs = np.concatenate([[0], np.cumsum(widths)])
    packed = jnp.concatenate(pieces, axis=1)
    gathered, = _exchange("gather_small", [packed], all_to_all=False)
    summed = _sum_small(gathered)
    part = lambda i: summed[:, offs[i]:offs[i + 1]]
    g_b_ada = summed[:, :N_MOD * d]
    g_norm_mix, g_norm_ffn, g_norm_fin, g_lb_row, g_gnorm, loss_vec, g_convw_flat = [part(i) for i in range(N_MOD, N_MOD + 7)]
    loss = loss_vec[0, 0]
    dmod_all = gathered[:, 0, :N_MOD * d]
    g_w_ada = _ada_wgrad(c_all, lax.dynamic_slice_in_dim(dmod_all, me * ada_cols, ada_cols, axis=1))
    g_lb = _lb_grad(lb_param, g_lb_row)
    cw_cols = conv_w.shape[2]
    g_conv_w = lax.dynamic_slice_in_dim(g_convw_flat.reshape(conv_w.shape[1], -1), me * cw_cols, cw_cols, axis=1)

    to8 = lambda a: a.reshape(NDEV, a.shape[0] // NDEV, a.shape[1])
    parts = _exchange("scatter_grads", [big["w_in"], big["w_co"], big["w_ho"], to8(big["w_o"]), to8(big["w_gt"]), to8(big["w_ut"]),
                                        to8(big["w_d"])], all_to_all=True)
    tiles = [256, 256, 256, 64, 176, 176, 176]
    names = ["w_in", "w_conv_out", "w_hgrn_out", "w_o", "w_ffn_gate", "w_ffn_up", "w_ffn_down"]
    gsum = [_sum_parts("sum_" + nm, p, min(tr, p.shape[1])) for nm, p, tr in zip(names, parts, tiles)]
    g_w_in, g_w_co, g_w_ho, g_w_o, g_w_gt, g_w_ut, g_w_d = gsum
    grads = dict(w_ada=g_w_ada, b_ada=g_b_ada, norm_mix_g=g_norm_mix, w_in=g_w_in, conv_w=g_conv_w, lb_param=g_lb, gnorm_g=g_gnorm,
                 w_conv_out=g_w_co, w_hgrn_out=g_w_ho, w_o=g_w_o, norm_ffn_g=g_norm_ffn, w_ffn_gate=g_w_gt.T, w_ffn_up=g_w_ut.T,
                 w_ffn_down=g_w_d, norm_final_g=g_norm_fin)
    weights = dict(w_ada=(w_ada, m_w_ada, v_w_ada), b_ada=(b_ada, m_b_ada, v_b_ada), norm_mix_g=(norm_mix_g, m_norm_mix_g, v_norm_mix_g),
                   w_in=(w_in, m_w_in, v_w_in), conv_w=(conv_w, m_conv_w, v_conv_w), lb_param=(lb_param, m_lb_param, v_lb_param),
                   gnorm_g=(gnorm_g, m_gnorm_g, v_gnorm_g), w_conv_out=(w_conv_out, m_w_conv_out, v_w_conv_out),
                   w_hgrn_out=(w_hgrn_out, m_w_hgrn_out, v_w_hgrn_out), w_o=(w_o, m_w_o, v_w_o),
                   norm_ffn_g=(norm_ffn_g, m_norm_ffn_g, v_norm_ffn_g), w_ffn_gate=(w_ffn_gate, m_w_ffn_gate, v_w_ffn_gate),
                   w_ffn_up=(w_ffn_up, m_w_ffn_up, v_w_ffn_up), w_ffn_down=(w_ffn_down, m_w_ffn_down, v_w_ffn_down),
                   norm_final_g=(norm_final_g, m_norm_final_g, v_norm_final_g))
    out_g, out_d, out_m, out_v = [], [], [], []
    for nm, (w, m, v) in weights.items():
        shape2 = (w.shape[-2], w.shape[-1]) if w.ndim >= 2 else (1, w.shape[0])
        g2 = grads[nm].reshape(shape2)
        dl, mn, vn = _adamw("adamw_" + nm, g2, w.reshape(shape2), m.reshape(shape2), v.reshape(shape2))
        out_g.append(g2.reshape(w.shape))
        out_d.append(dl.reshape(w.shape))
        out_m.append(mn.reshape(w.shape))
        out_v.append(vn.reshape(w.shape))
    return (loss, dx.reshape(x.shape), *out_g, *out_d, *out_m, *out_v)
```

```python
import functools

import numpy as np
import jax
import jax.numpy as jnp
from jax import lax
from jax.experimental import pallas as pl
from jax.experimental.pallas import tpu as pltpu

f32, bf16 = jnp.float32, jnp.bfloat16
SDS = jax.ShapeDtypeStruct

EPS = 1e-6
HEADS, DK, CHUNK = 8, 128, 64
N_MOD = 6
NDEV = 8
ADAM_LR, ADAM_B1, ADAM_B2, ADAM_EPS, ADAM_WD, ADAM_STEP = 0.001, 0.9, 0.999, 1e-08, 0.01, 10
LEVELS = (32, 16, 8, 4, 2, 1)
V7X_VMEM_LIMIT = 56 * 1024 * 1024
HBM = pl.BlockSpec(memory_space=pltpu.HBM)
MESH = pl.DeviceIdType.MESH


def _params(sem):
    return pltpu.CompilerParams(dimension_semantics=sem, vmem_limit_bytes=V7X_VMEM_LIMIT)


def _sigmoid(x):
    return jax.nn.sigmoid(x)


def _dsilu(x, s):
    return s * (1.0 + x * (1.0 - s))


def _mesh_pos():
    x, y, c = lax.axis_index("x"), lax.axis_index("y"), lax.axis_index("c")
    return x, y, c


def _peer(pos, j):
    x, y, c = pos
    return (1 - x if j & 4 else x, 1 - y if j & 2 else y, 1 - c if j & 1 else c)


def _flat(pos):
    return 4 * pos[0] + 2 * pos[1] + pos[2]


def _all_gather(name, arrs):
    n = len(arrs)
    out_shapes = [SDS((NDEV,) + a.shape, a.dtype) for a in arrs]

    def body(*refs):
        ins, outs = refs[:n], refs[n:2 * n]
        send_sems, recv_sems, local_sems = refs[2 * n:]
        pos = _mesh_pos()
        me = _flat(pos)

        def copy(a, j, frm, to_pos):
            k = a * (NDEV - 1) + j - 1
            return pltpu.make_async_remote_copy(src_ref=ins[a], dst_ref=outs[a].at[frm], send_sem=send_sems.at[k],
                                                recv_sem=recv_sems.at[k], device_id=to_pos, device_id_type=MESH)

        local = [pltpu.make_async_copy(ins[a], outs[a].at[me], local_sems.at[a]) for a in range(n)]
        for cp in local:
            cp.start()
        sends = [copy(a, j, me, _peer(pos, j)) for j in range(1, NDEV) for a in range(n)]
        for cp in sends:
            cp.start()
        for j in range(1, NDEV):
            for a in range(n):
                copy(a, j, _flat(_peer(pos, j)), pos).wait_recv()
        for cp in sends:
            cp.wait_send()
        for cp in local:
            cp.wait()

    return pl.pallas_call(
        body, name=name, out_shape=out_shapes, in_specs=[HBM] * n, out_specs=[HBM] * n,
        scratch_shapes=[pltpu.SemaphoreType.DMA((n * (NDEV - 1),)), pltpu.SemaphoreType.DMA((n * (NDEV - 1),)),
                        pltpu.SemaphoreType.DMA((n,))],
    )(*arrs)


SEM = pl.BlockSpec(memory_space=pltpu.SEMAPHORE)
ANY = pl.BlockSpec(memory_space=pl.ANY)
EFFECT = pltpu.SideEffectType.DATAFLOW_SIDE_EFFECTING
ICI_RELATIONS = (2, 4, 6)


def _chip(pos):
    return 2 * pos[0] + pos[1]


def _hbm(a):
    return pltpu.with_memory_space_constraint(a, pltpu.HBM)


def _plan_copy(plan_entry, k, pos, frm, to, src_refs, land_refs, send_sems, recv_sems):
    a, _, src_slot, dst_slot = plan_entry
    s = src_refs[a] if src_slot is None else src_refs[a].at[src_slot(frm, to)]
    return pltpu.make_async_remote_copy(src_ref=s, dst_ref=land_refs[a].at[dst_slot(frm, to)], send_sem=send_sems.at[k],
                                        recv_sem=recv_sems.at[k], device_id=to, device_id_type=MESH)


def _push_start(name, srcs, lands, plan):
    ns, nk = len(srcs), len(plan)

    def body(*refs):
        src_refs, land_refs = refs[:ns], refs[ns:2 * ns]
        send_sems, recv_sems = refs[2 * ns], refs[2 * ns + 1]
        pos = _mesh_pos()
        for k, e in enumerate(plan):
            _plan_copy(e, k, pos, pos, _peer(pos, e[1]), src_refs, land_refs, send_sems, recv_sems).start()
        refs[-1][...] = jnp.zeros_like(refs[-1])

    outs = pl.pallas_call(
        body, name=name,
        out_shape=[pltpu.SemaphoreType.DMA((nk,)), pltpu.SemaphoreType.DMA((nk,))] + [pltpu.HBM(a.shape, a.dtype) for a in srcs + lands]
        + [SDS((8, 128), f32)],
        in_specs=[HBM] * (2 * ns), out_specs=[SEM, SEM] + [HBM] * (2 * ns) + [pl.BlockSpec(memory_space=pltpu.VMEM)],
        input_output_aliases={i: 2 + i for i in range(2 * ns)},
        compiler_params=pltpu.CompilerParams(has_side_effects=EFFECT),
    )(*[_hbm(a) for a in srcs + lands])
    return outs[0], outs[1], list(outs[2:2 + ns]), list(outs[2 + ns:2 + 2 * ns]), outs[-1]


def _push_wait(name, send_sems, recv_sems, srcs, lands, plan, after):
    ns = len(srcs)

    def body(*refs):
        src_refs, land_refs = refs[:ns], refs[ns:2 * ns]
        ssem, rsem = refs[2 * ns], refs[2 * ns + 1]
        pos = _mesh_pos()
        for k, e in enumerate(plan):
            peer = _peer(pos, e[1])
            _plan_copy(e, k, pos, pos, peer, src_refs, land_refs, ssem, rsem).wait_send()
            _plan_copy(e, k, pos, peer, pos, src_refs, land_refs, ssem, rsem).wait_recv()

    outs = pl.pallas_call(
        body, name=name, out_shape=[pltpu.HBM(a.shape, a.dtype) for a in srcs + lands],
        in_specs=[HBM] * (2 * ns) + [SEM, SEM, ANY], out_specs=[HBM] * (2 * ns),
        input_output_aliases={i: i for i in range(2 * ns)},
        compiler_params=pltpu.CompilerParams(has_side_effects=EFFECT),
    )(*srcs, *lands, send_sems, recv_sems, after)
    return list(outs[:ns]), list(outs[ns:])


def _row_tile(r):
    return max(t for t in range(16, 257, 16) if r % t == 0)


def _sibling_forward(name, shards, lands):
    n = len(shards)
    nk = n * len(ICI_RELATIONS)

    def body(*refs):
        shard_refs, land_refs = refs[:n], refs[n:2 * n]
        send_sems, recv_sems, local_sems = refs[3 * n:]
        pos = _mesh_pos()
        sib = _peer(pos, 1)
        local = [pltpu.make_async_copy(shard_refs[a], land_refs[a].at[_flat(pos)], local_sems.at[a]) for a in range(n)]
        for cp in local:
            cp.start()

        def copy(a, i, frm, to):
            k = a * len(ICI_RELATIONS) + i
            slot = _flat(_peer(frm, ICI_RELATIONS[i]))
            return pltpu.make_async_remote_copy(src_ref=land_refs[a].at[slot], dst_ref=land_refs[a].at[slot], send_sem=send_sems.at[k],
                                                recv_sem=recv_sems.at[k], device_id=to, device_id_type=MESH)

        sends = [copy(a, i, pos, sib) for a in range(n) for i in range(len(ICI_RELATIONS))]
        for cp in sends:
            cp.start()
        for a in range(n):
            for i in range(len(ICI_RELATIONS)):
                copy(a, i, sib, pos).wait_recv()
        for cp in sends:
            cp.wait_send()
        for cp in local:
            cp.wait()

    outs = pl.pallas_call(
        body, name=name, out_shape=[SDS(a.shape, a.dtype) for a in lands], in_specs=[HBM] * (2 * n), out_specs=[HBM] * n,
        input_output_aliases={n + i: i for i in range(n)},
        scratch_shapes=[pltpu.SemaphoreType.DMA((nk,)), pltpu.SemaphoreType.DMA((nk,)), pltpu.SemaphoreType.DMA((n,))],
    )(*shards, *lands)
    return list(outs)


def _sibling_swap(name, grads):
    n = len(grads)
    nchip = NDEV // 2

    def body(*refs):
        g_refs, out_refs = refs[:n], refs[n:2 * n]
        send_sems, recv_sems = refs[2 * n:]
        pos = _mesh_pos()
        sib = _peer(pos, 1)
        sends = []
        for a in range(n):
            for q in range(nchip):
                k = a * nchip + q
                sends.append(pltpu.make_async_remote_copy(src_ref=g_refs[a].at[2 * q + sib[2]], dst_ref=out_refs[a].at[q],
                                                          send_sem=send_sems.at[k], recv_sem=recv_sems.at[k], device_id=sib,
                                                          device_id_type=MESH))
        for cp in sends:
            cp.start()
        for cp in sends:
            cp.wait()

    return pl.pallas_call(
        body, name=name, out_shape=[SDS((nchip,) + a.shape[1:], a.dtype) for a in grads], in_specs=[HBM] * n, out_specs=[HBM] * n,
        scratch_shapes=[pltpu.SemaphoreType.DMA((n * nchip,)), pltpu.SemaphoreType.DMA((n * nchip,))],
    )(*grads)


def _pair_sum(name, core, grad, recv, tr):
    _, r, c = grad.shape
    nchip = NDEV // 2

    def body(core_ref, g_ref, r_ref, o_ref):
        o_ref[...] = (g_ref[...].astype(f32) + r_ref[...].astype(f32)).astype(o_ref.dtype)

    return pl.pallas_call(
        body, name=name, out_shape=SDS((nchip, r, c), grad.dtype),
        grid_spec=pltpu.PrefetchScalarGridSpec(
            num_scalar_prefetch=1, grid=(nchip, r // tr),
            in_specs=[pl.BlockSpec((None, tr, c), lambda q, i, core_ref: (2 * q + core_ref[0], i, 0)),
                      pl.BlockSpec((None, tr, c), lambda q, i, core_ref: (q, i, 0))],
            out_specs=pl.BlockSpec((None, tr, c), lambda q, i, core_ref: (q, i, 0))),
        compiler_params=_params(("parallel", "parallel")),
    )(core, grad, recv)


def _chip_sum(name, chip, pairs, recv, tr):
    nchip, r, c = pairs.shape

    def body(chip_ref, p_ref, r_ref, o_ref):
        mine = chip_ref[0]
        acc = jnp.zeros(o_ref.shape, f32)
        for q in range(nchip):
            acc = acc + jnp.where(mine == q, p_ref[q].astype(f32), r_ref[q].astype(f32))
        o_ref[...] = acc

    return pl.pallas_call(
        body, name=name, out_shape=SDS((r, c), f32),
        grid_spec=pltpu.PrefetchScalarGridSpec(
            num_scalar_prefetch=1, grid=(r // tr,),
            in_specs=[pl.BlockSpec((nchip, tr, c), lambda i, chip_ref: (0, i, 0))] * 2,
            out_specs=pl.BlockSpec((tr, c), lambda i, chip_ref: (i, 0))),
        compiler_params=_params(("parallel",)),
    )(chip, pairs, recv)


def _matmul(name, a, b, out_shape, out_dtype, grid, a_spec, b_spec, o_spec, dims):
    ksteps = grid[2]
    acc_shape = tuple(d for d in o_spec.block_shape if d is not None)

    def body(a_ref, b_ref, o_ref, *acc):
        prod = lax.dot_general(a_ref[...], b_ref[...], (dims, ((), ())), preferred_element_type=f32)
        if ksteps == 1:
            o_ref[...] = prod.astype(o_ref.dtype)
        else:
            k = pl.program_id(2)

            @pl.when(k == 0)
            def _():
                acc[0][...] = prod

            @pl.when(k > 0)
            def _():
                acc[0][...] += prod

            @pl.when(k == ksteps - 1)
            def _():
                o_ref[...] = acc[0][...].astype(o_ref.dtype)

    return pl.pallas_call(
        body, name=name, grid=grid, in_specs=[a_spec, b_spec], out_specs=o_spec, out_shape=SDS(out_shape, out_dtype),
        scratch_shapes=[] if ksteps == 1 else [pltpu.VMEM(acc_shape, f32)],
        compiler_params=_params(("parallel", "parallel", "arbitrary")),
    )(a, b)


NN, NT, TN = ((1,), (0,)), ((1,), (1,)), ((0,), (0,))


def _rowwise(name, fn, n_rows, tm, rows, fulls, row_outs, acc_outs=(), ncol=1):
    assert ncol == 1 or not acc_outs
    nr, nf, no, na = len(rows), len(fulls), len(row_outs), len(acc_outs)
    in_specs = [pl.BlockSpec((tm, w), functools.partial(lambda i, j, cb: (i, cb + j), cb=cb)) for (_, w, cb) in rows]
    in_specs += [pl.BlockSpec(a.shape, functools.partial(lambda i, j, nd: (0,) * nd, nd=a.ndim)) for a in fulls]
    out_shape = [SDS((n_rows, w), dt) for (w, dt) in row_outs] + [SDS(s, f32) for s in acc_outs]
    out_specs = [pl.BlockSpec((tm, w // ncol), lambda i, j: (i, j)) for (w, _) in row_outs]
    out_specs += [pl.BlockSpec(s, functools.partial(lambda i, j, nd: (0,) * nd, nd=len(s))) for s in acc_outs]

    def body(*refs):
        if na:
            @pl.when(pl.program_id(0) == 0)
            def _():
                for r in refs[nr + nf + no:]:
                    r[...] = jnp.zeros(r.shape, r.dtype)
        fn(refs[:nr], refs[nr:nr + nf], refs[nr + nf:nr + nf + no], refs[nr + nf + no:])

    return pl.pallas_call(
        body, name=name, grid=(n_rows // tm, ncol), in_specs=in_specs, out_specs=out_specs, out_shape=out_shape,
        compiler_params=_params(("arbitrary" if na else "parallel", "arbitrary" if na else "parallel")),
    )(*[r[0] for r in rows], *fulls)


def _rms(x):
    r = lax.rsqrt(jnp.mean(x * x, axis=-1, keepdims=True) + EPS)
    return r, x * r


def _colsum(v):
    return jnp.sum(v, axis=0, keepdims=True)


def _shift_down(u, row, k):
    return jnp.where(row >= k, pltpu.roll(u, k, 0), 0.0)


def _shift_up(u, row, k):
    n = u.shape[0]
    return jnp.where(row < n - k, pltpu.roll(u, n - k, 0), 0.0)


def _conv_fwd(proj, conv_w, s, dc):
    nb = dc // 128

    def body(ab_ref, ac_ref, ax_ref, w_ref, z_ref):
        u = ac_ref[...] * ax_ref[...]
        row = lax.broadcasted_iota(jnp.int32, u.shape, 0)
        w = w_ref[...]
        cv = w[0:1] * _shift_down(u, row, 2) + w[1:2] * _shift_down(u, row, 1) + w[2:3] * u
        z_ref[...] = (ab_ref[...] * cv).astype(z_ref.dtype)

    col = lambda off: pl.BlockSpec((s, 128), functools.partial(lambda j, off: (0, off + j), off=off))
    return pl.pallas_call(
        body, name="conv_fwd", grid=(nb,), in_specs=[col(0), col(nb), col(2 * nb), pl.BlockSpec((3, 128), lambda j: (0, j))],
        out_specs=pl.BlockSpec((s, 128), lambda j: (0, j)), out_shape=SDS((s, dc), bf16), compiler_params=_params(("parallel",)),
    )(proj, proj, proj, conv_w)


def _conv_bwd(proj, conv_w, dz, s, dc):
    nb = dc // 128

    def body(ab_ref, ac_ref, ax_ref, w_ref, dz_ref, dab_ref, dac_ref, dax_ref, dw_ref):
        ab, ac, ax, dzv = ab_ref[...], ac_ref[...], ax_ref[...], dz_ref[...]
        u = ac * ax
        row = lax.broadcasted_iota(jnp.int32, u.shape, 0)
        w = w_ref[...]
        u1, u2 = _shift_down(u, row, 1), _shift_down(u, row, 2)
        cv = w[0:1] * u2 + w[1:2] * u1 + w[2:3] * u
        dcv = dzv * ab
        dab_ref[...] = (dzv * cv).astype(dab_ref.dtype)
        du = w[2:3] * dcv + w[1:2] * _shift_up(dcv, row, 1) + w[0:1] * _shift_up(dcv, row, 2)
        dac_ref[...] = (du * ax).astype(dac_ref.dtype)
        dax_ref[...] = (du * ac).astype(dax_ref.dtype)
        dw_ref[0:1, :] = _colsum(dcv * u2)
        dw_ref[1:2, :] = _colsum(dcv * u1)
        dw_ref[2:3, :] = _colsum(dcv * u)

    col = lambda off: pl.BlockSpec((s, 128), functools.partial(lambda j, off: (0, off + j), off=off))
    blk = pl.BlockSpec((s, 128), lambda j: (0, j))
    return pl.pallas_call(
        body, name="conv_bwd", grid=(nb,),
        in_specs=[col(0), col(nb), col(2 * nb), pl.BlockSpec((3, 128), lambda j: (0, j)), blk],
        out_specs=[blk, blk, blk, pl.BlockSpec((3, 128), lambda j: (0, j))],
        out_shape=[SDS((s, dc), bf16)] * 3 + [SDS((3, dc), f32)], compiler_params=_params(("parallel",)),
    )(proj, proj, proj, conv_w, dz)


def _level_masks():
    t = np.arange(CHUNK)[:, None]
    s = np.arange(CHUNK)[None, :]
    m = np.stack([((t & h) != 0) & ((s & h) == 0) & (t // (2 * h) == s // (2 * h)) for h in LEVELS]).astype(np.float32)
    return jnp.asarray(m), jnp.asarray(m.transpose(0, 2, 1))


def _cumsum_rows(x, row):
    for sh in (1, 2, 4, 8, 16, 32):
        x = x + jnp.where(row >= sh, pltpu.roll(x, sh, 0), 0.0)
    return x


def _rev_cumsum_rows(x, row):
    n = x.shape[0]
    for sh in (1, 2, 4, 8, 16, 32):
        x = x + jnp.where(row < n - sh, pltpu.roll(x, n - sh, 0), 0.0)
    return x


def _chunk_terms(qp, fl, lb):
    row = lax.broadcasted_iota(jnp.int32, qp.shape, 0)
    sig = _sigmoid(fl)
    f = lb + (1.0 - lb) * sig
    k = 1.0 - f
    sq = _sigmoid(qp)
    qh = qp * sq
    b = _cumsum_rows(jnp.log(f), row)
    sub = lax.broadcasted_iota(jnp.int32, (CHUNK // 8, 8, DK), 1)
    b8 = b.reshape(CHUNK // 8, 8, DK)
    us, exs, ups = [], [], []
    for m in LEVELS:
        sb = 2 * m
        if sb >= 8:
            b3 = b.reshape(CHUNK // sb, sb, DK)
            bref = jnp.broadcast_to(b3[:, m - 1:m, :], b3.shape).reshape(CHUNK, DK)
        else:
            bref8 = None
            for j in range(8 // sb):
                cand = jnp.broadcast_to(b8[:, j * sb + m - 1:j * sb + m, :], b8.shape)
                bref8 = cand if bref8 is None else jnp.where(sub >= j * sb, cand, bref8)
            bref = bref8.reshape(CHUNK, DK)
        up = (row & m) != 0
        ex = jnp.exp(jnp.where(up, b - bref, bref - b))
        us.append((jnp.where(up, qh, k) * ex).astype(bf16))
        exs.append(ex)
        ups.append(up)
    blast = b[CHUNK - 1:CHUNK, :]
    eb, ebl = jnp.exp(b), jnp.exp(blast - b)
    return dict(sig=sig, f=f, k=k, sq=sq, qh=qh, u=jnp.stack(us), ex=exs, up=ups, eb=eb, ebl=ebl, qt=qh * eb, kt=k * ebl,
                el=jnp.exp(blast), row=row)


def _scores(t, mask):
    pl_ = jnp.einsum("ltk,lsk->lts", t["u"], t["u"], preferred_element_type=f32)
    p = jnp.sum(pl_ * mask, axis=0)
    r = lax.broadcasted_iota(jnp.int32, (CHUNK, CHUNK), 0)
    c = lax.broadcasted_iota(jnp.int32, (CHUNK, CHUNK), 1)
    diag = jnp.sum(t["qh"] * t["k"], axis=-1, keepdims=True)
    return p + jnp.where(r == c, diag, 0.0)


def _hgrn_fwd(proj, lb_param, gnorm, s, dv_total, tb):
    nchunk = tb // CHUNK
    masks, _ = _level_masks()
    q0, f0, v0, g0 = 3 * HEADS, 4 * HEADS, 5 * HEADS, 6 * HEADS

    def body(q_ref, f_ref, v_ref, g_ref, lb_ref, gn_ref, mask_ref, og_ref, o_ref, st_ref, state):
        @pl.when(pl.program_id(1) == 0)
        def _():
            state[...] = jnp.zeros_like(state)

        lbp = lb_ref[...]
        lb = _sigmoid(lbp[0:1, :] - lbp[1:2, :])
        mask = mask_ref[...]
        for i in range(nchunk):
            rs = pl.ds(i * CHUNK, CHUNK)
            t = _chunk_terms(q_ref[rs, :], f_ref[rs, :], lb)
            v = v_ref[rs, :]
            vb = v.astype(bf16)
            st = state[...]
            st_ref[i] = st
            p = _scores(t, mask)
            o = jnp.dot(p.astype(bf16), vb, preferred_element_type=f32)
            o += lax.dot_general(t["qt"].astype(bf16), st.astype(bf16), (NT, ((), ())), preferred_element_type=f32)
            state[...] = st * t["el"] + lax.dot_general(vb, t["kt"].astype(bf16), (TN, ((), ())), preferred_element_type=f32)
            o_ref[rs, :] = o
            r, oh = _rms(o)
            g = g_ref[rs, :]
            og_ref[rs, :] = (oh * gn_ref[...] * (g * _sigmoid(g))).astype(og_ref.dtype)

    col = lambda off: pl.BlockSpec((tb, DK), functools.partial(lambda h, t, off: (t, off + h), off=off))
    blk = pl.BlockSpec((tb, DK), lambda h, t: (t, h))
    return pl.pallas_call(
        body, name="hgrn_fwd", grid=(HEADS, s // tb),
        in_specs=[col(q0), col(f0), col(v0), col(g0), pl.BlockSpec((2, DK), lambda h, t: (0, h)),
                  pl.BlockSpec((1, DK), lambda h, t: (0, 0)), pl.BlockSpec(masks.shape, lambda h, t: (0, 0, 0))],
        out_specs=[blk, blk, pl.BlockSpec((nchunk, None, DK, DK), lambda h, t: (t, h, 0, 0))],
        out_shape=[SDS((s, dv_total), bf16), SDS((s, dv_total), f32), SDS((s // CHUNK, HEADS, DK, DK), f32)],
        scratch_shapes=[pltpu.VMEM((DK, DK), f32)], compiler_params=_params(("parallel", "arbitrary")),
    )(proj, proj, proj, proj, lb_param, gnorm, masks)


def _hgrn_bwd(proj, lb_param, gnorm, o_saved, states, dog, s, dv_total, tb):
    nchunk = tb // CHUNK
    nt = s // tb
    nc_total = s // CHUNK
    masks, masks_t = _level_masks()
    q0, f0, v0, g0 = 3 * HEADS, 4 * HEADS, 5 * HEADS, 6 * HEADS

    def body(q_ref, f_ref, v_ref, g_ref, lb_ref, gn_ref, mask_ref, maskt_ref, o_ref, dog_ref, st_ref, stn_ref,
             dq_ref, df_ref, dv_ref, dg_ref, dlb_ref, dgn_ref, gstate):
        h_id, t_id = pl.program_id(0), pl.program_id(1)

        @pl.when(t_id == 0)
        def _():
            gstate[...] = jnp.zeros_like(gstate)
            dlb_ref[...] = jnp.zeros_like(dlb_ref)

        @pl.when((t_id == 0) & (h_id == 0))
        def _():
            dgn_ref[...] = jnp.zeros_like(dgn_ref)

        lbp = lb_ref[...]
        lb = _sigmoid(lbp[0:1, :] - lbp[1:2, :])
        mask, maskt = mask_ref[...], maskt_ref[...]
        gn = gn_ref[...]
        for i in reversed(range(nchunk)):
            rs = pl.ds(i * CHUNK, CHUNK)
            qp, fl, v, g = q_ref[rs, :], f_ref[rs, :], v_ref[rs, :], g_ref[rs, :]
            t = _chunk_terms(qp, fl, lb)
            vb = v.astype(bf16)
            st0 = st_ref[i]
            st1 = st_ref[i + 1] if i + 1 < nchunk else stn_ref[0]
            gt = gstate[...]
            o = o_ref[rs, :]
            r, oh = _rms(o)
            sg = _sigmoid(g)
            dog_v = dog_ref[rs, :]
            dg_ref[rs, :] = (dog_v * (oh * gn) * _dsilu(g, sg)).astype(dg_ref.dtype)
            don = dog_v * (g * sg)
            dgn_ref[...] += _colsum(don * oh)
            doh = don * gn
            do = r * (doh - oh * jnp.mean(doh * oh, axis=-1, keepdims=True))
            dob = do.astype(bf16)
            d = lax.dot_general(dob, vb, (NT, ((), ())), preferred_element_type=f32)
            dt = lax.dot_general(vb, dob, (NT, ((), ())), preferred_element_type=f32)
            z = (mask * d[None] + maskt * dt[None]).astype(bf16)
            rr = jnp.einsum("lts,lsk->ltk", z, t["u"], preferred_element_type=f32)
            dq = jnp.zeros((CHUNK, DK), f32)
            dk = jnp.zeros((CHUNK, DK), f32)
            qdk = jnp.zeros((CHUNK, DK), f32)
            for li in range(len(LEVELS)):
                du = t["ex"][li] * rr[li]
                dq += jnp.where(t["up"][li], du, 0.0)
                dk += jnp.where(t["up"][li], 0.0, du)
                e = t["u"][li].astype(f32) * rr[li]
                qdk += jnp.where(t["up"][li], e, -e)
            dd = jnp.sum(do * v, axis=-1, keepdims=True)
            dq += dd * t["k"]
            dk += dd * t["qh"]
            gtb = gt.astype(bf16)
            ktb, qtb = t["kt"].astype(bf16), t["qt"].astype(bf16)
            dq_in = jnp.dot(dob, st0.astype(bf16), preferred_element_type=f32)
            dk_in = jnp.dot(vb, gtb, preferred_element_type=f32)
            dq += t["eb"] * dq_in
            dk += t["ebl"] * dk_in
            qdk += qtb.astype(f32) * dq_in - ktb.astype(f32) * dk_in
            p = _scores(t, mask)
            dvv = lax.dot_general(p.astype(bf16), dob, (TN, ((), ())), preferred_element_type=f32)
            dvv += lax.dot_general(ktb, gtb, (NT, ((), ())), preferred_element_type=f32)
            dv_ref[rs, :] = dvv.astype(dv_ref.dtype)
            a_end = _colsum(gtb.astype(f32) * st1)
            dlf = _rev_cumsum_rows(qdk, t["row"]) + a_end
            dfv = dlf / t["f"] - dk
            df_ref[rs, :] = (dfv * (1.0 - lb) * t["sig"] * (1.0 - t["sig"])).astype(df_ref.dtype)
            dlb_ref[...] += _colsum(dfv * (1.0 - t["sig"]))
            dq_ref[rs, :] = (dq * _dsilu(qp, t["sq"])).astype(dq_ref.dtype)
            gstate[...] = gt * t["el"] + lax.dot_general(dob, qtb, (TN, ((), ())), preferred_element_type=f32)

    rev = lambda t: nt - 1 - t
    col = lambda off: pl.BlockSpec((tb, DK), functools.partial(lambda h, t, off: (rev(t), off + h), off=off))
    blk = pl.BlockSpec((tb, DK), lambda h, t: (rev(t), h))
    nxt = lambda h, t: (jnp.minimum((rev(t) + 1) * nchunk, nc_total - 1), h, 0, 0)
    return pl.pallas_call(
        body, name="hgrn_bwd", grid=(HEADS, nt),
        in_specs=[col(q0), col(f0), col(v0), col(g0), pl.BlockSpec((2, DK), lambda h, t: (0, h)),
                  pl.BlockSpec((1, DK), lambda h, t: (0, 0)), pl.BlockSpec(masks.shape, lambda h, t: (0, 0, 0)),
                  pl.BlockSpec(masks.shape, lambda h, t: (0, 0, 0)), blk, blk,
                  pl.BlockSpec((nchunk, None, DK, DK), lambda h, t: (rev(t), h, 0, 0)),
                  pl.BlockSpec((1, None, DK, DK), nxt)],
        out_specs=[blk, blk, blk, blk, pl.BlockSpec((1, DK), lambda h, t: (0, h)), pl.BlockSpec((1, DK), lambda h, t: (0, 0))],
        out_shape=[SDS((s, dv_total), bf16)] * 4 + [SDS((1, HEADS * DK), f32), SDS((1, DK), f32)],
        scratch_shapes=[pltpu.VMEM((DK, DK), f32)], compiler_params=_params(("arbitrary", "arbitrary")),
    )(proj, proj, proj, proj, lb_param, gnorm, masks, masks_t, o_saved, dog, states, states)


def _adamw(name, g, w, m, v):
    r, c = w.shape
    tr = r
    for cand in (256, 128, 64, 32, 16, 8):
        if r % cand == 0 and r > cand:
            tr = cand
            break

    def body(g_ref, w_ref, m_ref, v_ref, d_ref, mo_ref, vo_ref):
        gv = g_ref[...]
        mn = ADAM_B1 * m_ref[...] + (1.0 - ADAM_B1) * gv
        vn = ADAM_B2 * v_ref[...] + (1.0 - ADAM_B2) * jnp.square(gv)
        m_hat = mn / (1.0 - ADAM_B1 ** ADAM_STEP)
        v_hat = vn / (1.0 - ADAM_B2 ** ADAM_STEP)
        d_ref[...] = -ADAM_LR * (m_hat / (jnp.sqrt(v_hat) + ADAM_EPS) + ADAM_WD * w_ref[...])
        mo_ref[...] = mn
        vo_ref[...] = vn

    blk = pl.BlockSpec((tr, c), lambda i: (i, 0))
    return pl.pallas_call(
        body, name=name, grid=(r // tr,), in_specs=[blk] * 4, out_specs=[blk] * 3, out_shape=[SDS((r, c), f32)] * 3,
        compiler_params=_params(("parallel",)),
    )(g, w, m, v)


def _local_step(x, tgt, mod, g_mix, g_ffn, g_fin, lb_param, gnorm, conv_w, get_w, put_g):
    s, d = x.shape
    dc = d // 2
    tm = min(512, s)
    te = min(256, s)
    tb = min(256, s)
    mt = s // tm
    sh_m, sc_m, gt_m, sh_f, sc_f, gt_f = [mod[i] for i in range(N_MOD)]
    dh2 = d // 2

    def e1(rows, fulls, outs, accs):
        xv = rows[0][...]
        g, sc, sh = [r[...] for r in fulls]
        _, xh = _rms(xv)
        outs[0][...] = (xh * g * (1.0 + sc) + sh).astype(bf16)

    h, = _rowwise("prenorm_mix", e1, s, te, [(x, d, 0)], [g_mix, sc_m, sh_m], [(d, bf16)])
    w_in, = get_w("in", h)
    nsh, _, win_sh = w_in.shape
    proj = _matmul("proj", h, w_in, (s, nsh * win_sh), f32, (nsh, mt, 1), pl.BlockSpec((tm, d), lambda j, i, k: (i, 0)),
                   pl.BlockSpec((None, d, win_sh), lambda j, i, k: (j, 0, 0)), pl.BlockSpec((tm, win_sh), lambda j, i, k: (i, j)), NN)
    z_a = _conv_fwd(proj, conv_w, s, dc)
    og, o_saved, states = _hgrn_fwd(proj, lb_param, gnorm, s, dc, tb)
    tm2 = min(1024, s)
    w_co, w_ho, w_o = get_w("mix", og)

    def out_proj(name, a, w):
        return _matmul(name, a, w, (s, d), f32, (nsh, s // tm2, 1), pl.BlockSpec((tm2, dc), lambda j, i, k: (i, 0)),
                       pl.BlockSpec((None, dc, d // nsh), lambda j, i, k: (j, 0, 0)),
                       pl.BlockSpec((tm2, d // nsh), lambda j, i, k: (i, j)), NN)

    y_a = out_proj("conv_out", z_a, w_co)
    y_b = out_proj("hgrn_out", og, w_ho)
    ga0, gb0 = 7, 9

    def e5(rows, fulls, outs, accs):
        ya, yb, ga, gb = [r[...] for r in rows]
        outs[0][...] = (_sigmoid(ga) * ya + _sigmoid(gb) * yb).astype(bf16)

    merged, = _rowwise("merge", e5, s, te, [(y_a, dh2, 0), (y_b, dh2, 0), (proj, dh2, ga0), (proj, dh2, gb0)], [], [(d, bf16)], ncol=2)
    mo = _matmul("mix_out", merged, w_o, (s, d), f32, (2, mt, 1), pl.BlockSpec((tm, d), lambda j, i, k: (i, 0)),
                 pl.BlockSpec((d, dh2), lambda j, i, k: (0, j)), pl.BlockSpec((tm, dh2), lambda j, i, k: (i, j)), NN)

    def e6(rows, fulls, outs, accs):
        xv, mov = rows[0][...], rows[1][...]
        gt, g, sc, sh = [r[...] for r in fulls]
        x1 = xv + gt * mov
        outs[0][...] = x1
        _, xh = _rms(x1)
        outs[1][...] = (xh * g * (1.0 + sc) + sh).astype(bf16)

    x1, h2 = _rowwise("prenorm_ffn", e6, s, te, [(x, d, 0), (mo, d, 0)], [gt_m, g_ffn, sc_f, sh_f], [(d, f32), (d, bf16)])
    w_gt, w_ut, w_d = get_w("ffn", h2)
    dff_ = w_d.shape[0]
    ffb = dff_ // 4

    def ffn_in(name, w):
        return _matmul(name, h2, w, (s, dff_), f32, (4, mt, 1), pl.BlockSpec((tm, d), lambda j, i, k: (i, 0)),
                       pl.BlockSpec((ffb, d), lambda j, i, k: (j, 0)), pl.BlockSpec((tm, ffb), lambda j, i, k: (i, j)), NT)

    gg = ffn_in("ffn_gate", w_gt)
    uu = ffn_in("ffn_up", w_ut)

    def e8(rows, fulls, outs, accs):
        gv, uv = rows[0][...], rows[1][...]
        outs[0][...] = (gv * _sigmoid(gv) * uv).astype(bf16)

    act, = _rowwise("swiglu", e8, s, te, [(gg, ffb, 0), (uu, ffb, 0)], [], [(dff_, bf16)], ncol=4)
    ff = _matmul("ffn_down", act, w_d, (s, d), f32, (2, mt, 4), pl.BlockSpec((tm, ffb), lambda j, i, k: (i, k)),
                 pl.BlockSpec((ffb, dh2), lambda j, i, k: (k, j)), pl.BlockSpec((tm, dh2), lambda j, i, k: (i, j)), NN)

    def e9(rows, fulls, outs, accs):
        x1v, ffv, tv = [r[...] for r in rows]
        gt, gf = fulls[0][...], fulls[1][...]
        x2 = x1v + gt * ffv
        r, xh = _rms(x2)
        err = xh * gf - tv
        accs[0][...] += 0.5 * jnp.sum(jnp.mean(err * err, axis=-1, keepdims=True), axis=0, keepdims=True)
        dy = err / d
        accs[1][...] += _colsum(dy * xh)
        dxh = dy * gf
        dx2 = r * (dxh - xh * jnp.mean(dxh * xh, axis=-1, keepdims=True))
        outs[0][...] = dx2
        outs[1][...] = (dx2 * gt).astype(bf16)
        accs[2][...] += _colsum(dx2 * ffv)

    dx2, dff, loss_acc, dg_fin, dgt_f = _rowwise("loss_head", e9, s, te, [(x1, d, 0), (ff, d, 0), (tgt, d, 0)], [gt_f, g_fin],
                                                 [(d, f32), (d, bf16)], [(1, 128), (1, d), (1, d)])
    dact = _matmul("d_ffn_down_in", dff, w_d, (s, dff_), f32, (4, mt, 1), pl.BlockSpec((tm, d), lambda j, i, k: (i, 0)),
                   pl.BlockSpec((ffb, d), lambda j, i, k: (j, 0)), pl.BlockSpec((tm, ffb), lambda j, i, k: (i, j)), NT)

    def wgrad_rows(name, a, b, n_out, kb):
        return _matmul(name, a, b, (n_out, d), bf16, (n_out // kb, 2, mt), pl.BlockSpec((tm, kb), lambda i, j, k: (k, i)),
                       pl.BlockSpec((tm, dh2), lambda i, j, k: (k, j)), pl.BlockSpec((kb, dh2), lambda i, j, k: (i, j)), TN)

    gw_d = wgrad_rows("gw_ffn_down", act, dff, dff_, ffb)

    def b3(rows, fulls, outs, accs):
        da, gv, uv = [r[...] for r in rows]
        sg = _sigmoid(gv)
        outs[0][...] = (da * uv * _dsilu(gv, sg)).astype(bf16)
        outs[1][...] = (da * gv * sg).astype(bf16)

    dgg, duu = _rowwise("d_swiglu", b3, s, te, [(dact, ffb, 0), (gg, ffb, 0), (uu, ffb, 0)], [], [(dff_, bf16), (dff_, bf16)], ncol=4)

    def ffn_in_bwd(name, a, w):
        return _matmul(name, a, w, (s, d), f32, (2, mt, 4), pl.BlockSpec((tm, ffb), lambda j, i, k: (i, k)),
                       pl.BlockSpec((ffb, dh2), lambda j, i, k: (k, j)), pl.BlockSpec((tm, dh2), lambda j, i, k: (i, j)), NN)

    dh2a = ffn_in_bwd("d_ffn_gate_in", dgg, w_gt)
    dh2b = ffn_in_bwd("d_ffn_up_in", duu, w_ut)
    gw_gt = wgrad_rows("gw_ffn_gate", dgg, h2, dff_, ffb)
    gw_ut = wgrad_rows("gw_ffn_up", duu, h2, dff_, ffb)
    put_g("ffn", [gw_gt, gw_ut, gw_d])

    def b5(rows, fulls, outs, accs):
        da, db, x1v, dx2v, mov = [r[...] for r in rows]
        sc, g, gt = [r[...] for r in fulls]
        dh = da + db
        r, xh = _rms(x1v)
        accs[0][...] += _colsum(dh)
        accs[1][...] += _colsum(dh * (xh * g))
        dn = dh * (1.0 + sc)
        accs[2][...] += _colsum(dn * xh)
        dxh = dn * g
        dx1 = dx2v + r * (dxh - xh * jnp.mean(dxh * xh, axis=-1, keepdims=True))
        outs[0][...] = dx1
        accs[3][...] += _colsum(dx1 * mov)
        outs[1][...] = (dx1 * gt).astype(bf16)

    dx1, dmo, dsh_f, dsc_f, dg_ffn, dgt_m = _rowwise(
        "d_prenorm_ffn", b5, s, te, [(dh2a, d, 0), (dh2b, d, 0), (x1, d, 0), (dx2, d, 0), (mo, d, 0)], [sc_f, g_ffn, gt_m],
        [(d, f32), (d, bf16)], [(1, d)] * 4)
    dmerged = _matmul("d_mix_out_in", dmo, w_o, (s, d), f32, (2, mt, 1), pl.BlockSpec((tm, d), lambda j, i, k: (i, 0)),
                      pl.BlockSpec((dh2, d), lambda j, i, k: (j, 0)), pl.BlockSpec((tm, dh2), lambda j, i, k: (i, j)), NT)
    gw_o = wgrad_rows("gw_mix_out", merged, dmo, d, dh2)

    def b7(rows, fulls, outs, accs):
        dm, ya, yb, ga, gb = [r[...] for r in rows]
        sa, sb_ = _sigmoid(ga), _sigmoid(gb)
        outs[0][...] = (dm * sa).astype(bf16)
        outs[1][...] = (dm * sb_).astype(bf16)
        outs[2][...] = (dm * ya * sa * (1.0 - sa)).astype(bf16)
        outs[3][...] = (dm * yb * sb_ * (1.0 - sb_)).astype(bf16)

    dya, dyb, dga, dgb = _rowwise("d_merge", b7, s, te, [(dmerged, dh2, 0), (y_a, dh2, 0), (y_b, dh2, 0), (proj, dh2, ga0), (proj, dh2, gb0)],
                                  [], [(d, bf16)] * 4, ncol=2)

    def out_proj_bwd(name, dy, w):
        return _matmul(name, dy, w, (s, dc), f32, (1, s // tm2, nsh), pl.BlockSpec((tm2, d // nsh), lambda j, i, k: (i, k)),
                       pl.BlockSpec((None, dc, d // nsh), lambda j, i, k: (k, 0, 0)), pl.BlockSpec((tm2, dc), lambda j, i, k: (i, 0)), NT)

    def out_proj_wgrad(name, a, dy):
        return _matmul(name, a, dy, (nsh, dc, d // nsh), bf16, (1, nsh, mt), pl.BlockSpec((tm, dc), lambda i, j, k: (k, 0)),
                       pl.BlockSpec((tm, d // nsh), lambda i, j, k: (k, j)), pl.BlockSpec((None, dc, d // nsh), lambda i, j, k: (j, 0, 0)), TN)

    dz_a = out_proj_bwd("d_conv_out_in", dya, w_co)
    dog = out_proj_bwd("d_hgrn_out_in", dyb, w_ho)
    gw_co = out_proj_wgrad("gw_conv_out", z_a, dya)
    gw_ho = out_proj_wgrad("gw_hgrn_out", og, dyb)
    put_g("mix", [gw_co, gw_ho, gw_o])
    dab, dac, dax, dconv_w = _conv_bwd(proj, conv_w, dz_a, s, dc)
    dq, dfl, dvi, dgo, dlb, dgn = _hgrn_bwd(proj, lb_param, gnorm, o_saved, states, dog, s, dc, tb)
    dproj = jnp.concatenate([dab, dac, dax, dq, dfl, dvi, dgo, dga, dgb], axis=1)
    gw_in = _matmul("gw_proj", h, dproj, (nsh, d, win_sh), bf16, (2, nsh, mt), pl.BlockSpec((tm, dh2), lambda i, j, k: (k, i)),
                    pl.BlockSpec((tm, win_sh), lambda i, j, k: (k, j)), pl.BlockSpec((None, dh2, win_sh), lambda i, j, k: (j, i, 0)), TN)
    put_g("in", [gw_in])
    dh = _matmul("d_proj_in", dproj, w_in, (s, d), f32, (1, mt, nsh), pl.BlockSpec((tm, win_sh), lambda j, i, k: (i, k)),
                 pl.BlockSpec((None, d, win_sh), lambda j, i, k: (k, 0, 0)), pl.BlockSpec((tm, d), lambda j, i, k: (i, 0)), NT)

    def b12(rows, fulls, outs, accs):
        dhv, xv, dx1v = [r[...] for r in rows]
        sc, g = fulls[0][...], fulls[1][...]
        r, xh = _rms(xv)
        accs[0][...] += _colsum(dhv)
        accs[1][...] += _colsum(dhv * (xh * g))
        dn = dhv * (1.0 + sc)
        accs[2][...] += _colsum(dn * xh)
        dxh = dn * g
        outs[0][...] = dx1v + r * (dxh - xh * jnp.mean(dxh * xh, axis=-1, keepdims=True))

    dx, dsh_m, dsc_m, dg_mix = _rowwise("d_prenorm_mix", b12, s, te, [(dh, d, 0), (x, d, 0), (dx1, d, 0)], [sc_m, g_mix],
                                        [(d, f32)], [(1, d)] * 3)
    dmod = [dsh_m, dsc_m, dgt_m, dsh_f, dsc_f, dgt_f]
    small = dict(loss=loss_acc, g_mix=dg_mix, g_ffn=dg_ffn, g_fin=dg_fin, lb=dlb, gnorm=dgn, conv_w=dconv_w)
    return dx, dmod, small


def _ada_fwd(c_all, w_sh, b_sh):
    def body(c_ref, w_ref, b_ref, o_ref):
        cv = c_ref[...]
        ca = (cv * _sigmoid(cv)).astype(bf16)
        o_ref[...] = jnp.dot(ca, w_ref[...].astype(bf16), preferred_element_type=f32) + b_ref[...]

    return pl.pallas_call(body, name="ada_fwd", out_shape=SDS((c_all.shape[0], w_sh.shape[1]), f32),
                          compiler_params=pltpu.CompilerParams(vmem_limit_bytes=V7X_VMEM_LIMIT))(c_all, w_sh, b_sh)


def _ada_wgrad(c_all, dmod_sh):
    def body(c_ref, d_ref, o_ref):
        cv = c_ref[...]
        ca = (cv * _sigmoid(cv)).astype(bf16)
        o_ref[...] = lax.dot_general(ca, d_ref[...].astype(bf16), (TN, ((), ())), preferred_element_type=f32)

    return pl.pallas_call(body, name="ada_wgrad", out_shape=SDS((c_all.shape[1], dmod_sh.shape[1]), f32),
                          compiler_params=pltpu.CompilerParams(vmem_limit_bytes=V7X_VMEM_LIMIT))(c_all, dmod_sh)


def _lb_grad(lb_param, dlb):
    def body(p_ref, d_ref, o_ref):
        p = p_ref[...]
        lb = _sigmoid(p[0:1, :] - p[1:2, :])
        gl = d_ref[...] * lb * (1.0 - lb)
        o_ref[0:1, :] = gl
        o_ref[1:2, :] = -gl

    return pl.pallas_call(body, name="lb_grad", out_shape=SDS(lb_param.shape, f32))(lb_param, dlb)


def _sum_small(gathered):
    def body(g_ref, o_ref):
        acc = g_ref[0]
        for dd in range(1, NDEV):
            acc = acc + g_ref[dd]
        o_ref[...] = acc

    return pl.pallas_call(body, name="sum_small", out_shape=SDS(gathered.shape[1:], f32))(gathered)


def kernel(x, c, w_ada, b_ada, norm_mix_g, w_in, conv_w, lb_param, gnorm_g, w_conv_out, w_hgrn_out, w_o, norm_ffn_g, w_ffn_gate, w_ffn_up, w_ffn_down, norm_final_g, loss_target, m_w_ada, m_b_ada, m_norm_mix_g, m_w_in, m_conv_w, m_lb_param, m_gnorm_g, m_w_conv_out, m_w_hgrn_out, m_w_o, m_norm_ffn_g, m_w_ffn_gate, m_w_ffn_up, m_w_ffn_down, m_norm_final_g, v_w_ada, v_b_ada, v_norm_mix_g, v_w_in, v_conv_w, v_lb_param, v_gnorm_g, v_w_conv_out, v_w_hgrn_out, v_w_o, v_norm_ffn_g, v_w_ffn_gate, v_w_ffn_up, v_w_ffn_down, v_norm_final_g):
    assert lb_param.shape[0] == 2 and w_ada.shape[0] == 1
    s, d = x.shape[1], x.shape[2]
    me = 4 * lax.axis_index("x") + 2 * lax.axis_index("y") + lax.axis_index("c")
    ada_cols = w_ada.shape[2]

    c_all, cw_all = _all_gather("gather_cond", [c, conv_w[0]])
    c_all = c_all.reshape(NDEV, d)
    conv_w_full = jnp.transpose(cw_all, (1, 0, 2)).reshape(conv_w.shape[1], -1)
    b_sh = lax.dynamic_slice_in_dim(b_ada, me * ada_cols, ada_cols, axis=1)
    mod_cols = _ada_fwd(c_all, w_ada[0], b_sh)
    mod_all, = _all_gather("gather_mod", [mod_cols])
    mod = lax.dynamic_index_in_dim(mod_all, me, axis=1, keepdims=False).reshape(N_MOD, 1, d)

    shard_groups = {"in": [w_in[0].astype(bf16)],
                    "mix": [w_conv_out[0].astype(bf16), w_hgrn_out[0].astype(bf16), w_o[0].astype(bf16)],
                    "ffn": [w_ffn_gate[0].T.astype(bf16), w_ffn_up[0].T.astype(bf16), w_ffn_down[0].astype(bf16)]}
    gather_plan = lambda n: [(a, j, None, lambda frm, to: _flat(frm)) for a in range(n) for j in (1,) + ICI_RELATIONS]
    flat = lambda a: a.reshape(a.shape[0] * a.shape[1], a.shape[2])
    to8 = lambda a: a.reshape(NDEV, a.shape[0] // NDEV, a.shape[1])
    gathering, token = {}, 0.0
    for grp, sh in shard_groups.items():
        lands = [lax.empty((NDEV,) + a.shape, a.dtype) for a in sh]
        ss, rs, srcs, lands, tok = _push_start("gather_start_" + grp, sh, lands, gather_plan(len(sh)))
        gathering[grp] = (ss, rs, srcs, lands)
        token = token + tok[0, 0]

    def get_w(grp, after):
        ss, rs, srcs, lands = gathering[grp]
        srcs, lands = _push_wait("gather_wait_" + grp, ss, rs, srcs, lands, gather_plan(len(srcs)), after)
        full = _sibling_forward("gather_fwd_" + grp, srcs, lands)
        return [f if grp == "in" or i < 2 and grp == "mix" else flat(f) for i, f in enumerate(full)]

    core = lax.axis_index("c").astype(jnp.int32).reshape(1)
    chip = (2 * lax.axis_index("x") + lax.axis_index("y")).astype(jnp.int32).reshape(1)
    scatter_plan = lambda n: [(a, j, lambda frm, to: _chip(to), lambda frm, to: _chip(frm)) for a in range(n) for j in ICI_RELATIONS]
    scattering = {}

    def put_g(grp, grads):
        g8 = [g if g.ndim == 3 else to8(g) for g in grads]
        recv = _sibling_swap("scatter_pair_" + grp, g8)
        pairs = [_pair_sum(f"pair_sum_{grp}{i}", core, g, r, _row_tile(g.shape[1])) for i, (g, r) in enumerate(zip(g8, recv))]
        lands = [lax.empty(p.shape, p.dtype) for p in pairs]
        scattering[grp] = _push_start("scatter_start_" + grp, pairs, lands, scatter_plan(len(pairs)))[:4]

    def reduced(grp, after):
        ss, rs, srcs, lands = scattering[grp]
        srcs, lands = _push_wait("scatter_wait_" + grp, ss, rs, srcs, lands, scatter_plan(len(srcs)), after)
        return [_chip_sum(f"chip_sum_{grp}{i}", chip, p, r, _row_tile(p.shape[1])) for i, (p, r) in enumerate(zip(srcs, lands))]

    dx, dmod, small = _local_step(x[0], loss_target[0], mod, norm_mix_g + token, norm_ffn_g, norm_final_g.reshape(1, d), lb_param,
                                  gnorm_g, conv_w_full, get_w, put_g)

    pieces = [*dmod, small["g_mix"], small["g_ffn"], small["g_fin"], small["lb"], small["gnorm"], small["loss"],
              small["conv_w"].reshape(1, -1)]
    widths = [p.shape[1] for p in pieces]
    offs = np.concatenate([[0], np.cumsum(widths)])
    packed = jnp.concatenate(pieces, axis=1)
    gathered, = _all_gather("gather_small", [packed])
    summed = _sum_small(gathered)
    part = lambda i: summed[:, offs[i]:offs[i + 1]]
    g_b_ada = summed[:, :N_MOD * d]
    g_norm_mix, g_norm_ffn, g_norm_fin, g_lb_row, g_gnorm, loss_vec, g_convw_flat = [part(i) for i in range(N_MOD, N_MOD + 7)]
    loss = loss_vec[0, 0]
    dmod_all = gathered[:, 0, :N_MOD * d]
    g_w_ada = _ada_wgrad(c_all, lax.dynamic_slice_in_dim(dmod_all, me * ada_cols, ada_cols, axis=1))
    g_lb = _lb_grad(lb_param, g_lb_row)
    cw_cols = conv_w.shape[2]
    g_conv_w = lax.dynamic_slice_in_dim(g_convw_flat.reshape(conv_w.shape[1], -1), me * cw_cols, cw_cols, axis=1)

    grads = dict(w_ada=g_w_ada, b_ada=g_b_ada, norm_mix_g=g_norm_mix, conv_w=g_conv_w, lb_param=g_lb, gnorm_g=g_gnorm,
                 norm_ffn_g=g_norm_ffn, norm_final_g=g_norm_fin)
    weights = dict(w_ada=(w_ada, m_w_ada, v_w_ada), b_ada=(b_ada, m_b_ada, v_b_ada), norm_mix_g=(norm_mix_g, m_norm_mix_g, v_norm_mix_g),
                   w_in=(w_in, m_w_in, v_w_in), conv_w=(conv_w, m_conv_w, v_conv_w), lb_param=(lb_param, m_lb_param, v_lb_param),
                   gnorm_g=(gnorm_g, m_gnorm_g, v_gnorm_g), w_conv_out=(w_conv_out, m_w_conv_out, v_w_conv_out),
                   w_hgrn_out=(w_hgrn_out, m_w_hgrn_out, v_w_hgrn_out), w_o=(w_o, m_w_o, v_w_o),
                   norm_ffn_g=(norm_ffn_g, m_norm_ffn_g, v_norm_ffn_g), w_ffn_gate=(w_ffn_gate, m_w_ffn_gate, v_w_ffn_gate),
                   w_ffn_up=(w_ffn_up, m_w_ffn_up, v_w_ffn_up), w_ffn_down=(w_ffn_down, m_w_ffn_down, v_w_ffn_down),
                   norm_final_g=(norm_final_g, m_norm_final_g, v_norm_final_g))
    res = {}

    def update(nm):
        w, m, v = weights[nm]
        shape2 = (w.shape[-2], w.shape[-1]) if w.ndim >= 2 else (1, w.shape[0])
        g2 = grads[nm].reshape(shape2)
        dl, mn, vn = _adamw("adamw_" + nm, g2, w.reshape(shape2), m.reshape(shape2), v.reshape(shape2))
        res[nm] = [a.reshape(w.shape) for a in (g2, dl, mn, vn)]

    for nm in list(grads):
        update(nm)
    g_w_gt, g_w_ut, grads["w_ffn_down"] = reduced("ffn", res["w_ada"][1])
    grads["w_ffn_gate"], grads["w_ffn_up"] = g_w_gt.T, g_w_ut.T
    for nm in ("w_ffn_gate", "w_ffn_up", "w_ffn_down"):
        update(nm)
    grads["w_conv_out"], grads["w_hgrn_out"], grads["w_o"] = reduced("mix", res["w_ffn_down"][1])
    for nm in ("w_conv_out", "w_hgrn_out", "w_o"):
        update(nm)
    grads["w_in"], = reduced("in", res["w_o"][1])
    update("w_in")
    outs = [[res[nm][i] for nm in weights] for i in range(4)]
    return (loss, dx.reshape(x.shape), *outs[0], *outs[1], *outs[2], *outs[3])
```

```python
import functools

import numpy as np
import jax
import jax.numpy as jnp
from jax import lax
from jax.experimental import pallas as pl
from jax.experimental.pallas import tpu as pltpu

f32, bf16 = jnp.float32, jnp.bfloat16
SDS = jax.ShapeDtypeStruct

EPS = 1e-6
HEADS, DK, CHUNK = 8, 128, 64
N_MOD = 6
NDEV = 8
ADAM_LR, ADAM_B1, ADAM_B2, ADAM_EPS, ADAM_WD, ADAM_STEP = 0.001, 0.9, 0.999, 1e-08, 0.01, 10
LEVELS = (32, 16, 8, 4, 2, 1)
V7X_VMEM_LIMIT = 56 * 1024 * 1024
HBM = pl.BlockSpec(memory_space=pltpu.HBM)
MESH = pl.DeviceIdType.MESH


def _params(sem):
    return pltpu.CompilerParams(dimension_semantics=sem, vmem_limit_bytes=V7X_VMEM_LIMIT)


def _sigmoid(x):
    return jax.nn.sigmoid(x)


def _dsilu(x, s):
    return s * (1.0 + x * (1.0 - s))


def _mesh_pos():
    x, y, c = lax.axis_index("x"), lax.axis_index("y"), lax.axis_index("c")
    return x, y, c


def _peer(pos, j):
    x, y, c = pos
    return (1 - x if j & 4 else x, 1 - y if j & 2 else y, 1 - c if j & 1 else c)


def _flat(pos):
    return 4 * pos[0] + 2 * pos[1] + pos[2]


def _all_gather(name, arrs):
    n = len(arrs)
    out_shapes = [SDS((NDEV,) + a.shape, a.dtype) for a in arrs]

    def body(*refs):
        ins, outs = refs[:n], refs[n:2 * n]
        send_sems, recv_sems, local_sems = refs[2 * n:]
        pos = _mesh_pos()
        me = _flat(pos)

        def copy(a, j, frm, to_pos):
            k = a * (NDEV - 1) + j - 1
            return pltpu.make_async_remote_copy(src_ref=ins[a], dst_ref=outs[a].at[frm], send_sem=send_sems.at[k],
                                                recv_sem=recv_sems.at[k], device_id=to_pos, device_id_type=MESH)

        local = [pltpu.make_async_copy(ins[a], outs[a].at[me], local_sems.at[a]) for a in range(n)]
        for cp in local:
            cp.start()
        sends = [copy(a, j, me, _peer(pos, j)) for j in range(1, NDEV) for a in range(n)]
        for cp in sends:
            cp.start()
        for j in range(1, NDEV):
            for a in range(n):
                copy(a, j, _flat(_peer(pos, j)), pos).wait_recv()
        for cp in sends:
            cp.wait_send()
        for cp in local:
            cp.wait()

    return pl.pallas_call(
        body, name=name, out_shape=out_shapes, in_specs=[HBM] * n, out_specs=[HBM] * n,
        scratch_shapes=[pltpu.SemaphoreType.DMA((n * (NDEV - 1),)), pltpu.SemaphoreType.DMA((n * (NDEV - 1),)),
                        pltpu.SemaphoreType.DMA((n,))],
    )(*arrs)


SEM = pl.BlockSpec(memory_space=pltpu.SEMAPHORE)
ANY = pl.BlockSpec(memory_space=pl.ANY)
EFFECT = pltpu.SideEffectType.DATAFLOW_SIDE_EFFECTING
ICI_RELATIONS = (2, 4, 6)


def _chip(pos):
    return 2 * pos[0] + pos[1]


def _hbm(a):
    return pltpu.with_memory_space_constraint(a, pltpu.HBM)


def _plan_copy(plan_entry, k, pos, frm, to, src_refs, land_refs, send_sems, recv_sems):
    a, _, src_slot, dst_slot = plan_entry
    s = src_refs[a] if src_slot is None else src_refs[a].at[src_slot(frm, to)]
    return pltpu.make_async_remote_copy(src_ref=s, dst_ref=land_refs[a].at[dst_slot(frm, to)], send_sem=send_sems.at[k],
                                        recv_sem=recv_sems.at[k], device_id=to, device_id_type=MESH)


def _push_start(name, srcs, lands, plan):
    ns, nb, nk = len(srcs), len(srcs) + len(lands), len(plan)

    def body(*refs):
        land_refs = refs[ns:nb]
        src_refs = refs[:ns] if ns else land_refs
        send_sems, recv_sems = refs[nb], refs[nb + 1]
        pos = _mesh_pos()
        for k, e in enumerate(plan):
            _plan_copy(e, k, pos, pos, _peer(pos, e[1]), src_refs, land_refs, send_sems, recv_sems).start()
        refs[-1][...] = jnp.zeros_like(refs[-1])

    outs = pl.pallas_call(
        body, name=name,
        out_shape=[pltpu.SemaphoreType.DMA((nk,)), pltpu.SemaphoreType.DMA((nk,))] + [pltpu.HBM(a.shape, a.dtype) for a in srcs + lands]
        + [SDS((8, 128), f32)],
        in_specs=[HBM] * nb, out_specs=[SEM, SEM] + [HBM] * nb + [pl.BlockSpec(memory_space=pltpu.VMEM)],
        input_output_aliases={i: 2 + i for i in range(nb)},
        compiler_params=pltpu.CompilerParams(has_side_effects=EFFECT),
    )(*[_hbm(a) for a in srcs + lands])
    return outs[0], outs[1], list(outs[2:2 + ns]), list(outs[2 + ns:2 + nb]), outs[-1]


def _push_wait(name, send_sems, recv_sems, srcs, lands, plan, after):
    ns, nb = len(srcs), len(srcs) + len(lands)

    def body(*refs):
        land_refs = refs[ns:nb]
        src_refs = refs[:ns] if ns else land_refs
        ssem, rsem = refs[nb], refs[nb + 1]
        pos = _mesh_pos()
        for k, e in enumerate(plan):
            peer = _peer(pos, e[1])
            _plan_copy(e, k, pos, pos, peer, src_refs, land_refs, ssem, rsem).wait_send()
            _plan_copy(e, k, pos, peer, pos, src_refs, land_refs, ssem, rsem).wait_recv()

    outs = pl.pallas_call(
        body, name=name, out_shape=[pltpu.HBM(a.shape, a.dtype) for a in srcs + lands],
        in_specs=[HBM] * nb + [SEM, SEM, ANY], out_specs=[HBM] * nb,
        input_output_aliases={i: i for i in range(nb)},
        compiler_params=pltpu.CompilerParams(has_side_effects=EFFECT),
    )(*srcs, *lands, send_sems, recv_sems, after)
    return list(outs[:ns]), list(outs[ns:])


def _pin(value, token):
    return value if token is None else lax.optimization_barrier((value, token))[0]


def _place_shard(name, me, shard):
    r, c = shard.shape
    tr = _row_tile(r)

    def body(me_ref, s_ref, o_ref):
        o_ref[...] = s_ref[...].astype(bf16)

    return pl.pallas_call(
        body, name=name, out_shape=SDS((NDEV, r, c), bf16),
        grid_spec=pltpu.PrefetchScalarGridSpec(
            num_scalar_prefetch=1, grid=(r // tr,), in_specs=[pl.BlockSpec((tr, c), lambda i, me_ref: (i, 0))],
            out_specs=pl.BlockSpec((None, tr, c), lambda i, me_ref: (me_ref[0], i, 0))),
        compiler_params=_params(("parallel",)),
    )(me, shard)


def _row_tile(r):
    return max(t for t in range(16, 257, 16) if r % t == 0)


def _sibling_forward(name, lands):
    n = len(lands)
    nk = n * len(ICI_RELATIONS)

    def body(*refs):
        land_refs = refs[:n]
        send_sems, recv_sems = refs[2 * n:]
        pos = _mesh_pos()
        sib = _peer(pos, 1)

        def copy(a, i, frm, to):
            k = a * len(ICI_RELATIONS) + i
            slot = _flat(_peer(frm, ICI_RELATIONS[i]))
            return pltpu.make_async_remote_copy(src_ref=land_refs[a].at[slot], dst_ref=land_refs[a].at[slot], send_sem=send_sems.at[k],
                                                recv_sem=recv_sems.at[k], device_id=to, device_id_type=MESH)

        sends = [copy(a, i, pos, sib) for a in range(n) for i in range(len(ICI_RELATIONS))]
        for cp in sends:
            cp.start()
        for a in range(n):
            for i in range(len(ICI_RELATIONS)):
                copy(a, i, sib, pos).wait_recv()
        for cp in sends:
            cp.wait_send()

    outs = pl.pallas_call(
        body, name=name, out_shape=[SDS(a.shape, a.dtype) for a in lands], in_specs=[HBM] * n, out_specs=[HBM] * n,
        input_output_aliases={i: i for i in range(n)},
        scratch_shapes=[pltpu.SemaphoreType.DMA((nk,)), pltpu.SemaphoreType.DMA((nk,))],
    )(*lands)
    return list(outs)


def _sibling_swap(name, grads):
    n = len(grads)
    nchip = NDEV // 2

    def body(*refs):
        g_refs, out_refs = refs[:n], refs[n:2 * n]
        send_sems, recv_sems = refs[2 * n:]
        pos = _mesh_pos()
        sib = _peer(pos, 1)
        sends = []
        for a in range(n):
            for q in range(nchip):
                k = a * nchip + q
                sends.append(pltpu.make_async_remote_copy(src_ref=g_refs[a].at[2 * q + sib[2]], dst_ref=out_refs[a].at[q],
                                                          send_sem=send_sems.at[k], recv_sem=recv_sems.at[k], device_id=sib,
                                                          device_id_type=MESH))
        for cp in sends:
            cp.start()
        for cp in sends:
            cp.wait()

    return pl.pallas_call(
        body, name=name, out_shape=[SDS((nchip,) + a.shape[1:], a.dtype) for a in grads], in_specs=[HBM] * n, out_specs=[HBM] * n,
        scratch_shapes=[pltpu.SemaphoreType.DMA((n * nchip,)), pltpu.SemaphoreType.DMA((n * nchip,))],
    )(*grads)


def _pair_sum(name, core, grad, recv, tr):
    _, r, c = grad.shape
    nchip = NDEV // 2

    def body(core_ref, g_ref, r_ref, o_ref):
        o_ref[...] = (g_ref[...].astype(f32) + r_ref[...].astype(f32)).astype(o_ref.dtype)

    return pl.pallas_call(
        body, name=name, out_shape=SDS((nchip, r, c), grad.dtype),
        grid_spec=pltpu.PrefetchScalarGridSpec(
            num_scalar_prefetch=1, grid=(nchip, r // tr),
            in_specs=[pl.BlockSpec((None, tr, c), lambda q, i, core_ref: (2 * q + core_ref[0], i, 0)),
                      pl.BlockSpec((None, tr, c), lambda q, i, core_ref: (q, i, 0))],
            out_specs=pl.BlockSpec((None, tr, c), lambda q, i, core_ref: (q, i, 0))),
        compiler_params=_params(("parallel", "parallel")),
    )(core, grad, recv)


def _chip_sum(name, chip, pairs, recv, tr):
    nchip, r, c = pairs.shape

    def body(chip_ref, p_ref, r_ref, o_ref):
        mine = chip_ref[0]
        acc = jnp.zeros(o_ref.shape, f32)
        for q in range(nchip):
            acc = acc + jnp.where(mine == q, p_ref[q].astype(f32), r_ref[q].astype(f32))
        o_ref[...] = acc

    return pl.pallas_call(
        body, name=name, out_shape=SDS((r, c), f32),
        grid_spec=pltpu.PrefetchScalarGridSpec(
            num_scalar_prefetch=1, grid=(r // tr,),
            in_specs=[pl.BlockSpec((nchip, tr, c), lambda i, chip_ref: (0, i, 0))] * 2,
            out_specs=pl.BlockSpec((tr, c), lambda i, chip_ref: (i, 0))),
        compiler_params=_params(("parallel",)),
    )(chip, pairs, recv)


def _matmul(name, a, b, out_shape, out_dtype, grid, a_spec, b_spec, o_spec, dims):
    ksteps = grid[2]
    acc_shape = tuple(d for d in o_spec.block_shape if d is not None)

    def body(a_ref, b_ref, o_ref, *acc):
        prod = lax.dot_general(a_ref[...], b_ref[...], (dims, ((), ())), preferred_element_type=f32)
        if ksteps == 1:
            o_ref[...] = prod.astype(o_ref.dtype)
        else:
            k = pl.program_id(2)

            @pl.when(k == 0)
            def _():
                acc[0][...] = prod

            @pl.when(k > 0)
            def _():
                acc[0][...] += prod

            @pl.when(k == ksteps - 1)
            def _():
                o_ref[...] = acc[0][...].astype(o_ref.dtype)

    return pl.pallas_call(
        body, name=name, grid=grid, in_specs=[a_spec, b_spec], out_specs=o_spec, out_shape=SDS(out_shape, out_dtype),
        scratch_shapes=[] if ksteps == 1 else [pltpu.VMEM(acc_shape, f32)],
        compiler_params=_params(("parallel", "parallel", "arbitrary")),
    )(a, b)


NN, NT, TN = ((1,), (0,)), ((1,), (1,)), ((0,), (0,))


def _rowwise(name, fn, n_rows, tm, rows, fulls, row_outs, acc_outs=(), ncol=1):
    assert ncol == 1 or not acc_outs
    nr, nf, no, na = len(rows), len(fulls), len(row_outs), len(acc_outs)
    in_specs = [pl.BlockSpec((tm, w), functools.partial(lambda i, j, cb: (i, cb + j), cb=cb)) for (_, w, cb) in rows]
    in_specs += [pl.BlockSpec(a.shape, functools.partial(lambda i, j, nd: (0,) * nd, nd=a.ndim)) for a in fulls]
    out_shape = [SDS((n_rows, w), dt) for (w, dt) in row_outs] + [SDS(s, f32) for s in acc_outs]
    out_specs = [pl.BlockSpec((tm, w // ncol), lambda i, j: (i, j)) for (w, _) in row_outs]
    out_specs += [pl.BlockSpec(s, functools.partial(lambda i, j, nd: (0,) * nd, nd=len(s))) for s in acc_outs]

    def body(*refs):
        if na:
            @pl.when(pl.program_id(0) == 0)
            def _():
                for r in refs[nr + nf + no:]:
                    r[...] = jnp.zeros(r.shape, r.dtype)
        fn(refs[:nr], refs[nr:nr + nf], refs[nr + nf:nr + nf + no], refs[nr + nf + no:])

    return pl.pallas_call(
        body, name=name, grid=(n_rows // tm, ncol), in_specs=in_specs, out_specs=out_specs, out_shape=out_shape,
        compiler_params=_params(("arbitrary" if na else "parallel", "arbitrary" if na else "parallel")),
    )(*[r[0] for r in rows], *fulls)


def _rms(x):
    r = lax.rsqrt(jnp.mean(x * x, axis=-1, keepdims=True) + EPS)
    return r, x * r


def _colsum(v):
    return jnp.sum(v, axis=0, keepdims=True)


def _shift_down(u, row, k):
    return jnp.where(row >= k, pltpu.roll(u, k, 0), 0.0)


def _shift_up(u, row, k):
    n = u.shape[0]
    return jnp.where(row < n - k, pltpu.roll(u, n - k, 0), 0.0)


def _conv_fwd(proj, conv_w, s, dc):
    nb = dc // 128

    def body(ab_ref, ac_ref, ax_ref, w_ref, z_ref):
        u = ac_ref[...] * ax_ref[...]
        row = lax.broadcasted_iota(jnp.int32, u.shape, 0)
        w = w_ref[...]
        cv = w[0:1] * _shift_down(u, row, 2) + w[1:2] * _shift_down(u, row, 1) + w[2:3] * u
        z_ref[...] = (ab_ref[...] * cv).astype(z_ref.dtype)

    col = lambda off: pl.BlockSpec((s, 128), functools.partial(lambda j, off: (0, off + j), off=off))
    return pl.pallas_call(
        body, name="conv_fwd", grid=(nb,), in_specs=[col(0), col(nb), col(2 * nb), pl.BlockSpec((3, 128), lambda j: (0, j))],
        out_specs=pl.BlockSpec((s, 128), lambda j: (0, j)), out_shape=SDS((s, dc), bf16), compiler_params=_params(("parallel",)),
    )(proj, proj, proj, conv_w)


def _conv_bwd(proj, conv_w, dz, s, dc):
    nb = dc // 128

    def body(ab_ref, ac_ref, ax_ref, w_ref, dz_ref, dab_ref, dac_ref, dax_ref, dw_ref):
        ab, ac, ax, dzv = ab_ref[...], ac_ref[...], ax_ref[...], dz_ref[...]
        u = ac * ax
        row = lax.broadcasted_iota(jnp.int32, u.shape, 0)
        w = w_ref[...]
        u1, u2 = _shift_down(u, row, 1), _shift_down(u, row, 2)
        cv = w[0:1] * u2 + w[1:2] * u1 + w[2:3] * u
        dcv = dzv * ab
        dab_ref[...] = (dzv * cv).astype(dab_ref.dtype)
        du = w[2:3] * dcv + w[1:2] * _shift_up(dcv, row, 1) + w[0:1] * _shift_up(dcv, row, 2)
        dac_ref[...] = (du * ax).astype(dac_ref.dtype)
        dax_ref[...] = (du * ac).astype(dax_ref.dtype)
        dw_ref[0:1, :] = _colsum(dcv * u2)
        dw_ref[1:2, :] = _colsum(dcv * u1)
        dw_ref[2:3, :] = _colsum(dcv * u)

    col = lambda off: pl.BlockSpec((s, 128), functools.partial(lambda j, off: (0, off + j), off=off))
    blk = pl.BlockSpec((s, 128), lambda j: (0, j))
    return pl.pallas_call(
        body, name="conv_bwd", grid=(nb,),
        in_specs=[col(0), col(nb), col(2 * nb), pl.BlockSpec((3, 128), lambda j: (0, j)), blk],
        out_specs=[blk, blk, blk, pl.BlockSpec((3, 128), lambda j: (0, j))],
        out_shape=[SDS((s, dc), bf16)] * 3 + [SDS((3, dc), f32)], compiler_params=_params(("parallel",)),
    )(proj, proj, proj, conv_w, dz)


def _level_masks():
    t = np.arange(CHUNK)[:, None]
    s = np.arange(CHUNK)[None, :]
    m = np.stack([((t & h) != 0) & ((s & h) == 0) & (t // (2 * h) == s // (2 * h)) for h in LEVELS]).astype(np.float32)
    return jnp.asarray(m), jnp.asarray(m.transpose(0, 2, 1))


def _cumsum_rows(x, row):
    for sh in (1, 2, 4, 8, 16, 32):
        x = x + jnp.where(row >= sh, pltpu.roll(x, sh, 0), 0.0)
    return x


def _rev_cumsum_rows(x, row):
    n = x.shape[0]
    for sh in (1, 2, 4, 8, 16, 32):
        x = x + jnp.where(row < n - sh, pltpu.roll(x, n - sh, 0), 0.0)
    return x


def _chunk_terms(qp, fl, lb):
    row = lax.broadcasted_iota(jnp.int32, qp.shape, 0)
    sig = _sigmoid(fl)
    f = lb + (1.0 - lb) * sig
    k = 1.0 - f
    sq = _sigmoid(qp)
    qh = qp * sq
    b = _cumsum_rows(jnp.log(f), row)
    sub = lax.broadcasted_iota(jnp.int32, (CHUNK // 8, 8, DK), 1)
    b8 = b.reshape(CHUNK // 8, 8, DK)
    us, exs, ups = [], [], []
    for m in LEVELS:
        sb = 2 * m
        if sb >= 8:
            b3 = b.reshape(CHUNK // sb, sb, DK)
            bref = jnp.broadcast_to(b3[:, m - 1:m, :], b3.shape).reshape(CHUNK, DK)
        else:
            bref8 = None
            for j in range(8 // sb):
                cand = jnp.broadcast_to(b8[:, j * sb + m - 1:j * sb + m, :], b8.shape)
                bref8 = cand if bref8 is None else jnp.where(sub >= j * sb, cand, bref8)
            bref = bref8.reshape(CHUNK, DK)
        up = (row & m) != 0
        ex = jnp.exp(jnp.where(up, b - bref, bref - b))
        us.append((jnp.where(up, qh, k) * ex).astype(bf16))
        exs.append(ex)
        ups.append(up)
    blast = b[CHUNK - 1:CHUNK, :]
    eb, ebl = jnp.exp(b), jnp.exp(blast - b)
    return dict(sig=sig, f=f, k=k, sq=sq, qh=qh, u=jnp.stack(us), ex=exs, up=ups, eb=eb, ebl=ebl, qt=qh * eb, kt=k * ebl,
                el=jnp.exp(blast), row=row)


def _scores(t, mask):
    pl_ = jnp.einsum("ltk,lsk->lts", t["u"], t["u"], preferred_element_type=f32)
    p = jnp.sum(pl_ * mask, axis=0)
    r = lax.broadcasted_iota(jnp.int32, (CHUNK, CHUNK), 0)
    c = lax.broadcasted_iota(jnp.int32, (CHUNK, CHUNK), 1)
    diag = jnp.sum(t["qh"] * t["k"], axis=-1, keepdims=True)
    return p + jnp.where(r == c, diag, 0.0)


def _hgrn_fwd(proj, lb_param, gnorm, s, dv_total, tb):
    nchunk = tb // CHUNK
    masks, _ = _level_masks()
    q0, f0, v0, g0 = 3 * HEADS, 4 * HEADS, 5 * HEADS, 6 * HEADS

    def body(q_ref, f_ref, v_ref, g_ref, lb_ref, gn_ref, mask_ref, og_ref, o_ref, st_ref, state):
        @pl.when(pl.program_id(1) == 0)
        def _():
            state[...] = jnp.zeros_like(state)

        lbp = lb_ref[...]
        lb = _sigmoid(lbp[0:1, :] - lbp[1:2, :])
        mask = mask_ref[...]
        for i in range(nchunk):
            rs = pl.ds(i * CHUNK, CHUNK)
            t = _chunk_terms(q_ref[rs, :], f_ref[rs, :], lb)
            v = v_ref[rs, :]
            vb = v.astype(bf16)
            st = state[...]
            st_ref[i] = st
            p = _scores(t, mask)
            o = jnp.dot(p.astype(bf16), vb, preferred_element_type=f32)
            o += lax.dot_general(t["qt"].astype(bf16), st.astype(bf16), (NT, ((), ())), preferred_element_type=f32)
            state[...] = st * t["el"] + lax.dot_general(vb, t["kt"].astype(bf16), (TN, ((), ())), preferred_element_type=f32)
            o_ref[rs, :] = o
            r, oh = _rms(o)
            g = g_ref[rs, :]
            og_ref[rs, :] = (oh * gn_ref[...] * (g * _sigmoid(g))).astype(og_ref.dtype)

    col = lambda off: pl.BlockSpec((tb, DK), functools.partial(lambda h, t, off: (t, off + h), off=off))
    blk = pl.BlockSpec((tb, DK), lambda h, t: (t, h))
    return pl.pallas_call(
        body, name="hgrn_fwd", grid=(HEADS, s // tb),
        in_specs=[col(q0), col(f0), col(v0), col(g0), pl.BlockSpec((2, DK), lambda h, t: (0, h)),
                  pl.BlockSpec((1, DK), lambda h, t: (0, 0)), pl.BlockSpec(masks.shape, lambda h, t: (0, 0, 0))],
        out_specs=[blk, blk, pl.BlockSpec((nchunk, None, DK, DK), lambda h, t: (t, h, 0, 0))],
        out_shape=[SDS((s, dv_total), bf16), SDS((s, dv_total), f32), SDS((s // CHUNK, HEADS, DK, DK), f32)],
        scratch_shapes=[pltpu.VMEM((DK, DK), f32)], compiler_params=_params(("parallel", "arbitrary")),
    )(proj, proj, proj, proj, lb_param, gnorm, masks)


def _hgrn_bwd(proj, lb_param, gnorm, o_saved, states, dog, s, dv_total, tb):
    nchunk = tb // CHUNK
    nt = s // tb
    nc_total = s // CHUNK
    masks, masks_t = _level_masks()
    q0, f0, v0, g0 = 3 * HEADS, 4 * HEADS, 5 * HEADS, 6 * HEADS

    def body(q_ref, f_ref, v_ref, g_ref, lb_ref, gn_ref, mask_ref, maskt_ref, o_ref, dog_ref, st_ref, stn_ref,
             dq_ref, df_ref, dv_ref, dg_ref, dlb_ref, dgn_ref, gstate):
        h_id, t_id = pl.program_id(0), pl.program_id(1)

        @pl.when(t_id == 0)
        def _():
            gstate[...] = jnp.zeros_like(gstate)
            dlb_ref[...] = jnp.zeros_like(dlb_ref)

        @pl.when((t_id == 0) & (h_id == 0))
        def _():
            dgn_ref[...] = jnp.zeros_like(dgn_ref)

        lbp = lb_ref[...]
        lb = _sigmoid(lbp[0:1, :] - lbp[1:2, :])
        mask, maskt = mask_ref[...], maskt_ref[...]
        gn = gn_ref[...]
        for i in reversed(range(nchunk)):
            rs = pl.ds(i * CHUNK, CHUNK)
            qp, fl, v, g = q_ref[rs, :], f_ref[rs, :], v_ref[rs, :], g_ref[rs, :]
            t = _chunk_terms(qp, fl, lb)
            vb = v.astype(bf16)
            st0 = st_ref[i]
            st1 = st_ref[i + 1] if i + 1 < nchunk else stn_ref[0]
            gt = gstate[...]
            o = o_ref[rs, :]
            r, oh = _rms(o)
            sg = _sigmoid(g)
            dog_v = dog_ref[rs, :]
            dg_ref[rs, :] = (dog_v * (oh * gn) * _dsilu(g, sg)).astype(dg_ref.dtype)
            don = dog_v * (g * sg)
            dgn_ref[...] += _colsum(don * oh)
            doh = don * gn
            do = r * (doh - oh * jnp.mean(doh * oh, axis=-1, keepdims=True))
            dob = do.astype(bf16)
            d = lax.dot_general(dob, vb, (NT, ((), ())), preferred_element_type=f32)
            dt = lax.dot_general(vb, dob, (NT, ((), ())), preferred_element_type=f32)
            z = (mask * d[None] + maskt * dt[None]).astype(bf16)
            rr = jnp.einsum("lts,lsk->ltk", z, t["u"], preferred_element_type=f32)
            dq = jnp.zeros((CHUNK, DK), f32)
            dk = jnp.zeros((CHUNK, DK), f32)
            qdk = jnp.zeros((CHUNK, DK), f32)
            for li in range(len(LEVELS)):
                du = t["ex"][li] * rr[li]
                dq += jnp.where(t["up"][li], du, 0.0)
                dk += jnp.where(t["up"][li], 0.0, du)
                e = t["u"][li].astype(f32) * rr[li]
                qdk += jnp.where(t["up"][li], e, -e)
            dd = jnp.sum(do * v, axis=-1, keepdims=True)
            dq += dd * t["k"]
            dk += dd * t["qh"]
            gtb = gt.astype(bf16)
            ktb, qtb = t["kt"].astype(bf16), t["qt"].astype(bf16)
            dq_in = jnp.dot(dob, st0.astype(bf16), preferred_element_type=f32)
            dk_in = jnp.dot(vb, gtb, preferred_element_type=f32)
            dq += t["eb"] * dq_in
            dk += t["ebl"] * dk_in
            qdk += qtb.astype(f32) * dq_in - ktb.astype(f32) * dk_in
            p = _scores(t, mask)
            dvv = lax.dot_general(p.astype(bf16), dob, (TN, ((), ())), preferred_element_type=f32)
            dvv += lax.dot_general(ktb, gtb, (NT, ((), ())), preferred_element_type=f32)
            dv_ref[rs, :] = dvv.astype(dv_ref.dtype)
            a_end = _colsum(gtb.astype(f32) * st1)
            dlf = _rev_cumsum_rows(qdk, t["row"]) + a_end
            dfv = dlf / t["f"] - dk
            df_ref[rs, :] = (dfv * (1.0 - lb) * t["sig"] * (1.0 - t["sig"])).astype(df_ref.dtype)
            dlb_ref[...] += _colsum(dfv * (1.0 - t["sig"]))
            dq_ref[rs, :] = (dq * _dsilu(qp, t["sq"])).astype(dq_ref.dtype)
            gstate[...] = gt * t["el"] + lax.dot_general(dob, qtb, (TN, ((), ())), preferred_element_type=f32)

    rev = lambda t: nt - 1 - t
    col = lambda off: pl.BlockSpec((tb, DK), functools.partial(lambda h, t, off: (rev(t), off + h), off=off))
    blk = pl.BlockSpec((tb, DK), lambda h, t: (rev(t), h))
    nxt = lambda h, t: (jnp.minimum((rev(t) + 1) * nchunk, nc_total - 1), h, 0, 0)
    return pl.pallas_call(
        body, name="hgrn_bwd", grid=(HEADS, nt),
        in_specs=[col(q0), col(f0), col(v0), col(g0), pl.BlockSpec((2, DK), lambda h, t: (0, h)),
                  pl.BlockSpec((1, DK), lambda h, t: (0, 0)), pl.BlockSpec(masks.shape, lambda h, t: (0, 0, 0)),
                  pl.BlockSpec(masks.shape, lambda h, t: (0, 0, 0)), blk, blk,
                  pl.BlockSpec((nchunk, None, DK, DK), lambda h, t: (rev(t), h, 0, 0)),
                  pl.BlockSpec((1, None, DK, DK), nxt)],
        out_specs=[blk, blk, blk, blk, pl.BlockSpec((1, DK), lambda h, t: (0, h)), pl.BlockSpec((1, DK), lambda h, t: (0, 0))],
        out_shape=[SDS((s, dv_total), bf16)] * 4 + [SDS((1, HEADS * DK), f32), SDS((1, DK), f32)],
        scratch_shapes=[pltpu.VMEM((DK, DK), f32)], compiler_params=_params(("arbitrary", "arbitrary")),
    )(proj, proj, proj, proj, lb_param, gnorm, masks, masks_t, o_saved, dog, states, states)


def _adamw(name, g, w, m, v):
    r, c = w.shape
    tr = r
    for cand in (256, 128, 64, 32, 16, 8):
        if r % cand == 0 and r > cand:
            tr = cand
            break

    def body(g_ref, w_ref, m_ref, v_ref, d_ref, mo_ref, vo_ref):
        gv = g_ref[...]
        mn = ADAM_B1 * m_ref[...] + (1.0 - ADAM_B1) * gv
        vn = ADAM_B2 * v_ref[...] + (1.0 - ADAM_B2) * jnp.square(gv)
        m_hat = mn / (1.0 - ADAM_B1 ** ADAM_STEP)
        v_hat = vn / (1.0 - ADAM_B2 ** ADAM_STEP)
        d_ref[...] = -ADAM_LR * (m_hat / (jnp.sqrt(v_hat) + ADAM_EPS) + ADAM_WD * w_ref[...])
        mo_ref[...] = mn
        vo_ref[...] = vn

    blk = pl.BlockSpec((tr, c), lambda i: (i, 0))
    return pl.pallas_call(
        body, name=name, grid=(r // tr,), in_specs=[blk] * 4, out_specs=[blk] * 3, out_shape=[SDS((r, c), f32)] * 3,
        compiler_params=_params(("parallel",)),
    )(g, w, m, v)


def _local_step(x, tgt, mod, g_mix, g_ffn, g_fin, lb_param, gnorm, conv_w, get_w, put_g):
    s, d = x.shape
    dc = d // 2
    tm = min(512, s)
    te = min(256, s)
    tb = min(256, s)
    mt = s // tm
    sh_m, sc_m, gt_m, sh_f, sc_f, gt_f = [mod[i] for i in range(N_MOD)]
    dh2 = d // 2

    def e1(rows, fulls, outs, accs):
        xv = rows[0][...]
        g, sc, sh = [r[...] for r in fulls]
        _, xh = _rms(xv)
        outs[0][...] = (xh * g * (1.0 + sc) + sh).astype(bf16)

    h, = _rowwise("prenorm_mix", e1, s, te, [(x, d, 0)], [g_mix, sc_m, sh_m], [(d, bf16)])
    w_in, = get_w("in", h)
    nsh, _, win_sh = w_in.shape
    proj = _matmul("proj", h, w_in, (s, nsh * win_sh), f32, (nsh, mt, 1), pl.BlockSpec((tm, d), lambda j, i, k: (i, 0)),
                   pl.BlockSpec((None, d, win_sh), lambda j, i, k: (j, 0, 0)), pl.BlockSpec((tm, win_sh), lambda j, i, k: (i, j)), NN)
    z_a = _conv_fwd(proj, conv_w, s, dc)
    og, o_saved, states = _hgrn_fwd(proj, lb_param, gnorm, s, dc, tb)
    tm2 = min(1024, s)
    w_co, w_ho, w_o = get_w("mix", og)

    def out_proj(name, a, w):
        return _matmul(name, a, w, (s, d), f32, (nsh, s // tm2, 1), pl.BlockSpec((tm2, dc), lambda j, i, k: (i, 0)),
                       pl.BlockSpec((None, dc, d // nsh), lambda j, i, k: (j, 0, 0)),
                       pl.BlockSpec((tm2, d // nsh), lambda j, i, k: (i, j)), NN)

    y_a = out_proj("conv_out", z_a, w_co)
    y_b = out_proj("hgrn_out", og, w_ho)
    ga0, gb0 = 7, 9

    def e5(rows, fulls, outs, accs):
        ya, yb, ga, gb = [r[...] for r in rows]
        outs[0][...] = (_sigmoid(ga) * ya + _sigmoid(gb) * yb).astype(bf16)

    merged, = _rowwise("merge", e5, s, te, [(y_a, dh2, 0), (y_b, dh2, 0), (proj, dh2, ga0), (proj, dh2, gb0)], [], [(d, bf16)], ncol=2)
    mo = _matmul("mix_out", merged, w_o, (s, d), f32, (2, mt, 1), pl.BlockSpec((tm, d), lambda j, i, k: (i, 0)),
                 pl.BlockSpec((d, dh2), lambda j, i, k: (0, j)), pl.BlockSpec((tm, dh2), lambda j, i, k: (i, j)), NN)

    def e6(rows, fulls, outs, accs):
        xv, mov = rows[0][...], rows[1][...]
        gt, g, sc, sh = [r[...] for r in fulls]
        x1 = xv + gt * mov
        outs[0][...] = x1
        _, xh = _rms(x1)
        outs[1][...] = (xh * g * (1.0 + sc) + sh).astype(bf16)

    x1, h2 = _rowwise("prenorm_ffn", e6, s, te, [(x, d, 0), (mo, d, 0)], [gt_m, g_ffn, sc_f, sh_f], [(d, f32), (d, bf16)])
    w_gt, w_ut, w_d = get_w("ffn", h2)
    dff_ = w_d.shape[0]
    ffb = dff_ // 4

    def ffn_in(name, w):
        return _matmul(name, h2, w, (s, dff_), f32, (4, mt, 1), pl.BlockSpec((tm, d), lambda j, i, k: (i, 0)),
                       pl.BlockSpec((ffb, d), lambda j, i, k: (j, 0)), pl.BlockSpec((tm, ffb), lambda j, i, k: (i, j)), NT)

    gg = ffn_in("ffn_gate", w_gt)
    uu = ffn_in("ffn_up", w_ut)

    def e8(rows, fulls, outs, accs):
        gv, uv = rows[0][...], rows[1][...]
        outs[0][...] = (gv * _sigmoid(gv) * uv).astype(bf16)

    act, = _rowwise("swiglu", e8, s, te, [(gg, ffb, 0), (uu, ffb, 0)], [], [(dff_, bf16)], ncol=4)
    ff = _matmul("ffn_down", act, w_d, (s, d), f32, (2, mt, 4), pl.BlockSpec((tm, ffb), lambda j, i, k: (i, k)),
                 pl.BlockSpec((ffb, dh2), lambda j, i, k: (k, j)), pl.BlockSpec((tm, dh2), lambda j, i, k: (i, j)), NN)

    def e9(rows, fulls, outs, accs):
        x1v, ffv, tv = [r[...] for r in rows]
        gt, gf = fulls[0][...], fulls[1][...]
        x2 = x1v + gt * ffv
        r, xh = _rms(x2)
        err = xh * gf - tv
        accs[0][...] += 0.5 * jnp.sum(jnp.mean(err * err, axis=-1, keepdims=True), axis=0, keepdims=True)
        dy = err / d
        accs[1][...] += _colsum(dy * xh)
        dxh = dy * gf
        dx2 = r * (dxh - xh * jnp.mean(dxh * xh, axis=-1, keepdims=True))
        outs[0][...] = dx2
        outs[1][...] = (dx2 * gt).astype(bf16)
        accs[2][...] += _colsum(dx2 * ffv)

    dx2, dff, loss_acc, dg_fin, dgt_f = _rowwise("loss_head", e9, s, te, [(x1, d, 0), (ff, d, 0), (tgt, d, 0)], [gt_f, g_fin],
                                                 [(d, f32), (d, bf16)], [(1, 128), (1, d), (1, d)])
    dact = _matmul("d_ffn_down_in", dff, w_d, (s, dff_), f32, (4, mt, 1), pl.BlockSpec((tm, d), lambda j, i, k: (i, 0)),
                   pl.BlockSpec((ffb, d), lambda j, i, k: (j, 0)), pl.BlockSpec((tm, ffb), lambda j, i, k: (i, j)), NT)

    def wgrad_rows(name, a, b, n_out, kb):
        return _matmul(name, a, b, (n_out, d), bf16, (n_out // kb, 2, mt), pl.BlockSpec((tm, kb), lambda i, j, k: (k, i)),
                       pl.BlockSpec((tm, dh2), lambda i, j, k: (k, j)), pl.BlockSpec((kb, dh2), lambda i, j, k: (i, j)), TN)

    gw_d = wgrad_rows("gw_ffn_down", act, dff, dff_, ffb)

    def b3(rows, fulls, outs, accs):
        da, gv, uv = [r[...] for r in rows]
        sg = _sigmoid(gv)
        outs[0][...] = (da * uv * _dsilu(gv, sg)).astype(bf16)
        outs[1][...] = (da * gv * sg).astype(bf16)

    dgg, duu = _rowwise("d_swiglu", b3, s, te, [(dact, ffb, 0), (gg, ffb, 0), (uu, ffb, 0)], [], [(dff_, bf16), (dff_, bf16)], ncol=4)

    def ffn_in_bwd(name, a, w):
        return _matmul(name, a, w, (s, d), f32, (2, mt, 4), pl.BlockSpec((tm, ffb), lambda j, i, k: (i, k)),
                       pl.BlockSpec((ffb, dh2), lambda j, i, k: (k, j)), pl.BlockSpec((tm, dh2), lambda j, i, k: (i, j)), NN)

    dh2a = ffn_in_bwd("d_ffn_gate_in", dgg, w_gt)
    dh2b = ffn_in_bwd("d_ffn_up_in", duu, w_ut)
    gw_gt = wgrad_rows("gw_ffn_gate", dgg, h2, dff_, ffb)
    gw_ut = wgrad_rows("gw_ffn_up", duu, h2, dff_, ffb)
    dh2a = _pin(dh2a, put_g("ffn", [gw_gt, gw_ut, gw_d]))

    def b5(rows, fulls, outs, accs):
        da, db, x1v, dx2v, mov = [r[...] for r in rows]
        sc, g, gt = [r[...] for r in fulls]
        dh = da + db
        r, xh = _rms(x1v)
        accs[0][...] += _colsum(dh)
        accs[1][...] += _colsum(dh * (xh * g))
        dn = dh * (1.0 + sc)
        accs[2][...] += _colsum(dn * xh)
        dxh = dn * g
        dx1 = dx2v + r * (dxh - xh * jnp.mean(dxh * xh, axis=-1, keepdims=True))
        outs[0][...] = dx1
        accs[3][...] += _colsum(dx1 * mov)
        outs[1][...] = (dx1 * gt).astype(bf16)

    dx1, dmo, dsh_f, dsc_f, dg_ffn, dgt_m = _rowwise(
        "d_prenorm_ffn", b5, s, te, [(dh2a, d, 0), (dh2b, d, 0), (x1, d, 0), (dx2, d, 0), (mo, d, 0)], [sc_f, g_ffn, gt_m],
        [(d, f32), (d, bf16)], [(1, d)] * 4)
    dmerged = _matmul("d_mix_out_in", dmo, w_o, (s, d), f32, (2, mt, 1), pl.BlockSpec((tm, d), lambda j, i, k: (i, 0)),
                      pl.BlockSpec((dh2, d), lambda j, i, k: (j, 0)), pl.BlockSpec((tm, dh2), lambda j, i, k: (i, j)), NT)
    gw_o = wgrad_rows("gw_mix_out", merged, dmo, d, dh2)

    def b7(rows, fulls, outs, accs):
        dm, ya, yb, ga, gb = [r[...] for r in rows]
        sa, sb_ = _sigmoid(ga), _sigmoid(gb)
        outs[0][...] = (dm * sa).astype(bf16)
        outs[1][...] = (dm * sb_).astype(bf16)
        outs[2][...] = (dm * ya * sa * (1.0 - sa)).astype(bf16)
        outs[3][...] = (dm * yb * sb_ * (1.0 - sb_)).astype(bf16)

    dya, dyb, dga, dgb = _rowwise("d_merge", b7, s, te, [(dmerged, dh2, 0), (y_a, dh2, 0), (y_b, dh2, 0), (proj, dh2, ga0), (proj, dh2, gb0)],
                                  [], [(d, bf16)] * 4, ncol=2)

    def out_proj_bwd(name, dy, w):
        return _matmul(name, dy, w, (s, dc), f32, (1, s // tm2, nsh), pl.BlockSpec((tm2, d // nsh), lambda j, i, k: (i, k)),
                       pl.BlockSpec((None, dc, d // nsh), lambda j, i, k: (k, 0, 0)), pl.BlockSpec((tm2, dc), lambda j, i, k: (i, 0)), NT)

    def out_proj_wgrad(name, a, dy):
        return _matmul(name, a, dy, (nsh, dc, d // nsh), bf16, (1, nsh, mt), pl.BlockSpec((tm, dc), lambda i, j, k: (k, 0)),
                       pl.BlockSpec((tm, d // nsh), lambda i, j, k: (k, j)), pl.BlockSpec((None, dc, d // nsh), lambda i, j, k: (j, 0, 0)), TN)

    dz_a = out_proj_bwd("d_conv_out_in", dya, w_co)
    dog = out_proj_bwd("d_hgrn_out_in", dyb, w_ho)
    gw_co = out_proj_wgrad("gw_conv_out", z_a, dya)
    gw_ho = out_proj_wgrad("gw_hgrn_out", og, dyb)
    dz_a = _pin(dz_a, put_g("mix", [gw_co, gw_ho, gw_o]))
    dab, dac, dax, dconv_w = _conv_bwd(proj, conv_w, dz_a, s, dc)
    dq, dfl, dvi, dgo, dlb, dgn = _hgrn_bwd(proj, lb_param, gnorm, o_saved, states, dog, s, dc, tb)
    dproj = jnp.concatenate([dab, dac, dax, dq, dfl, dvi, dgo, dga, dgb], axis=1)
    gw_in = _matmul("gw_proj", h, dproj, (nsh, d, win_sh), bf16, (2, nsh, mt), pl.BlockSpec((tm, dh2), lambda i, j, k: (k, i)),
                    pl.BlockSpec((tm, win_sh), lambda i, j, k: (k, j)), pl.BlockSpec((None, dh2, win_sh), lambda i, j, k: (j, i, 0)), TN)
    dproj = _pin(dproj, put_g("in", [gw_in]))
    dh = _matmul("d_proj_in", dproj, w_in, (s, d), f32, (1, mt, nsh), pl.BlockSpec((tm, win_sh), lambda j, i, k: (i, k)),
                 pl.BlockSpec((None, d, win_sh), lambda j, i, k: (k, 0, 0)), pl.BlockSpec((tm, d), lambda j, i, k: (i, 0)), NT)

    def b12(rows, fulls, outs, accs):
        dhv, xv, dx1v = [r[...] for r in rows]
        sc, g = fulls[0][...], fulls[1][...]
        r, xh = _rms(xv)
        accs[0][...] += _colsum(dhv)
        accs[1][...] += _colsum(dhv * (xh * g))
        dn = dhv * (1.0 + sc)
        accs[2][...] += _colsum(dn * xh)
        dxh = dn * g
        outs[0][...] = dx1v + r * (dxh - xh * jnp.mean(dxh * xh, axis=-1, keepdims=True))

    dx, dsh_m, dsc_m, dg_mix = _rowwise("d_prenorm_mix", b12, s, te, [(dh, d, 0), (x, d, 0), (dx1, d, 0)], [sc_m, g_mix],
                                        [(d, f32)], [(1, d)] * 3)
    dmod = [dsh_m, dsc_m, dgt_m, dsh_f, dsc_f, dgt_f]
    small = dict(loss=loss_acc, g_mix=dg_mix, g_ffn=dg_ffn, g_fin=dg_fin, lb=dlb, gnorm=dgn, conv_w=dconv_w)
    return dx, dmod, small


def _ada_fwd(c_all, w_sh, b_sh):
    def body(c_ref, w_ref, b_ref, o_ref):
        cv = c_ref[...]
        ca = (cv * _sigmoid(cv)).astype(bf16)
        o_ref[...] = jnp.dot(ca, w_ref[...].astype(bf16), preferred_element_type=f32) + b_ref[...]

    return pl.pallas_call(body, name="ada_fwd", out_shape=SDS((c_all.shape[0], w_sh.shape[1]), f32),
                          compiler_params=pltpu.CompilerParams(vmem_limit_bytes=V7X_VMEM_LIMIT))(c_all, w_sh, b_sh)


def _ada_wgrad(c_all, dmod_sh):
    def body(c_ref, d_ref, o_ref):
        cv = c_ref[...]
        ca = (cv * _sigmoid(cv)).astype(bf16)
        o_ref[...] = lax.dot_general(ca, d_ref[...].astype(bf16), (TN, ((), ())), preferred_element_type=f32)

    return pl.pallas_call(body, name="ada_wgrad", out_shape=SDS((c_all.shape[1], dmod_sh.shape[1]), f32),
                          compiler_params=pltpu.CompilerParams(vmem_limit_bytes=V7X_VMEM_LIMIT))(c_all, dmod_sh)


def _lb_grad(lb_param, dlb):
    def body(p_ref, d_ref, o_ref):
        p = p_ref[...]
        lb = _sigmoid(p[0:1, :] - p[1:2, :])
        gl = d_ref[...] * lb * (1.0 - lb)
        o_ref[0:1, :] = gl
        o_ref[1:2, :] = -gl

    return pl.pallas_call(body, name="lb_grad", out_shape=SDS(lb_param.shape, f32))(lb_param, dlb)


def _sum_small(gathered):
    def body(g_ref, o_ref):
        acc = g_ref[0]
        for dd in range(1, NDEV):
            acc = acc + g_ref[dd]
        o_ref[...] = acc

    return pl.pallas_call(body, name="sum_small", out_shape=SDS(gathered.shape[1:], f32))(gathered)


def kernel(x, c, w_ada, b_ada, norm_mix_g, w_in, conv_w, lb_param, gnorm_g, w_conv_out, w_hgrn_out, w_o, norm_ffn_g, w_ffn_gate, w_ffn_up, w_ffn_down, norm_final_g, loss_target, m_w_ada, m_b_ada, m_norm_mix_g, m_w_in, m_conv_w, m_lb_param, m_gnorm_g, m_w_conv_out, m_w_hgrn_out, m_w_o, m_norm_ffn_g, m_w_ffn_gate, m_w_ffn_up, m_w_ffn_down, m_norm_final_g, v_w_ada, v_b_ada, v_norm_mix_g, v_w_in, v_conv_w, v_lb_param, v_gnorm_g, v_w_conv_out, v_w_hgrn_out, v_w_o, v_norm_ffn_g, v_w_ffn_gate, v_w_ffn_up, v_w_ffn_down, v_norm_final_g):
    assert lb_param.shape[0] == 2 and w_ada.shape[0] == 1
    s, d = x.shape[1], x.shape[2]
    me = 4 * lax.axis_index("x") + 2 * lax.axis_index("y") + lax.axis_index("c")
    ada_cols = w_ada.shape[2]

    me1 = me.astype(jnp.int32).reshape(1)
    shard_groups = {"in": [w_in[0]], "mix": [w_conv_out[0], w_hgrn_out[0], w_o[0]],
                    "ffn": [w_ffn_gate[0].T, w_ffn_up[0].T, w_ffn_down[0]]}
    own_slot = lambda frm, to: _flat(frm)
    gather_plan = lambda n: [(a, j, own_slot, own_slot) for a in range(n) for j in (1,) + ICI_RELATIONS]
    flat = lambda a: a.reshape(a.shape[0] * a.shape[1], a.shape[2])
    to8 = lambda a: a.reshape(NDEV, a.shape[0] // NDEV, a.shape[1])
    gathering, tokens = {}, []
    for grp, sh in shard_groups.items():
        lands = [_place_shard(f"place_{grp}{i}", me1, a) for i, a in enumerate(sh)]
        ss, rs, _, lands, tok = _push_start("gather_start_" + grp, [], lands, gather_plan(len(sh)))
        gathering[grp] = (ss, rs, lands)
        tokens.append(tok)

    def get_w(grp, after):
        ss, rs, lands = gathering[grp]
        _, lands = _push_wait("gather_wait_" + grp, ss, rs, [], lands, gather_plan(len(lands)), after)
        full = _sibling_forward("gather_fwd_" + grp, lands)
        return [f if grp == "in" or i < 2 and grp == "mix" else flat(f) for i, f in enumerate(full)]

    core = lax.axis_index("c").astype(jnp.int32).reshape(1)
    chip = (2 * lax.axis_index("x") + lax.axis_index("y")).astype(jnp.int32).reshape(1)
    scatter_plan = lambda n: [(a, j, lambda frm, to: _chip(to), lambda frm, to: _chip(frm)) for a in range(n) for j in ICI_RELATIONS]
    scattering = {}

    def put_g(grp, grads):
        g8 = [g if g.ndim == 3 else to8(g) for g in grads]
        recv = _sibling_swap("scatter_pair_" + grp, g8)
        pairs = [_pair_sum(f"pair_sum_{grp}{i}", core, g, r, _row_tile(g.shape[1])) for i, (g, r) in enumerate(zip(g8, recv))]
        lands = [lax.empty(p.shape, p.dtype) for p in pairs]
        ss, rs, srcs, lands, tok = _push_start("scatter_start_" + grp, pairs, lands, scatter_plan(len(pairs)))
        scattering[grp] = (ss, rs, srcs, lands)
        return tok

    def reduced(grp, after):
        ss, rs, srcs, lands = scattering[grp]
        srcs, lands = _push_wait("scatter_wait_" + grp, ss, rs, srcs, lands, scatter_plan(len(srcs)), after)
        return [_chip_sum(f"chip_sum_{grp}{i}", chip, p, r, _row_tile(p.shape[1])) for i, (p, r) in enumerate(zip(srcs, lands))]

    c_all, cw_all = _all_gather("gather_cond", [_pin(c, tokens), conv_w[0]])
    c_all = c_all.reshape(NDEV, d)
    conv_w_full = jnp.transpose(cw_all, (1, 0, 2)).reshape(conv_w.shape[1], -1)
    b_sh = lax.dynamic_slice_in_dim(b_ada, me * ada_cols, ada_cols, axis=1)
    mod_cols = _ada_fwd(c_all, w_ada[0], b_sh)
    mod_all, = _all_gather("gather_mod", [mod_cols])
    mod = lax.dynamic_index_in_dim(mod_all, me, axis=1, keepdims=False).reshape(N_MOD, 1, d)

    dx, dmod, small = _local_step(x[0], loss_target[0], mod, norm_mix_g, norm_ffn_g, norm_final_g.reshape(1, d), lb_param,
                                  gnorm_g, conv_w_full, get_w, put_g)

    pieces = [*dmod, small["g_mix"], small["g_ffn"], small["g_fin"], small["lb"], small["gnorm"], small["loss"],
              small["conv_w"].reshape(1, -1)]
    widths = [p.shape[1] for p in pieces]
    offs = np.concatenate([[0], np.cumsum(widths)])
    packed = jnp.concatenate(pieces, axis=1)
    gathered, = _all_gather("gather_small", [packed])
    summed = _sum_small(gathered)
    part = lambda i: summed[:, offs[i]:offs[i + 1]]
    g_b_ada = summed[:, :N_MOD * d]
    g_norm_mix, g_norm_ffn, g_norm_fin, g_lb_row, g_gnorm, loss_vec, g_convw_flat = [part(i) for i in range(N_MOD, N_MOD + 7)]
    loss = loss_vec[0, 0]
    dmod_all = gathered[:, 0, :N_MOD * d]
    g_w_ada = _ada_wgrad(c_all, lax.dynamic_slice_in_dim(dmod_all, me * ada_cols, ada_cols, axis=1))
    g_lb = _lb_grad(lb_param, g_lb_row)
    cw_cols = conv_w.shape[2]
    g_conv_w = lax.dynamic_slice_in_dim(g_convw_flat.reshape(conv_w.shape[1], -1), me * cw_cols, cw_cols, axis=1)

    grads = dict(w_ada=g_w_ada, b_ada=g_b_ada, norm_mix_g=g_norm_mix, conv_w=g_conv_w, lb_param=g_lb, gnorm_g=g_gnorm,
                 norm_ffn_g=g_norm_ffn, norm_final_g=g_norm_fin)
    weights = dict(w_ada=(w_ada, m_w_ada, v_w_ada), b_ada=(b_ada, m_b_ada, v_b_ada), norm_mix_g=(norm_mix_g, m_norm_mix_g, v_norm_mix_g),
                   w_in=(w_in, m_w_in, v_w_in), conv_w=(conv_w, m_conv_w, v_conv_w), lb_param=(lb_param, m_lb_param, v_lb_param),
                   gnorm_g=(gnorm_g, m_gnorm_g, v_gnorm_g), w_conv_out=(w_conv_out, m_w_conv_out, v_w_conv_out),
                   w_hgrn_out=(w_hgrn_out, m_w_hgrn_out, v_w_hgrn_out), w_o=(w_o, m_w_o, v_w_o),
                   norm_ffn_g=(norm_ffn_g, m_norm_ffn_g, v_norm_ffn_g), w_ffn_gate=(w_ffn_gate, m_w_ffn_gate, v_w_ffn_gate),
                   w_ffn_up=(w_ffn_up, m_w_ffn_up, v_w_ffn_up), w_ffn_down=(w_ffn_down, m_w_ffn_down, v_w_ffn_down),
                   norm_final_g=(norm_final_g, m_norm_final_g, v_norm_final_g))
    res = {}

    def update(nm):
        w, m, v = weights[nm]
        shape2 = (w.shape[-2], w.shape[-1]) if w.ndim >= 2 else (1, w.shape[0])
        g2 = grads[nm].reshape(shape2)
        dl, mn, vn = _adamw("adamw_" + nm, g2, w.reshape(shape2), m.reshape(shape2), v.reshape(shape2))
        res[nm] = [a.reshape(w.shape) for a in (g2, dl, mn, vn)]

    for nm in list(grads):
        update(nm)
    g_w_gt, g_w_ut, grads["w_ffn_down"] = reduced("ffn", res["w_ada"][1])
    grads["w_ffn_gate"], grads["w_ffn_up"] = g_w_gt.T, g_w_ut.T
    for nm in ("w_ffn_gate", "w_ffn_up", "w_ffn_down"):
        update(nm)
    grads["w_conv_out"], grads["w_hgrn_out"], grads["w_o"] = reduced("mix", res["w_ffn_down"][1])
    for nm in ("w_conv_out", "w_hgrn_out", "w_o"):
        update(nm)
    grads["w_in"], = reduced("in", res["w_o"][1])
    update("w_in")
    outs = [[res[nm][i] for nm in weights] for i in range(4)]
    return (loss, dx.reshape(x.shape), *outs[0], *outs[1], *outs[2], *outs[3])
```

```python
import functools

import numpy as np
import jax
import jax.numpy as jnp
from jax import lax
from jax.experimental import pallas as pl
from jax.experimental.pallas import tpu as pltpu

f32, bf16 = jnp.float32, jnp.bfloat16
SDS = jax.ShapeDtypeStruct

EPS = 1e-6
HEADS, DK, CHUNK = 8, 128, 64
N_MOD = 6
NDEV = 8
ADAM_LR, ADAM_B1, ADAM_B2, ADAM_EPS, ADAM_WD, ADAM_STEP = 0.001, 0.9, 0.999, 1e-08, 0.01, 10
LEVELS = (32, 16, 8, 4, 2, 1)
V7X_VMEM_LIMIT = 56 * 1024 * 1024
HBM = pl.BlockSpec(memory_space=pltpu.HBM)
MESH = pl.DeviceIdType.MESH


def _params(sem):
    return pltpu.CompilerParams(dimension_semantics=sem, vmem_limit_bytes=V7X_VMEM_LIMIT)


def _sigmoid(x):
    return jax.nn.sigmoid(x)


def _dsilu(x, s):
    return s * (1.0 + x * (1.0 - s))


def _mesh_pos():
    x, y, c = lax.axis_index("x"), lax.axis_index("y"), lax.axis_index("c")
    return x, y, c


def _peer(pos, j):
    x, y, c = pos
    return (1 - x if j & 4 else x, 1 - y if j & 2 else y, 1 - c if j & 1 else c)


def _flat(pos):
    return 4 * pos[0] + 2 * pos[1] + pos[2]


def _all_gather(name, arrs):
    n = len(arrs)
    out_shapes = [SDS((NDEV,) + a.shape, a.dtype) for a in arrs]

    def body(*refs):
        ins, outs = refs[:n], refs[n:2 * n]
        send_sems, recv_sems, local_sems = refs[2 * n:]
        pos = _mesh_pos()
        me = _flat(pos)

        def copy(a, j, frm, to_pos):
            k = a * (NDEV - 1) + j - 1
            return pltpu.make_async_remote_copy(src_ref=ins[a], dst_ref=outs[a].at[frm], send_sem=send_sems.at[k],
                                                recv_sem=recv_sems.at[k], device_id=to_pos, device_id_type=MESH)

        local = [pltpu.make_async_copy(ins[a], outs[a].at[me], local_sems.at[a]) for a in range(n)]
        for cp in local:
            cp.start()
        sends = [copy(a, j, me, _peer(pos, j)) for j in range(1, NDEV) for a in range(n)]
        for cp in sends:
            cp.start()
        for j in range(1, NDEV):
            for a in range(n):
                copy(a, j, _flat(_peer(pos, j)), pos).wait_recv()
        for cp in sends:
            cp.wait_send()
        for cp in local:
            cp.wait()

    return pl.pallas_call(
        body, name=name, out_shape=out_shapes, in_specs=[HBM] * n, out_specs=[HBM] * n,
        scratch_shapes=[pltpu.SemaphoreType.DMA((n * (NDEV - 1),)), pltpu.SemaphoreType.DMA((n * (NDEV - 1),)),
                        pltpu.SemaphoreType.DMA((n,))],
    )(*arrs)


SEM = pl.BlockSpec(memory_space=pltpu.SEMAPHORE)
ANY = pl.BlockSpec(memory_space=pl.ANY)
EFFECT = pltpu.SideEffectType.DATAFLOW_SIDE_EFFECTING
ICI_RELATIONS = (2, 4, 6)


def _chip(pos):
    return 2 * pos[0] + pos[1]


def _hbm(a):
    return pltpu.with_memory_space_constraint(a, pltpu.HBM)


def _plan_copy(plan_entry, k, pos, frm, to, src_refs, land_refs, send_sems, recv_sems):
    a, _, src_slot, dst_slot = plan_entry
    s = src_refs[a] if src_slot is None else src_refs[a].at[src_slot(frm, to)]
    return pltpu.make_async_remote_copy(src_ref=s, dst_ref=land_refs[a].at[dst_slot(frm, to)], send_sem=send_sems.at[k],
                                        recv_sem=recv_sems.at[k], device_id=to, device_id_type=MESH)


def _push_start(name, srcs, lands, plan):
    ns, nb, nk = len(srcs), len(srcs) + len(lands), len(plan)

    def body(*refs):
        land_refs = refs[ns:nb]
        src_refs = refs[:ns] if ns else land_refs
        send_sems, recv_sems = refs[nb], refs[nb + 1]
        pos = _mesh_pos()
        for k, e in enumerate(plan):
            _plan_copy(e, k, pos, pos, _peer(pos, e[1]), src_refs, land_refs, send_sems, recv_sems).start()
        refs[-1][...] = jnp.zeros_like(refs[-1])

    outs = pl.pallas_call(
        body, name=name,
        out_shape=[pltpu.SemaphoreType.DMA((nk,)), pltpu.SemaphoreType.DMA((nk,))] + [pltpu.HBM(a.shape, a.dtype) for a in srcs + lands]
        + [SDS((8, 128), f32)],
        in_specs=[HBM] * nb, out_specs=[SEM, SEM] + [HBM] * nb + [pl.BlockSpec(memory_space=pltpu.VMEM)],
        input_output_aliases={i: 2 + i for i in range(nb)},
        compiler_params=pltpu.CompilerParams(has_side_effects=EFFECT),
    )(*[_hbm(a) for a in srcs + lands])
    return outs[0], outs[1], list(outs[2:2 + ns]), list(outs[2 + ns:2 + nb]), outs[-1]


def _push_wait(name, send_sems, recv_sems, srcs, lands, plan, after):
    ns, nb = len(srcs), len(srcs) + len(lands)

    def body(*refs):
        land_refs = refs[ns:nb]
        src_refs = refs[:ns] if ns else land_refs
        ssem, rsem = refs[nb], refs[nb + 1]
        pos = _mesh_pos()
        for k, e in enumerate(plan):
            peer = _peer(pos, e[1])
            _plan_copy(e, k, pos, pos, peer, src_refs, land_refs, ssem, rsem).wait_send()
            _plan_copy(e, k, pos, peer, pos, src_refs, land_refs, ssem, rsem).wait_recv()

    outs = pl.pallas_call(
        body, name=name, out_shape=[pltpu.HBM(a.shape, a.dtype) for a in srcs + lands],
        in_specs=[HBM] * nb + [SEM, SEM, ANY], out_specs=[HBM] * nb,
        input_output_aliases={i: i for i in range(nb)},
        compiler_params=pltpu.CompilerParams(has_side_effects=EFFECT),
    )(*srcs, *lands, send_sems, recv_sems, after)
    return list(outs[:ns]), list(outs[ns:])


def _after_tokens(small, tokens):
    for t in tokens:
        if t is not None:
            small = small + t[0:1, 0:1].reshape((1,) * small.ndim)
    return small


def _place_shard(name, me, shard):
    r, c = shard.shape
    tr = _row_tile(r)

    def body(me_ref, s_ref, o_ref):
        o_ref[...] = s_ref[...].astype(bf16)

    return pl.pallas_call(
        body, name=name, out_shape=SDS((NDEV, r, c), bf16),
        grid_spec=pltpu.PrefetchScalarGridSpec(
            num_scalar_prefetch=1, grid=(r // tr,), in_specs=[pl.BlockSpec((tr, c), lambda i, me_ref: (i, 0))],
            out_specs=pl.BlockSpec((None, tr, c), lambda i, me_ref: (me_ref[0], i, 0))),
        compiler_params=_params(("parallel",)),
    )(me, shard)


def _row_tile(r):
    return max(t for t in range(16, 257, 16) if r % t == 0)


def _sibling_forward(name, lands):
    n = len(lands)
    nk = n * len(ICI_RELATIONS)

    def body(*refs):
        land_refs = refs[:n]
        send_sems, recv_sems = refs[2 * n:]
        pos = _mesh_pos()
        sib = _peer(pos, 1)

        def copy(a, i, frm, to):
            k = a * len(ICI_RELATIONS) + i
            slot = _flat(_peer(frm, ICI_RELATIONS[i]))
            return pltpu.make_async_remote_copy(src_ref=land_refs[a].at[slot], dst_ref=land_refs[a].at[slot], send_sem=send_sems.at[k],
                                                recv_sem=recv_sems.at[k], device_id=to, device_id_type=MESH)

        sends = [copy(a, i, pos, sib) for a in range(n) for i in range(len(ICI_RELATIONS))]
        for cp in sends:
            cp.start()
        for a in range(n):
            for i in range(len(ICI_RELATIONS)):
                copy(a, i, sib, pos).wait_recv()
        for cp in sends:
            cp.wait_send()

    outs = pl.pallas_call(
        body, name=name, out_shape=[SDS(a.shape, a.dtype) for a in lands], in_specs=[HBM] * n, out_specs=[HBM] * n,
        input_output_aliases={i: i for i in range(n)},
        scratch_shapes=[pltpu.SemaphoreType.DMA((nk,)), pltpu.SemaphoreType.DMA((nk,))],
    )(*lands)
    return list(outs)


def _sibling_swap(name, grads):
    n = len(grads)
    nchip = NDEV // 2

    def body(*refs):
        g_refs, out_refs = refs[:n], refs[n:2 * n]
        send_sems, recv_sems = refs[2 * n:]
        pos = _mesh_pos()
        sib = _peer(pos, 1)
        sends = []
        for a in range(n):
            for q in range(nchip):
                k = a * nchip + q
                sends.append(pltpu.make_async_remote_copy(src_ref=g_refs[a].at[2 * q + sib[2]], dst_ref=out_refs[a].at[q],
                                                          send_sem=send_sems.at[k], recv_sem=recv_sems.at[k], device_id=sib,
                                                          device_id_type=MESH))
        for cp in sends:
            cp.start()
        for cp in sends:
            cp.wait()

    return pl.pallas_call(
        body, name=name, out_shape=[SDS((nchip,) + a.shape[1:], a.dtype) for a in grads], in_specs=[HBM] * n, out_specs=[HBM] * n,
        scratch_shapes=[pltpu.SemaphoreType.DMA((n * nchip,)), pltpu.SemaphoreType.DMA((n * nchip,))],
    )(*grads)


def _pair_sum(name, core, grad, recv, tr):
    _, r, c = grad.shape
    nchip = NDEV // 2

    def body(core_ref, g_ref, r_ref, o_ref):
        o_ref[...] = (g_ref[...].astype(f32) + r_ref[...].astype(f32)).astype(o_ref.dtype)

    return pl.pallas_call(
        body, name=name, out_shape=SDS((nchip, r, c), grad.dtype),
        grid_spec=pltpu.PrefetchScalarGridSpec(
            num_scalar_prefetch=1, grid=(nchip, r // tr),
            in_specs=[pl.BlockSpec((None, tr, c), lambda q, i, core_ref: (2 * q + core_ref[0], i, 0)),
                      pl.BlockSpec((None, tr, c), lambda q, i, core_ref: (q, i, 0))],
            out_specs=pl.BlockSpec((None, tr, c), lambda q, i, core_ref: (q, i, 0))),
        compiler_params=_params(("parallel", "parallel")),
    )(core, grad, recv)


def _chip_sum(name, chip, pairs, recv, tr):
    nchip, r, c = pairs.shape

    def body(chip_ref, p_ref, r_ref, o_ref):
        mine = chip_ref[0]
        acc = jnp.zeros(o_ref.shape, f32)
        for q in range(nchip):
            acc = acc + jnp.where(mine == q, p_ref[q].astype(f32), r_ref[q].astype(f32))
        o_ref[...] = acc

    return pl.pallas_call(
        body, name=name, out_shape=SDS((r, c), f32),
        grid_spec=pltpu.PrefetchScalarGridSpec(
            num_scalar_prefetch=1, grid=(r // tr,),
            in_specs=[pl.BlockSpec((nchip, tr, c), lambda i, chip_ref: (0, i, 0))] * 2,
            out_specs=pl.BlockSpec((tr, c), lambda i, chip_ref: (i, 0))),
        compiler_params=_params(("parallel",)),
    )(chip, pairs, recv)


def _matmul(name, a, b, out_shape, out_dtype, grid, a_spec, b_spec, o_spec, dims, after=None):
    ksteps = grid[2]
    acc_shape = tuple(d for d in o_spec.block_shape if d is not None)
    extra = [] if after is None else [after]

    def body(a_ref, b_ref, *rest):
        o_ref, acc = rest[len(extra)], rest[len(extra) + 1:]
        prod = lax.dot_general(a_ref[...], b_ref[...], (dims, ((), ())), preferred_element_type=f32)
        if ksteps == 1:
            o_ref[...] = prod.astype(o_ref.dtype)
        else:
            k = pl.program_id(2)

            @pl.when(k == 0)
            def _():
                acc[0][...] = prod

            @pl.when(k > 0)
            def _():
                acc[0][...] += prod

            @pl.when(k == ksteps - 1)
            def _():
                o_ref[...] = acc[0][...].astype(o_ref.dtype)

    return pl.pallas_call(
        body, name=name, grid=grid, in_specs=[a_spec, b_spec] + [ANY] * len(extra), out_specs=o_spec,
        out_shape=SDS(out_shape, out_dtype), scratch_shapes=[] if ksteps == 1 else [pltpu.VMEM(acc_shape, f32)],
        compiler_params=_params(("parallel", "parallel", "arbitrary")),
    )(a, b, *extra)


NN, NT, TN = ((1,), (0,)), ((1,), (1,)), ((0,), (0,))


def _rowwise(name, fn, n_rows, tm, rows, fulls, row_outs, acc_outs=(), ncol=1):
    assert ncol == 1 or not acc_outs
    nr, nf, no, na = len(rows), len(fulls), len(row_outs), len(acc_outs)
    in_specs = [pl.BlockSpec((tm, w), functools.partial(lambda i, j, cb: (i, cb + j), cb=cb)) for (_, w, cb) in rows]
    in_specs += [pl.BlockSpec(a.shape, functools.partial(lambda i, j, nd: (0,) * nd, nd=a.ndim)) for a in fulls]
    out_shape = [SDS((n_rows, w), dt) for (w, dt) in row_outs] + [SDS(s, f32) for s in acc_outs]
    out_specs = [pl.BlockSpec((tm, w // ncol), lambda i, j: (i, j)) for (w, _) in row_outs]
    out_specs += [pl.BlockSpec(s, functools.partial(lambda i, j, nd: (0,) * nd, nd=len(s))) for s in acc_outs]

    def body(*refs):
        if na:
            @pl.when(pl.program_id(0) == 0)
            def _():
                for r in refs[nr + nf + no:]:
                    r[...] = jnp.zeros(r.shape, r.dtype)
        fn(refs[:nr], refs[nr:nr + nf], refs[nr + nf:nr + nf + no], refs[nr + nf + no:])

    return pl.pallas_call(
        body, name=name, grid=(n_rows // tm, ncol), in_specs=in_specs, out_specs=out_specs, out_shape=out_shape,
        compiler_params=_params(("arbitrary" if na else "parallel", "arbitrary" if na else "parallel")),
    )(*[r[0] for r in rows], *fulls)


def _rms(x):
    r = lax.rsqrt(jnp.mean(x * x, axis=-1, keepdims=True) + EPS)
    return r, x * r


def _colsum(v):
    return jnp.sum(v, axis=0, keepdims=True)


def _shift_down(u, row, k):
    return jnp.where(row >= k, pltpu.roll(u, k, 0), 0.0)


def _shift_up(u, row, k):
    n = u.shape[0]
    return jnp.where(row < n - k, pltpu.roll(u, n - k, 0), 0.0)


def _conv_fwd(proj, conv_w, s, dc):
    nb = dc // 128

    def body(ab_ref, ac_ref, ax_ref, w_ref, z_ref):
        u = ac_ref[...] * ax_ref[...]
        row = lax.broadcasted_iota(jnp.int32, u.shape, 0)
        w = w_ref[...]
        cv = w[0:1] * _shift_down(u, row, 2) + w[1:2] * _shift_down(u, row, 1) + w[2:3] * u
        z_ref[...] = (ab_ref[...] * cv).astype(z_ref.dtype)

    col = lambda off: pl.BlockSpec((s, 128), functools.partial(lambda j, off: (0, off + j), off=off))
    return pl.pallas_call(
        body, name="conv_fwd", grid=(nb,), in_specs=[col(0), col(nb), col(2 * nb), pl.BlockSpec((3, 128), lambda j: (0, j))],
        out_specs=pl.BlockSpec((s, 128), lambda j: (0, j)), out_shape=SDS((s, dc), bf16), compiler_params=_params(("parallel",)),
    )(proj, proj, proj, conv_w)


def _conv_bwd(proj, conv_w, dz, s, dc):
    nb = dc // 128

    def body(ab_ref, ac_ref, ax_ref, w_ref, dz_ref, dab_ref, dac_ref, dax_ref, dw_ref):
        ab, ac, ax, dzv = ab_ref[...], ac_ref[...], ax_ref[...], dz_ref[...]
        u = ac * ax
        row = lax.broadcasted_iota(jnp.int32, u.shape, 0)
        w = w_ref[...]
        u1, u2 = _shift_down(u, row, 1), _shift_down(u, row, 2)
        cv = w[0:1] * u2 + w[1:2] * u1 + w[2:3] * u
        dcv = dzv * ab
        dab_ref[...] = (dzv * cv).astype(dab_ref.dtype)
        du = w[2:3] * dcv + w[1:2] * _shift_up(dcv, row, 1) + w[0:1] * _shift_up(dcv, row, 2)
        dac_ref[...] = (du * ax).astype(dac_ref.dtype)
        dax_ref[...] = (du * ac).astype(dax_ref.dtype)
        dw_ref[0:1, :] = _colsum(dcv * u2)
        dw_ref[1:2, :] = _colsum(dcv * u1)
        dw_ref[2:3, :] = _colsum(dcv * u)

    col = lambda off: pl.BlockSpec((s, 128), functools.partial(lambda j, off: (0, off + j), off=off))
    blk = pl.BlockSpec((s, 128), lambda j: (0, j))
    return pl.pallas_call(
        body, name="conv_bwd", grid=(nb,),
        in_specs=[col(0), col(nb), col(2 * nb), pl.BlockSpec((3, 128), lambda j: (0, j)), blk],
        out_specs=[blk, blk, blk, pl.BlockSpec((3, 128), lambda j: (0, j))],
        out_shape=[SDS((s, dc), bf16)] * 3 + [SDS((3, dc), f32)], compiler_params=_params(("parallel",)),
    )(proj, proj, proj, conv_w, dz)


def _level_masks():
    t = np.arange(CHUNK)[:, None]
    s = np.arange(CHUNK)[None, :]
    m = np.stack([((t & h) != 0) & ((s & h) == 0) & (t // (2 * h) == s // (2 * h)) for h in LEVELS]).astype(np.float32)
    return jnp.asarray(m), jnp.asarray(m.transpose(0, 2, 1))


def _cumsum_rows(x, row):
    for sh in (1, 2, 4, 8, 16, 32):
        x = x + jnp.where(row >= sh, pltpu.roll(x, sh, 0), 0.0)
    return x


def _rev_cumsum_rows(x, row):
    n = x.shape[0]
    for sh in (1, 2, 4, 8, 16, 32):
        x = x + jnp.where(row < n - sh, pltpu.roll(x, n - sh, 0), 0.0)
    return x


def _chunk_terms(qp, fl, lb):
    row = lax.broadcasted_iota(jnp.int32, qp.shape, 0)
    sig = _sigmoid(fl)
    f = lb + (1.0 - lb) * sig
    k = 1.0 - f
    sq = _sigmoid(qp)
    qh = qp * sq
    b = _cumsum_rows(jnp.log(f), row)
    sub = lax.broadcasted_iota(jnp.int32, (CHUNK // 8, 8, DK), 1)
    b8 = b.reshape(CHUNK // 8, 8, DK)
    us, exs, ups = [], [], []
    for m in LEVELS:
        sb = 2 * m
        if sb >= 8:
            b3 = b.reshape(CHUNK // sb, sb, DK)
            bref = jnp.broadcast_to(b3[:, m - 1:m, :], b3.shape).reshape(CHUNK, DK)
        else:
            bref8 = None
            for j in range(8 // sb):
                cand = jnp.broadcast_to(b8[:, j * sb + m - 1:j * sb + m, :], b8.shape)
                bref8 = cand if bref8 is None else jnp.where(sub >= j * sb, cand, bref8)
            bref = bref8.reshape(CHUNK, DK)
        up = (row & m) != 0
        ex = jnp.exp(jnp.where(up, b - bref, bref - b))
        us.append((jnp.where(up, qh, k) * ex).astype(bf16))
        exs.append(ex)
        ups.append(up)
    blast = b[CHUNK - 1:CHUNK, :]
    eb, ebl = jnp.exp(b), jnp.exp(blast - b)
    return dict(sig=sig, f=f, k=k, sq=sq, qh=qh, u=jnp.stack(us), ex=exs, up=ups, eb=eb, ebl=ebl, qt=qh * eb, kt=k * ebl,
                el=jnp.exp(blast), row=row)


def _scores(t, mask):
    pl_ = jnp.einsum("ltk,lsk->lts", t["u"], t["u"], preferred_element_type=f32)
    p = jnp.sum(pl_ * mask, axis=0)
    r = lax.broadcasted_iota(jnp.int32, (CHUNK, CHUNK), 0)
    c = lax.broadcasted_iota(jnp.int32, (CHUNK, CHUNK), 1)
    diag = jnp.sum(t["qh"] * t["k"], axis=-1, keepdims=True)
    return p + jnp.where(r == c, diag, 0.0)


def _hgrn_fwd(proj, lb_param, gnorm, s, dv_total, tb):
    nchunk = tb // CHUNK
    masks, _ = _level_masks()
    q0, f0, v0, g0 = 3 * HEADS, 4 * HEADS, 5 * HEADS, 6 * HEADS

    def body(q_ref, f_ref, v_ref, g_ref, lb_ref, gn_ref, mask_ref, og_ref, o_ref, st_ref, state):
        @pl.when(pl.program_id(1) == 0)
        def _():
            state[...] = jnp.zeros_like(state)

        lbp = lb_ref[...]
        lb = _sigmoid(lbp[0:1, :] - lbp[1:2, :])
        mask = mask_ref[...]
        for i in range(nchunk):
            rs = pl.ds(i * CHUNK, CHUNK)
            t = _chunk_terms(q_ref[rs, :], f_ref[rs, :], lb)
            v = v_ref[rs, :]
            vb = v.astype(bf16)
            st = state[...]
            st_ref[i] = st
            p = _scores(t, mask)
            o = jnp.dot(p.astype(bf16), vb, preferred_element_type=f32)
            o += lax.dot_general(t["qt"].astype(bf16), st.astype(bf16), (NT, ((), ())), preferred_element_type=f32)
            state[...] = st * t["el"] + lax.dot_general(vb, t["kt"].astype(bf16), (TN, ((), ())), preferred_element_type=f32)
            o_ref[rs, :] = o
            r, oh = _rms(o)
            g = g_ref[rs, :]
            og_ref[rs, :] = (oh * gn_ref[...] * (g * _sigmoid(g))).astype(og_ref.dtype)

    col = lambda off: pl.BlockSpec((tb, DK), functools.partial(lambda h, t, off: (t, off + h), off=off))
    blk = pl.BlockSpec((tb, DK), lambda h, t: (t, h))
    return pl.pallas_call(
        body, name="hgrn_fwd", grid=(HEADS, s // tb),
        in_specs=[col(q0), col(f0), col(v0), col(g0), pl.BlockSpec((2, DK), lambda h, t: (0, h)),
                  pl.BlockSpec((1, DK), lambda h, t: (0, 0)), pl.BlockSpec(masks.shape, lambda h, t: (0, 0, 0))],
        out_specs=[blk, blk, pl.BlockSpec((nchunk, None, DK, DK), lambda h, t: (t, h, 0, 0))],
        out_shape=[SDS((s, dv_total), bf16), SDS((s, dv_total), f32), SDS((s // CHUNK, HEADS, DK, DK), f32)],
        scratch_shapes=[pltpu.VMEM((DK, DK), f32)], compiler_params=_params(("parallel", "arbitrary")),
    )(proj, proj, proj, proj, lb_param, gnorm, masks)


def _hgrn_bwd(proj, lb_param, gnorm, o_saved, states, dog, s, dv_total, tb):
    nchunk = tb // CHUNK
    nt = s // tb
    nc_total = s // CHUNK
    masks, masks_t = _level_masks()
    q0, f0, v0, g0 = 3 * HEADS, 4 * HEADS, 5 * HEADS, 6 * HEADS

    def body(q_ref, f_ref, v_ref, g_ref, lb_ref, gn_ref, mask_ref, maskt_ref, o_ref, dog_ref, st_ref, stn_ref,
             dq_ref, df_ref, dv_ref, dg_ref, dlb_ref, dgn_ref, gstate):
        h_id, t_id = pl.program_id(0), pl.program_id(1)

        @pl.when(t_id == 0)
        def _():
            gstate[...] = jnp.zeros_like(gstate)
            dlb_ref[...] = jnp.zeros_like(dlb_ref)

        @pl.when((t_id == 0) & (h_id == 0))
        def _():
            dgn_ref[...] = jnp.zeros_like(dgn_ref)

        lbp = lb_ref[...]
        lb = _sigmoid(lbp[0:1, :] - lbp[1:2, :])
        mask, maskt = mask_ref[...], maskt_ref[...]
        gn = gn_ref[...]
        for i in reversed(range(nchunk)):
            rs = pl.ds(i * CHUNK, CHUNK)
            qp, fl, v, g = q_ref[rs, :], f_ref[rs, :], v_ref[rs, :], g_ref[rs, :]
            t = _chunk_terms(qp, fl, lb)
            vb = v.astype(bf16)
            st0 = st_ref[i]
            st1 = st_ref[i + 1] if i + 1 < nchunk else stn_ref[0]
            gt = gstate[...]
            o = o_ref[rs, :]
            r, oh = _rms(o)
            sg = _sigmoid(g)
            dog_v = dog_ref[rs, :]
            dg_ref[rs, :] = (dog_v * (oh * gn) * _dsilu(g, sg)).astype(dg_ref.dtype)
            don = dog_v * (g * sg)
            dgn_ref[...] += _colsum(don * oh)
            doh = don * gn
            do = r * (doh - oh * jnp.mean(doh * oh, axis=-1, keepdims=True))
            dob = do.astype(bf16)
            d = lax.dot_general(dob, vb, (NT, ((), ())), preferred_element_type=f32)
            dt = lax.dot_general(vb, dob, (NT, ((), ())), preferred_element_type=f32)
            z = (mask * d[None] + maskt * dt[None]).astype(bf16)
            rr = jnp.einsum("lts,lsk->ltk", z, t["u"], preferred_element_type=f32)
            dq = jnp.zeros((CHUNK, DK), f32)
            dk = jnp.zeros((CHUNK, DK), f32)
            qdk = jnp.zeros((CHUNK, DK), f32)
            for li in range(len(LEVELS)):
                du = t["ex"][li] * rr[li]
                dq += jnp.where(t["up"][li], du, 0.0)
                dk += jnp.where(t["up"][li], 0.0, du)
                e = t["u"][li].astype(f32) * rr[li]
                qdk += jnp.where(t["up"][li], e, -e)
            dd = jnp.sum(do * v, axis=-1, keepdims=True)
            dq += dd * t["k"]
            dk += dd * t["qh"]
            gtb = gt.astype(bf16)
            ktb, qtb = t["kt"].astype(bf16), t["qt"].astype(bf16)
            dq_in = jnp.dot(dob, st0.astype(bf16), preferred_element_type=f32)
            dk_in = jnp.dot(vb, gtb, preferred_element_type=f32)
            dq += t["eb"] * dq_in
            dk += t["ebl"] * dk_in
            qdk += qtb.astype(f32) * dq_in - ktb.astype(f32) * dk_in
            p = _scores(t, mask)
            dvv = lax.dot_general(p.astype(bf16), dob, (TN, ((), ())), preferred_element_type=f32)
            dvv += lax.dot_general(ktb, gtb, (NT, ((), ())), preferred_element_type=f32)
            dv_ref[rs, :] = dvv.astype(dv_ref.dtype)
            a_end = _colsum(gtb.astype(f32) * st1)
            dlf = _rev_cumsum_rows(qdk, t["row"]) + a_end
            dfv = dlf / t["f"] - dk
            df_ref[rs, :] = (dfv * (1.0 - lb) * t["sig"] * (1.0 - t["sig"])).astype(df_ref.dtype)
            dlb_ref[...] += _colsum(dfv * (1.0 - t["sig"]))
            dq_ref[rs, :] = (dq * _dsilu(qp, t["sq"])).astype(dq_ref.dtype)
            gstate[...] = gt * t["el"] + lax.dot_general(dob, qtb, (TN, ((), ())), preferred_element_type=f32)

    rev = lambda t: nt - 1 - t
    col = lambda off: pl.BlockSpec((tb, DK), functools.partial(lambda h, t, off: (rev(t), off + h), off=off))
    blk = pl.BlockSpec((tb, DK), lambda h, t: (rev(t), h))
    nxt = lambda h, t: (jnp.minimum((rev(t) + 1) * nchunk, nc_total - 1), h, 0, 0)
    return pl.pallas_call(
        body, name="hgrn_bwd", grid=(HEADS, nt),
        in_specs=[col(q0), col(f0), col(v0), col(g0), pl.BlockSpec((2, DK), lambda h, t: (0, h)),
                  pl.BlockSpec((1, DK), lambda h, t: (0, 0)), pl.BlockSpec(masks.shape, lambda h, t: (0, 0, 0)),
                  pl.BlockSpec(masks.shape, lambda h, t: (0, 0, 0)), blk, blk,
                  pl.BlockSpec((nchunk, None, DK, DK), lambda h, t: (rev(t), h, 0, 0)),
                  pl.BlockSpec((1, None, DK, DK), nxt)],
        out_specs=[blk, blk, blk, blk, pl.BlockSpec((1, DK), lambda h, t: (0, h)), pl.BlockSpec((1, DK), lambda h, t: (0, 0))],
        out_shape=[SDS((s, dv_total), bf16)] * 4 + [SDS((1, HEADS * DK), f32), SDS((1, DK), f32)],
        scratch_shapes=[pltpu.VMEM((DK, DK), f32)], compiler_params=_params(("arbitrary", "arbitrary")),
    )(proj, proj, proj, proj, lb_param, gnorm, masks, masks_t, o_saved, dog, states, states)


def _adamw(name, g, w, m, v):
    r, c = w.shape
    tr = r
    for cand in (256, 128, 64, 32, 16, 8):
        if r % cand == 0 and r > cand:
            tr = cand
            break

    def body(g_ref, w_ref, m_ref, v_ref, d_ref, mo_ref, vo_ref):
        gv = g_ref[...]
        mn = ADAM_B1 * m_ref[...] + (1.0 - ADAM_B1) * gv
        vn = ADAM_B2 * v_ref[...] + (1.0 - ADAM_B2) * jnp.square(gv)
        m_hat = mn / (1.0 - ADAM_B1 ** ADAM_STEP)
        v_hat = vn / (1.0 - ADAM_B2 ** ADAM_STEP)
        d_ref[...] = -ADAM_LR * (m_hat / (jnp.sqrt(v_hat) + ADAM_EPS) + ADAM_WD * w_ref[...])
        mo_ref[...] = mn
        vo_ref[...] = vn

    blk = pl.BlockSpec((tr, c), lambda i: (i, 0))
    return pl.pallas_call(
        body, name=name, grid=(r // tr,), in_specs=[blk] * 4, out_specs=[blk] * 3, out_shape=[SDS((r, c), f32)] * 3,
        compiler_params=_params(("parallel",)),
    )(g, w, m, v)


def _local_step(x, tgt, mod, g_mix, g_ffn, g_fin, lb_param, gnorm, conv_w, get_w, put_g):
    s, d = x.shape
    dc = d // 2
    tm = min(512, s)
    te = min(256, s)
    tb = min(256, s)
    mt = s // tm
    sh_m, sc_m, gt_m, sh_f, sc_f, gt_f = [mod[i] for i in range(N_MOD)]
    dh2 = d // 2

    def e1(rows, fulls, outs, accs):
        xv = rows[0][...]
        g, sc, sh = [r[...] for r in fulls]
        _, xh = _rms(xv)
        outs[0][...] = (xh * g * (1.0 + sc) + sh).astype(bf16)

    h, = _rowwise("prenorm_mix", e1, s, te, [(x, d, 0)], [g_mix, sc_m, sh_m], [(d, bf16)])
    w_in, = get_w("in", h)
    nsh, _, win_sh = w_in.shape
    proj = _matmul("proj", h, w_in, (s, nsh * win_sh), f32, (nsh, mt, 1), pl.BlockSpec((tm, d), lambda j, i, k: (i, 0)),
                   pl.BlockSpec((None, d, win_sh), lambda j, i, k: (j, 0, 0)), pl.BlockSpec((tm, win_sh), lambda j, i, k: (i, j)), NN)
    z_a = _conv_fwd(proj, conv_w, s, dc)
    og, o_saved, states = _hgrn_fwd(proj, lb_param, gnorm, s, dc, tb)
    tm2 = min(1024, s)
    w_co, w_ho, w_o = get_w("mix", og)

    def out_proj(name, a, w):
        return _matmul(name, a, w, (s, d), f32, (nsh, s // tm2, 1), pl.BlockSpec((tm2, dc), lambda j, i, k: (i, 0)),
                       pl.BlockSpec((None, dc, d // nsh), lambda j, i, k: (j, 0, 0)),
                       pl.BlockSpec((tm2, d // nsh), lambda j, i, k: (i, j)), NN)

    y_a = out_proj("conv_out", z_a, w_co)
    y_b = out_proj("hgrn_out", og, w_ho)
    ga0, gb0 = 7, 9

    def e5(rows, fulls, outs, accs):
        ya, yb, ga, gb = [r[...] for r in rows]
        outs[0][...] = (_sigmoid(ga) * ya + _sigmoid(gb) * yb).astype(bf16)

    merged, = _rowwise("merge", e5, s, te, [(y_a, dh2, 0), (y_b, dh2, 0), (proj, dh2, ga0), (proj, dh2, gb0)], [], [(d, bf16)], ncol=2)
    mo = _matmul("mix_out", merged, w_o, (s, d), f32, (2, mt, 1), pl.BlockSpec((tm, d), lambda j, i, k: (i, 0)),
                 pl.BlockSpec((d, dh2), lambda j, i, k: (0, j)), pl.BlockSpec((tm, dh2), lambda j, i, k: (i, j)), NN)

    def e6(rows, fulls, outs, accs):
        xv, mov = rows[0][...], rows[1][...]
        gt, g, sc, sh = [r[...] for r in fulls]
        x1 = xv + gt * mov
        outs[0][...] = x1
        _, xh = _rms(x1)
        outs[1][...] = (xh * g * (1.0 + sc) + sh).astype(bf16)

    x1, h2 = _rowwise("prenorm_ffn", e6, s, te, [(x, d, 0), (mo, d, 0)], [gt_m, g_ffn, sc_f, sh_f], [(d, f32), (d, bf16)])
    w_gt, w_ut, w_d = get_w("ffn", h2)
    dff_ = w_d.shape[0]
    ffb = dff_ // 4

    def ffn_in(name, w):
        return _matmul(name, h2, w, (s, dff_), f32, (4, mt, 1), pl.BlockSpec((tm, d), lambda j, i, k: (i, 0)),
                       pl.BlockSpec((ffb, d), lambda j, i, k: (j, 0)), pl.BlockSpec((tm, ffb), lambda j, i, k: (i, j)), NT)

    gg = ffn_in("ffn_gate", w_gt)
    uu = ffn_in("ffn_up", w_ut)

    def e8(rows, fulls, outs, accs):
        gv, uv = rows[0][...], rows[1][...]
        outs[0][...] = (gv * _sigmoid(gv) * uv).astype(bf16)

    act, = _rowwise("swiglu", e8, s, te, [(gg, ffb, 0), (uu, ffb, 0)], [], [(dff_, bf16)], ncol=4)
    ff = _matmul("ffn_down", act, w_d, (s, d), f32, (2, mt, 4), pl.BlockSpec((tm, ffb), lambda j, i, k: (i, k)),
                 pl.BlockSpec((ffb, dh2), lambda j, i, k: (k, j)), pl.BlockSpec((tm, dh2), lambda j, i, k: (i, j)), NN)

    def e9(rows, fulls, outs, accs):
        x1v, ffv, tv = [r[...] for r in rows]
        gt, gf = fulls[0][...], fulls[1][...]
        x2 = x1v + gt * ffv
        r, xh = _rms(x2)
        err = xh * gf - tv
        accs[0][...] += 0.5 * jnp.sum(jnp.mean(err * err, axis=-1, keepdims=True), axis=0, keepdims=True)
        dy = err / d
        accs[1][...] += _colsum(dy * xh)
        dxh = dy * gf
        dx2 = r * (dxh - xh * jnp.mean(dxh * xh, axis=-1, keepdims=True))
        outs[0][...] = dx2
        outs[1][...] = (dx2 * gt).astype(bf16)
        accs[2][...] += _colsum(dx2 * ffv)

    dx2, dff, loss_acc, dg_fin, dgt_f = _rowwise("loss_head", e9, s, te, [(x1, d, 0), (ff, d, 0), (tgt, d, 0)], [gt_f, g_fin],
                                                 [(d, f32), (d, bf16)], [(1, 128), (1, d), (1, d)])
    dact = _matmul("d_ffn_down_in", dff, w_d, (s, dff_), f32, (4, mt, 1), pl.BlockSpec((tm, d), lambda j, i, k: (i, 0)),
                   pl.BlockSpec((ffb, d), lambda j, i, k: (j, 0)), pl.BlockSpec((tm, ffb), lambda j, i, k: (i, j)), NT)

    def wgrad_rows(name, a, b, n_out, kb):
        return _matmul(name, a, b, (n_out, d), bf16, (n_out // kb, 2, mt), pl.BlockSpec((tm, kb), lambda i, j, k: (k, i)),
                       pl.BlockSpec((tm, dh2), lambda i, j, k: (k, j)), pl.BlockSpec((kb, dh2), lambda i, j, k: (i, j)), TN)

    gw_d = wgrad_rows("gw_ffn_down", act, dff, dff_, ffb)

    def b3(rows, fulls, outs, accs):
        da, gv, uv = [r[...] for r in rows]
        sg = _sigmoid(gv)
        outs[0][...] = (da * uv * _dsilu(gv, sg)).astype(bf16)
        outs[1][...] = (da * gv * sg).astype(bf16)

    dgg, duu = _rowwise("d_swiglu", b3, s, te, [(dact, ffb, 0), (gg, ffb, 0), (uu, ffb, 0)], [], [(dff_, bf16), (dff_, bf16)], ncol=4)

    def ffn_in_bwd(name, a, w):
        return _matmul(name, a, w, (s, d), f32, (2, mt, 4), pl.BlockSpec((tm, ffb), lambda j, i, k: (i, k)),
                       pl.BlockSpec((ffb, dh2), lambda j, i, k: (k, j)), pl.BlockSpec((tm, dh2), lambda j, i, k: (i, j)), NN)

    dh2a = ffn_in_bwd("d_ffn_gate_in", dgg, w_gt)
    dh2b = ffn_in_bwd("d_ffn_up_in", duu, w_ut)
    gw_gt = wgrad_rows("gw_ffn_gate", dgg, h2, dff_, ffb)
    gw_ut = wgrad_rows("gw_ffn_up", duu, h2, dff_, ffb)
    sc_f_late = _after_tokens(sc_f, [put_g("ffn", [gw_gt, gw_ut, gw_d])])

    def b5(rows, fulls, outs, accs):
        da, db, x1v, dx2v, mov = [r[...] for r in rows]
        sc, g, gt = [r[...] for r in fulls]
        dh = da + db
        r, xh = _rms(x1v)
        accs[0][...] += _colsum(dh)
        accs[1][...] += _colsum(dh * (xh * g))
        dn = dh * (1.0 + sc)
        accs[2][...] += _colsum(dn * xh)
        dxh = dn * g
        dx1 = dx2v + r * (dxh - xh * jnp.mean(dxh * xh, axis=-1, keepdims=True))
        outs[0][...] = dx1
        accs[3][...] += _colsum(dx1 * mov)
        outs[1][...] = (dx1 * gt).astype(bf16)

    dx1, dmo, dsh_f, dsc_f, dg_ffn, dgt_m = _rowwise(
        "d_prenorm_ffn", b5, s, te, [(dh2a, d, 0), (dh2b, d, 0), (x1, d, 0), (dx2, d, 0), (mo, d, 0)], [sc_f_late, g_ffn, gt_m],
        [(d, f32), (d, bf16)], [(1, d)] * 4)
    dmerged = _matmul("d_mix_out_in", dmo, w_o, (s, d), f32, (2, mt, 1), pl.BlockSpec((tm, d), lambda j, i, k: (i, 0)),
                      pl.BlockSpec((dh2, d), lambda j, i, k: (j, 0)), pl.BlockSpec((tm, dh2), lambda j, i, k: (i, j)), NT)
    gw_o = wgrad_rows("gw_mix_out", merged, dmo, d, dh2)

    def b7(rows, fulls, outs, accs):
        dm, ya, yb, ga, gb = [r[...] for r in rows]
        sa, sb_ = _sigmoid(ga), _sigmoid(gb)
        outs[0][...] = (dm * sa).astype(bf16)
        outs[1][...] = (dm * sb_).astype(bf16)
        outs[2][...] = (dm * ya * sa * (1.0 - sa)).astype(bf16)
        outs[3][...] = (dm * yb * sb_ * (1.0 - sb_)).astype(bf16)

    dya, dyb, dga, dgb = _rowwise("d_merge", b7, s, te, [(dmerged, dh2, 0), (y_a, dh2, 0), (y_b, dh2, 0), (proj, dh2, ga0), (proj, dh2, gb0)],
                                  [], [(d, bf16)] * 4, ncol=2)

    def out_proj_bwd(name, dy, w):
        return _matmul(name, dy, w, (s, dc), f32, (1, s // tm2, nsh), pl.BlockSpec((tm2, d // nsh), lambda j, i, k: (i, k)),
                       pl.BlockSpec((None, dc, d // nsh), lambda j, i, k: (k, 0, 0)), pl.BlockSpec((tm2, dc), lambda j, i, k: (i, 0)), NT)

    def out_proj_wgrad(name, a, dy):
        return _matmul(name, a, dy, (nsh, dc, d // nsh), bf16, (1, nsh, mt), pl.BlockSpec((tm, dc), lambda i, j, k: (k, 0)),
                       pl.BlockSpec((tm, d // nsh), lambda i, j, k: (k, j)), pl.BlockSpec((None, dc, d // nsh), lambda i, j, k: (j, 0, 0)), TN)

    dz_a = out_proj_bwd("d_conv_out_in", dya, w_co)
    dog = out_proj_bwd("d_hgrn_out_in", dyb, w_ho)
    gw_co = out_proj_wgrad("gw_conv_out", z_a, dya)
    gw_ho = out_proj_wgrad("gw_hgrn_out", og, dyb)
    conv_w_late = _after_tokens(conv_w, [put_g("mix", [gw_co, gw_ho, gw_o])])
    dab, dac, dax, dconv_w = _conv_bwd(proj, conv_w_late, dz_a, s, dc)
    dq, dfl, dvi, dgo, dlb, dgn = _hgrn_bwd(proj, lb_param, gnorm, o_saved, states, dog, s, dc, tb)
    dproj = jnp.concatenate([dab, dac, dax, dq, dfl, dvi, dgo, dga, dgb], axis=1)
    gw_in = _matmul("gw_proj", h, dproj, (nsh, d, win_sh), bf16, (2, nsh, mt), pl.BlockSpec((tm, dh2), lambda i, j, k: (k, i)),
                    pl.BlockSpec((tm, win_sh), lambda i, j, k: (k, j)), pl.BlockSpec((None, dh2, win_sh), lambda i, j, k: (j, i, 0)), TN)
    dh = _matmul("d_proj_in", dproj, w_in, (s, d), f32, (1, mt, nsh), pl.BlockSpec((tm, win_sh), lambda j, i, k: (i, k)),
                 pl.BlockSpec((None, d, win_sh), lambda j, i, k: (k, 0, 0)), pl.BlockSpec((tm, d), lambda j, i, k: (i, 0)), NT,
                 after=put_g("in", [gw_in]))

    def b12(rows, fulls, outs, accs):
        dhv, xv, dx1v = [r[...] for r in rows]
        sc, g = fulls[0][...], fulls[1][...]
        r, xh = _rms(xv)
        accs[0][...] += _colsum(dhv)
        accs[1][...] += _colsum(dhv * (xh * g))
        dn = dhv * (1.0 + sc)
        accs[2][...] += _colsum(dn * xh)
        dxh = dn * g
        outs[0][...] = dx1v + r * (dxh - xh * jnp.mean(dxh * xh, axis=-1, keepdims=True))

    dx, dsh_m, dsc_m, dg_mix = _rowwise("d_prenorm_mix", b12, s, te, [(dh, d, 0), (x, d, 0), (dx1, d, 0)], [sc_m, g_mix],
                                        [(d, f32)], [(1, d)] * 3)
    dmod = [dsh_m, dsc_m, dgt_m, dsh_f, dsc_f, dgt_f]
    small = dict(loss=loss_acc, g_mix=dg_mix, g_ffn=dg_ffn, g_fin=dg_fin, lb=dlb, gnorm=dgn, conv_w=dconv_w)
    return dx, dmod, small


def _ada_fwd(c_all, w_sh, b_sh):
    def body(c_ref, w_ref, b_ref, o_ref):
        cv = c_ref[...]
        ca = (cv * _sigmoid(cv)).astype(bf16)
        o_ref[...] = jnp.dot(ca, w_ref[...].astype(bf16), preferred_element_type=f32) + b_ref[...]

    return pl.pallas_call(body, name="ada_fwd", out_shape=SDS((c_all.shape[0], w_sh.shape[1]), f32),
                          compiler_params=pltpu.CompilerParams(vmem_limit_bytes=V7X_VMEM_LIMIT))(c_all, w_sh, b_sh)


def _ada_wgrad(c_all, dmod_sh):
    def body(c_ref, d_ref, o_ref):
        cv = c_ref[...]
        ca = (cv * _sigmoid(cv)).astype(bf16)
        o_ref[...] = lax.dot_general(ca, d_ref[...].astype(bf16), (TN, ((), ())), preferred_element_type=f32)

    return pl.pallas_call(body, name="ada_wgrad", out_shape=SDS((c_all.shape[1], dmod_sh.shape[1]), f32),
                          compiler_params=pltpu.CompilerParams(vmem_limit_bytes=V7X_VMEM_LIMIT))(c_all, dmod_sh)


def _lb_grad(lb_param, dlb):
    def body(p_ref, d_ref, o_ref):
        p = p_ref[...]
        lb = _sigmoid(p[0:1, :] - p[1:2, :])
        gl = d_ref[...] * lb * (1.0 - lb)
        o_ref[0:1, :] = gl
        o_ref[1:2, :] = -gl

    return pl.pallas_call(body, name="lb_grad", out_shape=SDS(lb_param.shape, f32))(lb_param, dlb)


def _sum_small(gathered):
    def body(g_ref, o_ref):
        acc = g_ref[0]
        for dd in range(1, NDEV):
            acc = acc + g_ref[dd]
        o_ref[...] = acc

    return pl.pallas_call(body, name="sum_small", out_shape=SDS(gathered.shape[1:], f32))(gathered)


def kernel(x, c, w_ada, b_ada, norm_mix_g, w_in, conv_w, lb_param, gnorm_g, w_conv_out, w_hgrn_out, w_o, norm_ffn_g, w_ffn_gate, w_ffn_up, w_ffn_down, norm_final_g, loss_target, m_w_ada, m_b_ada, m_norm_mix_g, m_w_in, m_conv_w, m_lb_param, m_gnorm_g, m_w_conv_out, m_w_hgrn_out, m_w_o, m_norm_ffn_g, m_w_ffn_gate, m_w_ffn_up, m_w_ffn_down, m_norm_final_g, v_w_ada, v_b_ada, v_norm_mix_g, v_w_in, v_conv_w, v_lb_param, v_gnorm_g, v_w_conv_out, v_w_hgrn_out, v_w_o, v_norm_ffn_g, v_w_ffn_gate, v_w_ffn_up, v_w_ffn_down, v_norm_final_g):
    assert lb_param.shape[0] == 2 and w_ada.shape[0] == 1
    s, d = x.shape[1], x.shape[2]
    me = 4 * lax.axis_index("x") + 2 * lax.axis_index("y") + lax.axis_index("c")
    ada_cols = w_ada.shape[2]

    me1 = me.astype(jnp.int32).reshape(1)
    shard_groups = {"in": [w_in[0]], "mix": [w_conv_out[0], w_hgrn_out[0], w_o[0]],
                    "ffn": [w_ffn_gate[0].T, w_ffn_up[0].T, w_ffn_down[0]]}
    own_slot = lambda frm, to: _flat(frm)
    gather_plan = lambda n: [(a, j, own_slot, own_slot) for a in range(n) for j in (1,) + ICI_RELATIONS]
    flat = lambda a: a.reshape(a.shape[0] * a.shape[1], a.shape[2])
    to8 = lambda a: a.reshape(NDEV, a.shape[0] // NDEV, a.shape[1])
    gathering, tokens = {}, []
    for grp, sh in shard_groups.items():
        lands = [_place_shard(f"place_{grp}{i}", me1, a) for i, a in enumerate(sh)]
        ss, rs, _, lands, tok = _push_start("gather_start_" + grp, [], lands, gather_plan(len(sh)))
        gathering[grp] = (ss, rs, lands)
        tokens.append(tok)

    def get_w(grp, after):
        ss, rs, lands = gathering[grp]
        _, lands = _push_wait("gather_wait_" + grp, ss, rs, [], lands, gather_plan(len(lands)), after)
        full = _sibling_forward("gather_fwd_" + grp, lands)
        return [f if grp == "in" or i < 2 and grp == "mix" else flat(f) for i, f in enumerate(full)]

    core = lax.axis_index("c").astype(jnp.int32).reshape(1)
    chip = (2 * lax.axis_index("x") + lax.axis_index("y")).astype(jnp.int32).reshape(1)
    scatter_plan = lambda n: [(a, j, lambda frm, to: _chip(to), lambda frm, to: _chip(frm)) for a in range(n) for j in ICI_RELATIONS]
    scattering = {}

    def put_g(grp, grads):
        g8 = [g if g.ndim == 3 else to8(g) for g in grads]
        recv = _sibling_swap("scatter_pair_" + grp, g8)
        pairs = [_pair_sum(f"pair_sum_{grp}{i}", core, g, r, _row_tile(g.shape[1])) for i, (g, r) in enumerate(zip(g8, recv))]
        lands = [lax.empty(p.shape, p.dtype) for p in pairs]
        ss, rs, srcs, lands, tok = _push_start("scatter_start_" + grp, pairs, lands, scatter_plan(len(pairs)))
        scattering[grp] = (ss, rs, srcs, lands)
        return tok

    def reduced(grp, after):
        ss, rs, srcs, lands = scattering[grp]
        srcs, lands = _push_wait("scatter_wait_" + grp, ss, rs, srcs, lands, scatter_plan(len(srcs)), after)
        return [_chip_sum(f"chip_sum_{grp}{i}", chip, p, r, _row_tile(p.shape[1])) for i, (p, r) in enumerate(zip(srcs, lands))]

    c_all, cw_all = _all_gather("gather_cond", [_after_tokens(c, tokens), conv_w[0]])
    c_all = c_all.reshape(NDEV, d)
    conv_w_full = jnp.transpose(cw_all, (1, 0, 2)).reshape(conv_w.shape[1], -1)
    b_sh = lax.dynamic_slice_in_dim(b_ada, me * ada_cols, ada_cols, axis=1)
    mod_cols = _ada_fwd(c_all, w_ada[0], b_sh)
    mod_all, = _all_gather("gather_mod", [mod_cols])
    mod = lax.dynamic_index_in_dim(mod_all, me, axis=1, keepdims=False).reshape(N_MOD, 1, d)

    dx, dmod, small = _local_step(x[0], loss_target[0], mod, norm_mix_g, norm_ffn_g, norm_final_g.reshape(1, d), lb_param,
                                  gnorm_g, conv_w_full, get_w, put_g)

    pieces = [*dmod, small["g_mix"], small["g_ffn"], small["g_fin"], small["lb"], small["gnorm"], small["loss"],
              small["conv_w"].reshape(1, -1)]
    widths = [p.shape[1] for p in pieces]
    offs = np.concatenate([[0], np.cumsum(widths)])
    packed = jnp.concatenate(pieces, axis=1)
    gathered, = _all_gather("gather_small", [packed])
    summed = _sum_small(gathered)
    part = lambda i: summed[:, offs[i]:offs[i + 1]]
    g_b_ada = summed[:, :N_MOD * d]
    g_norm_mix, g_norm_ffn, g_norm_fin, g_lb_row, g_gnorm, loss_vec, g_convw_flat = [part(i) for i in range(N_MOD, N_MOD + 7)]
    loss = loss_vec[0, 0]
    dmod_all = gathered[:, 0, :N_MOD * d]
    g_w_ada = _ada_wgrad(c_all, lax.dynamic_slice_in_dim(dmod_all, me * ada_cols, ada_cols, axis=1))
    g_lb = _lb_grad(lb_param, g_lb_row)
    cw_cols = conv_w.shape[2]
    g_conv_w = lax.dynamic_slice_in_dim(g_convw_flat.reshape(conv_w.shape[1], -1), me * cw_cols, cw_cols, axis=1)

    grads = dict(w_ada=g_w_ada, b_ada=g_b_ada, norm_mix_g=g_norm_mix, conv_w=g_conv_w, lb_param=g_lb, gnorm_g=g_gnorm,
                 norm_ffn_g=g_norm_ffn, norm_final_g=g_norm_fin)
    weights = dict(w_ada=(w_ada, m_w_ada, v_w_ada), b_ada=(b_ada, m_b_ada, v_b_ada), norm_mix_g=(norm_mix_g, m_norm_mix_g, v_norm_mix_g),
                   w_in=(w_in, m_w_in, v_w_in), conv_w=(conv_w, m_conv_w, v_conv_w), lb_param=(lb_param, m_lb_param, v_lb_param),
                   gnorm_g=(gnorm_g, m_gnorm_g, v_gnorm_g), w_conv_out=(w_conv_out, m_w_conv_out, v_w_conv_out),
                   w_hgrn_out=(w_hgrn_out, m_w_hgrn_out, v_w_hgrn_out), w_o=(w_o, m_w_o, v_w_o),
                   norm_ffn_g=(norm_ffn_g, m_norm_ffn_g, v_norm_ffn_g), w_ffn_gate=(w_ffn_gate, m_w_ffn_gate, v_w_ffn_gate),
                   w_ffn_up=(w_ffn_up, m_w_ffn_up, v_w_ffn_up), w_ffn_down=(w_ffn_down, m_w_ffn_down, v_w_ffn_down),
                   norm_final_g=(norm_final_g, m_norm_final_g, v_norm_final_g))
    res = {}

    def update(nm):
        w, m, v = weights[nm]
        shape2 = (w.shape[-2], w.shape[-1]) if w.ndim >= 2 else (1, w.shape[0])
        g2 = grads[nm].reshape(shape2)
        dl, mn, vn = _adamw("adamw_" + nm, g2, w.reshape(shape2), m.reshape(shape2), v.reshape(shape2))
        res[nm] = [a.reshape(w.shape) for a in (g2, dl, mn, vn)]

    for nm in list(grads):
        update(nm)
    g_w_gt, g_w_ut, grads["w_ffn_down"] = reduced("ffn", res["w_ada"][1])
    grads["w_ffn_gate"], grads["w_ffn_up"] = g_w_gt.T, g_w_ut.T
    for nm in ("w_ffn_gate", "w_ffn_up", "w_ffn_down"):
        update(nm)
    grads["w_conv_out"], grads["w_hgrn_out"], grads["w_o"] = reduced("mix", res["w_ffn_down"][1])
    for nm in ("w_conv_out", "w_hgrn_out", "w_o"):
        update(nm)
    grads["w_in"], = reduced("in", res["w_o"][1])
    update("w_in")
    outs = [[res[nm][i] for nm in weights] for i in range(4)]
    return (loss, dx.reshape(x.shape), *outs[0], *outs[1], *outs[2], *outs[3])
```

```python
import functools

import numpy as np
import jax
import jax.numpy as jnp
from jax import lax
from jax.experimental import pallas as pl
from jax.experimental.pallas import tpu as pltpu

f32, bf16 = jnp.float32, jnp.bfloat16
SDS = jax.ShapeDtypeStruct

EPS = 1e-6
HEADS, DK, CHUNK = 8, 128, 64
N_MOD = 6
NDEV = 8
ADAM_LR, ADAM_B1, ADAM_B2, ADAM_EPS, ADAM_WD, ADAM_STEP = 0.001, 0.9, 0.999, 1e-08, 0.01, 10
LEVELS = (32, 16, 8, 4, 2, 1)
V7X_VMEM_LIMIT = 56 * 1024 * 1024
HBM = pl.BlockSpec(memory_space=pltpu.HBM)
MESH = pl.DeviceIdType.MESH


def _params(sem):
    return pltpu.CompilerParams(dimension_semantics=sem, vmem_limit_bytes=V7X_VMEM_LIMIT)


def _sigmoid(x):
    return jax.nn.sigmoid(x)


def _dsilu(x, s):
    return s * (1.0 + x * (1.0 - s))


def _mesh_pos():
    x, y, c = lax.axis_index("x"), lax.axis_index("y"), lax.axis_index("c")
    return x, y, c


def _peer(pos, j):
    x, y, c = pos
    return (1 - x if j & 4 else x, 1 - y if j & 2 else y, 1 - c if j & 1 else c)


def _flat(pos):
    return 4 * pos[0] + 2 * pos[1] + pos[2]


def _all_gather(name, arrs):
    n = len(arrs)
    out_shapes = [SDS((NDEV,) + a.shape, a.dtype) for a in arrs]

    def body(*refs):
        ins, outs = refs[:n], refs[n:2 * n]
        send_sems, recv_sems, local_sems = refs[2 * n:]
        pos = _mesh_pos()
        me = _flat(pos)

        def copy(a, j, frm, to_pos):
            k = a * (NDEV - 1) + j - 1
            return pltpu.make_async_remote_copy(src_ref=ins[a], dst_ref=outs[a].at[frm], send_sem=send_sems.at[k],
                                                recv_sem=recv_sems.at[k], device_id=to_pos, device_id_type=MESH)

        local = [pltpu.make_async_copy(ins[a], outs[a].at[me], local_sems.at[a]) for a in range(n)]
        for cp in local:
            cp.start()
        sends = [copy(a, j, me, _peer(pos, j)) for j in range(1, NDEV) for a in range(n)]
        for cp in sends:
            cp.start()
        for j in range(1, NDEV):
            for a in range(n):
                copy(a, j, _flat(_peer(pos, j)), pos).wait_recv()
        for cp in sends:
            cp.wait_send()
        for cp in local:
            cp.wait()

    return pl.pallas_call(
        body, name=name, out_shape=out_shapes, in_specs=[HBM] * n, out_specs=[HBM] * n,
        scratch_shapes=[pltpu.SemaphoreType.DMA((n * (NDEV - 1),)), pltpu.SemaphoreType.DMA((n * (NDEV - 1),)),
                        pltpu.SemaphoreType.DMA((n,))],
    )(*arrs)


SEM = pl.BlockSpec(memory_space=pltpu.SEMAPHORE)
ANY = pl.BlockSpec(memory_space=pl.ANY)
EFFECT = pltpu.SideEffectType.DATAFLOW_SIDE_EFFECTING
ICI_RELATIONS = (2, 4, 6)


def _chip(pos):
    return 2 * pos[0] + pos[1]


def _hbm(a):
    return pltpu.with_memory_space_constraint(a, pltpu.HBM)


def _plan_copy(plan_entry, k, pos, frm, to, src_refs, land_refs, send_sems, recv_sems):
    a, _, src_slot, dst_slot = plan_entry
    s = src_refs[a] if src_slot is None else src_refs[a].at[src_slot(frm, to)]
    return pltpu.make_async_remote_copy(src_ref=s, dst_ref=land_refs[a].at[dst_slot(frm, to)], send_sem=send_sems.at[k],
                                        recv_sem=recv_sems.at[k], device_id=to, device_id_type=MESH)


def _push_start(name, srcs, lands, plan, after=None):
    ns, nb, nk = len(srcs), len(srcs) + len(lands), len(plan)
    extra = [] if after is None else [after]

    def body(*refs):
        land_refs = refs[ns:nb]
        src_refs = refs[:ns] if ns else land_refs
        send_sems, recv_sems = refs[nb + len(extra)], refs[nb + len(extra) + 1]
        pos = _mesh_pos()
        for k, e in enumerate(plan):
            _plan_copy(e, k, pos, pos, _peer(pos, e[1]), src_refs, land_refs, send_sems, recv_sems).start()
        refs[-1][...] = jnp.zeros_like(refs[-1])

    outs = pl.pallas_call(
        body, name=name,
        out_shape=[pltpu.SemaphoreType.DMA((nk,)), pltpu.SemaphoreType.DMA((nk,))] + [pltpu.HBM(a.shape, a.dtype) for a in srcs + lands]
        + [SDS((8, 128), f32)],
        in_specs=[HBM] * nb + [ANY] * len(extra), out_specs=[SEM, SEM] + [HBM] * nb + [pl.BlockSpec(memory_space=pltpu.VMEM)],
        input_output_aliases={i: 2 + i for i in range(nb)},
        compiler_params=pltpu.CompilerParams(has_side_effects=EFFECT),
    )(*[_hbm(a) for a in srcs + lands], *extra)
    return outs[0], outs[1], list(outs[2:2 + ns]), list(outs[2 + ns:2 + nb]), outs[-1]


def _push_wait(name, send_sems, recv_sems, srcs, lands, plan, after):
    ns, nb = len(srcs), len(srcs) + len(lands)

    def body(*refs):
        land_refs = refs[ns:nb]
        src_refs = refs[:ns] if ns else land_refs
        ssem, rsem = refs[nb], refs[nb + 1]
        pos = _mesh_pos()
        for k, e in enumerate(plan):
            peer = _peer(pos, e[1])
            _plan_copy(e, k, pos, pos, peer, src_refs, land_refs, ssem, rsem).wait_send()
            _plan_copy(e, k, pos, peer, pos, src_refs, land_refs, ssem, rsem).wait_recv()

    outs = pl.pallas_call(
        body, name=name, out_shape=[pltpu.HBM(a.shape, a.dtype) for a in srcs + lands],
        in_specs=[HBM] * nb + [SEM, SEM, ANY], out_specs=[HBM] * nb,
        input_output_aliases={i: i for i in range(nb)},
        compiler_params=pltpu.CompilerParams(has_side_effects=EFFECT),
    )(*srcs, *lands, send_sems, recv_sems, after)
    return list(outs[:ns]), list(outs[ns:])


def _after_tokens(small, tokens):
    for t in tokens:
        if t is not None:
            small = small + t[0:1, 0:1].reshape((1,) * small.ndim)
    return small


def _place_shard(name, me, shard):
    r, c = shard.shape
    tr = _row_tile(r)

    def body(me_ref, s_ref, o_ref):
        o_ref[...] = s_ref[...].astype(bf16)

    return pl.pallas_call(
        body, name=name, out_shape=SDS((NDEV, r, c), bf16),
        grid_spec=pltpu.PrefetchScalarGridSpec(
            num_scalar_prefetch=1, grid=(r // tr,), in_specs=[pl.BlockSpec((tr, c), lambda i, me_ref: (i, 0))],
            out_specs=pl.BlockSpec((None, tr, c), lambda i, me_ref: (me_ref[0], i, 0))),
        compiler_params=_params(("parallel",)),
    )(me, shard)


def _row_tile(r):
    return max(t for t in range(16, 257, 16) if r % t == 0)


def _sibling_forward(name, lands):
    n = len(lands)
    nk = n * len(ICI_RELATIONS)

    def body(*refs):
        land_refs = refs[:n]
        send_sems, recv_sems = refs[2 * n:]
        pos = _mesh_pos()
        sib = _peer(pos, 1)

        def copy(a, i, frm, to):
            k = a * len(ICI_RELATIONS) + i
            slot = _flat(_peer(frm, ICI_RELATIONS[i]))
            return pltpu.make_async_remote_copy(src_ref=land_refs[a].at[slot], dst_ref=land_refs[a].at[slot], send_sem=send_sems.at[k],
                                                recv_sem=recv_sems.at[k], device_id=to, device_id_type=MESH)

        sends = [copy(a, i, pos, sib) for a in range(n) for i in range(len(ICI_RELATIONS))]
        for cp in sends:
            cp.start()
        for a in range(n):
            for i in range(len(ICI_RELATIONS)):
                copy(a, i, sib, pos).wait_recv()
        for cp in sends:
            cp.wait_send()

    outs = pl.pallas_call(
        body, name=name, out_shape=[SDS(a.shape, a.dtype) for a in lands], in_specs=[HBM] * n, out_specs=[HBM] * n,
        input_output_aliases={i: i for i in range(n)},
        scratch_shapes=[pltpu.SemaphoreType.DMA((nk,)), pltpu.SemaphoreType.DMA((nk,))],
    )(*lands)
    return list(outs)


def _sibling_swap(name, grads):
    n = len(grads)
    nchip = NDEV // 2

    def body(*refs):
        g_refs, out_refs = refs[:n], refs[n:2 * n]
        send_sems, recv_sems = refs[2 * n:]
        pos = _mesh_pos()
        sib = _peer(pos, 1)
        sends = []
        for a in range(n):
            for q in range(nchip):
                k = a * nchip + q
                sends.append(pltpu.make_async_remote_copy(src_ref=g_refs[a].at[2 * q + sib[2]], dst_ref=out_refs[a].at[q],
                                                          send_sem=send_sems.at[k], recv_sem=recv_sems.at[k], device_id=sib,
                                                          device_id_type=MESH))
        for cp in sends:
            cp.start()
        for cp in sends:
            cp.wait()

    return pl.pallas_call(
        body, name=name, out_shape=[SDS((nchip,) + a.shape[1:], a.dtype) for a in grads], in_specs=[HBM] * n, out_specs=[HBM] * n,
        scratch_shapes=[pltpu.SemaphoreType.DMA((n * nchip,)), pltpu.SemaphoreType.DMA((n * nchip,))],
    )(*grads)


def _pair_sum(name, core, grad, recv, tr):
    _, r, c = grad.shape
    nchip = NDEV // 2

    def body(core_ref, g_ref, r_ref, o_ref):
        o_ref[...] = (g_ref[...].astype(f32) + r_ref[...].astype(f32)).astype(o_ref.dtype)

    return pl.pallas_call(
        body, name=name, out_shape=SDS((nchip, r, c), grad.dtype),
        grid_spec=pltpu.PrefetchScalarGridSpec(
            num_scalar_prefetch=1, grid=(nchip, r // tr),
            in_specs=[pl.BlockSpec((None, tr, c), lambda q, i, core_ref: (2 * q + core_ref[0], i, 0)),
                      pl.BlockSpec((None, tr, c), lambda q, i, core_ref: (q, i, 0))],
            out_specs=pl.BlockSpec((None, tr, c), lambda q, i, core_ref: (q, i, 0))),
        compiler_params=_params(("parallel", "parallel")),
    )(core, grad, recv)


def _chip_sum(name, chip, pairs, recv, tr):
    nchip, r, c = pairs.shape

    def body(chip_ref, p_ref, r_ref, o_ref):
        mine = chip_ref[0]
        acc = jnp.zeros(o_ref.shape, f32)
        for q in range(nchip):
            acc = acc + jnp.where(mine == q, p_ref[q].astype(f32), r_ref[q].astype(f32))
        o_ref[...] = acc

    return pl.pallas_call(
        body, name=name, out_shape=SDS((r, c), f32),
        grid_spec=pltpu.PrefetchScalarGridSpec(
            num_scalar_prefetch=1, grid=(r // tr,),
            in_specs=[pl.BlockSpec((nchip, tr, c), lambda i, chip_ref: (0, i, 0))] * 2,
            out_specs=pl.BlockSpec((tr, c), lambda i, chip_ref: (i, 0))),
        compiler_params=_params(("parallel",)),
    )(chip, pairs, recv)


def _matmul(name, a, b, out_shape, out_dtype, grid, a_spec, b_spec, o_spec, dims, after=None):
    ksteps = grid[2]
    acc_shape = tuple(d for d in o_spec.block_shape if d is not None)
    extra = [] if after is None else [after]

    def body(a_ref, b_ref, *rest):
        o_ref, acc = rest[len(extra)], rest[len(extra) + 1:]
        prod = lax.dot_general(a_ref[...], b_ref[...], (dims, ((), ())), preferred_element_type=f32)
        if ksteps == 1:
            o_ref[...] = prod.astype(o_ref.dtype)
        else:
            k = pl.program_id(2)

            @pl.when(k == 0)
            def _():
                acc[0][...] = prod

            @pl.when(k > 0)
            def _():
                acc[0][...] += prod

            @pl.when(k == ksteps - 1)
            def _():
                o_ref[...] = acc[0][...].astype(o_ref.dtype)

    return pl.pallas_call(
        body, name=name, grid=grid, in_specs=[a_spec, b_spec] + [ANY] * len(extra), out_specs=o_spec,
        out_shape=SDS(out_shape, out_dtype), scratch_shapes=[] if ksteps == 1 else [pltpu.VMEM(acc_shape, f32)],
        compiler_params=_params(("parallel", "parallel", "arbitrary")),
    )(a, b, *extra)


NN, NT, TN = ((1,), (0,)), ((1,), (1,)), ((0,), (0,))


def _rowwise(name, fn, n_rows, tm, rows, fulls, row_outs, acc_outs=(), ncol=1):
    assert ncol == 1 or not acc_outs
    nr, nf, no, na = len(rows), len(fulls), len(row_outs), len(acc_outs)
    in_specs = [pl.BlockSpec((tm, w), functools.partial(lambda i, j, cb: (i, cb + j), cb=cb)) for (_, w, cb) in rows]
    in_specs += [pl.BlockSpec(a.shape, functools.partial(lambda i, j, nd: (0,) * nd, nd=a.ndim)) for a in fulls]
    out_shape = [SDS((n_rows, w), dt) for (w, dt) in row_outs] + [SDS(s, f32) for s in acc_outs]
    out_specs = [pl.BlockSpec((tm, w // ncol), lambda i, j: (i, j)) for (w, _) in row_outs]
    out_specs += [pl.BlockSpec(s, functools.partial(lambda i, j, nd: (0,) * nd, nd=len(s))) for s in acc_outs]

    def body(*refs):
        if na:
            @pl.when(pl.program_id(0) == 0)
            def _():
                for r in refs[nr + nf + no:]:
                    r[...] = jnp.zeros(r.shape, r.dtype)
        fn(refs[:nr], refs[nr:nr + nf], refs[nr + nf:nr + nf + no], refs[nr + nf + no:])

    return pl.pallas_call(
        body, name=name, grid=(n_rows // tm, ncol), in_specs=in_specs, out_specs=out_specs, out_shape=out_shape,
        compiler_params=_params(("arbitrary" if na else "parallel", "arbitrary" if na else "parallel")),
    )(*[r[0] for r in rows], *fulls)


def _rms(x):
    r = lax.rsqrt(jnp.mean(x * x, axis=-1, keepdims=True) + EPS)
    return r, x * r


def _colsum(v):
    return jnp.sum(v, axis=0, keepdims=True)


def _shift_down(u, row, k):
    return jnp.where(row >= k, pltpu.roll(u, k, 0), 0.0)


def _shift_up(u, row, k):
    n = u.shape[0]
    return jnp.where(row < n - k, pltpu.roll(u, n - k, 0), 0.0)


def _conv_fwd(proj, conv_w, s, dc):
    nb = dc // 128

    def body(ab_ref, ac_ref, ax_ref, w_ref, z_ref):
        u = ac_ref[...] * ax_ref[...]
        row = lax.broadcasted_iota(jnp.int32, u.shape, 0)
        w = w_ref[...]
        cv = w[0:1] * _shift_down(u, row, 2) + w[1:2] * _shift_down(u, row, 1) + w[2:3] * u
        z_ref[...] = (ab_ref[...] * cv).astype(z_ref.dtype)

    col = lambda off: pl.BlockSpec((s, 128), functools.partial(lambda j, off: (0, off + j), off=off))
    return pl.pallas_call(
        body, name="conv_fwd", grid=(nb,), in_specs=[col(0), col(nb), col(2 * nb), pl.BlockSpec((3, 128), lambda j: (0, j))],
        out_specs=pl.BlockSpec((s, 128), lambda j: (0, j)), out_shape=SDS((s, dc), bf16), compiler_params=_params(("parallel",)),
    )(proj, proj, proj, conv_w)


def _conv_bwd(proj, conv_w, dz, s, dc):
    nb = dc // 128

    def body(ab_ref, ac_ref, ax_ref, w_ref, dz_ref, dab_ref, dac_ref, dax_ref, dw_ref):
        ab, ac, ax, dzv = ab_ref[...], ac_ref[...], ax_ref[...], dz_ref[...]
        u = ac * ax
        row = lax.broadcasted_iota(jnp.int32, u.shape, 0)
        w = w_ref[...]
        u1, u2 = _shift_down(u, row, 1), _shift_down(u, row, 2)
        cv = w[0:1] * u2 + w[1:2] * u1 + w[2:3] * u
        dcv = dzv * ab
        dab_ref[...] = (dzv * cv).astype(dab_ref.dtype)
        du = w[2:3] * dcv + w[1:2] * _shift_up(dcv, row, 1) + w[0:1] * _shift_up(dcv, row, 2)
        dac_ref[...] = (du * ax).astype(dac_ref.dtype)
        dax_ref[...] = (du * ac).astype(dax_ref.dtype)
        dw_ref[0:1, :] = _colsum(dcv * u2)
        dw_ref[1:2, :] = _colsum(dcv * u1)
        dw_ref[2:3, :] = _colsum(dcv * u)

    col = lambda off: pl.BlockSpec((s, 128), functools.partial(lambda j, off: (0, off + j), off=off))
    blk = pl.BlockSpec((s, 128), lambda j: (0, j))
    return pl.pallas_call(
        body, name="conv_bwd", grid=(nb,),
        in_specs=[col(0), col(nb), col(2 * nb), pl.BlockSpec((3, 128), lambda j: (0, j)), blk],
        out_specs=[blk, blk, blk, pl.BlockSpec((3, 128), lambda j: (0, j))],
        out_shape=[SDS((s, dc), bf16)] * 3 + [SDS((3, dc), f32)], compiler_params=_params(("parallel",)),
    )(proj, proj, proj, conv_w, dz)


def _level_masks():
    t = np.arange(CHUNK)[:, None]
    s = np.arange(CHUNK)[None, :]
    m = np.stack([((t & h) != 0) & ((s & h) == 0) & (t // (2 * h) == s // (2 * h)) for h in LEVELS]).astype(np.float32)
    return jnp.asarray(m), jnp.asarray(m.transpose(0, 2, 1))


def _cumsum_rows(x, row):
    for sh in (1, 2, 4, 8, 16, 32):
        x = x + jnp.where(row >= sh, pltpu.roll(x, sh, 0), 0.0)
    return x


def _rev_cumsum_rows(x, row):
    n = x.shape[0]
    for sh in (1, 2, 4, 8, 16, 32):
        x = x + jnp.where(row < n - sh, pltpu.roll(x, n - sh, 0), 0.0)
    return x


def _chunk_terms(qp, fl, lb):
    row = lax.broadcasted_iota(jnp.int32, qp.shape, 0)
    sig = _sigmoid(fl)
    f = lb + (1.0 - lb) * sig
    k = 1.0 - f
    sq = _sigmoid(qp)
    qh = qp * sq
    b = _cumsum_rows(jnp.log(f), row)
    sub = lax.broadcasted_iota(jnp.int32, (CHUNK // 8, 8, DK), 1)
    b8 = b.reshape(CHUNK // 8, 8, DK)
    us, exs, ups = [], [], []
    for m in LEVELS:
        sb = 2 * m
        if sb >= 8:
            b3 = b.reshape(CHUNK // sb, sb, DK)
            bref = jnp.broadcast_to(b3[:, m - 1:m, :], b3.shape).reshape(CHUNK, DK)
        else:
            bref8 = None
            for j in range(8 // sb):
                cand = jnp.broadcast_to(b8[:, j * sb + m - 1:j * sb + m, :], b8.shape)
                bref8 = cand if bref8 is None else jnp.where(sub >= j * sb, cand, bref8)
            bref = bref8.reshape(CHUNK, DK)
        up = (row & m) != 0
        ex = jnp.exp(jnp.where(up, b - bref, bref - b))
        us.append((jnp.where(up, qh, k) * ex).astype(bf16))
        exs.append(ex)
        ups.append(up)
    blast = b[CHUNK - 1:CHUNK, :]
    eb, ebl = jnp.exp(b), jnp.exp(blast - b)
    return dict(sig=sig, f=f, k=k, sq=sq, qh=qh, u=jnp.stack(us), ex=exs, up=ups, eb=eb, ebl=ebl, qt=qh * eb, kt=k * ebl,
                el=jnp.exp(blast), row=row)


def _scores(t, mask):
    pl_ = jnp.einsum("ltk,lsk->lts", t["u"], t["u"], preferred_element_type=f32)
    p = jnp.sum(pl_ * mask, axis=0)
    r = lax.broadcasted_iota(jnp.int32, (CHUNK, CHUNK), 0)
    c = lax.broadcasted_iota(jnp.int32, (CHUNK, CHUNK), 1)
    diag = jnp.sum(t["qh"] * t["k"], axis=-1, keepdims=True)
    return p + jnp.where(r == c, diag, 0.0)


def _hgrn_fwd(proj, lb_param, gnorm, s, dv_total, tb):
    nchunk = tb // CHUNK
    masks, _ = _level_masks()
    q0, f0, v0, g0 = 3 * HEADS, 4 * HEADS, 5 * HEADS, 6 * HEADS

    def body(q_ref, f_ref, v_ref, g_ref, lb_ref, gn_ref, mask_ref, og_ref, o_ref, st_ref, state):
        @pl.when(pl.program_id(1) == 0)
        def _():
            state[...] = jnp.zeros_like(state)

        lbp = lb_ref[...]
        lb = _sigmoid(lbp[0:1, :] - lbp[1:2, :])
        mask = mask_ref[...]
        for i in range(nchunk):
            rs = pl.ds(i * CHUNK, CHUNK)
            t = _chunk_terms(q_ref[rs, :], f_ref[rs, :], lb)
            v = v_ref[rs, :]
            vb = v.astype(bf16)
            st = state[...]
            st_ref[i] = st
            p = _scores(t, mask)
            o = jnp.dot(p.astype(bf16), vb, preferred_element_type=f32)
            o += lax.dot_general(t["qt"].astype(bf16), st.astype(bf16), (NT, ((), ())), preferred_element_type=f32)
            state[...] = st * t["el"] + lax.dot_general(vb, t["kt"].astype(bf16), (TN, ((), ())), preferred_element_type=f32)
            o_ref[rs, :] = o
            r, oh = _rms(o)
            g = g_ref[rs, :]
            og_ref[rs, :] = (oh * gn_ref[...] * (g * _sigmoid(g))).astype(og_ref.dtype)

    col = lambda off: pl.BlockSpec((tb, DK), functools.partial(lambda h, t, off: (t, off + h), off=off))
    blk = pl.BlockSpec((tb, DK), lambda h, t: (t, h))
    return pl.pallas_call(
        body, name="hgrn_fwd", grid=(HEADS, s // tb),
        in_specs=[col(q0), col(f0), col(v0), col(g0), pl.BlockSpec((2, DK), lambda h, t: (0, h)),
                  pl.BlockSpec((1, DK), lambda h, t: (0, 0)), pl.BlockSpec(masks.shape, lambda h, t: (0, 0, 0))],
        out_specs=[blk, blk, pl.BlockSpec((nchunk, None, DK, DK), lambda h, t: (t, h, 0, 0))],
        out_shape=[SDS((s, dv_total), bf16), SDS((s, dv_total), f32), SDS((s // CHUNK, HEADS, DK, DK), f32)],
        scratch_shapes=[pltpu.VMEM((DK, DK), f32)], compiler_params=_params(("parallel", "arbitrary")),
    )(proj, proj, proj, proj, lb_param, gnorm, masks)


def _hgrn_bwd(proj, lb_param, gnorm, o_saved, states, dog, s, dv_total, tb):
    nchunk = tb // CHUNK
    nt = s // tb
    nc_total = s // CHUNK
    masks, masks_t = _level_masks()
    q0, f0, v0, g0 = 3 * HEADS, 4 * HEADS, 5 * HEADS, 6 * HEADS

    def body(q_ref, f_ref, v_ref, g_ref, lb_ref, gn_ref, mask_ref, maskt_ref, o_ref, dog_ref, st_ref, stn_ref,
             dq_ref, df_ref, dv_ref, dg_ref, dlb_ref, dgn_ref, gstate):
        h_id, t_id = pl.program_id(0), pl.program_id(1)

        @pl.when(t_id == 0)
        def _():
            gstate[...] = jnp.zeros_like(gstate)
            dlb_ref[...] = jnp.zeros_like(dlb_ref)

        @pl.when((t_id == 0) & (h_id == 0))
        def _():
            dgn_ref[...] = jnp.zeros_like(dgn_ref)

        lbp = lb_ref[...]
        lb = _sigmoid(lbp[0:1, :] - lbp[1:2, :])
        mask, maskt = mask_ref[...], maskt_ref[...]
        gn = gn_ref[...]
        for i in reversed(range(nchunk)):
            rs = pl.ds(i * CHUNK, CHUNK)
            qp, fl, v, g = q_ref[rs, :], f_ref[rs, :], v_ref[rs, :], g_ref[rs, :]
            t = _chunk_terms(qp, fl, lb)
            vb = v.astype(bf16)
            st0 = st_ref[i]
            st1 = st_ref[i + 1] if i + 1 < nchunk else stn_ref[0]
            gt = gstate[...]
            o = o_ref[rs, :]
            r, oh = _rms(o)
            sg = _sigmoid(g)
            dog_v = dog_ref[rs, :]
            dg_ref[rs, :] = (dog_v * (oh * gn) * _dsilu(g, sg)).astype(dg_ref.dtype)
            don = dog_v * (g * sg)
            dgn_ref[...] += _colsum(don * oh)
            doh = don * gn
            do = r * (doh - oh * jnp.mean(doh * oh, axis=-1, keepdims=True))
            dob = do.astype(bf16)
            d = lax.dot_general(dob, vb, (NT, ((), ())), preferred_element_type=f32)
            dt = lax.dot_general(vb, dob, (NT, ((), ())), preferred_element_type=f32)
            z = (mask * d[None] + maskt * dt[None]).astype(bf16)
            rr = jnp.einsum("lts,lsk->ltk", z, t["u"], preferred_element_type=f32)
            dq = jnp.zeros((CHUNK, DK), f32)
            dk = jnp.zeros((CHUNK, DK), f32)
            qdk = jnp.zeros((CHUNK, DK), f32)
            for li in range(len(LEVELS)):
                du = t["ex"][li] * rr[li]
                dq += jnp.where(t["up"][li], du, 0.0)
                dk += jnp.where(t["up"][li], 0.0, du)
                e = t["u"][li].astype(f32) * rr[li]
                qdk += jnp.where(t["up"][li], e, -e)
            dd = jnp.sum(do * v, axis=-1, keepdims=True)
            dq += dd * t["k"]
            dk += dd * t["qh"]
            gtb = gt.astype(bf16)
            ktb, qtb = t["kt"].astype(bf16), t["qt"].astype(bf16)
            dq_in = jnp.dot(dob, st0.astype(bf16), preferred_element_type=f32)
            dk_in = jnp.dot(vb, gtb, preferred_element_type=f32)
            dq += t["eb"] * dq_in
            dk += t["ebl"] * dk_in
            qdk += qtb.astype(f32) * dq_in - ktb.astype(f32) * dk_in
            p = _scores(t, mask)
            dvv = lax.dot_general(p.astype(bf16), dob, (TN, ((), ())), preferred_element_type=f32)
            dvv += lax.dot_general(ktb, gtb, (NT, ((), ())), preferred_element_type=f32)
            dv_ref[rs, :] = dvv.astype(dv_ref.dtype)
            a_end = _colsum(gtb.astype(f32) * st1)
            dlf = _rev_cumsum_rows(qdk, t["row"]) + a_end
            dfv = dlf / t["f"] - dk
            df_ref[rs, :] = (dfv * (1.0 - lb) * t["sig"] * (1.0 - t["sig"])).astype(df_ref.dtype)
            dlb_ref[...] += _colsum(dfv * (1.0 - t["sig"]))
            dq_ref[rs, :] = (dq * _dsilu(qp, t["sq"])).astype(dq_ref.dtype)
            gstate[...] = gt * t["el"] + lax.dot_general(dob, qtb, (TN, ((), ())), preferred_element_type=f32)

    rev = lambda t: nt - 1 - t
    col = lambda off: pl.BlockSpec((tb, DK), functools.partial(lambda h, t, off: (rev(t), off + h), off=off))
    blk = pl.BlockSpec((tb, DK), lambda h, t: (rev(t), h))
    nxt = lambda h, t: (jnp.minimum((rev(t) + 1) * nchunk, nc_total - 1), h, 0, 0)
    return pl.pallas_call(
        body, name="hgrn_bwd", grid=(HEADS, nt),
        in_specs=[col(q0), col(f0), col(v0), col(g0), pl.BlockSpec((2, DK), lambda h, t: (0, h)),
                  pl.BlockSpec((1, DK), lambda h, t: (0, 0)), pl.BlockSpec(masks.shape, lambda h, t: (0, 0, 0)),
                  pl.BlockSpec(masks.shape, lambda h, t: (0, 0, 0)), blk, blk,
                  pl.BlockSpec((nchunk, None, DK, DK), lambda h, t: (rev(t), h, 0, 0)),
                  pl.BlockSpec((1, None, DK, DK), nxt)],
        out_specs=[blk, blk, blk, blk, pl.BlockSpec((1, DK), lambda h, t: (0, h)), pl.BlockSpec((1, DK), lambda h, t: (0, 0))],
        out_shape=[SDS((s, dv_total), bf16)] * 4 + [SDS((1, HEADS * DK), f32), SDS((1, DK), f32)],
        scratch_shapes=[pltpu.VMEM((DK, DK), f32)], compiler_params=_params(("arbitrary", "arbitrary")),
    )(proj, proj, proj, proj, lb_param, gnorm, masks, masks_t, o_saved, dog, states, states)


def _adamw(name, g, w, m, v):
    r, c = w.shape
    tr = r
    for cand in (256, 128, 64, 32, 16, 8):
        if r % cand == 0 and r > cand:
            tr = cand
            break

    def body(g_ref, w_ref, m_ref, v_ref, d_ref, mo_ref, vo_ref):
        gv = g_ref[...]
        mn = ADAM_B1 * m_ref[...] + (1.0 - ADAM_B1) * gv
        vn = ADAM_B2 * v_ref[...] + (1.0 - ADAM_B2) * jnp.square(gv)
        m_hat = mn / (1.0 - ADAM_B1 ** ADAM_STEP)
        v_hat = vn / (1.0 - ADAM_B2 ** ADAM_STEP)
        d_ref[...] = -ADAM_LR * (m_hat / (jnp.sqrt(v_hat) + ADAM_EPS) + ADAM_WD * w_ref[...])
        mo_ref[...] = mn
        vo_ref[...] = vn

    blk = pl.BlockSpec((tr, c), lambda i: (i, 0))
    return pl.pallas_call(
        body, name=name, grid=(r // tr,), in_specs=[blk] * 4, out_specs=[blk] * 3, out_shape=[SDS((r, c), f32)] * 3,
        compiler_params=_params(("parallel",)),
    )(g, w, m, v)


def _local_step(x, tgt, mod, g_mix, g_ffn, g_fin, lb_param, gnorm, conv_w, get_w, put_g):
    s, d = x.shape
    dc = d // 2
    tm = min(512, s)
    te = min(256, s)
    tb = min(256, s)
    mt = s // tm
    sh_m, sc_m, gt_m, sh_f, sc_f, gt_f = [mod[i] for i in range(N_MOD)]
    dh2 = d // 2

    def e1(rows, fulls, outs, accs):
        xv = rows[0][...]
        g, sc, sh = [r[...] for r in fulls]
        _, xh = _rms(xv)
        outs[0][...] = (xh * g * (1.0 + sc) + sh).astype(bf16)

    h, = _rowwise("prenorm_mix", e1, s, te, [(x, d, 0)], [g_mix, sc_m, sh_m], [(d, bf16)])
    w_in, = get_w("in", h)
    nsh, _, win_sh = w_in.shape
    proj = _matmul("proj", h, w_in, (s, nsh * win_sh), f32, (nsh, mt, 1), pl.BlockSpec((tm, d), lambda j, i, k: (i, 0)),
                   pl.BlockSpec((None, d, win_sh), lambda j, i, k: (j, 0, 0)), pl.BlockSpec((tm, win_sh), lambda j, i, k: (i, j)), NN)
    z_a = _conv_fwd(proj, conv_w, s, dc)
    og, o_saved, states = _hgrn_fwd(proj, lb_param, gnorm, s, dc, tb)
    tm2 = min(1024, s)
    w_co, w_ho, w_o = get_w("mix", og)

    def out_proj(name, a, w):
        return _matmul(name, a, w, (s, d), f32, (nsh, s // tm2, 1), pl.BlockSpec((tm2, dc), lambda j, i, k: (i, 0)),
                       pl.BlockSpec((None, dc, d // nsh), lambda j, i, k: (j, 0, 0)),
                       pl.BlockSpec((tm2, d // nsh), lambda j, i, k: (i, j)), NN)

    y_a = out_proj("conv_out", z_a, w_co)
    y_b = out_proj("hgrn_out", og, w_ho)
    ga0, gb0 = 7, 9

    def e5(rows, fulls, outs, accs):
        ya, yb, ga, gb = [r[...] for r in rows]
        outs[0][...] = (_sigmoid(ga) * ya + _sigmoid(gb) * yb).astype(bf16)

    merged, = _rowwise("merge", e5, s, te, [(y_a, dh2, 0), (y_b, dh2, 0), (proj, dh2, ga0), (proj, dh2, gb0)], [], [(d, bf16)], ncol=2)
    mo = _matmul("mix_out", merged, w_o, (s, d), f32, (2, mt, 1), pl.BlockSpec((tm, d), lambda j, i, k: (i, 0)),
                 pl.BlockSpec((d, dh2), lambda j, i, k: (0, j)), pl.BlockSpec((tm, dh2), lambda j, i, k: (i, j)), NN)

    def e6(rows, fulls, outs, accs):
        xv, mov = rows[0][...], rows[1][...]
        gt, g, sc, sh = [r[...] for r in fulls]
        x1 = xv + gt * mov
        outs[0][...] = x1
        _, xh = _rms(x1)
        outs[1][...] = (xh * g * (1.0 + sc) + sh).astype(bf16)

    x1, h2 = _rowwise("prenorm_ffn", e6, s, te, [(x, d, 0), (mo, d, 0)], [gt_m, g_ffn, sc_f, sh_f], [(d, f32), (d, bf16)])
    w_gt, w_ut, w_d = get_w("ffn", h2)
    dff_ = w_d.shape[0]
    ffb = dff_ // 4

    def ffn_in(name, w):
        return _matmul(name, h2, w, (s, dff_), f32, (4, mt, 1), pl.BlockSpec((tm, d), lambda j, i, k: (i, 0)),
                       pl.BlockSpec((ffb, d), lambda j, i, k: (j, 0)), pl.BlockSpec((tm, ffb), lambda j, i, k: (i, j)), NT)

    gg = ffn_in("ffn_gate", w_gt)
    uu = ffn_in("ffn_up", w_ut)

    def e8(rows, fulls, outs, accs):
        gv, uv = rows[0][...], rows[1][...]
        outs[0][...] = (gv * _sigmoid(gv) * uv).astype(bf16)

    act, = _rowwise("swiglu", e8, s, te, [(gg, ffb, 0), (uu, ffb, 0)], [], [(dff_, bf16)], ncol=4)
    ff = _matmul("ffn_down", act, w_d, (s, d), f32, (2, mt, 1), pl.BlockSpec((tm, dff_), lambda j, i, k: (i, 0)),
                 pl.BlockSpec((dff_, dh2), lambda j, i, k: (0, j)), pl.BlockSpec((tm, dh2), lambda j, i, k: (i, j)), NN)

    def e9(rows, fulls, outs, accs):
        x1v, ffv, tv = [r[...] for r in rows]
        gt, gf = fulls[0][...], fulls[1][...]
        x2 = x1v + gt * ffv
        r, xh = _rms(x2)
        err = xh * gf - tv
        accs[0][...] += 0.5 * jnp.sum(jnp.mean(err * err, axis=-1, keepdims=True), axis=0, keepdims=True)
        dy = err / d
        accs[1][...] += _colsum(dy * xh)
        dxh = dy * gf
        dx2 = r * (dxh - xh * jnp.mean(dxh * xh, axis=-1, keepdims=True))
        outs[0][...] = dx2
        outs[1][...] = (dx2 * gt).astype(bf16)
        accs[2][...] += _colsum(dx2 * ffv)

    dx2, dff, loss_acc, dg_fin, dgt_f = _rowwise("loss_head", e9, s, te, [(x1, d, 0), (ff, d, 0), (tgt, d, 0)], [gt_f, g_fin],
                                                 [(d, f32), (d, bf16)], [(1, 128), (1, d), (1, d)])
    dact = _matmul("d_ffn_down_in", dff, w_d, (s, dff_), f32, (4, mt, 1), pl.BlockSpec((tm, d), lambda j, i, k: (i, 0)),
                   pl.BlockSpec((ffb, d), lambda j, i, k: (j, 0)), pl.BlockSpec((tm, ffb), lambda j, i, k: (i, j)), NT)

    def wgrad_rows(name, a, b, n_out):
        kb = 512
        return _matmul(name, a, b, (n_out, d), bf16, (n_out // kb, 2, 1), pl.BlockSpec((s, kb), lambda i, j, k: (0, i)),
                       pl.BlockSpec((s, dh2), lambda i, j, k: (0, j)), pl.BlockSpec((kb, dh2), lambda i, j, k: (i, j)), TN)

    gw_d = wgrad_rows("gw_ffn_down", act, dff, dff_)

    def b3(rows, fulls, outs, accs):
        da, gv, uv = [r[...] for r in rows]
        sg = _sigmoid(gv)
        outs[0][...] = (da * uv * _dsilu(gv, sg)).astype(bf16)
        outs[1][...] = (da * gv * sg).astype(bf16)

    dgg, duu = _rowwise("d_swiglu", b3, s, te, [(dact, ffb, 0), (gg, ffb, 0), (uu, ffb, 0)], [], [(dff_, bf16), (dff_, bf16)], ncol=4)

    def ffn_in_bwd(name, a, w):
        return _matmul(name, a, w, (s, d), f32, (2, mt, 1), pl.BlockSpec((tm, dff_), lambda j, i, k: (i, 0)),
                       pl.BlockSpec((dff_, dh2), lambda j, i, k: (0, j)), pl.BlockSpec((tm, dh2), lambda j, i, k: (i, j)), NN)

    dh2a = ffn_in_bwd("d_ffn_gate_in", dgg, w_gt)
    dh2b = ffn_in_bwd("d_ffn_up_in", duu, w_ut)
    gw_gt = wgrad_rows("gw_ffn_gate", dgg, h2, dff_)
    gw_ut = wgrad_rows("gw_ffn_up", duu, h2, dff_)
    sc_f_late = _after_tokens(sc_f, [put_g("ffn", [gw_gt, gw_ut, gw_d])])

    def b5(rows, fulls, outs, accs):
        da, db, x1v, dx2v, mov = [r[...] for r in rows]
        sc, g, gt = [r[...] for r in fulls]
        dh = da + db
        r, xh = _rms(x1v)
        accs[0][...] += _colsum(dh)
        accs[1][...] += _colsum(dh * (xh * g))
        dn = dh * (1.0 + sc)
        accs[2][...] += _colsum(dn * xh)
        dxh = dn * g
        dx1 = dx2v + r * (dxh - xh * jnp.mean(dxh * xh, axis=-1, keepdims=True))
        outs[0][...] = dx1
        accs[3][...] += _colsum(dx1 * mov)
        outs[1][...] = (dx1 * gt).astype(bf16)

    dx1, dmo, dsh_f, dsc_f, dg_ffn, dgt_m = _rowwise(
        "d_prenorm_ffn", b5, s, te, [(dh2a, d, 0), (dh2b, d, 0), (x1, d, 0), (dx2, d, 0), (mo, d, 0)], [sc_f_late, g_ffn, gt_m],
        [(d, f32), (d, bf16)], [(1, d)] * 4)
    dmerged = _matmul("d_mix_out_in", dmo, w_o, (s, d), f32, (2, mt, 1), pl.BlockSpec((tm, d), lambda j, i, k: (i, 0)),
                      pl.BlockSpec((dh2, d), lambda j, i, k: (j, 0)), pl.BlockSpec((tm, dh2), lambda j, i, k: (i, j)), NT)
    gw_o = wgrad_rows("gw_mix_out", merged, dmo, d)

    def b7(rows, fulls, outs, accs):
        dm, ya, yb, ga, gb = [r[...] for r in rows]
        sa, sb_ = _sigmoid(ga), _sigmoid(gb)
        outs[0][...] = (dm * sa).astype(bf16)
        outs[1][...] = (dm * sb_).astype(bf16)
        outs[2][...] = (dm * ya * sa * (1.0 - sa)).astype(bf16)
        outs[3][...] = (dm * yb * sb_ * (1.0 - sb_)).astype(bf16)

    dya, dyb, dga, dgb = _rowwise("d_merge", b7, s, te, [(dmerged, dh2, 0), (y_a, dh2, 0), (y_b, dh2, 0), (proj, dh2, ga0), (proj, dh2, gb0)],
                                  [], [(d, bf16)] * 4, ncol=2)

    def out_proj_bwd(name, dy, w):
        return _matmul(name, dy, w, (s, dc), f32, (1, s // tm2, nsh), pl.BlockSpec((tm2, d // nsh), lambda j, i, k: (i, k)),
                       pl.BlockSpec((None, dc, d // nsh), lambda j, i, k: (k, 0, 0)), pl.BlockSpec((tm2, dc), lambda j, i, k: (i, 0)), NT)

    def out_proj_wgrad(name, a, dy):
        return _matmul(name, a, dy, (nsh, dc, d // nsh), bf16, (1, nsh, 1), pl.BlockSpec((s, dc), lambda i, j, k: (0, 0)),
                       pl.BlockSpec((s, d // nsh), lambda i, j, k: (0, j)), pl.BlockSpec((None, dc, d // nsh), lambda i, j, k: (j, 0, 0)), TN)

    dz_a = out_proj_bwd("d_conv_out_in", dya, w_co)
    dog = out_proj_bwd("d_hgrn_out_in", dyb, w_ho)
    gw_co = out_proj_wgrad("gw_conv_out", z_a, dya)
    gw_ho = out_proj_wgrad("gw_hgrn_out", og, dyb)
    conv_w_late = _after_tokens(conv_w, [put_g("mix", [gw_co, gw_ho, gw_o])])
    dab, dac, dax, dconv_w = _conv_bwd(proj, conv_w_late, dz_a, s, dc)
    dq, dfl, dvi, dgo, dlb, dgn = _hgrn_bwd(proj, lb_param, gnorm, o_saved, states, dog, s, dc, tb)
    dproj = jnp.concatenate([dab, dac, dax, dq, dfl, dvi, dgo, dga, dgb], axis=1)
    gw_in = _matmul("gw_proj", h, dproj, (nsh, d, win_sh), bf16, (nsh, d // 512, 1), pl.BlockSpec((s, 512), lambda j, i, k: (0, i)),
                    pl.BlockSpec((s, win_sh), lambda j, i, k: (0, j)), pl.BlockSpec((None, 512, win_sh), lambda j, i, k: (j, i, 0)), TN)
    dh = _matmul("d_proj_in", dproj, w_in, (s, d), f32, (1, mt, nsh), pl.BlockSpec((tm, win_sh), lambda j, i, k: (i, k)),
                 pl.BlockSpec((None, d, win_sh), lambda j, i, k: (k, 0, 0)), pl.BlockSpec((tm, d), lambda j, i, k: (i, 0)), NT,
                 after=put_g("in", [gw_in]))

    def b12(rows, fulls, outs, accs):
        dhv, xv, dx1v = [r[...] for r in rows]
        sc, g = fulls[0][...], fulls[1][...]
        r, xh = _rms(xv)
        accs[0][...] += _colsum(dhv)
        accs[1][...] += _colsum(dhv * (xh * g))
        dn = dhv * (1.0 + sc)
        accs[2][...] += _colsum(dn * xh)
        dxh = dn * g
        outs[0][...] = dx1v + r * (dxh - xh * jnp.mean(dxh * xh, axis=-1, keepdims=True))

    dx, dsh_m, dsc_m, dg_mix = _rowwise("d_prenorm_mix", b12, s, te, [(dh, d, 0), (x, d, 0), (dx1, d, 0)], [sc_m, g_mix],
                                        [(d, f32)], [(1, d)] * 3)
    dmod = [dsh_m, dsc_m, dgt_m, dsh_f, dsc_f, dgt_f]
    small = dict(loss=loss_acc, g_mix=dg_mix, g_ffn=dg_ffn, g_fin=dg_fin, lb=dlb, gnorm=dgn, conv_w=dconv_w)
    return dx, dmod, small


def _ada_fwd(c_all, w_sh, b_sh):
    def body(c_ref, w_ref, b_ref, o_ref):
        cv = c_ref[...]
        ca = (cv * _sigmoid(cv)).astype(bf16)
        o_ref[...] = jnp.dot(ca, w_ref[...].astype(bf16), preferred_element_type=f32) + b_ref[...]

    return pl.pallas_call(body, name="ada_fwd", out_shape=SDS((c_all.shape[0], w_sh.shape[1]), f32),
                          compiler_params=pltpu.CompilerParams(vmem_limit_bytes=V7X_VMEM_LIMIT))(c_all, w_sh, b_sh)


def _ada_wgrad(c_all, dmod_sh):
    def body(c_ref, d_ref, o_ref):
        cv = c_ref[...]
        ca = (cv * _sigmoid(cv)).astype(bf16)
        o_ref[...] = lax.dot_general(ca, d_ref[...].astype(bf16), (TN, ((), ())), preferred_element_type=f32)

    return pl.pallas_call(body, name="ada_wgrad", out_shape=SDS((c_all.shape[1], dmod_sh.shape[1]), f32),
                          compiler_params=pltpu.CompilerParams(vmem_limit_bytes=V7X_VMEM_LIMIT))(c_all, dmod_sh)


def _lb_grad(lb_param, dlb):
    def body(p_ref, d_ref, o_ref):
        p = p_ref[...]
        lb = _sigmoid(p[0:1, :] - p[1:2, :])
        gl = d_ref[...] * lb * (1.0 - lb)
        o_ref[0:1, :] = gl
        o_ref[1:2, :] = -gl

    return pl.pallas_call(body, name="lb_grad", out_shape=SDS(lb_param.shape, f32))(lb_param, dlb)


def _sum_small(gathered):
    def body(g_ref, o_ref):
        acc = g_ref[0]
        for dd in range(1, NDEV):
            acc = acc + g_ref[dd]
        o_ref[...] = acc

    return pl.pallas_call(body, name="sum_small", out_shape=SDS(gathered.shape[1:], f32))(gathered)


def kernel(x, c, w_ada, b_ada, norm_mix_g, w_in, conv_w, lb_param, gnorm_g, w_conv_out, w_hgrn_out, w_o, norm_ffn_g, w_ffn_gate, w_ffn_up, w_ffn_down, norm_final_g, loss_target, m_w_ada, m_b_ada, m_norm_mix_g, m_w_in, m_conv_w, m_lb_param, m_gnorm_g, m_w_conv_out, m_w_hgrn_out, m_w_o, m_norm_ffn_g, m_w_ffn_gate, m_w_ffn_up, m_w_ffn_down, m_norm_final_g, v_w_ada, v_b_ada, v_norm_mix_g, v_w_in, v_conv_w, v_lb_param, v_gnorm_g, v_w_conv_out, v_w_hgrn_out, v_w_o, v_norm_ffn_g, v_w_ffn_gate, v_w_ffn_up, v_w_ffn_down, v_norm_final_g):
    assert lb_param.shape[0] == 2 and w_ada.shape[0] == 1
    s, d = x.shape[1], x.shape[2]
    me = 4 * lax.axis_index("x") + 2 * lax.axis_index("y") + lax.axis_index("c")
    ada_cols = w_ada.shape[2]

    c_all, cw_all = _all_gather("gather_cond", [c, conv_w[0]])
    c_all = c_all.reshape(NDEV, d)
    conv_w_full = jnp.transpose(cw_all, (1, 0, 2)).reshape(conv_w.shape[1], -1)
    b_sh = lax.dynamic_slice_in_dim(b_ada, me * ada_cols, ada_cols, axis=1)
    mod_cols = _ada_fwd(c_all, w_ada[0], b_sh)
    mod_all, = _all_gather("gather_mod", [mod_cols])
    mod = lax.dynamic_index_in_dim(mod_all, me, axis=1, keepdims=False).reshape(N_MOD, 1, d)

    me1 = me.astype(jnp.int32).reshape(1)
    shard_groups = {"in": [w_in[0]], "mix": [w_conv_out[0], w_hgrn_out[0], w_o[0]],
                    "ffn": [w_ffn_gate[0].T, w_ffn_up[0].T, w_ffn_down[0]]}
    own_slot = lambda frm, to: _flat(frm)
    gather_plan = lambda n: [(a, j, own_slot, own_slot) for a in range(n) for j in (1,) + ICI_RELATIONS]
    flat = lambda a: a.reshape(a.shape[0] * a.shape[1], a.shape[2])
    to8 = lambda a: a.reshape(NDEV, a.shape[0] // NDEV, a.shape[1])
    gathering, tokens = {}, []
    for grp, sh in shard_groups.items():
        lands = [_place_shard(f"place_{grp}{i}", me1, a) for i, a in enumerate(sh)]
        ss, rs, _, lands, tok = _push_start("gather_start_" + grp, [], lands, gather_plan(len(sh)), after=mod_all)
        gathering[grp] = (ss, rs, lands)
        tokens.append(tok)

    def get_w(grp, after):
        ss, rs, lands = gathering[grp]
        _, lands = _push_wait("gather_wait_" + grp, ss, rs, [], lands, gather_plan(len(lands)), after)
        full = _sibling_forward("gather_fwd_" + grp, lands)
        return [f if grp == "in" or i < 2 and grp == "mix" else flat(f) for i, f in enumerate(full)]

    core = lax.axis_index("c").astype(jnp.int32).reshape(1)
    chip = (2 * lax.axis_index("x") + lax.axis_index("y")).astype(jnp.int32).reshape(1)
    scatter_plan = lambda n: [(a, j, lambda frm, to: _chip(to), lambda frm, to: _chip(frm)) for a in range(n) for j in ICI_RELATIONS]
    scattering = {}

    def put_g(grp, grads):
        g8 = [g if g.ndim == 3 else to8(g) for g in grads]
        recv = _sibling_swap("scatter_pair_" + grp, g8)
        pairs = [_pair_sum(f"pair_sum_{grp}{i}", core, g, r, _row_tile(g.shape[1])) for i, (g, r) in enumerate(zip(g8, recv))]
        lands = [lax.empty(p.shape, p.dtype) for p in pairs]
        ss, rs, srcs, lands, tok = _push_start("scatter_start_" + grp, pairs, lands, scatter_plan(len(pairs)))
        scattering[grp] = (ss, rs, srcs, lands)
        return tok

    def reduced(grp, after):
        ss, rs, srcs, lands = scattering[grp]
        srcs, lands = _push_wait("scatter_wait_" + grp, ss, rs, srcs, lands, scatter_plan(len(srcs)), after)
        return [_chip_sum(f"chip_sum_{grp}{i}", chip, p, r, _row_tile(p.shape[1])) for i, (p, r) in enumerate(zip(srcs, lands))]

    dx, dmod, small = _local_step(x[0], loss_target[0], mod, _after_tokens(norm_mix_g, tokens), norm_ffn_g,
                                  norm_final_g.reshape(1, d), lb_param, gnorm_g, conv_w_full, get_w, put_g)

    pieces = [*dmod, small["g_mix"], small["g_ffn"], small["g_fin"], small["lb"], small["gnorm"], small["loss"],
              small["conv_w"].reshape(1, -1)]
    widths = [p.shape[1] for p in pieces]
    offs = np.concatenate([[0], np.cumsum(widths)])
    packed = jnp.concatenate(pieces, axis=1)
    gathered, = _all_gather("gather_small", [packed])
    summed = _sum_small(gathered)
    part = lambda i: summed[:, offs[i]:offs[i + 1]]
    g_b_ada = summed[:, :N_MOD * d]
    g_norm_mix, g_norm_ffn, g_norm_fin, g_lb_row, g_gnorm, loss_vec, g_convw_flat = [part(i) for i in range(N_MOD, N_MOD + 7)]
    loss = loss_vec[0, 0]
    dmod_all = gathered[:, 0, :N_MOD * d]
    g_w_ada = _ada_wgrad(c_all, lax.dynamic_slice_in_dim(dmod_all, me * ada_cols, ada_cols, axis=1))
    g_lb = _lb_grad(lb_param, g_lb_row)
    cw_cols = conv_w.shape[2]
    g_conv_w = lax.dynamic_slice_in_dim(g_convw_flat.reshape(conv_w.shape[1], -1), me * cw_cols, cw_cols, axis=1)

    grads = dict(w_ada=g_w_ada, b_ada=g_b_ada, norm_mix_g=g_norm_mix, conv_w=g_conv_w, lb_param=g_lb, gnorm_g=g_gnorm,
                 norm_ffn_g=g_norm_ffn, norm_final_g=g_norm_fin)
    weights = dict(w_ada=(w_ada, m_w_ada, v_w_ada), b_ada=(b_ada, m_b_ada, v_b_ada), norm_mix_g=(norm_mix_g, m_norm_mix_g, v_norm_mix_g),
                   w_in=(w_in, m_w_in, v_w_in), conv_w=(conv_w, m_conv_w, v_conv_w), lb_param=(lb_param, m_lb_param, v_lb_param),
                   gnorm_g=(gnorm_g, m_gnorm_g, v_gnorm_g), w_conv_out=(w_conv_out, m_w_conv_out, v_w_conv_out),
                   w_hgrn_out=(w_hgrn_out, m_w_hgrn_out, v_w_hgrn_out), w_o=(w_o, m_w_o, v_w_o),
                   norm_ffn_g=(norm_ffn_g, m_norm_ffn_g, v_norm_ffn_g), w_ffn_gate=(w_ffn_gate, m_w_ffn_gate, v_w_ffn_gate),
                   w_ffn_up=(w_ffn_up, m_w_ffn_up, v_w_ffn_up), w_ffn_down=(w_ffn_down, m_w_ffn_down, v_w_ffn_down),
                   norm_final_g=(norm_final_g, m_norm_final_g, v_norm_final_g))
    res = {}

    def update(nm):
        w, m, v = weights[nm]
        shape2 = (w.shape[-2], w.shape[-1]) if w.ndim >= 2 else (1, w.shape[0])
        g2 = grads[nm].reshape(shape2)
        dl, mn, vn = _adamw("adamw_" + nm, g2, w.reshape(shape2), m.reshape(shape2), v.reshape(shape2))
        res[nm] = [a.reshape(w.shape) for a in (g2, dl, mn, vn)]

    for nm in list(grads):
        update(nm)
    g_w_gt, g_w_ut, grads["w_ffn_down"] = reduced("ffn", res["w_ada"][1])
    grads["w_ffn_gate"], grads["w_ffn_up"] = g_w_gt.T, g_w_ut.T
    for nm in ("w_ffn_gate", "w_ffn_up", "w_ffn_down"):
        update(nm)
    grads["w_conv_out"], grads["w_hgrn_out"], grads["w_o"] = reduced("mix", res["w_ffn_down"][1])
    for nm in ("w_conv_out", "w_hgrn_out", "w_o"):
        update(nm)
    grads["w_in"], = reduced("in", res["w_o"][1])
    update("w_in")
    outs = [[res[nm][i] for nm in weights] for i in range(4)]
    return (loss, dx.reshape(x.shape), *outs[0], *outs[1], *outs[2], *outs[3])
```

```python
import functools

import numpy as np
import jax
import jax.numpy as jnp
from jax import lax
from jax.experimental import pallas as pl
from jax.experimental.pallas import tpu as pltpu

f32, bf16 = jnp.float32, jnp.bfloat16
SDS = jax.ShapeDtypeStruct

EPS = 1e-6
HEADS, DK, CHUNK = 8, 128, 64
N_MOD = 6
NDEV = 8
ADAM_LR, ADAM_B1, ADAM_B2, ADAM_EPS, ADAM_WD, ADAM_STEP = 0.001, 0.9, 0.999, 1e-08, 0.01, 10
LEVELS = (32, 16, 8, 4, 2, 1)
V7X_VMEM_LIMIT = 56 * 1024 * 1024
HBM = pl.BlockSpec(memory_space=pltpu.HBM)
MESH = pl.DeviceIdType.MESH


def _params(sem):
    return pltpu.CompilerParams(dimension_semantics=sem, vmem_limit_bytes=V7X_VMEM_LIMIT)


def _sigmoid(x):
    return jax.nn.sigmoid(x)


def _dsilu(x, s):
    return s * (1.0 + x * (1.0 - s))


def _mesh_pos():
    x, y, c = lax.axis_index("x"), lax.axis_index("y"), lax.axis_index("c")
    return x, y, c


def _peer(pos, j):
    x, y, c = pos
    return (1 - x if j & 4 else x, 1 - y if j & 2 else y, 1 - c if j & 1 else c)


def _flat(pos):
    return 4 * pos[0] + 2 * pos[1] + pos[2]


def _all_gather(name, arrs):
    n = len(arrs)
    out_shapes = [SDS((NDEV,) + a.shape, a.dtype) for a in arrs]

    def body(*refs):
        ins, outs = refs[:n], refs[n:2 * n]
        send_sems, recv_sems, local_sems = refs[2 * n:]
        pos = _mesh_pos()
        me = _flat(pos)

        def copy(a, j, frm, to_pos):
            k = a * (NDEV - 1) + j - 1
            return pltpu.make_async_remote_copy(src_ref=ins[a], dst_ref=outs[a].at[frm], send_sem=send_sems.at[k],
                                                recv_sem=recv_sems.at[k], device_id=to_pos, device_id_type=MESH)

        local = [pltpu.make_async_copy(ins[a], outs[a].at[me], local_sems.at[a]) for a in range(n)]
        for cp in local:
            cp.start()
        sends = [copy(a, j, me, _peer(pos, j)) for j in range(1, NDEV) for a in range(n)]
        for cp in sends:
            cp.start()
        for j in range(1, NDEV):
            for a in range(n):
                copy(a, j, _flat(_peer(pos, j)), pos).wait_recv()
        for cp in sends:
            cp.wait_send()
        for cp in local:
            cp.wait()

    return pl.pallas_call(
        body, name=name, out_shape=out_shapes, in_specs=[HBM] * n, out_specs=[HBM] * n,
        scratch_shapes=[pltpu.SemaphoreType.DMA((n * (NDEV - 1),)), pltpu.SemaphoreType.DMA((n * (NDEV - 1),)),
                        pltpu.SemaphoreType.DMA((n,))],
    )(*arrs)


SEM = pl.BlockSpec(memory_space=pltpu.SEMAPHORE)
ANY = pl.BlockSpec(memory_space=pl.ANY)
EFFECT = pltpu.SideEffectType.DATAFLOW_SIDE_EFFECTING
ICI_RELATIONS = (2, 4, 6)


def _chip(pos):
    return 2 * pos[0] + pos[1]


def _hbm(a):
    return pltpu.with_memory_space_constraint(a, pltpu.HBM)


def _plan_copy(plan_entry, k, pos, frm, to, src_refs, land_refs, send_sems, recv_sems):
    a, _, src_slot, dst_slot = plan_entry
    s = src_refs[a] if src_slot is None else src_refs[a].at[src_slot(frm, to)]
    return pltpu.make_async_remote_copy(src_ref=s, dst_ref=land_refs[a].at[dst_slot(frm, to)], send_sem=send_sems.at[k],
                                        recv_sem=recv_sems.at[k], device_id=to, device_id_type=MESH)


def _push_start(name, srcs, lands, plan, after=None):
    ns, nb, nk = len(srcs), len(srcs) + len(lands), len(plan)
    extra = [] if after is None else [after]

    def body(*refs):
        land_refs = refs[ns:nb]
        src_refs = refs[:ns] if ns else land_refs
        send_sems, recv_sems = refs[nb + len(extra)], refs[nb + len(extra) + 1]
        pos = _mesh_pos()
        for k, e in enumerate(plan):
            _plan_copy(e, k, pos, pos, _peer(pos, e[1]), src_refs, land_refs, send_sems, recv_sems).start()
        refs[-1][...] = jnp.zeros_like(refs[-1])

    outs = pl.pallas_call(
        body, name=name,
        out_shape=[pltpu.SemaphoreType.DMA((nk,)), pltpu.SemaphoreType.DMA((nk,))] + [pltpu.HBM(a.shape, a.dtype) for a in srcs + lands]
        + [SDS((8, 128), f32)],
        in_specs=[HBM] * nb + [ANY] * len(extra), out_specs=[SEM, SEM] + [HBM] * nb + [pl.BlockSpec(memory_space=pltpu.VMEM)],
        input_output_aliases={i: 2 + i for i in range(nb)},
        compiler_params=pltpu.CompilerParams(has_side_effects=EFFECT),
    )(*[_hbm(a) for a in srcs + lands], *extra)
    return outs[0], outs[1], list(outs[2:2 + ns]), list(outs[2 + ns:2 + nb]), outs[-1]


def _push_wait(name, send_sems, recv_sems, srcs, lands, plan, after):
    ns, nb = len(srcs), len(srcs) + len(lands)

    def body(*refs):
        land_refs = refs[ns:nb]
        src_refs = refs[:ns] if ns else land_refs
        ssem, rsem = refs[nb], refs[nb + 1]
        pos = _mesh_pos()
        for k, e in enumerate(plan):
            peer = _peer(pos, e[1])
            _plan_copy(e, k, pos, pos, peer, src_refs, land_refs, ssem, rsem).wait_send()
            _plan_copy(e, k, pos, peer, pos, src_refs, land_refs, ssem, rsem).wait_recv()

    outs = pl.pallas_call(
        body, name=name, out_shape=[pltpu.HBM(a.shape, a.dtype) for a in srcs + lands],
        in_specs=[HBM] * nb + [SEM, SEM, ANY], out_specs=[HBM] * nb,
        input_output_aliases={i: i for i in range(nb)},
        compiler_params=pltpu.CompilerParams(has_side_effects=EFFECT),
    )(*srcs, *lands, send_sems, recv_sems, after)
    return list(outs[:ns]), list(outs[ns:])


def _after_tokens(small, tokens):
    for t in tokens:
        if t is not None:
            small = small + t[0:1, 0:1].reshape((1,) * small.ndim)
    return small


def _place_shard(name, me, shard):
    r, c = shard.shape
    tr = _row_tile(r)

    def body(me_ref, s_ref, o_ref):
        o_ref[...] = s_ref[...].astype(bf16)

    return pl.pallas_call(
        body, name=name, out_shape=SDS((NDEV, r, c), bf16),
        grid_spec=pltpu.PrefetchScalarGridSpec(
            num_scalar_prefetch=1, grid=(r // tr,), in_specs=[pl.BlockSpec((tr, c), lambda i, me_ref: (i, 0))],
            out_specs=pl.BlockSpec((None, tr, c), lambda i, me_ref: (me_ref[0], i, 0))),
        compiler_params=_params(("parallel",)),
    )(me, shard)


def _row_tile(r):
    return max(t for t in range(16, 257, 16) if r % t == 0)


def _sibling_forward(name, lands):
    n = len(lands)
    nk = n * len(ICI_RELATIONS)

    def body(*refs):
        land_refs = refs[:n]
        send_sems, recv_sems = refs[2 * n:]
        pos = _mesh_pos()
        sib = _peer(pos, 1)

        def copy(a, i, frm, to):
            k = a * len(ICI_RELATIONS) + i
            slot = _flat(_peer(frm, ICI_RELATIONS[i]))
            return pltpu.make_async_remote_copy(src_ref=land_refs[a].at[slot], dst_ref=land_refs[a].at[slot], send_sem=send_sems.at[k],
                                                recv_sem=recv_sems.at[k], device_id=to, device_id_type=MESH)

        sends = [copy(a, i, pos, sib) for a in range(n) for i in range(len(ICI_RELATIONS))]
        for cp in sends:
            cp.start()
        for a in range(n):
            for i in range(len(ICI_RELATIONS)):
                copy(a, i, sib, pos).wait_recv()
        for cp in sends:
            cp.wait_send()

    outs = pl.pallas_call(
        body, name=name, out_shape=[SDS(a.shape, a.dtype) for a in lands], in_specs=[HBM] * n, out_specs=[HBM] * n,
        input_output_aliases={i: i for i in range(n)},
        scratch_shapes=[pltpu.SemaphoreType.DMA((nk,)), pltpu.SemaphoreType.DMA((nk,))],
    )(*lands)
    return list(outs)


def _sibling_swap(name, grads):
    n = len(grads)
    nchip = NDEV // 2

    def body(*refs):
        g_refs, out_refs = refs[:n], refs[n:2 * n]
        send_sems, recv_sems = refs[2 * n:]
        pos = _mesh_pos()
        sib = _peer(pos, 1)
        sends = []
        for a in range(n):
            for q in range(nchip):
                k = a * nchip + q
                sends.append(pltpu.make_async_remote_copy(src_ref=g_refs[a].at[2 * q + sib[2]], dst_ref=out_refs[a].at[q],
                                                          send_sem=send_sems.at[k], recv_sem=recv_sems.at[k], device_id=sib,
                                                          device_id_type=MESH))
        for cp in sends:
            cp.start()
        for cp in sends:
            cp.wait()

    return pl.pallas_call(
        body, name=name, out_shape=[SDS((nchip,) + a.shape[1:], a.dtype) for a in grads], in_specs=[HBM] * n, out_specs=[HBM] * n,
        scratch_shapes=[pltpu.SemaphoreType.DMA((n * nchip,)), pltpu.SemaphoreType.DMA((n * nchip,))],
    )(*grads)


def _pair_sum(name, core, grad, recv, tr):
    _, r, c = grad.shape
    nchip = NDEV // 2

    def body(core_ref, g_ref, r_ref, o_ref):
        o_ref[...] = (g_ref[...].astype(f32) + r_ref[...].astype(f32)).astype(o_ref.dtype)

    return pl.pallas_call(
        body, name=name, out_shape=SDS((nchip, r, c), grad.dtype),
        grid_spec=pltpu.PrefetchScalarGridSpec(
            num_scalar_prefetch=1, grid=(nchip, r // tr),
            in_specs=[pl.BlockSpec((None, tr, c), lambda q, i, core_ref: (2 * q + core_ref[0], i, 0)),
                      pl.BlockSpec((None, tr, c), lambda q, i, core_ref: (q, i, 0))],
            out_specs=pl.BlockSpec((None, tr, c), lambda q, i, core_ref: (q, i, 0))),
        compiler_params=_params(("parallel", "parallel")),
    )(core, grad, recv)


def _chip_sum(name, chip, pairs, recv, tr):
    nchip, r, c = pairs.shape

    def body(chip_ref, p_ref, r_ref, o_ref):
        mine = chip_ref[0]
        acc = jnp.zeros(o_ref.shape, f32)
        for q in range(nchip):
            acc = acc + jnp.where(mine == q, p_ref[q].astype(f32), r_ref[q].astype(f32))
        o_ref[...] = acc

    return pl.pallas_call(
        body, name=name, out_shape=SDS((r, c), f32),
        grid_spec=pltpu.PrefetchScalarGridSpec(
            num_scalar_prefetch=1, grid=(r // tr,),
            in_specs=[pl.BlockSpec((nchip, tr, c), lambda i, chip_ref: (0, i, 0))] * 2,
            out_specs=pl.BlockSpec((tr, c), lambda i, chip_ref: (i, 0))),
        compiler_params=_params(("parallel",)),
    )(chip, pairs, recv)


def _matmul(name, a, b, out_shape, out_dtype, grid, a_spec, b_spec, o_spec, dims, after=None):
    ksteps = grid[2]
    acc_shape = tuple(d for d in o_spec.block_shape if d is not None)
    extra = [] if after is None else [after]

    def body(a_ref, b_ref, *rest):
        o_ref, acc = rest[len(extra)], rest[len(extra) + 1:]
        prod = lax.dot_general(a_ref[...], b_ref[...], (dims, ((), ())), preferred_element_type=f32)
        if ksteps == 1:
            o_ref[...] = prod.astype(o_ref.dtype)
        else:
            k = pl.program_id(2)

            @pl.when(k == 0)
            def _():
                acc[0][...] = prod

            @pl.when(k > 0)
            def _():
                acc[0][...] += prod

            @pl.when(k == ksteps - 1)
            def _():
                o_ref[...] = acc[0][...].astype(o_ref.dtype)

    return pl.pallas_call(
        body, name=name, grid=grid, in_specs=[a_spec, b_spec] + [ANY] * len(extra), out_specs=o_spec,
        out_shape=SDS(out_shape, out_dtype), scratch_shapes=[] if ksteps == 1 else [pltpu.VMEM(acc_shape, f32)],
        compiler_params=_params(("parallel", "parallel", "arbitrary")),
    )(a, b, *extra)


NN, NT, TN = ((1,), (0,)), ((1,), (1,)), ((0,), (0,))


def _rowwise(name, fn, n_rows, tm, rows, fulls, row_outs, acc_outs=(), ncol=1):
    assert ncol == 1 or not acc_outs
    nr, nf, no, na = len(rows), len(fulls), len(row_outs), len(acc_outs)
    in_specs = [pl.BlockSpec((tm, w), functools.partial(lambda i, j, cb: (i, cb + j), cb=cb)) for (_, w, cb) in rows]
    in_specs += [pl.BlockSpec(a.shape, functools.partial(lambda i, j, nd: (0,) * nd, nd=a.ndim)) for a in fulls]
    out_shape = [SDS((n_rows, w), dt) for (w, dt) in row_outs] + [SDS(s, f32) for s in acc_outs]
    out_specs = [pl.BlockSpec((tm, w // ncol), lambda i, j: (i, j)) for (w, _) in row_outs]
    out_specs += [pl.BlockSpec(s, functools.partial(lambda i, j, nd: (0,) * nd, nd=len(s))) for s in acc_outs]

    def body(*refs):
        if na:
            @pl.when(pl.program_id(0) == 0)
            def _():
                for r in refs[nr + nf + no:]:
                    r[...] = jnp.zeros(r.shape, r.dtype)
        fn(refs[:nr], refs[nr:nr + nf], refs[nr + nf:nr + nf + no], refs[nr + nf + no:])

    return pl.pallas_call(
        body, name=name, grid=(n_rows // tm, ncol), in_specs=in_specs, out_specs=out_specs, out_shape=out_shape,
        compiler_params=_params(("arbitrary" if na else "parallel", "arbitrary" if na else "parallel")),
    )(*[r[0] for r in rows], *fulls)


def _rms(x):
    r = lax.rsqrt(jnp.mean(x * x, axis=-1, keepdims=True) + EPS)
    return r, x * r


def _colsum(v):
    return jnp.sum(v, axis=0, keepdims=True)


def _shift_down(u, row, k):
    return jnp.where(row >= k, pltpu.roll(u, k, 0), 0.0)


def _shift_up(u, row, k):
    n = u.shape[0]
    return jnp.where(row < n - k, pltpu.roll(u, n - k, 0), 0.0)


def _conv_fwd(proj, conv_w, s, dc):
    nb = dc // 128

    def body(ab_ref, ac_ref, ax_ref, w_ref, z_ref):
        u = ac_ref[...] * ax_ref[...]
        row = lax.broadcasted_iota(jnp.int32, u.shape, 0)
        w = w_ref[...]
        cv = w[0:1] * _shift_down(u, row, 2) + w[1:2] * _shift_down(u, row, 1) + w[2:3] * u
        z_ref[...] = (ab_ref[...] * cv).astype(z_ref.dtype)

    col = lambda off: pl.BlockSpec((s, 128), functools.partial(lambda j, off: (0, off + j), off=off))
    return pl.pallas_call(
        body, name="conv_fwd", grid=(nb,), in_specs=[col(0), col(nb), col(2 * nb), pl.BlockSpec((3, 128), lambda j: (0, j))],
        out_specs=pl.BlockSpec((s, 128), lambda j: (0, j)), out_shape=SDS((s, dc), bf16), compiler_params=_params(("parallel",)),
    )(proj, proj, proj, conv_w)


def _conv_bwd(proj, conv_w, dz, s, dc):
    nb = dc // 128

    def body(ab_ref, ac_ref, ax_ref, w_ref, dz_ref, dab_ref, dac_ref, dax_ref, dw_ref):
        ab, ac, ax, dzv = ab_ref[...], ac_ref[...], ax_ref[...], dz_ref[...]
        u = ac * ax
        row = lax.broadcasted_iota(jnp.int32, u.shape, 0)
        w = w_ref[...]
        u1, u2 = _shift_down(u, row, 1), _shift_down(u, row, 2)
        cv = w[0:1] * u2 + w[1:2] * u1 + w[2:3] * u
        dcv = dzv * ab
        dab_ref[...] = (dzv * cv).astype(dab_ref.dtype)
        du = w[2:3] * dcv + w[1:2] * _shift_up(dcv, row, 1) + w[0:1] * _shift_up(dcv, row, 2)
        dac_ref[...] = (du * ax).astype(dac_ref.dtype)
        dax_ref[...] = (du * ac).astype(dax_ref.dtype)
        dw_ref[0:1, :] = _colsum(dcv * u2)
        dw_ref[1:2, :] = _colsum(dcv * u1)
        dw_ref[2:3, :] = _colsum(dcv * u)

    col = lambda off: pl.BlockSpec((s, 128), functools.partial(lambda j, off: (0, off + j), off=off))
    blk = pl.BlockSpec((s, 128), lambda j: (0, j))
    return pl.pallas_call(
        body, name="conv_bwd", grid=(nb,),
        in_specs=[col(0), col(nb), col(2 * nb), pl.BlockSpec((3, 128), lambda j: (0, j)), blk],
        out_specs=[blk, blk, blk, pl.BlockSpec((3, 128), lambda j: (0, j))],
        out_shape=[SDS((s, dc), bf16)] * 3 + [SDS((3, dc), f32)], compiler_params=_params(("parallel",)),
    )(proj, proj, proj, conv_w, dz)


def _level_masks():
    t = np.arange(CHUNK)[:, None]
    s = np.arange(CHUNK)[None, :]
    m = np.stack([((t & h) != 0) & ((s & h) == 0) & (t // (2 * h) == s // (2 * h)) for h in LEVELS]).astype(np.float32)
    return jnp.asarray(m), jnp.asarray(m.transpose(0, 2, 1))


def _cumsum_rows(x, row):
    for sh in (1, 2, 4, 8, 16, 32):
        x = x + jnp.where(row >= sh, pltpu.roll(x, sh, 0), 0.0)
    return x


def _rev_cumsum_rows(x, row):
    n = x.shape[0]
    for sh in (1, 2, 4, 8, 16, 32):
        x = x + jnp.where(row < n - sh, pltpu.roll(x, n - sh, 0), 0.0)
    return x


def _chunk_terms(qp, fl, lb):
    row = lax.broadcasted_iota(jnp.int32, qp.shape, 0)
    sig = _sigmoid(fl)
    f = lb + (1.0 - lb) * sig
    k = 1.0 - f
    sq = _sigmoid(qp)
    qh = qp * sq
    b = _cumsum_rows(jnp.log(f), row)
    sub = lax.broadcasted_iota(jnp.int32, (CHUNK // 8, 8, DK), 1)
    b8 = b.reshape(CHUNK // 8, 8, DK)
    us, exs, ups = [], [], []
    for m in LEVELS:
        sb = 2 * m
        if sb >= 8:
            b3 = b.reshape(CHUNK // sb, sb, DK)
            bref = jnp.broadcast_to(b3[:, m - 1:m, :], b3.shape).reshape(CHUNK, DK)
        else:
            bref8 = None
            for j in range(8 // sb):
                cand = jnp.broadcast_to(b8[:, j * sb + m - 1:j * sb + m, :], b8.shape)
                bref8 = cand if bref8 is None else jnp.where(sub >= j * sb, cand, bref8)
            bref = bref8.reshape(CHUNK, DK)
        up = (row & m) != 0
        ex = jnp.exp(jnp.where(up, b - bref, bref - b))
        us.append((jnp.where(up, qh, k) * ex).astype(bf16))
        exs.append(ex)
        ups.append(up)
    blast = b[CHUNK - 1:CHUNK, :]
    eb, ebl = jnp.exp(b), jnp.exp(blast - b)
    return dict(sig=sig, f=f, k=k, sq=sq, qh=qh, u=jnp.stack(us), ex=exs, up=ups, eb=eb, ebl=ebl, qt=qh * eb, kt=k * ebl,
                el=jnp.exp(blast), row=row)


def _scores(t, mask):
    pl_ = jnp.einsum("ltk,lsk->lts", t["u"], t["u"], preferred_element_type=f32)
    p = jnp.sum(pl_ * mask, axis=0)
    r = lax.broadcasted_iota(jnp.int32, (CHUNK, CHUNK), 0)
    c = lax.broadcasted_iota(jnp.int32, (CHUNK, CHUNK), 1)
    diag = jnp.sum(t["qh"] * t["k"], axis=-1, keepdims=True)
    return p + jnp.where(r == c, diag, 0.0)


def _hgrn_fwd(proj, lb_param, gnorm, s, dv_total, tb):
    nchunk = tb // CHUNK
    masks, _ = _level_masks()
    q0, f0, v0, g0 = 3 * HEADS, 4 * HEADS, 5 * HEADS, 6 * HEADS

    def body(q_ref, f_ref, v_ref, g_ref, lb_ref, gn_ref, mask_ref, og_ref, o_ref, st_ref, state):
        @pl.when(pl.program_id(1) == 0)
        def _():
            state[...] = jnp.zeros_like(state)

        lbp = lb_ref[...]
        lb = _sigmoid(lbp[0:1, :] - lbp[1:2, :])
        mask = mask_ref[...]
        for i in range(nchunk):
            rs = pl.ds(i * CHUNK, CHUNK)
            t = _chunk_terms(q_ref[rs, :], f_ref[rs, :], lb)
            v = v_ref[rs, :]
            vb = v.astype(bf16)
            st = state[...]
            st_ref[i] = st
            p = _scores(t, mask)
            o = jnp.dot(p.astype(bf16), vb, preferred_element_type=f32)
            o += lax.dot_general(t["qt"].astype(bf16), st.astype(bf16), (NT, ((), ())), preferred_element_type=f32)
            state[...] = st * t["el"] + lax.dot_general(vb, t["kt"].astype(bf16), (TN, ((), ())), preferred_element_type=f32)
            o_ref[rs, :] = o
            r, oh = _rms(o)
            g = g_ref[rs, :]
            og_ref[rs, :] = (oh * gn_ref[...] * (g * _sigmoid(g))).astype(og_ref.dtype)

    col = lambda off: pl.BlockSpec((tb, DK), functools.partial(lambda h, t, off: (t, off + h), off=off))
    blk = pl.BlockSpec((tb, DK), lambda h, t: (t, h))
    return pl.pallas_call(
        body, name="hgrn_fwd", grid=(HEADS, s // tb),
        in_specs=[col(q0), col(f0), col(v0), col(g0), pl.BlockSpec((2, DK), lambda h, t: (0, h)),
                  pl.BlockSpec((1, DK), lambda h, t: (0, 0)), pl.BlockSpec(masks.shape, lambda h, t: (0, 0, 0))],
        out_specs=[blk, blk, pl.BlockSpec((nchunk, None, DK, DK), lambda h, t: (t, h, 0, 0))],
        out_shape=[SDS((s, dv_total), bf16), SDS((s, dv_total), f32), SDS((s // CHUNK, HEADS, DK, DK), f32)],
        scratch_shapes=[pltpu.VMEM((DK, DK), f32)], compiler_params=_params(("parallel", "arbitrary")),
    )(proj, proj, proj, proj, lb_param, gnorm, masks)


def _hgrn_bwd(proj, lb_param, gnorm, o_saved, states, dog, s, dv_total, tb):
    nchunk = tb // CHUNK
    nt = s // tb
    nc_total = s // CHUNK
    masks, masks_t = _level_masks()
    q0, f0, v0, g0 = 3 * HEADS, 4 * HEADS, 5 * HEADS, 6 * HEADS

    def body(q_ref, f_ref, v_ref, g_ref, lb_ref, gn_ref, mask_ref, maskt_ref, o_ref, dog_ref, st_ref, stn_ref,
             dq_ref, df_ref, dv_ref, dg_ref, dlb_ref, dgn_ref, gstate):
        h_id, t_id = pl.program_id(0), pl.program_id(1)

        @pl.when(t_id == 0)
        def _():
            gstate[...] = jnp.zeros_like(gstate)
            dlb_ref[...] = jnp.zeros_like(dlb_ref)

        @pl.when((t_id == 0) & (h_id == 0))
        def _():
            dgn_ref[...] = jnp.zeros_like(dgn_ref)

        lbp = lb_ref[...]
        lb = _sigmoid(lbp[0:1, :] - lbp[1:2, :])
        mask, maskt = mask_ref[...], maskt_ref[...]
        gn = gn_ref[...]
        for i in reversed(range(nchunk)):
            rs = pl.ds(i * CHUNK, CHUNK)
            qp, fl, v, g = q_ref[rs, :], f_ref[rs, :], v_ref[rs, :], g_ref[rs, :]
            t = _chunk_terms(qp, fl, lb)
            vb = v.astype(bf16)
            st0 = st_ref[i]
            st1 = st_ref[i + 1] if i + 1 < nchunk else stn_ref[0]
            gt = gstate[...]
            o = o_ref[rs, :]
            r, oh = _rms(o)
            sg = _sigmoid(g)
            dog_v = dog_ref[rs, :]
            dg_ref[rs, :] = (dog_v * (oh * gn) * _dsilu(g, sg)).astype(dg_ref.dtype)
            don = dog_v * (g * sg)
            dgn_ref[...] += _colsum(don * oh)
            doh = don * gn
            do = r * (doh - oh * jnp.mean(doh * oh, axis=-1, keepdims=True))
            dob = do.astype(bf16)
            d = lax.dot_general(dob, vb, (NT, ((), ())), preferred_element_type=f32)
            dt = lax.dot_general(vb, dob, (NT, ((), ())), preferred_element_type=f32)
            z = (mask * d[None] + maskt * dt[None]).astype(bf16)
            rr = jnp.einsum("lts,lsk->ltk", z, t["u"], preferred_element_type=f32)
            dq = jnp.zeros((CHUNK, DK), f32)
            dk = jnp.zeros((CHUNK, DK), f32)
            qdk = jnp.zeros((CHUNK, DK), f32)
            for li in range(len(LEVELS)):
                du = t["ex"][li] * rr[li]
                dq += jnp.where(t["up"][li], du, 0.0)
                dk += jnp.where(t["up"][li], 0.0, du)
                e = t["u"][li].astype(f32) * rr[li]
                qdk += jnp.where(t["up"][li], e, -e)
            dd = jnp.sum(do * v, axis=-1, keepdims=True)
            dq += dd * t["k"]
            dk += dd * t["qh"]
            gtb = gt.astype(bf16)
            ktb, qtb = t["kt"].astype(bf16), t["qt"].astype(bf16)
            dq_in = jnp.dot(dob, st0.astype(bf16), preferred_element_type=f32)
            dk_in = jnp.dot(vb, gtb, preferred_element_type=f32)
            dq += t["eb"] * dq_in
            dk += t["ebl"] * dk_in
            qdk += qtb.astype(f32) * dq_in - ktb.astype(f32) * dk_in
            p = _scores(t, mask)
            dvv = lax.dot_general(p.astype(bf16), dob, (TN, ((), ())), preferred_element_type=f32)
            dvv += lax.dot_general(ktb, gtb, (NT, ((), ())), preferred_element_type=f32)
            dv_ref[rs, :] = dvv.astype(dv_ref.dtype)
            a_end = _colsum(gtb.astype(f32) * st1)
            dlf = _rev_cumsum_rows(qdk, t["row"]) + a_end
            dfv = dlf / t["f"] - dk
            df_ref[rs, :] = (dfv * (1.0 - lb) * t["sig"] * (1.0 - t["sig"])).astype(df_ref.dtype)
            dlb_ref[...] += _colsum(dfv * (1.0 - t["sig"]))
            dq_ref[rs, :] = (dq * _dsilu(qp, t["sq"])).astype(dq_ref.dtype)
            gstate[...] = gt * t["el"] + lax.dot_general(dob, qtb, (TN, ((), ())), preferred_element_type=f32)

    rev = lambda t: nt - 1 - t
    col = lambda off: pl.BlockSpec((tb, DK), functools.partial(lambda h, t, off: (rev(t), off + h), off=off))
    blk = pl.BlockSpec((tb, DK), lambda h, t: (rev(t), h))
    nxt = lambda h, t: (jnp.minimum((rev(t) + 1) * nchunk, nc_total - 1), h, 0, 0)
    return pl.pallas_call(
        body, name="hgrn_bwd", grid=(HEADS, nt),
        in_specs=[col(q0), col(f0), col(v0), col(g0), pl.BlockSpec((2, DK), lambda h, t: (0, h)),
                  pl.BlockSpec((1, DK), lambda h, t: (0, 0)), pl.BlockSpec(masks.shape, lambda h, t: (0, 0, 0)),
                  pl.BlockSpec(masks.shape, lambda h, t: (0, 0, 0)), blk, blk,
                  pl.BlockSpec((nchunk, None, DK, DK), lambda h, t: (rev(t), h, 0, 0)),
                  pl.BlockSpec((1, None, DK, DK), nxt)],
        out_specs=[blk, blk, blk, blk, pl.BlockSpec((1, DK), lambda h, t: (0, h)), pl.BlockSpec((1, DK), lambda h, t: (0, 0))],
        out_shape=[SDS((s, dv_total), bf16)] * 4 + [SDS((1, HEADS * DK), f32), SDS((1, DK), f32)],
        scratch_shapes=[pltpu.VMEM((DK, DK), f32)], compiler_params=_params(("arbitrary", "arbitrary")),
    )(proj, proj, proj, proj, lb_param, gnorm, masks, masks_t, o_saved, dog, states, states)


def _adamw(name, g, w, m, v):
    r, c = w.shape
    tr = r
    for cand in (256, 128, 64, 32, 16, 8):
        if r % cand == 0 and r > cand:
            tr = cand
            break

    def body(g_ref, w_ref, m_ref, v_ref, d_ref, mo_ref, vo_ref):
        gv = g_ref[...]
        mn = ADAM_B1 * m_ref[...] + (1.0 - ADAM_B1) * gv
        vn = ADAM_B2 * v_ref[...] + (1.0 - ADAM_B2) * jnp.square(gv)
        m_hat = mn / (1.0 - ADAM_B1 ** ADAM_STEP)
        v_hat = vn / (1.0 - ADAM_B2 ** ADAM_STEP)
        d_ref[...] = -ADAM_LR * (m_hat / (jnp.sqrt(v_hat) + ADAM_EPS) + ADAM_WD * w_ref[...])
        mo_ref[...] = mn
        vo_ref[...] = vn

    blk = pl.BlockSpec((tr, c), lambda i: (i, 0))
    return pl.pallas_call(
        body, name=name, grid=(r // tr,), in_specs=[blk] * 4, out_specs=[blk] * 3, out_shape=[SDS((r, c), f32)] * 3,
        compiler_params=_params(("parallel",)),
    )(g, w, m, v)


def _ffn_in(h2, w_gt, w_ut, tm, ffb):
    s, d = h2.shape
    dff = w_gt.shape[0]

    def body(a_ref, wg_ref, wu_ref, g_ref, u_ref, act_ref):
        a = a_ref[...]
        g = lax.dot_general(a, wg_ref[...], (NT, ((), ())), preferred_element_type=f32)
        u = lax.dot_general(a, wu_ref[...], (NT, ((), ())), preferred_element_type=f32)
        g_ref[...] = g.astype(bf16)
        u_ref[...] = u.astype(bf16)
        act_ref[...] = (g * _sigmoid(g) * u).astype(bf16)

    w_spec = pl.BlockSpec((ffb, d), lambda j, i: (j, 0))
    o_spec = pl.BlockSpec((tm, ffb), lambda j, i: (i, j))
    return pl.pallas_call(
        body, name="ffn_in", grid=(dff // ffb, s // tm), in_specs=[pl.BlockSpec((tm, d), lambda j, i: (i, 0)), w_spec, w_spec],
        out_specs=[o_spec] * 3, out_shape=[SDS((s, dff), bf16)] * 3, compiler_params=_params(("parallel", "parallel")),
    )(h2, w_gt, w_ut)


def _ffn_down_bwd(dff_out, w_d, gg, uu, tm, ffb):
    s, d = dff_out.shape
    dff = w_d.shape[0]

    def body(a_ref, w_ref, g_ref, u_ref, dg_ref, du_ref):
        da = lax.dot_general(a_ref[...], w_ref[...], (NT, ((), ())), preferred_element_type=f32)
        g, u = g_ref[...].astype(f32), u_ref[...].astype(f32)
        sg = _sigmoid(g)
        dg_ref[...] = (da * u * _dsilu(g, sg)).astype(bf16)
        du_ref[...] = (da * g * sg).astype(bf16)

    t_spec = pl.BlockSpec((tm, ffb), lambda j, i: (i, j))
    return pl.pallas_call(
        body, name="d_ffn_down_in", grid=(dff // ffb, s // tm),
        in_specs=[pl.BlockSpec((tm, d), lambda j, i: (i, 0)), pl.BlockSpec((ffb, d), lambda j, i: (j, 0)), t_spec, t_spec],
        out_specs=[t_spec] * 2, out_shape=[SDS((s, dff), bf16)] * 2, compiler_params=_params(("parallel", "parallel")),
    )(dff_out, w_d, gg, uu)


def _local_step(x, tgt, mod, g_mix, g_ffn, g_fin, lb_param, gnorm, conv_w, get_w, put_g):
    s, d = x.shape
    dc = d // 2
    tm = min(512, s)
    te = min(256, s)
    tb = min(256, s)
    mt = s // tm
    sh_m, sc_m, gt_m, sh_f, sc_f, gt_f = [mod[i] for i in range(N_MOD)]
    dh2 = d // 2

    def e1(rows, fulls, outs, accs):
        xv = rows[0][...]
        g, sc, sh = [r[...] for r in fulls]
        _, xh = _rms(xv)
        outs[0][...] = (xh * g * (1.0 + sc) + sh).astype(bf16)

    h, = _rowwise("prenorm_mix", e1, s, te, [(x, d, 0)], [g_mix, sc_m, sh_m], [(d, bf16)])
    w_in, = get_w("in", h)
    nsh, _, win_sh = w_in.shape
    proj = _matmul("proj", h, w_in, (s, nsh * win_sh), f32, (nsh, mt, 1), pl.BlockSpec((tm, d), lambda j, i, k: (i, 0)),
                   pl.BlockSpec((None, d, win_sh), lambda j, i, k: (j, 0, 0)), pl.BlockSpec((tm, win_sh), lambda j, i, k: (i, j)), NN)
    z_a = _conv_fwd(proj, conv_w, s, dc)
    og, o_saved, states = _hgrn_fwd(proj, lb_param, gnorm, s, dc, tb)
    tm2 = min(1024, s)
    w_co, w_ho, w_o = get_w("mix", og)

    def out_proj(name, a, w):
        return _matmul(name, a, w, (s, d), f32, (nsh, s // tm2, 1), pl.BlockSpec((tm2, dc), lambda j, i, k: (i, 0)),
                       pl.BlockSpec((None, dc, d // nsh), lambda j, i, k: (j, 0, 0)),
                       pl.BlockSpec((tm2, d // nsh), lambda j, i, k: (i, j)), NN)

    y_a = out_proj("conv_out", z_a, w_co)
    y_b = out_proj("hgrn_out", og, w_ho)
    ga0, gb0 = 7, 9

    def e5(rows, fulls, outs, accs):
        ya, yb, ga, gb = [r[...] for r in rows]
        outs[0][...] = (_sigmoid(ga) * ya + _sigmoid(gb) * yb).astype(bf16)

    merged, = _rowwise("merge", e5, s, te, [(y_a, dh2, 0), (y_b, dh2, 0), (proj, dh2, ga0), (proj, dh2, gb0)], [], [(d, bf16)], ncol=2)
    mo = _matmul("mix_out", merged, w_o, (s, d), f32, (2, mt, 1), pl.BlockSpec((tm, d), lambda j, i, k: (i, 0)),
                 pl.BlockSpec((d, dh2), lambda j, i, k: (0, j)), pl.BlockSpec((tm, dh2), lambda j, i, k: (i, j)), NN)

    def e6(rows, fulls, outs, accs):
        xv, mov = rows[0][...], rows[1][...]
        gt, g, sc, sh = [r[...] for r in fulls]
        x1 = xv + gt * mov
        outs[0][...] = x1
        _, xh = _rms(x1)
        outs[1][...] = (xh * g * (1.0 + sc) + sh).astype(bf16)

    x1, h2 = _rowwise("prenorm_ffn", e6, s, te, [(x, d, 0), (mo, d, 0)], [gt_m, g_ffn, sc_f, sh_f], [(d, f32), (d, bf16)])
    w_gt, w_ut, w_d = get_w("ffn", h2)
    dff_ = w_d.shape[0]
    ffb = dff_ // 4

    gg, uu, act = _ffn_in(h2, w_gt, w_ut, tm, ffb)
    ff = _matmul("ffn_down", act, w_d, (s, d), f32, (2, mt, 1), pl.BlockSpec((tm, dff_), lambda j, i, k: (i, 0)),
                 pl.BlockSpec((dff_, dh2), lambda j, i, k: (0, j)), pl.BlockSpec((tm, dh2), lambda j, i, k: (i, j)), NN)

    def e9(rows, fulls, outs, accs):
        x1v, ffv, tv = [r[...] for r in rows]
        gt, gf = fulls[0][...], fulls[1][...]
        x2 = x1v + gt * ffv
        r, xh = _rms(x2)
        err = xh * gf - tv
        accs[0][...] += 0.5 * jnp.sum(jnp.mean(err * err, axis=-1, keepdims=True), axis=0, keepdims=True)
        dy = err / d
        accs[1][...] += _colsum(dy * xh)
        dxh = dy * gf
        dx2 = r * (dxh - xh * jnp.mean(dxh * xh, axis=-1, keepdims=True))
        outs[0][...] = dx2
        outs[1][...] = (dx2 * gt).astype(bf16)
        accs[2][...] += _colsum(dx2 * ffv)

    dx2, dff, loss_acc, dg_fin, dgt_f = _rowwise("loss_head", e9, s, te, [(x1, d, 0), (ff, d, 0), (tgt, d, 0)], [gt_f, g_fin],
                                                 [(d, f32), (d, bf16)], [(1, 128), (1, d), (1, d)])
    dgg, duu = _ffn_down_bwd(dff, w_d, gg, uu, tm, ffb)

    def wgrad_rows(name, a, b, n_out):
        kb = 512
        return _matmul(name, a, b, (n_out, d), bf16, (n_out // kb, 2, 1), pl.BlockSpec((s, kb), lambda i, j, k: (0, i)),
                       pl.BlockSpec((s, dh2), lambda i, j, k: (0, j)), pl.BlockSpec((kb, dh2), lambda i, j, k: (i, j)), TN)

    gw_d = wgrad_rows("gw_ffn_down", act, dff, dff_)

    def ffn_in_bwd(name, a, w):
        return _matmul(name, a, w, (s, d), f32, (2, mt, 1), pl.BlockSpec((tm, dff_), lambda j, i, k: (i, 0)),
                       pl.BlockSpec((dff_, dh2), lambda j, i, k: (0, j)), pl.BlockSpec((tm, dh2), lambda j, i, k: (i, j)), NN)

    dh2a = ffn_in_bwd("d_ffn_gate_in", dgg, w_gt)
    dh2b = ffn_in_bwd("d_ffn_up_in", duu, w_ut)
    gw_gt = wgrad_rows("gw_ffn_gate", dgg, h2, dff_)
    gw_ut = wgrad_rows("gw_ffn_up", duu, h2, dff_)
    sc_f_late = _after_tokens(sc_f, [put_g("ffn", [gw_gt, gw_ut, gw_d])])

    def b5(rows, fulls, outs, accs):
        da, db, x1v, dx2v, mov = [r[...] for r in rows]
        sc, g, gt = [r[...] for r in fulls]
        dh = da + db
        r, xh = _rms(x1v)
        accs[0][...] += _colsum(dh)
        accs[1][...] += _colsum(dh * (xh * g))
        dn = dh * (1.0 + sc)
        accs[2][...] += _colsum(dn * xh)
        dxh = dn * g
        dx1 = dx2v + r * (dxh - xh * jnp.mean(dxh * xh, axis=-1, keepdims=True))
        outs[0][...] = dx1
        accs[3][...] += _colsum(dx1 * mov)
        outs[1][...] = (dx1 * gt).astype(bf16)

    dx1, dmo, dsh_f, dsc_f, dg_ffn, dgt_m = _rowwise(
        "d_prenorm_ffn", b5, s, te, [(dh2a, d, 0), (dh2b, d, 0), (x1, d, 0), (dx2, d, 0), (mo, d, 0)], [sc_f_late, g_ffn, gt_m],
        [(d, f32), (d, bf16)], [(1, d)] * 4)
    dmerged = _matmul("d_mix_out_in", dmo, w_o, (s, d), f32, (2, mt, 1), pl.BlockSpec((tm, d), lambda j, i, k: (i, 0)),
                      pl.BlockSpec((dh2, d), lambda j, i, k: (j, 0)), pl.BlockSpec((tm, dh2), lambda j, i, k: (i, j)), NT)
    gw_o = wgrad_rows("gw_mix_out", merged, dmo, d)

    def b7(rows, fulls, outs, accs):
        dm, ya, yb, ga, gb = [r[...] for r in rows]
        sa, sb_ = _sigmoid(ga), _sigmoid(gb)
        outs[0][...] = (dm * sa).astype(bf16)
        outs[1][...] = (dm * sb_).astype(bf16)
        outs[2][...] = (dm * ya * sa * (1.0 - sa)).astype(bf16)
        outs[3][...] = (dm * yb * sb_ * (1.0 - sb_)).astype(bf16)

    dya, dyb, dga, dgb = _rowwise("d_merge", b7, s, te, [(dmerged, dh2, 0), (y_a, dh2, 0), (y_b, dh2, 0), (proj, dh2, ga0), (proj, dh2, gb0)],
                                  [], [(d, bf16)] * 4, ncol=2)

    def out_proj_bwd(name, dy, w):
        return _matmul(name, dy, w, (s, dc), f32, (1, s // tm2, nsh), pl.BlockSpec((tm2, d // nsh), lambda j, i, k: (i, k)),
                       pl.BlockSpec((None, dc, d // nsh), lambda j, i, k: (k, 0, 0)), pl.BlockSpec((tm2, dc), lambda j, i, k: (i, 0)), NT)

    def out_proj_wgrad(name, a, dy):
        return _matmul(name, a, dy, (nsh, dc, d // nsh), bf16, (1, nsh, 1), pl.BlockSpec((s, dc), lambda i, j, k: (0, 0)),
                       pl.BlockSpec((s, d // nsh), lambda i, j, k: (0, j)), pl.BlockSpec((None, dc, d // nsh), lambda i, j, k: (j, 0, 0)), TN)

    dz_a = out_proj_bwd("d_conv_out_in", dya, w_co)
    dog = out_proj_bwd("d_hgrn_out_in", dyb, w_ho)
    gw_co = out_proj_wgrad("gw_conv_out", z_a, dya)
    gw_ho = out_proj_wgrad("gw_hgrn_out", og, dyb)
    conv_w_late = _after_tokens(conv_w, [put_g("mix", [gw_co, gw_ho, gw_o])])
    dab, dac, dax, dconv_w = _conv_bwd(proj, conv_w_late, dz_a, s, dc)
    dq, dfl, dvi, dgo, dlb, dgn = _hgrn_bwd(proj, lb_param, gnorm, o_saved, states, dog, s, dc, tb)
    dproj = jnp.concatenate([dab, dac, dax, dq, dfl, dvi, dgo, dga, dgb], axis=1)
    gw_in = _matmul("gw_proj", h, dproj, (nsh, d, win_sh), bf16, (nsh, d // 512, 1), pl.BlockSpec((s, 512), lambda j, i, k: (0, i)),
                    pl.BlockSpec((s, win_sh), lambda j, i, k: (0, j)), pl.BlockSpec((None, 512, win_sh), lambda j, i, k: (j, i, 0)), TN)
    dh = _matmul("d_proj_in", dproj, w_in, (s, d), f32, (1, s // tm2, nsh), pl.BlockSpec((tm2, win_sh), lambda j, i, k: (i, k)),
                 pl.BlockSpec((None, d, win_sh), lambda j, i, k: (k, 0, 0)), pl.BlockSpec((tm2, d), lambda j, i, k: (i, 0)), NT,
                 after=put_g("in", [gw_in]))

    def b12(rows, fulls, outs, accs):
        dhv, xv, dx1v = [r[...] for r in rows]
        sc, g = fulls[0][...], fulls[1][...]
        r, xh = _rms(xv)
        accs[0][...] += _colsum(dhv)
        accs[1][...] += _colsum(dhv * (xh * g))
        dn = dhv * (1.0 + sc)
        accs[2][...] += _colsum(dn * xh)
        dxh = dn * g
        outs[0][...] = dx1v + r * (dxh - xh * jnp.mean(dxh * xh, axis=-1, keepdims=True))

    dx, dsh_m, dsc_m, dg_mix = _rowwise("d_prenorm_mix", b12, s, te, [(dh, d, 0), (x, d, 0), (dx1, d, 0)], [sc_m, g_mix],
                                        [(d, f32)], [(1, d)] * 3)
    dmod = [dsh_m, dsc_m, dgt_m, dsh_f, dsc_f, dgt_f]
    small = dict(loss=loss_acc, g_mix=dg_mix, g_ffn=dg_ffn, g_fin=dg_fin, lb=dlb, gnorm=dgn, conv_w=dconv_w)
    return dx, dmod, small


def _ada_fwd(c_all, w_sh, b_sh):
    def body(c_ref, w_ref, b_ref, o_ref):
        cv = c_ref[...]
        ca = (cv * _sigmoid(cv)).astype(bf16)
        o_ref[...] = jnp.dot(ca, w_ref[...].astype(bf16), preferred_element_type=f32) + b_ref[...]

    return pl.pallas_call(body, name="ada_fwd", out_shape=SDS((c_all.shape[0], w_sh.shape[1]), f32),
                          compiler_params=pltpu.CompilerParams(vmem_limit_bytes=V7X_VMEM_LIMIT))(c_all, w_sh, b_sh)


def _ada_wgrad(c_all, dmod_sh):
    def body(c_ref, d_ref, o_ref):
        cv = c_ref[...]
        ca = (cv * _sigmoid(cv)).astype(bf16)
        o_ref[...] = lax.dot_general(ca, d_ref[...].astype(bf16), (TN, ((), ())), preferred_element_type=f32)

    return pl.pallas_call(body, name="ada_wgrad", out_shape=SDS((c_all.shape[1], dmod_sh.shape[1]), f32),
                          compiler_params=pltpu.CompilerParams(vmem_limit_bytes=V7X_VMEM_LIMIT))(c_all, dmod_sh)


def _lb_grad(lb_param, dlb):
    def body(p_ref, d_ref, o_ref):
        p = p_ref[...]
        lb = _sigmoid(p[0:1, :] - p[1:2, :])
        gl = d_ref[...] * lb * (1.0 - lb)
        o_ref[0:1, :] = gl
        o_ref[1:2, :] = -gl

    return pl.pallas_call(body, name="lb_grad", out_shape=SDS(lb_param.shape, f32))(lb_param, dlb)


def _sum_small(gathered):
    def body(g_ref, o_ref):
        acc = g_ref[0]
        for dd in range(1, NDEV):
            acc = acc + g_ref[dd]
        o_ref[...] = acc

    return pl.pallas_call(body, name="sum_small", out_shape=SDS(gathered.shape[1:], f32))(gathered)


def kernel(x, c, w_ada, b_ada, norm_mix_g, w_in, conv_w, lb_param, gnorm_g, w_conv_out, w_hgrn_out, w_o, norm_ffn_g, w_ffn_gate, w_ffn_up, w_ffn_down, norm_final_g, loss_target, m_w_ada, m_b_ada, m_norm_mix_g, m_w_in, m_conv_w, m_lb_param, m_gnorm_g, m_w_conv_out, m_w_hgrn_out, m_w_o, m_norm_ffn_g, m_w_ffn_gate, m_w_ffn_up, m_w_ffn_down, m_norm_final_g, v_w_ada, v_b_ada, v_norm_mix_g, v_w_in, v_conv_w, v_lb_param, v_gnorm_g, v_w_conv_out, v_w_hgrn_out, v_w_o, v_norm_ffn_g, v_w_ffn_gate, v_w_ffn_up, v_w_ffn_down, v_norm_final_g):
    assert lb_param.shape[0] == 2 and w_ada.shape[0] == 1
    s, d = x.shape[1], x.shape[2]
    me = 4 * lax.axis_index("x") + 2 * lax.axis_index("y") + lax.axis_index("c")
    ada_cols = w_ada.shape[2]

    c_all, cw_all = _all_gather("gather_cond", [c, conv_w[0]])
    c_all = c_all.reshape(NDEV, d)
    conv_w_full = jnp.transpose(cw_all, (1, 0, 2)).reshape(conv_w.shape[1], -1)
    b_sh = lax.dynamic_slice_in_dim(b_ada, me * ada_cols, ada_cols, axis=1)
    mod_cols = _ada_fwd(c_all, w_ada[0], b_sh)
    mod_all, = _all_gather("gather_mod", [mod_cols])
    mod = lax.dynamic_index_in_dim(mod_all, me, axis=1, keepdims=False).reshape(N_MOD, 1, d)

    me1 = me.astype(jnp.int32).reshape(1)
    shard_groups = {"in": [w_in[0]], "mix": [w_conv_out[0], w_hgrn_out[0], w_o[0]],
                    "ffn": [w_ffn_gate[0].T, w_ffn_up[0].T, w_ffn_down[0]]}
    own_slot = lambda frm, to: _flat(frm)
    gather_plan = lambda n: [(a, j, own_slot, own_slot) for a in range(n) for j in (1,) + ICI_RELATIONS]
    flat = lambda a: a.reshape(a.shape[0] * a.shape[1], a.shape[2])
    to8 = lambda a: a.reshape(NDEV, a.shape[0] // NDEV, a.shape[1])
    gathering, tokens = {}, []
    for grp, sh in shard_groups.items():
        lands = [_place_shard(f"place_{grp}{i}", me1, a) for i, a in enumerate(sh)]
        ss, rs, _, lands, tok = _push_start("gather_start_" + grp, [], lands, gather_plan(len(sh)),
                                            after=tokens[-1] if tokens else mod_all)
        gathering[grp] = (ss, rs, lands)
        tokens.append(tok)

    def get_w(grp, after):
        ss, rs, lands = gathering[grp]
        _, lands = _push_wait("gather_wait_" + grp, ss, rs, [], lands, gather_plan(len(lands)), after)
        full = _sibling_forward("gather_fwd_" + grp, lands)
        return [f if grp == "in" or i < 2 and grp == "mix" else flat(f) for i, f in enumerate(full)]

    core = lax.axis_index("c").astype(jnp.int32).reshape(1)
    chip = (2 * lax.axis_index("x") + lax.axis_index("y")).astype(jnp.int32).reshape(1)
    scatter_plan = lambda n: [(a, j, lambda frm, to: _chip(to), lambda frm, to: _chip(frm)) for a in range(n) for j in ICI_RELATIONS]
    scattering = {}

    def put_g(grp, grads):
        g8 = [g if g.ndim == 3 else to8(g) for g in grads]
        recv = _sibling_swap("scatter_pair_" + grp, g8)
        pairs = [_pair_sum(f"pair_sum_{grp}{i}", core, g, r, _row_tile(g.shape[1])) for i, (g, r) in enumerate(zip(g8, recv))]
        lands = [lax.empty(p.shape, p.dtype) for p in pairs]
        ss, rs, srcs, lands, tok = _push_start("scatter_start_" + grp, pairs, lands, scatter_plan(len(pairs)))
        scattering[grp] = (ss, rs, srcs, lands)
        return tok

    def reduced(grp, after):
        ss, rs, srcs, lands = scattering[grp]
        srcs, lands = _push_wait("scatter_wait_" + grp, ss, rs, srcs, lands, scatter_plan(len(srcs)), after)
        return [_chip_sum(f"chip_sum_{grp}{i}", chip, p, r, _row_tile(p.shape[1])) for i, (p, r) in enumerate(zip(srcs, lands))]

    dx, dmod, small = _local_step(x[0], loss_target[0], mod, _after_tokens(norm_mix_g, tokens), norm_ffn_g,
                                  norm_final_g.reshape(1, d), lb_param, gnorm_g, conv_w_full, get_w, put_g)

    pieces = [*dmod, small["g_mix"], small["g_ffn"], small["g_fin"], small["lb"], small["gnorm"], small["loss"],
              small["conv_w"].reshape(1, -1)]
    widths = [p.shape[1] for p in pieces]
    offs = np.concatenate([[0], np.cumsum(widths)])
    packed = jnp.concatenate(pieces, axis=1)
    gathered, = _all_gather("gather_small", [packed])
    summed = _sum_small(gathered)
    part = lambda i: summed[:, offs[i]:offs[i + 1]]
    g_b_ada = summed[:, :N_MOD * d]
    g_norm_mix, g_norm_ffn, g_norm_fin, g_lb_row, g_gnorm, loss_vec, g_convw_flat = [part(i) for i in range(N_MOD, N_MOD + 7)]
    loss = loss_vec[0, 0]
    dmod_all = gathered[:, 0, :N_MOD * d]
    g_w_ada = _ada_wgrad(c_all, lax.dynamic_slice_in_dim(dmod_all, me * ada_cols, ada_cols, axis=1))
    g_lb = _lb_grad(lb_param, g_lb_row)
    cw_cols = conv_w.shape[2]
    g_conv_w = lax.dynamic_slice_in_dim(g_convw_flat.reshape(conv_w.shape[1], -1), me * cw_cols, cw_cols, axis=1)

    grads = dict(w_ada=g_w_ada, b_ada=g_b_ada, norm_mix_g=g_norm_mix, conv_w=g_conv_w, lb_param=g_lb, gnorm_g=g_gnorm,
                 norm_ffn_g=g_norm_ffn, norm_final_g=g_norm_fin)
    weights = dict(w_ada=(w_ada, m_w_ada, v_w_ada), b_ada=(b_ada, m_b_ada, v_b_ada), norm_mix_g=(norm_mix_g, m_norm_mix_g, v_norm_mix_g),
                   w_in=(w_in, m_w_in, v_w_in), conv_w=(conv_w, m_conv_w, v_conv_w), lb_param=(lb_param, m_lb_param, v_lb_param),
                   gnorm_g=(gnorm_g, m_gnorm_g, v_gnorm_g), w_conv_out=(w_conv_out, m_w_conv_out, v_w_conv_out),
                   w_hgrn_out=(w_hgrn_out, m_w_hgrn_out, v_w_hgrn_out), w_o=(w_o, m_w_o, v_w_o),
                   norm_ffn_g=(norm_ffn_g, m_norm_ffn_g, v_norm_ffn_g), w_ffn_gate=(w_ffn_gate, m_w_ffn_gate, v_w_ffn_gate),
                   w_ffn_up=(w_ffn_up, m_w_ffn_up, v_w_ffn_up), w_ffn_down=(w_ffn_down, m_w_ffn_down, v_w_ffn_down),
                   norm_final_g=(norm_final_g, m_norm_final_g, v_norm_final_g))
    res = {}

    def update(nm):
        w, m, v = weights[nm]
        shape2 = (w.shape[-2], w.shape[-1]) if w.ndim >= 2 else (1, w.shape[0])
        g2 = grads[nm].reshape(shape2)
        dl, mn, vn = _adamw("adamw_" + nm, g2, w.reshape(shape2), m.reshape(shape2), v.reshape(shape2))
        res[nm] = [a.reshape(w.shape) for a in (g2, dl, mn, vn)]

    for nm in list(grads):
        update(nm)
    g_w_gt, g_w_ut, grads["w_ffn_down"] = reduced("ffn", res["w_ada"][1])
    grads["w_ffn_gate"], grads["w_ffn_up"] = g_w_gt.T, g_w_ut.T
    for nm in ("w_ffn_gate", "w_ffn_up", "w_ffn_down"):
        update(nm)
    grads["w_conv_out"], grads["w_hgrn_out"], grads["w_o"] = reduced("mix", res["w_ffn_down"][1])
    for nm in ("w_conv_out", "w_hgrn_out", "w_o"):
        update(nm)
    grads["w_in"], = reduced("in", res["w_o"][1])
    update("w_in")
    outs = [[res[nm][i] for nm in weights] for i in range(4)]
    return (loss, dx.reshape(x.shape), *outs[0], *outs[1], *outs[2], *outs[3])
```

```python
import functools

import numpy as np
import jax
import jax.numpy as jnp
from jax import lax
from jax.experimental import pallas as pl
from jax.experimental.pallas import tpu as pltpu

f32, bf16 = jnp.float32, jnp.bfloat16
SDS = jax.ShapeDtypeStruct

EPS = 1e-6
HEADS, DK, CHUNK = 8, 128, 64
N_MOD = 6
NDEV = 8
ADAM_LR, ADAM_B1, ADAM_B2, ADAM_EPS, ADAM_WD, ADAM_STEP = 0.001, 0.9, 0.999, 1e-08, 0.01, 10
LEVELS = (32, 16, 8, 4, 2, 1)
V7X_VMEM_LIMIT = 56 * 1024 * 1024
HBM = pl.BlockSpec(memory_space=pltpu.HBM)
MESH = pl.DeviceIdType.MESH


def _params(sem):
    return pltpu.CompilerParams(dimension_semantics=sem, vmem_limit_bytes=V7X_VMEM_LIMIT)


def _sigmoid(x):
    return jax.nn.sigmoid(x)


def _dsilu(x, s):
    return s * (1.0 + x * (1.0 - s))


def _mesh_pos():
    x, y, c = lax.axis_index("x"), lax.axis_index("y"), lax.axis_index("c")
    return x, y, c


def _peer(pos, j):
    x, y, c = pos
    return (1 - x if j & 4 else x, 1 - y if j & 2 else y, 1 - c if j & 1 else c)


def _flat(pos):
    return 4 * pos[0] + 2 * pos[1] + pos[2]


def _all_gather(name, arrs, after=()):
    n, ne = len(arrs), len(after)
    out_shapes = [SDS((NDEV,) + a.shape, a.dtype) for a in arrs]

    def body(*refs):
        ins, outs = refs[:n], refs[n + ne:2 * n + ne]
        send_sems, recv_sems, local_sems = refs[2 * n + ne:]
        pos = _mesh_pos()
        me = _flat(pos)

        def copy(a, j, frm, to_pos):
            k = a * (NDEV - 1) + j - 1
            return pltpu.make_async_remote_copy(src_ref=ins[a], dst_ref=outs[a].at[frm], send_sem=send_sems.at[k],
                                                recv_sem=recv_sems.at[k], device_id=to_pos, device_id_type=MESH)

        local = [pltpu.make_async_copy(ins[a], outs[a].at[me], local_sems.at[a]) for a in range(n)]
        for cp in local:
            cp.start()
        sends = [copy(a, j, me, _peer(pos, j)) for j in range(1, NDEV) for a in range(n)]
        for cp in sends:
            cp.start()
        for j in range(1, NDEV):
            for a in range(n):
                copy(a, j, _flat(_peer(pos, j)), pos).wait_recv()
        for cp in sends:
            cp.wait_send()
        for cp in local:
            cp.wait()

    return pl.pallas_call(
        body, name=name, out_shape=out_shapes, in_specs=[HBM] * n + [ANY] * ne, out_specs=[HBM] * n,
        scratch_shapes=[pltpu.SemaphoreType.DMA((n * (NDEV - 1),)), pltpu.SemaphoreType.DMA((n * (NDEV - 1),)),
                        pltpu.SemaphoreType.DMA((n,))],
    )(*arrs, *after)


SEM = pl.BlockSpec(memory_space=pltpu.SEMAPHORE)
ANY = pl.BlockSpec(memory_space=pl.ANY)
EFFECT = pltpu.SideEffectType.DATAFLOW_SIDE_EFFECTING
ICI_RELATIONS = (2, 4, 6)


def _chip(pos):
    return 2 * pos[0] + pos[1]


def _hbm(a):
    return pltpu.with_memory_space_constraint(a, pltpu.HBM)


def _plan_copy(plan_entry, k, pos, frm, to, src_refs, land_refs, send_sems, recv_sems):
    a, _, src_slot, dst_slot = plan_entry
    s = src_refs[a] if src_slot is None else src_refs[a].at[src_slot(frm, to)]
    return pltpu.make_async_remote_copy(src_ref=s, dst_ref=land_refs[a].at[dst_slot(frm, to)], send_sem=send_sems.at[k],
                                        recv_sem=recv_sems.at[k], device_id=to, device_id_type=MESH)


def _push_start(name, srcs, lands, plan, after=None):
    ns, nb, nk = len(srcs), len(srcs) + len(lands), len(plan)
    extra = [] if after is None else [after]

    def body(*refs):
        land_refs = refs[ns:nb]
        src_refs = refs[:ns] if ns else land_refs
        send_sems, recv_sems = refs[nb + len(extra)], refs[nb + len(extra) + 1]
        pos = _mesh_pos()
        for k, e in enumerate(plan):
            _plan_copy(e, k, pos, pos, _peer(pos, e[1]), src_refs, land_refs, send_sems, recv_sems).start()
        refs[-1][...] = jnp.zeros_like(refs[-1])

    outs = pl.pallas_call(
        body, name=name,
        out_shape=[pltpu.SemaphoreType.DMA((nk,)), pltpu.SemaphoreType.DMA((nk,))] + [pltpu.HBM(a.shape, a.dtype) for a in srcs + lands]
        + [SDS((8, 128), f32)],
        in_specs=[HBM] * nb + [ANY] * len(extra), out_specs=[SEM, SEM] + [HBM] * nb + [pl.BlockSpec(memory_space=pltpu.VMEM)],
        input_output_aliases={i: 2 + i for i in range(nb)},
        compiler_params=pltpu.CompilerParams(has_side_effects=EFFECT),
    )(*[_hbm(a) for a in srcs + lands], *extra)
    return outs[0], outs[1], list(outs[2:2 + ns]), list(outs[2 + ns:2 + nb]), outs[-1]


def _push_wait(name, send_sems, recv_sems, srcs, lands, plan, after):
    ns, nb = len(srcs), len(srcs) + len(lands)

    def body(*refs):
        land_refs = refs[ns:nb]
        src_refs = refs[:ns] if ns else land_refs
        ssem, rsem = refs[nb], refs[nb + 1]
        pos = _mesh_pos()
        for k, e in enumerate(plan):
            peer = _peer(pos, e[1])
            _plan_copy(e, k, pos, pos, peer, src_refs, land_refs, ssem, rsem).wait_send()
            _plan_copy(e, k, pos, peer, pos, src_refs, land_refs, ssem, rsem).wait_recv()

    outs = pl.pallas_call(
        body, name=name, out_shape=[pltpu.HBM(a.shape, a.dtype) for a in srcs + lands],
        in_specs=[HBM] * nb + [SEM, SEM, ANY], out_specs=[HBM] * nb,
        input_output_aliases={i: i for i in range(nb)},
        compiler_params=pltpu.CompilerParams(has_side_effects=EFFECT),
    )(*srcs, *lands, send_sems, recv_sems, after)
    return list(outs[:ns]), list(outs[ns:])


def _after_tokens(small, tokens):
    for t in tokens:
        if t is not None:
            small = small + t[0:1, 0:1].reshape((1,) * small.ndim)
    return small


def _place_shard(name, me, shard, after=None):
    r, c = shard.shape
    tr = _row_tile(r)
    extra = [] if after is None else [after]

    def body(me_ref, s_ref, *rest):
        rest[-1][...] = s_ref[...].astype(bf16)

    return pl.pallas_call(
        body, name=name, out_shape=SDS((NDEV, r, c), bf16),
        grid_spec=pltpu.PrefetchScalarGridSpec(
            num_scalar_prefetch=1, grid=(r // tr,), in_specs=[pl.BlockSpec((tr, c), lambda i, me_ref: (i, 0))] + [ANY] * len(extra),
            out_specs=pl.BlockSpec((None, tr, c), lambda i, me_ref: (me_ref[0], i, 0))),
        compiler_params=_params(("parallel",)),
    )(me, shard, *extra)


def _row_tile(r):
    return max(t for t in range(16, 257, 16) if r % t == 0)


def _sibling_forward(name, lands):
    n = len(lands)
    nk = n * len(ICI_RELATIONS)

    def body(*refs):
        land_refs = refs[:n]
        send_sems, recv_sems = refs[2 * n:]
        pos = _mesh_pos()
        sib = _peer(pos, 1)

        def copy(a, i, frm, to):
            k = a * len(ICI_RELATIONS) + i
            slot = _flat(_peer(frm, ICI_RELATIONS[i]))
            return pltpu.make_async_remote_copy(src_ref=land_refs[a].at[slot], dst_ref=land_refs[a].at[slot], send_sem=send_sems.at[k],
                                                recv_sem=recv_sems.at[k], device_id=to, device_id_type=MESH)

        sends = [copy(a, i, pos, sib) for a in range(n) for i in range(len(ICI_RELATIONS))]
        for cp in sends:
            cp.start()
        for a in range(n):
            for i in range(len(ICI_RELATIONS)):
                copy(a, i, sib, pos).wait_recv()
        for cp in sends:
            cp.wait_send()

    outs = pl.pallas_call(
        body, name=name, out_shape=[SDS(a.shape, a.dtype) for a in lands], in_specs=[HBM] * n, out_specs=[HBM] * n,
        input_output_aliases={i: i for i in range(n)},
        scratch_shapes=[pltpu.SemaphoreType.DMA((nk,)), pltpu.SemaphoreType.DMA((nk,))],
    )(*lands)
    return list(outs)


def _sibling_swap(name, grads):
    n = len(grads)
    nchip = NDEV // 2

    def body(*refs):
        g_refs, out_refs = refs[:n], refs[n:2 * n]
        send_sems, recv_sems = refs[2 * n:]
        pos = _mesh_pos()
        sib = _peer(pos, 1)
        sends = []
        for a in range(n):
            for q in range(nchip):
                k = a * nchip + q
                sends.append(pltpu.make_async_remote_copy(src_ref=g_refs[a].at[2 * q + sib[2]], dst_ref=out_refs[a].at[q],
                                                          send_sem=send_sems.at[k], recv_sem=recv_sems.at[k], device_id=sib,
                                                          device_id_type=MESH))
        for cp in sends:
            cp.start()
        for cp in sends:
            cp.wait()

    return pl.pallas_call(
        body, name=name, out_shape=[SDS((nchip,) + a.shape[1:], a.dtype) for a in grads], in_specs=[HBM] * n, out_specs=[HBM] * n,
        scratch_shapes=[pltpu.SemaphoreType.DMA((n * nchip,)), pltpu.SemaphoreType.DMA((n * nchip,))],
    )(*grads)


def _pair_sum(name, core, grad, recv, tr):
    _, r, c = grad.shape
    nchip = NDEV // 2

    def body(core_ref, g_ref, r_ref, o_ref):
        o_ref[...] = (g_ref[...].astype(f32) + r_ref[...].astype(f32)).astype(o_ref.dtype)

    return pl.pallas_call(
        body, name=name, out_shape=SDS((nchip, r, c), grad.dtype),
        grid_spec=pltpu.PrefetchScalarGridSpec(
            num_scalar_prefetch=1, grid=(nchip, r // tr),
            in_specs=[pl.BlockSpec((None, tr, c), lambda q, i, core_ref: (2 * q + core_ref[0], i, 0)),
                      pl.BlockSpec((None, tr, c), lambda q, i, core_ref: (q, i, 0))],
            out_specs=pl.BlockSpec((None, tr, c), lambda q, i, core_ref: (q, i, 0))),
        compiler_params=_params(("parallel", "parallel")),
    )(core, grad, recv)


def _chip_sum(name, chip, pairs, recv, tr):
    nchip, r, c = pairs.shape

    def body(chip_ref, p_ref, r_ref, o_ref):
        mine = chip_ref[0]
        acc = jnp.zeros(o_ref.shape, f32)
        for q in range(nchip):
            acc = acc + jnp.where(mine == q, p_ref[q].astype(f32), r_ref[q].astype(f32))
        o_ref[...] = acc

    return pl.pallas_call(
        body, name=name, out_shape=SDS((r, c), f32),
        grid_spec=pltpu.PrefetchScalarGridSpec(
            num_scalar_prefetch=1, grid=(r // tr,),
            in_specs=[pl.BlockSpec((nchip, tr, c), lambda i, chip_ref: (0, i, 0))] * 2,
            out_specs=pl.BlockSpec((tr, c), lambda i, chip_ref: (i, 0))),
        compiler_params=_params(("parallel",)),
    )(chip, pairs, recv)


def _matmul(name, a, b, out_shape, out_dtype, grid, a_spec, b_spec, o_spec, dims, after=None):
    ksteps = grid[2]
    acc_shape = tuple(d for d in o_spec.block_shape if d is not None)
    extra = [] if after is None else [after]

    def body(a_ref, b_ref, *rest):
        o_ref, acc = rest[len(extra)], rest[len(extra) + 1:]
        prod = lax.dot_general(a_ref[...], b_ref[...], (dims, ((), ())), preferred_element_type=f32)
        if ksteps == 1:
            o_ref[...] = prod.astype(o_ref.dtype)
        else:
            k = pl.program_id(2)

            @pl.when(k == 0)
            def _():
                acc[0][...] = prod

            @pl.when(k > 0)
            def _():
                acc[0][...] += prod

            @pl.when(k == ksteps - 1)
            def _():
                o_ref[...] = acc[0][...].astype(o_ref.dtype)

    return pl.pallas_call(
        body, name=name, grid=grid, in_specs=[a_spec, b_spec] + [ANY] * len(extra), out_specs=o_spec,
        out_shape=SDS(out_shape, out_dtype), scratch_shapes=[] if ksteps == 1 else [pltpu.VMEM(acc_shape, f32)],
        compiler_params=_params(("parallel", "parallel", "arbitrary")),
    )(a, b, *extra)


NN, NT, TN = ((1,), (0,)), ((1,), (1,)), ((0,), (0,))


def _rowwise(name, fn, n_rows, tm, rows, fulls, row_outs, acc_outs=(), ncol=1):
    assert ncol == 1 or not acc_outs
    nr, nf, no, na = len(rows), len(fulls), len(row_outs), len(acc_outs)
    in_specs = [pl.BlockSpec((tm, w), functools.partial(lambda i, j, cb: (i, cb + j), cb=cb)) for (_, w, cb) in rows]
    in_specs += [pl.BlockSpec(a.shape, functools.partial(lambda i, j, nd: (0,) * nd, nd=a.ndim)) for a in fulls]
    out_shape = [SDS((n_rows, w), dt) for (w, dt) in row_outs] + [SDS(s, f32) for s in acc_outs]
    out_specs = [pl.BlockSpec((tm, w // ncol), lambda i, j: (i, j)) for (w, _) in row_outs]
    out_specs += [pl.BlockSpec(s, functools.partial(lambda i, j, nd: (0,) * nd, nd=len(s))) for s in acc_outs]

    def body(*refs):
        if na:
            @pl.when(pl.program_id(0) == 0)
            def _():
                for r in refs[nr + nf + no:]:
                    r[...] = jnp.zeros(r.shape, r.dtype)
        fn(refs[:nr], refs[nr:nr + nf], refs[nr + nf:nr + nf + no], refs[nr + nf + no:])

    return pl.pallas_call(
        body, name=name, grid=(n_rows // tm, ncol), in_specs=in_specs, out_specs=out_specs, out_shape=out_shape,
        compiler_params=_params(("arbitrary" if na else "parallel", "arbitrary" if na else "parallel")),
    )(*[r[0] for r in rows], *fulls)


def _rms(x):
    r = lax.rsqrt(jnp.mean(x * x, axis=-1, keepdims=True) + EPS)
    return r, x * r


def _colsum(v):
    return jnp.sum(v, axis=0, keepdims=True)


def _shift_down(u, row, k):
    return jnp.where(row >= k, pltpu.roll(u, k, 0), 0.0)


def _shift_up(u, row, k):
    n = u.shape[0]
    return jnp.where(row < n - k, pltpu.roll(u, n - k, 0), 0.0)


def _conv_fwd(proj, conv_w, s, dc):
    nb = dc // 128

    def body(ab_ref, ac_ref, ax_ref, w_ref, z_ref):
        u = ac_ref[...] * ax_ref[...]
        row = lax.broadcasted_iota(jnp.int32, u.shape, 0)
        w = w_ref[...]
        cv = w[0:1] * _shift_down(u, row, 2) + w[1:2] * _shift_down(u, row, 1) + w[2:3] * u
        z_ref[...] = (ab_ref[...] * cv).astype(z_ref.dtype)

    col = lambda off: pl.BlockSpec((s, 128), functools.partial(lambda j, off: (0, off + j), off=off))
    return pl.pallas_call(
        body, name="conv_fwd", grid=(nb,), in_specs=[col(0), col(nb), col(2 * nb), pl.BlockSpec((3, 128), lambda j: (0, j))],
        out_specs=pl.BlockSpec((s, 128), lambda j: (0, j)), out_shape=SDS((s, dc), bf16), compiler_params=_params(("parallel",)),
    )(proj, proj, proj, conv_w)


def _conv_bwd(proj, conv_w, dz, s, dc):
    nb = dc // 128

    def body(ab_ref, ac_ref, ax_ref, w_ref, dz_ref, dab_ref, dac_ref, dax_ref, dw_ref):
        ab, ac, ax, dzv = ab_ref[...], ac_ref[...], ax_ref[...], dz_ref[...]
        u = ac * ax
        row = lax.broadcasted_iota(jnp.int32, u.shape, 0)
        w = w_ref[...]
        u1, u2 = _shift_down(u, row, 1), _shift_down(u, row, 2)
        cv = w[0:1] * u2 + w[1:2] * u1 + w[2:3] * u
        dcv = dzv * ab
        dab_ref[...] = (dzv * cv).astype(dab_ref.dtype)
        du = w[2:3] * dcv + w[1:2] * _shift_up(dcv, row, 1) + w[0:1] * _shift_up(dcv, row, 2)
        dac_ref[...] = (du * ax).astype(dac_ref.dtype)
        dax_ref[...] = (du * ac).astype(dax_ref.dtype)
        dw_ref[0:1, :] = _colsum(dcv * u2)
        dw_ref[1:2, :] = _colsum(dcv * u1)
        dw_ref[2:3, :] = _colsum(dcv * u)

    col = lambda off: pl.BlockSpec((s, 128), functools.partial(lambda j, off: (0, off + j), off=off))
    blk = pl.BlockSpec((s, 128), lambda j: (0, j))
    return pl.pallas_call(
        body, name="conv_bwd", grid=(nb,),
        in_specs=[col(0), col(nb), col(2 * nb), pl.BlockSpec((3, 128), lambda j: (0, j)), blk],
        out_specs=[blk, blk, blk, pl.BlockSpec((3, 128), lambda j: (0, j))],
        out_shape=[SDS((s, dc), bf16)] * 3 + [SDS((3, dc), f32)], compiler_params=_params(("parallel",)),
    )(proj, proj, proj, conv_w, dz)


def _level_masks():
    t = np.arange(CHUNK)[:, None]
    s = np.arange(CHUNK)[None, :]
    m = np.stack([((t & h) != 0) & ((s & h) == 0) & (t // (2 * h) == s // (2 * h)) for h in LEVELS]).astype(np.float32)
    return jnp.asarray(m), jnp.asarray(m.transpose(0, 2, 1))


def _cumsum_rows(x, row):
    for sh in (1, 2, 4, 8, 16, 32):
        x = x + jnp.where(row >= sh, pltpu.roll(x, sh, 0), 0.0)
    return x


def _rev_cumsum_rows(x, row):
    n = x.shape[0]
    for sh in (1, 2, 4, 8, 16, 32):
        x = x + jnp.where(row < n - sh, pltpu.roll(x, n - sh, 0), 0.0)
    return x


def _chunk_terms(qp, fl, lb):
    row = lax.broadcasted_iota(jnp.int32, qp.shape, 0)
    sig = _sigmoid(fl)
    f = lb + (1.0 - lb) * sig
    k = 1.0 - f
    sq = _sigmoid(qp)
    qh = qp * sq
    b = _cumsum_rows(jnp.log(f), row)
    sub = lax.broadcasted_iota(jnp.int32, (CHUNK // 8, 8, DK), 1)
    b8 = b.reshape(CHUNK // 8, 8, DK)
    us, exs, ups = [], [], []
    for m in LEVELS:
        sb = 2 * m
        if sb >= 8:
            b3 = b.reshape(CHUNK // sb, sb, DK)
            bref = jnp.broadcast_to(b3[:, m - 1:m, :], b3.shape).reshape(CHUNK, DK)
        else:
            bref8 = None
            for j in range(8 // sb):
                cand = jnp.broadcast_to(b8[:, j * sb + m - 1:j * sb + m, :], b8.shape)
                bref8 = cand if bref8 is None else jnp.where(sub >= j * sb, cand, bref8)
            bref = bref8.reshape(CHUNK, DK)
        up = (row & m) != 0
        ex = jnp.exp(jnp.where(up, b - bref, bref - b))
        us.append((jnp.where(up, qh, k) * ex).astype(bf16))
        exs.append(ex)
        ups.append(up)
    blast = b[CHUNK - 1:CHUNK, :]
    eb, ebl = jnp.exp(b), jnp.exp(blast - b)
    return dict(sig=sig, f=f, k=k, sq=sq, qh=qh, u=jnp.stack(us), ex=exs, up=ups, eb=eb, ebl=ebl, qt=qh * eb, kt=k * ebl,
                el=jnp.exp(blast), row=row)


def _scores(t, mask):
    pl_ = jnp.einsum("ltk,lsk->lts", t["u"], t["u"], preferred_element_type=f32)
    p = jnp.sum(pl_ * mask, axis=0)
    r = lax.broadcasted_iota(jnp.int32, (CHUNK, CHUNK), 0)
    c = lax.broadcasted_iota(jnp.int32, (CHUNK, CHUNK), 1)
    diag = jnp.sum(t["qh"] * t["k"], axis=-1, keepdims=True)
    return p + jnp.where(r == c, diag, 0.0)


def _hgrn_fwd(proj, lb_param, gnorm, s, dv_total, tb):
    nchunk = tb // CHUNK
    masks, _ = _level_masks()
    q0, f0, v0, g0 = 3 * HEADS, 4 * HEADS, 5 * HEADS, 6 * HEADS

    def body(q_ref, f_ref, v_ref, g_ref, lb_ref, gn_ref, mask_ref, og_ref, o_ref, st_ref, state):
        @pl.when(pl.program_id(1) == 0)
        def _():
            state[...] = jnp.zeros_like(state)

        lbp = lb_ref[...]
        lb = _sigmoid(lbp[0:1, :] - lbp[1:2, :])
        mask = mask_ref[...]
        for i in range(nchunk):
            rs = pl.ds(i * CHUNK, CHUNK)
            t = _chunk_terms(q_ref[rs, :], f_ref[rs, :], lb)
            v = v_ref[rs, :]
            vb = v.astype(bf16)
            st = state[...]
            st_ref[i] = st
            p = _scores(t, mask)
            o = jnp.dot(p.astype(bf16), vb, preferred_element_type=f32)
            o += lax.dot_general(t["qt"].astype(bf16), st.astype(bf16), (NT, ((), ())), preferred_element_type=f32)
            state[...] = st * t["el"] + lax.dot_general(vb, t["kt"].astype(bf16), (TN, ((), ())), preferred_element_type=f32)
            o_ref[rs, :] = o
            r, oh = _rms(o)
            g = g_ref[rs, :]
            og_ref[rs, :] = (oh * gn_ref[...] * (g * _sigmoid(g))).astype(og_ref.dtype)

    col = lambda off: pl.BlockSpec((tb, DK), functools.partial(lambda h, t, off: (t, off + h), off=off))
    blk = pl.BlockSpec((tb, DK), lambda h, t: (t, h))
    return pl.pallas_call(
        body, name="hgrn_fwd", grid=(HEADS, s // tb),
        in_specs=[col(q0), col(f0), col(v0), col(g0), pl.BlockSpec((2, DK), lambda h, t: (0, h)),
                  pl.BlockSpec((1, DK), lambda h, t: (0, 0)), pl.BlockSpec(masks.shape, lambda h, t: (0, 0, 0))],
        out_specs=[blk, blk, pl.BlockSpec((nchunk, None, DK, DK), lambda h, t: (t, h, 0, 0))],
        out_shape=[SDS((s, dv_total), bf16), SDS((s, dv_total), f32), SDS((s // CHUNK, HEADS, DK, DK), f32)],
        scratch_shapes=[pltpu.VMEM((DK, DK), f32)], compiler_params=_params(("parallel", "arbitrary")),
    )(proj, proj, proj, proj, lb_param, gnorm, masks)


def _hgrn_bwd(proj, lb_param, gnorm, o_saved, states, dog, s, dv_total, tb):
    nchunk = tb // CHUNK
    nt = s // tb
    nc_total = s // CHUNK
    masks, masks_t = _level_masks()
    q0, f0, v0, g0 = 3 * HEADS, 4 * HEADS, 5 * HEADS, 6 * HEADS

    def body(q_ref, f_ref, v_ref, g_ref, lb_ref, gn_ref, mask_ref, maskt_ref, o_ref, dog_ref, st_ref, stn_ref,
             dq_ref, df_ref, dv_ref, dg_ref, dlb_ref, dgn_ref, gstate):
        h_id, t_id = pl.program_id(0), pl.program_id(1)

        @pl.when(t_id == 0)
        def _():
            gstate[...] = jnp.zeros_like(gstate)
            dlb_ref[...] = jnp.zeros_like(dlb_ref)

        @pl.when((t_id == 0) & (h_id == 0))
        def _():
            dgn_ref[...] = jnp.zeros_like(dgn_ref)

        lbp = lb_ref[...]
        lb = _sigmoid(lbp[0:1, :] - lbp[1:2, :])
        mask, maskt = mask_ref[...], maskt_ref[...]
        gn = gn_ref[...]
        for i in reversed(range(nchunk)):
            rs = pl.ds(i * CHUNK, CHUNK)
            qp, fl, v, g = q_ref[rs, :], f_ref[rs, :], v_ref[rs, :], g_ref[rs, :]
            t = _chunk_terms(qp, fl, lb)
            vb = v.astype(bf16)
            st0 = st_ref[i]
            st1 = st_ref[i + 1] if i + 1 < nchunk else stn_ref[0]
            gt = gstate[...]
            o = o_ref[rs, :]
            r, oh = _rms(o)
            sg = _sigmoid(g)
            dog_v = dog_ref[rs, :]
            dg_ref[rs, :] = (dog_v * (oh * gn) * _dsilu(g, sg)).astype(dg_ref.dtype)
            don = dog_v * (g * sg)
            dgn_ref[...] += _colsum(don * oh)
            doh = don * gn
            do = r * (doh - oh * jnp.mean(doh * oh, axis=-1, keepdims=True))
            dob = do.astype(bf16)
            d = lax.dot_general(dob, vb, (NT, ((), ())), preferred_element_type=f32)
            dt = lax.dot_general(vb, dob, (NT, ((), ())), preferred_element_type=f32)
            z = (mask * d[None] + maskt * dt[None]).astype(bf16)
            rr = jnp.einsum("lts,lsk->ltk", z, t["u"], preferred_element_type=f32)
            dq = jnp.zeros((CHUNK, DK), f32)
            dk = jnp.zeros((CHUNK, DK), f32)
            qdk = jnp.zeros((CHUNK, DK), f32)
            for li in range(len(LEVELS)):
                du = t["ex"][li] * rr[li]
                dq += jnp.where(t["up"][li], du, 0.0)
                dk += jnp.where(t["up"][li], 0.0, du)
                e = t["u"][li].astype(f32) * rr[li]
                qdk += jnp.where(t["up"][li], e, -e)
            dd = jnp.sum(do * v, axis=-1, keepdims=True)
            dq += dd * t["k"]
            dk += dd * t["qh"]
            gtb = gt.astype(bf16)
            ktb, qtb = t["kt"].astype(bf16), t["qt"].astype(bf16)
            dq_in = jnp.dot(dob, st0.astype(bf16), preferred_element_type=f32)
            dk_in = jnp.dot(vb, gtb, preferred_element_type=f32)
            dq += t["eb"] * dq_in
            dk += t["ebl"] * dk_in
            qdk += qtb.astype(f32) * dq_in - ktb.astype(f32) * dk_in
            p = _scores(t, mask)
            dvv = lax.dot_general(p.astype(bf16), dob, (TN, ((), ())), preferred_element_type=f32)
            dvv += lax.dot_general(ktb, gtb, (NT, ((), ())), preferred_element_type=f32)
            dv_ref[rs, :] = dvv.astype(dv_ref.dtype)
            a_end = _colsum(gtb.astype(f32) * st1)
            dlf = _rev_cumsum_rows(qdk, t["row"]) + a_end
            dfv = dlf / t["f"] - dk
            df_ref[rs, :] = (dfv * (1.0 - lb) * t["sig"] * (1.0 - t["sig"])).astype(df_ref.dtype)
            dlb_ref[...] += _colsum(dfv * (1.0 - t["sig"]))
            dq_ref[rs, :] = (dq * _dsilu(qp, t["sq"])).astype(dq_ref.dtype)
            gstate[...] = gt * t["el"] + lax.dot_general(dob, qtb, (TN, ((), ())), preferred_element_type=f32)

    rev = lambda t: nt - 1 - t
    col = lambda off: pl.BlockSpec((tb, DK), functools.partial(lambda h, t, off: (rev(t), off + h), off=off))
    blk = pl.BlockSpec((tb, DK), lambda h, t: (rev(t), h))
    nxt = lambda h, t: (jnp.minimum((rev(t) + 1) * nchunk, nc_total - 1), h, 0, 0)
    return pl.pallas_call(
        body, name="hgrn_bwd", grid=(HEADS, nt),
        in_specs=[col(q0), col(f0), col(v0), col(g0), pl.BlockSpec((2, DK), lambda h, t: (0, h)),
                  pl.BlockSpec((1, DK), lambda h, t: (0, 0)), pl.BlockSpec(masks.shape, lambda h, t: (0, 0, 0)),
                  pl.BlockSpec(masks.shape, lambda h, t: (0, 0, 0)), blk, blk,
                  pl.BlockSpec((nchunk, None, DK, DK), lambda h, t: (rev(t), h, 0, 0)),
                  pl.BlockSpec((1, None, DK, DK), nxt)],
        out_specs=[blk, blk, blk, blk, pl.BlockSpec((1, DK), lambda h, t: (0, h)), pl.BlockSpec((1, DK), lambda h, t: (0, 0))],
        out_shape=[SDS((s, dv_total), bf16)] * 4 + [SDS((1, HEADS * DK), f32), SDS((1, DK), f32)],
        scratch_shapes=[pltpu.VMEM((DK, DK), f32)], compiler_params=_params(("arbitrary", "arbitrary")),
    )(proj, proj, proj, proj, lb_param, gnorm, masks, masks_t, o_saved, dog, states, states)


def _adamw(name, g, w, m, v):
    r, c = w.shape
    tr = r
    for cand in (256, 128, 64, 32, 16, 8):
        if r % cand == 0 and r > cand:
            tr = cand
            break

    def body(g_ref, w_ref, m_ref, v_ref, d_ref, mo_ref, vo_ref):
        gv = g_ref[...]
        mn = ADAM_B1 * m_ref[...] + (1.0 - ADAM_B1) * gv
        vn = ADAM_B2 * v_ref[...] + (1.0 - ADAM_B2) * jnp.square(gv)
        m_hat = mn / (1.0 - ADAM_B1 ** ADAM_STEP)
        v_hat = vn / (1.0 - ADAM_B2 ** ADAM_STEP)
        d_ref[...] = -ADAM_LR * (m_hat / (jnp.sqrt(v_hat) + ADAM_EPS) + ADAM_WD * w_ref[...])
        mo_ref[...] = mn
        vo_ref[...] = vn

    blk = pl.BlockSpec((tr, c), lambda i: (i, 0))
    return pl.pallas_call(
        body, name=name, grid=(r // tr,), in_specs=[blk] * 4, out_specs=[blk] * 3, out_shape=[SDS((r, c), f32)] * 3,
        compiler_params=_params(("parallel",)),
    )(g, w, m, v)


def _ffn_in(h2, w_gt, w_ut, tm, ffb):
    s, d = h2.shape
    dff = w_gt.shape[0]

    def body(a_ref, wg_ref, wu_ref, g_ref, u_ref, act_ref):
        a = a_ref[...]
        g = lax.dot_general(a, wg_ref[...], (NT, ((), ())), preferred_element_type=f32)
        u = lax.dot_general(a, wu_ref[...], (NT, ((), ())), preferred_element_type=f32)
        g_ref[...] = g.astype(bf16)
        u_ref[...] = u.astype(bf16)
        act_ref[...] = (g * _sigmoid(g) * u).astype(bf16)

    w_spec = pl.BlockSpec((ffb, d), lambda j, i: (j, 0))
    o_spec = pl.BlockSpec((tm, ffb), lambda j, i: (i, j))
    return pl.pallas_call(
        body, name="ffn_in", grid=(dff // ffb, s // tm), in_specs=[pl.BlockSpec((tm, d), lambda j, i: (i, 0)), w_spec, w_spec],
        out_specs=[o_spec] * 3, out_shape=[SDS((s, dff), bf16)] * 3, compiler_params=_params(("parallel", "parallel")),
    )(h2, w_gt, w_ut)


def _ffn_down_bwd(dff_out, w_d, gg, uu, tm, ffb):
    s, d = dff_out.shape
    dff = w_d.shape[0]

    def body(a_ref, w_ref, g_ref, u_ref, dg_ref, du_ref):
        da = lax.dot_general(a_ref[...], w_ref[...], (NT, ((), ())), preferred_element_type=f32)
        g, u = g_ref[...].astype(f32), u_ref[...].astype(f32)
        sg = _sigmoid(g)
        dg_ref[...] = (da * u * _dsilu(g, sg)).astype(bf16)
        du_ref[...] = (da * g * sg).astype(bf16)

    t_spec = pl.BlockSpec((tm, ffb), lambda j, i: (i, j))
    return pl.pallas_call(
        body, name="d_ffn_down_in", grid=(dff // ffb, s // tm),
        in_specs=[pl.BlockSpec((tm, d), lambda j, i: (i, 0)), pl.BlockSpec((ffb, d), lambda j, i: (j, 0)), t_spec, t_spec],
        out_specs=[t_spec] * 2, out_shape=[SDS((s, dff), bf16)] * 2, compiler_params=_params(("parallel", "parallel")),
    )(dff_out, w_d, gg, uu)


def _branch_merge(z_a, og, w_co, w_ho, proj, tm, gate_a0, gate_b0):
    s, dc = z_a.shape
    nsh, _, n = w_co.shape

    def body(za_ref, og_ref, wa_ref, wb_ref, ga_ref, gb_ref, ya_ref, yb_ref, m_ref):
        ya = jnp.dot(za_ref[...], wa_ref[...], preferred_element_type=f32)
        yb = jnp.dot(og_ref[...], wb_ref[...], preferred_element_type=f32)
        ya_ref[...] = ya.astype(bf16)
        yb_ref[...] = yb.astype(bf16)
        m_ref[...] = (_sigmoid(ga_ref[...]) * ya + _sigmoid(gb_ref[...]) * yb).astype(bf16)

    a_spec = pl.BlockSpec((tm, dc), lambda j, i: (i, 0))
    w_spec = pl.BlockSpec((None, dc, n), lambda j, i: (j, 0, 0))
    gate = lambda c0: pl.BlockSpec((tm, n), functools.partial(lambda j, i, cb: (i, cb + j), cb=c0 // n))
    o_spec = pl.BlockSpec((tm, n), lambda j, i: (i, j))
    return pl.pallas_call(
        body, name="branch_merge", grid=(nsh, s // tm), in_specs=[a_spec, a_spec, w_spec, w_spec, gate(gate_a0), gate(gate_b0)],
        out_specs=[o_spec] * 3, out_shape=[SDS((s, nsh * n), bf16)] * 3, compiler_params=_params(("parallel", "parallel")),
    )(z_a, og, w_co, w_ho, proj, proj)


def _d_branch_merge(dmo, w_o, y_a, y_b, proj, tm, gate_a0, gate_b0):
    s, d = dmo.shape
    n = d // 2

    def body(a_ref, w_ref, ya_ref, yb_ref, ga_ref, gb_ref, dya_ref, dyb_ref, dga_ref, dgb_ref):
        dm = lax.dot_general(a_ref[...], w_ref[...], (NT, ((), ())), preferred_element_type=f32)
        sa, sb_ = _sigmoid(ga_ref[...]), _sigmoid(gb_ref[...])
        dya_ref[...] = (dm * sa).astype(bf16)
        dyb_ref[...] = (dm * sb_).astype(bf16)
        dga_ref[...] = (dm * ya_ref[...].astype(f32) * sa * (1.0 - sa)).astype(bf16)
        dgb_ref[...] = (dm * yb_ref[...].astype(f32) * sb_ * (1.0 - sb_)).astype(bf16)

    t_spec = pl.BlockSpec((tm, n), lambda j, i: (i, j))
    gate = lambda c0: pl.BlockSpec((tm, n), functools.partial(lambda j, i, cb: (i, cb + j), cb=c0 // n))
    return pl.pallas_call(
        body, name="d_branch_merge", grid=(d // n, s // tm),
        in_specs=[pl.BlockSpec((tm, d), lambda j, i: (i, 0)), pl.BlockSpec((n, d), lambda j, i: (j, 0)), t_spec, t_spec,
                  gate(gate_a0), gate(gate_b0)],
        out_specs=[t_spec] * 4, out_shape=[SDS((s, d), bf16)] * 4, compiler_params=_params(("parallel", "parallel")),
    )(dmo, w_o, y_a, y_b, proj, proj)


def _local_step(x, tgt, mod, g_mix, g_ffn, g_fin, lb_param, gnorm, conv_w, get_w, put_g):
    s, d = x.shape
    dc = d // 2
    tm = min(512, s)
    te = min(256, s)
    tb = min(256, s)
    mt = s // tm
    sh_m, sc_m, gt_m, sh_f, sc_f, gt_f = [mod[i] for i in range(N_MOD)]
    dh2 = d // 2

    def e1(rows, fulls, outs, accs):
        xv = rows[0][...]
        g, sc, sh = [r[...] for r in fulls]
        _, xh = _rms(xv)
        outs[0][...] = (xh * g * (1.0 + sc) + sh).astype(bf16)

    h, = _rowwise("prenorm_mix", e1, s, te, [(x, d, 0)], [g_mix, sc_m, sh_m], [(d, bf16)])
    w_in, = get_w("in", h)
    nsh, _, win_sh = w_in.shape
    proj = _matmul("proj", h, w_in, (s, nsh * win_sh), f32, (nsh, mt, 1), pl.BlockSpec((tm, d), lambda j, i, k: (i, 0)),
                   pl.BlockSpec((None, d, win_sh), lambda j, i, k: (j, 0, 0)), pl.BlockSpec((tm, win_sh), lambda j, i, k: (i, j)), NN)
    z_a = _conv_fwd(proj, conv_w, s, dc)
    og, o_saved, states = _hgrn_fwd(proj, lb_param, gnorm, s, dc, tb)
    tm2 = min(1024, s)
    w_co, w_ho, w_o = get_w("mix", og)

    gate_a0, gate_b0 = 7 * dh2, 9 * dh2
    y_a, y_b, merged = _branch_merge(z_a, og, w_co, w_ho, proj, tm2, gate_a0, gate_b0)
    mo = _matmul("mix_out", merged, w_o, (s, d), f32, (2, mt, 1), pl.BlockSpec((tm, d), lambda j, i, k: (i, 0)),
                 pl.BlockSpec((d, dh2), lambda j, i, k: (0, j)), pl.BlockSpec((tm, dh2), lambda j, i, k: (i, j)), NN)

    def e6(rows, fulls, outs, accs):
        xv, mov = rows[0][...], rows[1][...]
        gt, g, sc, sh = [r[...] for r in fulls]
        x1 = xv + gt * mov
        outs[0][...] = x1
        _, xh = _rms(x1)
        outs[1][...] = (xh * g * (1.0 + sc) + sh).astype(bf16)

    x1, h2 = _rowwise("prenorm_ffn", e6, s, te, [(x, d, 0), (mo, d, 0)], [gt_m, g_ffn, sc_f, sh_f], [(d, f32), (d, bf16)])
    w_gt, w_ut, w_d = get_w("ffn", h2)
    dff_ = w_d.shape[0]
    ffb = dff_ // 4

    gg, uu, act = _ffn_in(h2, w_gt, w_ut, tm, ffb)
    ff = _matmul("ffn_down", act, w_d, (s, d), f32, (2, mt, 1), pl.BlockSpec((tm, dff_), lambda j, i, k: (i, 0)),
                 pl.BlockSpec((dff_, dh2), lambda j, i, k: (0, j)), pl.BlockSpec((tm, dh2), lambda j, i, k: (i, j)), NN)

    def e9(rows, fulls, outs, accs):
        x1v, ffv, tv = [r[...] for r in rows]
        gt, gf = fulls[0][...], fulls[1][...]
        x2 = x1v + gt * ffv
        r, xh = _rms(x2)
        err = xh * gf - tv
        accs[0][...] += 0.5 * jnp.sum(jnp.mean(err * err, axis=-1, keepdims=True), axis=0, keepdims=True)
        dy = err / d
        accs[1][...] += _colsum(dy * xh)
        dxh = dy * gf
        dx2 = r * (dxh - xh * jnp.mean(dxh * xh, axis=-1, keepdims=True))
        outs[0][...] = dx2
        outs[1][...] = (dx2 * gt).astype(bf16)
        accs[2][...] += _colsum(dx2 * ffv)

    dx2, dff, loss_acc, dg_fin, dgt_f = _rowwise("loss_head", e9, s, te, [(x1, d, 0), (ff, d, 0), (tgt, d, 0)], [gt_f, g_fin],
                                                 [(d, f32), (d, bf16)], [(1, 128), (1, d), (1, d)])
    dgg, duu = _ffn_down_bwd(dff, w_d, gg, uu, tm, ffb)

    def wgrad_rows(name, a, b, n_out):
        kb = 512
        return _matmul(name, a, b, (n_out, d), bf16, (n_out // kb, 2, 1), pl.BlockSpec((s, kb), lambda i, j, k: (0, i)),
                       pl.BlockSpec((s, dh2), lambda i, j, k: (0, j)), pl.BlockSpec((kb, dh2), lambda i, j, k: (i, j)), TN)

    gw_d = wgrad_rows("gw_ffn_down", act, dff, dff_)

    def ffn_in_bwd(name, a, w):
        return _matmul(name, a, w, (s, d), f32, (2, mt, 1), pl.BlockSpec((tm, dff_), lambda j, i, k: (i, 0)),
                       pl.BlockSpec((dff_, dh2), lambda j, i, k: (0, j)), pl.BlockSpec((tm, dh2), lambda j, i, k: (i, j)), NN)

    dh2a = ffn_in_bwd("d_ffn_gate_in", dgg, w_gt)
    dh2b = ffn_in_bwd("d_ffn_up_in", duu, w_ut)
    gw_gt = wgrad_rows("gw_ffn_gate", dgg, h2, dff_)
    gw_ut = wgrad_rows("gw_ffn_up", duu, h2, dff_)
    sc_f_late = _after_tokens(sc_f, [put_g("ffn", [gw_gt, gw_ut, gw_d])])

    def b5(rows, fulls, outs, accs):
        da, db, x1v, dx2v, mov = [r[...] for r in rows]
        sc, g, gt = [r[...] for r in fulls]
        dh = da + db
        r, xh = _rms(x1v)
        accs[0][...] += _colsum(dh)
        accs[1][...] += _colsum(dh * (xh * g))
        dn = dh * (1.0 + sc)
        accs[2][...] += _colsum(dn * xh)
        dxh = dn * g
        dx1 = dx2v + r * (dxh - xh * jnp.mean(dxh * xh, axis=-1, keepdims=True))
        outs[0][...] = dx1
        accs[3][...] += _colsum(dx1 * mov)
        outs[1][...] = (dx1 * gt).astype(bf16)

    dx1, dmo, dsh_f, dsc_f, dg_ffn, dgt_m = _rowwise(
        "d_prenorm_ffn", b5, s, te, [(dh2a, d, 0), (dh2b, d, 0), (x1, d, 0), (dx2, d, 0), (mo, d, 0)], [sc_f_late, g_ffn, gt_m],
        [(d, f32), (d, bf16)], [(1, d)] * 4)
    dya, dyb, dga, dgb = _d_branch_merge(dmo, w_o, y_a, y_b, proj, tm, gate_a0, gate_b0)
    gw_o = wgrad_rows("gw_mix_out", merged, dmo, d)

    def out_proj_bwd(name, dy, w):
        return _matmul(name, dy, w, (s, dc), f32, (1, s // tm2, nsh), pl.BlockSpec((tm2, d // nsh), lambda j, i, k: (i, k)),
                       pl.BlockSpec((None, dc, d // nsh), lambda j, i, k: (k, 0, 0)), pl.BlockSpec((tm2, dc), lambda j, i, k: (i, 0)), NT)

    def out_proj_wgrad(name, a, dy):
        return _matmul(name, a, dy, (nsh, dc, d // nsh), bf16, (1, nsh, 1), pl.BlockSpec((s, dc), lambda i, j, k: (0, 0)),
                       pl.BlockSpec((s, d // nsh), lambda i, j, k: (0, j)), pl.BlockSpec((None, dc, d // nsh), lambda i, j, k: (j, 0, 0)), TN)

    dz_a = out_proj_bwd("d_conv_out_in", dya, w_co)
    dog = out_proj_bwd("d_hgrn_out_in", dyb, w_ho)
    gw_co = out_proj_wgrad("gw_conv_out", z_a, dya)
    gw_ho = out_proj_wgrad("gw_hgrn_out", og, dyb)
    conv_w_late = _after_tokens(conv_w, [put_g("mix", [gw_co, gw_ho, gw_o])])
    dab, dac, dax, dconv_w = _conv_bwd(proj, conv_w_late, dz_a, s, dc)
    dq, dfl, dvi, dgo, dlb, dgn = _hgrn_bwd(proj, lb_param, gnorm, o_saved, states, dog, s, dc, tb)
    dproj = jnp.concatenate([dab, dac, dax, dq, dfl, dvi, dgo, dga, dgb], axis=1)
    gw_in = _matmul("gw_proj", h, dproj, (nsh, d, win_sh), bf16, (nsh, d // 512, 1), pl.BlockSpec((s, 512), lambda j, i, k: (0, i)),
                    pl.BlockSpec((s, win_sh), lambda j, i, k: (0, j)), pl.BlockSpec((None, 512, win_sh), lambda j, i, k: (j, i, 0)), TN)
    dh = _matmul("d_proj_in", dproj, w_in, (s, d), f32, (1, s // tm2, nsh), pl.BlockSpec((tm2, win_sh), lambda j, i, k: (i, k)),
                 pl.BlockSpec((None, d, win_sh), lambda j, i, k: (k, 0, 0)), pl.BlockSpec((tm2, d), lambda j, i, k: (i, 0)), NT,
                 after=put_g("in", [gw_in]))

    def b12(rows, fulls, outs, accs):
        dhv, xv, dx1v = [r[...] for r in rows]
        sc, g = fulls[0][...], fulls[1][...]
        r, xh = _rms(xv)
        accs[0][...] += _colsum(dhv)
        accs[1][...] += _colsum(dhv * (xh * g))
        dn = dhv * (1.0 + sc)
        accs[2][...] += _colsum(dn * xh)
        dxh = dn * g
        outs[0][...] = dx1v + r * (dxh - xh * jnp.mean(dxh * xh, axis=-1, keepdims=True))

    dx, dsh_m, dsc_m, dg_mix = _rowwise("d_prenorm_mix", b12, s, te, [(dh, d, 0), (x, d, 0), (dx1, d, 0)], [sc_m, g_mix],
                                        [(d, f32)], [(1, d)] * 3)
    dmod = [dsh_m, dsc_m, dgt_m, dsh_f, dsc_f, dgt_f]
    small = dict(loss=loss_acc, g_mix=dg_mix, g_ffn=dg_ffn, g_fin=dg_fin, lb=dlb, gnorm=dgn, conv_w=dconv_w)
    return dx, dmod, small


def _ada_fwd(c_all, w_sh, b_sh):
    def body(c_ref, w_ref, b_ref, o_ref):
        cv = c_ref[...]
        ca = (cv * _sigmoid(cv)).astype(bf16)
        o_ref[...] = jnp.dot(ca, w_ref[...].astype(bf16), preferred_element_type=f32) + b_ref[...]

    return pl.pallas_call(body, name="ada_fwd", out_shape=SDS((c_all.shape[0], w_sh.shape[1]), f32),
                          compiler_params=pltpu.CompilerParams(vmem_limit_bytes=V7X_VMEM_LIMIT))(c_all, w_sh, b_sh)


def _ada_wgrad(c_all, dmod_sh):
    def body(c_ref, d_ref, o_ref):
        cv = c_ref[...]
        ca = (cv * _sigmoid(cv)).astype(bf16)
        o_ref[...] = lax.dot_general(ca, d_ref[...].astype(bf16), (TN, ((), ())), preferred_element_type=f32)

    return pl.pallas_call(body, name="ada_wgrad", out_shape=SDS((c_all.shape[1], dmod_sh.shape[1]), f32),
                          compiler_params=pltpu.CompilerParams(vmem_limit_bytes=V7X_VMEM_LIMIT))(c_all, dmod_sh)


def _lb_grad(lb_param, dlb):
    def body(p_ref, d_ref, o_ref):
        p = p_ref[...]
        lb = _sigmoid(p[0:1, :] - p[1:2, :])
        gl = d_ref[...] * lb * (1.0 - lb)
        o_ref[0:1, :] = gl
        o_ref[1:2, :] = -gl

    return pl.pallas_call(body, name="lb_grad", out_shape=SDS(lb_param.shape, f32))(lb_param, dlb)


def _sum_small(gathered):
    def body(g_ref, o_ref):
        acc = g_ref[0]
        for dd in range(1, NDEV):
            acc = acc + g_ref[dd]
        o_ref[...] = acc

    return pl.pallas_call(body, name="sum_small", out_shape=SDS(gathered.shape[1:], f32))(gathered)


def kernel(x, c, w_ada, b_ada, norm_mix_g, w_in, conv_w, lb_param, gnorm_g, w_conv_out, w_hgrn_out, w_o, norm_ffn_g, w_ffn_gate, w_ffn_up, w_ffn_down, norm_final_g, loss_target, m_w_ada, m_b_ada, m_norm_mix_g, m_w_in, m_conv_w, m_lb_param, m_gnorm_g, m_w_conv_out, m_w_hgrn_out, m_w_o, m_norm_ffn_g, m_w_ffn_gate, m_w_ffn_up, m_w_ffn_down, m_norm_final_g, v_w_ada, v_b_ada, v_norm_mix_g, v_w_in, v_conv_w, v_lb_param, v_gnorm_g, v_w_conv_out, v_w_hgrn_out, v_w_o, v_norm_ffn_g, v_w_ffn_gate, v_w_ffn_up, v_w_ffn_down, v_norm_final_g):
    assert lb_param.shape[0] == 2 and w_ada.shape[0] == 1
    s, d = x.shape[1], x.shape[2]
    me = 4 * lax.axis_index("x") + 2 * lax.axis_index("y") + lax.axis_index("c")
    ada_cols = w_ada.shape[2]

    me1 = me.astype(jnp.int32).reshape(1)
    placed_in = [_place_shard("place_in0", me1, w_in[0])]
    c_all, cw_all = _all_gather("gather_cond", [c, conv_w[0]], after=placed_in)
    c_all = c_all.reshape(NDEV, d)
    conv_w_full = jnp.transpose(cw_all, (1, 0, 2)).reshape(conv_w.shape[1], -1)
    b_sh = lax.dynamic_slice_in_dim(b_ada, me * ada_cols, ada_cols, axis=1)
    mod_cols = _ada_fwd(c_all, w_ada[0], b_sh)
    mod_all, = _all_gather("gather_mod", [mod_cols])
    mod = lax.dynamic_index_in_dim(mod_all, me, axis=1, keepdims=False).reshape(N_MOD, 1, d)

    shard_groups = {"in": None, "mix": [w_conv_out[0], w_hgrn_out[0], w_o[0]],
                    "ffn": [w_ffn_gate[0].T, w_ffn_up[0].T, w_ffn_down[0]]}
    own_slot = lambda frm, to: _flat(frm)
    gather_plan = lambda n: [(a, j, own_slot, own_slot) for a in range(n) for j in (1,) + ICI_RELATIONS]
    flat = lambda a: a.reshape(a.shape[0] * a.shape[1], a.shape[2])
    to8 = lambda a: a.reshape(NDEV, a.shape[0] // NDEV, a.shape[1])
    gathering, tokens = {}, []
    for grp, sh in shard_groups.items():
        lands = placed_in if sh is None else [_place_shard(f"place_{grp}{i}", me1, a, after=tokens[-1]) for i, a in enumerate(sh)]
        ss, rs, _, lands, tok = _push_start("gather_start_" + grp, [], lands, gather_plan(len(lands)),
                                            after=tokens[-1] if tokens else mod_all)
        gathering[grp] = (ss, rs, lands)
        tokens.append(tok)

    def get_w(grp, after):
        ss, rs, lands = gathering[grp]
        _, lands = _push_wait("gather_wait_" + grp, ss, rs, [], lands, gather_plan(len(lands)), after)
        full = _sibling_forward("gather_fwd_" + grp, lands)
        return [f if grp == "in" or i < 2 and grp == "mix" else flat(f) for i, f in enumerate(full)]

    core = lax.axis_index("c").astype(jnp.int32).reshape(1)
    chip = (2 * lax.axis_index("x") + lax.axis_index("y")).astype(jnp.int32).reshape(1)
    scatter_plan = lambda n: [(a, j, lambda frm, to: _chip(to), lambda frm, to: _chip(frm)) for a in range(n) for j in ICI_RELATIONS]
    scattering = {}

    def put_g(grp, grads):
        g8 = [g if g.ndim == 3 else to8(g) for g in grads]
        recv = _sibling_swap("scatter_pair_" + grp, g8)
        pairs = [_pair_sum(f"pair_sum_{grp}{i}", core, g, r, _row_tile(g.shape[1])) for i, (g, r) in enumerate(zip(g8, recv))]
        lands = [lax.empty(p.shape, p.dtype) for p in pairs]
        ss, rs, srcs, lands, tok = _push_start("scatter_start_" + grp, pairs, lands, scatter_plan(len(pairs)))
        scattering[grp] = (ss, rs, srcs, lands)
        return tok

    def reduced(grp, after):
        ss, rs, srcs, lands = scattering[grp]
        srcs, lands = _push_wait("scatter_wait_" + grp, ss, rs, srcs, lands, scatter_plan(len(srcs)), after)
        return [_chip_sum(f"chip_sum_{grp}{i}", chip, p, r, _row_tile(p.shape[1])) for i, (p, r) in enumerate(zip(srcs, lands))]

    dx, dmod, small = _local_step(x[0], loss_target[0], mod, _after_tokens(norm_mix_g, tokens), norm_ffn_g,
                                  norm_final_g.reshape(1, d), lb_param, gnorm_g, conv_w_full, get_w, put_g)

    pieces = [*dmod, small["g_mix"], small["g_ffn"], small["g_fin"], small["lb"], small["gnorm"], small["loss"],
              small["conv_w"].reshape(1, -1)]
    widths = [p.shape[1] for p in pieces]
    offs = np.concatenate([[0], np.cumsum(widths)])
    packed = jnp.concatenate(pieces, axis=1)
    gathered, = _all_gather("gather_small", [packed])
    summed = _sum_small(gathered)
    part = lambda i: summed[:, offs[i]:offs[i + 1]]
    g_b_ada = summed[:, :N_MOD * d]
    g_norm_mix, g_norm_ffn, g_norm_fin, g_lb_row, g_gnorm, loss_vec, g_convw_flat = [part(i) for i in range(N_MOD, N_MOD + 7)]
    loss = loss_vec[0, 0]
    dmod_all = gathered[:, 0, :N_MOD * d]
    g_w_ada = _ada_wgrad(c_all, lax.dynamic_slice_in_dim(dmod_all, me * ada_cols, ada_cols, axis=1))
    g_lb = _lb_grad(lb_param, g_lb_row)
    cw_cols = conv_w.shape[2]
    g_conv_w = lax.dynamic_slice_in_dim(g_convw_flat.reshape(conv_w.shape[1], -1), me * cw_cols, cw_cols, axis=1)

    grads = dict(w_ada=g_w_ada, b_ada=g_b_ada, norm_mix_g=g_norm_mix, conv_w=g_conv_w, lb_param=g_lb, gnorm_g=g_gnorm,
                 norm_ffn_g=g_norm_ffn, norm_final_g=g_norm_fin)
    weights = dict(w_ada=(w_ada, m_w_ada, v_w_ada), b_ada=(b_ada, m_b_ada, v_b_ada), norm_mix_g=(norm_mix_g, m_norm_mix_g, v_norm_mix_g),
                   w_in=(w_in, m_w_in, v_w_in), conv_w=(conv_w, m_conv_w, v_conv_w), lb_param=(lb_param, m_lb_param, v_lb_param),
                   gnorm_g=(gnorm_g, m_gnorm_g, v_gnorm_g), w_conv_out=(w_conv_out, m_w_conv_out, v_w_conv_out),
                   w_hgrn_out=(w_hgrn_out, m_w_hgrn_out, v_w_hgrn_out), w_o=(w_o, m_w_o, v_w_o),
                   norm_ffn_g=(norm_ffn_g, m_norm_ffn_g, v_norm_ffn_g), w_ffn_gate=(w_ffn_gate, m_w_ffn_gate, v_w_ffn_gate),
                   w_ffn_up=(w_ffn_up, m_w_ffn_up, v_w_ffn_up), w_ffn_down=(w_ffn_down, m_w_ffn_down, v_w_ffn_down),
                   norm_final_g=(norm_final_g, m_norm_final_g, v_norm_final_g))
    res = {}

    def update(nm):
        w, m, v = weights[nm]
        shape2 = (w.shape[-2], w.shape[-1]) if w.ndim >= 2 else (1, w.shape[0])
        g2 = grads[nm].reshape(shape2)
        dl, mn, vn = _adamw("adamw_" + nm, g2, w.reshape(shape2), m.reshape(shape2), v.reshape(shape2))
        res[nm] = [a.reshape(w.shape) for a in (g2, dl, mn, vn)]

    for nm in list(grads):
        update(nm)
    g_w_gt, g_w_ut, grads["w_ffn_down"] = reduced("ffn", res["w_ada"][1])
    grads["w_ffn_gate"], grads["w_ffn_up"] = g_w_gt.T, g_w_ut.T
    for nm in ("w_ffn_gate", "w_ffn_up", "w_ffn_down"):
        update(nm)
    grads["w_conv_out"], grads["w_hgrn_out"], grads["w_o"] = reduced("mix", res["w_ffn_down"][1])
    for nm in ("w_conv_out", "w_hgrn_out", "w_o"):
        update(nm)
    grads["w_in"], = reduced("in", res["w_o"][1])
    update("w_in")
    outs = [[res[nm][i] for nm in weights] for i in range(4)]
    return (loss, dx.reshape(x.shape), *outs[0], *outs[1], *outs[2], *outs[3])
```

```python
import functools

import numpy as np
import jax
import jax.numpy as jnp
from jax import lax
from jax.experimental import pallas as pl
from jax.experimental.pallas import tpu as pltpu

f32, bf16 = jnp.float32, jnp.bfloat16
SDS = jax.ShapeDtypeStruct

EPS = 1e-6
HEADS, DK, CHUNK = 8, 128, 64
N_MOD = 6
NDEV = 8
ADAM_LR, ADAM_B1, ADAM_B2, ADAM_EPS, ADAM_WD, ADAM_STEP = 0.001, 0.9, 0.999, 1e-08, 0.01, 10
LEVELS = (32, 16, 8, 4, 2, 1)
V7X_VMEM_LIMIT = 56 * 1024 * 1024
HBM = pl.BlockSpec(memory_space=pltpu.HBM)
MESH = pl.DeviceIdType.MESH


def _params(sem):
    return pltpu.CompilerParams(dimension_semantics=sem, vmem_limit_bytes=V7X_VMEM_LIMIT)


def _sigmoid(x):
    return jax.nn.sigmoid(x)


def _dsilu(x, s):
    return s * (1.0 + x * (1.0 - s))


def _mesh_pos():
    x, y, c = lax.axis_index("x"), lax.axis_index("y"), lax.axis_index("c")
    return x, y, c


def _peer(pos, j):
    x, y, c = pos
    return (1 - x if j & 4 else x, 1 - y if j & 2 else y, 1 - c if j & 1 else c)


def _flat(pos):
    return 4 * pos[0] + 2 * pos[1] + pos[2]


def _all_gather(name, arrs, after=()):
    n, ne = len(arrs), len(after)
    out_shapes = [SDS((NDEV,) + a.shape, a.dtype) for a in arrs]

    def body(*refs):
        ins, outs = refs[:n], refs[n + ne:2 * n + ne]
        send_sems, recv_sems, local_sems = refs[2 * n + ne:]
        pos = _mesh_pos()
        me = _flat(pos)

        def copy(a, j, frm, to_pos):
            k = a * (NDEV - 1) + j - 1
            return pltpu.make_async_remote_copy(src_ref=ins[a], dst_ref=outs[a].at[frm], send_sem=send_sems.at[k],
                                                recv_sem=recv_sems.at[k], device_id=to_pos, device_id_type=MESH)

        local = [pltpu.make_async_copy(ins[a], outs[a].at[me], local_sems.at[a]) for a in range(n)]
        for cp in local:
            cp.start()
        sends = [copy(a, j, me, _peer(pos, j)) for j in range(1, NDEV) for a in range(n)]
        for cp in sends:
            cp.start()
        for j in range(1, NDEV):
            for a in range(n):
                copy(a, j, _flat(_peer(pos, j)), pos).wait_recv()
        for cp in sends:
            cp.wait_send()
        for cp in local:
            cp.wait()

    return pl.pallas_call(
        body, name=name, out_shape=out_shapes, in_specs=[HBM] * n + [ANY] * ne, out_specs=[HBM] * n,
        scratch_shapes=[pltpu.SemaphoreType.DMA((n * (NDEV - 1),)), pltpu.SemaphoreType.DMA((n * (NDEV - 1),)),
                        pltpu.SemaphoreType.DMA((n,))],
    )(*arrs, *after)


SEM = pl.BlockSpec(memory_space=pltpu.SEMAPHORE)
ANY = pl.BlockSpec(memory_space=pl.ANY)
EFFECT = pltpu.SideEffectType.DATAFLOW_SIDE_EFFECTING
ICI_RELATIONS = (2, 4, 6)


def _chip(pos):
    return 2 * pos[0] + pos[1]


def _hbm(a):
    return pltpu.with_memory_space_constraint(a, pltpu.HBM)


def _plan_copy(plan_entry, k, pos, frm, to, src_refs, land_refs, send_sems, recv_sems):
    a, _, src_slot, dst_slot = plan_entry
    s = src_refs[a] if src_slot is None else src_refs[a].at[src_slot(frm, to)]
    return pltpu.make_async_remote_copy(src_ref=s, dst_ref=land_refs[a].at[dst_slot(frm, to)], send_sem=send_sems.at[k],
                                        recv_sem=recv_sems.at[k], device_id=to, device_id_type=MESH)


def _push_start(name, srcs, lands, plan, after=None):
    ns, nb, nk = len(srcs), len(srcs) + len(lands), len(plan)
    extra = [] if after is None else [after]

    def body(*refs):
        land_refs = refs[ns:nb]
        src_refs = refs[:ns] if ns else land_refs
        send_sems, recv_sems = refs[nb + len(extra)], refs[nb + len(extra) + 1]
        pos = _mesh_pos()
        for k, e in enumerate(plan):
            _plan_copy(e, k, pos, pos, _peer(pos, e[1]), src_refs, land_refs, send_sems, recv_sems).start()
        refs[-1][...] = jnp.zeros_like(refs[-1])

    outs = pl.pallas_call(
        body, name=name,
        out_shape=[pltpu.SemaphoreType.DMA((nk,)), pltpu.SemaphoreType.DMA((nk,))] + [pltpu.HBM(a.shape, a.dtype) for a in srcs + lands]
        + [SDS((8, 128), f32)],
        in_specs=[HBM] * nb + [ANY] * len(extra), out_specs=[SEM, SEM] + [HBM] * nb + [pl.BlockSpec(memory_space=pltpu.VMEM)],
        input_output_aliases={i: 2 + i for i in range(nb)},
        compiler_params=pltpu.CompilerParams(has_side_effects=EFFECT),
    )(*[_hbm(a) for a in srcs + lands], *extra)
    return outs[0], outs[1], list(outs[2:2 + ns]), list(outs[2 + ns:2 + nb]), outs[-1]


def _push_wait(name, send_sems, recv_sems, srcs, lands, plan, after):
    ns, nb = len(srcs), len(srcs) + len(lands)

    def body(*refs):
        land_refs = refs[ns:nb]
        src_refs = refs[:ns] if ns else land_refs
        ssem, rsem = refs[nb], refs[nb + 1]
        pos = _mesh_pos()
        for k, e in enumerate(plan):
            peer = _peer(pos, e[1])
            _plan_copy(e, k, pos, pos, peer, src_refs, land_refs, ssem, rsem).wait_send()
            _plan_copy(e, k, pos, peer, pos, src_refs, land_refs, ssem, rsem).wait_recv()

    outs = pl.pallas_call(
        body, name=name, out_shape=[pltpu.HBM(a.shape, a.dtype) for a in srcs + lands],
        in_specs=[HBM] * nb + [SEM, SEM, ANY], out_specs=[HBM] * nb,
        input_output_aliases={i: i for i in range(nb)},
        compiler_params=pltpu.CompilerParams(has_side_effects=EFFECT),
    )(*srcs, *lands, send_sems, recv_sems, after)
    return list(outs[:ns]), list(outs[ns:])


def _after_tokens(small, tokens):
    for t in tokens:
        if t is not None:
            small = small + t[0:1, 0:1].reshape((1,) * small.ndim)
    return small


def _place_shard(name, me, shard, after=None):
    r, c = shard.shape
    tr = _row_tile(r)
    extra = [] if after is None else [after]

    def body(me_ref, s_ref, *rest):
        rest[-1][...] = s_ref[...].astype(bf16)

    return pl.pallas_call(
        body, name=name, out_shape=SDS((NDEV, r, c), bf16),
        grid_spec=pltpu.PrefetchScalarGridSpec(
            num_scalar_prefetch=1, grid=(r // tr,), in_specs=[pl.BlockSpec((tr, c), lambda i, me_ref: (i, 0))] + [ANY] * len(extra),
            out_specs=pl.BlockSpec((None, tr, c), lambda i, me_ref: (me_ref[0], i, 0))),
        compiler_params=_params(("parallel",)),
    )(me, shard, *extra)


def _row_tile(r):
    return max(t for t in range(16, 257, 16) if r % t == 0)


def _sibling_forward(name, lands):
    n = len(lands)
    nk = n * len(ICI_RELATIONS)

    def body(*refs):
        land_refs = refs[:n]
        send_sems, recv_sems = refs[2 * n:]
        pos = _mesh_pos()
        sib = _peer(pos, 1)

        def copy(a, i, frm, to):
            k = a * len(ICI_RELATIONS) + i
            slot = _flat(_peer(frm, ICI_RELATIONS[i]))
            return pltpu.make_async_remote_copy(src_ref=land_refs[a].at[slot], dst_ref=land_refs[a].at[slot], send_sem=send_sems.at[k],
                                                recv_sem=recv_sems.at[k], device_id=to, device_id_type=MESH)

        sends = [copy(a, i, pos, sib) for a in range(n) for i in range(len(ICI_RELATIONS))]
        for cp in sends:
            cp.start()
        for a in range(n):
            for i in range(len(ICI_RELATIONS)):
                copy(a, i, sib, pos).wait_recv()
        for cp in sends:
            cp.wait_send()

    outs = pl.pallas_call(
        body, name=name, out_shape=[SDS(a.shape, a.dtype) for a in lands], in_specs=[HBM] * n, out_specs=[HBM] * n,
        input_output_aliases={i: i for i in range(n)},
        scratch_shapes=[pltpu.SemaphoreType.DMA((nk,)), pltpu.SemaphoreType.DMA((nk,))],
    )(*lands)
    return list(outs)


def _sibling_swap(name, grads):
    n = len(grads)
    nchip = NDEV // 2

    def body(*refs):
        g_refs, out_refs = refs[:n], refs[n:2 * n]
        send_sems, recv_sems = refs[2 * n:]
        pos = _mesh_pos()
        sib = _peer(pos, 1)
        sends = []
        for a in range(n):
            for q in range(nchip):
                k = a * nchip + q
                sends.append(pltpu.make_async_remote_copy(src_ref=g_refs[a].at[2 * q + sib[2]], dst_ref=out_refs[a].at[q],
                                                          send_sem=send_sems.at[k], recv_sem=recv_sems.at[k], device_id=sib,
                                                          device_id_type=MESH))
        for cp in sends:
            cp.start()
        for cp in sends:
            cp.wait()

    return pl.pallas_call(
        body, name=name, out_shape=[SDS((nchip,) + a.shape[1:], a.dtype) for a in grads], in_specs=[HBM] * n, out_specs=[HBM] * n,
        scratch_shapes=[pltpu.SemaphoreType.DMA((n * nchip,)), pltpu.SemaphoreType.DMA((n * nchip,))],
    )(*grads)


def _pair_sum(name, core, grad, recv, tr):
    _, r, c = grad.shape
    nchip = NDEV // 2

    def body(core_ref, g_ref, r_ref, o_ref):
        o_ref[...] = (g_ref[...].astype(f32) + r_ref[...].astype(f32)).astype(o_ref.dtype)

    return pl.pallas_call(
        body, name=name, out_shape=SDS((nchip, r, c), grad.dtype),
        grid_spec=pltpu.PrefetchScalarGridSpec(
            num_scalar_prefetch=1, grid=(nchip, r // tr),
            in_specs=[pl.BlockSpec((None, tr, c), lambda q, i, core_ref: (2 * q + core_ref[0], i, 0)),
                      pl.BlockSpec((None, tr, c), lambda q, i, core_ref: (q, i, 0))],
            out_specs=pl.BlockSpec((None, tr, c), lambda q, i, core_ref: (q, i, 0))),
        compiler_params=_params(("parallel", "parallel")),
    )(core, grad, recv)


def _adamw_math(g, w, m, v):
    mn = ADAM_B1 * m + (1.0 - ADAM_B1) * g
    vn = ADAM_B2 * v + (1.0 - ADAM_B2) * jnp.square(g)
    m_hat = mn / (1.0 - ADAM_B1 ** ADAM_STEP)
    v_hat = vn / (1.0 - ADAM_B2 ** ADAM_STEP)
    return -ADAM_LR * (m_hat / (jnp.sqrt(v_hat) + ADAM_EPS) + ADAM_WD * w), mn, vn


def _chip_sum(name, chip, pairs, recv, tr, wmv=None):
    nchip, r, c = pairs.shape
    n_out = 1 if wmv is None else 4

    def body(chip_ref, p_ref, r_ref, *rest):
        mine = chip_ref[0]
        acc = jnp.zeros((tr, c), f32)
        for q in range(nchip):
            acc = acc + jnp.where(mine == q, p_ref[q].astype(f32), r_ref[q].astype(f32))
        outs = rest[-n_out:]
        outs[0][...] = acc
        if wmv is not None:
            w_ref, m_ref, v_ref = rest[:3]
            outs[1][...], outs[2][...], outs[3][...] = _adamw_math(acc, w_ref[...], m_ref[...], v_ref[...])

    blk = pl.BlockSpec((tr, c), lambda i, chip_ref: (i, 0))
    outs = pl.pallas_call(
        body, name=name, out_shape=[SDS((r, c), f32)] * n_out,
        grid_spec=pltpu.PrefetchScalarGridSpec(
            num_scalar_prefetch=1, grid=(r // tr,),
            in_specs=[pl.BlockSpec((nchip, tr, c), lambda i, chip_ref: (0, i, 0))] * 2 + [blk] * (n_out - 1),
            out_specs=[blk] * n_out),
        compiler_params=_params(("parallel",)),
    )(chip, pairs, recv, *(wmv or ()))
    return outs[0] if wmv is None else outs


def _matmul(name, a, b, out_shape, out_dtype, grid, a_spec, b_spec, o_spec, dims, after=None):
    ksteps = grid[2]
    acc_shape = tuple(d for d in o_spec.block_shape if d is not None)
    extra = [] if after is None else [after]

    def body(a_ref, b_ref, *rest):
        o_ref, acc = rest[len(extra)], rest[len(extra) + 1:]
        prod = lax.dot_general(a_ref[...], b_ref[...], (dims, ((), ())), preferred_element_type=f32)
        if ksteps == 1:
            o_ref[...] = prod.astype(o_ref.dtype)
        else:
            k = pl.program_id(2)

            @pl.when(k == 0)
            def _():
                acc[0][...] = prod

            @pl.when(k > 0)
            def _():
                acc[0][...] += prod

            @pl.when(k == ksteps - 1)
            def _():
                o_ref[...] = acc[0][...].astype(o_ref.dtype)

    return pl.pallas_call(
        body, name=name, grid=grid, in_specs=[a_spec, b_spec] + [ANY] * len(extra), out_specs=o_spec,
        out_shape=SDS(out_shape, out_dtype), scratch_shapes=[] if ksteps == 1 else [pltpu.VMEM(acc_shape, f32)],
        compiler_params=_params(("parallel", "parallel", "arbitrary")),
    )(a, b, *extra)


NN, NT, TN = ((1,), (0,)), ((1,), (1,)), ((0,), (0,))


def _rowwise(name, fn, n_rows, tm, rows, fulls, row_outs, acc_outs=(), ncol=1):
    assert ncol == 1 or not acc_outs
    nr, nf, no, na = len(rows), len(fulls), len(row_outs), len(acc_outs)
    in_specs = [pl.BlockSpec((tm, w), functools.partial(lambda i, j, cb: (i, cb + j), cb=cb)) for (_, w, cb) in rows]
    in_specs += [pl.BlockSpec(a.shape, functools.partial(lambda i, j, nd: (0,) * nd, nd=a.ndim)) for a in fulls]
    out_shape = [SDS((n_rows, w), dt) for (w, dt) in row_outs] + [SDS(s, f32) for s in acc_outs]
    out_specs = [pl.BlockSpec((tm, w // ncol), lambda i, j: (i, j)) for (w, _) in row_outs]
    out_specs += [pl.BlockSpec(s, functools.partial(lambda i, j, nd: (0,) * nd, nd=len(s))) for s in acc_outs]

    def body(*refs):
        if na:
            @pl.when(pl.program_id(0) == 0)
            def _():
                for r in refs[nr + nf + no:]:
                    r[...] = jnp.zeros(r.shape, r.dtype)
        fn(refs[:nr], refs[nr:nr + nf], refs[nr + nf:nr + nf + no], refs[nr + nf + no:])

    return pl.pallas_call(
        body, name=name, grid=(n_rows // tm, ncol), in_specs=in_specs, out_specs=out_specs, out_shape=out_shape,
        compiler_params=_params(("arbitrary" if na else "parallel", "arbitrary" if na else "parallel")),
    )(*[r[0] for r in rows], *fulls)


def _rms(x):
    r = lax.rsqrt(jnp.mean(x * x, axis=-1, keepdims=True) + EPS)
    return r, x * r


def _colsum(v):
    return jnp.sum(v, axis=0, keepdims=True)


def _shift_down(u, row, k):
    return jnp.where(row >= k, pltpu.roll(u, k, 0), 0.0)


def _shift_up(u, row, k):
    n = u.shape[0]
    return jnp.where(row < n - k, pltpu.roll(u, n - k, 0), 0.0)


def _conv_fwd(proj, conv_w, s, dc):
    nb = dc // 128

    def body(ab_ref, ac_ref, ax_ref, w_ref, z_ref):
        u = ac_ref[...] * ax_ref[...]
        row = lax.broadcasted_iota(jnp.int32, u.shape, 0)
        w = w_ref[...]
        cv = w[0:1] * _shift_down(u, row, 2) + w[1:2] * _shift_down(u, row, 1) + w[2:3] * u
        z_ref[...] = (ab_ref[...] * cv).astype(z_ref.dtype)

    col = lambda off: pl.BlockSpec((s, 128), functools.partial(lambda j, off: (0, off + j), off=off))
    return pl.pallas_call(
        body, name="conv_fwd", grid=(nb,), in_specs=[col(0), col(nb), col(2 * nb), pl.BlockSpec((3, 128), lambda j: (0, j))],
        out_specs=pl.BlockSpec((s, 128), lambda j: (0, j)), out_shape=SDS((s, dc), bf16), compiler_params=_params(("parallel",)),
    )(proj, proj, proj, conv_w)


def _conv_bwd(proj, conv_w, dz, s, dc):
    nb = dc // 128

    def body(ab_ref, ac_ref, ax_ref, w_ref, dz_ref, dab_ref, dac_ref, dax_ref, dw_ref):
        ab, ac, ax, dzv = ab_ref[...], ac_ref[...], ax_ref[...], dz_ref[...]
        u = ac * ax
        row = lax.broadcasted_iota(jnp.int32, u.shape, 0)
        w = w_ref[...]
        u1, u2 = _shift_down(u, row, 1), _shift_down(u, row, 2)
        cv = w[0:1] * u2 + w[1:2] * u1 + w[2:3] * u
        dcv = dzv * ab
        dab_ref[...] = (dzv * cv).astype(dab_ref.dtype)
        du = w[2:3] * dcv + w[1:2] * _shift_up(dcv, row, 1) + w[0:1] * _shift_up(dcv, row, 2)
        dac_ref[...] = (du * ax).astype(dac_ref.dtype)
        dax_ref[...] = (du * ac).astype(dax_ref.dtype)
        dw_ref[0:1, :] = _colsum(dcv * u2)
        dw_ref[1:2, :] = _colsum(dcv * u1)
        dw_ref[2:3, :] = _colsum(dcv * u)

    col = lambda off: pl.BlockSpec((s, 128), functools.partial(lambda j, off: (0, off + j), off=off))
    blk = pl.BlockSpec((s, 128), lambda j: (0, j))
    return pl.pallas_call(
        body, name="conv_bwd", grid=(nb,),
        in_specs=[col(0), col(nb), col(2 * nb), pl.BlockSpec((3, 128), lambda j: (0, j)), blk],
        out_specs=[blk, blk, blk, pl.BlockSpec((3, 128), lambda j: (0, j))],
        out_shape=[SDS((s, dc), bf16)] * 3 + [SDS((3, dc), f32)], compiler_params=_params(("parallel",)),
    )(proj, proj, proj, conv_w, dz)


def _level_masks():
    t = np.arange(CHUNK)[:, None]
    s = np.arange(CHUNK)[None, :]
    m = np.stack([((t & h) != 0) & ((s & h) == 0) & (t // (2 * h) == s // (2 * h)) for h in LEVELS]).astype(np.float32)
    return jnp.asarray(m), jnp.asarray(m.transpose(0, 2, 1))


def _cumsum_rows(x, row):
    for sh in (1, 2, 4, 8, 16, 32):
        x = x + jnp.where(row >= sh, pltpu.roll(x, sh, 0), 0.0)
    return x


def _rev_cumsum_rows(x, row):
    n = x.shape[0]
    for sh in (1, 2, 4, 8, 16, 32):
        x = x + jnp.where(row < n - sh, pltpu.roll(x, n - sh, 0), 0.0)
    return x


def _chunk_terms(qp, fl, lb):
    row = lax.broadcasted_iota(jnp.int32, qp.shape, 0)
    sig = _sigmoid(fl)
    f = lb + (1.0 - lb) * sig
    k = 1.0 - f
    sq = _sigmoid(qp)
    qh = qp * sq
    b = _cumsum_rows(jnp.log(f), row)
    sub = lax.broadcasted_iota(jnp.int32, (CHUNK // 8, 8, DK), 1)
    b8 = b.reshape(CHUNK // 8, 8, DK)
    us, exs, ups = [], [], []
    for m in LEVELS:
        sb = 2 * m
        if sb >= 8:
            b3 = b.reshape(CHUNK // sb, sb, DK)
            bref = jnp.broadcast_to(b3[:, m - 1:m, :], b3.shape).reshape(CHUNK, DK)
        else:
            bref8 = None
            for j in range(8 // sb):
                cand = jnp.broadcast_to(b8[:, j * sb + m - 1:j * sb + m, :], b8.shape)
                bref8 = cand if bref8 is None else jnp.where(sub >= j * sb, cand, bref8)
            bref = bref8.reshape(CHUNK, DK)
        up = (row & m) != 0
        ex = jnp.exp(jnp.where(up, b - bref, bref - b))
        us.append((jnp.where(up, qh, k) * ex).astype(bf16))
        exs.append(ex)
        ups.append(up)
    blast = b[CHUNK - 1:CHUNK, :]
    eb, ebl = jnp.exp(b), jnp.exp(blast - b)
    return dict(sig=sig, f=f, k=k, sq=sq, qh=qh, u=jnp.stack(us), ex=exs, up=ups, eb=eb, ebl=ebl, qt=qh * eb, kt=k * ebl,
                el=jnp.exp(blast), row=row)


def _scores(t, mask):
    pl_ = jnp.einsum("ltk,lsk->lts", t["u"], t["u"], preferred_element_type=f32)
    p = jnp.sum(pl_ * mask, axis=0)
    r = lax.broadcasted_iota(jnp.int32, (CHUNK, CHUNK), 0)
    c = lax.broadcasted_iota(jnp.int32, (CHUNK, CHUNK), 1)
    diag = jnp.sum(t["qh"] * t["k"], axis=-1, keepdims=True)
    return p + jnp.where(r == c, diag, 0.0)


def _hgrn_fwd(proj, lb_param, gnorm, s, dv_total, tb):
    nchunk = tb // CHUNK
    masks, _ = _level_masks()
    q0, f0, v0, g0 = 3 * HEADS, 4 * HEADS, 5 * HEADS, 6 * HEADS

    def body(q_ref, f_ref, v_ref, g_ref, lb_ref, gn_ref, mask_ref, og_ref, o_ref, st_ref, state):
        @pl.when(pl.program_id(1) == 0)
        def _():
            state[...] = jnp.zeros_like(state)

        lbp = lb_ref[...]
        lb = _sigmoid(lbp[0:1, :] - lbp[1:2, :])
        mask = mask_ref[...]
        for i in range(nchunk):
            rs = pl.ds(i * CHUNK, CHUNK)
            t = _chunk_terms(q_ref[rs, :], f_ref[rs, :], lb)
            v = v_ref[rs, :]
            vb = v.astype(bf16)
            st = state[...]
            st_ref[i] = st
            p = _scores(t, mask)
            o = jnp.dot(p.astype(bf16), vb, preferred_element_type=f32)
            o += lax.dot_general(t["qt"].astype(bf16), st.astype(bf16), (NT, ((), ())), preferred_element_type=f32)
            state[...] = st * t["el"] + lax.dot_general(vb, t["kt"].astype(bf16), (TN, ((), ())), preferred_element_type=f32)
            o_ref[rs, :] = o
            r, oh = _rms(o)
            g = g_ref[rs, :]
            og_ref[rs, :] = (oh * gn_ref[...] * (g * _sigmoid(g))).astype(og_ref.dtype)

    col = lambda off: pl.BlockSpec((tb, DK), functools.partial(lambda h, t, off: (t, off + h), off=off))
    blk = pl.BlockSpec((tb, DK), lambda h, t: (t, h))
    return pl.pallas_call(
        body, name="hgrn_fwd", grid=(HEADS, s // tb),
        in_specs=[col(q0), col(f0), col(v0), col(g0), pl.BlockSpec((2, DK), lambda h, t: (0, h)),
                  pl.BlockSpec((1, DK), lambda h, t: (0, 0)), pl.BlockSpec(masks.shape, lambda h, t: (0, 0, 0))],
        out_specs=[blk, blk, pl.BlockSpec((nchunk, None, DK, DK), lambda h, t: (t, h, 0, 0))],
        out_shape=[SDS((s, dv_total), bf16), SDS((s, dv_total), f32), SDS((s // CHUNK, HEADS, DK, DK), f32)],
        scratch_shapes=[pltpu.VMEM((DK, DK), f32)], compiler_params=_params(("parallel", "arbitrary")),
    )(proj, proj, proj, proj, lb_param, gnorm, masks)


def _hgrn_bwd(proj, lb_param, gnorm, o_saved, states, dog, s, dv_total, tb):
    nchunk = tb // CHUNK
    nt = s // tb
    nc_total = s // CHUNK
    masks, masks_t = _level_masks()
    q0, f0, v0, g0 = 3 * HEADS, 4 * HEADS, 5 * HEADS, 6 * HEADS

    def body(q_ref, f_ref, v_ref, g_ref, lb_ref, gn_ref, mask_ref, maskt_ref, o_ref, dog_ref, st_ref, stn_ref,
             dq_ref, df_ref, dv_ref, dg_ref, dlb_ref, dgn_ref, gstate):
        h_id, t_id = pl.program_id(0), pl.program_id(1)

        @pl.when(t_id == 0)
        def _():
            gstate[...] = jnp.zeros_like(gstate)
            dlb_ref[...] = jnp.zeros_like(dlb_ref)

        @pl.when((t_id == 0) & (h_id == 0))
        def _():
            dgn_ref[...] = jnp.zeros_like(dgn_ref)

        lbp = lb_ref[...]
        lb = _sigmoid(lbp[0:1, :] - lbp[1:2, :])
        mask, maskt = mask_ref[...], maskt_ref[...]
        gn = gn_ref[...]
        for i in reversed(range(nchunk)):
            rs = pl.ds(i * CHUNK, CHUNK)
            qp, fl, v, g = q_ref[rs, :], f_ref[rs, :], v_ref[rs, :], g_ref[rs, :]
            t = _chunk_terms(qp, fl, lb)
            vb = v.astype(bf16)
            st0 = st_ref[i]
            st1 = st_ref[i + 1] if i + 1 < nchunk else stn_ref[0]
            gt = gstate[...]
            o = o_ref[rs, :]
            r, oh = _rms(o)
            sg = _sigmoid(g)
            dog_v = dog_ref[rs, :]
            dg_ref[rs, :] = (dog_v * (oh * gn) * _dsilu(g, sg)).astype(dg_ref.dtype)
            don = dog_v * (g * sg)
            dgn_ref[...] += _colsum(don * oh)
            doh = don * gn
            do = r * (doh - oh * jnp.mean(doh * oh, axis=-1, keepdims=True))
            dob = do.astype(bf16)
            d = lax.dot_general(dob, vb, (NT, ((), ())), preferred_element_type=f32)
            dt = lax.dot_general(vb, dob, (NT, ((), ())), preferred_element_type=f32)
            z = (mask * d[None] + maskt * dt[None]).astype(bf16)
            rr = jnp.einsum("lts,lsk->ltk", z, t["u"], preferred_element_type=f32)
            dq = jnp.zeros((CHUNK, DK), f32)
            dk = jnp.zeros((CHUNK, DK), f32)
            qdk = jnp.zeros((CHUNK, DK), f32)
            for li in range(len(LEVELS)):
                du = t["ex"][li] * rr[li]
                dq += jnp.where(t["up"][li], du, 0.0)
                dk += jnp.where(t["up"][li], 0.0, du)
                e = t["u"][li].astype(f32) * rr[li]
                qdk += jnp.where(t["up"][li], e, -e)
            dd = jnp.sum(do * v, axis=-1, keepdims=True)
            dq += dd * t["k"]
            dk += dd * t["qh"]
            gtb = gt.astype(bf16)
            ktb, qtb = t["kt"].astype(bf16), t["qt"].astype(bf16)
            dq_in = jnp.dot(dob, st0.astype(bf16), preferred_element_type=f32)
            dk_in = jnp.dot(vb, gtb, preferred_element_type=f32)
            dq += t["eb"] * dq_in
            dk += t["ebl"] * dk_in
            qdk += qtb.astype(f32) * dq_in - ktb.astype(f32) * dk_in
            p = _scores(t, mask)
            dvv = lax.dot_general(p.astype(bf16), dob, (TN, ((), ())), preferred_element_type=f32)
            dvv += lax.dot_general(ktb, gtb, (NT, ((), ())), preferred_element_type=f32)
            dv_ref[rs, :] = dvv.astype(dv_ref.dtype)
            a_end = _colsum(gtb.astype(f32) * st1)
            dlf = _rev_cumsum_rows(qdk, t["row"]) + a_end
            dfv = dlf / t["f"] - dk
            df_ref[rs, :] = (dfv * (1.0 - lb) * t["sig"] * (1.0 - t["sig"])).astype(df_ref.dtype)
            dlb_ref[...] += _colsum(dfv * (1.0 - t["sig"]))
            dq_ref[rs, :] = (dq * _dsilu(qp, t["sq"])).astype(dq_ref.dtype)
            gstate[...] = gt * t["el"] + lax.dot_general(dob, qtb, (TN, ((), ())), preferred_element_type=f32)

    rev = lambda t: nt - 1 - t
    col = lambda off: pl.BlockSpec((tb, DK), functools.partial(lambda h, t, off: (rev(t), off + h), off=off))
    blk = pl.BlockSpec((tb, DK), lambda h, t: (rev(t), h))
    nxt = lambda h, t: (jnp.minimum((rev(t) + 1) * nchunk, nc_total - 1), h, 0, 0)
    return pl.pallas_call(
        body, name="hgrn_bwd", grid=(HEADS, nt),
        in_specs=[col(q0), col(f0), col(v0), col(g0), pl.BlockSpec((2, DK), lambda h, t: (0, h)),
                  pl.BlockSpec((1, DK), lambda h, t: (0, 0)), pl.BlockSpec(masks.shape, lambda h, t: (0, 0, 0)),
                  pl.BlockSpec(masks.shape, lambda h, t: (0, 0, 0)), blk, blk,
                  pl.BlockSpec((nchunk, None, DK, DK), lambda h, t: (rev(t), h, 0, 0)),
                  pl.BlockSpec((1, None, DK, DK), nxt)],
        out_specs=[blk, blk, blk, blk, pl.BlockSpec((1, DK), lambda h, t: (0, h)), pl.BlockSpec((1, DK), lambda h, t: (0, 0))],
        out_shape=[SDS((s, dv_total), bf16)] * 4 + [SDS((1, HEADS * DK), f32), SDS((1, DK), f32)],
        scratch_shapes=[pltpu.VMEM((DK, DK), f32)], compiler_params=_params(("arbitrary", "arbitrary")),
    )(proj, proj, proj, proj, lb_param, gnorm, masks, masks_t, o_saved, dog, states, states)


def _adamw(name, g, w, m, v):
    r, c = w.shape
    tr = r
    for cand in (256, 128, 64, 32, 16, 8):
        if r % cand == 0 and r > cand:
            tr = cand
            break

    def body(g_ref, w_ref, m_ref, v_ref, d_ref, mo_ref, vo_ref):
        d_ref[...], mo_ref[...], vo_ref[...] = _adamw_math(g_ref[...], w_ref[...], m_ref[...], v_ref[...])

    blk = pl.BlockSpec((tr, c), lambda i: (i, 0))
    return pl.pallas_call(
        body, name=name, grid=(r // tr,), in_specs=[blk] * 4, out_specs=[blk] * 3, out_shape=[SDS((r, c), f32)] * 3,
        compiler_params=_params(("parallel",)),
    )(g, w, m, v)


def _ffn_in(h2, w_gt, w_ut, tm, ffb):
    s, d = h2.shape
    dff = w_gt.shape[0]

    def body(a_ref, wg_ref, wu_ref, dg_ref, du_ref, act_ref):
        a = a_ref[...]
        g = lax.dot_general(a, wg_ref[...], (NT, ((), ())), preferred_element_type=f32)
        u = lax.dot_general(a, wu_ref[...], (NT, ((), ())), preferred_element_type=f32)
        sg = _sigmoid(g)
        silu = g * sg
        dg_ref[...] = (u * _dsilu(g, sg)).astype(bf16)
        du_ref[...] = silu.astype(bf16)
        act_ref[...] = (silu * u).astype(bf16)

    w_spec = pl.BlockSpec((ffb, d), lambda j, i: (j, 0))
    o_spec = pl.BlockSpec((tm, ffb), lambda j, i: (i, j))
    return pl.pallas_call(
        body, name="ffn_in", grid=(dff // ffb, s // tm), in_specs=[pl.BlockSpec((tm, d), lambda j, i: (i, 0)), w_spec, w_spec],
        out_specs=[o_spec] * 3, out_shape=[SDS((s, dff), bf16)] * 3, compiler_params=_params(("parallel", "parallel")),
    )(h2, w_gt, w_ut)


def _ffn_down_bwd(dff_out, w_d, act_dg, act_du, tm, ffb):
    s, d = dff_out.shape
    dff = w_d.shape[0]

    def body(a_ref, w_ref, fg_ref, fu_ref, dg_ref, du_ref):
        da = lax.dot_general(a_ref[...], w_ref[...], (NT, ((), ())), preferred_element_type=f32)
        dg_ref[...] = (da * fg_ref[...].astype(f32)).astype(bf16)
        du_ref[...] = (da * fu_ref[...].astype(f32)).astype(bf16)

    t_spec = pl.BlockSpec((tm, ffb), lambda j, i: (i, j))
    return pl.pallas_call(
        body, name="d_ffn_down_in", grid=(dff // ffb, s // tm),
        in_specs=[pl.BlockSpec((tm, d), lambda j, i: (i, 0)), pl.BlockSpec((ffb, d), lambda j, i: (j, 0)), t_spec, t_spec],
        out_specs=[t_spec] * 2, out_shape=[SDS((s, dff), bf16)] * 2, compiler_params=_params(("parallel", "parallel")),
    )(dff_out, w_d, act_dg, act_du)


def _branch_merge(z_a, og, w_co, w_ho, proj, tm, gate_a0, gate_b0):
    s, dc = z_a.shape
    nsh, _, n = w_co.shape

    def body(za_ref, og_ref, wa_ref, wb_ref, ga_ref, gb_ref, m_ref, sa_ref, sb_ref, fa_ref, fb_ref):
        ya = jnp.dot(za_ref[...], wa_ref[...], preferred_element_type=f32)
        yb = jnp.dot(og_ref[...], wb_ref[...], preferred_element_type=f32)
        sa, sb_ = _sigmoid(ga_ref[...]), _sigmoid(gb_ref[...])
        m_ref[...] = (sa * ya + sb_ * yb).astype(bf16)
        sa_ref[...] = sa.astype(bf16)
        sb_ref[...] = sb_.astype(bf16)
        fa_ref[...] = (ya * sa * (1.0 - sa)).astype(bf16)
        fb_ref[...] = (yb * sb_ * (1.0 - sb_)).astype(bf16)

    a_spec = pl.BlockSpec((tm, dc), lambda i, j: (i, 0))
    w_spec = pl.BlockSpec((None, dc, n), lambda i, j: (j, 0, 0))
    gate = lambda c0: pl.BlockSpec((tm, n), functools.partial(lambda i, j, cb: (i, cb + j), cb=c0 // n))
    o_spec = pl.BlockSpec((tm, n), lambda i, j: (i, j))
    return pl.pallas_call(
        body, name="branch_merge", grid=(s // tm, nsh), in_specs=[a_spec, a_spec, w_spec, w_spec, gate(gate_a0), gate(gate_b0)],
        out_specs=[o_spec] * 5, out_shape=[SDS((s, nsh * n), bf16)] * 5, compiler_params=_params(("parallel", "parallel")),
    )(z_a, og, w_co, w_ho, proj, proj)


def _d_branch_merge(dmo, w_o, factors, tm):
    s, d = dmo.shape
    n = d // 2

    def body(a_ref, w_ref, sa_ref, sb_ref, fa_ref, fb_ref, dya_ref, dyb_ref, dga_ref, dgb_ref):
        dm = lax.dot_general(a_ref[...], w_ref[...], (NT, ((), ())), preferred_element_type=f32)
        for f_ref, o_ref in ((sa_ref, dya_ref), (sb_ref, dyb_ref), (fa_ref, dga_ref), (fb_ref, dgb_ref)):
            o_ref[...] = (dm * f_ref[...].astype(f32)).astype(bf16)

    t_spec = pl.BlockSpec((tm, n), lambda j, i: (i, j))
    return pl.pallas_call(
        body, name="d_branch_merge", grid=(d // n, s // tm),
        in_specs=[pl.BlockSpec((tm, d), lambda j, i: (i, 0)), pl.BlockSpec((n, d), lambda j, i: (j, 0))] + [t_spec] * 4,
        out_specs=[t_spec] * 4, out_shape=[SDS((s, d), bf16)] * 4, compiler_params=_params(("parallel", "parallel")),
    )(dmo, w_o, *factors)


def _local_step(x, tgt, mod, g_mix, g_ffn, g_fin, lb_param, gnorm, conv_w, get_w, put_g, sent):
    s, d = x.shape
    dc = d // 2
    tm = min(512, s)
    te = min(256, s)
    tb = min(256, s)
    mt = s // tm
    sh_m, sc_m, gt_m, sh_f, sc_f, gt_f = [mod[i] for i in range(N_MOD)]
    dh2 = d // 2

    def e1(rows, fulls, outs, accs):
        xv = rows[0][...]
        g, sc, sh = [r[...] for r in fulls]
        _, xh = _rms(xv)
        outs[0][...] = (xh * g * (1.0 + sc) + sh).astype(bf16)

    h, = _rowwise("prenorm_mix", e1, s, te, [(x, d, 0)], [g_mix, sc_m, sh_m], [(d, bf16)])
    w_in, = get_w("in", h)
    nsh, _, win_sh = w_in.shape
    proj = _matmul("proj", h, w_in, (s, nsh * win_sh), f32, (nsh, mt, 1), pl.BlockSpec((tm, d), lambda j, i, k: (i, 0)),
                   pl.BlockSpec((None, d, win_sh), lambda j, i, k: (j, 0, 0)), pl.BlockSpec((tm, win_sh), lambda j, i, k: (i, j)), NN)
    z_a = _conv_fwd(proj, conv_w, s, dc)
    og, o_saved, states = _hgrn_fwd(proj, lb_param, gnorm, s, dc, tb)
    tm2 = min(1024, s)
    w_co, w_ho, w_o = get_w("mix", og)

    gate_a0, gate_b0 = 7 * dh2, 9 * dh2
    merged, *merge_factors = _branch_merge(z_a, og, w_co, w_ho, proj, tm2, gate_a0, gate_b0)
    mo = _matmul("mix_out", merged, w_o, (s, d), f32, (2, mt, 1), pl.BlockSpec((tm, d), lambda j, i, k: (i, 0)),
                 pl.BlockSpec((d, dh2), lambda j, i, k: (0, j)), pl.BlockSpec((tm, dh2), lambda j, i, k: (i, j)), NN)

    def e6(rows, fulls, outs, accs):
        xv, mov = rows[0][...], rows[1][...]
        gt, g, sc, sh = [r[...] for r in fulls]
        x1 = xv + gt * mov
        outs[0][...] = x1
        _, xh = _rms(x1)
        outs[1][...] = (xh * g * (1.0 + sc) + sh).astype(bf16)

    x1, h2 = _rowwise("prenorm_ffn", e6, s, te, [(x, d, 0), (mo, d, 0)], [gt_m, g_ffn, sc_f, sh_f], [(d, f32), (d, bf16)])
    w_gt, w_ut, w_d = get_w("ffn", h2)
    dff_ = w_d.shape[0]
    ffb = dff_ // 4

    act_dg, act_du, act = _ffn_in(h2, w_gt, w_ut, tm, ffb)
    ff = _matmul("ffn_down", act, w_d, (s, d), f32, (2, mt, 1), pl.BlockSpec((tm, dff_), lambda j, i, k: (i, 0)),
                 pl.BlockSpec((dff_, dh2), lambda j, i, k: (0, j)), pl.BlockSpec((tm, dh2), lambda j, i, k: (i, j)), NN)

    def e9(rows, fulls, outs, accs):
        x1v, ffv, tv = [r[...] for r in rows]
        gt, gf = fulls[0][...], fulls[1][...]
        x2 = x1v + gt * ffv
        r, xh = _rms(x2)
        err = xh * gf - tv
        accs[0][...] += 0.5 * jnp.sum(jnp.mean(err * err, axis=-1, keepdims=True), axis=0, keepdims=True)
        dy = err / d
        accs[1][...] += _colsum(dy * xh)
        dxh = dy * gf
        dx2 = r * (dxh - xh * jnp.mean(dxh * xh, axis=-1, keepdims=True))
        outs[0][...] = dx2
        outs[1][...] = (dx2 * gt).astype(bf16)
        accs[2][...] += _colsum(dx2 * ffv)

    dx2, dff, loss_acc, dg_fin, dgt_f = _rowwise("loss_head", e9, s, te, [(x1, d, 0), (ff, d, 0), (tgt, d, 0)], [gt_f, g_fin],
                                                 [(d, f32), (d, bf16)], [(1, 128), (1, d), (1, d)])
    dgg, duu = _ffn_down_bwd(dff, w_d, act_dg, act_du, tm, ffb)

    def wgrad_rows(name, a, b, n_out):
        kb = 512
        return _matmul(name, a, b, (n_out, d), bf16, (n_out // kb, 2, 1), pl.BlockSpec((s, kb), lambda i, j, k: (0, i)),
                       pl.BlockSpec((s, dh2), lambda i, j, k: (0, j)), pl.BlockSpec((kb, dh2), lambda i, j, k: (i, j)), TN)

    gw_d = wgrad_rows("gw_ffn_down", act, dff, dff_)

    def ffn_in_bwd(name, a, w, after=None):
        return _matmul(name, a, w, (s, d), f32, (2, mt, 1), pl.BlockSpec((tm, dff_), lambda j, i, k: (i, 0)),
                       pl.BlockSpec((dff_, dh2), lambda j, i, k: (0, j)), pl.BlockSpec((tm, dh2), lambda j, i, k: (i, j)), NN,
                       after=after)

    dh2b = ffn_in_bwd("d_ffn_up_in", duu, w_ut)
    gw_ut = wgrad_rows("gw_ffn_up", duu, h2, dff_)
    gw_gt = wgrad_rows("gw_ffn_gate", dgg, h2, dff_)
    dh2a = ffn_in_bwd("d_ffn_gate_in", dgg, w_gt, after=put_g("ffn", [gw_gt, gw_ut, gw_d]))
    sc_f_late = _after_tokens(sc_f, [sent("ffn", dh2a)])

    def b5(rows, fulls, outs, accs):
        da, db, x1v, dx2v, mov = [r[...] for r in rows]
        sc, g, gt = [r[...] for r in fulls]
        dh = da + db
        r, xh = _rms(x1v)
        accs[0][...] += _colsum(dh)
        accs[1][...] += _colsum(dh * (xh * g))
        dn = dh * (1.0 + sc)
        accs[2][...] += _colsum(dn * xh)
        dxh = dn * g
        dx1 = dx2v + r * (dxh - xh * jnp.mean(dxh * xh, axis=-1, keepdims=True))
        outs[0][...] = dx1
        accs[3][...] += _colsum(dx1 * mov)
        outs[1][...] = (dx1 * gt).astype(bf16)

    dx1, dmo, dsh_f, dsc_f, dg_ffn, dgt_m = _rowwise(
        "d_prenorm_ffn", b5, s, te, [(dh2a, d, 0), (dh2b, d, 0), (x1, d, 0), (dx2, d, 0), (mo, d, 0)], [sc_f_late, g_ffn, gt_m],
        [(d, f32), (d, bf16)], [(1, d)] * 4)
    dya, dyb, dga, dgb = _d_branch_merge(dmo, w_o, merge_factors, tm)
    gw_o = wgrad_rows("gw_mix_out", merged, dmo, d)

    def out_proj_bwd(name, dy, w):
        return _matmul(name, dy, w, (s, dc), f32, (1, s // tm2, nsh), pl.BlockSpec((tm2, d // nsh), lambda j, i, k: (i, k)),
                       pl.BlockSpec((None, dc, d // nsh), lambda j, i, k: (k, 0, 0)), pl.BlockSpec((tm2, dc), lambda j, i, k: (i, 0)), NT)

    def out_proj_wgrad(name, a, dy):
        return _matmul(name, a, dy, (nsh, dc, d // nsh), bf16, (1, nsh, 1), pl.BlockSpec((s, dc), lambda i, j, k: (0, 0)),
                       pl.BlockSpec((s, d // nsh), lambda i, j, k: (0, j)), pl.BlockSpec((None, dc, d // nsh), lambda i, j, k: (j, 0, 0)), TN)

    dz_a = out_proj_bwd("d_conv_out_in", dya, w_co)
    dog = out_proj_bwd("d_hgrn_out_in", dyb, w_ho)
    gw_co = out_proj_wgrad("gw_conv_out", z_a, dya)
    gw_ho = out_proj_wgrad("gw_hgrn_out", og, dyb)
    conv_w_late = _after_tokens(conv_w, [put_g("mix", [gw_co, gw_ho, gw_o])])
    dab, dac, dax, dconv_w = _conv_bwd(proj, conv_w_late, dz_a, s, dc)
    lb_param_late = _after_tokens(lb_param, [sent("mix", dab)])
    dq, dfl, dvi, dgo, dlb, dgn = _hgrn_bwd(proj, lb_param_late, gnorm, o_saved, states, dog, s, dc, tb)
    dproj = jnp.concatenate([dab, dac, dax, dq, dfl, dvi, dgo, dga, dgb], axis=1)
    gw_in = _matmul("gw_proj", h, dproj, (nsh, d, win_sh), bf16, (nsh, d // 512, 1), pl.BlockSpec((s, 512), lambda j, i, k: (0, i)),
                    pl.BlockSpec((s, win_sh), lambda j, i, k: (0, j)), pl.BlockSpec((None, 512, win_sh), lambda j, i, k: (j, i, 0)), TN)
    dh = _matmul("d_proj_in", dproj, w_in, (s, d), f32, (1, s // tm2, nsh), pl.BlockSpec((tm2, win_sh), lambda j, i, k: (i, k)),
                 pl.BlockSpec((None, d, win_sh), lambda j, i, k: (k, 0, 0)), pl.BlockSpec((tm2, d), lambda j, i, k: (i, 0)), NT,
                 after=put_g("in", [gw_in]))

    def b12(rows, fulls, outs, accs):
        dhv, xv, dx1v = [r[...] for r in rows]
        sc, g = fulls[0][...], fulls[1][...]
        r, xh = _rms(xv)
        accs[0][...] += _colsum(dhv)
        accs[1][...] += _colsum(dhv * (xh * g))
        dn = dhv * (1.0 + sc)
        accs[2][...] += _colsum(dn * xh)
        dxh = dn * g
        outs[0][...] = dx1v + r * (dxh - xh * jnp.mean(dxh * xh, axis=-1, keepdims=True))

    dx, dsh_m, dsc_m, dg_mix = _rowwise("d_prenorm_mix", b12, s, te, [(dh, d, 0), (x, d, 0), (dx1, d, 0)], [sc_m, g_mix],
                                        [(d, f32)], [(1, d)] * 3)
    dmod = [dsh_m, dsc_m, dgt_m, dsh_f, dsc_f, dgt_f]
    small = dict(loss=loss_acc, g_mix=dg_mix, g_ffn=dg_ffn, g_fin=dg_fin, lb=dlb, gnorm=dgn, conv_w=dconv_w)
    return dx, dmod, small


def _ada_fwd(c_all, w_sh, b_sh):
    def body(c_ref, w_ref, b_ref, o_ref):
        cv = c_ref[...]
        ca = (cv * _sigmoid(cv)).astype(bf16)
        o_ref[...] = jnp.dot(ca, w_ref[...].astype(bf16), preferred_element_type=f32) + b_ref[...]

    return pl.pallas_call(body, name="ada_fwd", out_shape=SDS((c_all.shape[0], w_sh.shape[1]), f32),
                          compiler_params=pltpu.CompilerParams(vmem_limit_bytes=V7X_VMEM_LIMIT))(c_all, w_sh, b_sh)


def _ada_wgrad(c_all, dmod_sh):
    def body(c_ref, d_ref, o_ref):
        cv = c_ref[...]
        ca = (cv * _sigmoid(cv)).astype(bf16)
        o_ref[...] = lax.dot_general(ca, d_ref[...].astype(bf16), (TN, ((), ())), preferred_element_type=f32)

    return pl.pallas_call(body, name="ada_wgrad", out_shape=SDS((c_all.shape[1], dmod_sh.shape[1]), f32),
                          compiler_params=pltpu.CompilerParams(vmem_limit_bytes=V7X_VMEM_LIMIT))(c_all, dmod_sh)


def _lb_grad(lb_param, dlb):
    def body(p_ref, d_ref, o_ref):
        p = p_ref[...]
        lb = _sigmoid(p[0:1, :] - p[1:2, :])
        gl = d_ref[...] * lb * (1.0 - lb)
        o_ref[0:1, :] = gl
        o_ref[1:2, :] = -gl

    return pl.pallas_call(body, name="lb_grad", out_shape=SDS(lb_param.shape, f32))(lb_param, dlb)


def _sum_small(gathered):
    def body(g_ref, o_ref):
        acc = g_ref[0]
        for dd in range(1, NDEV):
            acc = acc + g_ref[dd]
        o_ref[...] = acc

    return pl.pallas_call(body, name="sum_small", out_shape=SDS(gathered.shape[1:], f32))(gathered)


def kernel(x, c, w_ada, b_ada, norm_mix_g, w_in, conv_w, lb_param, gnorm_g, w_conv_out, w_hgrn_out, w_o, norm_ffn_g, w_ffn_gate, w_ffn_up, w_ffn_down, norm_final_g, loss_target, m_w_ada, m_b_ada, m_norm_mix_g, m_w_in, m_conv_w, m_lb_param, m_gnorm_g, m_w_conv_out, m_w_hgrn_out, m_w_o, m_norm_ffn_g, m_w_ffn_gate, m_w_ffn_up, m_w_ffn_down, m_norm_final_g, v_w_ada, v_b_ada, v_norm_mix_g, v_w_in, v_conv_w, v_lb_param, v_gnorm_g, v_w_conv_out, v_w_hgrn_out, v_w_o, v_norm_ffn_g, v_w_ffn_gate, v_w_ffn_up, v_w_ffn_down, v_norm_final_g):
    assert lb_param.shape[0] == 2 and w_ada.shape[0] == 1
    s, d = x.shape[1], x.shape[2]
    me = 4 * lax.axis_index("x") + 2 * lax.axis_index("y") + lax.axis_index("c")
    ada_cols = w_ada.shape[2]

    me1 = me.astype(jnp.int32).reshape(1)
    placed_in = [_place_shard("place_in0", me1, w_in[0])]
    c_all, cw_all = _all_gather("gather_cond", [c, conv_w[0]], after=placed_in)
    c_all = c_all.reshape(NDEV, d)
    conv_w_full = jnp.transpose(cw_all, (1, 0, 2)).reshape(conv_w.shape[1], -1)
    b_sh = lax.dynamic_slice_in_dim(b_ada, me * ada_cols, ada_cols, axis=1)
    mod_cols = _ada_fwd(c_all, w_ada[0], b_sh)
    mod_all, = _all_gather("gather_mod", [mod_cols])
    mod = lax.dynamic_index_in_dim(mod_all, me, axis=1, keepdims=False).reshape(N_MOD, 1, d)

    shard_groups = {"in": None, "mix": [w_conv_out[0], w_hgrn_out[0], w_o[0]],
                    "ffn": [w_ffn_gate[0].T, w_ffn_up[0].T, w_ffn_down[0]]}
    own_slot = lambda frm, to: _flat(frm)
    gather_plan = lambda n: [(a, j, own_slot, own_slot) for a in range(n) for j in (1,) + ICI_RELATIONS]
    flat = lambda a: a.reshape(a.shape[0] * a.shape[1], a.shape[2])
    to8 = lambda a: a.reshape(NDEV, a.shape[0] // NDEV, a.shape[1])
    gathering, tokens = {}, []
    for grp, sh in shard_groups.items():
        lands = placed_in if sh is None else [_place_shard(f"place_{grp}{i}", me1, a, after=tokens[-1]) for i, a in enumerate(sh)]
        ss, rs, _, lands, tok = _push_start("gather_start_" + grp, [], lands, gather_plan(len(lands)),
                                            after=tokens[-1] if tokens else mod_all)
        gathering[grp] = (ss, rs, lands)
        tokens.append(tok)

    def get_w(grp, after):
        ss, rs, lands = gathering[grp]
        _, lands = _push_wait("gather_wait_" + grp, ss, rs, [], lands, gather_plan(len(lands)), after)
        full = _sibling_forward("gather_fwd_" + grp, lands)
        return [f if grp == "in" or i < 2 and grp == "mix" else flat(f) for i, f in enumerate(full)]

    core = lax.axis_index("c").astype(jnp.int32).reshape(1)
    chip = (2 * lax.axis_index("x") + lax.axis_index("y")).astype(jnp.int32).reshape(1)
    scatter_plan = lambda n: [(a, j, lambda frm, to: _chip(to), lambda frm, to: _chip(frm)) for a in range(n) for j in ICI_RELATIONS]
    scattering = {}

    swap_plan = lambda n: [(a, 1, functools.partial(lambda frm, to, q: 2 * q + to[2], q=q), functools.partial(lambda frm, to, q: q, q=q))
                           for a in range(n) for q in range(NDEV // 2)]
    swapping = {}

    def start_ici(grp, g8, recv):
        pairs = [_pair_sum(f"pair_sum_{grp}{i}", core, g, r, _row_tile(g.shape[1])) for i, (g, r) in enumerate(zip(g8, recv))]
        lands = [lax.empty(p.shape, p.dtype) for p in pairs]
        ss, rs, srcs, lands, tok = _push_start("scatter_start_" + grp, pairs, lands, scatter_plan(len(pairs)))
        scattering[grp] = (ss, rs, srcs, lands)
        return tok

    def put_g(grp, grads):
        g8 = [g if g.ndim == 3 else to8(g) for g in grads]
        if grp == "in":
            return start_ici(grp, g8, _sibling_swap("scatter_pair_" + grp, g8))
        lands = [lax.empty((NDEV // 2,) + g.shape[1:], g.dtype) for g in g8]
        ss, rs, srcs, lands, tok = _push_start("scatter_swap_" + grp, g8, lands, swap_plan(len(g8)))
        swapping[grp] = (ss, rs, srcs, lands)
        return tok

    def sent(grp, after):
        ss, rs, srcs, lands = swapping[grp]
        g8, recv = _push_wait("scatter_swapped_" + grp, ss, rs, srcs, lands, swap_plan(len(srcs)), after)
        return start_ici(grp, g8, recv)

    def reduced(grp, after, names):
        ss, rs, srcs, lands = scattering[grp]
        srcs, lands = _push_wait("scatter_wait_" + grp, ss, rs, srcs, lands, scatter_plan(len(srcs)), after)
        out = []
        for i, (p, r, nm) in enumerate(zip(srcs, lands, names)):
            wmv = None if nm is None else tuple(a[0] for a in weights[nm])
            out.append(_chip_sum(f"chip_sum_{grp}{i}", chip, p, r, _row_tile(p.shape[1]), wmv))
            if nm is not None:
                res[nm] = [a.reshape(weights[nm][0].shape) for a in out[-1]]
        return out

    dx, dmod, small = _local_step(x[0], loss_target[0], mod, _after_tokens(norm_mix_g, tokens), norm_ffn_g,
                                  norm_final_g.reshape(1, d), lb_param, gnorm_g, conv_w_full, get_w, put_g, sent)

    pieces = [*dmod, small["g_mix"], small["g_ffn"], small["g_fin"], small["lb"], small["gnorm"], small["loss"],
              small["conv_w"].reshape(1, -1)]
    widths = [p.shape[1] for p in pieces]
    offs = np.concatenate([[0], np.cumsum(widths)])
    packed = jnp.concatenate(pieces, axis=1)
    gathered, = _all_gather("gather_small", [packed])
    summed = _sum_small(gathered)
    part = lambda i: summed[:, offs[i]:offs[i + 1]]
    g_b_ada = summed[:, :N_MOD * d]
    g_norm_mix, g_norm_ffn, g_norm_fin, g_lb_row, g_gnorm, loss_vec, g_convw_flat = [part(i) for i in range(N_MOD, N_MOD + 7)]
    loss = loss_vec[0, 0]
    dmod_all = gathered[:, 0, :N_MOD * d]
    g_w_ada = _ada_wgrad(c_all, lax.dynamic_slice_in_dim(dmod_all, me * ada_cols, ada_cols, axis=1))
    g_lb = _lb_grad(lb_param, g_lb_row)
    cw_cols = conv_w.shape[2]
    g_conv_w = lax.dynamic_slice_in_dim(g_convw_flat.reshape(conv_w.shape[1], -1), me * cw_cols, cw_cols, axis=1)

    grads = dict(w_ada=g_w_ada, b_ada=g_b_ada, norm_mix_g=g_norm_mix, conv_w=g_conv_w, lb_param=g_lb, gnorm_g=g_gnorm,
                 norm_ffn_g=g_norm_ffn, norm_final_g=g_norm_fin)
    weights = dict(w_ada=(w_ada, m_w_ada, v_w_ada), b_ada=(b_ada, m_b_ada, v_b_ada), norm_mix_g=(norm_mix_g, m_norm_mix_g, v_norm_mix_g),
                   w_in=(w_in, m_w_in, v_w_in), conv_w=(conv_w, m_conv_w, v_conv_w), lb_param=(lb_param, m_lb_param, v_lb_param),
                   gnorm_g=(gnorm_g, m_gnorm_g, v_gnorm_g), w_conv_out=(w_conv_out, m_w_conv_out, v_w_conv_out),
                   w_hgrn_out=(w_hgrn_out, m_w_hgrn_out, v_w_hgrn_out), w_o=(w_o, m_w_o, v_w_o),
                   norm_ffn_g=(norm_ffn_g, m_norm_ffn_g, v_norm_ffn_g), w_ffn_gate=(w_ffn_gate, m_w_ffn_gate, v_w_ffn_gate),
                   w_ffn_up=(w_ffn_up, m_w_ffn_up, v_w_ffn_up), w_ffn_down=(w_ffn_down, m_w_ffn_down, v_w_ffn_down),
                   norm_final_g=(norm_final_g, m_norm_final_g, v_norm_final_g))
    res = {}

    def update(nm):
        w, m, v = weights[nm]
        shape2 = (w.shape[-2], w.shape[-1]) if w.ndim >= 2 else (1, w.shape[0])
        g2 = grads[nm].reshape(shape2)
        dl, mn, vn = _adamw("adamw_" + nm, g2, w.reshape(shape2), m.reshape(shape2), v.reshape(shape2))
        res[nm] = [a.reshape(w.shape) for a in (g2, dl, mn, vn)]

    for nm in list(grads):
        update(nm)
    g_w_gt, g_w_ut, _ = reduced("ffn", res["w_ada"][1], (None, None, "w_ffn_down"))
    grads["w_ffn_gate"], grads["w_ffn_up"] = g_w_gt.T, g_w_ut.T
    for nm in ("w_ffn_gate", "w_ffn_up"):
        update(nm)
    reduced("mix", res["w_ffn_up"][1], ("w_conv_out", "w_hgrn_out", "w_o"))
    reduced("in", res["w_o"][1], ("w_in",))
    outs = [[res[nm][i] for nm in weights] for i in range(4)]
    return (loss, dx.reshape(x.shape), *outs[0], *outs[1], *outs[2], *outs[3])
```

```python
import functools

import numpy as np
import jax
import jax.numpy as jnp
from jax import lax
from jax.experimental import pallas as pl
from jax.experimental.pallas import tpu as pltpu

f32, bf16 = jnp.float32, jnp.bfloat16
SDS = jax.ShapeDtypeStruct

EPS = 1e-6
HEADS, DK, CHUNK = 8, 128, 64
HEADS_PER_STEP = 2
N_MOD = 6
NDEV = 8
ADAM_LR, ADAM_B1, ADAM_B2, ADAM_EPS, ADAM_WD, ADAM_STEP = 0.001, 0.9, 0.999, 1e-08, 0.01, 10
LEVELS = (32, 16, 8, 4, 2, 1)
V7X_VMEM_LIMIT = 56 * 1024 * 1024
HBM = pl.BlockSpec(memory_space=pltpu.HBM)
MESH = pl.DeviceIdType.MESH


def _params(sem):
    return pltpu.CompilerParams(dimension_semantics=sem, vmem_limit_bytes=V7X_VMEM_LIMIT)


def _sigmoid(x):
    return jax.nn.sigmoid(x)


def _dsilu(x, s):
    return s * (1.0 + x * (1.0 - s))


def _mesh_pos():
    x, y, c = lax.axis_index("x"), lax.axis_index("y"), lax.axis_index("c")
    return x, y, c


def _peer(pos, j):
    x, y, c = pos
    return (1 - x if j & 4 else x, 1 - y if j & 2 else y, 1 - c if j & 1 else c)


def _flat(pos):
    return 4 * pos[0] + 2 * pos[1] + pos[2]


def _all_gather(name, arrs, after=()):
    n, ne = len(arrs), len(after)
    out_shapes = [SDS((NDEV,) + a.shape, a.dtype) for a in arrs]

    def body(*refs):
        ins, outs = refs[:n], refs[n + ne:2 * n + ne]
        send_sems, recv_sems, local_sems = refs[2 * n + ne:]
        pos = _mesh_pos()
        me = _flat(pos)

        def copy(a, j, frm, to_pos):
            k = a * (NDEV - 1) + j - 1
            return pltpu.make_async_remote_copy(src_ref=ins[a], dst_ref=outs[a].at[frm], send_sem=send_sems.at[k],
                                                recv_sem=recv_sems.at[k], device_id=to_pos, device_id_type=MESH)

        local = [pltpu.make_async_copy(ins[a], outs[a].at[me], local_sems.at[a]) for a in range(n)]
        for cp in local:
            cp.start()
        sends = [copy(a, j, me, _peer(pos, j)) for j in range(1, NDEV) for a in range(n)]
        for cp in sends:
            cp.start()
        for j in range(1, NDEV):
            for a in range(n):
                copy(a, j, _flat(_peer(pos, j)), pos).wait_recv()
        for cp in sends:
            cp.wait_send()
        for cp in local:
            cp.wait()

    return pl.pallas_call(
        body, name=name, out_shape=out_shapes, in_specs=[HBM] * n + [ANY] * ne, out_specs=[HBM] * n,
        scratch_shapes=[pltpu.SemaphoreType.DMA((n * (NDEV - 1),)), pltpu.SemaphoreType.DMA((n * (NDEV - 1),)),
                        pltpu.SemaphoreType.DMA((n,))],
    )(*arrs, *after)


SEM = pl.BlockSpec(memory_space=pltpu.SEMAPHORE)
ANY = pl.BlockSpec(memory_space=pl.ANY)
EFFECT = pltpu.SideEffectType.DATAFLOW_SIDE_EFFECTING
ICI_RELATIONS = (2, 4, 6)


def _chip(pos):
    return 2 * pos[0] + pos[1]


def _hbm(a):
    return pltpu.with_memory_space_constraint(a, pltpu.HBM)


def _plan_copy(plan_entry, k, pos, frm, to, src_refs, land_refs, send_sems, recv_sems):
    a, _, src_slot, dst_slot = plan_entry
    s = src_refs[a] if src_slot is None else src_refs[a].at[src_slot(frm, to)]
    return pltpu.make_async_remote_copy(src_ref=s, dst_ref=land_refs[a].at[dst_slot(frm, to)], send_sem=send_sems.at[k],
                                        recv_sem=recv_sems.at[k], device_id=to, device_id_type=MESH)


def _push_start(name, srcs, lands, plan, after=None):
    ns, nb, nk = len(srcs), len(srcs) + len(lands), len(plan)
    extra = [] if after is None else [after]

    def body(*refs):
        land_refs = refs[ns:nb]
        src_refs = refs[:ns] if ns else land_refs
        send_sems, recv_sems = refs[nb + len(extra)], refs[nb + len(extra) + 1]
        pos = _mesh_pos()
        for k, e in enumerate(plan):
            _plan_copy(e, k, pos, pos, _peer(pos, e[1]), src_refs, land_refs, send_sems, recv_sems).start()
        refs[-1][...] = jnp.zeros_like(refs[-1])

    outs = pl.pallas_call(
        body, name=name,
        out_shape=[pltpu.SemaphoreType.DMA((nk,)), pltpu.SemaphoreType.DMA((nk,))] + [pltpu.HBM(a.shape, a.dtype) for a in srcs + lands]
        + [SDS((8, 128), f32)],
        in_specs=[HBM] * nb + [ANY] * len(extra), out_specs=[SEM, SEM] + [HBM] * nb + [pl.BlockSpec(memory_space=pltpu.VMEM)],
        input_output_aliases={i: 2 + i for i in range(nb)},
        compiler_params=pltpu.CompilerParams(has_side_effects=EFFECT),
    )(*[_hbm(a) for a in srcs + lands], *extra)
    return outs[0], outs[1], list(outs[2:2 + ns]), list(outs[2 + ns:2 + nb]), outs[-1]


def _push_wait(name, send_sems, recv_sems, srcs, lands, plan, after):
    ns, nb = len(srcs), len(srcs) + len(lands)

    def body(*refs):
        land_refs = refs[ns:nb]
        src_refs = refs[:ns] if ns else land_refs
        ssem, rsem = refs[nb], refs[nb + 1]
        pos = _mesh_pos()
        for k, e in enumerate(plan):
            peer = _peer(pos, e[1])
            _plan_copy(e, k, pos, pos, peer, src_refs, land_refs, ssem, rsem).wait_send()
            _plan_copy(e, k, pos, peer, pos, src_refs, land_refs, ssem, rsem).wait_recv()

    outs = pl.pallas_call(
        body, name=name, out_shape=[pltpu.HBM(a.shape, a.dtype) for a in srcs + lands],
        in_specs=[HBM] * nb + [SEM, SEM, ANY], out_specs=[HBM] * nb,
        input_output_aliases={i: i for i in range(nb)},
        compiler_params=pltpu.CompilerParams(has_side_effects=EFFECT),
    )(*srcs, *lands, send_sems, recv_sems, after)
    return list(outs[:ns]), list(outs[ns:])


def _after_tokens(small, tokens):
    for t in tokens:
        if t is not None:
            small = small + t[0:1, 0:1].reshape((1,) * small.ndim)
    return small


def _place_shard(name, me, shard, after=None):
    r, c = shard.shape
    tr = _row_tile(r)
    extra = [] if after is None else [after]

    def body(me_ref, s_ref, *rest):
        rest[-1][...] = s_ref[...].astype(bf16)

    return pl.pallas_call(
        body, name=name, out_shape=SDS((NDEV, r, c), bf16),
        grid_spec=pltpu.PrefetchScalarGridSpec(
            num_scalar_prefetch=1, grid=(r // tr,), in_specs=[pl.BlockSpec((tr, c), lambda i, me_ref: (i, 0))] + [ANY] * len(extra),
            out_specs=pl.BlockSpec((None, tr, c), lambda i, me_ref: (me_ref[0], i, 0))),
        compiler_params=_params(("parallel",)),
    )(me, shard, *extra)


def _row_tile(r, most=256):
    return max(t for t in range(16, most + 1, 16) if r % t == 0)


def _sibling_forward(name, lands):
    n = len(lands)
    nk = n * len(ICI_RELATIONS)

    def body(*refs):
        land_refs = refs[:n]
        send_sems, recv_sems = refs[2 * n:]
        pos = _mesh_pos()
        sib = _peer(pos, 1)

        def copy(a, i, frm, to):
            k = a * len(ICI_RELATIONS) + i
            slot = _flat(_peer(frm, ICI_RELATIONS[i]))
            return pltpu.make_async_remote_copy(src_ref=land_refs[a].at[slot], dst_ref=land_refs[a].at[slot], send_sem=send_sems.at[k],
                                                recv_sem=recv_sems.at[k], device_id=to, device_id_type=MESH)

        sends = [copy(a, i, pos, sib) for a in range(n) for i in range(len(ICI_RELATIONS))]
        for cp in sends:
            cp.start()
        for a in range(n):
            for i in range(len(ICI_RELATIONS)):
                copy(a, i, sib, pos).wait_recv()
        for cp in sends:
            cp.wait_send()

    outs = pl.pallas_call(
        body, name=name, out_shape=[SDS(a.shape, a.dtype) for a in lands], in_specs=[HBM] * n, out_specs=[HBM] * n,
        input_output_aliases={i: i for i in range(n)},
        scratch_shapes=[pltpu.SemaphoreType.DMA((nk,)), pltpu.SemaphoreType.DMA((nk,))],
    )(*lands)
    return list(outs)


def _sibling_swap(name, grads):
    n = len(grads)
    nchip = NDEV // 2

    def body(*refs):
        g_refs, out_refs = refs[:n], refs[n:2 * n]
        send_sems, recv_sems = refs[2 * n:]
        pos = _mesh_pos()
        sib = _peer(pos, 1)
        sends = []
        for a in range(n):
            for q in range(nchip):
                k = a * nchip + q
                sends.append(pltpu.make_async_remote_copy(src_ref=g_refs[a].at[2 * q + sib[2]], dst_ref=out_refs[a].at[q],
                                                          send_sem=send_sems.at[k], recv_sem=recv_sems.at[k], device_id=sib,
                                                          device_id_type=MESH))
        for cp in sends:
            cp.start()
        for cp in sends:
            cp.wait()

    return pl.pallas_call(
        body, name=name, out_shape=[SDS((nchip,) + a.shape[1:], a.dtype) for a in grads], in_specs=[HBM] * n, out_specs=[HBM] * n,
        scratch_shapes=[pltpu.SemaphoreType.DMA((n * nchip,)), pltpu.SemaphoreType.DMA((n * nchip,))],
    )(*grads)


def _pair_sum(name, core, grad, recv, tr):
    _, r, c = grad.shape
    nchip = NDEV // 2

    def body(core_ref, g_ref, r_ref, o_ref):
        o_ref[...] = (g_ref[...].astype(f32) + r_ref[...].astype(f32)).astype(o_ref.dtype)

    return pl.pallas_call(
        body, name=name, out_shape=SDS((nchip, r, c), grad.dtype),
        grid_spec=pltpu.PrefetchScalarGridSpec(
            num_scalar_prefetch=1, grid=(nchip, r // tr),
            in_specs=[pl.BlockSpec((None, tr, c), lambda q, i, core_ref: (2 * q + core_ref[0], i, 0)),
                      pl.BlockSpec((None, tr, c), lambda q, i, core_ref: (q, i, 0))],
            out_specs=pl.BlockSpec((None, tr, c), lambda q, i, core_ref: (q, i, 0))),
        compiler_params=_params(("parallel", "parallel")),
    )(core, grad, recv)


def _adamw_math(g, w, m, v):
    mn = ADAM_B1 * m + (1.0 - ADAM_B1) * g
    vn = ADAM_B2 * v + (1.0 - ADAM_B2) * jnp.square(g)
    m_hat = mn / (1.0 - ADAM_B1 ** ADAM_STEP)
    v_hat = vn / (1.0 - ADAM_B2 ** ADAM_STEP)
    return -ADAM_LR * (m_hat / (jnp.sqrt(v_hat) + ADAM_EPS) + ADAM_WD * w), mn, vn


def _chip_sum(name, chip, pairs, recv, tr, wmv=None):
    nchip, r, c = pairs.shape
    n_out = 1 if wmv is None else 4

    def body(chip_ref, p_ref, r_ref, *rest):
        mine = chip_ref[0]
        acc = jnp.zeros((tr, c), f32)
        for q in range(nchip):
            acc = acc + jnp.where(mine == q, p_ref[q].astype(f32), r_ref[q].astype(f32))
        outs = rest[-n_out:]
        outs[0][...] = acc
        if wmv is not None:
            w_ref, m_ref, v_ref = rest[:3]
            outs[1][...], outs[2][...], outs[3][...] = _adamw_math(acc, w_ref[...], m_ref[...], v_ref[...])

    blk = pl.BlockSpec((tr, c), lambda i, chip_ref: (i, 0))
    outs = pl.pallas_call(
        body, name=name, out_shape=[SDS((r, c), f32)] * n_out,
        grid_spec=pltpu.PrefetchScalarGridSpec(
            num_scalar_prefetch=1, grid=(r // tr,),
            in_specs=[pl.BlockSpec((nchip, tr, c), lambda i, chip_ref: (0, i, 0))] * 2 + [blk] * (n_out - 1),
            out_specs=[blk] * n_out),
        compiler_params=_params(("parallel",)),
    )(chip, pairs, recv, *(wmv or ()))
    return outs[0] if wmv is None else outs


def _matmul(name, a, b, out_shape, out_dtype, grid, a_spec, b_spec, o_spec, dims, after=None):
    ksteps = grid[2]
    acc_shape = tuple(d for d in o_spec.block_shape if d is not None)
    extra = [] if after is None else [after]

    def body(a_ref, b_ref, *rest):
        o_ref, acc = rest[len(extra)], rest[len(extra) + 1:]
        prod = lax.dot_general(a_ref[...], b_ref[...], (dims, ((), ())), preferred_element_type=f32)
        if ksteps == 1:
            o_ref[...] = prod.astype(o_ref.dtype)
        else:
            k = pl.program_id(2)

            @pl.when(k == 0)
            def _():
                acc[0][...] = prod

            @pl.when(k > 0)
            def _():
                acc[0][...] += prod

            @pl.when(k == ksteps - 1)
            def _():
                o_ref[...] = acc[0][...].astype(o_ref.dtype)

    return pl.pallas_call(
        body, name=name, grid=grid, in_specs=[a_spec, b_spec] + [ANY] * len(extra), out_specs=o_spec,
        out_shape=SDS(out_shape, out_dtype), scratch_shapes=[] if ksteps == 1 else [pltpu.VMEM(acc_shape, f32)],
        compiler_params=_params(("parallel", "parallel", "arbitrary")),
    )(a, b, *extra)


NN, NT, TN = ((1,), (0,)), ((1,), (1,)), ((0,), (0,))


def _rowwise(name, fn, n_rows, tm, rows, fulls, row_outs, acc_outs=(), ncol=1):
    assert ncol == 1 or not acc_outs
    nr, nf, no, na = len(rows), len(fulls), len(row_outs), len(acc_outs)
    in_specs = [pl.BlockSpec((tm, w), functools.partial(lambda i, j, cb: (i, cb + j), cb=cb)) for (_, w, cb) in rows]
    in_specs += [pl.BlockSpec(a.shape, functools.partial(lambda i, j, nd: (0,) * nd, nd=a.ndim)) for a in fulls]
    out_shape = [SDS((n_rows, w), dt) for (w, dt) in row_outs] + [SDS(s, f32) for s in acc_outs]
    out_specs = [pl.BlockSpec((tm, w // ncol), lambda i, j: (i, j)) for (w, _) in row_outs]
    out_specs += [pl.BlockSpec(s, functools.partial(lambda i, j, nd: (0,) * nd, nd=len(s))) for s in acc_outs]

    def body(*refs):
        if na:
            @pl.when(pl.program_id(0) == 0)
            def _():
                for r in refs[nr + nf + no:]:
                    r[...] = jnp.zeros(r.shape, r.dtype)
        fn(refs[:nr], refs[nr:nr + nf], refs[nr + nf:nr + nf + no], refs[nr + nf + no:])

    return pl.pallas_call(
        body, name=name, grid=(n_rows // tm, ncol), in_specs=in_specs, out_specs=out_specs, out_shape=out_shape,
        compiler_params=_params(("arbitrary" if na else "parallel", "arbitrary" if na else "parallel")),
    )(*[r[0] for r in rows], *fulls)


def _rms(x):
    r = lax.rsqrt(jnp.mean(x * x, axis=-1, keepdims=True) + EPS)
    return r, x * r


def _colsum(v):
    return jnp.sum(v, axis=0, keepdims=True)


def _shift_down(u, row, k):
    return jnp.where(row >= k, pltpu.roll(u, k, 0), 0.0)


def _shift_up(u, row, k):
    n = u.shape[0]
    return jnp.where(row < n - k, pltpu.roll(u, n - k, 0), 0.0)


def _conv_fwd(proj, conv_w, s, dc):
    nb = dc // 128

    def body(ab_ref, ac_ref, ax_ref, w_ref, z_ref):
        u = ac_ref[...] * ax_ref[...]
        row = lax.broadcasted_iota(jnp.int32, u.shape, 0)
        w = w_ref[...]
        cv = w[0:1] * _shift_down(u, row, 2) + w[1:2] * _shift_down(u, row, 1) + w[2:3] * u
        z_ref[...] = (ab_ref[...] * cv).astype(z_ref.dtype)

    col = lambda off: pl.BlockSpec((s, 128), functools.partial(lambda j, off: (0, off + j), off=off))
    return pl.pallas_call(
        body, name="conv_fwd", grid=(nb,), in_specs=[col(0), col(nb), col(2 * nb), pl.BlockSpec((3, 128), lambda j: (0, j))],
        out_specs=pl.BlockSpec((s, 128), lambda j: (0, j)), out_shape=SDS((s, dc), bf16), compiler_params=_params(("parallel",)),
    )(proj, proj, proj, conv_w)


def _conv_bwd(proj, conv_w, dz, s, dc):
    nb = dc // 128

    def body(ab_ref, ac_ref, ax_ref, w_ref, dz_ref, dab_ref, dac_ref, dax_ref, dw_ref):
        ab, ac, ax, dzv = ab_ref[...], ac_ref[...], ax_ref[...], dz_ref[...]
        u = ac * ax
        row = lax.broadcasted_iota(jnp.int32, u.shape, 0)
        w = w_ref[...]
        u1, u2 = _shift_down(u, row, 1), _shift_down(u, row, 2)
        cv = w[0:1] * u2 + w[1:2] * u1 + w[2:3] * u
        dcv = dzv * ab
        dab_ref[...] = (dzv * cv).astype(dab_ref.dtype)
        du = w[2:3] * dcv + w[1:2] * _shift_up(dcv, row, 1) + w[0:1] * _shift_up(dcv, row, 2)
        dac_ref[...] = (du * ax).astype(dac_ref.dtype)
        dax_ref[...] = (du * ac).astype(dax_ref.dtype)
        dw_ref[0:1, :] = _colsum(dcv * u2)
        dw_ref[1:2, :] = _colsum(dcv * u1)
        dw_ref[2:3, :] = _colsum(dcv * u)

    col = lambda off: pl.BlockSpec((s, 128), functools.partial(lambda j, off: (0, off + j), off=off))
    blk = pl.BlockSpec((s, 128), lambda j: (0, j))
    return pl.pallas_call(
        body, name="conv_bwd", grid=(nb,),
        in_specs=[col(0), col(nb), col(2 * nb), pl.BlockSpec((3, 128), lambda j: (0, j)), blk],
        out_specs=[blk, blk, blk, pl.BlockSpec((3, 128), lambda j: (0, j))],
        out_shape=[SDS((s, dc), bf16)] * 3 + [SDS((3, dc), f32)], compiler_params=_params(("parallel",)),
    )(proj, proj, proj, conv_w, dz)


def _level_masks():
    t = np.arange(CHUNK)[:, None]
    s = np.arange(CHUNK)[None, :]
    m = np.stack([((t & h) != 0) & ((s & h) == 0) & (t // (2 * h) == s // (2 * h)) for h in LEVELS]).astype(np.float32)
    return jnp.asarray(m), jnp.asarray(m.transpose(0, 2, 1))


def _cumsum_rows(x, row):
    for sh in (1, 2, 4, 8, 16, 32):
        x = x + jnp.where(row >= sh, pltpu.roll(x, sh, 0), 0.0)
    return x


def _rev_cumsum_rows(x, row):
    n = x.shape[0]
    for sh in (1, 2, 4, 8, 16, 32):
        x = x + jnp.where(row < n - sh, pltpu.roll(x, n - sh, 0), 0.0)
    return x


def _chunk_terms(qp, fl, lb):
    row = lax.broadcasted_iota(jnp.int32, qp.shape, 0)
    sig = _sigmoid(fl)
    f = lb + (1.0 - lb) * sig
    k = 1.0 - f
    sq = _sigmoid(qp)
    qh = qp * sq
    b = _cumsum_rows(jnp.log(f), row)
    sub = lax.broadcasted_iota(jnp.int32, (CHUNK // 8, 8, DK), 1)
    b8 = b.reshape(CHUNK // 8, 8, DK)
    us, exs, ups = [], [], []
    for m in LEVELS:
        sb = 2 * m
        if sb >= 8:
            b3 = b.reshape(CHUNK // sb, sb, DK)
            bref = jnp.broadcast_to(b3[:, m - 1:m, :], b3.shape).reshape(CHUNK, DK)
        else:
            bref8 = None
            for j in range(8 // sb):
                cand = jnp.broadcast_to(b8[:, j * sb + m - 1:j * sb + m, :], b8.shape)
                bref8 = cand if bref8 is None else jnp.where(sub >= j * sb, cand, bref8)
            bref = bref8.reshape(CHUNK, DK)
        up = (row & m) != 0
        ex = jnp.exp(jnp.where(up, b - bref, bref - b))
        us.append((jnp.where(up, qh, k) * ex).astype(bf16))
        exs.append(ex)
        ups.append(up)
    blast = b[CHUNK - 1:CHUNK, :]
    eb, ebl = jnp.exp(b), jnp.exp(blast - b)
    return dict(sig=sig, f=f, k=k, sq=sq, qh=qh, u=jnp.stack(us), ex=exs, up=ups, eb=eb, ebl=ebl, qt=qh * eb, kt=k * ebl,
                el=jnp.exp(blast), row=row)


def _scores(t, mask):
    pl_ = jnp.einsum("ltk,lsk->lts", t["u"], t["u"], preferred_element_type=f32)
    p = jnp.sum(pl_ * mask, axis=0)
    r = lax.broadcasted_iota(jnp.int32, (CHUNK, CHUNK), 0)
    c = lax.broadcasted_iota(jnp.int32, (CHUNK, CHUNK), 1)
    diag = jnp.sum(t["qh"] * t["k"], axis=-1, keepdims=True)
    return p + jnp.where(r == c, diag, 0.0)


def _hgrn_fwd(proj, lb_param, gnorm, s, dv_total, tb):
    nchunk = tb // CHUNK
    masks, _ = _level_masks()
    q0, f0, v0, g0 = 3 * HEADS, 4 * HEADS, 5 * HEADS, 6 * HEADS

    def body(q_ref, f_ref, v_ref, g_ref, lb_ref, gn_ref, mask_ref, og_ref, o_ref, st_ref, state):
        @pl.when(pl.program_id(1) == 0)
        def _():
            state[...] = jnp.zeros_like(state)

        lbp = lb_ref[...]
        lb_all = _sigmoid(lbp[0:1, :] - lbp[1:2, :])
        mask = mask_ref[...]
        for i, hh in [(i, hh) for i in range(nchunk) for hh in range(HEADS_PER_STEP)]:
            rs, cs = pl.ds(i * CHUNK, CHUNK), pl.ds(hh * DK, DK)
            t = _chunk_terms(q_ref[rs, cs], f_ref[rs, cs], lb_all[:, hh * DK:(hh + 1) * DK])
            v = v_ref[rs, cs]
            vb = v.astype(bf16)
            st = state[hh]
            st_ref[i, hh] = st
            p = _scores(t, mask)
            o = jnp.dot(p.astype(bf16), vb, preferred_element_type=f32)
            o += lax.dot_general(t["qt"].astype(bf16), st.astype(bf16), (NT, ((), ())), preferred_element_type=f32)
            state[hh] = st * t["el"] + lax.dot_general(vb, t["kt"].astype(bf16), (TN, ((), ())), preferred_element_type=f32)
            o_ref[rs, cs] = o
            r, oh = _rms(o)
            g = g_ref[rs, cs]
            og_ref[rs, cs] = (oh * gn_ref[...] * (g * _sigmoid(g))).astype(og_ref.dtype)

    hp, wide = HEADS_PER_STEP, HEADS_PER_STEP * DK
    col = lambda off: pl.BlockSpec((tb, wide), functools.partial(lambda h, t, off: (t, off + h), off=off // hp))
    blk = pl.BlockSpec((tb, wide), lambda h, t: (t, h))
    return pl.pallas_call(
        body, name="hgrn_fwd", grid=(HEADS // hp, s // tb),
        in_specs=[col(q0), col(f0), col(v0), col(g0), pl.BlockSpec((2, wide), lambda h, t: (0, h)),
                  pl.BlockSpec((1, DK), lambda h, t: (0, 0)), pl.BlockSpec(masks.shape, lambda h, t: (0, 0, 0))],
        out_specs=[blk, blk, pl.BlockSpec((nchunk, hp, DK, DK), lambda h, t: (t, h, 0, 0))],
        out_shape=[SDS((s, dv_total), bf16), SDS((s, dv_total), f32), SDS((s // CHUNK, HEADS, DK, DK), f32)],
        scratch_shapes=[pltpu.VMEM((hp, DK, DK), f32)], compiler_params=_params(("parallel", "arbitrary")),
    )(proj, proj, proj, proj, lb_param, gnorm, masks)


def _hgrn_bwd(proj, lb_param, gnorm, o_saved, states, dog, s, dv_total, tb):
    nchunk = tb // CHUNK
    nt = s // tb
    nc_total = s // CHUNK
    masks, masks_t = _level_masks()
    q0, f0, v0, g0 = 3 * HEADS, 4 * HEADS, 5 * HEADS, 6 * HEADS

    def body(q_ref, f_ref, v_ref, g_ref, lb_ref, gn_ref, mask_ref, maskt_ref, o_ref, dog_ref, st_ref, stn_ref,
             dq_ref, df_ref, dv_ref, dg_ref, dlb_ref, dgn_ref, gstate):
        h_id, t_id = pl.program_id(0), pl.program_id(1)

        @pl.when(t_id == 0)
        def _():
            gstate[...] = jnp.zeros_like(gstate)
            dlb_ref[...] = jnp.zeros_like(dlb_ref)

        @pl.when((t_id == 0) & (h_id == 0))
        def _():
            dgn_ref[...] = jnp.zeros_like(dgn_ref)

        lbp = lb_ref[...]
        lb_all = _sigmoid(lbp[0:1, :] - lbp[1:2, :])
        mask, maskt = mask_ref[...], maskt_ref[...]
        gn = gn_ref[...]
        for i, hh in [(i, hh) for i in reversed(range(nchunk)) for hh in range(HEADS_PER_STEP)]:
            rs, cs = pl.ds(i * CHUNK, CHUNK), pl.ds(hh * DK, DK)
            lb = lb_all[:, hh * DK:(hh + 1) * DK]
            qp, fl, v, g = q_ref[rs, cs], f_ref[rs, cs], v_ref[rs, cs], g_ref[rs, cs]
            t = _chunk_terms(qp, fl, lb)
            vb = v.astype(bf16)
            st0 = st_ref[i, hh]
            st1 = st_ref[i + 1, hh] if i + 1 < nchunk else stn_ref[0, hh]
            gt = gstate[hh]
            o = o_ref[rs, cs]
            r, oh = _rms(o)
            sg = _sigmoid(g)
            dog_v = dog_ref[rs, cs]
            dg_ref[rs, cs] = (dog_v * (oh * gn) * _dsilu(g, sg)).astype(dg_ref.dtype)
            don = dog_v * (g * sg)
            dgn_ref[...] += _colsum(don * oh)
            doh = don * gn
            do = r * (doh - oh * jnp.mean(doh * oh, axis=-1, keepdims=True))
            dob = do.astype(bf16)
            d = lax.dot_general(dob, vb, (NT, ((), ())), preferred_element_type=f32)
            dt = lax.dot_general(vb, dob, (NT, ((), ())), preferred_element_type=f32)
            z = (mask * d[None] + maskt * dt[None]).astype(bf16)
            rr = jnp.einsum("lts,lsk->ltk", z, t["u"], preferred_element_type=f32)
            dq = jnp.zeros((CHUNK, DK), f32)
            dk = jnp.zeros((CHUNK, DK), f32)
            qdk = jnp.zeros((CHUNK, DK), f32)
            for li in range(len(LEVELS)):
                du = t["ex"][li] * rr[li]
                dq += jnp.where(t["up"][li], du, 0.0)
                dk += jnp.where(t["up"][li], 0.0, du)
                e = t["u"][li].astype(f32) * rr[li]
                qdk += jnp.where(t["up"][li], e, -e)
            dd = jnp.sum(do * v, axis=-1, keepdims=True)
            dq += dd * t["k"]
            dk += dd * t["qh"]
            gtb = gt.astype(bf16)
            ktb, qtb = t["kt"].astype(bf16), t["qt"].astype(bf16)
            dq_in = jnp.dot(dob, st0.astype(bf16), preferred_element_type=f32)
            dk_in = jnp.dot(vb, gtb, preferred_element_type=f32)
            dq += t["eb"] * dq_in
            dk += t["ebl"] * dk_in
            qdk += qtb.astype(f32) * dq_in - ktb.astype(f32) * dk_in
            p = _scores(t, mask)
            dvv = lax.dot_general(p.astype(bf16), dob, (TN, ((), ())), preferred_element_type=f32)
            dvv += lax.dot_general(ktb, gtb, (NT, ((), ())), preferred_element_type=f32)
            dv_ref[rs, cs] = dvv.astype(dv_ref.dtype)
            a_end = _colsum(gtb.astype(f32) * st1)
            dlf = _rev_cumsum_rows(qdk, t["row"]) + a_end
            dfv = dlf / t["f"] - dk
            df_ref[rs, cs] = (dfv * (1.0 - lb) * t["sig"] * (1.0 - t["sig"])).astype(df_ref.dtype)
            dlb_ref[:, cs] += _colsum(dfv * (1.0 - t["sig"]))
            dq_ref[rs, cs] = (dq * _dsilu(qp, t["sq"])).astype(dq_ref.dtype)
            gstate[hh] = gt * t["el"] + lax.dot_general(dob, qtb, (TN, ((), ())), preferred_element_type=f32)

    hp, wide = HEADS_PER_STEP, HEADS_PER_STEP * DK
    rev = lambda t: nt - 1 - t
    col = lambda off: pl.BlockSpec((tb, wide), functools.partial(lambda h, t, off: (rev(t), off + h), off=off // hp))
    blk = pl.BlockSpec((tb, wide), lambda h, t: (rev(t), h))
    nxt = lambda h, t: (jnp.minimum((rev(t) + 1) * nchunk, nc_total - 1), h, 0, 0)
    return pl.pallas_call(
        body, name="hgrn_bwd", grid=(HEADS // hp, nt),
        in_specs=[col(q0), col(f0), col(v0), col(g0), pl.BlockSpec((2, wide), lambda h, t: (0, h)),
                  pl.BlockSpec((1, DK), lambda h, t: (0, 0)), pl.BlockSpec(masks.shape, lambda h, t: (0, 0, 0)),
                  pl.BlockSpec(masks.shape, lambda h, t: (0, 0, 0)), blk, blk,
                  pl.BlockSpec((nchunk, hp, DK, DK), lambda h, t: (rev(t), h, 0, 0)),
                  pl.BlockSpec((1, hp, DK, DK), nxt)],
        out_specs=[blk, blk, blk, blk, pl.BlockSpec((1, wide), lambda h, t: (0, h)), pl.BlockSpec((1, DK), lambda h, t: (0, 0))],
        out_shape=[SDS((s, dv_total), bf16)] * 4 + [SDS((1, HEADS * DK), f32), SDS((1, DK), f32)],
        scratch_shapes=[pltpu.VMEM((hp, DK, DK), f32)], compiler_params=_params(("arbitrary", "arbitrary")),
    )(proj, proj, proj, proj, lb_param, gnorm, masks, masks_t, o_saved, dog, states, states)


def _adamw(name, g, w, m, v):
    r, c = w.shape
    tr = r
    for cand in (256, 128, 64, 32, 16, 8):
        if r % cand == 0 and r > cand:
            tr = cand
            break

    def body(g_ref, w_ref, m_ref, v_ref, d_ref, mo_ref, vo_ref):
        d_ref[...], mo_ref[...], vo_ref[...] = _adamw_math(g_ref[...], w_ref[...], m_ref[...], v_ref[...])

    blk = pl.BlockSpec((tr, c), lambda i: (i, 0))
    return pl.pallas_call(
        body, name=name, grid=(r // tr,), in_specs=[blk] * 4, out_specs=[blk] * 3, out_shape=[SDS((r, c), f32)] * 3,
        compiler_params=_params(("parallel",)),
    )(g, w, m, v)


def _ffn_in(h2, w_gt, w_ut, tm, ffb):
    s, d = h2.shape
    dff = w_gt.shape[0]

    def body(a_ref, wg_ref, wu_ref, dg_ref, du_ref, act_ref):
        a = a_ref[...]
        g = lax.dot_general(a, wg_ref[...], (NT, ((), ())), preferred_element_type=f32)
        u = lax.dot_general(a, wu_ref[...], (NT, ((), ())), preferred_element_type=f32)
        sg = _sigmoid(g)
        silu = g * sg
        dg_ref[...] = (u * _dsilu(g, sg)).astype(bf16)
        du_ref[...] = silu.astype(bf16)
        act_ref[...] = (silu * u).astype(bf16)

    w_spec = pl.BlockSpec((ffb, d), lambda j, i: (j, 0))
    o_spec = pl.BlockSpec((tm, ffb), lambda j, i: (i, j))
    return pl.pallas_call(
        body, name="ffn_in", grid=(dff // ffb, s // tm), in_specs=[pl.BlockSpec((tm, d), lambda j, i: (i, 0)), w_spec, w_spec],
        out_specs=[o_spec] * 3, out_shape=[SDS((s, dff), bf16)] * 3, compiler_params=_params(("parallel", "parallel")),
    )(h2, w_gt, w_ut)


def _ffn_down_bwd(dff_out, w_d, act_dg, act_du, tm, ffb):
    s, d = dff_out.shape
    dff = w_d.shape[0]

    def body(a_ref, w_ref, fg_ref, fu_ref, dg_ref, du_ref):
        da = lax.dot_general(a_ref[...], w_ref[...], (NT, ((), ())), preferred_element_type=f32)
        dg_ref[...] = (da * fg_ref[...].astype(f32)).astype(bf16)
        du_ref[...] = (da * fu_ref[...].astype(f32)).astype(bf16)

    t_spec = pl.BlockSpec((tm, ffb), lambda j, i: (i, j))
    return pl.pallas_call(
        body, name="d_ffn_down_in", grid=(dff // ffb, s // tm),
        in_specs=[pl.BlockSpec((tm, d), lambda j, i: (i, 0)), pl.BlockSpec((ffb, d), lambda j, i: (j, 0)), t_spec, t_spec],
        out_specs=[t_spec] * 2, out_shape=[SDS((s, dff), bf16)] * 2, compiler_params=_params(("parallel", "parallel")),
    )(dff_out, w_d, act_dg, act_du)


def _branch_merge(z_a, og, w_co, w_ho, proj, tm, gate_a0, gate_b0):
    s, dc = z_a.shape
    nsh, _, n = w_co.shape

    def body(za_ref, og_ref, wa_ref, wb_ref, ga_ref, gb_ref, m_ref, sa_ref, sb_ref, fa_ref, fb_ref):
        ya = jnp.dot(za_ref[...], wa_ref[...], preferred_element_type=f32)
        yb = jnp.dot(og_ref[...], wb_ref[...], preferred_element_type=f32)
        sa, sb_ = _sigmoid(ga_ref[...]), _sigmoid(gb_ref[...])
        m_ref[...] = (sa * ya + sb_ * yb).astype(bf16)
        sa_ref[...] = sa.astype(bf16)
        sb_ref[...] = sb_.astype(bf16)
        fa_ref[...] = (ya * sa * (1.0 - sa)).astype(bf16)
        fb_ref[...] = (yb * sb_ * (1.0 - sb_)).astype(bf16)

    a_spec = pl.BlockSpec((tm, dc), lambda i, j: (i, 0))
    w_spec = pl.BlockSpec((None, dc, n), lambda i, j: (j, 0, 0))
    gate = lambda c0: pl.BlockSpec((tm, n), functools.partial(lambda i, j, cb: (i, cb + j), cb=c0 // n))
    o_spec = pl.BlockSpec((tm, n), lambda i, j: (i, j))
    return pl.pallas_call(
        body, name="branch_merge", grid=(s // tm, nsh), in_specs=[a_spec, a_spec, w_spec, w_spec, gate(gate_a0), gate(gate_b0)],
        out_specs=[o_spec] * 5, out_shape=[SDS((s, nsh * n), bf16)] * 5, compiler_params=_params(("parallel", "parallel")),
    )(z_a, og, w_co, w_ho, proj, proj)


def _d_branch_merge(dmo, w_o, factors, tm):
    s, d = dmo.shape
    n = d // 2

    def body(a_ref, w_ref, sa_ref, sb_ref, fa_ref, fb_ref, dya_ref, dyb_ref, dga_ref, dgb_ref):
        dm = lax.dot_general(a_ref[...], w_ref[...], (NT, ((), ())), preferred_element_type=f32)
        for f_ref, o_ref in ((sa_ref, dya_ref), (sb_ref, dyb_ref), (fa_ref, dga_ref), (fb_ref, dgb_ref)):
            o_ref[...] = (dm * f_ref[...].astype(f32)).astype(bf16)

    t_spec = pl.BlockSpec((tm, n), lambda j, i: (i, j))
    return pl.pallas_call(
        body, name="d_branch_merge", grid=(d // n, s // tm),
        in_specs=[pl.BlockSpec((tm, d), lambda j, i: (i, 0)), pl.BlockSpec((n, d), lambda j, i: (j, 0))] + [t_spec] * 4,
        out_specs=[t_spec] * 4, out_shape=[SDS((s, d), bf16)] * 4, compiler_params=_params(("parallel", "parallel")),
    )(dmo, w_o, *factors)


def _local_step(x, tgt, mod, g_mix, g_ffn, g_fin, lb_param, gnorm, conv_w, get_w, put_g, sent):
    s, d = x.shape
    dc = d // 2
    tm = min(512, s)
    tm2 = min(1024, s)
    te = min(256, s)
    tb = min(256, s)
    mt = s // tm
    nb = 512
    sh_m, sc_m, gt_m, sh_f, sc_f, gt_f = [mod[i] for i in range(N_MOD)]
    dh2 = d // 2

    def e1(rows, fulls, outs, accs):
        xv = rows[0][...]
        g, sc, sh = [r[...] for r in fulls]
        _, xh = _rms(xv)
        outs[0][...] = (xh * g * (1.0 + sc) + sh).astype(bf16)

    h, = _rowwise("prenorm_mix", e1, s, te, [(x, d, 0)], [g_mix, sc_m, sh_m], [(d, bf16)])
    w_in, = get_w("in", h)
    nsh, _, win_sh = w_in.shape
    proj = _matmul("proj", h, w_in, (s, nsh * win_sh), f32, (nsh, s // tm2, 1), pl.BlockSpec((tm2, d), lambda j, i, k: (i, 0)),
                   pl.BlockSpec((None, d, win_sh), lambda j, i, k: (j, 0, 0)), pl.BlockSpec((tm2, win_sh), lambda j, i, k: (i, j)), NN)
    z_a = _conv_fwd(proj, conv_w, s, dc)
    og, o_saved, states = _hgrn_fwd(proj, lb_param, gnorm, s, dc, tb)
    w_co, w_ho, w_o = get_w("mix", og)

    gate_a0, gate_b0 = 7 * dh2, 9 * dh2
    merged, *merge_factors = _branch_merge(z_a, og, w_co, w_ho, proj, tm2, gate_a0, gate_b0)
    mo = _matmul("mix_out", merged, w_o, (s, d), f32, (2, s // tm2, 1), pl.BlockSpec((tm2, d), lambda j, i, k: (i, 0)),
                 pl.BlockSpec((d, dh2), lambda j, i, k: (0, j)), pl.BlockSpec((tm2, dh2), lambda j, i, k: (i, j)), NN)

    def e6(rows, fulls, outs, accs):
        xv, mov = rows[0][...], rows[1][...]
        gt, g, sc, sh = [r[...] for r in fulls]
        x1 = xv + gt * mov
        outs[0][...] = x1
        _, xh = _rms(x1)
        outs[1][...] = (xh * g * (1.0 + sc) + sh).astype(bf16)

    x1, h2 = _rowwise("prenorm_ffn", e6, s, te, [(x, d, 0), (mo, d, 0)], [gt_m, g_ffn, sc_f, sh_f], [(d, f32), (d, bf16)])
    w_gt, w_ut, w_d = get_w("ffn", h2)
    dff_ = w_d.shape[0]
    act_dg, act_du, act = _ffn_in(h2, w_gt, w_ut, tm2, nb)
    ff = _matmul("ffn_down", act, w_d, (s, d), f32, (2, mt, 1), pl.BlockSpec((tm, dff_), lambda j, i, k: (i, 0)),
                 pl.BlockSpec((dff_, dh2), lambda j, i, k: (0, j)), pl.BlockSpec((tm, dh2), lambda j, i, k: (i, j)), NN)

    def e9(rows, fulls, outs, accs):
        x1v, ffv, tv = [r[...] for r in rows]
        gt, gf = fulls[0][...], fulls[1][...]
        x2 = x1v + gt * ffv
        r, xh = _rms(x2)
        err = xh * gf - tv
        accs[0][...] += 0.5 * jnp.sum(jnp.mean(err * err, axis=-1, keepdims=True), axis=0, keepdims=True)
        dy = err / d
        accs[1][...] += _colsum(dy * xh)
        dxh = dy * gf
        dx2 = r * (dxh - xh * jnp.mean(dxh * xh, axis=-1, keepdims=True))
        outs[0][...] = dx2
        outs[1][...] = (dx2 * gt).astype(bf16)
        accs[2][...] += _colsum(dx2 * ffv)

    dx2, dff, loss_acc, dg_fin, dgt_f = _rowwise("loss_head", e9, s, te, [(x1, d, 0), (ff, d, 0), (tgt, d, 0)], [gt_f, g_fin],
                                                 [(d, f32), (d, bf16)], [(1, 128), (1, d), (1, d)])
    dgg, duu = _ffn_down_bwd(dff, w_d, act_dg, act_du, tm, dff_ // 4)

    def wgrad_rows(name, a, b, n_out):
        kb = 512
        return _matmul(name, a, b, (n_out, d), bf16, (n_out // kb, 2, 1), pl.BlockSpec((s, kb), lambda i, j, k: (0, i)),
                       pl.BlockSpec((s, dh2), lambda i, j, k: (0, j)), pl.BlockSpec((kb, dh2), lambda i, j, k: (i, j)), TN)

    gw_d = wgrad_rows("gw_ffn_down", act, dff, dff_)

    def ffn_in_bwd(name, a, w, after=None):
        return _matmul(name, a, w, (s, d), f32, (2, mt, 1), pl.BlockSpec((tm, dff_), lambda j, i, k: (i, 0)),
                       pl.BlockSpec((dff_, dh2), lambda j, i, k: (0, j)), pl.BlockSpec((tm, dh2), lambda j, i, k: (i, j)), NN,
                       after=after)

    dh2b = ffn_in_bwd("d_ffn_up_in", duu, w_ut)
    gw_ut = wgrad_rows("gw_ffn_up", duu, h2, dff_)
    gw_gt = wgrad_rows("gw_ffn_gate", dgg, h2, dff_)
    dh2a = ffn_in_bwd("d_ffn_gate_in", dgg, w_gt, after=put_g("ffn", [gw_gt, gw_ut, gw_d]))
    sc_f_late = _after_tokens(sc_f, [sent("ffn", dh2a)])

    def b5(rows, fulls, outs, accs):
        da, db, x1v, dx2v, mov = [r[...] for r in rows]
        sc, g, gt = [r[...] for r in fulls]
        dh = da + db
        r, xh = _rms(x1v)
        accs[0][...] += _colsum(dh)
        accs[1][...] += _colsum(dh * (xh * g))
        dn = dh * (1.0 + sc)
        accs[2][...] += _colsum(dn * xh)
        dxh = dn * g
        dx1 = dx2v + r * (dxh - xh * jnp.mean(dxh * xh, axis=-1, keepdims=True))
        outs[0][...] = dx1
        accs[3][...] += _colsum(dx1 * mov)
        outs[1][...] = (dx1 * gt).astype(bf16)

    dx1, dmo, dsh_f, dsc_f, dg_ffn, dgt_m = _rowwise(
        "d_prenorm_ffn", b5, s, te, [(dh2a, d, 0), (dh2b, d, 0), (x1, d, 0), (dx2, d, 0), (mo, d, 0)], [sc_f_late, g_ffn, gt_m],
        [(d, f32), (d, bf16)], [(1, d)] * 4)
    dya, dyb, dga, dgb = _d_branch_merge(dmo, w_o, merge_factors, tm)
    gw_o = wgrad_rows("gw_mix_out", merged, dmo, d)

    def out_proj_bwd(name, dy, w):
        return _matmul(name, dy, w, (s, dc), f32, (1, s // tm2, nsh), pl.BlockSpec((tm2, d // nsh), lambda j, i, k: (i, k)),
                       pl.BlockSpec((None, dc, d // nsh), lambda j, i, k: (k, 0, 0)), pl.BlockSpec((tm2, dc), lambda j, i, k: (i, 0)), NT)

    def out_proj_wgrad(name, a, dy):
        return _matmul(name, a, dy, (nsh, dc, d // nsh), bf16, (1, nsh, 1), pl.BlockSpec((s, dc), lambda i, j, k: (0, 0)),
                       pl.BlockSpec((s, d // nsh), lambda i, j, k: (0, j)), pl.BlockSpec((None, dc, d // nsh), lambda i, j, k: (j, 0, 0)), TN)

    dz_a = out_proj_bwd("d_conv_out_in", dya, w_co)
    dog = out_proj_bwd("d_hgrn_out_in", dyb, w_ho)
    gw_co = out_proj_wgrad("gw_conv_out", z_a, dya)
    gw_ho = out_proj_wgrad("gw_hgrn_out", og, dyb)
    conv_w_late = _after_tokens(conv_w, [put_g("mix", [gw_co, gw_ho, gw_o])])
    dab, dac, dax, dconv_w = _conv_bwd(proj, conv_w_late, dz_a, s, dc)
    lb_param_late = _after_tokens(lb_param, [sent("mix", dab)])
    dq, dfl, dvi, dgo, dlb, dgn = _hgrn_bwd(proj, lb_param_late, gnorm, o_saved, states, dog, s, dc, tb)
    dproj = jnp.concatenate([dab, dac, dax, dq, dfl, dvi, dgo, dga, dgb], axis=1)
    gw_in = _matmul("gw_proj", h, dproj, (nsh, d, win_sh), bf16, (nsh, d // 512, 1), pl.BlockSpec((s, 512), lambda j, i, k: (0, i)),
                    pl.BlockSpec((s, win_sh), lambda j, i, k: (0, j)), pl.BlockSpec((None, 512, win_sh), lambda j, i, k: (j, i, 0)), TN)
    dh = _matmul("d_proj_in", dproj, w_in, (s, d), f32, (1, s // tm2, nsh), pl.BlockSpec((tm2, win_sh), lambda j, i, k: (i, k)),
                 pl.BlockSpec((None, d, win_sh), lambda j, i, k: (k, 0, 0)), pl.BlockSpec((tm2, d), lambda j, i, k: (i, 0)), NT,
                 after=put_g("in", [gw_in]))

    def b12(rows, fulls, outs, accs):
        dhv, xv, dx1v = [r[...] for r in rows]
        sc, g = fulls[0][...], fulls[1][...]
        r, xh = _rms(xv)
        accs[0][...] += _colsum(dhv)
        accs[1][...] += _colsum(dhv * (xh * g))
        dn = dhv * (1.0 + sc)
        accs[2][...] += _colsum(dn * xh)
        dxh = dn * g
        outs[0][...] = dx1v + r * (dxh - xh * jnp.mean(dxh * xh, axis=-1, keepdims=True))

    dx, dsh_m, dsc_m, dg_mix = _rowwise("d_prenorm_mix", b12, s, te, [(dh, d, 0), (x, d, 0), (dx1, d, 0)], [sc_m, g_mix],
                                        [(d, f32)], [(1, d)] * 3)
    dmod = [dsh_m, dsc_m, dgt_m, dsh_f, dsc_f, dgt_f]
    small = dict(loss=loss_acc, g_mix=dg_mix, g_ffn=dg_ffn, g_fin=dg_fin, lb=dlb, gnorm=dgn, conv_w=dconv_w)
    return dx, dmod, small


def _ada_fwd(c_all, w_sh, b_sh):
    def body(c_ref, w_ref, b_ref, o_ref):
        cv = c_ref[...]
        ca = (cv * _sigmoid(cv)).astype(bf16)
        o_ref[...] = jnp.dot(ca, w_ref[...].astype(bf16), preferred_element_type=f32) + b_ref[...]

    return pl.pallas_call(body, name="ada_fwd", out_shape=SDS((c_all.shape[0], w_sh.shape[1]), f32),
                          compiler_params=pltpu.CompilerParams(vmem_limit_bytes=V7X_VMEM_LIMIT))(c_all, w_sh, b_sh)


def _ada_wgrad(c_all, dmod_sh):
    def body(c_ref, d_ref, o_ref):
        cv = c_ref[...]
        ca = (cv * _sigmoid(cv)).astype(bf16)
        o_ref[...] = lax.dot_general(ca, d_ref[...].astype(bf16), (TN, ((), ())), preferred_element_type=f32)

    return pl.pallas_call(body, name="ada_wgrad", out_shape=SDS((c_all.shape[1], dmod_sh.shape[1]), f32),
                          compiler_params=pltpu.CompilerParams(vmem_limit_bytes=V7X_VMEM_LIMIT))(c_all, dmod_sh)


def _lb_grad(lb_param, dlb):
    def body(p_ref, d_ref, o_ref):
        p = p_ref[...]
        lb = _sigmoid(p[0:1, :] - p[1:2, :])
        gl = d_ref[...] * lb * (1.0 - lb)
        o_ref[0:1, :] = gl
        o_ref[1:2, :] = -gl

    return pl.pallas_call(body, name="lb_grad", out_shape=SDS(lb_param.shape, f32))(lb_param, dlb)


def _sum_small(gathered):
    def body(g_ref, o_ref):
        acc = g_ref[0]
        for dd in range(1, NDEV):
            acc = acc + g_ref[dd]
        o_ref[...] = acc

    return pl.pallas_call(body, name="sum_small", out_shape=SDS(gathered.shape[1:], f32))(gathered)


def kernel(x, c, w_ada, b_ada, norm_mix_g, w_in, conv_w, lb_param, gnorm_g, w_conv_out, w_hgrn_out, w_o, norm_ffn_g, w_ffn_gate, w_ffn_up, w_ffn_down, norm_final_g, loss_target, m_w_ada, m_b_ada, m_norm_mix_g, m_w_in, m_conv_w, m_lb_param, m_gnorm_g, m_w_conv_out, m_w_hgrn_out, m_w_o, m_norm_ffn_g, m_w_ffn_gate, m_w_ffn_up, m_w_ffn_down, m_norm_final_g, v_w_ada, v_b_ada, v_norm_mix_g, v_w_in, v_conv_w, v_lb_param, v_gnorm_g, v_w_conv_out, v_w_hgrn_out, v_w_o, v_norm_ffn_g, v_w_ffn_gate, v_w_ffn_up, v_w_ffn_down, v_norm_final_g):
    assert lb_param.shape[0] == 2 and w_ada.shape[0] == 1
    s, d = x.shape[1], x.shape[2]
    me = 4 * lax.axis_index("x") + 2 * lax.axis_index("y") + lax.axis_index("c")
    ada_cols = w_ada.shape[2]

    me1 = me.astype(jnp.int32).reshape(1)
    placed_in = [_place_shard("place_in0", me1, w_in[0])]
    c_all, cw_all = _all_gather("gather_cond", [c, conv_w[0]], after=placed_in)
    c_all = c_all.reshape(NDEV, d)
    conv_w_full = jnp.transpose(cw_all, (1, 0, 2)).reshape(conv_w.shape[1], -1)
    b_sh = lax.dynamic_slice_in_dim(b_ada, me * ada_cols, ada_cols, axis=1)
    mod_cols = _ada_fwd(c_all, w_ada[0], b_sh)
    mod_all, = _all_gather("gather_mod", [mod_cols])
    mod = lax.dynamic_index_in_dim(mod_all, me, axis=1, keepdims=False).reshape(N_MOD, 1, d)

    shard_groups = {"in": None, "mix": [w_conv_out[0], w_hgrn_out[0], w_o[0]],
                    "ffn": [w_ffn_gate[0].T, w_ffn_up[0].T, w_ffn_down[0]]}
    own_slot = lambda frm, to: _flat(frm)
    gather_plan = lambda n: [(a, j, own_slot, own_slot) for a in range(n) for j in (1,) + ICI_RELATIONS]
    flat = lambda a: a.reshape(a.shape[0] * a.shape[1], a.shape[2])
    to8 = lambda a: a.reshape(NDEV, a.shape[0] // NDEV, a.shape[1])
    gathering, tokens = {}, []
    for grp, sh in shard_groups.items():
        lands = placed_in if sh is None else [_place_shard(f"place_{grp}{i}", me1, a, after=tokens[-1]) for i, a in enumerate(sh)]
        ss, rs, _, lands, tok = _push_start("gather_start_" + grp, [], lands, gather_plan(len(lands)),
                                            after=tokens[-1] if tokens else mod_all)
        gathering[grp] = (ss, rs, lands)
        tokens.append(tok)

    def get_w(grp, after):
        ss, rs, lands = gathering[grp]
        _, lands = _push_wait("gather_wait_" + grp, ss, rs, [], lands, gather_plan(len(lands)), after)
        full = _sibling_forward("gather_fwd_" + grp, lands)
        return [f if grp == "in" or i < 2 and grp == "mix" else flat(f) for i, f in enumerate(full)]

    core = lax.axis_index("c").astype(jnp.int32).reshape(1)
    chip = (2 * lax.axis_index("x") + lax.axis_index("y")).astype(jnp.int32).reshape(1)
    scatter_plan = lambda n: [(a, j, lambda frm, to: _chip(to), lambda frm, to: _chip(frm)) for a in range(n) for j in ICI_RELATIONS]
    scattering = {}

    swap_plan = lambda n: [(a, 1, functools.partial(lambda frm, to, q: 2 * q + to[2], q=q), functools.partial(lambda frm, to, q: q, q=q))
                           for a in range(n) for q in range(NDEV // 2)]
    swapping = {}

    def start_ici(grp, g8, recv):
        pairs = [_pair_sum(f"pair_sum_{grp}{i}", core, g, r, _row_tile(g.shape[1], 1024)) for i, (g, r) in enumerate(zip(g8, recv))]
        lands = [lax.empty(p.shape, p.dtype) for p in pairs]
        ss, rs, srcs, lands, tok = _push_start("scatter_start_" + grp, pairs, lands, scatter_plan(len(pairs)))
        scattering[grp] = (ss, rs, srcs, lands)
        return tok

    def put_g(grp, grads):
        g8 = [g if g.ndim == 3 else to8(g) for g in grads]
        if grp == "in":
            return start_ici(grp, g8, _sibling_swap("scatter_pair_" + grp, g8))
        lands = [lax.empty((NDEV // 2,) + g.shape[1:], g.dtype) for g in g8]
        ss, rs, srcs, lands, tok = _push_start("scatter_swap_" + grp, g8, lands, swap_plan(len(g8)))
        swapping[grp] = (ss, rs, srcs, lands)
        return tok

    def sent(grp, after):
        ss, rs, srcs, lands = swapping[grp]
        g8, recv = _push_wait("scatter_swapped_" + grp, ss, rs, srcs, lands, swap_plan(len(srcs)), after)
        return start_ici(grp, g8, recv)

    def reduced(grp, after, names):
        ss, rs, srcs, lands = scattering[grp]
        srcs, lands = _push_wait("scatter_wait_" + grp, ss, rs, srcs, lands, scatter_plan(len(srcs)), after)
        for i, (p, r, nm) in enumerate(zip(srcs, lands, names)):
            tr = nm in ("w_ffn_gate", "w_ffn_up")
            wmv = tuple(a[0].T if tr else a[0] for a in weights[nm])
            out = _chip_sum(f"chip_sum_{grp}{i}", chip, p, r, _row_tile(p.shape[1]), wmv)
            res[nm] = [(a.T if tr else a).reshape(weights[nm][0].shape) for a in out]

    dx, dmod, small = _local_step(x[0], loss_target[0], mod, _after_tokens(norm_mix_g, tokens), norm_ffn_g,
                                  norm_final_g.reshape(1, d), lb_param, gnorm_g, conv_w_full, get_w, put_g, sent)

    pieces = [*dmod, small["g_mix"], small["g_ffn"], small["g_fin"], small["lb"], small["gnorm"], small["loss"],
              small["conv_w"].reshape(1, -1)]
    widths = [p.shape[1] for p in pieces]
    offs = np.concatenate([[0], np.cumsum(widths)])
    packed = jnp.concatenate(pieces, axis=1)
    gathered, = _all_gather("gather_small", [packed])
    summed = _sum_small(gathered)
    part = lambda i: summed[:, offs[i]:offs[i + 1]]
    g_b_ada = summed[:, :N_MOD * d]
    g_norm_mix, g_norm_ffn, g_norm_fin, g_lb_row, g_gnorm, loss_vec, g_convw_flat = [part(i) for i in range(N_MOD, N_MOD + 7)]
    loss = loss_vec[0, 0]
    dmod_all = gathered[:, 0, :N_MOD * d]
    g_w_ada = _ada_wgrad(c_all, lax.dynamic_slice_in_dim(dmod_all, me * ada_cols, ada_cols, axis=1))
    g_lb = _lb_grad(lb_param, g_lb_row)
    cw_cols = conv_w.shape[2]
    g_conv_w = lax.dynamic_slice_in_dim(g_convw_flat.reshape(conv_w.shape[1], -1), me * cw_cols, cw_cols, axis=1)

    grads = dict(w_ada=g_w_ada, b_ada=g_b_ada, norm_mix_g=g_norm_mix, conv_w=g_conv_w, lb_param=g_lb, gnorm_g=g_gnorm,
                 norm_ffn_g=g_norm_ffn, norm_final_g=g_norm_fin)
    weights = dict(w_ada=(w_ada, m_w_ada, v_w_ada), b_ada=(b_ada, m_b_ada, v_b_ada), norm_mix_g=(norm_mix_g, m_norm_mix_g, v_norm_mix_g),
                   w_in=(w_in, m_w_in, v_w_in), conv_w=(conv_w, m_conv_w, v_conv_w), lb_param=(lb_param, m_lb_param, v_lb_param),
                   gnorm_g=(gnorm_g, m_gnorm_g, v_gnorm_g), w_conv_out=(w_conv_out, m_w_conv_out, v_w_conv_out),
                   w_hgrn_out=(w_hgrn_out, m_w_hgrn_out, v_w_hgrn_out), w_o=(w_o, m_w_o, v_w_o),
                   norm_ffn_g=(norm_ffn_g, m_norm_ffn_g, v_norm_ffn_g), w_ffn_gate=(w_ffn_gate, m_w_ffn_gate, v_w_ffn_gate),
                   w_ffn_up=(w_ffn_up, m_w_ffn_up, v_w_ffn_up), w_ffn_down=(w_ffn_down, m_w_ffn_down, v_w_ffn_down),
                   norm_final_g=(norm_final_g, m_norm_final_g, v_norm_final_g))
    res = {}

    def update(nm):
        w, m, v = weights[nm]
        shape2 = (w.shape[-2], w.shape[-1]) if w.ndim >= 2 else (1, w.shape[0])
        g2 = grads[nm].reshape(shape2)
        dl, mn, vn = _adamw("adamw_" + nm, g2, w.reshape(shape2), m.reshape(shape2), v.reshape(shape2))
        res[nm] = [a.reshape(w.shape) for a in (g2, dl, mn, vn)]

    for nm in list(grads):
        update(nm)
    reduced("ffn", res["w_ada"][1], ("w_ffn_gate", "w_ffn_up", "w_ffn_down"))
    reduced("mix", res["w_ffn_down"][1], ("w_conv_out", "w_hgrn_out", "w_o"))
    reduced("in", res["w_o"][1], ("w_in",))
    outs = [[res[nm][i] for nm in weights] for i in range(4)]
    return (loss, dx.reshape(x.shape), *outs[0], *outs[1], *outs[2], *outs[3])
```

```python
import functools

import numpy as np
import jax
import jax.numpy as jnp
from jax import lax
from jax.experimental import pallas as pl
from jax.experimental.pallas import tpu as pltpu

f32, bf16 = jnp.float32, jnp.bfloat16
SDS = jax.ShapeDtypeStruct

EPS = 1e-6
HEADS, DK, CHUNK = 8, 128, 64
HEADS_PER_STEP = 4
N_MOD = 6
NDEV = 8
ADAM_LR, ADAM_B1, ADAM_B2, ADAM_EPS, ADAM_WD, ADAM_STEP = 0.001, 0.9, 0.999, 1e-08, 0.01, 10
LEVELS = (32, 16, 8, 4, 2, 1)
V7X_VMEM_LIMIT = 56 * 1024 * 1024
HBM = pl.BlockSpec(memory_space=pltpu.HBM)
MESH = pl.DeviceIdType.MESH


def _params(sem):
    return pltpu.CompilerParams(dimension_semantics=sem, vmem_limit_bytes=V7X_VMEM_LIMIT)


def _sigmoid(x):
    return jax.nn.sigmoid(x)


def _dsilu(x, s):
    return s * (1.0 + x * (1.0 - s))


def _mesh_pos():
    x, y, c = lax.axis_index("x"), lax.axis_index("y"), lax.axis_index("c")
    return x, y, c


def _peer(pos, j):
    x, y, c = pos
    return (1 - x if j & 4 else x, 1 - y if j & 2 else y, 1 - c if j & 1 else c)


def _flat(pos):
    return 4 * pos[0] + 2 * pos[1] + pos[2]


def _all_gather(name, arrs, after=()):
    n, ne = len(arrs), len(after)
    out_shapes = [SDS((NDEV,) + a.shape, a.dtype) for a in arrs]

    def body(*refs):
        ins, outs = refs[:n], refs[n + ne:2 * n + ne]
        send_sems, recv_sems, local_sems = refs[2 * n + ne:]
        pos = _mesh_pos()
        me = _flat(pos)

        def copy(a, j, frm, to_pos):
            k = a * (NDEV - 1) + j - 1
            return pltpu.make_async_remote_copy(src_ref=ins[a], dst_ref=outs[a].at[frm], send_sem=send_sems.at[k],
                                                recv_sem=recv_sems.at[k], device_id=to_pos, device_id_type=MESH)

        local = [pltpu.make_async_copy(ins[a], outs[a].at[me], local_sems.at[a]) for a in range(n)]
        for cp in local:
            cp.start()
        sends = [copy(a, j, me, _peer(pos, j)) for j in range(1, NDEV) for a in range(n)]
        for cp in sends:
            cp.start()
        for j in range(1, NDEV):
            for a in range(n):
                copy(a, j, _flat(_peer(pos, j)), pos).wait_recv()
        for cp in sends:
            cp.wait_send()
        for cp in local:
            cp.wait()

    return pl.pallas_call(
        body, name=name, out_shape=out_shapes, in_specs=[HBM] * n + [ANY] * ne, out_specs=[HBM] * n,
        scratch_shapes=[pltpu.SemaphoreType.DMA((n * (NDEV - 1),)), pltpu.SemaphoreType.DMA((n * (NDEV - 1),)),
                        pltpu.SemaphoreType.DMA((n,))],
    )(*arrs, *after)


SEM = pl.BlockSpec(memory_space=pltpu.SEMAPHORE)
ANY = pl.BlockSpec(memory_space=pl.ANY)
EFFECT = pltpu.SideEffectType.DATAFLOW_SIDE_EFFECTING
ICI_RELATIONS = (2, 4, 6)


def _chip(pos):
    return 2 * pos[0] + pos[1]


def _hbm(a):
    return pltpu.with_memory_space_constraint(a, pltpu.HBM)


def _plan_copy(plan_entry, k, pos, frm, to, src_refs, land_refs, send_sems, recv_sems):
    a, _, src_slot, dst_slot = plan_entry
    s = src_refs[a] if src_slot is None else src_refs[a].at[src_slot(frm, to)]
    return pltpu.make_async_remote_copy(src_ref=s, dst_ref=land_refs[a].at[dst_slot(frm, to)], send_sem=send_sems.at[k],
                                        recv_sem=recv_sems.at[k], device_id=to, device_id_type=MESH)


def _push_start(name, srcs, lands, plan, after=None):
    ns, nb, nk = len(srcs), len(srcs) + len(lands), len(plan)
    extra = [] if after is None else [after]

    def body(*refs):
        land_refs = refs[ns:nb]
        src_refs = refs[:ns] if ns else land_refs
        send_sems, recv_sems = refs[nb + len(extra)], refs[nb + len(extra) + 1]
        pos = _mesh_pos()
        for k, e in enumerate(plan):
            _plan_copy(e, k, pos, pos, _peer(pos, e[1]), src_refs, land_refs, send_sems, recv_sems).start()
        refs[-1][...] = jnp.zeros_like(refs[-1])

    outs = pl.pallas_call(
        body, name=name,
        out_shape=[pltpu.SemaphoreType.DMA((nk,)), pltpu.SemaphoreType.DMA((nk,))] + [pltpu.HBM(a.shape, a.dtype) for a in srcs + lands]
        + [SDS((8, 128), f32)],
        in_specs=[HBM] * nb + [ANY] * len(extra), out_specs=[SEM, SEM] + [HBM] * nb + [pl.BlockSpec(memory_space=pltpu.VMEM)],
        input_output_aliases={i: 2 + i for i in range(nb)},
        compiler_params=pltpu.CompilerParams(has_side_effects=EFFECT),
    )(*[_hbm(a) for a in srcs + lands], *extra)
    return outs[0], outs[1], list(outs[2:2 + ns]), list(outs[2 + ns:2 + nb]), outs[-1]


def _push_wait(name, send_sems, recv_sems, srcs, lands, plan, after):
    ns, nb = len(srcs), len(srcs) + len(lands)

    def body(*refs):
        land_refs = refs[ns:nb]
        src_refs = refs[:ns] if ns else land_refs
        ssem, rsem = refs[nb], refs[nb + 1]
        pos = _mesh_pos()
        for k, e in enumerate(plan):
            peer = _peer(pos, e[1])
            _plan_copy(e, k, pos, pos, peer, src_refs, land_refs, ssem, rsem).wait_send()
            _plan_copy(e, k, pos, peer, pos, src_refs, land_refs, ssem, rsem).wait_recv()

    outs = pl.pallas_call(
        body, name=name, out_shape=[pltpu.HBM(a.shape, a.dtype) for a in srcs + lands],
        in_specs=[HBM] * nb + [SEM, SEM, ANY], out_specs=[HBM] * nb,
        input_output_aliases={i: i for i in range(nb)},
        compiler_params=pltpu.CompilerParams(has_side_effects=EFFECT),
    )(*srcs, *lands, send_sems, recv_sems, after)
    return list(outs[:ns]), list(outs[ns:])


def _after_tokens(small, tokens):
    for t in tokens:
        if t is not None:
            small = small + t[0:1, 0:1].reshape((1,) * small.ndim)
    return small


def _place_shard(name, me, shard, after=None):
    r, c = shard.shape
    tr = _row_tile(r)
    extra = [] if after is None else [after]

    def body(me_ref, s_ref, *rest):
        rest[-1][...] = s_ref[...].astype(bf16)

    return pl.pallas_call(
        body, name=name, out_shape=SDS((NDEV, r, c), bf16),
        grid_spec=pltpu.PrefetchScalarGridSpec(
            num_scalar_prefetch=1, grid=(r // tr,), in_specs=[pl.BlockSpec((tr, c), lambda i, me_ref: (i, 0))] + [ANY] * len(extra),
            out_specs=pl.BlockSpec((None, tr, c), lambda i, me_ref: (me_ref[0], i, 0))),
        compiler_params=_params(("parallel",)),
    )(me, shard, *extra)


def _row_tile(r, most=256):
    return max(t for t in range(16, most + 1, 16) if r % t == 0)


def _sibling_forward(name, lands):
    n = len(lands)
    nk = n * len(ICI_RELATIONS)

    def body(*refs):
        land_refs = refs[:n]
        send_sems, recv_sems = refs[2 * n:]
        pos = _mesh_pos()
        sib = _peer(pos, 1)

        def copy(a, i, frm, to):
            k = a * len(ICI_RELATIONS) + i
            slot = _flat(_peer(frm, ICI_RELATIONS[i]))
            return pltpu.make_async_remote_copy(src_ref=land_refs[a].at[slot], dst_ref=land_refs[a].at[slot], send_sem=send_sems.at[k],
                                                recv_sem=recv_sems.at[k], device_id=to, device_id_type=MESH)

        sends = [copy(a, i, pos, sib) for a in range(n) for i in range(len(ICI_RELATIONS))]
        for cp in sends:
            cp.start()
        for a in range(n):
            for i in range(len(ICI_RELATIONS)):
                copy(a, i, sib, pos).wait_recv()
        for cp in sends:
            cp.wait_send()

    outs = pl.pallas_call(
        body, name=name, out_shape=[SDS(a.shape, a.dtype) for a in lands], in_specs=[HBM] * n, out_specs=[HBM] * n,
        input_output_aliases={i: i for i in range(n)},
        scratch_shapes=[pltpu.SemaphoreType.DMA((nk,)), pltpu.SemaphoreType.DMA((nk,))],
    )(*lands)
    return list(outs)


def _sibling_swap(name, grads):
    n = len(grads)
    nchip = NDEV // 2

    def body(*refs):
        g_refs, out_refs = refs[:n], refs[n:2 * n]
        send_sems, recv_sems = refs[2 * n:]
        pos = _mesh_pos()
        sib = _peer(pos, 1)
        sends = []
        for a in range(n):
            for q in range(nchip):
                k = a * nchip + q
                sends.append(pltpu.make_async_remote_copy(src_ref=g_refs[a].at[2 * q + sib[2]], dst_ref=out_refs[a].at[q],
                                                          send_sem=send_sems.at[k], recv_sem=recv_sems.at[k], device_id=sib,
                                                          device_id_type=MESH))
        for cp in sends:
            cp.start()
        for cp in sends:
            cp.wait()

    return pl.pallas_call(
        body, name=name, out_shape=[SDS((nchip,) + a.shape[1:], a.dtype) for a in grads], in_specs=[HBM] * n, out_specs=[HBM] * n,
        scratch_shapes=[pltpu.SemaphoreType.DMA((n * nchip,)), pltpu.SemaphoreType.DMA((n * nchip,))],
    )(*grads)


def _pair_sum(name, core, grad, recv, tr):
    _, r, c = grad.shape
    nchip = NDEV // 2

    def body(core_ref, g_ref, r_ref, o_ref):
        o_ref[...] = (g_ref[...].astype(f32) + r_ref[...].astype(f32)).astype(o_ref.dtype)

    return pl.pallas_call(
        body, name=name, out_shape=SDS((nchip, r, c), grad.dtype),
        grid_spec=pltpu.PrefetchScalarGridSpec(
            num_scalar_prefetch=1, grid=(nchip, r // tr),
            in_specs=[pl.BlockSpec((None, tr, c), lambda q, i, core_ref: (2 * q + core_ref[0], i, 0)),
                      pl.BlockSpec((None, tr, c), lambda q, i, core_ref: (q, i, 0))],
            out_specs=pl.BlockSpec((None, tr, c), lambda q, i, core_ref: (q, i, 0))),
        compiler_params=_params(("parallel", "parallel")),
    )(core, grad, recv)


def _adamw_math(g, w, m, v):
    mn = ADAM_B1 * m + (1.0 - ADAM_B1) * g
    vn = ADAM_B2 * v + (1.0 - ADAM_B2) * jnp.square(g)
    m_hat = mn / (1.0 - ADAM_B1 ** ADAM_STEP)
    v_hat = vn / (1.0 - ADAM_B2 ** ADAM_STEP)
    return -ADAM_LR * (m_hat / (jnp.sqrt(v_hat) + ADAM_EPS) + ADAM_WD * w), mn, vn


def _chip_sum(name, chip, pairs, recv, tr, wmv=None):
    nchip, r, c = pairs.shape
    n_out = 1 if wmv is None else 4

    def body(chip_ref, p_ref, r_ref, *rest):
        mine = chip_ref[0]
        acc = jnp.zeros((tr, c), f32)
        for q in range(nchip):
            acc = acc + jnp.where(mine == q, p_ref[q].astype(f32), r_ref[q].astype(f32))
        outs = rest[-n_out:]
        outs[0][...] = acc
        if wmv is not None:
            w_ref, m_ref, v_ref = rest[:3]
            outs[1][...], outs[2][...], outs[3][...] = _adamw_math(acc, w_ref[...], m_ref[...], v_ref[...])

    blk = pl.BlockSpec((tr, c), lambda i, chip_ref: (i, 0))
    outs = pl.pallas_call(
        body, name=name, out_shape=[SDS((r, c), f32)] * n_out,
        grid_spec=pltpu.PrefetchScalarGridSpec(
            num_scalar_prefetch=1, grid=(r // tr,),
            in_specs=[pl.BlockSpec((nchip, tr, c), lambda i, chip_ref: (0, i, 0))] * 2 + [blk] * (n_out - 1),
            out_specs=[blk] * n_out),
        compiler_params=_params(("parallel",)),
    )(chip, pairs, recv, *(wmv or ()))
    return outs[0] if wmv is None else outs


def _matmul(name, a, b, out_shape, out_dtype, grid, a_spec, b_spec, o_spec, dims, after=None):
    ksteps = grid[2]
    acc_shape = tuple(d for d in o_spec.block_shape if d is not None)
    extra = [] if after is None else [after]

    def body(a_ref, b_ref, *rest):
        o_ref, acc = rest[len(extra)], rest[len(extra) + 1:]
        prod = lax.dot_general(a_ref[...], b_ref[...], (dims, ((), ())), preferred_element_type=f32)
        if ksteps == 1:
            o_ref[...] = prod.astype(o_ref.dtype)
        else:
            k = pl.program_id(2)

            @pl.when(k == 0)
            def _():
                acc[0][...] = prod

            @pl.when(k > 0)
            def _():
                acc[0][...] += prod

            @pl.when(k == ksteps - 1)
            def _():
                o_ref[...] = acc[0][...].astype(o_ref.dtype)

    return pl.pallas_call(
        body, name=name, grid=grid, in_specs=[a_spec, b_spec] + [ANY] * len(extra), out_specs=o_spec,
        out_shape=SDS(out_shape, out_dtype), scratch_shapes=[] if ksteps == 1 else [pltpu.VMEM(acc_shape, f32)],
        compiler_params=_params(("parallel", "parallel", "arbitrary")),
    )(a, b, *extra)


NN, NT, TN = ((1,), (0,)), ((1,), (1,)), ((0,), (0,))


def _rowwise(name, fn, n_rows, tm, rows, fulls, row_outs, acc_outs=(), ncol=1):
    assert ncol == 1 or not acc_outs
    nr, nf, no, na = len(rows), len(fulls), len(row_outs), len(acc_outs)
    in_specs = [pl.BlockSpec((tm, w), functools.partial(lambda i, j, cb: (i, cb + j), cb=cb)) for (_, w, cb) in rows]
    in_specs += [pl.BlockSpec(a.shape, functools.partial(lambda i, j, nd: (0,) * nd, nd=a.ndim)) for a in fulls]
    out_shape = [SDS((n_rows, w), dt) for (w, dt) in row_outs] + [SDS(s, f32) for s in acc_outs]
    out_specs = [pl.BlockSpec((tm, w // ncol), lambda i, j: (i, j)) for (w, _) in row_outs]
    out_specs += [pl.BlockSpec(s, functools.partial(lambda i, j, nd: (0,) * nd, nd=len(s))) for s in acc_outs]

    def body(*refs):
        if na:
            @pl.when(pl.program_id(0) == 0)
            def _():
                for r in refs[nr + nf + no:]:
                    r[...] = jnp.zeros(r.shape, r.dtype)
        fn(refs[:nr], refs[nr:nr + nf], refs[nr + nf:nr + nf + no], refs[nr + nf + no:])

    return pl.pallas_call(
        body, name=name, grid=(n_rows // tm, ncol), in_specs=in_specs, out_specs=out_specs, out_shape=out_shape,
        compiler_params=_params(("arbitrary" if na else "parallel", "arbitrary" if na else "parallel")),
    )(*[r[0] for r in rows], *fulls)


def _rms(x):
    r = lax.rsqrt(jnp.mean(x * x, axis=-1, keepdims=True) + EPS)
    return r, x * r


def _colsum(v):
    return jnp.sum(v, axis=0, keepdims=True)


def _shift_down(u, row, k):
    return jnp.where(row >= k, pltpu.roll(u, k, 0), 0.0)


def _shift_up(u, row, k):
    n = u.shape[0]
    return jnp.where(row < n - k, pltpu.roll(u, n - k, 0), 0.0)


def _conv_fwd(proj, conv_w, s, dc):
    nb = dc // 128

    def body(ab_ref, ac_ref, ax_ref, w_ref, z_ref):
        u = ac_ref[...] * ax_ref[...]
        row = lax.broadcasted_iota(jnp.int32, u.shape, 0)
        w = w_ref[...]
        cv = w[0:1] * _shift_down(u, row, 2) + w[1:2] * _shift_down(u, row, 1) + w[2:3] * u
        z_ref[...] = (ab_ref[...] * cv).astype(z_ref.dtype)

    col = lambda off: pl.BlockSpec((s, 128), functools.partial(lambda j, off: (0, off + j), off=off))
    return pl.pallas_call(
        body, name="conv_fwd", grid=(nb,), in_specs=[col(0), col(nb), col(2 * nb), pl.BlockSpec((3, 128), lambda j: (0, j))],
        out_specs=pl.BlockSpec((s, 128), lambda j: (0, j)), out_shape=SDS((s, dc), bf16), compiler_params=_params(("parallel",)),
    )(proj, proj, proj, conv_w)


def _conv_bwd(proj, conv_w, dz, s, dc):
    nb = dc // 128

    def body(ab_ref, ac_ref, ax_ref, w_ref, dz_ref, dab_ref, dac_ref, dax_ref, dw_ref):
        ab, ac, ax, dzv = ab_ref[...], ac_ref[...], ax_ref[...], dz_ref[...]
        u = ac * ax
        row = lax.broadcasted_iota(jnp.int32, u.shape, 0)
        w = w_ref[...]
        u1, u2 = _shift_down(u, row, 1), _shift_down(u, row, 2)
        cv = w[0:1] * u2 + w[1:2] * u1 + w[2:3] * u
        dcv = dzv * ab
        dab_ref[...] = (dzv * cv).astype(dab_ref.dtype)
        du = w[2:3] * dcv + w[1:2] * _shift_up(dcv, row, 1) + w[0:1] * _shift_up(dcv, row, 2)
        dac_ref[...] = (du * ax).astype(dac_ref.dtype)
        dax_ref[...] = (du * ac).astype(dax_ref.dtype)
        dw_ref[0:1, :] = _colsum(dcv * u2)
        dw_ref[1:2, :] = _colsum(dcv * u1)
        dw_ref[2:3, :] = _colsum(dcv * u)

    col = lambda off: pl.BlockSpec((s, 128), functools.partial(lambda j, off: (0, off + j), off=off))
    blk = pl.BlockSpec((s, 128), lambda j: (0, j))
    return pl.pallas_call(
        body, name="conv_bwd", grid=(nb,),
        in_specs=[col(0), col(nb), col(2 * nb), pl.BlockSpec((3, 128), lambda j: (0, j)), blk],
        out_specs=[blk, blk, blk, pl.BlockSpec((3, 128), lambda j: (0, j))],
        out_shape=[SDS((s, dc), bf16)] * 3 + [SDS((3, dc), f32)], compiler_params=_params(("parallel",)),
    )(proj, proj, proj, conv_w, dz)


def _level_masks():
    t = np.arange(CHUNK)[:, None]
    s = np.arange(CHUNK)[None, :]
    m = np.stack([((t & h) != 0) & ((s & h) == 0) & (t // (2 * h) == s // (2 * h)) for h in LEVELS]).astype(np.float32)
    return jnp.asarray(m), jnp.asarray(m.transpose(0, 2, 1))


def _cumsum_rows(x, row):
    for sh in (1, 2, 4, 8, 16, 32):
        x = x + jnp.where(row >= sh, pltpu.roll(x, sh, 0), 0.0)
    return x


def _rev_cumsum_rows(x, row):
    n = x.shape[0]
    for sh in (1, 2, 4, 8, 16, 32):
        x = x + jnp.where(row < n - sh, pltpu.roll(x, n - sh, 0), 0.0)
    return x


def _chunk_terms(qp, fl, lb):
    row = lax.broadcasted_iota(jnp.int32, qp.shape, 0)
    sig = _sigmoid(fl)
    f = lb + (1.0 - lb) * sig
    k = 1.0 - f
    sq = _sigmoid(qp)
    qh = qp * sq
    b = _cumsum_rows(jnp.log(f), row)
    sub = lax.broadcasted_iota(jnp.int32, (CHUNK // 8, 8, DK), 1)
    b8 = b.reshape(CHUNK // 8, 8, DK)
    us, exs, ups = [], [], []
    for m in LEVELS:
        sb = 2 * m
        if sb >= 8:
            b3 = b.reshape(CHUNK // sb, sb, DK)
            bref = jnp.broadcast_to(b3[:, m - 1:m, :], b3.shape).reshape(CHUNK, DK)
        else:
            bref8 = None
            for j in range(8 // sb):
                cand = jnp.broadcast_to(b8[:, j * sb + m - 1:j * sb + m, :], b8.shape)
                bref8 = cand if bref8 is None else jnp.where(sub >= j * sb, cand, bref8)
            bref = bref8.reshape(CHUNK, DK)
        up = (row & m) != 0
        ex = jnp.exp(jnp.where(up, b - bref, bref - b))
        us.append((jnp.where(up, qh, k) * ex).astype(bf16))
        exs.append(ex)
        ups.append(up)
    blast = b[CHUNK - 1:CHUNK, :]
    eb, ebl = jnp.exp(b), jnp.exp(blast - b)
    return dict(sig=sig, f=f, k=k, sq=sq, qh=qh, u=jnp.stack(us), ex=exs, up=ups, eb=eb, ebl=ebl, qt=qh * eb, kt=k * ebl,
                el=jnp.exp(blast), row=row)


def _scores(t, mask):
    pl_ = jnp.einsum("ltk,lsk->lts", t["u"], t["u"], preferred_element_type=f32)
    p = jnp.sum(pl_ * mask, axis=0)
    r = lax.broadcasted_iota(jnp.int32, (CHUNK, CHUNK), 0)
    c = lax.broadcasted_iota(jnp.int32, (CHUNK, CHUNK), 1)
    diag = jnp.sum(t["qh"] * t["k"], axis=-1, keepdims=True)
    return p + jnp.where(r == c, diag, 0.0)


def _hgrn_fwd(proj, lb_param, gnorm, s, dv_total, tb):
    nchunk = tb // CHUNK
    masks, _ = _level_masks()
    q0, f0, v0, g0 = 3 * HEADS, 4 * HEADS, 5 * HEADS, 6 * HEADS

    def body(q_ref, f_ref, v_ref, g_ref, lb_ref, gn_ref, mask_ref, og_ref, o_ref, st_ref, state):
        @pl.when(pl.program_id(1) == 0)
        def _():
            state[...] = jnp.zeros_like(state)

        lbp = lb_ref[...]
        lb_all = _sigmoid(lbp[0:1, :] - lbp[1:2, :])
        mask = mask_ref[...]
        for i, hh in [(i, hh) for i in range(nchunk) for hh in range(HEADS_PER_STEP)]:
            rs, cs = pl.ds(i * CHUNK, CHUNK), pl.ds(hh * DK, DK)
            t = _chunk_terms(q_ref[rs, cs], f_ref[rs, cs], lb_all[:, hh * DK:(hh + 1) * DK])
            v = v_ref[rs, cs]
            vb = v.astype(bf16)
            st = state[hh]
            st_ref[i, hh] = st
            p = _scores(t, mask)
            o = jnp.dot(p.astype(bf16), vb, preferred_element_type=f32)
            o += lax.dot_general(t["qt"].astype(bf16), st.astype(bf16), (NT, ((), ())), preferred_element_type=f32)
            state[hh] = st * t["el"] + lax.dot_general(vb, t["kt"].astype(bf16), (TN, ((), ())), preferred_element_type=f32)
            o_ref[rs, cs] = o
            r, oh = _rms(o)
            g = g_ref[rs, cs]
            og_ref[rs, cs] = (oh * gn_ref[...] * (g * _sigmoid(g))).astype(og_ref.dtype)

    hp, wide = HEADS_PER_STEP, HEADS_PER_STEP * DK
    col = lambda off: pl.BlockSpec((tb, wide), functools.partial(lambda h, t, off: (t, off + h), off=off // hp))
    blk = pl.BlockSpec((tb, wide), lambda h, t: (t, h))
    return pl.pallas_call(
        body, name="hgrn_fwd", grid=(HEADS // hp, s // tb),
        in_specs=[col(q0), col(f0), col(v0), col(g0), pl.BlockSpec((2, wide), lambda h, t: (0, h)),
                  pl.BlockSpec((1, DK), lambda h, t: (0, 0)), pl.BlockSpec(masks.shape, lambda h, t: (0, 0, 0))],
        out_specs=[blk, blk, pl.BlockSpec((nchunk, hp, DK, DK), lambda h, t: (t, h, 0, 0))],
        out_shape=[SDS((s, dv_total), bf16), SDS((s, dv_total), f32), SDS((s // CHUNK, HEADS, DK, DK), f32)],
        scratch_shapes=[pltpu.VMEM((hp, DK, DK), f32)], compiler_params=_params(("parallel", "arbitrary")),
    )(proj, proj, proj, proj, lb_param, gnorm, masks)


def _hgrn_bwd(proj, lb_param, gnorm, o_saved, states, dog, s, dv_total, tb):
    nchunk = tb // CHUNK
    nt = s // tb
    nc_total = s // CHUNK
    masks, masks_t = _level_masks()
    q0, f0, v0, g0 = 3 * HEADS, 4 * HEADS, 5 * HEADS, 6 * HEADS

    def body(q_ref, f_ref, v_ref, g_ref, lb_ref, gn_ref, mask_ref, maskt_ref, o_ref, dog_ref, st_ref, stn_ref,
             dq_ref, df_ref, dv_ref, dg_ref, dlb_ref, dgn_ref, gstate):
        h_id, t_id = pl.program_id(0), pl.program_id(1)

        @pl.when(t_id == 0)
        def _():
            gstate[...] = jnp.zeros_like(gstate)
            dlb_ref[...] = jnp.zeros_like(dlb_ref)

        @pl.when((t_id == 0) & (h_id == 0))
        def _():
            dgn_ref[...] = jnp.zeros_like(dgn_ref)

        lbp = lb_ref[...]
        lb_all = _sigmoid(lbp[0:1, :] - lbp[1:2, :])
        mask, maskt = mask_ref[...], maskt_ref[...]
        gn = gn_ref[...]
        for i, hh in [(i, hh) for i in reversed(range(nchunk)) for hh in range(HEADS_PER_STEP)]:
            rs, cs = pl.ds(i * CHUNK, CHUNK), pl.ds(hh * DK, DK)
            lb = lb_all[:, hh * DK:(hh + 1) * DK]
            qp, fl, v, g = q_ref[rs, cs], f_ref[rs, cs], v_ref[rs, cs], g_ref[rs, cs]
            t = _chunk_terms(qp, fl, lb)
            vb = v.astype(bf16)
            st0 = st_ref[i, hh]
            st1 = st_ref[i + 1, hh] if i + 1 < nchunk else stn_ref[0, hh]
            gt = gstate[hh]
            o = o_ref[rs, cs]
            r, oh = _rms(o)
            sg = _sigmoid(g)
            dog_v = dog_ref[rs, cs]
            dg_ref[rs, cs] = (dog_v * (oh * gn) * _dsilu(g, sg)).astype(dg_ref.dtype)
            don = dog_v * (g * sg)
            dgn_ref[...] += _colsum(don * oh)
            doh = don * gn
            do = r * (doh - oh * jnp.mean(doh * oh, axis=-1, keepdims=True))
            dob = do.astype(bf16)
            d = lax.dot_general(dob, vb, (NT, ((), ())), preferred_element_type=f32)
            dt = lax.dot_general(vb, dob, (NT, ((), ())), preferred_element_type=f32)
            z = (mask * d[None] + maskt * dt[None]).astype(bf16)
            rr = jnp.einsum("lts,lsk->ltk", z, t["u"], preferred_element_type=f32)
            dq = jnp.zeros((CHUNK, DK), f32)
            dk = jnp.zeros((CHUNK, DK), f32)
            qdk = jnp.zeros((CHUNK, DK), f32)
            for li in range(len(LEVELS)):
                du = t["ex"][li] * rr[li]
                dq += jnp.where(t["up"][li], du, 0.0)
                dk += jnp.where(t["up"][li], 0.0, du)
                e = t["u"][li].astype(f32) * rr[li]
                qdk += jnp.where(t["up"][li], e, -e)
            dd = jnp.sum(do * v, axis=-1, keepdims=True)
            dq += dd * t["k"]
            dk += dd * t["qh"]
            gtb = gt.astype(bf16)
            ktb, qtb = t["kt"].astype(bf16), t["qt"].astype(bf16)
            dq_in = jnp.dot(dob, st0.astype(bf16), preferred_element_type=f32)
            dk_in = jnp.dot(vb, gtb, preferred_element_type=f32)
            dq += t["eb"] * dq_in
            dk += t["ebl"] * dk_in
            qdk += qtb.astype(f32) * dq_in - ktb.astype(f32) * dk_in
            p = _scores(t, mask)
            dvv = lax.dot_general(p.astype(bf16), dob, (TN, ((), ())), preferred_element_type=f32)
            dvv += lax.dot_general(ktb, gtb, (NT, ((), ())), preferred_element_type=f32)
            dv_ref[rs, cs] = dvv.astype(dv_ref.dtype)
            a_end = _colsum(gtb.astype(f32) * st1)
            dlf = _rev_cumsum_rows(qdk, t["row"]) + a_end
            dfv = dlf / t["f"] - dk
            df_ref[rs, cs] = (dfv * (1.0 - lb) * t["sig"] * (1.0 - t["sig"])).astype(df_ref.dtype)
            dlb_ref[:, cs] += _colsum(dfv * (1.0 - t["sig"]))
            dq_ref[rs, cs] = (dq * _dsilu(qp, t["sq"])).astype(dq_ref.dtype)
            gstate[hh] = gt * t["el"] + lax.dot_general(dob, qtb, (TN, ((), ())), preferred_element_type=f32)

    hp, wide = HEADS_PER_STEP, HEADS_PER_STEP * DK
    rev = lambda t: nt - 1 - t
    col = lambda off: pl.BlockSpec((tb, wide), functools.partial(lambda h, t, off: (rev(t), off + h), off=off // hp))
    blk = pl.BlockSpec((tb, wide), lambda h, t: (rev(t), h))
    nxt = lambda h, t: (jnp.minimum((rev(t) + 1) * nchunk, nc_total - 1), h, 0, 0)
    return pl.pallas_call(
        body, name="hgrn_bwd", grid=(HEADS // hp, nt),
        in_specs=[col(q0), col(f0), col(v0), col(g0), pl.BlockSpec((2, wide), lambda h, t: (0, h)),
                  pl.BlockSpec((1, DK), lambda h, t: (0, 0)), pl.BlockSpec(masks.shape, lambda h, t: (0, 0, 0)),
                  pl.BlockSpec(masks.shape, lambda h, t: (0, 0, 0)), blk, blk,
                  pl.BlockSpec((nchunk, hp, DK, DK), lambda h, t: (rev(t), h, 0, 0)),
                  pl.BlockSpec((1, hp, DK, DK), nxt)],
        out_specs=[blk, blk, blk, blk, pl.BlockSpec((1, wide), lambda h, t: (0, h)), pl.BlockSpec((1, DK), lambda h, t: (0, 0))],
        out_shape=[SDS((s, dv_total), bf16)] * 4 + [SDS((1, HEADS * DK), f32), SDS((1, DK), f32)],
        scratch_shapes=[pltpu.VMEM((hp, DK, DK), f32)], compiler_params=_params(("arbitrary", "arbitrary")),
    )(proj, proj, proj, proj, lb_param, gnorm, masks, masks_t, o_saved, dog, states, states)


def _adamw(name, g, w, m, v):
    r, c = w.shape
    tr = r
    for cand in (256, 128, 64, 32, 16, 8):
        if r % cand == 0 and r > cand:
            tr = cand
            break

    def body(g_ref, w_ref, m_ref, v_ref, d_ref, mo_ref, vo_ref):
        d_ref[...], mo_ref[...], vo_ref[...] = _adamw_math(g_ref[...], w_ref[...], m_ref[...], v_ref[...])

    blk = pl.BlockSpec((tr, c), lambda i: (i, 0))
    return pl.pallas_call(
        body, name=name, grid=(r // tr,), in_specs=[blk] * 4, out_specs=[blk] * 3, out_shape=[SDS((r, c), f32)] * 3,
        compiler_params=_params(("parallel",)),
    )(g, w, m, v)


def _ffn_in(h2, w_gt, w_ut, tm, ffb):
    s, d = h2.shape
    dff = w_gt.shape[0]

    def body(a_ref, wg_ref, wu_ref, dg_ref, du_ref, act_ref):
        a = a_ref[...]
        g = lax.dot_general(a, wg_ref[...], (NT, ((), ())), preferred_element_type=f32)
        u = lax.dot_general(a, wu_ref[...], (NT, ((), ())), preferred_element_type=f32)
        sg = _sigmoid(g)
        silu = g * sg
        dg_ref[...] = (u * _dsilu(g, sg)).astype(bf16)
        du_ref[...] = silu.astype(bf16)
        act_ref[...] = (silu * u).astype(bf16)

    w_spec = pl.BlockSpec((ffb, d), lambda j, i: (j, 0))
    o_spec = pl.BlockSpec((tm, ffb), lambda j, i: (i, j))
    return pl.pallas_call(
        body, name="ffn_in", grid=(dff // ffb, s // tm), in_specs=[pl.BlockSpec((tm, d), lambda j, i: (i, 0)), w_spec, w_spec],
        out_specs=[o_spec] * 3, out_shape=[SDS((s, dff), bf16)] * 3, compiler_params=_params(("parallel", "parallel")),
    )(h2, w_gt, w_ut)


def _ffn_down_bwd(dff_out, w_d, act_dg, act_du, tm, ffb):
    s, d = dff_out.shape
    dff = w_d.shape[0]

    def body(a_ref, w_ref, fg_ref, fu_ref, dg_ref, du_ref):
        da = lax.dot_general(a_ref[...], w_ref[...], (NT, ((), ())), preferred_element_type=f32)
        dg_ref[...] = (da * fg_ref[...].astype(f32)).astype(bf16)
        du_ref[...] = (da * fu_ref[...].astype(f32)).astype(bf16)

    t_spec = pl.BlockSpec((tm, ffb), lambda j, i: (i, j))
    return pl.pallas_call(
        body, name="d_ffn_down_in", grid=(dff // ffb, s // tm),
        in_specs=[pl.BlockSpec((tm, d), lambda j, i: (i, 0)), pl.BlockSpec((ffb, d), lambda j, i: (j, 0)), t_spec, t_spec],
        out_specs=[t_spec] * 2, out_shape=[SDS((s, dff), bf16)] * 2, compiler_params=_params(("parallel", "parallel")),
    )(dff_out, w_d, act_dg, act_du)


def _branch_merge(z_a, og, w_co, w_ho, proj, tm, gate_a0, gate_b0):
    s, dc = z_a.shape
    nsh, _, n = w_co.shape

    def body(za_ref, og_ref, wa_ref, wb_ref, ga_ref, gb_ref, m_ref, sa_ref, sb_ref, fa_ref, fb_ref):
        ya = jnp.dot(za_ref[...], wa_ref[...], preferred_element_type=f32)
        yb = jnp.dot(og_ref[...], wb_ref[...], preferred_element_type=f32)
        sa, sb_ = _sigmoid(ga_ref[...]), _sigmoid(gb_ref[...])
        m_ref[...] = (sa * ya + sb_ * yb).astype(bf16)
        sa_ref[...] = sa.astype(bf16)
        sb_ref[...] = sb_.astype(bf16)
        fa_ref[...] = (ya * sa * (1.0 - sa)).astype(bf16)
        fb_ref[...] = (yb * sb_ * (1.0 - sb_)).astype(bf16)

    a_spec = pl.BlockSpec((tm, dc), lambda i, j: (i, 0))
    w_spec = pl.BlockSpec((None, dc, n), lambda i, j: (j, 0, 0))
    gate = lambda c0: pl.BlockSpec((tm, n), functools.partial(lambda i, j, cb: (i, cb + j), cb=c0 // n))
    o_spec = pl.BlockSpec((tm, n), lambda i, j: (i, j))
    return pl.pallas_call(
        body, name="branch_merge", grid=(s // tm, nsh), in_specs=[a_spec, a_spec, w_spec, w_spec, gate(gate_a0), gate(gate_b0)],
        out_specs=[o_spec] * 5, out_shape=[SDS((s, nsh * n), bf16)] * 5, compiler_params=_params(("parallel", "parallel")),
    )(z_a, og, w_co, w_ho, proj, proj)


def _d_branch_merge(dmo, w_o, factors, tm):
    s, d = dmo.shape
    n = d // 2

    def body(a_ref, w_ref, sa_ref, sb_ref, fa_ref, fb_ref, dya_ref, dyb_ref, dga_ref, dgb_ref):
        dm = lax.dot_general(a_ref[...], w_ref[...], (NT, ((), ())), preferred_element_type=f32)
        for f_ref, o_ref in ((sa_ref, dya_ref), (sb_ref, dyb_ref), (fa_ref, dga_ref), (fb_ref, dgb_ref)):
            o_ref[...] = (dm * f_ref[...].astype(f32)).astype(bf16)

    t_spec = pl.BlockSpec((tm, n), lambda j, i: (i, j))
    return pl.pallas_call(
        body, name="d_branch_merge", grid=(d // n, s // tm),
        in_specs=[pl.BlockSpec((tm, d), lambda j, i: (i, 0)), pl.BlockSpec((n, d), lambda j, i: (j, 0))] + [t_spec] * 4,
        out_specs=[t_spec] * 4, out_shape=[SDS((s, d), bf16)] * 4, compiler_params=_params(("parallel", "parallel")),
    )(dmo, w_o, *factors)


def _local_step(x, tgt, mod, g_mix, g_ffn, g_fin, lb_param, gnorm, conv_w, get_w, prefetch, put_g, sent):
    s, d = x.shape
    dc = d // 2
    tm = min(512, s)
    tm2 = min(1024, s)
    te = min(256, s)
    tb = min(128, s)
    mt = s // tm
    nb = 512
    sh_m, sc_m, gt_m, sh_f, sc_f, gt_f = [mod[i] for i in range(N_MOD)]
    dh2 = d // 2

    def e1(rows, fulls, outs, accs):
        xv = rows[0][...]
        g, sc, sh = [r[...] for r in fulls]
        _, xh = _rms(xv)
        outs[0][...] = (xh * g * (1.0 + sc) + sh).astype(bf16)

    h, = _rowwise("prenorm_mix", e1, s, te, [(x, d, 0)], [g_mix, sc_m, sh_m], [(d, bf16)])
    w_in, = get_w("in", h)
    nsh, _, win_sh = w_in.shape
    proj = _matmul("proj", h, w_in, (s, nsh * win_sh), f32, (nsh, s // tm2, 1), pl.BlockSpec((tm2, d), lambda j, i, k: (i, 0)),
                   pl.BlockSpec((None, d, win_sh), lambda j, i, k: (j, 0, 0)), pl.BlockSpec((tm2, win_sh), lambda j, i, k: (i, j)), NN)
    z_a = _conv_fwd(proj, _after_tokens(conv_w, [prefetch("mix", proj)]), s, dc)
    og, o_saved, states = _hgrn_fwd(proj, lb_param, gnorm, s, dc, tb)
    w_co, w_ho, w_o = get_w("mix", og)

    gate_a0, gate_b0 = 7 * dh2, 9 * dh2
    merged, *merge_factors = _branch_merge(z_a, og, w_co, w_ho, proj, tm2, gate_a0, gate_b0)
    mo = _matmul("mix_out", merged, w_o, (s, d), f32, (2, s // tm2, 1), pl.BlockSpec((tm2, d), lambda j, i, k: (i, 0)),
                 pl.BlockSpec((d, dh2), lambda j, i, k: (0, j)), pl.BlockSpec((tm2, dh2), lambda j, i, k: (i, j)), NN,
                 after=prefetch("ffn", merged))

    def e6(rows, fulls, outs, accs):
        xv, mov = rows[0][...], rows[1][...]
        gt, g, sc, sh = [r[...] for r in fulls]
        x1 = xv + gt * mov
        outs[0][...] = x1
        _, xh = _rms(x1)
        outs[1][...] = (xh * g * (1.0 + sc) + sh).astype(bf16)

    x1, h2 = _rowwise("prenorm_ffn", e6, s, te, [(x, d, 0), (mo, d, 0)], [gt_m, g_ffn, sc_f, sh_f], [(d, f32), (d, bf16)])
    w_gt, w_ut, w_d = get_w("ffn", h2)
    dff_ = w_d.shape[0]
    act_dg, act_du, act = _ffn_in(h2, w_gt, w_ut, tm2, nb)
    ff = _matmul("ffn_down", act, w_d, (s, d), f32, (2, mt, 1), pl.BlockSpec((tm, dff_), lambda j, i, k: (i, 0)),
                 pl.BlockSpec((dff_, dh2), lambda j, i, k: (0, j)), pl.BlockSpec((tm, dh2), lambda j, i, k: (i, j)), NN)

    def e9(rows, fulls, outs, accs):
        x1v, ffv, tv = [r[...] for r in rows]
        gt, gf = fulls[0][...], fulls[1][...]
        x2 = x1v + gt * ffv
        r, xh = _rms(x2)
        err = xh * gf - tv
        accs[0][...] += 0.5 * jnp.sum(jnp.mean(err * err, axis=-1, keepdims=True), axis=0, keepdims=True)
        dy = err / d
        accs[1][...] += _colsum(dy * xh)
        dxh = dy * gf
        dx2 = r * (dxh - xh * jnp.mean(dxh * xh, axis=-1, keepdims=True))
        outs[0][...] = dx2
        outs[1][...] = (dx2 * gt).astype(bf16)
        accs[2][...] += _colsum(dx2 * ffv)

    dx2, dff, loss_acc, dg_fin, dgt_f = _rowwise("loss_head", e9, s, te, [(x1, d, 0), (ff, d, 0), (tgt, d, 0)], [gt_f, g_fin],
                                                 [(d, f32), (d, bf16)], [(1, 128), (1, d), (1, d)])
    dgg, duu = _ffn_down_bwd(dff, w_d, act_dg, act_du, tm, dff_ // 4)

    def wgrad_rows(name, a, b, n_out):
        kb = 512
        return _matmul(name, a, b, (n_out, d), bf16, (n_out // kb, 2, 1), pl.BlockSpec((s, kb), lambda i, j, k: (0, i)),
                       pl.BlockSpec((s, dh2), lambda i, j, k: (0, j)), pl.BlockSpec((kb, dh2), lambda i, j, k: (i, j)), TN)

    gw_d = wgrad_rows("gw_ffn_down", act, dff, dff_)

    def ffn_in_bwd(name, a, w, after=None):
        return _matmul(name, a, w, (s, d), f32, (2, mt, 1), pl.BlockSpec((tm, dff_), lambda j, i, k: (i, 0)),
                       pl.BlockSpec((dff_, dh2), lambda j, i, k: (0, j)), pl.BlockSpec((tm, dh2), lambda j, i, k: (i, j)), NN,
                       after=after)

    dh2b = ffn_in_bwd("d_ffn_up_in", duu, w_ut)
    gw_ut = wgrad_rows("gw_ffn_up", duu, h2, dff_)
    gw_gt = wgrad_rows("gw_ffn_gate", dgg, h2, dff_)
    dh2a = ffn_in_bwd("d_ffn_gate_in", dgg, w_gt, after=put_g("ffn", [gw_gt, gw_ut, gw_d]))
    sc_f_late = _after_tokens(sc_f, [sent("ffn", dh2a)])

    def b5(rows, fulls, outs, accs):
        da, db, x1v, dx2v, mov = [r[...] for r in rows]
        sc, g, gt = [r[...] for r in fulls]
        dh = da + db
        r, xh = _rms(x1v)
        accs[0][...] += _colsum(dh)
        accs[1][...] += _colsum(dh * (xh * g))
        dn = dh * (1.0 + sc)
        accs[2][...] += _colsum(dn * xh)
        dxh = dn * g
        dx1 = dx2v + r * (dxh - xh * jnp.mean(dxh * xh, axis=-1, keepdims=True))
        outs[0][...] = dx1
        accs[3][...] += _colsum(dx1 * mov)
        outs[1][...] = (dx1 * gt).astype(bf16)

    dx1, dmo, dsh_f, dsc_f, dg_ffn, dgt_m = _rowwise(
        "d_prenorm_ffn", b5, s, te, [(dh2a, d, 0), (dh2b, d, 0), (x1, d, 0), (dx2, d, 0), (mo, d, 0)], [sc_f_late, g_ffn, gt_m],
        [(d, f32), (d, bf16)], [(1, d)] * 4)
    dya, dyb, dga, dgb = _d_branch_merge(dmo, w_o, merge_factors, tm)
    gw_o = wgrad_rows("gw_mix_out", merged, dmo, d)

    def out_proj_bwd(name, dy, w):
        return _matmul(name, dy, w, (s, dc), f32, (1, s // tm2, nsh), pl.BlockSpec((tm2, d // nsh), lambda j, i, k: (i, k)),
                       pl.BlockSpec((None, dc, d // nsh), lambda j, i, k: (k, 0, 0)), pl.BlockSpec((tm2, dc), lambda j, i, k: (i, 0)), NT)

    def out_proj_wgrad(name, a, dy):
        return _matmul(name, a, dy, (nsh, dc, d // nsh), bf16, (1, nsh, 1), pl.BlockSpec((s, dc), lambda i, j, k: (0, 0)),
                       pl.BlockSpec((s, d // nsh), lambda i, j, k: (0, j)), pl.BlockSpec((None, dc, d // nsh), lambda i, j, k: (j, 0, 0)), TN)

    dz_a = out_proj_bwd("d_conv_out_in", dya, w_co)
    dog = out_proj_bwd("d_hgrn_out_in", dyb, w_ho)
    gw_co = out_proj_wgrad("gw_conv_out", z_a, dya)
    gw_ho = out_proj_wgrad("gw_hgrn_out", og, dyb)
    conv_w_late = _after_tokens(conv_w, [put_g("mix", [gw_co, gw_ho, gw_o])])
    dab, dac, dax, dconv_w = _conv_bwd(proj, conv_w_late, dz_a, s, dc)
    lb_param_late = _after_tokens(lb_param, [sent("mix", dab)])
    dq, dfl, dvi, dgo, dlb, dgn = _hgrn_bwd(proj, lb_param_late, gnorm, o_saved, states, dog, s, dc, tb)
    dproj = jnp.concatenate([dab, dac, dax, dq, dfl, dvi, dgo, dga, dgb], axis=1)
    gw_in = _matmul("gw_proj", h, dproj, (nsh, d, win_sh), bf16, (nsh, d // 512, 1), pl.BlockSpec((s, 512), lambda j, i, k: (0, i)),
                    pl.BlockSpec((s, win_sh), lambda j, i, k: (0, j)), pl.BlockSpec((None, 512, win_sh), lambda j, i, k: (j, i, 0)), TN)
    dh = _matmul("d_proj_in", dproj, w_in, (s, d), f32, (1, s // tm2, nsh), pl.BlockSpec((tm2, win_sh), lambda j, i, k: (i, k)),
                 pl.BlockSpec((None, d, win_sh), lambda j, i, k: (k, 0, 0)), pl.BlockSpec((tm2, d), lambda j, i, k: (i, 0)), NT,
                 after=put_g("in", [gw_in]))

    def b12(rows, fulls, outs, accs):
        dhv, xv, dx1v = [r[...] for r in rows]
        sc, g = fulls[0][...], fulls[1][...]
        r, xh = _rms(xv)
        accs[0][...] += _colsum(dhv)
        accs[1][...] += _colsum(dhv * (xh * g))
        dn = dhv * (1.0 + sc)
        accs[2][...] += _colsum(dn * xh)
        dxh = dn * g
        outs[0][...] = dx1v + r * (dxh - xh * jnp.mean(dxh * xh, axis=-1, keepdims=True))

    dx, dsh_m, dsc_m, dg_mix = _rowwise("d_prenorm_mix", b12, s, te, [(dh, d, 0), (x, d, 0), (dx1, d, 0)], [sc_m, g_mix],
                                        [(d, f32)], [(1, d)] * 3)
    dmod = [dsh_m, dsc_m, dgt_m, dsh_f, dsc_f, dgt_f]
    small = dict(loss=loss_acc, g_mix=dg_mix, g_ffn=dg_ffn, g_fin=dg_fin, lb=dlb, gnorm=dgn, conv_w=dconv_w)
    return dx, dmod, small


def _ada_fwd(c_all, w_sh, b_sh):
    def body(c_ref, w_ref, b_ref, o_ref):
        cv = c_ref[...]
        ca = (cv * _sigmoid(cv)).astype(bf16)
        o_ref[...] = jnp.dot(ca, w_ref[...].astype(bf16), preferred_element_type=f32) + b_ref[...]

    return pl.pallas_call(body, name="ada_fwd", out_shape=SDS((c_all.shape[0], w_sh.shape[1]), f32),
                          compiler_params=pltpu.CompilerParams(vmem_limit_bytes=V7X_VMEM_LIMIT))(c_all, w_sh, b_sh)


def _ada_wgrad(c_all, dmod_sh):
    def body(c_ref, d_ref, o_ref):
        cv = c_ref[...]
        ca = (cv * _sigmoid(cv)).astype(bf16)
        o_ref[...] = lax.dot_general(ca, d_ref[...].astype(bf16), (TN, ((), ())), preferred_element_type=f32)

    return pl.pallas_call(body, name="ada_wgrad", out_shape=SDS((c_all.shape[1], dmod_sh.shape[1]), f32),
                          compiler_params=pltpu.CompilerParams(vmem_limit_bytes=V7X_VMEM_LIMIT))(c_all, dmod_sh)


def _lb_grad(lb_param, dlb):
    def body(p_ref, d_ref, o_ref):
        p = p_ref[...]
        lb = _sigmoid(p[0:1, :] - p[1:2, :])
        gl = d_ref[...] * lb * (1.0 - lb)
        o_ref[0:1, :] = gl
        o_ref[1:2, :] = -gl

    return pl.pallas_call(body, name="lb_grad", out_shape=SDS(lb_param.shape, f32))(lb_param, dlb)


def _sum_small(gathered):
    def body(g_ref, o_ref):
        acc = g_ref[0]
        for dd in range(1, NDEV):
            acc = acc + g_ref[dd]
        o_ref[...] = acc

    return pl.pallas_call(body, name="sum_small", out_shape=SDS(gathered.shape[1:], f32))(gathered)


def kernel(x, c, w_ada, b_ada, norm_mix_g, w_in, conv_w, lb_param, gnorm_g, w_conv_out, w_hgrn_out, w_o, norm_ffn_g, w_ffn_gate, w_ffn_up, w_ffn_down, norm_final_g, loss_target, m_w_ada, m_b_ada, m_norm_mix_g, m_w_in, m_conv_w, m_lb_param, m_gnorm_g, m_w_conv_out, m_w_hgrn_out, m_w_o, m_norm_ffn_g, m_w_ffn_gate, m_w_ffn_up, m_w_ffn_down, m_norm_final_g, v_w_ada, v_b_ada, v_norm_mix_g, v_w_in, v_conv_w, v_lb_param, v_gnorm_g, v_w_conv_out, v_w_hgrn_out, v_w_o, v_norm_ffn_g, v_w_ffn_gate, v_w_ffn_up, v_w_ffn_down, v_norm_final_g):
    assert lb_param.shape[0] == 2 and w_ada.shape[0] == 1
    s, d = x.shape[1], x.shape[2]
    me = 4 * lax.axis_index("x") + 2 * lax.axis_index("y") + lax.axis_index("c")
    ada_cols = w_ada.shape[2]

    me1 = me.astype(jnp.int32).reshape(1)
    placed_in = [_place_shard("place_in0", me1, w_in[0])]
    c_all, cw_all = _all_gather("gather_cond", [c, conv_w[0]], after=placed_in)
    c_all = c_all.reshape(NDEV, d)
    conv_w_full = jnp.transpose(cw_all, (1, 0, 2)).reshape(conv_w.shape[1], -1)
    b_sh = lax.dynamic_slice_in_dim(b_ada, me * ada_cols, ada_cols, axis=1)
    mod_cols = _ada_fwd(c_all, w_ada[0], b_sh)
    mod_all, = _all_gather("gather_mod", [mod_cols])
    mod = lax.dynamic_index_in_dim(mod_all, me, axis=1, keepdims=False).reshape(N_MOD, 1, d)

    shard_groups = {"in": None, "mix": [w_conv_out[0], w_hgrn_out[0], w_o[0]],
                    "ffn": [w_ffn_gate[0].T, w_ffn_up[0].T, w_ffn_down[0]]}
    own_slot = lambda frm, to: _flat(frm)
    gather_plan = lambda n: [(a, j, own_slot, own_slot) for a in range(n) for j in (1,) + ICI_RELATIONS]
    flat = lambda a: a.reshape(a.shape[0] * a.shape[1], a.shape[2])
    to8 = lambda a: a.reshape(NDEV, a.shape[0] // NDEV, a.shape[1])
    gathering, tokens = {}, []
    for grp, sh in shard_groups.items():
        lands = placed_in if sh is None else [_place_shard(f"place_{grp}{i}", me1, a, after=tokens[-1]) for i, a in enumerate(sh)]
        ss, rs, _, lands, tok = _push_start("gather_start_" + grp, [], lands, gather_plan(len(lands)),
                                            after=tokens[-1] if tokens else mod_all)
        gathering[grp] = (ss, rs, lands)
        tokens.append(tok)

    fwd_slot = lambda i: (lambda frm, to: _flat(_peer(frm, ICI_RELATIONS[i])))
    fwd_plan = lambda n: [(a, 1, fwd_slot(i), fwd_slot(i)) for a in range(n) for i in range(len(ICI_RELATIONS))]
    forwarding = {}

    def prefetch(grp, after):
        ss, rs, lands = gathering[grp]
        _, lands = _push_wait("gather_wait_" + grp, ss, rs, [], lands, gather_plan(len(lands)), after)
        ss, rs, _, lands, tok = _push_start("gather_fwd_" + grp, [], lands, fwd_plan(len(lands)))
        forwarding[grp] = (ss, rs, lands)
        return tok

    def get_w(grp, after):
        if grp in forwarding:
            ss, rs, lands = forwarding[grp]
            _, full = _push_wait("gather_fwd_wait_" + grp, ss, rs, [], lands, fwd_plan(len(lands)), after)
        else:
            ss, rs, lands = gathering[grp]
            _, lands = _push_wait("gather_wait_" + grp, ss, rs, [], lands, gather_plan(len(lands)), after)
            full = _sibling_forward("gather_fwd_" + grp, lands)
        return [f if grp == "in" or i < 2 and grp == "mix" else flat(f) for i, f in enumerate(full)]

    core = lax.axis_index("c").astype(jnp.int32).reshape(1)
    chip = (2 * lax.axis_index("x") + lax.axis_index("y")).astype(jnp.int32).reshape(1)
    scatter_plan = lambda n: [(a, j, lambda frm, to: _chip(to), lambda frm, to: _chip(frm)) for a in range(n) for j in ICI_RELATIONS]
    scattering = {}

    swap_plan = lambda n: [(a, 1, functools.partial(lambda frm, to, q: 2 * q + to[2], q=q), functools.partial(lambda frm, to, q: q, q=q))
                           for a in range(n) for q in range(NDEV // 2)]
    swapping = {}

    def start_ici(grp, g8, recv):
        pairs = [_pair_sum(f"pair_sum_{grp}{i}", core, g, r, _row_tile(g.shape[1], 1024)) for i, (g, r) in enumerate(zip(g8, recv))]
        lands = [lax.empty(p.shape, p.dtype) for p in pairs]
        ss, rs, srcs, lands, tok = _push_start("scatter_start_" + grp, pairs, lands, scatter_plan(len(pairs)))
        scattering[grp] = (ss, rs, srcs, lands)
        return tok

    def put_g(grp, grads):
        g8 = [g if g.ndim == 3 else to8(g) for g in grads]
        if grp == "in":
            return start_ici(grp, g8, _sibling_swap("scatter_pair_" + grp, g8))
        lands = [lax.empty((NDEV // 2,) + g.shape[1:], g.dtype) for g in g8]
        ss, rs, srcs, lands, tok = _push_start("scatter_swap_" + grp, g8, lands, swap_plan(len(g8)))
        swapping[grp] = (ss, rs, srcs, lands)
        return tok

    def sent(grp, after):
        ss, rs, srcs, lands = swapping[grp]
        g8, recv = _push_wait("scatter_swapped_" + grp, ss, rs, srcs, lands, swap_plan(len(srcs)), after)
        return start_ici(grp, g8, recv)

    def reduced(grp, after, names):
        ss, rs, srcs, lands = scattering[grp]
        srcs, lands = _push_wait("scatter_wait_" + grp, ss, rs, srcs, lands, scatter_plan(len(srcs)), after)
        for i, (p, r, nm) in enumerate(zip(srcs, lands, names)):
            tr = nm in ("w_ffn_gate", "w_ffn_up")
            wmv = tuple(a[0].T if tr else a[0] for a in weights[nm])
            out = _chip_sum(f"chip_sum_{grp}{i}", chip, p, r, _row_tile(p.shape[1]), wmv)
            res[nm] = [(a.T if tr else a).reshape(weights[nm][0].shape) for a in out]

    dx, dmod, small = _local_step(x[0], loss_target[0], mod, _after_tokens(norm_mix_g, tokens), norm_ffn_g,
                                  norm_final_g.reshape(1, d), lb_param, gnorm_g, conv_w_full, get_w, prefetch, put_g, sent)

    pieces = [*dmod, small["g_mix"], small["g_ffn"], small["g_fin"], small["lb"], small["gnorm"], small["loss"],
              small["conv_w"].reshape(1, -1)]
    widths = [p.shape[1] for p in pieces]
    offs = np.concatenate([[0], np.cumsum(widths)])
    packed = jnp.concatenate(pieces, axis=1)
    gathered, = _all_gather("gather_small", [packed])
    summed = _sum_small(gathered)
    part = lambda i: summed[:, offs[i]:offs[i + 1]]
    g_b_ada = summed[:, :N_MOD * d]
    g_norm_mix, g_norm_ffn, g_norm_fin, g_lb_row, g_gnorm, loss_vec, g_convw_flat = [part(i) for i in range(N_MOD, N_MOD + 7)]
    loss = loss_vec[0, 0]
    dmod_all = gathered[:, 0, :N_MOD * d]
    g_w_ada = _ada_wgrad(c_all, lax.dynamic_slice_in_dim(dmod_all, me * ada_cols, ada_cols, axis=1))
    g_lb = _lb_grad(lb_param, g_lb_row)
    cw_cols = conv_w.shape[2]
    g_conv_w = lax.dynamic_slice_in_dim(g_convw_flat.reshape(conv_w.shape[1], -1), me * cw_cols, cw_cols, axis=1)

    grads = dict(w_ada=g_w_ada, b_ada=g_b_ada, norm_mix_g=g_norm_mix, conv_w=g_conv_w, lb_param=g_lb, gnorm_g=g_gnorm,
                 norm_ffn_g=g_norm_ffn, norm_final_g=g_norm_fin)
    weights = dict(w_ada=(w_ada, m_w_ada, v_w_ada), b_ada=(b_ada, m_b_ada, v_b_ada), norm_mix_g=(norm_mix_g, m_norm_mix_g, v_norm_mix_g),
                   w_in=(w_in, m_w_in, v_w_in), conv_w=(conv_w, m_conv_w, v_conv_w), lb_param=(lb_param, m_lb_param, v_lb_param),
                   gnorm_g=(gnorm_g, m_gnorm_g, v_gnorm_g), w_conv_out=(w_conv_out, m_w_conv_out, v_w_conv_out),
                   w_hgrn_out=(w_hgrn_out, m_w_hgrn_out, v_w_hgrn_out), w_o=(w_o, m_w_o, v_w_o),
                   norm_ffn_g=(norm_ffn_g, m_norm_ffn_g, v_norm_ffn_g), w_ffn_gate=(w_ffn_gate, m_w_ffn_gate, v_w_ffn_gate),
                   w_ffn_up=(w_ffn_up, m_w_ffn_up, v_w_ffn_up), w_ffn_down=(w_ffn_down, m_w_ffn_down, v_w_ffn_down),
                   norm_final_g=(norm_final_g, m_norm_final_g, v_norm_final_g))
    res = {}

    def update(nm):
        w, m, v = weights[nm]
        shape2 = (w.shape[-2], w.shape[-1]) if w.ndim >= 2 else (1, w.shape[0])
        g2 = grads[nm].reshape(shape2)
        dl, mn, vn = _adamw("adamw_" + nm, g2, w.reshape(shape2), m.reshape(shape2), v.reshape(shape2))
        res[nm] = [a.reshape(w.shape) for a in (g2, dl, mn, vn)]

    for nm in list(grads):
        update(nm)
    reduced("ffn", res["w_ada"][1], ("w_ffn_gate", "w_ffn_up", "w_ffn_down"))
    reduced("mix", res["w_ffn_down"][1], ("w_conv_out", "w_hgrn_out", "w_o"))
    reduced("in", res["w_o"][1], ("w_in",))
    outs = [[res[nm][i] for nm in weights] for i in range(4)]
    return (loss, dx.reshape(x.shape), *outs[0], *outs[1], *outs[2], *outs[3])
```

```python
import functools

import numpy as np
import jax
import jax.numpy as jnp
from jax import lax
from jax.experimental import pallas as pl
from jax.experimental.pallas import tpu as pltpu

f32, bf16 = jnp.float32, jnp.bfloat16
SDS = jax.ShapeDtypeStruct

EPS = 1e-6
HEADS, DK, CHUNK = 8, 128, 64
HEADS_PER_STEP = 4
N_MOD = 6
NDEV = 8
ADAM_LR, ADAM_B1, ADAM_B2, ADAM_EPS, ADAM_WD, ADAM_STEP = 0.001, 0.9, 0.999, 1e-08, 0.01, 10
LEVELS = (32, 16, 8, 4, 2, 1)
V7X_VMEM_LIMIT = 56 * 1024 * 1024
HBM = pl.BlockSpec(memory_space=pltpu.HBM)
MESH = pl.DeviceIdType.MESH


def _params(sem):
    return pltpu.CompilerParams(dimension_semantics=sem, vmem_limit_bytes=V7X_VMEM_LIMIT)


def _sigmoid(x):
    return jax.nn.sigmoid(x)


def _dsilu(x, s):
    return s * (1.0 + x * (1.0 - s))


def _mesh_pos():
    x, y, c = lax.axis_index("x"), lax.axis_index("y"), lax.axis_index("c")
    return x, y, c


def _peer(pos, j):
    x, y, c = pos
    return (1 - x if j & 4 else x, 1 - y if j & 2 else y, 1 - c if j & 1 else c)


def _flat(pos):
    return 4 * pos[0] + 2 * pos[1] + pos[2]


def _all_gather(name, arrs, after=()):
    n, ne = len(arrs), len(after)
    out_shapes = [SDS((NDEV,) + a.shape, a.dtype) for a in arrs]

    def body(*refs):
        ins, outs = refs[:n], refs[n + ne:2 * n + ne]
        send_sems, recv_sems, local_sems = refs[2 * n + ne:]
        pos = _mesh_pos()
        me = _flat(pos)

        def copy(a, j, frm, to_pos):
            k = a * (NDEV - 1) + j - 1
            return pltpu.make_async_remote_copy(src_ref=ins[a], dst_ref=outs[a].at[frm], send_sem=send_sems.at[k],
                                                recv_sem=recv_sems.at[k], device_id=to_pos, device_id_type=MESH)

        local = [pltpu.make_async_copy(ins[a], outs[a].at[me], local_sems.at[a]) for a in range(n)]
        for cp in local:
            cp.start()
        sends = [copy(a, j, me, _peer(pos, j)) for j in range(1, NDEV) for a in range(n)]
        for cp in sends:
            cp.start()
        for j in range(1, NDEV):
            for a in range(n):
                copy(a, j, _flat(_peer(pos, j)), pos).wait_recv()
        for cp in sends:
            cp.wait_send()
        for cp in local:
            cp.wait()

    return pl.pallas_call(
        body, name=name, out_shape=out_shapes, in_specs=[HBM] * n + [ANY] * ne, out_specs=[HBM] * n,
        scratch_shapes=[pltpu.SemaphoreType.DMA((n * (NDEV - 1),)), pltpu.SemaphoreType.DMA((n * (NDEV - 1),)),
                        pltpu.SemaphoreType.DMA((n,))],
    )(*arrs, *after)


SEM = pl.BlockSpec(memory_space=pltpu.SEMAPHORE)
ANY = pl.BlockSpec(memory_space=pl.ANY)
EFFECT = pltpu.SideEffectType.DATAFLOW_SIDE_EFFECTING
ICI_RELATIONS = (2, 4, 6)


def _chip(pos):
    return 2 * pos[0] + pos[1]


def _hbm(a):
    return pltpu.with_memory_space_constraint(a, pltpu.HBM)


def _plan_copy(plan_entry, k, pos, frm, to, src_refs, land_refs, send_sems, recv_sems):
    a, _, src_slot, dst_slot = plan_entry
    s = src_refs[a] if src_slot is None else src_refs[a].at[src_slot(frm, to)]
    return pltpu.make_async_remote_copy(src_ref=s, dst_ref=land_refs[a].at[dst_slot(frm, to)], send_sem=send_sems.at[k],
                                        recv_sem=recv_sems.at[k], device_id=to, device_id_type=MESH)


def _push_start(name, srcs, lands, plan, after=None):
    ns, nb, nk = len(srcs), len(srcs) + len(lands), len(plan)
    extra = [] if after is None else [after]

    def body(*refs):
        land_refs = refs[ns:nb]
        src_refs = refs[:ns] if ns else land_refs
        send_sems, recv_sems = refs[nb + len(extra)], refs[nb + len(extra) + 1]
        pos = _mesh_pos()
        for k, e in enumerate(plan):
            _plan_copy(e, k, pos, pos, _peer(pos, e[1]), src_refs, land_refs, send_sems, recv_sems).start()
        refs[-1][...] = jnp.zeros_like(refs[-1])

    outs = pl.pallas_call(
        body, name=name,
        out_shape=[pltpu.SemaphoreType.DMA((nk,)), pltpu.SemaphoreType.DMA((nk,))] + [pltpu.HBM(a.shape, a.dtype) for a in srcs + lands]
        + [SDS((8, 128), f32)],
        in_specs=[HBM] * nb + [ANY] * len(extra), out_specs=[SEM, SEM] + [HBM] * nb + [pl.BlockSpec(memory_space=pltpu.VMEM)],
        input_output_aliases={i: 2 + i for i in range(nb)},
        compiler_params=pltpu.CompilerParams(has_side_effects=EFFECT),
    )(*[_hbm(a) for a in srcs + lands], *extra)
    return outs[0], outs[1], list(outs[2:2 + ns]), list(outs[2 + ns:2 + nb]), outs[-1]


def _push_wait(name, send_sems, recv_sems, srcs, lands, plan, after):
    ns, nb = len(srcs), len(srcs) + len(lands)

    def body(*refs):
        land_refs = refs[ns:nb]
        src_refs = refs[:ns] if ns else land_refs
        ssem, rsem = refs[nb], refs[nb + 1]
        pos = _mesh_pos()
        for k, e in enumerate(plan):
            peer = _peer(pos, e[1])
            _plan_copy(e, k, pos, pos, peer, src_refs, land_refs, ssem, rsem).wait_send()
            _plan_copy(e, k, pos, peer, pos, src_refs, land_refs, ssem, rsem).wait_recv()

    outs = pl.pallas_call(
        body, name=name, out_shape=[pltpu.HBM(a.shape, a.dtype) for a in srcs + lands],
        in_specs=[HBM] * nb + [SEM, SEM, ANY], out_specs=[HBM] * nb,
        input_output_aliases={i: i for i in range(nb)},
        compiler_params=pltpu.CompilerParams(has_side_effects=EFFECT),
    )(*srcs, *lands, send_sems, recv_sems, after)
    return list(outs[:ns]), list(outs[ns:])


def _after_tokens(small, tokens):
    for t in tokens:
        if t is not None:
            small = small + t[0:1, 0:1].reshape((1,) * small.ndim)
    return small


def _place_shard(name, me, shard, after=None):
    r, c = shard.shape
    tr = _row_tile(r)
    extra = [] if after is None else [after]

    def body(me_ref, s_ref, *rest):
        rest[-1][...] = s_ref[...].astype(bf16)

    return pl.pallas_call(
        body, name=name, out_shape=SDS((NDEV, r, c), bf16),
        grid_spec=pltpu.PrefetchScalarGridSpec(
            num_scalar_prefetch=1, grid=(r // tr,), in_specs=[pl.BlockSpec((tr, c), lambda i, me_ref: (i, 0))] + [ANY] * len(extra),
            out_specs=pl.BlockSpec((None, tr, c), lambda i, me_ref: (me_ref[0], i, 0))),
        compiler_params=_params(("parallel",)),
    )(me, shard, *extra)


def _row_tile(r, most=256):
    return max(t for t in range(16, most + 1, 16) if r % t == 0)


def _sibling_swap(name, grads):
    n = len(grads)
    nchip = NDEV // 2

    def body(*refs):
        g_refs, out_refs = refs[:n], refs[n:2 * n]
        send_sems, recv_sems = refs[2 * n:]
        pos = _mesh_pos()
        sib = _peer(pos, 1)
        sends = []
        for a in range(n):
            for q in range(nchip):
                k = a * nchip + q
                sends.append(pltpu.make_async_remote_copy(src_ref=g_refs[a].at[2 * q + sib[2]], dst_ref=out_refs[a].at[q],
                                                          send_sem=send_sems.at[k], recv_sem=recv_sems.at[k], device_id=sib,
                                                          device_id_type=MESH))
        for cp in sends:
            cp.start()
        for cp in sends:
            cp.wait()

    return pl.pallas_call(
        body, name=name, out_shape=[SDS((nchip,) + a.shape[1:], a.dtype) for a in grads], in_specs=[HBM] * n, out_specs=[HBM] * n,
        scratch_shapes=[pltpu.SemaphoreType.DMA((n * nchip,)), pltpu.SemaphoreType.DMA((n * nchip,))],
    )(*grads)


def _pair_sum(name, core, grad, recv, tr):
    _, r, c = grad.shape
    nchip = NDEV // 2

    def body(core_ref, g_ref, r_ref, o_ref):
        o_ref[...] = (g_ref[...].astype(f32) + r_ref[...].astype(f32)).astype(o_ref.dtype)

    return pl.pallas_call(
        body, name=name, out_shape=SDS((nchip, r, c), grad.dtype),
        grid_spec=pltpu.PrefetchScalarGridSpec(
            num_scalar_prefetch=1, grid=(nchip, r // tr),
            in_specs=[pl.BlockSpec((None, tr, c), lambda q, i, core_ref: (2 * q + core_ref[0], i, 0)),
                      pl.BlockSpec((None, tr, c), lambda q, i, core_ref: (q, i, 0))],
            out_specs=pl.BlockSpec((None, tr, c), lambda q, i, core_ref: (q, i, 0))),
        compiler_params=_params(("parallel", "parallel")),
    )(core, grad, recv)


def _adamw_math(g, w, m, v):
    mn = ADAM_B1 * m + (1.0 - ADAM_B1) * g
    vn = ADAM_B2 * v + (1.0 - ADAM_B2) * jnp.square(g)
    m_hat = mn / (1.0 - ADAM_B1 ** ADAM_STEP)
    v_hat = vn / (1.0 - ADAM_B2 ** ADAM_STEP)
    return -ADAM_LR * (m_hat / (jnp.sqrt(v_hat) + ADAM_EPS) + ADAM_WD * w), mn, vn


def _chip_sum_adamw(name, chip, pairs, recv, tr, wmv):
    nchip, r, c = pairs.shape

    def body(chip_ref, p_ref, r_ref, w_ref, m_ref, v_ref, g_ref, d_ref, mo_ref, vo_ref):
        mine = chip_ref[0]
        own = p_ref[...].astype(f32)
        acc = jnp.zeros((tr, c), f32)
        for q in range(nchip):
            acc = acc + jnp.where(mine == q, own, r_ref[q].astype(f32))
        g_ref[...] = acc
        d_ref[...], mo_ref[...], vo_ref[...] = _adamw_math(acc, w_ref[...], m_ref[...], v_ref[...])

    blk = pl.BlockSpec((tr, c), lambda i, chip_ref: (i, 0))
    return pl.pallas_call(
        body, name=name, out_shape=[SDS((r, c), f32)] * 4,
        grid_spec=pltpu.PrefetchScalarGridSpec(
            num_scalar_prefetch=1, grid=(r // tr,),
            in_specs=[pl.BlockSpec((None, tr, c), lambda i, chip_ref: (chip_ref[0], i, 0)),
                      pl.BlockSpec((nchip, tr, c), lambda i, chip_ref: (0, i, 0))] + [blk] * 3,
            out_specs=[blk] * 4),
        compiler_params=_params(("parallel",)),
    )(chip, pairs, recv, *wmv)


def _matmul(name, a, b, out_shape, out_dtype, grid, a_spec, b_spec, o_spec, dims, after=None):
    ksteps = grid[2]
    acc_shape = tuple(d for d in o_spec.block_shape if d is not None)
    extra = [] if after is None else [after]

    def body(a_ref, b_ref, *rest):
        o_ref, acc = rest[len(extra)], rest[len(extra) + 1:]
        prod = lax.dot_general(a_ref[...], b_ref[...], (dims, ((), ())), preferred_element_type=f32)
        if ksteps == 1:
            o_ref[...] = prod.astype(o_ref.dtype)
        else:
            k = pl.program_id(2)

            @pl.when(k == 0)
            def _():
                acc[0][...] = prod

            @pl.when(k > 0)
            def _():
                acc[0][...] += prod

            @pl.when(k == ksteps - 1)
            def _():
                o_ref[...] = acc[0][...].astype(o_ref.dtype)

    return pl.pallas_call(
        body, name=name, grid=grid, in_specs=[a_spec, b_spec] + [ANY] * len(extra), out_specs=o_spec,
        out_shape=SDS(out_shape, out_dtype), scratch_shapes=[] if ksteps == 1 else [pltpu.VMEM(acc_shape, f32)],
        compiler_params=_params(("parallel", "parallel", "arbitrary")),
    )(a, b, *extra)


NN, NT, TN = ((1,), (0,)), ((1,), (1,)), ((0,), (0,))


def _proj_part(name, h, w_sh, ids, prev, tm, after=None):
    s, d = h.shape
    nsh, _, n = w_sh.shape
    extra = ([] if prev is None else [prev]) + ([] if after is None else [after])

    def body(ids_ref, a_ref, b_ref, *rest):
        rest[-1][...] = jnp.dot(a_ref[...], b_ref[...], preferred_element_type=f32)

    return pl.pallas_call(
        body, name=name, out_shape=SDS((s, nsh * n), f32),
        grid_spec=pltpu.PrefetchScalarGridSpec(
            num_scalar_prefetch=1, grid=(ids.shape[0], s // tm),
            in_specs=[pl.BlockSpec((tm, d), lambda j, i, ids_ref: (i, 0)),
                      pl.BlockSpec((None, d, n), lambda j, i, ids_ref: (ids_ref[j], 0, 0))] + [ANY] * len(extra),
            out_specs=pl.BlockSpec((tm, n), lambda j, i, ids_ref: (i, ids_ref[j]))),
        input_output_aliases={} if prev is None else {3: 0},
        compiler_params=_params(("parallel", "parallel")),
    )(ids, h, w_sh, *extra)


def _rowwise(name, fn, n_rows, tm, rows, fulls, row_outs, acc_outs=(), ncol=1):
    assert ncol == 1 or not acc_outs
    nr, nf, no, na = len(rows), len(fulls), len(row_outs), len(acc_outs)
    in_specs = [pl.BlockSpec((tm, w), functools.partial(lambda i, j, cb: (i, cb + j), cb=cb)) for (_, w, cb) in rows]
    in_specs += [pl.BlockSpec(a.shape, functools.partial(lambda i, j, nd: (0,) * nd, nd=a.ndim)) for a in fulls]
    out_shape = [SDS((n_rows, w), dt) for (w, dt) in row_outs] + [SDS(s, f32) for s in acc_outs]
    out_specs = [pl.BlockSpec((tm, w // ncol), lambda i, j: (i, j)) for (w, _) in row_outs]
    out_specs += [pl.BlockSpec(s, functools.partial(lambda i, j, nd: (0,) * nd, nd=len(s))) for s in acc_outs]

    def body(*refs):
        if na:
            @pl.when(pl.program_id(0) == 0)
            def _():
                for r in refs[nr + nf + no:]:
                    r[...] = jnp.zeros(r.shape, r.dtype)
        fn(refs[:nr], refs[nr:nr + nf], refs[nr + nf:nr + nf + no], refs[nr + nf + no:])

    return pl.pallas_call(
        body, name=name, grid=(n_rows // tm, ncol), in_specs=in_specs, out_specs=out_specs, out_shape=out_shape,
        compiler_params=_params(("arbitrary" if na else "parallel", "arbitrary" if na else "parallel")),
    )(*[r[0] for r in rows], *fulls)


def _rms(x):
    r = lax.rsqrt(jnp.mean(x * x, axis=-1, keepdims=True) + EPS)
    return r, x * r


def _colsum(v):
    return jnp.sum(v, axis=0, keepdims=True)


def _shift_down(u, row, k):
    return jnp.where(row >= k, pltpu.roll(u, k, 0), 0.0)


def _shift_up(u, row, k):
    n = u.shape[0]
    return jnp.where(row < n - k, pltpu.roll(u, n - k, 0), 0.0)


def _conv_fwd(proj, conv_w, s, dc):
    nb = dc // 128

    def body(ab_ref, ac_ref, ax_ref, w_ref, z_ref):
        u = ac_ref[...] * ax_ref[...]
        row = lax.broadcasted_iota(jnp.int32, u.shape, 0)
        w = w_ref[...]
        cv = w[0:1] * _shift_down(u, row, 2) + w[1:2] * _shift_down(u, row, 1) + w[2:3] * u
        z_ref[...] = (ab_ref[...] * cv).astype(z_ref.dtype)

    col = lambda off: pl.BlockSpec((s, 128), functools.partial(lambda j, off: (0, off + j), off=off))
    return pl.pallas_call(
        body, name="conv_fwd", grid=(nb,), in_specs=[col(0), col(nb), col(2 * nb), pl.BlockSpec((3, 128), lambda j: (0, j))],
        out_specs=pl.BlockSpec((s, 128), lambda j: (0, j)), out_shape=SDS((s, dc), bf16), compiler_params=_params(("parallel",)),
    )(proj, proj, proj, conv_w)


def _conv_bwd(proj, conv_w, dz, s, dc):
    nb = dc // 128

    def body(ab_ref, ac_ref, ax_ref, w_ref, dz_ref, dab_ref, dac_ref, dax_ref, dw_ref):
        ab, ac, ax, dzv = ab_ref[...], ac_ref[...], ax_ref[...], dz_ref[...]
        u = ac * ax
        row = lax.broadcasted_iota(jnp.int32, u.shape, 0)
        w = w_ref[...]
        u1, u2 = _shift_down(u, row, 1), _shift_down(u, row, 2)
        cv = w[0:1] * u2 + w[1:2] * u1 + w[2:3] * u
        dcv = dzv * ab
        dab_ref[...] = (dzv * cv).astype(dab_ref.dtype)
        du = w[2:3] * dcv + w[1:2] * _shift_up(dcv, row, 1) + w[0:1] * _shift_up(dcv, row, 2)
        dac_ref[...] = (du * ax).astype(dac_ref.dtype)
        dax_ref[...] = (du * ac).astype(dax_ref.dtype)
        dw_ref[0:1, :] = _colsum(dcv * u2)
        dw_ref[1:2, :] = _colsum(dcv * u1)
        dw_ref[2:3, :] = _colsum(dcv * u)

    col = lambda off: pl.BlockSpec((s, 128), functools.partial(lambda j, off: (0, off + j), off=off))
    blk = pl.BlockSpec((s, 128), lambda j: (0, j))
    return pl.pallas_call(
        body, name="conv_bwd", grid=(nb,),
        in_specs=[col(0), col(nb), col(2 * nb), pl.BlockSpec((3, 128), lambda j: (0, j)), blk],
        out_specs=[blk, blk, blk, pl.BlockSpec((3, 128), lambda j: (0, j))],
        out_shape=[SDS((s, dc), bf16)] * 3 + [SDS((3, dc), f32)], compiler_params=_params(("parallel",)),
    )(proj, proj, proj, conv_w, dz)


def _level_masks():
    t = np.arange(CHUNK)[:, None]
    s = np.arange(CHUNK)[None, :]
    m = np.stack([((t & h) != 0) & ((s & h) == 0) & (t // (2 * h) == s // (2 * h)) for h in LEVELS]).astype(np.float32)
    return jnp.asarray(m), jnp.asarray(m.transpose(0, 2, 1))


def _cumsum_rows(x, row):
    for sh in (1, 2, 4, 8, 16, 32):
        x = x + jnp.where(row >= sh, pltpu.roll(x, sh, 0), 0.0)
    return x


def _rev_cumsum_rows(x, row):
    n = x.shape[0]
    for sh in (1, 2, 4, 8, 16, 32):
        x = x + jnp.where(row < n - sh, pltpu.roll(x, n - sh, 0), 0.0)
    return x


def _chunk_terms(qp, fl, lb):
    row = lax.broadcasted_iota(jnp.int32, qp.shape, 0)
    sig = _sigmoid(fl)
    f = lb + (1.0 - lb) * sig
    k = 1.0 - f
    sq = _sigmoid(qp)
    qh = qp * sq
    b = _cumsum_rows(jnp.log(f), row)
    sub = lax.broadcasted_iota(jnp.int32, (CHUNK // 8, 8, DK), 1)
    b8 = b.reshape(CHUNK // 8, 8, DK)
    us, exs, ups = [], [], []
    for m in LEVELS:
        sb = 2 * m
        if sb >= 8:
            b3 = b.reshape(CHUNK // sb, sb, DK)
            bref = jnp.broadcast_to(b3[:, m - 1:m, :], b3.shape).reshape(CHUNK, DK)
        else:
            bref8 = None
            for j in range(8 // sb):
                cand = jnp.broadcast_to(b8[:, j * sb + m - 1:j * sb + m, :], b8.shape)
                bref8 = cand if bref8 is None else jnp.where(sub >= j * sb, cand, bref8)
            bref = bref8.reshape(CHUNK, DK)
        up = (row & m) != 0
        ex = jnp.exp(jnp.where(up, b - bref, bref - b))
        us.append((jnp.where(up, qh, k) * ex).astype(bf16))
        exs.append(ex)
        ups.append(up)
    blast = b[CHUNK - 1:CHUNK, :]
    eb, ebl = jnp.exp(b), jnp.exp(blast - b)
    return dict(sig=sig, f=f, k=k, sq=sq, qh=qh, u=jnp.stack(us), ex=exs, up=ups, eb=eb, ebl=ebl, qt=qh * eb, kt=k * ebl,
                el=jnp.exp(blast), row=row)


def _scores(t, mask):
    pl_ = jnp.einsum("ltk,lsk->lts", t["u"], t["u"], preferred_element_type=f32)
    p = jnp.sum(pl_ * mask, axis=0)
    r = lax.broadcasted_iota(jnp.int32, (CHUNK, CHUNK), 0)
    c = lax.broadcasted_iota(jnp.int32, (CHUNK, CHUNK), 1)
    diag = jnp.sum(t["qh"] * t["k"], axis=-1, keepdims=True)
    return p + jnp.where(r == c, diag, 0.0)


def _hgrn_fwd(proj, lb_param, gnorm, s, dv_total, tb):
    nchunk = tb // CHUNK
    masks, _ = _level_masks()
    q0, f0, v0, g0 = 3 * HEADS, 4 * HEADS, 5 * HEADS, 6 * HEADS

    def body(q_ref, f_ref, v_ref, g_ref, lb_ref, gn_ref, mask_ref, og_ref, o_ref, st_ref, state):
        @pl.when(pl.program_id(1) == 0)
        def _():
            state[...] = jnp.zeros_like(state)

        lbp = lb_ref[...]
        lb_all = _sigmoid(lbp[0:1, :] - lbp[1:2, :])
        mask = mask_ref[...]
        for i, hh in [(i, hh) for i in range(nchunk) for hh in range(HEADS_PER_STEP)]:
            rs, cs = pl.ds(i * CHUNK, CHUNK), pl.ds(hh * DK, DK)
            t = _chunk_terms(q_ref[rs, cs], f_ref[rs, cs], lb_all[:, hh * DK:(hh + 1) * DK])
            v = v_ref[rs, cs]
            vb = v.astype(bf16)
            st = state[hh]
            st_ref[i, hh] = st
            p = _scores(t, mask)
            o = jnp.dot(p.astype(bf16), vb, preferred_element_type=f32)
            o += lax.dot_general(t["qt"].astype(bf16), st.astype(bf16), (NT, ((), ())), preferred_element_type=f32)
            state[hh] = st * t["el"] + lax.dot_general(vb, t["kt"].astype(bf16), (TN, ((), ())), preferred_element_type=f32)
            o_ref[rs, cs] = o
            r, oh = _rms(o)
            g = g_ref[rs, cs]
            og_ref[rs, cs] = (oh * gn_ref[...] * (g * _sigmoid(g))).astype(og_ref.dtype)

    hp, wide = HEADS_PER_STEP, HEADS_PER_STEP * DK
    col = lambda off: pl.BlockSpec((tb, wide), functools.partial(lambda h, t, off: (t, off + h), off=off // hp))
    blk = pl.BlockSpec((tb, wide), lambda h, t: (t, h))
    return pl.pallas_call(
        body, name="hgrn_fwd", grid=(HEADS // hp, s // tb),
        in_specs=[col(q0), col(f0), col(v0), col(g0), pl.BlockSpec((2, wide), lambda h, t: (0, h)),
                  pl.BlockSpec((1, DK), lambda h, t: (0, 0)), pl.BlockSpec(masks.shape, lambda h, t: (0, 0, 0))],
        out_specs=[blk, blk, pl.BlockSpec((nchunk, hp, DK, DK), lambda h, t: (t, h, 0, 0))],
        out_shape=[SDS((s, dv_total), bf16), SDS((s, dv_total), f32), SDS((s // CHUNK, HEADS, DK, DK), f32)],
        scratch_shapes=[pltpu.VMEM((hp, DK, DK), f32)], compiler_params=_params(("parallel", "arbitrary")),
    )(proj, proj, proj, proj, lb_param, gnorm, masks)


def _hgrn_bwd(proj, lb_param, gnorm, o_saved, states, dog, s, dv_total, tb):
    nchunk = tb // CHUNK
    nt = s // tb
    nc_total = s // CHUNK
    masks, masks_t = _level_masks()
    q0, f0, v0, g0 = 3 * HEADS, 4 * HEADS, 5 * HEADS, 6 * HEADS

    def body(q_ref, f_ref, v_ref, g_ref, lb_ref, gn_ref, mask_ref, maskt_ref, o_ref, dog_ref, st_ref, stn_ref,
             dq_ref, df_ref, dv_ref, dg_ref, dlb_ref, dgn_ref, gstate):
        h_id, t_id = pl.program_id(0), pl.program_id(1)

        @pl.when(t_id == 0)
        def _():
            gstate[...] = jnp.zeros_like(gstate)
            dlb_ref[...] = jnp.zeros_like(dlb_ref)

        @pl.when((t_id == 0) & (h_id == 0))
        def _():
            dgn_ref[...] = jnp.zeros_like(dgn_ref)

        lbp = lb_ref[...]
        lb_all = _sigmoid(lbp[0:1, :] - lbp[1:2, :])
        mask, maskt = mask_ref[...], maskt_ref[...]
        gn = gn_ref[...]
        for i, hh in [(i, hh) for i in reversed(range(nchunk)) for hh in range(HEADS_PER_STEP)]:
            rs, cs = pl.ds(i * CHUNK, CHUNK), pl.ds(hh * DK, DK)
            lb = lb_all[:, hh * DK:(hh + 1) * DK]
            qp, fl, v, g = q_ref[rs, cs], f_ref[rs, cs], v_ref[rs, cs], g_ref[rs, cs]
            t = _chunk_terms(qp, fl, lb)
            vb = v.astype(bf16)
            st0 = st_ref[i, hh]
            st1 = st_ref[i + 1, hh] if i + 1 < nchunk else stn_ref[0, hh]
            gt = gstate[hh]
            o = o_ref[rs, cs]
            r, oh = _rms(o)
            sg = _sigmoid(g)
            dog_v = dog_ref[rs, cs]
            dg_ref[rs, cs] = (dog_v * (oh * gn) * _dsilu(g, sg)).astype(dg_ref.dtype)
            don = dog_v * (g * sg)
            dgn_ref[...] += _colsum(don * oh)
            doh = don * gn
            do = r * (doh - oh * jnp.mean(doh * oh, axis=-1, keepdims=True))
            dob = do.astype(bf16)
            d = lax.dot_general(dob, vb, (NT, ((), ())), preferred_element_type=f32)
            dt = lax.dot_general(vb, dob, (NT, ((), ())), preferred_element_type=f32)
            z = (mask * d[None] + maskt * dt[None]).astype(bf16)
            rr = jnp.einsum("lts,lsk->ltk", z, t["u"], preferred_element_type=f32)
            dq = jnp.zeros((CHUNK, DK), f32)
            dk = jnp.zeros((CHUNK, DK), f32)
            qdk = jnp.zeros((CHUNK, DK), f32)
            for li in range(len(LEVELS)):
                du = t["ex"][li] * rr[li]
                dq += jnp.where(t["up"][li], du, 0.0)
                dk += jnp.where(t["up"][li], 0.0, du)
                e = t["u"][li].astype(f32) * rr[li]
                qdk += jnp.where(t["up"][li], e, -e)
            dd = jnp.sum(do * v, axis=-1, keepdims=True)
            dq += dd * t["k"]
            dk += dd * t["qh"]
            gtb = gt.astype(bf16)
            ktb, qtb = t["kt"].astype(bf16), t["qt"].astype(bf16)
            dq_in = jnp.dot(dob, st0.astype(bf16), preferred_element_type=f32)
            dk_in = jnp.dot(vb, gtb, preferred_element_type=f32)
            dq += t["eb"] * dq_in
            dk += t["ebl"] * dk_in
            qdk += qtb.astype(f32) * dq_in - ktb.astype(f32) * dk_in
            p = _scores(t, mask)
            dvv = lax.dot_general(p.astype(bf16), dob, (TN, ((), ())), preferred_element_type=f32)
            dvv += lax.dot_general(ktb, gtb, (NT, ((), ())), preferred_element_type=f32)
            dv_ref[rs, cs] = dvv.astype(dv_ref.dtype)
            a_end = _colsum(gtb.astype(f32) * st1)
            dlf = _rev_cumsum_rows(qdk, t["row"]) + a_end
            dfv = dlf / t["f"] - dk
            df_ref[rs, cs] = (dfv * (1.0 - lb) * t["sig"] * (1.0 - t["sig"])).astype(df_ref.dtype)
            dlb_ref[:, cs] += _colsum(dfv * (1.0 - t["sig"]))
            dq_ref[rs, cs] = (dq * _dsilu(qp, t["sq"])).astype(dq_ref.dtype)
            gstate[hh] = gt * t["el"] + lax.dot_general(dob, qtb, (TN, ((), ())), preferred_element_type=f32)

    hp, wide = HEADS_PER_STEP, HEADS_PER_STEP * DK
    rev = lambda t: nt - 1 - t
    col = lambda off: pl.BlockSpec((tb, wide), functools.partial(lambda h, t, off: (rev(t), off + h), off=off // hp))
    blk = pl.BlockSpec((tb, wide), lambda h, t: (rev(t), h))
    nxt = lambda h, t: (jnp.minimum((rev(t) + 1) * nchunk, nc_total - 1), h, 0, 0)
    return pl.pallas_call(
        body, name="hgrn_bwd", grid=(HEADS // hp, nt),
        in_specs=[col(q0), col(f0), col(v0), col(g0), pl.BlockSpec((2, wide), lambda h, t: (0, h)),
                  pl.BlockSpec((1, DK), lambda h, t: (0, 0)), pl.BlockSpec(masks.shape, lambda h, t: (0, 0, 0)),
                  pl.BlockSpec(masks.shape, lambda h, t: (0, 0, 0)), blk, blk,
                  pl.BlockSpec((nchunk, hp, DK, DK), lambda h, t: (rev(t), h, 0, 0)),
                  pl.BlockSpec((1, hp, DK, DK), nxt)],
        out_specs=[blk, blk, blk, blk, pl.BlockSpec((1, wide), lambda h, t: (0, h)), pl.BlockSpec((1, DK), lambda h, t: (0, 0))],
        out_shape=[SDS((s, dv_total), bf16)] * 4 + [SDS((1, HEADS * DK), f32), SDS((1, DK), f32)],
        scratch_shapes=[pltpu.VMEM((hp, DK, DK), f32)], compiler_params=_params(("arbitrary", "arbitrary")),
    )(proj, proj, proj, proj, lb_param, gnorm, masks, masks_t, o_saved, dog, states, states)


def _adamw(name, g, w, m, v):
    r, c = w.shape
    tr = r
    for cand in (256, 128, 64, 32, 16, 8):
        if r % cand == 0 and r > cand:
            tr = cand
            break

    def body(g_ref, w_ref, m_ref, v_ref, d_ref, mo_ref, vo_ref):
        d_ref[...], mo_ref[...], vo_ref[...] = _adamw_math(g_ref[...], w_ref[...], m_ref[...], v_ref[...])

    blk = pl.BlockSpec((tr, c), lambda i: (i, 0))
    return pl.pallas_call(
        body, name=name, grid=(r // tr,), in_specs=[blk] * 4, out_specs=[blk] * 3, out_shape=[SDS((r, c), f32)] * 3,
        compiler_params=_params(("parallel",)),
    )(g, w, m, v)


def _ffn_in(h2, w_gt, w_ut, tm, ffb):
    s, d = h2.shape
    dff = w_gt.shape[0]

    def body(a_ref, wg_ref, wu_ref, dg_ref, du_ref, act_ref):
        a = a_ref[...]
        g = lax.dot_general(a, wg_ref[...], (NT, ((), ())), preferred_element_type=f32)
        u = lax.dot_general(a, wu_ref[...], (NT, ((), ())), preferred_element_type=f32)
        sg = _sigmoid(g)
        silu = g * sg
        dg_ref[...] = (u * _dsilu(g, sg)).astype(bf16)
        du_ref[...] = silu.astype(bf16)
        act_ref[...] = (silu * u).astype(bf16)

    w_spec = pl.BlockSpec((ffb, d), lambda j, i: (j, 0))
    o_spec = pl.BlockSpec((tm, ffb), lambda j, i: (i, j))
    return pl.pallas_call(
        body, name="ffn_in", grid=(dff // ffb, s // tm), in_specs=[pl.BlockSpec((tm, d), lambda j, i: (i, 0)), w_spec, w_spec],
        out_specs=[o_spec] * 3, out_shape=[SDS((s, dff), bf16)] * 3, compiler_params=_params(("parallel", "parallel")),
    )(h2, w_gt, w_ut)


def _ffn_down_bwd(dff_out, w_d, act_dg, act_du, tm, ffb):
    s, d = dff_out.shape
    dff = w_d.shape[0]

    def body(a_ref, w_ref, fg_ref, fu_ref, dg_ref, du_ref):
        da = lax.dot_general(a_ref[...], w_ref[...], (NT, ((), ())), preferred_element_type=f32)
        dg_ref[...] = (da * fg_ref[...].astype(f32)).astype(bf16)
        du_ref[...] = (da * fu_ref[...].astype(f32)).astype(bf16)

    t_spec = pl.BlockSpec((tm, ffb), lambda j, i: (i, j))
    return pl.pallas_call(
        body, name="d_ffn_down_in", grid=(dff // ffb, s // tm),
        in_specs=[pl.BlockSpec((tm, d), lambda j, i: (i, 0)), pl.BlockSpec((ffb, d), lambda j, i: (j, 0)), t_spec, t_spec],
        out_specs=[t_spec] * 2, out_shape=[SDS((s, dff), bf16)] * 2, compiler_params=_params(("parallel", "parallel")),
    )(dff_out, w_d, act_dg, act_du)


def _branch_merge(z_a, og, w_co, w_ho, proj, tm, gate_a0, gate_b0):
    s, dc = z_a.shape
    nsh, _, n = w_co.shape

    def body(za_ref, og_ref, wa_ref, wb_ref, ga_ref, gb_ref, m_ref, sa_ref, sb_ref, fa_ref, fb_ref):
        ya = jnp.dot(za_ref[...], wa_ref[...], preferred_element_type=f32)
        yb = jnp.dot(og_ref[...], wb_ref[...], preferred_element_type=f32)
        sa, sb_ = _sigmoid(ga_ref[...]), _sigmoid(gb_ref[...])
        m_ref[...] = (sa * ya + sb_ * yb).astype(bf16)
        sa_ref[...] = sa.astype(bf16)
        sb_ref[...] = sb_.astype(bf16)
        fa_ref[...] = (ya * sa * (1.0 - sa)).astype(bf16)
        fb_ref[...] = (yb * sb_ * (1.0 - sb_)).astype(bf16)

    a_spec = pl.BlockSpec((tm, dc), lambda i, j: (i, 0))
    w_spec = pl.BlockSpec((None, dc, n), lambda i, j: (j, 0, 0))
    gate = lambda c0: pl.BlockSpec((tm, n), functools.partial(lambda i, j, cb: (i, cb + j), cb=c0 // n))
    o_spec = pl.BlockSpec((tm, n), lambda i, j: (i, j))
    return pl.pallas_call(
        body, name="branch_merge", grid=(s // tm, nsh), in_specs=[a_spec, a_spec, w_spec, w_spec, gate(gate_a0), gate(gate_b0)],
        out_specs=[o_spec] * 5, out_shape=[SDS((s, nsh * n), bf16)] * 5, compiler_params=_params(("parallel", "parallel")),
    )(z_a, og, w_co, w_ho, proj, proj)


def _d_branch_merge(dmo, w_o, factors, tm):
    s, d = dmo.shape
    n = d // 2

    def body(a_ref, w_ref, sa_ref, sb_ref, fa_ref, fb_ref, dya_ref, dyb_ref, dga_ref, dgb_ref):
        dm = lax.dot_general(a_ref[...], w_ref[...], (NT, ((), ())), preferred_element_type=f32)
        for f_ref, o_ref in ((sa_ref, dya_ref), (sb_ref, dyb_ref), (fa_ref, dga_ref), (fb_ref, dgb_ref)):
            o_ref[...] = (dm * f_ref[...].astype(f32)).astype(bf16)

    t_spec = pl.BlockSpec((tm, n), lambda j, i: (i, j))
    return pl.pallas_call(
        body, name="d_branch_merge", grid=(d // n, s // tm),
        in_specs=[pl.BlockSpec((tm, d), lambda j, i: (i, 0)), pl.BlockSpec((n, d), lambda j, i: (j, 0))] + [t_spec] * 4,
        out_specs=[t_spec] * 4, out_shape=[SDS((s, d), bf16)] * 4, compiler_params=_params(("parallel", "parallel")),
    )(dmo, w_o, *factors)


def _local_step(x, tgt, mod, g_mix, g_ffn, g_fin, lb_param, gnorm, conv_w, project, get_w, prefetch, put_g, sent):
    s, d = x.shape
    dc = d // 2
    tm = min(512, s)
    tm2 = min(1024, s)
    te = min(256, s)
    tb = min(128, s)
    mt = s // tm
    nb = 512
    sh_m, sc_m, gt_m, sh_f, sc_f, gt_f = [mod[i] for i in range(N_MOD)]
    dh2 = d // 2

    def e1(rows, fulls, outs, accs):
        xv = rows[0][...]
        g, sc, sh = [r[...] for r in fulls]
        _, xh = _rms(xv)
        outs[0][...] = (xh * g * (1.0 + sc) + sh).astype(bf16)

    h, = _rowwise("prenorm_mix", e1, s, te, [(x, d, 0)], [g_mix, sc_m, sh_m], [(d, bf16)])
    proj, w_in = project(h, tm2)
    nsh, _, win_sh = w_in.shape
    z_a = _conv_fwd(proj, _after_tokens(conv_w, [prefetch("mix", proj)]), s, dc)
    og, o_saved, states = _hgrn_fwd(proj, lb_param, gnorm, s, dc, tb)
    w_co, w_ho, w_o = get_w("mix", og)

    gate_a0, gate_b0 = 7 * dh2, 9 * dh2
    merged, *merge_factors = _branch_merge(z_a, og, w_co, w_ho, proj, tm2, gate_a0, gate_b0)
    mo = _matmul("mix_out", merged, w_o, (s, d), f32, (2, s // tm2, 1), pl.BlockSpec((tm2, d), lambda j, i, k: (i, 0)),
                 pl.BlockSpec((d, dh2), lambda j, i, k: (0, j)), pl.BlockSpec((tm2, dh2), lambda j, i, k: (i, j)), NN,
                 after=prefetch("ffn", merged))

    def e6(rows, fulls, outs, accs):
        xv, mov = rows[0][...], rows[1][...]
        gt, g, sc, sh = [r[...] for r in fulls]
        x1 = xv + gt * mov
        outs[0][...] = x1
        _, xh = _rms(x1)
        outs[1][...] = (xh * g * (1.0 + sc) + sh).astype(bf16)

    x1, h2 = _rowwise("prenorm_ffn", e6, s, tm, [(x, d, 0),(mo, d, 0)], [gt_m, g_ffn, sc_f, sh_f], [(d, f32), (d, bf16)])
    w_gt, w_ut, w_d = get_w("ffn", h2)
    dff_ = w_d.shape[0]
    act_dg, act_du, act = _ffn_in(h2, w_gt, w_ut, tm2, nb)
    ff = _matmul("ffn_down", act, w_d, (s, d), f32, (2, mt, 1), pl.BlockSpec((tm, dff_), lambda j, i, k: (i, 0)),
                 pl.BlockSpec((dff_, dh2), lambda j, i, k: (0, j)), pl.BlockSpec((tm, dh2), lambda j, i, k: (i, j)), NN)

    def e9(rows, fulls, outs, accs):
        x1v, ffv, tv = [r[...] for r in rows]
        gt, gf = fulls[0][...], fulls[1][...]
        x2 = x1v + gt * ffv
        r, xh = _rms(x2)
        err = xh * gf - tv
        accs[0][...] += 0.5 * jnp.sum(jnp.mean(err * err, axis=-1, keepdims=True), axis=0, keepdims=True)
        dy = err / d
        accs[1][...] += _colsum(dy * xh)
        dxh = dy * gf
        dx2 = r * (dxh - xh * jnp.mean(dxh * xh, axis=-1, keepdims=True))
        outs[0][...] = dx2
        outs[1][...] = (dx2 * gt).astype(bf16)
        accs[2][...] += _colsum(dx2 * ffv)

    dx2, dff, loss_acc, dg_fin, dgt_f = _rowwise("loss_head", e9, s, tm, [(x1, d, 0), (ff, d, 0), (tgt, d, 0)], [gt_f, g_fin],
                                                 [(d, f32), (d, bf16)], [(1, 128), (1, d), (1, d)])
    dgg, duu = _ffn_down_bwd(dff, w_d, act_dg, act_du, tm, dff_ // 4)

    def wgrad_rows(name, a, b, n_out):
        kb = 512
        return _matmul(name, a, b, (n_out, d), bf16, (n_out // kb, 2, 1), pl.BlockSpec((s, kb), lambda i, j, k: (0, i)),
                       pl.BlockSpec((s, dh2), lambda i, j, k: (0, j)), pl.BlockSpec((kb, dh2), lambda i, j, k: (i, j)), TN)

    gw_d = wgrad_rows("gw_ffn_down", act, dff, dff_)

    def ffn_in_bwd(name, a, w, after=None):
        return _matmul(name, a, w, (s, d), f32, (2, mt, 1), pl.BlockSpec((tm, dff_), lambda j, i, k: (i, 0)),
                       pl.BlockSpec((dff_, dh2), lambda j, i, k: (0, j)), pl.BlockSpec((tm, dh2), lambda j, i, k: (i, j)), NN,
                       after=after)

    dh2b = ffn_in_bwd("d_ffn_up_in", duu, w_ut)
    gw_ut = wgrad_rows("gw_ffn_up", duu, h2, dff_)
    gw_gt = wgrad_rows("gw_ffn_gate", dgg, h2, dff_)
    dh2a = ffn_in_bwd("d_ffn_gate_in", dgg, w_gt, after=put_g("ffn", [gw_gt, gw_ut, gw_d]))
    sc_f_late = _after_tokens(sc_f, [sent("ffn", dh2a)])

    def b5(rows, fulls, outs, accs):
        da, db, x1v, dx2v, mov = [r[...] for r in rows]
        sc, g, gt = [r[...] for r in fulls]
        dh = da + db
        r, xh = _rms(x1v)
        accs[0][...] += _colsum(dh)
        accs[1][...] += _colsum(dh * (xh * g))
        dn = dh * (1.0 + sc)
        accs[2][...] += _colsum(dn * xh)
        dxh = dn * g
        dx1 = dx2v + r * (dxh - xh * jnp.mean(dxh * xh, axis=-1, keepdims=True))
        outs[0][...] = dx1
        accs[3][...] += _colsum(dx1 * mov)
        outs[1][...] = (dx1 * gt).astype(bf16)

    dx1, dmo, dsh_f, dsc_f, dg_ffn, dgt_m = _rowwise(
        "d_prenorm_ffn", b5, s, te, [(dh2a, d, 0), (dh2b, d, 0), (x1, d, 0), (dx2, d, 0), (mo, d, 0)], [sc_f_late, g_ffn, gt_m],
        [(d, f32), (d, bf16)], [(1, d)] * 4)
    dya, dyb, dga, dgb = _d_branch_merge(dmo, w_o, merge_factors, tm)
    gw_o = wgrad_rows("gw_mix_out", merged, dmo, d)

    def out_proj_bwd(name, dy, w):
        return _matmul(name, dy, w, (s, dc), f32, (1, s // tm2, nsh), pl.BlockSpec((tm2, d // nsh), lambda j, i, k: (i, k)),
                       pl.BlockSpec((None, dc, d // nsh), lambda j, i, k: (k, 0, 0)), pl.BlockSpec((tm2, dc), lambda j, i, k: (i, 0)), NT)

    def out_proj_wgrad(name, a, dy):
        return _matmul(name, a, dy, (nsh, dc, d // nsh), bf16, (1, nsh, 1), pl.BlockSpec((s, dc), lambda i, j, k: (0, 0)),
                       pl.BlockSpec((s, d // nsh), lambda i, j, k: (0, j)), pl.BlockSpec((None, dc, d // nsh), lambda i, j, k: (j, 0, 0)), TN)

    dz_a = out_proj_bwd("d_conv_out_in", dya, w_co)
    dog = out_proj_bwd("d_hgrn_out_in", dyb, w_ho)
    gw_co = out_proj_wgrad("gw_conv_out", z_a, dya)
    gw_ho = out_proj_wgrad("gw_hgrn_out", og, dyb)
    conv_w_late = _after_tokens(conv_w, [put_g("mix", [gw_co, gw_ho, gw_o])])
    dab, dac, dax, dconv_w = _conv_bwd(proj, conv_w_late, dz_a, s, dc)
    lb_param_late = _after_tokens(lb_param, [sent("mix", dab)])
    dq, dfl, dvi, dgo, dlb, dgn = _hgrn_bwd(proj, lb_param_late, gnorm, o_saved, states, dog, s, dc, tb)
    dproj = jnp.concatenate([dab, dac, dax, dq, dfl, dvi, dgo, dga, dgb], axis=1)
    gw_in = _matmul("gw_proj", h, dproj, (nsh, d, win_sh), bf16, (nsh, d // 512, 1), pl.BlockSpec((s, 512), lambda j, i, k: (0, i)),
                    pl.BlockSpec((s, win_sh), lambda j, i, k: (0, j)), pl.BlockSpec((None, 512, win_sh), lambda j, i, k: (j, i, 0)), TN)
    dh = _matmul("d_proj_in", dproj, w_in, (s, d), f32, (1, s // tm2, nsh), pl.BlockSpec((tm2, win_sh), lambda j, i, k: (i, k)),
                 pl.BlockSpec((None, d, win_sh), lambda j, i, k: (k, 0, 0)), pl.BlockSpec((tm2, d), lambda j, i, k: (i, 0)), NT,
                 after=put_g("in", [gw_in]))

    def b12(rows, fulls, outs, accs):
        dhv, xv, dx1v = [r[...] for r in rows]
        sc, g = fulls[0][...], fulls[1][...]
        r, xh = _rms(xv)
        accs[0][...] += _colsum(dhv)
        accs[1][...] += _colsum(dhv * (xh * g))
        dn = dhv * (1.0 + sc)
        accs[2][...] += _colsum(dn * xh)
        dxh = dn * g
        outs[0][...] = dx1v + r * (dxh - xh * jnp.mean(dxh * xh, axis=-1, keepdims=True))

    dx, dsh_m, dsc_m, dg_mix = _rowwise("d_prenorm_mix", b12, s, tm, [(dh, d, 0), (x, d, 0), (dx1, d, 0)], [sc_m, g_mix],
                                        [(d, f32)], [(1, d)] * 3)
    dmod = [dsh_m, dsc_m, dgt_m, dsh_f, dsc_f, dgt_f]
    small = dict(loss=loss_acc, g_mix=dg_mix, g_ffn=dg_ffn, g_fin=dg_fin, lb=dlb, gnorm=dgn, conv_w=dconv_w)
    return dx, dmod, small


def _ada_fwd(c_all, w_sh, b_sh):
    def body(c_ref, w_ref, b_ref, o_ref):
        cv = c_ref[...]
        ca = (cv * _sigmoid(cv)).astype(bf16)
        o_ref[...] = jnp.dot(ca, w_ref[...].astype(bf16), preferred_element_type=f32) + b_ref[...]

    return pl.pallas_call(body, name="ada_fwd", out_shape=SDS((c_all.shape[0], w_sh.shape[1]), f32),
                          compiler_params=pltpu.CompilerParams(vmem_limit_bytes=V7X_VMEM_LIMIT))(c_all, w_sh, b_sh)


def _ada_wgrad(c_all, dmod_sh):
    def body(c_ref, d_ref, o_ref):
        cv = c_ref[...]
        ca = (cv * _sigmoid(cv)).astype(bf16)
        o_ref[...] = lax.dot_general(ca, d_ref[...].astype(bf16), (TN, ((), ())), preferred_element_type=f32)

    return pl.pallas_call(body, name="ada_wgrad", out_shape=SDS((c_all.shape[1], dmod_sh.shape[1]), f32),
                          compiler_params=pltpu.CompilerParams(vmem_limit_bytes=V7X_VMEM_LIMIT))(c_all, dmod_sh)


def _lb_grad(lb_param, dlb):
    def body(p_ref, d_ref, o_ref):
        p = p_ref[...]
        lb = _sigmoid(p[0:1, :] - p[1:2, :])
        gl = d_ref[...] * lb * (1.0 - lb)
        o_ref[0:1, :] = gl
        o_ref[1:2, :] = -gl

    return pl.pallas_call(body, name="lb_grad", out_shape=SDS(lb_param.shape, f32))(lb_param, dlb)


def _sum_small(gathered):
    def body(g_ref, o_ref):
        acc = g_ref[0]
        for dd in range(1, NDEV):
            acc = acc + g_ref[dd]
        o_ref[...] = acc

    return pl.pallas_call(body, name="sum_small", out_shape=SDS(gathered.shape[1:], f32))(gathered)


def kernel(x, c, w_ada, b_ada, norm_mix_g, w_in, conv_w, lb_param, gnorm_g, w_conv_out, w_hgrn_out, w_o, norm_ffn_g, w_ffn_gate, w_ffn_up, w_ffn_down, norm_final_g, loss_target, m_w_ada, m_b_ada, m_norm_mix_g, m_w_in, m_conv_w, m_lb_param, m_gnorm_g, m_w_conv_out, m_w_hgrn_out, m_w_o, m_norm_ffn_g, m_w_ffn_gate, m_w_ffn_up, m_w_ffn_down, m_norm_final_g, v_w_ada, v_b_ada, v_norm_mix_g, v_w_in, v_conv_w, v_lb_param, v_gnorm_g, v_w_conv_out, v_w_hgrn_out, v_w_o, v_norm_ffn_g, v_w_ffn_gate, v_w_ffn_up, v_w_ffn_down, v_norm_final_g):
    assert lb_param.shape[0] == 2 and w_ada.shape[0] == 1
    s, d = x.shape[1], x.shape[2]
    me = 4 * lax.axis_index("x") + 2 * lax.axis_index("y") + lax.axis_index("c")
    ada_cols = w_ada.shape[2]

    me1 = me.astype(jnp.int32).reshape(1)
    placed_in = [_place_shard("place_in0", me1, w_in[0])]
    c_all, cw_all = _all_gather("gather_cond", [c, conv_w[0]], after=placed_in)
    c_all = c_all.reshape(NDEV, d)
    conv_w_full = jnp.transpose(cw_all, (1, 0, 2)).reshape(conv_w.shape[1], -1)
    b_sh = lax.dynamic_slice_in_dim(b_ada, me * ada_cols, ada_cols, axis=1)
    mod_cols = _ada_fwd(c_all, w_ada[0], b_sh)
    mod_all, = _all_gather("gather_mod", [mod_cols])
    mod = lax.dynamic_index_in_dim(mod_all, me, axis=1, keepdims=False).reshape(N_MOD, 1, d)

    shard_groups = {"mix": [w_conv_out[0], w_hgrn_out[0], w_o[0]], "ffn": [w_ffn_gate[0].T, w_ffn_up[0].T, w_ffn_down[0]]}
    own_slot = lambda frm, to: _flat(frm)
    gather_plan = lambda n: [(a, j, own_slot, own_slot) for a in range(n) for j in (1,) + ICI_RELATIONS]
    flat = lambda a: a.reshape(a.shape[0] * a.shape[1], a.shape[2])
    to8 = lambda a: a.reshape(NDEV, a.shape[0] // NDEV, a.shape[1])
    sib_plan = [(0, 1, own_slot, own_slot)]
    ici_plan = [(0, j, own_slot, own_slot) for j in ICI_RELATIONS]
    ss_a, rs_a, _, lands, tok_a = _push_start("gather_start_in_sib", [], placed_in, sib_plan, after=mod_all)
    ss_b, rs_b, _, lands_in, tok_b = _push_start("gather_start_in", [], lands, ici_plan, after=tok_a)
    gathering, tokens = {}, [tok_a, tok_b]
    for grp, sh in shard_groups.items():
        lands = [_place_shard(f"place_{grp}{i}", me1, a, after=tokens[-1]) for i, a in enumerate(sh)]
        ss, rs, _, lands, tok = _push_start("gather_start_" + grp, [], lands, gather_plan(len(lands)), after=tokens[-1])
        gathering[grp] = (ss, rs, lands)
        tokens.append(tok)

    pos = (lax.axis_index("x"), lax.axis_index("y"), lax.axis_index("c"))
    ids = lambda frm, rels: jnp.stack([_flat(_peer(frm, j)) for j in rels]).astype(jnp.int32)

    def project(h, tm):
        _, lands = _push_wait("gather_wait_in_sib", ss_a, rs_a, [], lands_in, sib_plan, h)
        proj = _proj_part("proj_local", h, lands[0], ids(pos, (0, 1)), None, tm)
        _, lands = _push_wait("gather_wait_in", ss_b, rs_b, [], lands, ici_plan, proj)
        ss, rs, _, lands, tok = _push_start("gather_fwd_in", [], lands, fwd_plan(1))
        proj = _proj_part("proj_ici", h, lands[0], ids(pos, ICI_RELATIONS), proj, tm, after=tok)
        _, lands = _push_wait("gather_fwd_wait_in", ss, rs, [], lands, fwd_plan(1), proj)
        proj = _proj_part("proj_fwd", h, lands[0], ids(_peer(pos, 1), ICI_RELATIONS), proj, tm)
        return proj, lands[0]

    fwd_slot = lambda i: (lambda frm, to: _flat(_peer(frm, ICI_RELATIONS[i])))
    fwd_plan = lambda n: [(a, 1, fwd_slot(i), fwd_slot(i)) for a in range(n) for i in range(len(ICI_RELATIONS))]
    forwarding = {}

    def prefetch(grp, after):
        ss, rs, lands = gathering[grp]
        _, lands = _push_wait("gather_wait_" + grp, ss, rs, [], lands, gather_plan(len(lands)), after)
        ss, rs, _, lands, tok = _push_start("gather_fwd_" + grp, [], lands, fwd_plan(len(lands)))
        forwarding[grp] = (ss, rs, lands)
        return tok

    def get_w(grp, after):
        ss, rs, lands = forwarding[grp]
        _, full = _push_wait("gather_fwd_wait_" + grp, ss, rs, [], lands, fwd_plan(len(lands)), after)
        return [f if i < 2 and grp == "mix" else flat(f) for i, f in enumerate(full)]

    core = lax.axis_index("c").astype(jnp.int32).reshape(1)
    chip = (2 * lax.axis_index("x") + lax.axis_index("y")).astype(jnp.int32).reshape(1)
    scatter_plan = lambda n: [(a, j, lambda frm, to: _chip(to), lambda frm, to: _chip(frm)) for a in range(n) for j in ICI_RELATIONS]
    scattering = {}

    swap_plan = lambda n: [(a, 1, functools.partial(lambda frm, to, q: 2 * q + to[2], q=q), functools.partial(lambda frm, to, q: q, q=q))
                           for a in range(n) for q in range(NDEV // 2)]
    swapping = {}

    def start_ici(grp, g8, recv):
        pairs = [_pair_sum(f"pair_sum_{grp}{i}", core, g, r, _row_tile(g.shape[1], 1024)) for i, (g, r) in enumerate(zip(g8, recv))]
        lands = [lax.empty(p.shape, p.dtype) for p in pairs]
        ss, rs, srcs, lands, tok = _push_start("scatter_start_" + grp, pairs, lands, scatter_plan(len(pairs)))
        scattering[grp] = (ss, rs, srcs, lands)
        return tok

    def put_g(grp, grads):
        g8 = [g if g.ndim == 3 else to8(g) for g in grads]
        if grp == "in":
            return start_ici(grp, g8, _sibling_swap("scatter_pair_" + grp, g8))
        lands = [lax.empty((NDEV // 2,) + g.shape[1:], g.dtype) for g in g8]
        ss, rs, srcs, lands, tok = _push_start("scatter_swap_" + grp, g8, lands, swap_plan(len(g8)))
        swapping[grp] = (ss, rs, srcs, lands)
        return tok

    def sent(grp, after):
        ss, rs, srcs, lands = swapping[grp]
        g8, recv = _push_wait("scatter_swapped_" + grp, ss, rs, srcs, lands, swap_plan(len(srcs)), after)
        return start_ici(grp, g8, recv)

    def reduced(grp, after, names):
        ss, rs, srcs, lands = scattering[grp]
        srcs, lands = _push_wait("scatter_wait_" + grp, ss, rs, srcs, lands, scatter_plan(len(srcs)), after)
        for i, (p, r, nm) in enumerate(zip(srcs, lands, names)):
            tr = nm in ("w_ffn_gate", "w_ffn_up")
            wmv = tuple(a[0].T if tr else a[0] for a in weights[nm])
            out = _chip_sum_adamw(f"chip_sum_{grp}{i}", chip, p, r, _row_tile(p.shape[1]), wmv)
            res[nm] = [(a.T if tr else a).reshape(weights[nm][0].shape) for a in out]

    dx, dmod, small = _local_step(x[0], loss_target[0], mod, _after_tokens(norm_mix_g, tokens), norm_ffn_g,
                                  norm_final_g.reshape(1, d), lb_param, gnorm_g, conv_w_full, project, get_w, prefetch, put_g, sent)

    pieces = [*dmod, small["g_mix"], small["g_ffn"], small["g_fin"], small["lb"], small["gnorm"], small["loss"],
              small["conv_w"].reshape(1, -1)]
    widths = [p.shape[1] for p in pieces]
    offs = np.concatenate([[0], np.cumsum(widths)])
    packed = jnp.concatenate(pieces, axis=1)
    gathered, = _all_gather("gather_small", [packed])
    summed = _sum_small(gathered)
    part = lambda i: summed[:, offs[i]:offs[i + 1]]
    g_b_ada = summed[:, :N_MOD * d]
    g_norm_mix, g_norm_ffn, g_norm_fin, g_lb_row, g_gnorm, loss_vec, g_convw_flat = [part(i) for i in range(N_MOD, N_MOD + 7)]
    loss = loss_vec[0, 0]
    dmod_all = gathered[:, 0, :N_MOD * d]
    g_w_ada = _ada_wgrad(c_all, lax.dynamic_slice_in_dim(dmod_all, me * ada_cols, ada_cols, axis=1))
    g_lb = _lb_grad(lb_param, g_lb_row)
    cw_cols = conv_w.shape[2]
    g_conv_w = lax.dynamic_slice_in_dim(g_convw_flat.reshape(conv_w.shape[1], -1), me * cw_cols, cw_cols, axis=1)

    grads = dict(w_ada=g_w_ada, b_ada=g_b_ada, norm_mix_g=g_norm_mix, conv_w=g_conv_w, lb_param=g_lb, gnorm_g=g_gnorm,
                 norm_ffn_g=g_norm_ffn, norm_final_g=g_norm_fin)
    weights = dict(w_ada=(w_ada, m_w_ada, v_w_ada), b_ada=(b_ada, m_b_ada, v_b_ada), norm_mix_g=(norm_mix_g, m_norm_mix_g, v_norm_mix_g),
                   w_in=(w_in, m_w_in, v_w_in), conv_w=(conv_w, m_conv_w, v_conv_w), lb_param=(lb_param, m_lb_param, v_lb_param),
                   gnorm_g=(gnorm_g, m_gnorm_g, v_gnorm_g), w_conv_out=(w_conv_out, m_w_conv_out, v_w_conv_out),
                   w_hgrn_out=(w_hgrn_out, m_w_hgrn_out, v_w_hgrn_out), w_o=(w_o, m_w_o, v_w_o),
                   norm_ffn_g=(norm_ffn_g, m_norm_ffn_g, v_norm_ffn_g), w_ffn_gate=(w_ffn_gate, m_w_ffn_gate, v_w_ffn_gate),
                   w_ffn_up=(w_ffn_up, m_w_ffn_up, v_w_ffn_up), w_ffn_down=(w_ffn_down, m_w_ffn_down, v_w_ffn_down),
                   norm_final_g=(norm_final_g, m_norm_final_g, v_norm_final_g))
    res = {}

    def update(nm):
        w, m, v = weights[nm]
        shape2 = (w.shape[-2], w.shape[-1]) if w.ndim >= 2 else (1, w.shape[0])
        g2 = grads[nm].reshape(shape2)
        dl, mn, vn = _adamw("adamw_" + nm, g2, w.reshape(shape2), m.reshape(shape2), v.reshape(shape2))
        res[nm] = [a.reshape(w.shape) for a in (g2, dl, mn, vn)]

    for nm in list(grads):
        update(nm)
    reduced("ffn", res["w_ada"][1], ("w_ffn_gate", "w_ffn_up", "w_ffn_down"))
    reduced("mix", res["w_ffn_down"][1], ("w_conv_out", "w_hgrn_out", "w_o"))
    reduced("in", res["w_o"][1], ("w_in",))
    outs = [[res[nm][i] for nm in weights] for i in range(4)]
    return (loss, dx.reshape(x.shape), *outs[0], *outs[1], *outs[2], *outs[3])
```

```python
import functools

import numpy as np
import jax
import jax.numpy as jnp
from jax import lax
from jax.experimental import pallas as pl
from jax.experimental.pallas import tpu as pltpu

f32, bf16 = jnp.float32, jnp.bfloat16
SDS = jax.ShapeDtypeStruct

EPS = 1e-6
HEADS, DK, CHUNK = 8, 128, 64
HEADS_PER_STEP = 4
N_MOD = 6
NDEV = 8
ADAM_LR, ADAM_B1, ADAM_B2, ADAM_EPS, ADAM_WD, ADAM_STEP = 0.001, 0.9, 0.999, 1e-08, 0.01, 10
LEVELS = (32, 16, 8, 4, 2, 1)
V7X_VMEM_LIMIT = 56 * 1024 * 1024
HBM = pl.BlockSpec(memory_space=pltpu.HBM)
MESH = pl.DeviceIdType.MESH


def _params(sem):
    return pltpu.CompilerParams(dimension_semantics=sem, vmem_limit_bytes=V7X_VMEM_LIMIT)


def _sigmoid(x):
    return jax.nn.sigmoid(x)


def _dsilu(x, s):
    return s * (1.0 + x * (1.0 - s))


def _mesh_pos():
    x, y, c = lax.axis_index("x"), lax.axis_index("y"), lax.axis_index("c")
    return x, y, c


def _peer(pos, j):
    x, y, c = pos
    return (1 - x if j & 4 else x, 1 - y if j & 2 else y, 1 - c if j & 1 else c)


def _flat(pos):
    return 4 * pos[0] + 2 * pos[1] + pos[2]


def _all_gather(name, arrs, after=()):
    n, ne = len(arrs), len(after)
    out_shapes = [SDS((NDEV,) + a.shape, a.dtype) for a in arrs]

    def body(*refs):
        ins, outs = refs[:n], refs[n + ne:2 * n + ne]
        send_sems, recv_sems, local_sems = refs[2 * n + ne:]
        pos = _mesh_pos()
        me = _flat(pos)

        def copy(a, j, frm, to_pos):
            k = a * (NDEV - 1) + j - 1
            return pltpu.make_async_remote_copy(src_ref=ins[a], dst_ref=outs[a].at[frm], send_sem=send_sems.at[k],
                                                recv_sem=recv_sems.at[k], device_id=to_pos, device_id_type=MESH)

        local = [pltpu.make_async_copy(ins[a], outs[a].at[me], local_sems.at[a]) for a in range(n)]
        for cp in local:
            cp.start()
        sends = [copy(a, j, me, _peer(pos, j)) for j in range(1, NDEV) for a in range(n)]
        for cp in sends:
            cp.start()
        for j in range(1, NDEV):
            for a in range(n):
                copy(a, j, _flat(_peer(pos, j)), pos).wait_recv()
        for cp in sends:
            cp.wait_send()
        for cp in local:
            cp.wait()

    return pl.pallas_call(
        body, name=name, out_shape=out_shapes, in_specs=[HBM] * n + [ANY] * ne, out_specs=[HBM] * n,
        scratch_shapes=[pltpu.SemaphoreType.DMA((n * (NDEV - 1),)), pltpu.SemaphoreType.DMA((n * (NDEV - 1),)),
                        pltpu.SemaphoreType.DMA((n,))],
    )(*arrs, *after)


SEM = pl.BlockSpec(memory_space=pltpu.SEMAPHORE)
ANY = pl.BlockSpec(memory_space=pl.ANY)
EFFECT = pltpu.SideEffectType.DATAFLOW_SIDE_EFFECTING
ICI_RELATIONS = (2, 4, 6)


def _chip(pos):
    return 2 * pos[0] + pos[1]


def _hbm(a):
    return pltpu.with_memory_space_constraint(a, pltpu.HBM)


def _plan_copy(plan_entry, k, pos, frm, to, src_refs, land_refs, send_sems, recv_sems):
    a, _, src_slot, dst_slot = plan_entry
    s = src_refs[a] if src_slot is None else src_refs[a].at[src_slot(frm, to)]
    return pltpu.make_async_remote_copy(src_ref=s, dst_ref=land_refs[a].at[dst_slot(frm, to)], send_sem=send_sems.at[k],
                                        recv_sem=recv_sems.at[k], device_id=to, device_id_type=MESH)


def _push_start(name, srcs, lands, plan, after=None):
    ns, nb, nk = len(srcs), len(srcs) + len(lands), len(plan)
    extra = [] if after is None else [after]

    def body(*refs):
        land_refs = refs[ns:nb]
        src_refs = refs[:ns] if ns else land_refs
        send_sems, recv_sems = refs[nb + len(extra)], refs[nb + len(extra) + 1]
        pos = _mesh_pos()
        for k, e in enumerate(plan):
            _plan_copy(e, k, pos, pos, _peer(pos, e[1]), src_refs, land_refs, send_sems, recv_sems).start()
        refs[-1][...] = jnp.zeros_like(refs[-1])

    outs = pl.pallas_call(
        body, name=name,
        out_shape=[pltpu.SemaphoreType.DMA((nk,)), pltpu.SemaphoreType.DMA((nk,))] + [pltpu.HBM(a.shape, a.dtype) for a in srcs + lands]
        + [SDS((8, 128), f32)],
        in_specs=[HBM] * nb + [ANY] * len(extra), out_specs=[SEM, SEM] + [HBM] * nb + [pl.BlockSpec(memory_space=pltpu.VMEM)],
        input_output_aliases={i: 2 + i for i in range(nb)},
        compiler_params=pltpu.CompilerParams(has_side_effects=EFFECT),
    )(*[_hbm(a) for a in srcs + lands], *extra)
    return outs[0], outs[1], list(outs[2:2 + ns]), list(outs[2 + ns:2 + nb]), outs[-1]


def _push_wait(name, send_sems, recv_sems, srcs, lands, plan, after):
    ns, nb = len(srcs), len(srcs) + len(lands)

    def body(*refs):
        land_refs = refs[ns:nb]
        src_refs = refs[:ns] if ns else land_refs
        ssem, rsem = refs[nb], refs[nb + 1]
        pos = _mesh_pos()
        for k, e in enumerate(plan):
            peer = _peer(pos, e[1])
            _plan_copy(e, k, pos, pos, peer, src_refs, land_refs, ssem, rsem).wait_send()
            _plan_copy(e, k, pos, peer, pos, src_refs, land_refs, ssem, rsem).wait_recv()

    outs = pl.pallas_call(
        body, name=name, out_shape=[pltpu.HBM(a.shape, a.dtype) for a in srcs + lands],
        in_specs=[HBM] * nb + [SEM, SEM, ANY], out_specs=[HBM] * nb,
        input_output_aliases={i: i for i in range(nb)},
        compiler_params=pltpu.CompilerParams(has_side_effects=EFFECT),
    )(*srcs, *lands, send_sems, recv_sems, after)
    return list(outs[:ns]), list(outs[ns:])


def _after_tokens(small, tokens):
    for t in tokens:
        if t is not None:
            small = small + t[0:1, 0:1].reshape((1,) * small.ndim)
    return small


def _place_shard(name, me, shard, after=None):
    r, c = shard.shape
    tr = _row_tile(r)
    extra = [] if after is None else [after]

    def body(me_ref, s_ref, *rest):
        rest[-1][...] = s_ref[...].astype(bf16)

    return pl.pallas_call(
        body, name=name, out_shape=SDS((NDEV, r, c), bf16),
        grid_spec=pltpu.PrefetchScalarGridSpec(
            num_scalar_prefetch=1, grid=(r // tr,), in_specs=[pl.BlockSpec((tr, c), lambda i, me_ref: (i, 0))] + [ANY] * len(extra),
            out_specs=pl.BlockSpec((None, tr, c), lambda i, me_ref: (me_ref[0], i, 0))),
        compiler_params=_params(("parallel",)),
    )(me, shard, *extra)


def _row_tile(r, most=256):
    return max(t for t in range(16, most + 1, 16) if r % t == 0)


def _sibling_swap(name, grads):
    n = len(grads)
    nchip = NDEV // 2

    def body(*refs):
        g_refs, out_refs = refs[:n], refs[n:2 * n]
        send_sems, recv_sems = refs[2 * n:]
        pos = _mesh_pos()
        sib = _peer(pos, 1)
        sends = []
        for a in range(n):
            for q in range(nchip):
                k = a * nchip + q
                sends.append(pltpu.make_async_remote_copy(src_ref=g_refs[a].at[2 * q + sib[2]], dst_ref=out_refs[a].at[q],
                                                          send_sem=send_sems.at[k], recv_sem=recv_sems.at[k], device_id=sib,
                                                          device_id_type=MESH))
        for cp in sends:
            cp.start()
        for cp in sends:
            cp.wait()

    return pl.pallas_call(
        body, name=name, out_shape=[SDS((nchip,) + a.shape[1:], a.dtype) for a in grads], in_specs=[HBM] * n, out_specs=[HBM] * n,
        scratch_shapes=[pltpu.SemaphoreType.DMA((n * nchip,)), pltpu.SemaphoreType.DMA((n * nchip,))],
    )(*grads)


def _pair_sum(name, core, grad, recv, tr):
    _, r, c = grad.shape
    nchip = NDEV // 2

    def body(core_ref, g_ref, r_ref, o_ref):
        o_ref[...] = (g_ref[...].astype(f32) + r_ref[...].astype(f32)).astype(o_ref.dtype)

    return pl.pallas_call(
        body, name=name, out_shape=SDS((nchip, r, c), grad.dtype),
        grid_spec=pltpu.PrefetchScalarGridSpec(
            num_scalar_prefetch=1, grid=(nchip, r // tr),
            in_specs=[pl.BlockSpec((None, tr, c), lambda q, i, core_ref: (2 * q + core_ref[0], i, 0)),
                      pl.BlockSpec((None, tr, c), lambda q, i, core_ref: (q, i, 0))],
            out_specs=pl.BlockSpec((None, tr, c), lambda q, i, core_ref: (q, i, 0))),
        compiler_params=_params(("parallel", "parallel")),
    )(core, grad, recv)


def _adamw_math(g, w, m, v):
    mn = ADAM_B1 * m + (1.0 - ADAM_B1) * g
    vn = ADAM_B2 * v + (1.0 - ADAM_B2) * jnp.square(g)
    m_hat = mn / (1.0 - ADAM_B1 ** ADAM_STEP)
    v_hat = vn / (1.0 - ADAM_B2 ** ADAM_STEP)
    return -ADAM_LR * (m_hat / (jnp.sqrt(v_hat) + ADAM_EPS) + ADAM_WD * w), mn, vn


def _chip_sum_adamw(name, chip, pairs, recv, tr, wmv):
    nchip, r, c = pairs.shape

    def body(chip_ref, p_ref, r_ref, w_ref, m_ref, v_ref, g_ref, d_ref, mo_ref, vo_ref):
        mine = chip_ref[0]
        own = p_ref[...].astype(f32)
        acc = jnp.zeros((tr, c), f32)
        for q in range(nchip):
            acc = acc + jnp.where(mine == q, own, r_ref[q].astype(f32))
        g_ref[...] = acc
        d_ref[...], mo_ref[...], vo_ref[...] = _adamw_math(acc, w_ref[...], m_ref[...], v_ref[...])

    blk = pl.BlockSpec((tr, c), lambda i, chip_ref: (i, 0))
    return pl.pallas_call(
        body, name=name, out_shape=[SDS((r, c), f32)] * 4,
        grid_spec=pltpu.PrefetchScalarGridSpec(
            num_scalar_prefetch=1, grid=(r // tr,),
            in_specs=[pl.BlockSpec((None, tr, c), lambda i, chip_ref: (chip_ref[0], i, 0)),
                      pl.BlockSpec((nchip, tr, c), lambda i, chip_ref: (0, i, 0))] + [blk] * 3,
            out_specs=[blk] * 4),
        compiler_params=_params(("parallel",)),
    )(chip, pairs, recv, *wmv)


def _matmul(name, a, b, out_shape, out_dtype, grid, a_spec, b_spec, o_spec, dims, after=None):
    ksteps = grid[2]
    acc_shape = tuple(d for d in o_spec.block_shape if d is not None)
    extra = [] if after is None else [after]

    def body(a_ref, b_ref, *rest):
        o_ref, acc = rest[len(extra)], rest[len(extra) + 1:]
        prod = lax.dot_general(a_ref[...], b_ref[...], (dims, ((), ())), preferred_element_type=f32)
        if ksteps == 1:
            o_ref[...] = prod.astype(o_ref.dtype)
        else:
            k = pl.program_id(2)

            @pl.when(k == 0)
            def _():
                acc[0][...] = prod

            @pl.when(k > 0)
            def _():
                acc[0][...] += prod

            @pl.when(k == ksteps - 1)
            def _():
                o_ref[...] = acc[0][...].astype(o_ref.dtype)

    return pl.pallas_call(
        body, name=name, grid=grid, in_specs=[a_spec, b_spec] + [ANY] * len(extra), out_specs=o_spec,
        out_shape=SDS(out_shape, out_dtype), scratch_shapes=[] if ksteps == 1 else [pltpu.VMEM(acc_shape, f32)],
        compiler_params=_params(("parallel", "parallel", "arbitrary")),
    )(a, b, *extra)


NN, NT, TN = ((1,), (0,)), ((1,), (1,)), ((0,), (0,))


def _proj_part(name, h, w_sh, ids, prev, tm, after=None):
    s, d = h.shape
    nsh, _, n = w_sh.shape
    extra = ([] if prev is None else [prev]) + ([] if after is None else [after])

    def body(ids_ref, a_ref, b_ref, *rest):
        rest[-1][...] = jnp.dot(a_ref[...], b_ref[...], preferred_element_type=f32)

    return pl.pallas_call(
        body, name=name, out_shape=SDS((s, nsh * n), f32),
        grid_spec=pltpu.PrefetchScalarGridSpec(
            num_scalar_prefetch=1, grid=(ids.shape[0], s // tm),
            in_specs=[pl.BlockSpec((tm, d), lambda j, i, ids_ref: (i, 0)),
                      pl.BlockSpec((None, d, n), lambda j, i, ids_ref: (ids_ref[j], 0, 0))] + [ANY] * len(extra),
            out_specs=pl.BlockSpec((tm, n), lambda j, i, ids_ref: (i, ids_ref[j]))),
        input_output_aliases={} if prev is None else {3: 0},
        compiler_params=_params(("parallel", "parallel")),
    )(ids, h, w_sh, *extra)


def _rowwise(name, fn, n_rows, tm, rows, fulls, row_outs, acc_outs=(), ncol=1):
    assert ncol == 1 or not acc_outs
    nr, nf, no, na = len(rows), len(fulls), len(row_outs), len(acc_outs)
    in_specs = [pl.BlockSpec((tm, w), functools.partial(lambda i, j, cb: (i, cb + j), cb=cb)) for (_, w, cb) in rows]
    in_specs += [pl.BlockSpec(a.shape, functools.partial(lambda i, j, nd: (0,) * nd, nd=a.ndim)) for a in fulls]
    out_shape = [SDS((n_rows, w), dt) for (w, dt) in row_outs] + [SDS(s, f32) for s in acc_outs]
    out_specs = [pl.BlockSpec((tm, w // ncol), lambda i, j: (i, j)) for (w, _) in row_outs]
    out_specs += [pl.BlockSpec(s, functools.partial(lambda i, j, nd: (0,) * nd, nd=len(s))) for s in acc_outs]

    def body(*refs):
        if na:
            @pl.when(pl.program_id(0) == 0)
            def _():
                for r in refs[nr + nf + no:]:
                    r[...] = jnp.zeros(r.shape, r.dtype)
        fn(refs[:nr], refs[nr:nr + nf], refs[nr + nf:nr + nf + no], refs[nr + nf + no:])

    return pl.pallas_call(
        body, name=name, grid=(n_rows // tm, ncol), in_specs=in_specs, out_specs=out_specs, out_shape=out_shape,
        compiler_params=_params(("arbitrary" if na else "parallel", "arbitrary" if na else "parallel")),
    )(*[r[0] for r in rows], *fulls)


def _rms(x):
    r = lax.rsqrt(jnp.mean(x * x, axis=-1, keepdims=True) + EPS)
    return r, x * r


def _colsum(v):
    return jnp.sum(v, axis=0, keepdims=True)


def _shift_down(u, row, k):
    return jnp.where(row >= k, pltpu.roll(u, k, 0), 0.0)


def _shift_up(u, row, k):
    n = u.shape[0]
    return jnp.where(row < n - k, pltpu.roll(u, n - k, 0), 0.0)


def _conv_fwd(proj, conv_w, s, dc):
    nb = dc // 128

    def body(ab_ref, ac_ref, ax_ref, w_ref, z_ref):
        u = ac_ref[...] * ax_ref[...]
        row = lax.broadcasted_iota(jnp.int32, u.shape, 0)
        w = w_ref[...]
        cv = w[0:1] * _shift_down(u, row, 2) + w[1:2] * _shift_down(u, row, 1) + w[2:3] * u
        z_ref[...] = (ab_ref[...] * cv).astype(z_ref.dtype)

    col = lambda off: pl.BlockSpec((s, 128), functools.partial(lambda j, off: (0, off + j), off=off))
    return pl.pallas_call(
        body, name="conv_fwd", grid=(nb,), in_specs=[col(0), col(nb), col(2 * nb), pl.BlockSpec((3, 128), lambda j: (0, j))],
        out_specs=pl.BlockSpec((s, 128), lambda j: (0, j)), out_shape=SDS((s, dc), bf16), compiler_params=_params(("parallel",)),
    )(proj, proj, proj, conv_w)


def _conv_bwd(proj, conv_w, dz, s, dc):
    nb = dc // 128

    def body(ab_ref, ac_ref, ax_ref, w_ref, dz_ref, dab_ref, dac_ref, dax_ref, dw_ref):
        ab, ac, ax, dzv = ab_ref[...], ac_ref[...], ax_ref[...], dz_ref[...].astype(f32)
        u = ac * ax
        row = lax.broadcasted_iota(jnp.int32, u.shape, 0)
        w = w_ref[...]
        u1, u2 = _shift_down(u, row, 1), _shift_down(u, row, 2)
        cv = w[0:1] * u2 + w[1:2] * u1 + w[2:3] * u
        dcv = dzv * ab
        dab_ref[...] = (dzv * cv).astype(dab_ref.dtype)
        du = w[2:3] * dcv + w[1:2] * _shift_up(dcv, row, 1) + w[0:1] * _shift_up(dcv, row, 2)
        dac_ref[...] = (du * ax).astype(dac_ref.dtype)
        dax_ref[...] = (du * ac).astype(dax_ref.dtype)
        dw_ref[0:1, :] = _colsum(dcv * u2)
        dw_ref[1:2, :] = _colsum(dcv * u1)
        dw_ref[2:3, :] = _colsum(dcv * u)

    col = lambda off: pl.BlockSpec((s, 128), functools.partial(lambda j, off: (0, off + j), off=off))
    blk = pl.BlockSpec((s, 128), lambda j: (0, j))
    return pl.pallas_call(
        body, name="conv_bwd", grid=(nb,),
        in_specs=[col(0), col(nb), col(2 * nb), pl.BlockSpec((3, 128), lambda j: (0, j)), blk],
        out_specs=[blk, blk, blk, pl.BlockSpec((3, 128), lambda j: (0, j))],
        out_shape=[SDS((s, dc), bf16)] * 3 + [SDS((3, dc), f32)], compiler_params=_params(("parallel",)),
    )(proj, proj, proj, conv_w, dz)


def _level_masks():
    t = np.arange(CHUNK)[:, None]
    s = np.arange(CHUNK)[None, :]
    m = np.stack([((t & h) != 0) & ((s & h) == 0) & (t // (2 * h) == s // (2 * h)) for h in LEVELS]).astype(np.float32)
    return jnp.asarray(m), jnp.asarray(m.transpose(0, 2, 1))


def _cumsum_rows(x, row):
    for sh in (1, 2, 4, 8, 16, 32):
        x = x + jnp.where(row >= sh, pltpu.roll(x, sh, 0), 0.0)
    return x


def _rev_cumsum_rows(x, row):
    n = x.shape[0]
    for sh in (1, 2, 4, 8, 16, 32):
        x = x + jnp.where(row < n - sh, pltpu.roll(x, n - sh, 0), 0.0)
    return x


def _chunk_terms(qp, fl, lb):
    row = lax.broadcasted_iota(jnp.int32, qp.shape, 0)
    sig = _sigmoid(fl)
    f = lb + (1.0 - lb) * sig
    k = 1.0 - f
    sq = _sigmoid(qp)
    qh = qp * sq
    b = _cumsum_rows(jnp.log(f), row)
    sub = lax.broadcasted_iota(jnp.int32, (CHUNK // 8, 8, DK), 1)
    b8 = b.reshape(CHUNK // 8, 8, DK)
    us, exs, ups = [], [], []
    for m in LEVELS:
        sb = 2 * m
        if sb >= 8:
            b3 = b.reshape(CHUNK // sb, sb, DK)
            bref = jnp.broadcast_to(b3[:, m - 1:m, :], b3.shape).reshape(CHUNK, DK)
        else:
            bref8 = None
            for j in range(8 // sb):
                cand = jnp.broadcast_to(b8[:, j * sb + m - 1:j * sb + m, :], b8.shape)
                bref8 = cand if bref8 is None else jnp.where(sub >= j * sb, cand, bref8)
            bref = bref8.reshape(CHUNK, DK)
        up = (row & m) != 0
        ex = jnp.exp(jnp.where(up, b - bref, bref - b))
        us.append((jnp.where(up, qh, k) * ex).astype(bf16))
        exs.append(ex)
        ups.append(up)
    blast = b[CHUNK - 1:CHUNK, :]
    eb, ebl = jnp.exp(b), jnp.exp(blast - b)
    return dict(sig=sig, f=f, k=k, sq=sq, qh=qh, u=jnp.stack(us), ex=exs, up=ups, eb=eb, ebl=ebl, qt=qh * eb, kt=k * ebl,
                el=jnp.exp(blast), row=row)


def _scores(t, mask):
    pl_ = jnp.einsum("ltk,lsk->lts", t["u"], t["u"], preferred_element_type=f32)
    p = jnp.sum(pl_ * mask, axis=0)
    r = lax.broadcasted_iota(jnp.int32, (CHUNK, CHUNK), 0)
    c = lax.broadcasted_iota(jnp.int32, (CHUNK, CHUNK), 1)
    diag = jnp.sum(t["qh"] * t["k"], axis=-1, keepdims=True)
    return p + jnp.where(r == c, diag, 0.0)


def _hgrn_fwd(proj, lb_param, gnorm, s, dv_total, tb):
    nchunk = tb // CHUNK
    masks, _ = _level_masks()
    q0, f0, v0, g0 = 3 * HEADS, 4 * HEADS, 5 * HEADS, 6 * HEADS

    def body(q_ref, f_ref, v_ref, g_ref, lb_ref, gn_ref, mask_ref, og_ref, o_ref, st_ref, state):
        @pl.when(pl.program_id(1) == 0)
        def _():
            state[...] = jnp.zeros_like(state)

        lbp = lb_ref[...]
        lb_all = _sigmoid(lbp[0:1, :] - lbp[1:2, :])
        mask = mask_ref[...]
        for i, hh in [(i, hh) for i in range(nchunk) for hh in range(HEADS_PER_STEP)]:
            rs, cs = pl.ds(i * CHUNK, CHUNK), pl.ds(hh * DK, DK)
            t = _chunk_terms(q_ref[rs, cs], f_ref[rs, cs], lb_all[:, hh * DK:(hh + 1) * DK])
            v = v_ref[rs, cs]
            vb = v.astype(bf16)
            st = state[hh]
            st_ref[i, hh] = st
            p = _scores(t, mask)
            o = jnp.dot(p.astype(bf16), vb, preferred_element_type=f32)
            o += lax.dot_general(t["qt"].astype(bf16), st.astype(bf16), (NT, ((), ())), preferred_element_type=f32)
            state[hh] = st * t["el"] + lax.dot_general(vb, t["kt"].astype(bf16), (TN, ((), ())), preferred_element_type=f32)
            o_ref[rs, cs] = o
            r, oh = _rms(o)
            g = g_ref[rs, cs]
            og_ref[rs, cs] = (oh * gn_ref[...] * (g * _sigmoid(g))).astype(og_ref.dtype)

    hp, wide = HEADS_PER_STEP, HEADS_PER_STEP * DK
    col = lambda off: pl.BlockSpec((tb, wide), functools.partial(lambda h, t, off: (t, off + h), off=off // hp))
    blk = pl.BlockSpec((tb, wide), lambda h, t: (t, h))
    return pl.pallas_call(
        body, name="hgrn_fwd", grid=(HEADS // hp, s // tb),
        in_specs=[col(q0), col(f0), col(v0), col(g0), pl.BlockSpec((2, wide), lambda h, t: (0, h)),
                  pl.BlockSpec((1, DK), lambda h, t: (0, 0)), pl.BlockSpec(masks.shape, lambda h, t: (0, 0, 0))],
        out_specs=[blk, blk, pl.BlockSpec((nchunk, hp, DK, DK), lambda h, t: (t, h, 0, 0))],
        out_shape=[SDS((s, dv_total), bf16), SDS((s, dv_total), f32), SDS((s // CHUNK, HEADS, DK, DK), f32)],
        scratch_shapes=[pltpu.VMEM((hp, DK, DK), f32)], compiler_params=_params(("parallel", "arbitrary")),
    )(proj, proj, proj, proj, lb_param, gnorm, masks)


def _hgrn_bwd(proj, lb_param, gnorm, o_saved, states, dog, s, dv_total, tb):
    nchunk = tb // CHUNK
    nt = s // tb
    nc_total = s // CHUNK
    masks, masks_t = _level_masks()
    q0, f0, v0, g0 = 3 * HEADS, 4 * HEADS, 5 * HEADS, 6 * HEADS

    def body(q_ref, f_ref, v_ref, g_ref, lb_ref, gn_ref, mask_ref, maskt_ref, o_ref, dog_ref, st_ref, stn_ref,
             dq_ref, df_ref, dv_ref, dg_ref, dlb_ref, dgn_ref, gstate):
        h_id, t_id = pl.program_id(0), pl.program_id(1)

        @pl.when(t_id == 0)
        def _():
            gstate[...] = jnp.zeros_like(gstate)
            dlb_ref[...] = jnp.zeros_like(dlb_ref)

        @pl.when((t_id == 0) & (h_id == 0))
        def _():
            dgn_ref[...] = jnp.zeros_like(dgn_ref)

        lbp = lb_ref[...]
        lb_all = _sigmoid(lbp[0:1, :] - lbp[1:2, :])
        mask, maskt = mask_ref[...], maskt_ref[...]
        gn = gn_ref[...]
        for i, hh in [(i, hh) for i in reversed(range(nchunk)) for hh in range(HEADS_PER_STEP)]:
            rs, cs = pl.ds(i * CHUNK, CHUNK), pl.ds(hh * DK, DK)
            lb = lb_all[:, hh * DK:(hh + 1) * DK]
            qp, fl, v, g = q_ref[rs, cs], f_ref[rs, cs], v_ref[rs, cs], g_ref[rs, cs]
            t = _chunk_terms(qp, fl, lb)
            vb = v.astype(bf16)
            st0 = st_ref[i, hh]
            st1 = st_ref[i + 1, hh] if i + 1 < nchunk else stn_ref[0, hh]
            gt = gstate[hh]
            o = o_ref[rs, cs]
            r, oh = _rms(o)
            sg = _sigmoid(g)
            dog_v = dog_ref[rs, cs].astype(f32)
            dg_ref[rs, cs] = (dog_v * (oh * gn) * _dsilu(g, sg)).astype(dg_ref.dtype)
            don = dog_v * (g * sg)
            dgn_ref[...] += _colsum(don * oh)
            doh = don * gn
            do = r * (doh - oh * jnp.mean(doh * oh, axis=-1, keepdims=True))
            dob = do.astype(bf16)
            d = lax.dot_general(dob, vb, (NT, ((), ())), preferred_element_type=f32)
            dt = lax.dot_general(vb, dob, (NT, ((), ())), preferred_element_type=f32)
            z = (mask * d[None] + maskt * dt[None]).astype(bf16)
            rr = jnp.einsum("lts,lsk->ltk", z, t["u"], preferred_element_type=f32)
            dq = jnp.zeros((CHUNK, DK), f32)
            dk = jnp.zeros((CHUNK, DK), f32)
            qdk = jnp.zeros((CHUNK, DK), f32)
            for li in range(len(LEVELS)):
                du = t["ex"][li] * rr[li]
                dq += jnp.where(t["up"][li], du, 0.0)
                dk += jnp.where(t["up"][li], 0.0, du)
                e = t["u"][li].astype(f32) * rr[li]
                qdk += jnp.where(t["up"][li], e, -e)
            dd = jnp.sum(do * v, axis=-1, keepdims=True)
            dq += dd * t["k"]
            dk += dd * t["qh"]
            gtb = gt.astype(bf16)
            ktb, qtb = t["kt"].astype(bf16), t["qt"].astype(bf16)
            dq_in = jnp.dot(dob, st0.astype(bf16), preferred_element_type=f32)
            dk_in = jnp.dot(vb, gtb, preferred_element_type=f32)
            dq += t["eb"] * dq_in
            dk += t["ebl"] * dk_in
            qdk += qtb.astype(f32) * dq_in - ktb.astype(f32) * dk_in
            p = _scores(t, mask)
            dvv = lax.dot_general(p.astype(bf16), dob, (TN, ((), ())), preferred_element_type=f32)
            dvv += lax.dot_general(ktb, gtb, (NT, ((), ())), preferred_element_type=f32)
            dv_ref[rs, cs] = dvv.astype(dv_ref.dtype)
            a_end = _colsum(gtb.astype(f32) * st1)
            dlf = _rev_cumsum_rows(qdk, t["row"]) + a_end
            dfv = dlf / t["f"] - dk
            df_ref[rs, cs] = (dfv * (1.0 - lb) * t["sig"] * (1.0 - t["sig"])).astype(df_ref.dtype)
            dlb_ref[:, cs] += _colsum(dfv * (1.0 - t["sig"]))
            dq_ref[rs, cs] = (dq * _dsilu(qp, t["sq"])).astype(dq_ref.dtype)
            gstate[hh] = gt * t["el"] + lax.dot_general(dob, qtb, (TN, ((), ())), preferred_element_type=f32)

    hp, wide = HEADS_PER_STEP, HEADS_PER_STEP * DK
    rev = lambda t: nt - 1 - t
    col = lambda off: pl.BlockSpec((tb, wide), functools.partial(lambda h, t, off: (rev(t), off + h), off=off // hp))
    blk = pl.BlockSpec((tb, wide), lambda h, t: (rev(t), h))
    nxt = lambda h, t: (jnp.minimum((rev(t) + 1) * nchunk, nc_total - 1), h, 0, 0)
    return pl.pallas_call(
        body, name="hgrn_bwd", grid=(HEADS // hp, nt),
        in_specs=[col(q0), col(f0), col(v0), col(g0), pl.BlockSpec((2, wide), lambda h, t: (0, h)),
                  pl.BlockSpec((1, DK), lambda h, t: (0, 0)), pl.BlockSpec(masks.shape, lambda h, t: (0, 0, 0)),
                  pl.BlockSpec(masks.shape, lambda h, t: (0, 0, 0)), blk, blk,
                  pl.BlockSpec((nchunk, hp, DK, DK), lambda h, t: (rev(t), h, 0, 0)),
                  pl.BlockSpec((1, hp, DK, DK), nxt)],
        out_specs=[blk, blk, blk, blk, pl.BlockSpec((1, wide), lambda h, t: (0, h)), pl.BlockSpec((1, DK), lambda h, t: (0, 0))],
        out_shape=[SDS((s, dv_total), bf16)] * 4 + [SDS((1, HEADS * DK), f32), SDS((1, DK), f32)],
        scratch_shapes=[pltpu.VMEM((hp, DK, DK), f32)], compiler_params=_params(("arbitrary", "arbitrary")),
    )(proj, proj, proj, proj, lb_param, gnorm, masks, masks_t, o_saved, dog, states, states)


def _adamw(name, g, w, m, v):
    r, c = w.shape
    tr = r
    for cand in (256, 128, 64, 32, 16, 8):
        if r % cand == 0 and r > cand:
            tr = cand
            break

    def body(g_ref, w_ref, m_ref, v_ref, d_ref, mo_ref, vo_ref):
        d_ref[...], mo_ref[...], vo_ref[...] = _adamw_math(g_ref[...], w_ref[...], m_ref[...], v_ref[...])

    blk = pl.BlockSpec((tr, c), lambda i: (i, 0))
    return pl.pallas_call(
        body, name=name, grid=(r // tr,), in_specs=[blk] * 4, out_specs=[blk] * 3, out_shape=[SDS((r, c), f32)] * 3,
        compiler_params=_params(("parallel",)),
    )(g, w, m, v)


def _ffn_in(h2, w_gt, w_ut, tm, ffb):
    s, d = h2.shape
    dff = w_gt.shape[0]

    def body(a_ref, wg_ref, wu_ref, dg_ref, du_ref, act_ref):
        a = a_ref[...]
        g = lax.dot_general(a, wg_ref[...], (NT, ((), ())), preferred_element_type=f32)
        u = lax.dot_general(a, wu_ref[...], (NT, ((), ())), preferred_element_type=f32)
        sg = _sigmoid(g)
        silu = g * sg
        dg_ref[...] = (u * _dsilu(g, sg)).astype(bf16)
        du_ref[...] = silu.astype(bf16)
        act_ref[...] = (silu * u).astype(bf16)

    w_spec = pl.BlockSpec((ffb, d), lambda j, i: (j, 0))
    o_spec = pl.BlockSpec((tm, ffb), lambda j, i: (i, j))
    return pl.pallas_call(
        body, name="ffn_in", grid=(dff // ffb, s // tm), in_specs=[pl.BlockSpec((tm, d), lambda j, i: (i, 0)), w_spec, w_spec],
        out_specs=[o_spec] * 3, out_shape=[SDS((s, dff), bf16)] * 3, compiler_params=_params(("parallel", "parallel")),
    )(h2, w_gt, w_ut)


def _ffn_down_bwd(dff_out, w_d, act_dg, act_du, tm, ffb):
    s, d = dff_out.shape
    dff = w_d.shape[0]

    def body(a_ref, w_ref, fg_ref, fu_ref, dg_ref, du_ref):
        da = lax.dot_general(a_ref[...], w_ref[...], (NT, ((), ())), preferred_element_type=f32)
        dg_ref[...] = (da * fg_ref[...].astype(f32)).astype(bf16)
        du_ref[...] = (da * fu_ref[...].astype(f32)).astype(bf16)

    t_spec = pl.BlockSpec((tm, ffb), lambda j, i: (i, j))
    return pl.pallas_call(
        body, name="d_ffn_down_in", grid=(dff // ffb, s // tm),
        in_specs=[pl.BlockSpec((tm, d), lambda j, i: (i, 0)), pl.BlockSpec((ffb, d), lambda j, i: (j, 0)), t_spec, t_spec],
        out_specs=[t_spec] * 2, out_shape=[SDS((s, dff), bf16)] * 2, compiler_params=_params(("parallel", "parallel")),
    )(dff_out, w_d, act_dg, act_du)


def _branch_merge(z_a, og, w_co, w_ho, proj, tm, gate_a0, gate_b0):
    s, dc = z_a.shape
    nsh, _, n = w_co.shape

    def body(za_ref, og_ref, wa_ref, wb_ref, ga_ref, gb_ref, m_ref, sa_ref, sb_ref, fa_ref, fb_ref):
        ya = jnp.dot(za_ref[...], wa_ref[...], preferred_element_type=f32)
        yb = jnp.dot(og_ref[...], wb_ref[...], preferred_element_type=f32)
        sa, sb_ = _sigmoid(ga_ref[...]), _sigmoid(gb_ref[...])
        m_ref[...] = (sa * ya + sb_ * yb).astype(bf16)
        sa_ref[...] = sa.astype(bf16)
        sb_ref[...] = sb_.astype(bf16)
        fa_ref[...] = (ya * sa * (1.0 - sa)).astype(bf16)
        fb_ref[...] = (yb * sb_ * (1.0 - sb_)).astype(bf16)

    a_spec = pl.BlockSpec((tm, dc), lambda i, j: (i, 0))
    w_spec = pl.BlockSpec((None, dc, n), lambda i, j: (j, 0, 0))
    gate = lambda c0: pl.BlockSpec((tm, n), functools.partial(lambda i, j, cb: (i, cb + j), cb=c0 // n))
    o_spec = pl.BlockSpec((tm, n), lambda i, j: (i, j))
    return pl.pallas_call(
        body, name="branch_merge", grid=(s // tm, nsh), in_specs=[a_spec, a_spec, w_spec, w_spec, gate(gate_a0), gate(gate_b0)],
        out_specs=[o_spec] * 5, out_shape=[SDS((s, nsh * n), bf16)] * 5, compiler_params=_params(("parallel", "parallel")),
    )(z_a, og, w_co, w_ho, proj, proj)


def _d_branch_merge(dmo, w_o, factors, tm):
    s, d = dmo.shape
    n = d // 2

    def body(a_ref, w_ref, sa_ref, sb_ref, fa_ref, fb_ref, dya_ref, dyb_ref, dga_ref, dgb_ref):
        dm = lax.dot_general(a_ref[...], w_ref[...], (NT, ((), ())), preferred_element_type=f32)
        for f_ref, o_ref in ((sa_ref, dya_ref), (sb_ref, dyb_ref), (fa_ref, dga_ref), (fb_ref, dgb_ref)):
            o_ref[...] = (dm * f_ref[...].astype(f32)).astype(bf16)

    t_spec = pl.BlockSpec((tm, n), lambda j, i: (i, j))
    return pl.pallas_call(
        body, name="d_branch_merge", grid=(d // n, s // tm),
        in_specs=[pl.BlockSpec((tm, d), lambda j, i: (i, 0)), pl.BlockSpec((n, d), lambda j, i: (j, 0))] + [t_spec] * 4,
        out_specs=[t_spec] * 4, out_shape=[SDS((s, d), bf16)] * 4, compiler_params=_params(("parallel", "parallel")),
    )(dmo, w_o, *factors)


def _local_step(x, tgt, mod, g_mix, g_ffn, g_fin, lb_param, gnorm, conv_w, project, get_w, prefetch, put_g, sent):
    s, d = x.shape
    dc = d // 2
    tm = min(512, s)
    tm2 = min(1024, s)
    te = min(256, s)
    tb = min(128, s)
    mt = s // tm
    nb = 512
    sh_m, sc_m, gt_m, sh_f, sc_f, gt_f = [mod[i] for i in range(N_MOD)]
    dh2 = d // 2

    def e1(rows, fulls, outs, accs):
        xv = rows[0][...]
        g, sc, sh = [r[...] for r in fulls]
        _, xh = _rms(xv)
        outs[0][...] = (xh * g * (1.0 + sc) + sh).astype(bf16)

    h, = _rowwise("prenorm_mix", e1, s, te, [(x, d, 0)], [g_mix, sc_m, sh_m], [(d, bf16)])
    proj, w_in = project(h, tm2)
    nsh, _, win_sh = w_in.shape
    z_a = _conv_fwd(proj, _after_tokens(conv_w, [prefetch("mix", proj)]), s, dc)
    og, o_saved, states = _hgrn_fwd(proj, lb_param, gnorm, s, dc, tb)
    w_co, w_ho, w_o = get_w("mix", og)

    gate_a0, gate_b0 = 7 * dh2, 9 * dh2
    merged, *merge_factors = _branch_merge(z_a, og, w_co, w_ho, proj, tm2, gate_a0, gate_b0)
    mo = _matmul("mix_out", merged, w_o, (s, d), bf16, (2, s // tm2, 1), pl.BlockSpec((tm2, d), lambda j, i, k: (i, 0)),
                 pl.BlockSpec((d, dh2), lambda j, i, k: (0, j)), pl.BlockSpec((tm2, dh2), lambda j, i, k: (i, j)), NN,
                 after=prefetch("ffn", merged))

    def e6(rows, fulls, outs, accs):
        xv, mov = rows[0][...], rows[1][...].astype(f32)
        gt, g, sc, sh = [r[...] for r in fulls]
        x1 = xv + gt * mov
        outs[0][...] = x1
        _, xh = _rms(x1)
        outs[1][...] = (xh * g * (1.0 + sc) + sh).astype(bf16)

    gt_m_late = _after_tokens(gt_m, [prefetch("ffn_down", mo)])
    x1, h2 = _rowwise("prenorm_ffn", e6, s, tm, [(x, d, 0), (mo, d, 0)], [gt_m_late, g_ffn, sc_f, sh_f], [(d, f32), (d, bf16)])
    w_gt, w_ut = get_w("ffn", h2)
    dff_ = w_gt.shape[0]
    act_dg, act_du, act = _ffn_in(h2, w_gt, w_ut, tm2, nb)
    w_d, = get_w("ffn_down", act)
    ff = _matmul("ffn_down", act, w_d, (s, d), bf16, (2, mt, 1), pl.BlockSpec((tm, dff_), lambda j, i, k: (i, 0)),
                 pl.BlockSpec((dff_, dh2), lambda j, i, k: (0, j)), pl.BlockSpec((tm, dh2), lambda j, i, k: (i, j)), NN)

    def e9(rows, fulls, outs, accs):
        x1v, ffv, tv = [r[...].astype(f32) for r in rows]
        gt, gf = fulls[0][...], fulls[1][...]
        x2 = x1v + gt * ffv
        r, xh = _rms(x2)
        err = xh * gf - tv
        accs[0][...] += 0.5 * jnp.sum(jnp.mean(err * err, axis=-1, keepdims=True), axis=0, keepdims=True)
        dy = err / d
        accs[1][...] += _colsum(dy * xh)
        dxh = dy * gf
        dx2 = r * (dxh - xh * jnp.mean(dxh * xh, axis=-1, keepdims=True))
        outs[0][...] = dx2
        outs[1][...] = (dx2 * gt).astype(bf16)
        accs[2][...] += _colsum(dx2 * ffv)

    dx2, dff, loss_acc, dg_fin, dgt_f = _rowwise("loss_head", e9, s, tm, [(x1, d, 0), (ff, d, 0), (tgt, d, 0)], [gt_f, g_fin],
                                                 [(d, f32), (d, bf16)], [(1, 128), (1, d), (1, d)])
    dgg, duu = _ffn_down_bwd(dff, w_d, act_dg, act_du, tm, dff_ // 4)

    def wgrad_rows(name, a, b, n_out):
        kb = 512
        return _matmul(name, a, b, (n_out, d), bf16, (n_out // kb, 2, 1), pl.BlockSpec((s, kb), lambda i, j, k: (0, i)),
                       pl.BlockSpec((s, dh2), lambda i, j, k: (0, j)), pl.BlockSpec((kb, dh2), lambda i, j, k: (i, j)), TN)

    gw_d = wgrad_rows("gw_ffn_down", act, dff, dff_)

    def ffn_in_bwd(name, a, w, after=None):
        return _matmul(name, a, w, (s, d), bf16, (2, mt, 1), pl.BlockSpec((tm, dff_), lambda j, i, k: (i, 0)),
                       pl.BlockSpec((dff_, dh2), lambda j, i, k: (0, j)), pl.BlockSpec((tm, dh2), lambda j, i, k: (i, j)), NN,
                       after=after)

    dh2b = ffn_in_bwd("d_ffn_up_in", duu, w_ut)
    gw_ut = wgrad_rows("gw_ffn_up", duu, h2, dff_)
    gw_gt = wgrad_rows("gw_ffn_gate", dgg, h2, dff_)
    dh2a = ffn_in_bwd("d_ffn_gate_in", dgg, w_gt, after=put_g("ffn", [gw_gt, gw_ut, gw_d]))
    sc_f_late = _after_tokens(sc_f, [sent("ffn", dh2a)])

    def b5(rows, fulls, outs, accs):
        da, db, x1v, dx2v, mov = [r[...].astype(f32) for r in rows]
        sc, g, gt = [r[...] for r in fulls]
        dh = da + db
        r, xh = _rms(x1v)
        accs[0][...] += _colsum(dh)
        accs[1][...] += _colsum(dh * (xh * g))
        dn = dh * (1.0 + sc)
        accs[2][...] += _colsum(dn * xh)
        dxh = dn * g
        dx1 = dx2v + r * (dxh - xh * jnp.mean(dxh * xh, axis=-1, keepdims=True))
        outs[0][...] = dx1
        accs[3][...] += _colsum(dx1 * mov)
        outs[1][...] = (dx1 * gt).astype(bf16)

    dx1, dmo, dsh_f, dsc_f, dg_ffn, dgt_m = _rowwise(
        "d_prenorm_ffn", b5, s, te, [(dh2a, d, 0), (dh2b, d, 0), (x1, d, 0), (dx2, d, 0), (mo, d, 0)], [sc_f_late, g_ffn, gt_m],
        [(d, f32), (d, bf16)], [(1, d)] * 4)
    dya, dyb, dga, dgb = _d_branch_merge(dmo, w_o, merge_factors, tm)
    gw_o = wgrad_rows("gw_mix_out", merged, dmo, d)

    def out_proj_bwd(name, dy, w):
        return _matmul(name, dy, w, (s, dc), bf16, (1, s // tm2, nsh), pl.BlockSpec((tm2, d // nsh), lambda j, i, k: (i, k)),
                       pl.BlockSpec((None, dc, d // nsh), lambda j, i, k: (k, 0, 0)), pl.BlockSpec((tm2, dc), lambda j, i, k: (i, 0)), NT)

    def out_proj_wgrad(name, a, dy):
        return _matmul(name, a, dy, (nsh, dc, d // nsh), bf16, (1, nsh, 1), pl.BlockSpec((s, dc), lambda i, j, k: (0, 0)),
                       pl.BlockSpec((s, d // nsh), lambda i, j, k: (0, j)), pl.BlockSpec((None, dc, d // nsh), lambda i, j, k: (j, 0, 0)), TN)

    dz_a = out_proj_bwd("d_conv_out_in", dya, w_co)
    dog = out_proj_bwd("d_hgrn_out_in", dyb, w_ho)
    gw_co = out_proj_wgrad("gw_conv_out", z_a, dya)
    gw_ho = out_proj_wgrad("gw_hgrn_out", og, dyb)
    conv_w_late = _after_tokens(conv_w, [put_g("mix", [gw_co, gw_ho, gw_o])])
    dab, dac, dax, dconv_w = _conv_bwd(proj, conv_w_late, dz_a, s, dc)
    lb_param_late = _after_tokens(lb_param, [sent("mix", dab)])
    dq, dfl, dvi, dgo, dlb, dgn = _hgrn_bwd(proj, lb_param_late, gnorm, o_saved, states, dog, s, dc, tb)
    dproj = jnp.concatenate([dab, dac, dax, dq, dfl, dvi, dgo, dga, dgb], axis=1)
    gw_in = _matmul("gw_proj", h, dproj, (nsh, d, win_sh), bf16, (nsh, d // 512, 1), pl.BlockSpec((s, 512), lambda j, i, k: (0, i)),
                    pl.BlockSpec((s, win_sh), lambda j, i, k: (0, j)), pl.BlockSpec((None, 512, win_sh), lambda j, i, k: (j, i, 0)), TN)
    dh = _matmul("d_proj_in", dproj, w_in, (s, d), bf16, (1, s // tm2, nsh), pl.BlockSpec((tm2, win_sh), lambda j, i, k: (i, k)),
                 pl.BlockSpec((None, d, win_sh), lambda j, i, k: (k, 0, 0)), pl.BlockSpec((tm2, d), lambda j, i, k: (i, 0)), NT,
                 after=put_g("in", [gw_in]))

    def b12(rows, fulls, outs, accs):
        dhv, xv, dx1v = [r[...].astype(f32) for r in rows]
        sc, g = fulls[0][...], fulls[1][...]
        r, xh = _rms(xv)
        accs[0][...] += _colsum(dhv)
        accs[1][...] += _colsum(dhv * (xh * g))
        dn = dhv * (1.0 + sc)
        accs[2][...] += _colsum(dn * xh)
        dxh = dn * g
        outs[0][...] = dx1v + r * (dxh - xh * jnp.mean(dxh * xh, axis=-1, keepdims=True))

    dx, dsh_m, dsc_m, dg_mix = _rowwise("d_prenorm_mix", b12, s, tm, [(dh, d, 0), (x, d, 0), (dx1, d, 0)], [sc_m, g_mix],
                                        [(d, f32)], [(1, d)] * 3)
    dmod = [dsh_m, dsc_m, dgt_m, dsh_f, dsc_f, dgt_f]
    small = dict(loss=loss_acc, g_mix=dg_mix, g_ffn=dg_ffn, g_fin=dg_fin, lb=dlb, gnorm=dgn, conv_w=dconv_w)
    return dx, dmod, small


def _ada_fwd(c_all, w_sh, b_sh):
    def body(c_ref, w_ref, b_ref, o_ref):
        cv = c_ref[...]
        ca = (cv * _sigmoid(cv)).astype(bf16)
        o_ref[...] = jnp.dot(ca, w_ref[...].astype(bf16), preferred_element_type=f32) + b_ref[...]

    return pl.pallas_call(body, name="ada_fwd", out_shape=SDS((c_all.shape[0], w_sh.shape[1]), f32),
                          compiler_params=pltpu.CompilerParams(vmem_limit_bytes=V7X_VMEM_LIMIT))(c_all, w_sh, b_sh)


def _ada_wgrad(c_all, dmod_sh):
    def body(c_ref, d_ref, o_ref):
        cv = c_ref[...]
        ca = (cv * _sigmoid(cv)).astype(bf16)
        o_ref[...] = lax.dot_general(ca, d_ref[...].astype(bf16), (TN, ((), ())), preferred_element_type=f32)

    return pl.pallas_call(body, name="ada_wgrad", out_shape=SDS((c_all.shape[1], dmod_sh.shape[1]), f32),
                          compiler_params=pltpu.CompilerParams(vmem_limit_bytes=V7X_VMEM_LIMIT))(c_all, dmod_sh)


def _lb_grad(lb_param, dlb):
    def body(p_ref, d_ref, o_ref):
        p = p_ref[...]
        lb = _sigmoid(p[0:1, :] - p[1:2, :])
        gl = d_ref[...] * lb * (1.0 - lb)
        o_ref[0:1, :] = gl
        o_ref[1:2, :] = -gl

    return pl.pallas_call(body, name="lb_grad", out_shape=SDS(lb_param.shape, f32))(lb_param, dlb)


def _sum_small(gathered):
    def body(g_ref, o_ref):
        acc = g_ref[0]
        for dd in range(1, NDEV):
            acc = acc + g_ref[dd]
        o_ref[...] = acc

    return pl.pallas_call(body, name="sum_small", out_shape=SDS(gathered.shape[1:], f32))(gathered)


def kernel(x, c, w_ada, b_ada, norm_mix_g, w_in, conv_w, lb_param, gnorm_g, w_conv_out, w_hgrn_out, w_o, norm_ffn_g, w_ffn_gate, w_ffn_up, w_ffn_down, norm_final_g, loss_target, m_w_ada, m_b_ada, m_norm_mix_g, m_w_in, m_conv_w, m_lb_param, m_gnorm_g, m_w_conv_out, m_w_hgrn_out, m_w_o, m_norm_ffn_g, m_w_ffn_gate, m_w_ffn_up, m_w_ffn_down, m_norm_final_g, v_w_ada, v_b_ada, v_norm_mix_g, v_w_in, v_conv_w, v_lb_param, v_gnorm_g, v_w_conv_out, v_w_hgrn_out, v_w_o, v_norm_ffn_g, v_w_ffn_gate, v_w_ffn_up, v_w_ffn_down, v_norm_final_g):
    assert lb_param.shape[0] == 2 and w_ada.shape[0] == 1
    s, d = x.shape[1], x.shape[2]
    me = 4 * lax.axis_index("x") + 2 * lax.axis_index("y") + lax.axis_index("c")
    ada_cols = w_ada.shape[2]

    me1 = me.astype(jnp.int32).reshape(1)
    placed_in = [_place_shard("place_in0", me1, w_in[0])]
    c_all, cw_all = _all_gather("gather_cond", [c, conv_w[0]], after=placed_in)
    c_all = c_all.reshape(NDEV, d)
    conv_w_full = jnp.transpose(cw_all, (1, 0, 2)).reshape(conv_w.shape[1], -1)
    b_sh = lax.dynamic_slice_in_dim(b_ada, me * ada_cols, ada_cols, axis=1)
    mod_cols = _ada_fwd(c_all, w_ada[0], b_sh)
    mod_all, = _all_gather("gather_mod", [mod_cols])
    mod = lax.dynamic_index_in_dim(mod_all, me, axis=1, keepdims=False).reshape(N_MOD, 1, d)

    shard_groups = {"mix": [w_conv_out[0], w_hgrn_out[0], w_o[0]], "ffn": [w_ffn_gate[0].T, w_ffn_up[0].T],
                    "ffn_down": [w_ffn_down[0]]}
    own_slot = lambda frm, to: _flat(frm)
    gather_plan = lambda n: [(a, j, own_slot, own_slot) for a in range(n) for j in (1,) + ICI_RELATIONS]
    flat = lambda a: a.reshape(a.shape[0] * a.shape[1], a.shape[2])
    to8 = lambda a: a.reshape(NDEV, a.shape[0] // NDEV, a.shape[1])
    sib_plan = [(0, 1, own_slot, own_slot)]
    ici_plan = [(0, j, own_slot, own_slot) for j in ICI_RELATIONS]
    ss_a, rs_a, _, lands, tok_a = _push_start("gather_start_in_sib", [], placed_in, sib_plan, after=mod_all)
    ss_b, rs_b, _, lands_in, tok_b = _push_start("gather_start_in", [], lands, ici_plan, after=tok_a)
    gathering, tokens = {}, [tok_a, tok_b]
    for grp, sh in shard_groups.items():
        lands = [_place_shard(f"place_{grp}{i}", me1, a, after=tokens[-1]) for i, a in enumerate(sh)]
        ss, rs, _, lands, tok = _push_start("gather_start_" + grp, [], lands, gather_plan(len(lands)), after=tokens[-1])
        gathering[grp] = (ss, rs, lands)
        tokens.append(tok)

    pos = (lax.axis_index("x"), lax.axis_index("y"), lax.axis_index("c"))
    ids = lambda frm, rels: jnp.stack([_flat(_peer(frm, j)) for j in rels]).astype(jnp.int32)

    def project(h, tm):
        _, lands = _push_wait("gather_wait_in_sib", ss_a, rs_a, [], lands_in, sib_plan, h)
        proj = _proj_part("proj_local", h, lands[0], ids(pos, (0, 1)), None, tm)
        _, lands = _push_wait("gather_wait_in", ss_b, rs_b, [], lands, ici_plan, proj)
        ss, rs, _, lands, tok = _push_start("gather_fwd_in", [], lands, fwd_plan(1))
        proj = _proj_part("proj_ici", h, lands[0], ids(pos, ICI_RELATIONS), proj, tm, after=tok)
        _, lands = _push_wait("gather_fwd_wait_in", ss, rs, [], lands, fwd_plan(1), proj)
        proj = _proj_part("proj_fwd", h, lands[0], ids(_peer(pos, 1), ICI_RELATIONS), proj, tm)
        return proj, lands[0]

    fwd_slot = lambda i: (lambda frm, to: _flat(_peer(frm, ICI_RELATIONS[i])))
    fwd_plan = lambda n: [(a, 1, fwd_slot(i), fwd_slot(i)) for a in range(n) for i in range(len(ICI_RELATIONS))]
    forwarding = {}

    def prefetch(grp, after):
        ss, rs, lands = gathering[grp]
        _, lands = _push_wait("gather_wait_" + grp, ss, rs, [], lands, gather_plan(len(lands)), after)
        ss, rs, _, lands, tok = _push_start("gather_fwd_" + grp, [], lands, fwd_plan(len(lands)))
        forwarding[grp] = (ss, rs, lands)
        return tok

    def get_w(grp, after):
        ss, rs, lands = forwarding[grp]
        _, full = _push_wait("gather_fwd_wait_" + grp, ss, rs, [], lands, fwd_plan(len(lands)), after)
        return [f if i < 2 and grp == "mix" else flat(f) for i, f in enumerate(full)]

    core = lax.axis_index("c").astype(jnp.int32).reshape(1)
    chip = (2 * lax.axis_index("x") + lax.axis_index("y")).astype(jnp.int32).reshape(1)
    scatter_plan = lambda n: [(a, j, lambda frm, to: _chip(to), lambda frm, to: _chip(frm)) for a in range(n) for j in ICI_RELATIONS]
    scattering = {}

    swap_plan = lambda n: [(a, 1, functools.partial(lambda frm, to, q: 2 * q + to[2], q=q), functools.partial(lambda frm, to, q: q, q=q))
                           for a in range(n) for q in range(NDEV // 2)]
    swapping = {}

    def start_ici(grp, g8, recv):
        pairs = [_pair_sum(f"pair_sum_{grp}{i}", core, g, r, _row_tile(g.shape[1], 1024)) for i, (g, r) in enumerate(zip(g8, recv))]
        lands = [lax.empty(p.shape, p.dtype) for p in pairs]
        ss, rs, srcs, lands, tok = _push_start("scatter_start_" + grp, pairs, lands, scatter_plan(len(pairs)))
        scattering[grp] = (ss, rs, srcs, lands)
        return tok

    def put_g(grp, grads):
        g8 = [g if g.ndim == 3 else to8(g) for g in grads]
        if grp == "in":
            return start_ici(grp, g8, _sibling_swap("scatter_pair_" + grp, g8))
        lands = [lax.empty((NDEV // 2,) + g.shape[1:], g.dtype) for g in g8]
        ss, rs, srcs, lands, tok = _push_start("scatter_swap_" + grp, g8, lands, swap_plan(len(g8)))
        swapping[grp] = (ss, rs, srcs, lands)
        return tok

    def sent(grp, after):
        ss, rs, srcs, lands = swapping[grp]
        g8, recv = _push_wait("scatter_swapped_" + grp, ss, rs, srcs, lands, swap_plan(len(srcs)), after)
        return start_ici(grp, g8, recv)

    def reduced(grp, after, names):
        ss, rs, srcs, lands = scattering[grp]
        srcs, lands = _push_wait("scatter_wait_" + grp, ss, rs, srcs, lands, scatter_plan(len(srcs)), after)
        for i, (p, r, nm) in enumerate(zip(srcs, lands, names)):
            tr = nm in ("w_ffn_gate", "w_ffn_up")
            wmv = tuple(a[0].T if tr else a[0] for a in weights[nm])
            out = _chip_sum_adamw(f"chip_sum_{grp}{i}", chip, p, r, _row_tile(p.shape[1]), wmv)
            res[nm] = [(a.T if tr else a).reshape(weights[nm][0].shape) for a in out]

    dx, dmod, small = _local_step(x[0], loss_target[0], mod, _after_tokens(norm_mix_g, tokens), norm_ffn_g,
                                  norm_final_g.reshape(1, d), lb_param, gnorm_g, conv_w_full, project, get_w, prefetch, put_g, sent)

    pieces = [*dmod, small["g_mix"], small["g_ffn"], small["g_fin"], small["lb"], small["gnorm"], small["loss"],
              small["conv_w"].reshape(1, -1)]
    widths = [p.shape[1] for p in pieces]
    offs = np.concatenate([[0], np.cumsum(widths)])
    packed = jnp.concatenate(pieces, axis=1)
    gathered, = _all_gather("gather_small", [packed])
    summed = _sum_small(gathered)
    part = lambda i: summed[:, offs[i]:offs[i + 1]]
    g_b_ada = summed[:, :N_MOD * d]
    g_norm_mix, g_norm_ffn, g_norm_fin, g_lb_row, g_gnorm, loss_vec, g_convw_flat = [part(i) for i in range(N_MOD, N_MOD + 7)]
    loss = loss_vec[0, 0]
    dmod_all = gathered[:, 0, :N_MOD * d]
    g_w_ada = _ada_wgrad(c_all, lax.dynamic_slice_in_dim(dmod_all, me * ada_cols, ada_cols, axis=1))
    g_lb = _lb_grad(lb_param, g_lb_row)
    cw_cols = conv_w.shape[2]
    g_conv_w = lax.dynamic_slice_in_dim(g_convw_flat.reshape(conv_w.shape[1], -1), me * cw_cols, cw_cols, axis=1)

    grads = dict(w_ada=g_w_ada, b_ada=g_b_ada, norm_mix_g=g_norm_mix, conv_w=g_conv_w, lb_param=g_lb, gnorm_g=g_gnorm,
                 norm_ffn_g=g_norm_ffn, norm_final_g=g_norm_fin)
    weights = dict(w_ada=(w_ada, m_w_ada, v_w_ada), b_ada=(b_ada, m_b_ada, v_b_ada), norm_mix_g=(norm_mix_g, m_norm_mix_g, v_norm_mix_g),
                   w_in=(w_in, m_w_in, v_w_in), conv_w=(conv_w, m_conv_w, v_conv_w), lb_param=(lb_param, m_lb_param, v_lb_param),
                   gnorm_g=(gnorm_g, m_gnorm_g, v_gnorm_g), w_conv_out=(w_conv_out, m_w_conv_out, v_w_conv_out),
                   w_hgrn_out=(w_hgrn_out, m_w_hgrn_out, v_w_hgrn_out), w_o=(w_o, m_w_o, v_w_o),
                   norm_ffn_g=(norm_ffn_g, m_norm_ffn_g, v_norm_ffn_g), w_ffn_gate=(w_ffn_gate, m_w_ffn_gate, v_w_ffn_gate),
                   w_ffn_up=(w_ffn_up, m_w_ffn_up, v_w_ffn_up), w_ffn_down=(w_ffn_down, m_w_ffn_down, v_w_ffn_down),
                   norm_final_g=(norm_final_g, m_norm_final_g, v_norm_final_g))
    res = {}

    def update(nm):
        w, m, v = weights[nm]
        shape2 = (w.shape[-2], w.shape[-1]) if w.ndim >= 2 else (1, w.shape[0])
        g2 = grads[nm].reshape(shape2)
        dl, mn, vn = _adamw("adamw_" + nm, g2, w.reshape(shape2), m.reshape(shape2), v.reshape(shape2))
        res[nm] = [a.reshape(w.shape) for a in (g2, dl, mn, vn)]

    for nm in list(grads):
        update(nm)
    reduced("ffn", res["w_ada"][1], ("w_ffn_gate", "w_ffn_up", "w_ffn_down"))
    reduced("mix", res["w_ffn_down"][1], ("w_conv_out", "w_hgrn_out", "w_o"))
    reduced("in", res["w_o"][1], ("w_in",))
    outs = [[res[nm][i] for nm in weights] for i in range(4)]
    return (loss, dx.reshape(x.shape), *outs[0], *outs[1], *outs[2], *outs[3])
```

```python
import functools

import numpy as np
import jax
import jax.numpy as jnp
from jax import lax
from jax.experimental import pallas as pl
from jax.experimental.pallas import tpu as pltpu

f32, bf16 = jnp.float32, jnp.bfloat16
SDS = jax.ShapeDtypeStruct

EPS = 1e-6
HEADS, DK, CHUNK = 8, 128, 64
HEADS_PER_STEP = 4
N_MOD = 6
NDEV = 8
ADAM_LR, ADAM_B1, ADAM_B2, ADAM_EPS, ADAM_WD, ADAM_STEP = 0.001, 0.9, 0.999, 1e-08, 0.01, 10
LEVELS = (32, 16, 8, 4, 2, 1)
V7X_VMEM_LIMIT = 56 * 1024 * 1024
HBM = pl.BlockSpec(memory_space=pltpu.HBM)
MESH = pl.DeviceIdType.MESH


def _params(sem):
    return pltpu.CompilerParams(dimension_semantics=sem, vmem_limit_bytes=V7X_VMEM_LIMIT)


def _sigmoid(x):
    return jax.nn.sigmoid(x)


def _dsilu(x, s):
    return s * (1.0 + x * (1.0 - s))


def _mesh_pos():
    x, y, c = lax.axis_index("x"), lax.axis_index("y"), lax.axis_index("c")
    return x, y, c


def _peer(pos, j):
    x, y, c = pos
    return (1 - x if j & 4 else x, 1 - y if j & 2 else y, 1 - c if j & 1 else c)


def _flat(pos):
    return 4 * pos[0] + 2 * pos[1] + pos[2]


def _all_gather(name, arrs, after=()):
    n, ne = len(arrs), len(after)
    out_shapes = [SDS((NDEV,) + a.shape, a.dtype) for a in arrs]

    def body(*refs):
        ins, outs = refs[:n], refs[n + ne:2 * n + ne]
        send_sems, recv_sems, local_sems = refs[2 * n + ne:]
        pos = _mesh_pos()
        me = _flat(pos)

        def copy(a, j, frm, to_pos):
            k = a * (NDEV - 1) + j - 1
            return pltpu.make_async_remote_copy(src_ref=ins[a], dst_ref=outs[a].at[frm], send_sem=send_sems.at[k],
                                                recv_sem=recv_sems.at[k], device_id=to_pos, device_id_type=MESH)

        local = [pltpu.make_async_copy(ins[a], outs[a].at[me], local_sems.at[a]) for a in range(n)]
        for cp in local:
            cp.start()
        sends = [copy(a, j, me, _peer(pos, j)) for j in range(1, NDEV) for a in range(n)]
        for cp in sends:
            cp.start()
        for j in range(1, NDEV):
            for a in range(n):
                copy(a, j, _flat(_peer(pos, j)), pos).wait_recv()
        for cp in sends:
            cp.wait_send()
        for cp in local:
            cp.wait()

    return pl.pallas_call(
        body, name=name, out_shape=out_shapes, in_specs=[HBM] * n + [ANY] * ne, out_specs=[HBM] * n,
        scratch_shapes=[pltpu.SemaphoreType.DMA((n * (NDEV - 1),)), pltpu.SemaphoreType.DMA((n * (NDEV - 1),)),
                        pltpu.SemaphoreType.DMA((n,))],
    )(*arrs, *after)


SEM = pl.BlockSpec(memory_space=pltpu.SEMAPHORE)
ANY = pl.BlockSpec(memory_space=pl.ANY)
EFFECT = pltpu.SideEffectType.DATAFLOW_SIDE_EFFECTING
ICI_RELATIONS = (2, 4, 6)


def _chip(pos):
    return 2 * pos[0] + pos[1]


def _hbm(a):
    return pltpu.with_memory_space_constraint(a, pltpu.HBM)


def _plan_copy(plan_entry, k, pos, frm, to, src_refs, land_refs, send_sems, recv_sems):
    a, _, src_slot, dst_slot = plan_entry
    s = src_refs[a] if src_slot is None else src_refs[a].at[src_slot(frm, to)]
    return pltpu.make_async_remote_copy(src_ref=s, dst_ref=land_refs[a].at[dst_slot(frm, to)], send_sem=send_sems.at[k],
                                        recv_sem=recv_sems.at[k], device_id=to, device_id_type=MESH)


def _push_start(name, srcs, lands, plan, after=None):
    ns, nb, nk = len(srcs), len(srcs) + len(lands), len(plan)
    extra = [] if after is None else [after]

    def body(*refs):
        land_refs = refs[ns:nb]
        src_refs = refs[:ns] if ns else land_refs
        send_sems, recv_sems = refs[nb + len(extra)], refs[nb + len(extra) + 1]
        pos = _mesh_pos()
        for k, e in enumerate(plan):
            _plan_copy(e, k, pos, pos, _peer(pos, e[1]), src_refs, land_refs, send_sems, recv_sems).start()
        refs[-1][...] = jnp.zeros_like(refs[-1])

    outs = pl.pallas_call(
        body, name=name,
        out_shape=[pltpu.SemaphoreType.DMA((nk,)), pltpu.SemaphoreType.DMA((nk,))] + [pltpu.HBM(a.shape, a.dtype) for a in srcs + lands]
        + [SDS((8, 128), f32)],
        in_specs=[HBM] * nb + [ANY] * len(extra), out_specs=[SEM, SEM] + [HBM] * nb + [pl.BlockSpec(memory_space=pltpu.VMEM)],
        input_output_aliases={i: 2 + i for i in range(nb)},
        compiler_params=pltpu.CompilerParams(has_side_effects=EFFECT),
    )(*[_hbm(a) for a in srcs + lands], *extra)
    return outs[0], outs[1], list(outs[2:2 + ns]), list(outs[2 + ns:2 + nb]), outs[-1]


def _push_wait(name, send_sems, recv_sems, srcs, lands, plan, after):
    ns, nb = len(srcs), len(srcs) + len(lands)

    def body(*refs):
        land_refs = refs[ns:nb]
        src_refs = refs[:ns] if ns else land_refs
        ssem, rsem = refs[nb], refs[nb + 1]
        pos = _mesh_pos()
        for k, e in enumerate(plan):
            peer = _peer(pos, e[1])
            _plan_copy(e, k, pos, pos, peer, src_refs, land_refs, ssem, rsem).wait_send()
            _plan_copy(e, k, pos, peer, pos, src_refs, land_refs, ssem, rsem).wait_recv()

    outs = pl.pallas_call(
        body, name=name, out_shape=[pltpu.HBM(a.shape, a.dtype) for a in srcs + lands],
        in_specs=[HBM] * nb + [SEM, SEM, ANY], out_specs=[HBM] * nb,
        input_output_aliases={i: i for i in range(nb)},
        compiler_params=pltpu.CompilerParams(has_side_effects=EFFECT),
    )(*srcs, *lands, send_sems, recv_sems, after)
    return list(outs[:ns]), list(outs[ns:])


def _after_tokens(small, tokens):
    for t in tokens:
        if t is not None:
            small = small + t[0:1, 0:1].reshape((1,) * small.ndim)
    return small


def _place_shard(name, me, shard, after=None):
    r, c = shard.shape
    tr = _row_tile(r)
    extra = [] if after is None else [after]

    def body(me_ref, s_ref, *rest):
        rest[-1][...] = s_ref[...].astype(bf16)

    return pl.pallas_call(
        body, name=name, out_shape=SDS((NDEV, r, c), bf16),
        grid_spec=pltpu.PrefetchScalarGridSpec(
            num_scalar_prefetch=1, grid=(r // tr,), in_specs=[pl.BlockSpec((tr, c), lambda i, me_ref: (i, 0))] + [ANY] * len(extra),
            out_specs=pl.BlockSpec((None, tr, c), lambda i, me_ref: (me_ref[0], i, 0))),
        compiler_params=_params(("parallel",)),
    )(me, shard, *extra)


def _row_tile(r, most=256):
    return max(t for t in range(16, most + 1, 16) if r % t == 0)


def _sibling_swap(name, grads):
    n = len(grads)
    nchip = NDEV // 2

    def body(*refs):
        g_refs, out_refs = refs[:n], refs[n:2 * n]
        send_sems, recv_sems = refs[2 * n:]
        pos = _mesh_pos()
        sib = _peer(pos, 1)
        sends = []
        for a in range(n):
            for q in range(nchip):
                k = a * nchip + q
                sends.append(pltpu.make_async_remote_copy(src_ref=g_refs[a].at[2 * q + sib[2]], dst_ref=out_refs[a].at[q],
                                                          send_sem=send_sems.at[k], recv_sem=recv_sems.at[k], device_id=sib,
                                                          device_id_type=MESH))
        for cp in sends:
            cp.start()
        for cp in sends:
            cp.wait()

    return pl.pallas_call(
        body, name=name, out_shape=[SDS((nchip,) + a.shape[1:], a.dtype) for a in grads], in_specs=[HBM] * n, out_specs=[HBM] * n,
        scratch_shapes=[pltpu.SemaphoreType.DMA((n * nchip,)), pltpu.SemaphoreType.DMA((n * nchip,))],
    )(*grads)


def _pair_sum(name, core, grad, recv, tr):
    _, r, c = grad.shape
    nchip = NDEV // 2

    def body(core_ref, g_ref, r_ref, o_ref):
        o_ref[...] = (g_ref[...].astype(f32) + r_ref[...].astype(f32)).astype(o_ref.dtype)

    return pl.pallas_call(
        body, name=name, out_shape=SDS((nchip, r, c), grad.dtype),
        grid_spec=pltpu.PrefetchScalarGridSpec(
            num_scalar_prefetch=1, grid=(nchip, r // tr),
            in_specs=[pl.BlockSpec((None, tr, c), lambda q, i, core_ref: (2 * q + core_ref[0], i, 0)),
                      pl.BlockSpec((None, tr, c), lambda q, i, core_ref: (q, i, 0))],
            out_specs=pl.BlockSpec((None, tr, c), lambda q, i, core_ref: (q, i, 0))),
        compiler_params=_params(("parallel", "parallel")),
    )(core, grad, recv)


def _adamw_math(g, w, m, v):
    mn = ADAM_B1 * m + (1.0 - ADAM_B1) * g
    vn = ADAM_B2 * v + (1.0 - ADAM_B2) * jnp.square(g)
    m_hat = mn / (1.0 - ADAM_B1 ** ADAM_STEP)
    v_hat = vn / (1.0 - ADAM_B2 ** ADAM_STEP)
    return -ADAM_LR * (m_hat / (jnp.sqrt(v_hat) + ADAM_EPS) + ADAM_WD * w), mn, vn


def _chip_sum_adamw(name, chip, pairs, recv, tr, wmv):
    nchip, r, c = pairs.shape

    def body(chip_ref, p_ref, r_ref, w_ref, m_ref, v_ref, g_ref, d_ref, mo_ref, vo_ref):
        mine = chip_ref[0]
        own = p_ref[...].astype(f32)
        acc = jnp.zeros((tr, c), f32)
        for q in range(nchip):
            acc = acc + jnp.where(mine == q, own, r_ref[q].astype(f32))
        g_ref[...] = acc
        d_ref[...], mo_ref[...], vo_ref[...] = _adamw_math(acc, w_ref[...], m_ref[...], v_ref[...])

    blk = pl.BlockSpec((tr, c), lambda i, chip_ref: (i, 0))
    return pl.pallas_call(
        body, name=name, out_shape=[SDS((r, c), f32)] * 4,
        grid_spec=pltpu.PrefetchScalarGridSpec(
            num_scalar_prefetch=1, grid=(r // tr,),
            in_specs=[pl.BlockSpec((None, tr, c), lambda i, chip_ref: (chip_ref[0], i, 0)),
                      pl.BlockSpec((nchip, tr, c), lambda i, chip_ref: (0, i, 0))] + [blk] * 3,
            out_specs=[blk] * 4),
        compiler_params=_params(("parallel",)),
    )(chip, pairs, recv, *wmv)


def _matmul(name, a, b, out_shape, out_dtype, grid, a_spec, b_spec, o_spec, dims, after=None):
    ksteps = grid[2]
    acc_shape = tuple(d for d in o_spec.block_shape if d is not None)
    extra = [] if after is None else [after]

    def body(a_ref, b_ref, *rest):
        o_ref, acc = rest[len(extra)], rest[len(extra) + 1:]
        prod = lax.dot_general(a_ref[...], b_ref[...], (dims, ((), ())), preferred_element_type=f32)
        if ksteps == 1:
            o_ref[...] = prod.astype(o_ref.dtype)
        else:
            k = pl.program_id(2)

            @pl.when(k == 0)
            def _():
                acc[0][...] = prod

            @pl.when(k > 0)
            def _():
                acc[0][...] += prod

            @pl.when(k == ksteps - 1)
            def _():
                o_ref[...] = acc[0][...].astype(o_ref.dtype)

    return pl.pallas_call(
        body, name=name, grid=grid, in_specs=[a_spec, b_spec] + [ANY] * len(extra), out_specs=o_spec,
        out_shape=SDS(out_shape, out_dtype), scratch_shapes=[] if ksteps == 1 else [pltpu.VMEM(acc_shape, f32)],
        compiler_params=_params(("parallel", "parallel", "arbitrary")),
    )(a, b, *extra)


NN, NT, TN = ((1,), (0,)), ((1,), (1,)), ((0,), (0,))


def _proj_part(name, h, w_sh, ids, prev, tm, after=None):
    s, d = h.shape
    nsh, _, n = w_sh.shape
    extra = ([] if prev is None else [prev]) + ([] if after is None else [after])

    def body(ids_ref, a_ref, b_ref, *rest):
        rest[-1][...] = jnp.dot(a_ref[...], b_ref[...], preferred_element_type=f32)

    return pl.pallas_call(
        body, name=name, out_shape=SDS((s, nsh * n), f32),
        grid_spec=pltpu.PrefetchScalarGridSpec(
            num_scalar_prefetch=1, grid=(ids.shape[0], s // tm),
            in_specs=[pl.BlockSpec((tm, d), lambda j, i, ids_ref: (i, 0)),
                      pl.BlockSpec((None, d, n), lambda j, i, ids_ref: (ids_ref[j], 0, 0))] + [ANY] * len(extra),
            out_specs=pl.BlockSpec((tm, n), lambda j, i, ids_ref: (i, ids_ref[j]))),
        input_output_aliases={} if prev is None else {3: 0},
        compiler_params=_params(("parallel", "parallel")),
    )(ids, h, w_sh, *extra)


def _rowwise(name, fn, n_rows, tm, rows, fulls, row_outs, acc_outs=(), ncol=1):
    assert ncol == 1 or not acc_outs
    nr, nf, no, na = len(rows), len(fulls), len(row_outs), len(acc_outs)
    in_specs = [pl.BlockSpec((tm, w), functools.partial(lambda i, j, cb: (i, cb + j), cb=cb)) for (_, w, cb) in rows]
    in_specs += [pl.BlockSpec(a.shape, functools.partial(lambda i, j, nd: (0,) * nd, nd=a.ndim)) for a in fulls]
    out_shape = [SDS((n_rows, w), dt) for (w, dt) in row_outs] + [SDS(s, f32) for s in acc_outs]
    out_specs = [pl.BlockSpec((tm, w // ncol), lambda i, j: (i, j)) for (w, _) in row_outs]
    out_specs += [pl.BlockSpec(s, functools.partial(lambda i, j, nd: (0,) * nd, nd=len(s))) for s in acc_outs]

    def body(*refs):
        if na:
            @pl.when(pl.program_id(0) == 0)
            def _():
                for r in refs[nr + nf + no:]:
                    r[...] = jnp.zeros(r.shape, r.dtype)
        fn(refs[:nr], refs[nr:nr + nf], refs[nr + nf:nr + nf + no], refs[nr + nf + no:])

    return pl.pallas_call(
        body, name=name, grid=(n_rows // tm, ncol), in_specs=in_specs, out_specs=out_specs, out_shape=out_shape,
        compiler_params=_params(("arbitrary" if na else "parallel", "arbitrary" if na else "parallel")),
    )(*[r[0] for r in rows], *fulls)


def _rms(x):
    r = lax.rsqrt(jnp.mean(x * x, axis=-1, keepdims=True) + EPS)
    return r, x * r


def _colsum(v):
    return jnp.sum(v, axis=0, keepdims=True)


def _shift_down(u, row, k):
    return jnp.where(row >= k, pltpu.roll(u, k, 0), 0.0)


def _shift_up(u, row, k):
    n = u.shape[0]
    return jnp.where(row < n - k, pltpu.roll(u, n - k, 0), 0.0)


def _conv_fwd(proj, conv_w, s, dc):
    nb = dc // 128

    def body(ab_ref, ac_ref, ax_ref, w_ref, z_ref):
        u = ac_ref[...] * ax_ref[...]
        row = lax.broadcasted_iota(jnp.int32, u.shape, 0)
        w = w_ref[...]
        cv = w[0:1] * _shift_down(u, row, 2) + w[1:2] * _shift_down(u, row, 1) + w[2:3] * u
        z_ref[...] = (ab_ref[...] * cv).astype(z_ref.dtype)

    col = lambda off: pl.BlockSpec((s, 128), functools.partial(lambda j, off: (0, off + j), off=off))
    return pl.pallas_call(
        body, name="conv_fwd", grid=(nb,), in_specs=[col(0), col(nb), col(2 * nb), pl.BlockSpec((3, 128), lambda j: (0, j))],
        out_specs=pl.BlockSpec((s, 128), lambda j: (0, j)), out_shape=SDS((s, dc), bf16), compiler_params=_params(("parallel",)),
    )(proj, proj, proj, conv_w)


def _conv_bwd(proj, conv_w, dz, s, dc):
    nb = dc // 128

    def body(ab_ref, ac_ref, ax_ref, w_ref, dz_ref, dab_ref, dac_ref, dax_ref, dw_ref):
        ab, ac, ax, dzv = ab_ref[...], ac_ref[...], ax_ref[...], dz_ref[...].astype(f32)
        u = ac * ax
        row = lax.broadcasted_iota(jnp.int32, u.shape, 0)
        w = w_ref[...]
        u1, u2 = _shift_down(u, row, 1), _shift_down(u, row, 2)
        cv = w[0:1] * u2 + w[1:2] * u1 + w[2:3] * u
        dcv = dzv * ab
        dab_ref[...] = (dzv * cv).astype(dab_ref.dtype)
        du = w[2:3] * dcv + w[1:2] * _shift_up(dcv, row, 1) + w[0:1] * _shift_up(dcv, row, 2)
        dac_ref[...] = (du * ax).astype(dac_ref.dtype)
        dax_ref[...] = (du * ac).astype(dax_ref.dtype)
        dw_ref[0:1, :] = _colsum(dcv * u2)
        dw_ref[1:2, :] = _colsum(dcv * u1)
        dw_ref[2:3, :] = _colsum(dcv * u)

    col = lambda off: pl.BlockSpec((s, 128), functools.partial(lambda j, off: (0, off + j), off=off))
    blk = pl.BlockSpec((s, 128), lambda j: (0, j))
    return pl.pallas_call(
        body, name="conv_bwd", grid=(nb,),
        in_specs=[col(0), col(nb), col(2 * nb), pl.BlockSpec((3, 128), lambda j: (0, j)), blk],
        out_specs=[blk, blk, blk, pl.BlockSpec((3, 128), lambda j: (0, j))],
        out_shape=[SDS((s, dc), bf16)] * 3 + [SDS((3, dc), f32)], compiler_params=_params(("parallel",)),
    )(proj, proj, proj, conv_w, dz)


def _level_masks():
    t = np.arange(CHUNK)[:, None]
    s = np.arange(CHUNK)[None, :]
    m = np.stack([((t & h) != 0) & ((s & h) == 0) & (t // (2 * h) == s // (2 * h)) for h in LEVELS]).astype(np.float32)
    return jnp.asarray(m), jnp.asarray(m.transpose(0, 2, 1))


def _cumsum_rows(x, row):
    for sh in (1, 2, 4, 8, 16, 32):
        x = x + jnp.where(row >= sh, pltpu.roll(x, sh, 0), 0.0)
    return x


def _rev_cumsum_rows(x, row):
    n = x.shape[0]
    for sh in (1, 2, 4, 8, 16, 32):
        x = x + jnp.where(row < n - sh, pltpu.roll(x, n - sh, 0), 0.0)
    return x


def _chunk_terms(qp, fl, lb):
    row = lax.broadcasted_iota(jnp.int32, qp.shape, 0)
    sig = _sigmoid(fl)
    f = lb + (1.0 - lb) * sig
    k = 1.0 - f
    sq = _sigmoid(qp)
    qh = qp * sq
    b = _cumsum_rows(jnp.log(f), row)
    sub = lax.broadcasted_iota(jnp.int32, (CHUNK // 8, 8, DK), 1)
    b8 = b.reshape(CHUNK // 8, 8, DK)
    us, exs, ups = [], [], []
    for m in LEVELS:
        sb = 2 * m
        if sb >= 8:
            b3 = b.reshape(CHUNK // sb, sb, DK)
            bref = jnp.broadcast_to(b3[:, m - 1:m, :], b3.shape).reshape(CHUNK, DK)
        else:
            bref8 = None
            for j in range(8 // sb):
                cand = jnp.broadcast_to(b8[:, j * sb + m - 1:j * sb + m, :], b8.shape)
                bref8 = cand if bref8 is None else jnp.where(sub >= j * sb, cand, bref8)
            bref = bref8.reshape(CHUNK, DK)
        up = (row & m) != 0
        ex = jnp.exp(jnp.where(up, b - bref, bref - b))
        us.append((jnp.where(up, qh, k) * ex).astype(bf16))
        exs.append(ex)
        ups.append(up)
    blast = b[CHUNK - 1:CHUNK, :]
    eb, ebl = jnp.exp(b), jnp.exp(blast - b)
    return dict(sig=sig, f=f, k=k, sq=sq, qh=qh, u=jnp.stack(us), ex=exs, up=ups, eb=eb, ebl=ebl, qt=qh * eb, kt=k * ebl,
                el=jnp.exp(blast), row=row)


def _scores(t, mask):
    pl_ = jnp.einsum("ltk,lsk->lts", t["u"], t["u"], preferred_element_type=f32)
    p = jnp.sum(pl_ * mask, axis=0)
    r = lax.broadcasted_iota(jnp.int32, (CHUNK, CHUNK), 0)
    c = lax.broadcasted_iota(jnp.int32, (CHUNK, CHUNK), 1)
    diag = jnp.sum(t["qh"] * t["k"], axis=-1, keepdims=True)
    return p + jnp.where(r == c, diag, 0.0)


def _hgrn_fwd(proj, lb_param, gnorm, s, dv_total, tb):
    nchunk = tb // CHUNK
    masks, _ = _level_masks()
    q0, f0, v0, g0 = 3 * HEADS, 4 * HEADS, 5 * HEADS, 6 * HEADS

    def body(q_ref, f_ref, v_ref, g_ref, lb_ref, gn_ref, mask_ref, og_ref, o_ref, st_ref, state):
        @pl.when(pl.program_id(1) == 0)
        def _():
            state[...] = jnp.zeros_like(state)

        lbp = lb_ref[...]
        lb_all = _sigmoid(lbp[0:1, :] - lbp[1:2, :])
        mask = mask_ref[...]
        for i, hh in [(i, hh) for i in range(nchunk) for hh in range(HEADS_PER_STEP)]:
            rs, cs = pl.ds(i * CHUNK, CHUNK), pl.ds(hh * DK, DK)
            t = _chunk_terms(q_ref[rs, cs], f_ref[rs, cs], lb_all[:, hh * DK:(hh + 1) * DK])
            v = v_ref[rs, cs]
            vb = v.astype(bf16)
            st = state[hh]
            st_ref[i, hh] = st
            p = _scores(t, mask)
            o = jnp.dot(p.astype(bf16), vb, preferred_element_type=f32)
            o += lax.dot_general(t["qt"].astype(bf16), st.astype(bf16), (NT, ((), ())), preferred_element_type=f32)
            state[hh] = st * t["el"] + lax.dot_general(vb, t["kt"].astype(bf16), (TN, ((), ())), preferred_element_type=f32)
            o_ref[rs, cs] = o
            r, oh = _rms(o)
            g = g_ref[rs, cs]
            og_ref[rs, cs] = (oh * gn_ref[...] * (g * _sigmoid(g))).astype(og_ref.dtype)

    hp, wide = HEADS_PER_STEP, HEADS_PER_STEP * DK
    col = lambda off: pl.BlockSpec((tb, wide), functools.partial(lambda h, t, off: (t, off + h), off=off // hp))
    blk = pl.BlockSpec((tb, wide), lambda h, t: (t, h))
    return pl.pallas_call(
        body, name="hgrn_fwd", grid=(HEADS // hp, s // tb),
        in_specs=[col(q0), col(f0), col(v0), col(g0), pl.BlockSpec((2, wide), lambda h, t: (0, h)),
                  pl.BlockSpec((1, DK), lambda h, t: (0, 0)), pl.BlockSpec(masks.shape, lambda h, t: (0, 0, 0))],
        out_specs=[blk, blk, pl.BlockSpec((nchunk, hp, DK, DK), lambda h, t: (t, h, 0, 0))],
        out_shape=[SDS((s, dv_total), bf16), SDS((s, dv_total), f32), SDS((s // CHUNK, HEADS, DK, DK), f32)],
        scratch_shapes=[pltpu.VMEM((hp, DK, DK), f32)], compiler_params=_params(("parallel", "arbitrary")),
    )(proj, proj, proj, proj, lb_param, gnorm, masks)


def _hgrn_bwd(proj, lb_param, gnorm, o_saved, states, dog, s, dv_total, tb):
    nchunk = tb // CHUNK
    nt = s // tb
    nc_total = s // CHUNK
    masks, masks_t = _level_masks()
    q0, f0, v0, g0 = 3 * HEADS, 4 * HEADS, 5 * HEADS, 6 * HEADS

    def body(q_ref, f_ref, v_ref, g_ref, lb_ref, gn_ref, mask_ref, maskt_ref, o_ref, dog_ref, st_ref, stn_ref,
             dq_ref, df_ref, dv_ref, dg_ref, dlb_ref, dgn_ref, gstate):
        h_id, t_id = pl.program_id(0), pl.program_id(1)

        @pl.when(t_id == 0)
        def _():
            gstate[...] = jnp.zeros_like(gstate)
            dlb_ref[...] = jnp.zeros_like(dlb_ref)

        @pl.when((t_id == 0) & (h_id == 0))
        def _():
            dgn_ref[...] = jnp.zeros_like(dgn_ref)

        lbp = lb_ref[...]
        lb_all = _sigmoid(lbp[0:1, :] - lbp[1:2, :])
        mask, maskt = mask_ref[...], maskt_ref[...]
        gn = gn_ref[...]
        for i, hh in [(i, hh) for i in reversed(range(nchunk)) for hh in range(HEADS_PER_STEP)]:
            rs, cs = pl.ds(i * CHUNK, CHUNK), pl.ds(hh * DK, DK)
            lb = lb_all[:, hh * DK:(hh + 1) * DK]
            qp, fl, v, g = q_ref[rs, cs], f_ref[rs, cs], v_ref[rs, cs], g_ref[rs, cs]
            t = _chunk_terms(qp, fl, lb)
            vb = v.astype(bf16)
            st0 = st_ref[i, hh]
            st1 = st_ref[i + 1, hh] if i + 1 < nchunk else stn_ref[0, hh]
            gt = gstate[hh]
            o = o_ref[rs, cs]
            r, oh = _rms(o)
            sg = _sigmoid(g)
            dog_v = dog_ref[rs, cs].astype(f32)
            dg_ref[rs, cs] = (dog_v * (oh * gn) * _dsilu(g, sg)).astype(dg_ref.dtype)
            don = dog_v * (g * sg)
            dgn_ref[...] += _colsum(don * oh)
            doh = don * gn
            do = r * (doh - oh * jnp.mean(doh * oh, axis=-1, keepdims=True))
            dob = do.astype(bf16)
            d = lax.dot_general(dob, vb, (NT, ((), ())), preferred_element_type=f32)
            dt = lax.dot_general(vb, dob, (NT, ((), ())), preferred_element_type=f32)
            z = (mask * d[None] + maskt * dt[None]).astype(bf16)
            rr = jnp.einsum("lts,lsk->ltk", z, t["u"], preferred_element_type=f32)
            dq = jnp.zeros((CHUNK, DK), f32)
            dk = jnp.zeros((CHUNK, DK), f32)
            qdk = jnp.zeros((CHUNK, DK), f32)
            for li in range(len(LEVELS)):
                du = t["ex"][li] * rr[li]
                dq += jnp.where(t["up"][li], du, 0.0)
                dk += jnp.where(t["up"][li], 0.0, du)
                e = t["u"][li].astype(f32) * rr[li]
                qdk += jnp.where(t["up"][li], e, -e)
            dd = jnp.sum(do * v, axis=-1, keepdims=True)
            dq += dd * t["k"]
            dk += dd * t["qh"]
            gtb = gt.astype(bf16)
            ktb, qtb = t["kt"].astype(bf16), t["qt"].astype(bf16)
            dq_in = jnp.dot(dob, st0.astype(bf16), preferred_element_type=f32)
            dk_in = jnp.dot(vb, gtb, preferred_element_type=f32)
            dq += t["eb"] * dq_in
            dk += t["ebl"] * dk_in
            qdk += qtb.astype(f32) * dq_in - ktb.astype(f32) * dk_in
            p = _scores(t, mask)
            dvv = lax.dot_general(p.astype(bf16), dob, (TN, ((), ())), preferred_element_type=f32)
            dvv += lax.dot_general(ktb, gtb, (NT, ((), ())), preferred_element_type=f32)
            dv_ref[rs, cs] = dvv.astype(dv_ref.dtype)
            a_end = _colsum(gtb.astype(f32) * st1)
            dlf = _rev_cumsum_rows(qdk, t["row"]) + a_end
            dfv = dlf / t["f"] - dk
            df_ref[rs, cs] = (dfv * (1.0 - lb) * t["sig"] * (1.0 - t["sig"])).astype(df_ref.dtype)
            dlb_ref[:, cs] += _colsum(dfv * (1.0 - t["sig"]))
            dq_ref[rs, cs] = (dq * _dsilu(qp, t["sq"])).astype(dq_ref.dtype)
            gstate[hh] = gt * t["el"] + lax.dot_general(dob, qtb, (TN, ((), ())), preferred_element_type=f32)

    hp, wide = HEADS_PER_STEP, HEADS_PER_STEP * DK
    rev = lambda t: nt - 1 - t
    col = lambda off: pl.BlockSpec((tb, wide), functools.partial(lambda h, t, off: (rev(t), off + h), off=off // hp))
    blk = pl.BlockSpec((tb, wide), lambda h, t: (rev(t), h))
    nxt = lambda h, t: (jnp.minimum((rev(t) + 1) * nchunk, nc_total - 1), h, 0, 0)
    return pl.pallas_call(
        body, name="hgrn_bwd", grid=(HEADS // hp, nt),
        in_specs=[col(q0), col(f0), col(v0), col(g0), pl.BlockSpec((2, wide), lambda h, t: (0, h)),
                  pl.BlockSpec((1, DK), lambda h, t: (0, 0)), pl.BlockSpec(masks.shape, lambda h, t: (0, 0, 0)),
                  pl.BlockSpec(masks.shape, lambda h, t: (0, 0, 0)), blk, blk,
                  pl.BlockSpec((nchunk, hp, DK, DK), lambda h, t: (rev(t), h, 0, 0)),
                  pl.BlockSpec((1, hp, DK, DK), nxt)],
        out_specs=[blk, blk, blk, blk, pl.BlockSpec((1, wide), lambda h, t: (0, h)), pl.BlockSpec((1, DK), lambda h, t: (0, 0))],
        out_shape=[SDS((s, dv_total), bf16)] * 4 + [SDS((1, HEADS * DK), f32), SDS((1, DK), f32)],
        scratch_shapes=[pltpu.VMEM((hp, DK, DK), f32)], compiler_params=_params(("arbitrary", "arbitrary")),
    )(proj, proj, proj, proj, lb_param, gnorm, masks, masks_t, o_saved, dog, states, states)


def _adamw(name, g, w, m, v):
    r, c = w.shape
    tr = r
    for cand in (256, 128, 64, 32, 16, 8):
        if r % cand == 0 and r > cand:
            tr = cand
            break

    def body(g_ref, w_ref, m_ref, v_ref, d_ref, mo_ref, vo_ref):
        d_ref[...], mo_ref[...], vo_ref[...] = _adamw_math(g_ref[...], w_ref[...], m_ref[...], v_ref[...])

    blk = pl.BlockSpec((tr, c), lambda i: (i, 0))
    return pl.pallas_call(
        body, name=name, grid=(r // tr,), in_specs=[blk] * 4, out_specs=[blk] * 3, out_shape=[SDS((r, c), f32)] * 3,
        compiler_params=_params(("parallel",)),
    )(g, w, m, v)


def _ffn_in(h2, w_gt, w_ut, tm, ffb):
    s, d = h2.shape
    dff = w_gt.shape[0]

    def body(a_ref, wg_ref, wu_ref, dg_ref, du_ref, act_ref):
        a = a_ref[...]
        g = lax.dot_general(a, wg_ref[...], (NT, ((), ())), preferred_element_type=f32)
        u = lax.dot_general(a, wu_ref[...], (NT, ((), ())), preferred_element_type=f32)
        sg = _sigmoid(g)
        silu = g * sg
        dg_ref[...] = (u * _dsilu(g, sg)).astype(bf16)
        du_ref[...] = silu.astype(bf16)
        act_ref[...] = (silu * u).astype(bf16)

    w_spec = pl.BlockSpec((ffb, d), lambda j, i: (j, 0))
    o_spec = pl.BlockSpec((tm, ffb), lambda j, i: (i, j))
    return pl.pallas_call(
        body, name="ffn_in", grid=(dff // ffb, s // tm), in_specs=[pl.BlockSpec((tm, d), lambda j, i: (i, 0)), w_spec, w_spec],
        out_specs=[o_spec] * 3, out_shape=[SDS((s, dff), bf16)] * 3, compiler_params=_params(("parallel", "parallel")),
    )(h2, w_gt, w_ut)


def _ffn_down_bwd(dff_out, w_d, act_dg, act_du, tm, ffb):
    s, d = dff_out.shape
    dff = w_d.shape[0]

    def body(a_ref, w_ref, fg_ref, fu_ref, dg_ref, du_ref):
        da = lax.dot_general(a_ref[...], w_ref[...], (NT, ((), ())), preferred_element_type=f32)
        dg_ref[...] = (da * fg_ref[...].astype(f32)).astype(bf16)
        du_ref[...] = (da * fu_ref[...].astype(f32)).astype(bf16)

    t_spec = pl.BlockSpec((tm, ffb), lambda j, i: (i, j))
    return pl.pallas_call(
        body, name="d_ffn_down_in", grid=(dff // ffb, s // tm),
        in_specs=[pl.BlockSpec((tm, d), lambda j, i: (i, 0)), pl.BlockSpec((ffb, d), lambda j, i: (j, 0)), t_spec, t_spec],
        out_specs=[t_spec] * 2, out_shape=[SDS((s, dff), bf16)] * 2, compiler_params=_params(("parallel", "parallel")),
    )(dff_out, w_d, act_dg, act_du)


def _branch_merge(z_a, og, w_co, w_ho, proj, tm, gate_a0, gate_b0):
    s, dc = z_a.shape
    nsh, _, n = w_co.shape

    def body(za_ref, og_ref, wa_ref, wb_ref, ga_ref, gb_ref, m_ref, sa_ref, sb_ref, fa_ref, fb_ref):
        ya = jnp.dot(za_ref[...], wa_ref[...], preferred_element_type=f32)
        yb = jnp.dot(og_ref[...], wb_ref[...], preferred_element_type=f32)
        sa, sb_ = _sigmoid(ga_ref[...]), _sigmoid(gb_ref[...])
        m_ref[...] = (sa * ya + sb_ * yb).astype(bf16)
        sa_ref[...] = sa.astype(bf16)
        sb_ref[...] = sb_.astype(bf16)
        fa_ref[...] = (ya * sa * (1.0 - sa)).astype(bf16)
        fb_ref[...] = (yb * sb_ * (1.0 - sb_)).astype(bf16)

    a_spec = pl.BlockSpec((tm, dc), lambda i, j: (i, 0))
    w_spec = pl.BlockSpec((None, dc, n), lambda i, j: (j, 0, 0))
    gate = lambda c0: pl.BlockSpec((tm, n), functools.partial(lambda i, j, cb: (i, cb + j), cb=c0 // n))
    o_spec = pl.BlockSpec((tm, n), lambda i, j: (i, j))
    return pl.pallas_call(
        body, name="branch_merge", grid=(s // tm, nsh), in_specs=[a_spec, a_spec, w_spec, w_spec, gate(gate_a0), gate(gate_b0)],
        out_specs=[o_spec] * 5, out_shape=[SDS((s, nsh * n), bf16)] * 5, compiler_params=_params(("parallel", "parallel")),
    )(z_a, og, w_co, w_ho, proj, proj)


def _d_branch_merge(dmo, w_o, factors, tm):
    s, d = dmo.shape
    n = d // 2

    def body(a_ref, w_ref, sa_ref, sb_ref, fa_ref, fb_ref, dya_ref, dyb_ref, dga_ref, dgb_ref):
        dm = lax.dot_general(a_ref[...], w_ref[...], (NT, ((), ())), preferred_element_type=f32)
        for f_ref, o_ref in ((sa_ref, dya_ref), (sb_ref, dyb_ref), (fa_ref, dga_ref), (fb_ref, dgb_ref)):
            o_ref[...] = (dm * f_ref[...].astype(f32)).astype(bf16)

    t_spec = pl.BlockSpec((tm, n), lambda j, i: (i, j))
    return pl.pallas_call(
        body, name="d_branch_merge", grid=(d // n, s // tm),
        in_specs=[pl.BlockSpec((tm, d), lambda j, i: (i, 0)), pl.BlockSpec((n, d), lambda j, i: (j, 0))] + [t_spec] * 4,
        out_specs=[t_spec] * 4, out_shape=[SDS((s, d), bf16)] * 4, compiler_params=_params(("parallel", "parallel")),
    )(dmo, w_o, *factors)


def _local_step(x, tgt, mod, g_mix, g_ffn, g_fin, lb_param, gnorm, conv_w, project, get_w, prefetch, put_g, sent):
    s, d = x.shape
    dc = d // 2
    tm = min(512, s)
    tm2 = min(1024, s)
    te = min(256, s)
    tb = min(128, s)
    mt = s // tm
    nb = 512
    sh_m, sc_m, gt_m, sh_f, sc_f, gt_f = [mod[i] for i in range(N_MOD)]
    dh2 = d // 2

    def e1(rows, fulls, outs, accs):
        xv = rows[0][...]
        g, sc, sh = [r[...] for r in fulls]
        _, xh = _rms(xv)
        outs[0][...] = (xh * g * (1.0 + sc) + sh).astype(bf16)

    h, = _rowwise("prenorm_mix", e1, s, te, [(x, d, 0)], [g_mix, sc_m, sh_m], [(d, bf16)])
    proj, w_in = project(h, tm2)
    nsh, _, win_sh = w_in.shape
    z_a = _conv_fwd(proj, _after_tokens(conv_w, [prefetch("mix", proj)]), s, dc)
    og, o_saved, states = _hgrn_fwd(proj, lb_param, gnorm, s, dc, tb)
    w_co, w_ho, w_o = get_w("mix", og)

    gate_a0, gate_b0 = 7 * dh2, 9 * dh2
    merged, *merge_factors = _branch_merge(z_a, og, w_co, w_ho, proj, tm2, gate_a0, gate_b0)
    mo = _matmul("mix_out", merged, w_o, (s, d), bf16, (2, s // tm2, 1), pl.BlockSpec((tm2, d), lambda j, i, k: (i, 0)),
                 pl.BlockSpec((d, dh2), lambda j, i, k: (0, j)), pl.BlockSpec((tm2, dh2), lambda j, i, k: (i, j)), NN,
                 after=prefetch("ffn", merged))

    def e6(rows, fulls, outs, accs):
        xv, mov = rows[0][...], rows[1][...].astype(f32)
        gt, g, sc, sh = [r[...] for r in fulls]
        x1 = xv + gt * mov
        outs[0][...] = x1
        _, xh = _rms(x1)
        outs[1][...] = (xh * g * (1.0 + sc) + sh).astype(bf16)

    gt_m_late = _after_tokens(gt_m, [prefetch("ffn_down", mo)])
    x1, h2 = _rowwise("prenorm_ffn", e6, s, tm, [(x, d, 0), (mo, d, 0)], [gt_m_late, g_ffn, sc_f, sh_f], [(d, f32), (d, bf16)])
    w_gt, w_ut = get_w("ffn", h2)
    dff_ = w_gt.shape[0]
    act_dg, act_du, act = _ffn_in(h2, w_gt, w_ut, tm2, nb)
    w_d, = get_w("ffn_down", act)
    ff = _matmul("ffn_down", act, w_d, (s, d), bf16, (2, mt, 1), pl.BlockSpec((tm, dff_), lambda j, i, k: (i, 0)),
                 pl.BlockSpec((dff_, dh2), lambda j, i, k: (0, j)), pl.BlockSpec((tm, dh2), lambda j, i, k: (i, j)), NN)

    def e9(rows, fulls, outs, accs):
        x1v, ffv, tv = [r[...].astype(f32) for r in rows]
        gt, gf = fulls[0][...], fulls[1][...]
        x2 = x1v + gt * ffv
        r, xh = _rms(x2)
        err = xh * gf - tv
        accs[0][...] += 0.5 * jnp.sum(jnp.mean(err * err, axis=-1, keepdims=True), axis=0, keepdims=True)
        dy = err / d
        accs[1][...] += _colsum(dy * xh)
        dxh = dy * gf
        dx2 = r * (dxh - xh * jnp.mean(dxh * xh, axis=-1, keepdims=True))
        outs[0][...] = dx2
        outs[1][...] = (dx2 * gt).astype(bf16)
        accs[2][...] += _colsum(dx2 * ffv)

    dx2, dff, loss_acc, dg_fin, dgt_f = _rowwise("loss_head", e9, s, tm, [(x1, d, 0), (ff, d, 0), (tgt, d, 0)], [gt_f, g_fin],
                                                 [(d, f32), (d, bf16)], [(1, 128), (1, d), (1, d)])
    dgg, duu = _ffn_down_bwd(dff, w_d, act_dg, act_du, tm, dff_ // 4)

    def wgrad_rows(name, a, b, n_out):
        kb = 512
        return _matmul(name, a, b, (n_out, d), bf16, (n_out // kb, 2, 1), pl.BlockSpec((s, kb), lambda i, j, k: (0, i)),
                       pl.BlockSpec((s, dh2), lambda i, j, k: (0, j)), pl.BlockSpec((kb, dh2), lambda i, j, k: (i, j)), TN)

    gw_d = wgrad_rows("gw_ffn_down", act, dff, dff_)

    def ffn_in_bwd(name, a, w, after=None):
        return _matmul(name, a, w, (s, d), bf16, (2, mt, 1), pl.BlockSpec((tm, dff_), lambda j, i, k: (i, 0)),
                       pl.BlockSpec((dff_, dh2), lambda j, i, k: (0, j)), pl.BlockSpec((tm, dh2), lambda j, i, k: (i, j)), NN,
                       after=after)

    dh2b = ffn_in_bwd("d_ffn_up_in", duu, w_ut)
    gw_ut = wgrad_rows("gw_ffn_up", duu, h2, dff_)
    gw_gt = wgrad_rows("gw_ffn_gate", dgg, h2, dff_)
    dh2a = ffn_in_bwd("d_ffn_gate_in", dgg, w_gt, after=put_g("ffn", [gw_gt, gw_ut, gw_d]))
    sc_f_late = _after_tokens(sc_f, [sent("ffn", dh2a)])

    def b5(rows, fulls, outs, accs):
        da, db, x1v, dx2v, mov = [r[...].astype(f32) for r in rows]
        sc, g, gt = [r[...] for r in fulls]
        dh = da + db
        r, xh = _rms(x1v)
        accs[0][...] += _colsum(dh)
        accs[1][...] += _colsum(dh * (xh * g))
        dn = dh * (1.0 + sc)
        accs[2][...] += _colsum(dn * xh)
        dxh = dn * g
        dx1 = dx2v + r * (dxh - xh * jnp.mean(dxh * xh, axis=-1, keepdims=True))
        outs[0][...] = dx1
        accs[3][...] += _colsum(dx1 * mov)
        outs[1][...] = (dx1 * gt).astype(bf16)

    dx1, dmo, dsh_f, dsc_f, dg_ffn, dgt_m = _rowwise(
        "d_prenorm_ffn", b5, s, te, [(dh2a, d, 0), (dh2b, d, 0), (x1, d, 0), (dx2, d, 0), (mo, d, 0)], [sc_f_late, g_ffn, gt_m],
        [(d, f32), (d, bf16)], [(1, d)] * 4)
    dya, dyb, dga, dgb = _d_branch_merge(dmo, w_o, merge_factors, tm)
    gw_o = wgrad_rows("gw_mix_out", merged, dmo, d)

    def out_proj_bwd(name, dy, w):
        return _matmul(name, dy, w, (s, dc), bf16, (1, s // tm2, nsh), pl.BlockSpec((tm2, d // nsh), lambda j, i, k: (i, k)),
                       pl.BlockSpec((None, dc, d // nsh), lambda j, i, k: (k, 0, 0)), pl.BlockSpec((tm2, dc), lambda j, i, k: (i, 0)), NT)

    def out_proj_wgrad(name, a, dy):
        return _matmul(name, a, dy, (nsh, dc, d // nsh), bf16, (1, nsh, 1), pl.BlockSpec((s, dc), lambda i, j, k: (0, 0)),
                       pl.BlockSpec((s, d // nsh), lambda i, j, k: (0, j)), pl.BlockSpec((None, dc, d // nsh), lambda i, j, k: (j, 0, 0)), TN)

    dz_a = out_proj_bwd("d_conv_out_in", dya, w_co)
    dog = out_proj_bwd("d_hgrn_out_in", dyb, w_ho)
    gw_co = out_proj_wgrad("gw_conv_out", z_a, dya)
    gw_ho = out_proj_wgrad("gw_hgrn_out", og, dyb)
    conv_w_late = _after_tokens(conv_w, [put_g("mix", [gw_co, gw_ho, gw_o])])
    dab, dac, dax, dconv_w = _conv_bwd(proj, conv_w_late, dz_a, s, dc)
    lb_param_late = _after_tokens(lb_param, [sent("mix", dab)])
    dq, dfl, dvi, dgo, dlb, dgn = _hgrn_bwd(proj, lb_param_late, gnorm, o_saved, states, dog, s, dc, tb)
    dproj = jnp.concatenate([dab, dac, dax, dq, dfl, dvi, dgo, dga, dgb], axis=1)
    gw_in = _matmul("gw_proj", h, dproj, (nsh, d, win_sh), bf16, (nsh, d // 512, 1), pl.BlockSpec((s, 512), lambda j, i, k: (0, i)),
                    pl.BlockSpec((s, win_sh), lambda j, i, k: (0, j)), pl.BlockSpec((None, 512, win_sh), lambda j, i, k: (j, i, 0)), TN)
    dh = _matmul("d_proj_in", dproj, w_in, (s, d), bf16, (1, s // tm2, nsh), pl.BlockSpec((tm2, win_sh), lambda j, i, k: (i, k)),
                 pl.BlockSpec((None, d, win_sh), lambda j, i, k: (k, 0, 0)), pl.BlockSpec((tm2, d), lambda j, i, k: (i, 0)), NT,
                 after=put_g("in", [gw_in]))

    def b12(rows, fulls, outs, accs):
        dhv, xv, dx1v = [r[...].astype(f32) for r in rows]
        sc, g = fulls[0][...], fulls[1][...]
        r, xh = _rms(xv)
        accs[0][...] += _colsum(dhv)
        accs[1][...] += _colsum(dhv * (xh * g))
        dn = dhv * (1.0 + sc)
        accs[2][...] += _colsum(dn * xh)
        dxh = dn * g
        outs[0][...] = dx1v + r * (dxh - xh * jnp.mean(dxh * xh, axis=-1, keepdims=True))

    dx, dsh_m, dsc_m, dg_mix = _rowwise("d_prenorm_mix", b12, s, tm, [(dh, d, 0), (x, d, 0), (dx1, d, 0)], [sc_m, g_mix],
                                        [(d, f32)], [(1, d)] * 3)
    dmod = [dsh_m, dsc_m, dgt_m, dsh_f, dsc_f, dgt_f]
    small = dict(loss=loss_acc, g_mix=dg_mix, g_ffn=dg_ffn, g_fin=dg_fin, lb=dlb, gnorm=dgn, conv_w=dconv_w)
    return dx, dmod, small


def _ada_fwd(c_all, w_sh, b_sh):
    def body(c_ref, w_ref, b_ref, o_ref):
        cv = c_ref[...]
        ca = (cv * _sigmoid(cv)).astype(bf16)
        o_ref[...] = jnp.dot(ca, w_ref[...].astype(bf16), preferred_element_type=f32) + b_ref[...]

    return pl.pallas_call(body, name="ada_fwd", out_shape=SDS((c_all.shape[0], w_sh.shape[1]), f32),
                          compiler_params=pltpu.CompilerParams(vmem_limit_bytes=V7X_VMEM_LIMIT))(c_all, w_sh, b_sh)


def _ada_wgrad(c_all, dmod_sh):
    def body(c_ref, d_ref, o_ref):
        cv = c_ref[...]
        ca = (cv * _sigmoid(cv)).astype(bf16)
        o_ref[...] = lax.dot_general(ca, d_ref[...].astype(bf16), (TN, ((), ())), preferred_element_type=f32)

    return pl.pallas_call(body, name="ada_wgrad", out_shape=SDS((c_all.shape[1], dmod_sh.shape[1]), f32),
                          compiler_params=pltpu.CompilerParams(vmem_limit_bytes=V7X_VMEM_LIMIT))(c_all, dmod_sh)


def _lb_grad(lb_param, dlb):
    def body(p_ref, d_ref, o_ref):
        p = p_ref[...]
        lb = _sigmoid(p[0:1, :] - p[1:2, :])
        gl = d_ref[...] * lb * (1.0 - lb)
        o_ref[0:1, :] = gl
        o_ref[1:2, :] = -gl

    return pl.pallas_call(body, name="lb_grad", out_shape=SDS(lb_param.shape, f32))(lb_param, dlb)


def _sum_small(gathered):
    def body(g_ref, o_ref):
        acc = g_ref[0]
        for dd in range(1, NDEV):
            acc = acc + g_ref[dd]
        o_ref[...] = acc

    return pl.pallas_call(body, name="sum_small", out_shape=SDS(gathered.shape[1:], f32))(gathered)


def kernel(x, c, w_ada, b_ada, norm_mix_g, w_in, conv_w, lb_param, gnorm_g, w_conv_out, w_hgrn_out, w_o, norm_ffn_g, w_ffn_gate, w_ffn_up, w_ffn_down, norm_final_g, loss_target, m_w_ada, m_b_ada, m_norm_mix_g, m_w_in, m_conv_w, m_lb_param, m_gnorm_g, m_w_conv_out, m_w_hgrn_out, m_w_o, m_norm_ffn_g, m_w_ffn_gate, m_w_ffn_up, m_w_ffn_down, m_norm_final_g, v_w_ada, v_b_ada, v_norm_mix_g, v_w_in, v_conv_w, v_lb_param, v_gnorm_g, v_w_conv_out, v_w_hgrn_out, v_w_o, v_norm_ffn_g, v_w_ffn_gate, v_w_ffn_up, v_w_ffn_down, v_norm_final_g):
    assert lb_param.shape[0] == 2 and w_ada.shape[0] == 1
    s, d = x.shape[1], x.shape[2]
    me = 4 * lax.axis_index("x") + 2 * lax.axis_index("y") + lax.axis_index("c")
    ada_cols = w_ada.shape[2]

    me1 = me.astype(jnp.int32).reshape(1)
    placed_in = [_place_shard("place_in0", me1, w_in[0])]
    c_all, cw_all = _all_gather("gather_cond", [c, conv_w[0]], after=placed_in)
    c_all = c_all.reshape(NDEV, d)
    conv_w_full = jnp.transpose(cw_all, (1, 0, 2)).reshape(conv_w.shape[1], -1)
    b_sh = lax.dynamic_slice_in_dim(b_ada, me * ada_cols, ada_cols, axis=1)
    mod_cols = _ada_fwd(c_all, w_ada[0], b_sh)
    mod_all, = _all_gather("gather_mod", [mod_cols])
    mod = lax.dynamic_index_in_dim(mod_all, me, axis=1, keepdims=False).reshape(N_MOD, 1, d)

    shard_groups = {"mix": [w_conv_out[0], w_hgrn_out[0], w_o[0]], "ffn": [w_ffn_gate[0].T, w_ffn_up[0].T],
                    "ffn_down": [w_ffn_down[0]]}
    own_slot = lambda frm, to: _flat(frm)
    gather_plan = lambda n: [(a, j, own_slot, own_slot) for a in range(n) for j in (1,) + ICI_RELATIONS]
    flat = lambda a: a.reshape(a.shape[0] * a.shape[1], a.shape[2])
    to8 = lambda a: a.reshape(NDEV, a.shape[0] // NDEV, a.shape[1])
    near, far = ICI_RELATIONS[:2], ICI_RELATIONS[2:]
    in_plan = lambda rels: [(0, j, own_slot, own_slot) for j in rels]
    ss_a, rs_a, _, lands, tok_a = _push_start("gather_start_in_sib", [], placed_in, in_plan((1,)), after=mod_all)
    ss_b, rs_b, _, lands, tok_b = _push_start("gather_start_in_near", [], lands, in_plan(near), after=tok_a)
    ss_c, rs_c, _, lands_in, tok_c = _push_start("gather_start_in_far", [], lands, in_plan(far), after=tok_b)
    gathering, tokens = {}, [tok_a, tok_b, tok_c]
    for grp, sh in shard_groups.items():
        lands = [_place_shard(f"place_{grp}{i}", me1, a, after=tokens[-1]) for i, a in enumerate(sh)]
        ss, rs, _, lands, tok = _push_start("gather_start_" + grp, [], lands, gather_plan(len(lands)), after=tokens[-1])
        gathering[grp] = (ss, rs, lands)
        tokens.append(tok)

    pos = (lax.axis_index("x"), lax.axis_index("y"), lax.axis_index("c"))
    ids = lambda frm, rels: jnp.stack([_flat(_peer(frm, j)) for j in rels]).astype(jnp.int32)

    def project(h, tm):
        sib = _peer(pos, 1)
        fwd_of = lambda idxs: [(0, 1, fwd_slot(i), fwd_slot(i)) for i in idxs]
        _, lands = _push_wait("gather_wait_in_sib", ss_a, rs_a, [], lands_in, in_plan((1,)), h)
        proj = _proj_part("proj_local", h, lands[0], ids(pos, (0, 1)), None, tm)
        _, lands = _push_wait("gather_wait_in_near", ss_b, rs_b, [], lands, in_plan(near), proj)
        ss1, rs1, _, lands, tok = _push_start("gather_fwd_in_near", [], lands, fwd_of((0, 1)))
        proj = _proj_part("proj_near", h, lands[0], ids(pos, near), proj, tm, after=tok)
        _, lands = _push_wait("gather_wait_in_far", ss_c, rs_c, [], lands, in_plan(far), proj)
        ss2, rs2, _, lands, tok = _push_start("gather_fwd_in_far", [], lands, fwd_of((2,)))
        proj = _proj_part("proj_far", h, lands[0], ids(pos, far), proj, tm, after=tok)
        _, lands = _push_wait("gather_fwd_wait_in_near", ss1, rs1, [], lands, fwd_of((0, 1)), proj)
        proj = _proj_part("proj_fwd_near", h, lands[0], ids(sib, near), proj, tm)
        _, lands = _push_wait("gather_fwd_wait_in_far", ss2, rs2, [], lands, fwd_of((2,)), proj)
        proj = _proj_part("proj_fwd_far", h, lands[0], ids(sib, far), proj, tm)
        return proj, lands[0]

    fwd_slot = lambda i: (lambda frm, to: _flat(_peer(frm, ICI_RELATIONS[i])))
    fwd_plan = lambda n: [(a, 1, fwd_slot(i), fwd_slot(i)) for a in range(n) for i in range(len(ICI_RELATIONS))]
    forwarding = {}

    def prefetch(grp, after):
        ss, rs, lands = gathering[grp]
        _, lands = _push_wait("gather_wait_" + grp, ss, rs, [], lands, gather_plan(len(lands)), after)
        ss, rs, _, lands, tok = _push_start("gather_fwd_" + grp, [], lands, fwd_plan(len(lands)))
        forwarding[grp] = (ss, rs, lands)
        return tok

    def get_w(grp, after):
        ss, rs, lands = forwarding[grp]
        _, full = _push_wait("gather_fwd_wait_" + grp, ss, rs, [], lands, fwd_plan(len(lands)), after)
        return [f if i < 2 and grp == "mix" else flat(f) for i, f in enumerate(full)]

    core = lax.axis_index("c").astype(jnp.int32).reshape(1)
    chip = (2 * lax.axis_index("x") + lax.axis_index("y")).astype(jnp.int32).reshape(1)
    scatter_plan = lambda n: [(a, j, lambda frm, to: _chip(to), lambda frm, to: _chip(frm)) for a in range(n) for j in ICI_RELATIONS]
    scattering = {}

    swap_plan = lambda n: [(a, 1, functools.partial(lambda frm, to, q: 2 * q + to[2], q=q), functools.partial(lambda frm, to, q: q, q=q))
                           for a in range(n) for q in range(NDEV // 2)]
    swapping = {}

    def start_ici(grp, g8, recv):
        pairs = [_pair_sum(f"pair_sum_{grp}{i}", core, g, r, _row_tile(g.shape[1], 1024)) for i, (g, r) in enumerate(zip(g8, recv))]
        lands = [lax.empty(p.shape, p.dtype) for p in pairs]
        ss, rs, srcs, lands, tok = _push_start("scatter_start_" + grp, pairs, lands, scatter_plan(len(pairs)))
        scattering[grp] = (ss, rs, srcs, lands)
        return tok

    def put_g(grp, grads):
        g8 = [g if g.ndim == 3 else to8(g) for g in grads]
        if grp == "in":
            return start_ici(grp, g8, _sibling_swap("scatter_pair_" + grp, g8))
        lands = [lax.empty((NDEV // 2,) + g.shape[1:], g.dtype) for g in g8]
        ss, rs, srcs, lands, tok = _push_start("scatter_swap_" + grp, g8, lands, swap_plan(len(g8)))
        swapping[grp] = (ss, rs, srcs, lands)
        return tok

    def sent(grp, after):
        ss, rs, srcs, lands = swapping[grp]
        g8, recv = _push_wait("scatter_swapped_" + grp, ss, rs, srcs, lands, swap_plan(len(srcs)), after)
        return start_ici(grp, g8, recv)

    def reduced(grp, after, names):
        ss, rs, srcs, lands = scattering[grp]
        srcs, lands = _push_wait("scatter_wait_" + grp, ss, rs, srcs, lands, scatter_plan(len(srcs)), after)
        for i, (p, r, nm) in enumerate(zip(srcs, lands, names)):
            tr = nm in ("w_ffn_gate", "w_ffn_up")
            wmv = tuple(a[0].T if tr else a[0] for a in weights[nm])
            out = _chip_sum_adamw(f"chip_sum_{grp}{i}", chip, p, r, _row_tile(p.shape[1]), wmv)
            res[nm] = [(a.T if tr else a).reshape(weights[nm][0].shape) for a in out]

    dx, dmod, small = _local_step(x[0], loss_target[0], mod, _after_tokens(norm_mix_g, tokens), norm_ffn_g,
                                  norm_final_g.reshape(1, d), lb_param, gnorm_g, conv_w_full, project, get_w, prefetch, put_g, sent)

    pieces = [*dmod, small["g_mix"], small["g_ffn"], small["g_fin"], small["lb"], small["gnorm"], small["loss"],
              small["conv_w"].reshape(1, -1)]
    widths = [p.shape[1] for p in pieces]
    offs = np.concatenate([[0], np.cumsum(widths)])
    packed = jnp.concatenate(pieces, axis=1)
    gathered, = _all_gather("gather_small", [packed])
    summed = _sum_small(gathered)
    part = lambda i: summed[:, offs[i]:offs[i + 1]]
    g_b_ada = summed[:, :N_MOD * d]
    g_norm_mix, g_norm_ffn, g_norm_fin, g_lb_row, g_gnorm, loss_vec, g_convw_flat = [part(i) for i in range(N_MOD, N_MOD + 7)]
    loss = loss_vec[0, 0]
    dmod_all = gathered[:, 0, :N_MOD * d]
    g_w_ada = _ada_wgrad(c_all, lax.dynamic_slice_in_dim(dmod_all, me * ada_cols, ada_cols, axis=1))
    g_lb = _lb_grad(lb_param, g_lb_row)
    cw_cols = conv_w.shape[2]
    g_conv_w = lax.dynamic_slice_in_dim(g_convw_flat.reshape(conv_w.shape[1], -1), me * cw_cols, cw_cols, axis=1)

    grads = dict(w_ada=g_w_ada, b_ada=g_b_ada, norm_mix_g=g_norm_mix, conv_w=g_conv_w, lb_param=g_lb, gnorm_g=g_gnorm,
                 norm_ffn_g=g_norm_ffn, norm_final_g=g_norm_fin)
    weights = dict(w_ada=(w_ada, m_w_ada, v_w_ada), b_ada=(b_ada, m_b_ada, v_b_ada), norm_mix_g=(norm_mix_g, m_norm_mix_g, v_norm_mix_g),
                   w_in=(w_in, m_w_in, v_w_in), conv_w=(conv_w, m_conv_w, v_conv_w), lb_param=(lb_param, m_lb_param, v_lb_param),
                   gnorm_g=(gnorm_g, m_gnorm_g, v_gnorm_g), w_conv_out=(w_conv_out, m_w_conv_out, v_w_conv_out),
                   w_hgrn_out=(w_hgrn_out, m_w_hgrn_out, v_w_hgrn_out), w_o=(w_o, m_w_o, v_w_o),
                   norm_ffn_g=(norm_ffn_g, m_norm_ffn_g, v_norm_ffn_g), w_ffn_gate=(w_ffn_gate, m_w_ffn_gate, v_w_ffn_gate),
                   w_ffn_up=(w_ffn_up, m_w_ffn_up, v_w_ffn_up), w_ffn_down=(w_ffn_down, m_w_ffn_down, v_w_ffn_down),
                   norm_final_g=(norm_final_g, m_norm_final_g, v_norm_final_g))
    res = {}

    def update(nm):
        w, m, v = weights[nm]
        shape2 = (w.shape[-2], w.shape[-1]) if w.ndim >= 2 else (1, w.shape[0])
        g2 = grads[nm].reshape(shape2)
        dl, mn, vn = _adamw("adamw_" + nm, g2, w.reshape(shape2), m.reshape(shape2), v.reshape(shape2))
        res[nm] = [a.reshape(w.shape) for a in (g2, dl, mn, vn)]

    for nm in list(grads):
        update(nm)
    reduced("ffn", res["w_ada"][1], ("w_ffn_gate", "w_ffn_up", "w_ffn_down"))
    reduced("mix", res["w_ffn_down"][1], ("w_conv_out", "w_hgrn_out", "w_o"))
    reduced("in", res["w_o"][1], ("w_in",))
    outs = [[res[nm][i] for nm in weights] for i in range(4)]
    return (loss, dx.reshape(x.shape), *outs[0], *outs[1], *outs[2], *outs[3])
```

```python
import functools

import numpy as np
import jax
import jax.numpy as jnp
from jax import lax
from jax.experimental import pallas as pl
from jax.experimental.pallas import tpu as pltpu

f32, bf16 = jnp.float32, jnp.bfloat16
SDS = jax.ShapeDtypeStruct

EPS = 1e-6
HEADS, DK, CHUNK = 8, 128, 64
HEADS_PER_STEP = 4
N_MOD = 6
NDEV = 8
ADAM_LR, ADAM_B1, ADAM_B2, ADAM_EPS, ADAM_WD, ADAM_STEP = 0.001, 0.9, 0.999, 1e-08, 0.01, 10
LEVELS = (32, 16, 8, 4, 2, 1)
V7X_VMEM_LIMIT = 56 * 1024 * 1024
HBM = pl.BlockSpec(memory_space=pltpu.HBM)
MESH = pl.DeviceIdType.MESH


def _params(sem):
    return pltpu.CompilerParams(dimension_semantics=sem, vmem_limit_bytes=V7X_VMEM_LIMIT)


def _sigmoid(x):
    return jax.nn.sigmoid(x)


def _dsilu(x, s):
    return s * (1.0 + x * (1.0 - s))


def _mesh_pos():
    x, y, c = lax.axis_index("x"), lax.axis_index("y"), lax.axis_index("c")
    return x, y, c


def _peer(pos, j):
    x, y, c = pos
    return (1 - x if j & 4 else x, 1 - y if j & 2 else y, 1 - c if j & 1 else c)


def _flat(pos):
    return 4 * pos[0] + 2 * pos[1] + pos[2]


def _all_gather(name, arrs, after=()):
    n, ne = len(arrs), len(after)
    out_shapes = [SDS((NDEV,) + a.shape, a.dtype) for a in arrs]

    def body(*refs):
        ins, outs = refs[:n], refs[n + ne:2 * n + ne]
        send_sems, recv_sems, local_sems = refs[2 * n + ne:]
        pos = _mesh_pos()
        me = _flat(pos)

        def copy(a, j, frm, to_pos):
            k = a * (NDEV - 1) + j - 1
            return pltpu.make_async_remote_copy(src_ref=ins[a], dst_ref=outs[a].at[frm], send_sem=send_sems.at[k],
                                                recv_sem=recv_sems.at[k], device_id=to_pos, device_id_type=MESH)

        local = [pltpu.make_async_copy(ins[a], outs[a].at[me], local_sems.at[a]) for a in range(n)]
        for cp in local:
            cp.start()
        sends = [copy(a, j, me, _peer(pos, j)) for j in range(1, NDEV) for a in range(n)]
        for cp in sends:
            cp.start()
        for j in range(1, NDEV):
            for a in range(n):
                copy(a, j, _flat(_peer(pos, j)), pos).wait_recv()
        for cp in sends:
            cp.wait_send()
        for cp in local:
            cp.wait()

    return pl.pallas_call(
        body, name=name, out_shape=out_shapes, in_specs=[HBM] * n + [ANY] * ne, out_specs=[HBM] * n,
        scratch_shapes=[pltpu.SemaphoreType.DMA((n * (NDEV - 1),)), pltpu.SemaphoreType.DMA((n * (NDEV - 1),)),
                        pltpu.SemaphoreType.DMA((n,))],
    )(*arrs, *after)


SEM = pl.BlockSpec(memory_space=pltpu.SEMAPHORE)
ANY = pl.BlockSpec(memory_space=pl.ANY)
EFFECT = pltpu.SideEffectType.DATAFLOW_SIDE_EFFECTING
ICI_RELATIONS = (2, 4, 6)


def _chip(pos):
    return 2 * pos[0] + pos[1]


def _hbm(a):
    return pltpu.with_memory_space_constraint(a, pltpu.HBM)


def _plan_copy(plan_entry, k, pos, frm, to, src_refs, land_refs, send_sems, recv_sems):
    a, _, src_slot, dst_slot = plan_entry
    s = src_refs[a] if src_slot is None else src_refs[a].at[src_slot(frm, to)]
    return pltpu.make_async_remote_copy(src_ref=s, dst_ref=land_refs[a].at[dst_slot(frm, to)], send_sem=send_sems.at[k],
                                        recv_sem=recv_sems.at[k], device_id=to, device_id_type=MESH)


def _push_start(name, srcs, lands, plan, after=None):
    ns, nb, nk = len(srcs), len(srcs) + len(lands), len(plan)
    extra = [] if after is None else [after]

    def body(*refs):
        land_refs = refs[ns:nb]
        src_refs = refs[:ns] if ns else land_refs
        send_sems, recv_sems = refs[nb + len(extra)], refs[nb + len(extra) + 1]
        pos = _mesh_pos()
        for k, e in enumerate(plan):
            _plan_copy(e, k, pos, pos, _peer(pos, e[1]), src_refs, land_refs, send_sems, recv_sems).start()
        refs[-1][...] = jnp.zeros_like(refs[-1])

    outs = pl.pallas_call(
        body, name=name,
        out_shape=[pltpu.SemaphoreType.DMA((nk,)), pltpu.SemaphoreType.DMA((nk,))] + [pltpu.HBM(a.shape, a.dtype) for a in srcs + lands]
        + [SDS((8, 128), f32)],
        in_specs=[HBM] * nb + [ANY] * len(extra), out_specs=[SEM, SEM] + [HBM] * nb + [pl.BlockSpec(memory_space=pltpu.VMEM)],
        input_output_aliases={i: 2 + i for i in range(nb)},
        compiler_params=pltpu.CompilerParams(has_side_effects=EFFECT),
    )(*[_hbm(a) for a in srcs + lands], *extra)
    return outs[0], outs[1], list(outs[2:2 + ns]), list(outs[2 + ns:2 + nb]), outs[-1]


def _push_wait(name, send_sems, recv_sems, srcs, lands, plan, after):
    ns, nb = len(srcs), len(srcs) + len(lands)

    def body(*refs):
        land_refs = refs[ns:nb]
        src_refs = refs[:ns] if ns else land_refs
        ssem, rsem = refs[nb], refs[nb + 1]
        pos = _mesh_pos()
        for k, e in enumerate(plan):
            peer = _peer(pos, e[1])
            _plan_copy(e, k, pos, pos, peer, src_refs, land_refs, ssem, rsem).wait_send()
            _plan_copy(e, k, pos, peer, pos, src_refs, land_refs, ssem, rsem).wait_recv()

    outs = pl.pallas_call(
        body, name=name, out_shape=[pltpu.HBM(a.shape, a.dtype) for a in srcs + lands],
        in_specs=[HBM] * nb + [SEM, SEM, ANY], out_specs=[HBM] * nb,
        input_output_aliases={i: i for i in range(nb)},
        compiler_params=pltpu.CompilerParams(has_side_effects=EFFECT),
    )(*srcs, *lands, send_sems, recv_sems, after)
    return list(outs[:ns]), list(outs[ns:])


def _after_tokens(small, tokens):
    for t in tokens:
        if t is not None:
            small = small + t[0:1, 0:1].reshape((1,) * small.ndim)
    return small


def _place_shard(name, me, shard, after=None):
    r, c = shard.shape
    tr = _row_tile(r)
    extra = [] if after is None else [after]

    def body(me_ref, s_ref, *rest):
        rest[-2][...] = s_ref[...].astype(bf16)
        rest[-1][...] = jnp.zeros_like(rest[-1])

    return pl.pallas_call(
        body, name=name, out_shape=[SDS((NDEV, r, c), bf16), SDS((8, 128), f32)],
        grid_spec=pltpu.PrefetchScalarGridSpec(
            num_scalar_prefetch=1, grid=(r // tr,), in_specs=[pl.BlockSpec((tr, c), lambda i, me_ref: (i, 0))] + [ANY] * len(extra),
            out_specs=[pl.BlockSpec((None, tr, c), lambda i, me_ref: (me_ref[0], i, 0)),
                       pl.BlockSpec((8, 128), lambda i, me_ref: (0, 0))]),
        compiler_params=_params(("arbitrary",)),
    )(me, shard, *extra)


def _row_tile(r, most=256):
    return max(t for t in range(16, most + 1, 16) if r % t == 0)


def _sibling_swap(name, grads):
    n = len(grads)
    nchip = NDEV // 2

    def body(*refs):
        g_refs, out_refs = refs[:n], refs[n:2 * n]
        send_sems, recv_sems = refs[2 * n:]
        pos = _mesh_pos()
        sib = _peer(pos, 1)
        sends = []
        for a in range(n):
            for q in range(nchip):
                k = a * nchip + q
                sends.append(pltpu.make_async_remote_copy(src_ref=g_refs[a].at[2 * q + sib[2]], dst_ref=out_refs[a].at[q],
                                                          send_sem=send_sems.at[k], recv_sem=recv_sems.at[k], device_id=sib,
                                                          device_id_type=MESH))
        for cp in sends:
            cp.start()
        for cp in sends:
            cp.wait()

    return pl.pallas_call(
        body, name=name, out_shape=[SDS((nchip,) + a.shape[1:], a.dtype) for a in grads], in_specs=[HBM] * n, out_specs=[HBM] * n,
        scratch_shapes=[pltpu.SemaphoreType.DMA((n * nchip,)), pltpu.SemaphoreType.DMA((n * nchip,))],
    )(*grads)


def _pair_sum(name, core, grad, recv, tr):
    _, r, c = grad.shape
    nchip = NDEV // 2

    def body(core_ref, g_ref, r_ref, o_ref):
        o_ref[...] = (g_ref[...].astype(f32) + r_ref[...].astype(f32)).astype(o_ref.dtype)

    return pl.pallas_call(
        body, name=name, out_shape=SDS((nchip, r, c), grad.dtype),
        grid_spec=pltpu.PrefetchScalarGridSpec(
            num_scalar_prefetch=1, grid=(nchip, r // tr),
            in_specs=[pl.BlockSpec((None, tr, c), lambda q, i, core_ref: (2 * q + core_ref[0], i, 0)),
                      pl.BlockSpec((None, tr, c), lambda q, i, core_ref: (q, i, 0))],
            out_specs=pl.BlockSpec((None, tr, c), lambda q, i, core_ref: (q, i, 0))),
        compiler_params=_params(("parallel", "parallel")),
    )(core, grad, recv)


def _adamw_math(g, w, m, v):
    mn = ADAM_B1 * m + (1.0 - ADAM_B1) * g
    vn = ADAM_B2 * v + (1.0 - ADAM_B2) * jnp.square(g)
    m_hat = mn / (1.0 - ADAM_B1 ** ADAM_STEP)
    v_hat = vn / (1.0 - ADAM_B2 ** ADAM_STEP)
    return -ADAM_LR * (m_hat / (jnp.sqrt(v_hat) + ADAM_EPS) + ADAM_WD * w), mn, vn


def _chip_sum_adamw(name, chip, pairs, recv, tr, wmv):
    nchip, r, c = pairs.shape

    def body(chip_ref, p_ref, r_ref, w_ref, m_ref, v_ref, g_ref, d_ref, mo_ref, vo_ref):
        mine = chip_ref[0]
        own = p_ref[...].astype(f32)
        acc = jnp.zeros((tr, c), f32)
        for q in range(nchip):
            acc = acc + jnp.where(mine == q, own, r_ref[q].astype(f32))
        g_ref[...] = acc
        d_ref[...], mo_ref[...], vo_ref[...] = _adamw_math(acc, w_ref[...], m_ref[...], v_ref[...])

    blk = pl.BlockSpec((tr, c), lambda i, chip_ref: (i, 0))
    return pl.pallas_call(
        body, name=name, out_shape=[SDS((r, c), f32)] * 4,
        grid_spec=pltpu.PrefetchScalarGridSpec(
            num_scalar_prefetch=1, grid=(r // tr,),
            in_specs=[pl.BlockSpec((None, tr, c), lambda i, chip_ref: (chip_ref[0], i, 0)),
                      pl.BlockSpec((nchip, tr, c), lambda i, chip_ref: (0, i, 0))] + [blk] * 3,
            out_specs=[blk] * 4),
        compiler_params=_params(("parallel",)),
    )(chip, pairs, recv, *wmv)


def _matmul(name, a, b, out_shape, out_dtype, grid, a_spec, b_spec, o_spec, dims, after=None):
    ksteps = grid[2]
    acc_shape = tuple(d for d in o_spec.block_shape if d is not None)
    extra = [] if after is None else [after]

    def body(a_ref, b_ref, *rest):
        o_ref, acc = rest[len(extra)], rest[len(extra) + 1:]
        prod = lax.dot_general(a_ref[...], b_ref[...], (dims, ((), ())), preferred_element_type=f32)
        if ksteps == 1:
            o_ref[...] = prod.astype(o_ref.dtype)
        else:
            k = pl.program_id(2)

            @pl.when(k == 0)
            def _():
                acc[0][...] = prod

            @pl.when(k > 0)
            def _():
                acc[0][...] += prod

            @pl.when(k == ksteps - 1)
            def _():
                o_ref[...] = acc[0][...].astype(o_ref.dtype)

    return pl.pallas_call(
        body, name=name, grid=grid, in_specs=[a_spec, b_spec] + [ANY] * len(extra), out_specs=o_spec,
        out_shape=SDS(out_shape, out_dtype), scratch_shapes=[] if ksteps == 1 else [pltpu.VMEM(acc_shape, f32)],
        compiler_params=_params(("parallel", "parallel", "arbitrary")),
    )(a, b, *extra)


NN, NT, TN = ((1,), (0,)), ((1,), (1,)), ((0,), (0,))


def _proj_part(name, h, w_sh, ids, prev, tm, after=None):
    s, d = h.shape
    nsh, _, n = w_sh.shape
    extra = ([] if prev is None else [prev]) + ([] if after is None else [after])

    def body(ids_ref, a_ref, b_ref, *rest):
        rest[-1][...] = jnp.dot(a_ref[...], b_ref[...], preferred_element_type=f32)

    return pl.pallas_call(
        body, name=name, out_shape=SDS((s, nsh * n), f32),
        grid_spec=pltpu.PrefetchScalarGridSpec(
            num_scalar_prefetch=1, grid=(ids.shape[0], s // tm),
            in_specs=[pl.BlockSpec((tm, d), lambda j, i, ids_ref: (i, 0)),
                      pl.BlockSpec((None, d, n), lambda j, i, ids_ref: (ids_ref[j], 0, 0))] + [ANY] * len(extra),
            out_specs=pl.BlockSpec((tm, n), lambda j, i, ids_ref: (i, ids_ref[j]))),
        input_output_aliases={} if prev is None else {3: 0},
        compiler_params=_params(("parallel", "parallel")),
    )(ids, h, w_sh, *extra)


def _rowwise(name, fn, n_rows, tm, rows, fulls, row_outs, acc_outs=(), ncol=1):
    assert ncol == 1 or not acc_outs
    nr, nf, no, na = len(rows), len(fulls), len(row_outs), len(acc_outs)
    in_specs = [pl.BlockSpec((tm, w), functools.partial(lambda i, j, cb: (i, cb + j), cb=cb)) for (_, w, cb) in rows]
    in_specs += [pl.BlockSpec(a.shape, functools.partial(lambda i, j, nd: (0,) * nd, nd=a.ndim)) for a in fulls]
    out_shape = [SDS((n_rows, w), dt) for (w, dt) in row_outs] + [SDS(s, f32) for s in acc_outs]
    out_specs = [pl.BlockSpec((tm, w // ncol), lambda i, j: (i, j)) for (w, _) in row_outs]
    out_specs += [pl.BlockSpec(s, functools.partial(lambda i, j, nd: (0,) * nd, nd=len(s))) for s in acc_outs]

    def body(*refs):
        if na:
            @pl.when(pl.program_id(0) == 0)
            def _():
                for r in refs[nr + nf + no:]:
                    r[...] = jnp.zeros(r.shape, r.dtype)
        fn(refs[:nr], refs[nr:nr + nf], refs[nr + nf:nr + nf + no], refs[nr + nf + no:])

    return pl.pallas_call(
        body, name=name, grid=(n_rows // tm, ncol), in_specs=in_specs, out_specs=out_specs, out_shape=out_shape,
        compiler_params=_params(("arbitrary" if na else "parallel", "arbitrary" if na else "parallel")),
    )(*[r[0] for r in rows], *fulls)


def _rms(x):
    r = lax.rsqrt(jnp.mean(x * x, axis=-1, keepdims=True) + EPS)
    return r, x * r


def _colsum(v):
    return jnp.sum(v, axis=0, keepdims=True)


def _shift_down(u, row, k):
    return jnp.where(row >= k, pltpu.roll(u, k, 0), 0.0)


def _shift_up(u, row, k):
    n = u.shape[0]
    return jnp.where(row < n - k, pltpu.roll(u, n - k, 0), 0.0)


def _conv_fwd(proj, conv_w, s, dc):
    nb = dc // 128

    def body(ab_ref, ac_ref, ax_ref, w_ref, z_ref):
        u = ac_ref[...] * ax_ref[...]
        row = lax.broadcasted_iota(jnp.int32, u.shape, 0)
        w = w_ref[...]
        cv = w[0:1] * _shift_down(u, row, 2) + w[1:2] * _shift_down(u, row, 1) + w[2:3] * u
        z_ref[...] = (ab_ref[...] * cv).astype(z_ref.dtype)

    col = lambda off: pl.BlockSpec((s, 128), functools.partial(lambda j, off: (0, off + j), off=off))
    return pl.pallas_call(
        body, name="conv_fwd", grid=(nb,), in_specs=[col(0), col(nb), col(2 * nb), pl.BlockSpec((3, 128), lambda j: (0, j))],
        out_specs=pl.BlockSpec((s, 128), lambda j: (0, j)), out_shape=SDS((s, dc), bf16), compiler_params=_params(("parallel",)),
    )(proj, proj, proj, conv_w)


def _conv_bwd(proj, conv_w, dz, s, dc):
    nb = dc // 128

    def body(ab_ref, ac_ref, ax_ref, w_ref, dz_ref, dab_ref, dac_ref, dax_ref, dw_ref):
        ab, ac, ax, dzv = ab_ref[...], ac_ref[...], ax_ref[...], dz_ref[...].astype(f32)
        u = ac * ax
        row = lax.broadcasted_iota(jnp.int32, u.shape, 0)
        w = w_ref[...]
        u1, u2 = _shift_down(u, row, 1), _shift_down(u, row, 2)
        cv = w[0:1] * u2 + w[1:2] * u1 + w[2:3] * u
        dcv = dzv * ab
        dab_ref[...] = (dzv * cv).astype(dab_ref.dtype)
        du = w[2:3] * dcv + w[1:2] * _shift_up(dcv, row, 1) + w[0:1] * _shift_up(dcv, row, 2)
        dac_ref[...] = (du * ax).astype(dac_ref.dtype)
        dax_ref[...] = (du * ac).astype(dax_ref.dtype)
        dw_ref[0:1, :] = _colsum(dcv * u2)
        dw_ref[1:2, :] = _colsum(dcv * u1)
        dw_ref[2:3, :] = _colsum(dcv * u)

    col = lambda off: pl.BlockSpec((s, 128), functools.partial(lambda j, off: (0, off + j), off=off))
    blk = pl.BlockSpec((s, 128), lambda j: (0, j))
    return pl.pallas_call(
        body, name="conv_bwd", grid=(nb,),
        in_specs=[col(0), col(nb), col(2 * nb), pl.BlockSpec((3, 128), lambda j: (0, j)), blk],
        out_specs=[blk, blk, blk, pl.BlockSpec((3, 128), lambda j: (0, j))],
        out_shape=[SDS((s, dc), bf16)] * 3 + [SDS((3, dc), f32)], compiler_params=_params(("parallel",)),
    )(proj, proj, proj, conv_w, dz)


def _level_masks():
    t = np.arange(CHUNK)[:, None]
    s = np.arange(CHUNK)[None, :]
    m = np.stack([((t & h) != 0) & ((s & h) == 0) & (t // (2 * h) == s // (2 * h)) for h in LEVELS]).astype(np.float32)
    return jnp.asarray(m), jnp.asarray(m.transpose(0, 2, 1))


def _cumsum_rows(x, row):
    for sh in (1, 2, 4, 8, 16, 32):
        x = x + jnp.where(row >= sh, pltpu.roll(x, sh, 0), 0.0)
    return x


def _rev_cumsum_rows(x, row):
    n = x.shape[0]
    for sh in (1, 2, 4, 8, 16, 32):
        x = x + jnp.where(row < n - sh, pltpu.roll(x, n - sh, 0), 0.0)
    return x


def _chunk_terms(qp, fl, lb):
    row = lax.broadcasted_iota(jnp.int32, qp.shape, 0)
    sig = _sigmoid(fl)
    f = lb + (1.0 - lb) * sig
    k = 1.0 - f
    sq = _sigmoid(qp)
    qh = qp * sq
    b = _cumsum_rows(jnp.log(f), row)
    sub = lax.broadcasted_iota(jnp.int32, (CHUNK // 8, 8, DK), 1)
    b8 = b.reshape(CHUNK // 8, 8, DK)
    us, exs, ups = [], [], []
    for m in LEVELS:
        sb = 2 * m
        if sb >= 8:
            b3 = b.reshape(CHUNK // sb, sb, DK)
            bref = jnp.broadcast_to(b3[:, m - 1:m, :], b3.shape).reshape(CHUNK, DK)
        else:
            bref8 = None
            for j in range(8 // sb):
                cand = jnp.broadcast_to(b8[:, j * sb + m - 1:j * sb + m, :], b8.shape)
                bref8 = cand if bref8 is None else jnp.where(sub >= j * sb, cand, bref8)
            bref = bref8.reshape(CHUNK, DK)
        up = (row & m) != 0
        ex = jnp.exp(jnp.where(up, b - bref, bref - b))
        us.append((jnp.where(up, qh, k) * ex).astype(bf16))
        exs.append(ex)
        ups.append(up)
    blast = b[CHUNK - 1:CHUNK, :]
    eb, ebl = jnp.exp(b), jnp.exp(blast - b)
    return dict(sig=sig, f=f, k=k, sq=sq, qh=qh, u=jnp.stack(us), ex=exs, up=ups, eb=eb, ebl=ebl, qt=qh * eb, kt=k * ebl,
                el=jnp.exp(blast), row=row)


def _scores(t, mask):
    pl_ = jnp.einsum("ltk,lsk->lts", t["u"], t["u"], preferred_element_type=f32)
    p = jnp.sum(pl_ * mask, axis=0)
    r = lax.broadcasted_iota(jnp.int32, (CHUNK, CHUNK), 0)
    c = lax.broadcasted_iota(jnp.int32, (CHUNK, CHUNK), 1)
    diag = jnp.sum(t["qh"] * t["k"], axis=-1, keepdims=True)
    return p + jnp.where(r == c, diag, 0.0)


def _hgrn_fwd(proj, lb_param, gnorm, s, dv_total, tb):
    nchunk = tb // CHUNK
    masks, _ = _level_masks()
    q0, f0, v0, g0 = 3 * HEADS, 4 * HEADS, 5 * HEADS, 6 * HEADS

    def body(q_ref, f_ref, v_ref, g_ref, lb_ref, gn_ref, mask_ref, og_ref, o_ref, st_ref, state):
        @pl.when(pl.program_id(1) == 0)
        def _():
            state[...] = jnp.zeros_like(state)

        lbp = lb_ref[...]
        lb_all = _sigmoid(lbp[0:1, :] - lbp[1:2, :])
        mask = mask_ref[...]
        for i, hh in [(i, hh) for i in range(nchunk) for hh in range(HEADS_PER_STEP)]:
            rs, cs = pl.ds(i * CHUNK, CHUNK), pl.ds(hh * DK, DK)
            t = _chunk_terms(q_ref[rs, cs], f_ref[rs, cs], lb_all[:, hh * DK:(hh + 1) * DK])
            v = v_ref[rs, cs]
            vb = v.astype(bf16)
            st = state[hh]
            st_ref[i, hh] = st
            p = _scores(t, mask)
            o = jnp.dot(p.astype(bf16), vb, preferred_element_type=f32)
            o += lax.dot_general(t["qt"].astype(bf16), st.astype(bf16), (NT, ((), ())), preferred_element_type=f32)
            state[hh] = st * t["el"] + lax.dot_general(vb, t["kt"].astype(bf16), (TN, ((), ())), preferred_element_type=f32)
            o_ref[rs, cs] = o
            r, oh = _rms(o)
            g = g_ref[rs, cs]
            og_ref[rs, cs] = (oh * gn_ref[...] * (g * _sigmoid(g))).astype(og_ref.dtype)

    hp, wide = HEADS_PER_STEP, HEADS_PER_STEP * DK
    col = lambda off: pl.BlockSpec((tb, wide), functools.partial(lambda h, t, off: (t, off + h), off=off // hp))
    blk = pl.BlockSpec((tb, wide), lambda h, t: (t, h))
    return pl.pallas_call(
        body, name="hgrn_fwd", grid=(HEADS // hp, s // tb),
        in_specs=[col(q0), col(f0), col(v0), col(g0), pl.BlockSpec((2, wide), lambda h, t: (0, h)),
                  pl.BlockSpec((1, DK), lambda h, t: (0, 0)), pl.BlockSpec(masks.shape, lambda h, t: (0, 0, 0))],
        out_specs=[blk, blk, pl.BlockSpec((nchunk, hp, DK, DK), lambda h, t: (t, h, 0, 0))],
        out_shape=[SDS((s, dv_total), bf16), SDS((s, dv_total), f32), SDS((s // CHUNK, HEADS, DK, DK), f32)],
        scratch_shapes=[pltpu.VMEM((hp, DK, DK), f32)], compiler_params=_params(("parallel", "arbitrary")),
    )(proj, proj, proj, proj, lb_param, gnorm, masks)


def _hgrn_bwd(proj, lb_param, gnorm, o_saved, states, dog, s, dv_total, tb):
    nchunk = tb // CHUNK
    nt = s // tb
    nc_total = s // CHUNK
    masks, masks_t = _level_masks()
    q0, f0, v0, g0 = 3 * HEADS, 4 * HEADS, 5 * HEADS, 6 * HEADS

    def body(q_ref, f_ref, v_ref, g_ref, lb_ref, gn_ref, mask_ref, maskt_ref, o_ref, dog_ref, st_ref, stn_ref,
             dq_ref, df_ref, dv_ref, dg_ref, dlb_ref, dgn_ref, gstate):
        h_id, t_id = pl.program_id(0), pl.program_id(1)

        @pl.when(t_id == 0)
        def _():
            gstate[...] = jnp.zeros_like(gstate)
            dlb_ref[...] = jnp.zeros_like(dlb_ref)

        @pl.when((t_id == 0) & (h_id == 0))
        def _():
            dgn_ref[...] = jnp.zeros_like(dgn_ref)

        lbp = lb_ref[...]
        lb_all = _sigmoid(lbp[0:1, :] - lbp[1:2, :])
        mask, maskt = mask_ref[...], maskt_ref[...]
        gn = gn_ref[...]
        for i, hh in [(i, hh) for i in reversed(range(nchunk)) for hh in range(HEADS_PER_STEP)]:
            rs, cs = pl.ds(i * CHUNK, CHUNK), pl.ds(hh * DK, DK)
            lb = lb_all[:, hh * DK:(hh + 1) * DK]
            qp, fl, v, g = q_ref[rs, cs], f_ref[rs, cs], v_ref[rs, cs], g_ref[rs, cs]
            t = _chunk_terms(qp, fl, lb)
            vb = v.astype(bf16)
            st0 = st_ref[i, hh]
            st1 = st_ref[i + 1, hh] if i + 1 < nchunk else stn_ref[0, hh]
            gt = gstate[hh]
            o = o_ref[rs, cs]
            r, oh = _rms(o)
            sg = _sigmoid(g)
            dog_v = dog_ref[rs, cs].astype(f32)
            dg_ref[rs, cs] = (dog_v * (oh * gn) * _dsilu(g, sg)).astype(dg_ref.dtype)
            don = dog_v * (g * sg)
            dgn_ref[...] += _colsum(don * oh)
            doh = don * gn
            do = r * (doh - oh * jnp.mean(doh * oh, axis=-1, keepdims=True))
            dob = do.astype(bf16)
            d = lax.dot_general(dob, vb, (NT, ((), ())), preferred_element_type=f32)
            dt = lax.dot_general(vb, dob, (NT, ((), ())), preferred_element_type=f32)
            z = (mask * d[None] + maskt * dt[None]).astype(bf16)
            rr = jnp.einsum("lts,lsk->ltk", z, t["u"], preferred_element_type=f32)
            dq = jnp.zeros((CHUNK, DK), f32)
            dk = jnp.zeros((CHUNK, DK), f32)
            qdk = jnp.zeros((CHUNK, DK), f32)
            for li in range(len(LEVELS)):
                du = t["ex"][li] * rr[li]
                dq += jnp.where(t["up"][li], du, 0.0)
                dk += jnp.where(t["up"][li], 0.0, du)
                e = t["u"][li].astype(f32) * rr[li]
                qdk += jnp.where(t["up"][li], e, -e)
            dd = jnp.sum(do * v, axis=-1, keepdims=True)
            dq += dd * t["k"]
            dk += dd * t["qh"]
            gtb = gt.astype(bf16)
            ktb, qtb = t["kt"].astype(bf16), t["qt"].astype(bf16)
            dq_in = jnp.dot(dob, st0.astype(bf16), preferred_element_type=f32)
            dk_in = jnp.dot(vb, gtb, preferred_element_type=f32)
            dq += t["eb"] * dq_in
            dk += t["ebl"] * dk_in
            qdk += qtb.astype(f32) * dq_in - ktb.astype(f32) * dk_in
            p = _scores(t, mask)
            dvv = lax.dot_general(p.astype(bf16), dob, (TN, ((), ())), preferred_element_type=f32)
            dvv += lax.dot_general(ktb, gtb, (NT, ((), ())), preferred_element_type=f32)
            dv_ref[rs, cs] = dvv.astype(dv_ref.dtype)
            a_end = _colsum(gtb.astype(f32) * st1)
            dlf = _rev_cumsum_rows(qdk, t["row"]) + a_end
            dfv = dlf / t["f"] - dk
            df_ref[rs, cs] = (dfv * (1.0 - lb) * t["sig"] * (1.0 - t["sig"])).astype(df_ref.dtype)
            dlb_ref[:, cs] += _colsum(dfv * (1.0 - t["sig"]))
            dq_ref[rs, cs] = (dq * _dsilu(qp, t["sq"])).astype(dq_ref.dtype)
            gstate[hh] = gt * t["el"] + lax.dot_general(dob, qtb, (TN, ((), ())), preferred_element_type=f32)

    hp, wide = HEADS_PER_STEP, HEADS_PER_STEP * DK
    rev = lambda t: nt - 1 - t
    col = lambda off: pl.BlockSpec((tb, wide), functools.partial(lambda h, t, off: (rev(t), off + h), off=off // hp))
    blk = pl.BlockSpec((tb, wide), lambda h, t: (rev(t), h))
    nxt = lambda h, t: (jnp.minimum((rev(t) + 1) * nchunk, nc_total - 1), h, 0, 0)
    return pl.pallas_call(
        body, name="hgrn_bwd", grid=(HEADS // hp, nt),
        in_specs=[col(q0), col(f0), col(v0), col(g0), pl.BlockSpec((2, wide), lambda h, t: (0, h)),
                  pl.BlockSpec((1, DK), lambda h, t: (0, 0)), pl.BlockSpec(masks.shape, lambda h, t: (0, 0, 0)),
                  pl.BlockSpec(masks.shape, lambda h, t: (0, 0, 0)), blk, blk,
                  pl.BlockSpec((nchunk, hp, DK, DK), lambda h, t: (rev(t), h, 0, 0)),
                  pl.BlockSpec((1, hp, DK, DK), nxt)],
        out_specs=[blk, blk, blk, blk, pl.BlockSpec((1, wide), lambda h, t: (0, h)), pl.BlockSpec((1, DK), lambda h, t: (0, 0))],
        out_shape=[SDS((s, dv_total), bf16)] * 4 + [SDS((1, HEADS * DK), f32), SDS((1, DK), f32)],
        scratch_shapes=[pltpu.VMEM((hp, DK, DK), f32)], compiler_params=_params(("arbitrary", "arbitrary")),
    )(proj, proj, proj, proj, lb_param, gnorm, masks, masks_t, o_saved, dog, states, states)


def _adamw(name, g, w, m, v):
    r, c = w.shape
    tr = r
    for cand in (256, 128, 64, 32, 16, 8):
        if r % cand == 0 and r > cand:
            tr = cand
            break

    def body(g_ref, w_ref, m_ref, v_ref, d_ref, mo_ref, vo_ref):
        d_ref[...], mo_ref[...], vo_ref[...] = _adamw_math(g_ref[...], w_ref[...], m_ref[...], v_ref[...])

    blk = pl.BlockSpec((tr, c), lambda i: (i, 0))
    return pl.pallas_call(
        body, name=name, grid=(r // tr,), in_specs=[blk] * 4, out_specs=[blk] * 3, out_shape=[SDS((r, c), f32)] * 3,
        compiler_params=_params(("parallel",)),
    )(g, w, m, v)


def _ffn_in(h2, w_gt, w_ut, tm, ffb):
    s, d = h2.shape
    dff = w_gt.shape[0]

    def body(a_ref, wg_ref, wu_ref, dg_ref, du_ref, act_ref):
        a = a_ref[...]
        g = lax.dot_general(a, wg_ref[...], (NT, ((), ())), preferred_element_type=f32)
        u = lax.dot_general(a, wu_ref[...], (NT, ((), ())), preferred_element_type=f32)
        sg = _sigmoid(g)
        silu = g * sg
        dg_ref[...] = (u * _dsilu(g, sg)).astype(bf16)
        du_ref[...] = silu.astype(bf16)
        act_ref[...] = (silu * u).astype(bf16)

    w_spec = pl.BlockSpec((ffb, d), lambda j, i: (j, 0))
    o_spec = pl.BlockSpec((tm, ffb), lambda j, i: (i, j))
    return pl.pallas_call(
        body, name="ffn_in", grid=(dff // ffb, s // tm), in_specs=[pl.BlockSpec((tm, d), lambda j, i: (i, 0)), w_spec, w_spec],
        out_specs=[o_spec] * 3, out_shape=[SDS((s, dff), bf16)] * 3, compiler_params=_params(("parallel", "parallel")),
    )(h2, w_gt, w_ut)


def _ffn_down_bwd(dff_out, w_d, act_dg, act_du, tm, ffb):
    s, d = dff_out.shape
    dff = w_d.shape[0]

    def body(a_ref, w_ref, fg_ref, fu_ref, dg_ref, du_ref):
        da = lax.dot_general(a_ref[...], w_ref[...], (NT, ((), ())), preferred_element_type=f32)
        dg_ref[...] = (da * fg_ref[...].astype(f32)).astype(bf16)
        du_ref[...] = (da * fu_ref[...].astype(f32)).astype(bf16)

    t_spec = pl.BlockSpec((tm, ffb), lambda j, i: (i, j))
    return pl.pallas_call(
        body, name="d_ffn_down_in", grid=(dff // ffb, s // tm),
        in_specs=[pl.BlockSpec((tm, d), lambda j, i: (i, 0)), pl.BlockSpec((ffb, d), lambda j, i: (j, 0)), t_spec, t_spec],
        out_specs=[t_spec] * 2, out_shape=[SDS((s, dff), bf16)] * 2, compiler_params=_params(("parallel", "parallel")),
    )(dff_out, w_d, act_dg, act_du)


def _branch_merge(z_a, og, w_co, w_ho, proj, tm, gate_a0, gate_b0):
    s, dc = z_a.shape
    nsh, _, n = w_co.shape

    def body(za_ref, og_ref, wa_ref, wb_ref, ga_ref, gb_ref, m_ref, sa_ref, sb_ref, fa_ref, fb_ref):
        ya = jnp.dot(za_ref[...], wa_ref[...], preferred_element_type=f32)
        yb = jnp.dot(og_ref[...], wb_ref[...], preferred_element_type=f32)
        sa, sb_ = _sigmoid(ga_ref[...]), _sigmoid(gb_ref[...])
        m_ref[...] = (sa * ya + sb_ * yb).astype(bf16)
        sa_ref[...] = sa.astype(bf16)
        sb_ref[...] = sb_.astype(bf16)
        fa_ref[...] = (ya * sa * (1.0 - sa)).astype(bf16)
        fb_ref[...] = (yb * sb_ * (1.0 - sb_)).astype(bf16)

    a_spec = pl.BlockSpec((tm, dc), lambda i, j: (i, 0))
    w_spec = pl.BlockSpec((None, dc, n), lambda i, j: (j, 0, 0))
    gate = lambda c0: pl.BlockSpec((tm, n), functools.partial(lambda i, j, cb: (i, cb + j), cb=c0 // n))
    o_spec = pl.BlockSpec((tm, n), lambda i, j: (i, j))
    return pl.pallas_call(
        body, name="branch_merge", grid=(s // tm, nsh), in_specs=[a_spec, a_spec, w_spec, w_spec, gate(gate_a0), gate(gate_b0)],
        out_specs=[o_spec] * 5, out_shape=[SDS((s, nsh * n), bf16)] * 5, compiler_params=_params(("parallel", "parallel")),
    )(z_a, og, w_co, w_ho, proj, proj)


def _d_branch_merge(dmo, w_o, factors, tm):
    s, d = dmo.shape
    n = d // 2

    def body(a_ref, w_ref, sa_ref, sb_ref, fa_ref, fb_ref, dya_ref, dyb_ref, dga_ref, dgb_ref):
        dm = lax.dot_general(a_ref[...], w_ref[...], (NT, ((), ())), preferred_element_type=f32)
        for f_ref, o_ref in ((sa_ref, dya_ref), (sb_ref, dyb_ref), (fa_ref, dga_ref), (fb_ref, dgb_ref)):
            o_ref[...] = (dm * f_ref[...].astype(f32)).astype(bf16)

    t_spec = pl.BlockSpec((tm, n), lambda j, i: (i, j))
    return pl.pallas_call(
        body, name="d_branch_merge", grid=(d // n, s // tm),
        in_specs=[pl.BlockSpec((tm, d), lambda j, i: (i, 0)), pl.BlockSpec((n, d), lambda j, i: (j, 0))] + [t_spec] * 4,
        out_specs=[t_spec] * 4, out_shape=[SDS((s, d), bf16)] * 4, compiler_params=_params(("parallel", "parallel")),
    )(dmo, w_o, *factors)


def _local_step(x, tgt, mod, g_mix, g_ffn, g_fin, lb_param, gnorm, conv_w, project, get_w, prefetch, put_g, sent):
    s, d = x.shape
    dc = d // 2
    tm = min(512, s)
    tm2 = min(1024, s)
    te = min(256, s)
    tb = min(128, s)
    mt = s // tm
    nb = 512
    sh_m, sc_m, gt_m, sh_f, sc_f, gt_f = [mod[i] for i in range(N_MOD)]
    dh2 = d // 2

    def e1(rows, fulls, outs, accs):
        xv = rows[0][...]
        g, sc, sh = [r[...] for r in fulls]
        _, xh = _rms(xv)
        outs[0][...] = (xh * g * (1.0 + sc) + sh).astype(bf16)

    h, = _rowwise("prenorm_mix", e1, s, te, [(x, d, 0)], [g_mix, sc_m, sh_m], [(d, bf16)])
    proj, w_in = project(h, tm2)
    nsh, _, win_sh = w_in.shape
    z_a = _conv_fwd(proj, _after_tokens(conv_w, [prefetch("mix", proj)]), s, dc)
    og, o_saved, states = _hgrn_fwd(proj, lb_param, gnorm, s, dc, tb)
    w_co, w_ho, w_o = get_w("mix", og)

    gate_a0, gate_b0 = 7 * dh2, 9 * dh2
    merged, *merge_factors = _branch_merge(z_a, og, w_co, w_ho, proj, tm2, gate_a0, gate_b0)
    mo = _matmul("mix_out", merged, w_o, (s, d), bf16, (2, s // tm2, 1), pl.BlockSpec((tm2, d), lambda j, i, k: (i, 0)),
                 pl.BlockSpec((d, dh2), lambda j, i, k: (0, j)), pl.BlockSpec((tm2, dh2), lambda j, i, k: (i, j)), NN,
                 after=prefetch("ffn", merged))

    def e6(rows, fulls, outs, accs):
        xv, mov = rows[0][...], rows[1][...].astype(f32)
        gt, g, sc, sh = [r[...] for r in fulls]
        x1 = xv + gt * mov
        outs[0][...] = x1
        _, xh = _rms(x1)
        outs[1][...] = (xh * g * (1.0 + sc) + sh).astype(bf16)

    x1, h2 = _rowwise("prenorm_ffn", e6, s, tm, [(x, d, 0), (mo, d, 0)], [gt_m, g_ffn, sc_f, sh_f], [(d, f32), (d, bf16)])
    w_gt, w_ut = get_w("ffn", h2)
    dff_ = w_gt.shape[0]
    act_dg, act_du, act = _ffn_in(h2, w_gt, w_ut, tm2, nb)
    prefetch("ffn_down", act)
    w_d, = get_w("ffn_down", act)
    ff = _matmul("ffn_down", act, w_d, (s, d), bf16, (2, mt, 1), pl.BlockSpec((tm, dff_), lambda j, i, k: (i, 0)),
                 pl.BlockSpec((dff_, dh2), lambda j, i, k: (0, j)), pl.BlockSpec((tm, dh2), lambda j, i, k: (i, j)), NN)

    def e9(rows, fulls, outs, accs):
        x1v, ffv, tv = [r[...].astype(f32) for r in rows]
        gt, gf = fulls[0][...], fulls[1][...]
        x2 = x1v + gt * ffv
        r, xh = _rms(x2)
        err = xh * gf - tv
        accs[0][...] += 0.5 * jnp.sum(jnp.mean(err * err, axis=-1, keepdims=True), axis=0, keepdims=True)
        dy = err / d
        accs[1][...] += _colsum(dy * xh)
        dxh = dy * gf
        dx2 = r * (dxh - xh * jnp.mean(dxh * xh, axis=-1, keepdims=True))
        outs[0][...] = dx2
        outs[1][...] = (dx2 * gt).astype(bf16)
        accs[2][...] += _colsum(dx2 * ffv)

    dx2, dff, loss_acc, dg_fin, dgt_f = _rowwise("loss_head", e9, s, tm, [(x1, d, 0), (ff, d, 0), (tgt, d, 0)], [gt_f, g_fin],
                                                 [(d, f32), (d, bf16)], [(1, 128), (1, d), (1, d)])
    dgg, duu = _ffn_down_bwd(dff, w_d, act_dg, act_du, tm, dff_ // 4)

    def wgrad_rows(name, a, b, n_out):
        kb = 512
        return _matmul(name, a, b, (n_out, d), bf16, (n_out // kb, 2, 1), pl.BlockSpec((s, kb), lambda i, j, k: (0, i)),
                       pl.BlockSpec((s, dh2), lambda i, j, k: (0, j)), pl.BlockSpec((kb, dh2), lambda i, j, k: (i, j)), TN)

    gw_d = wgrad_rows("gw_ffn_down", act, dff, dff_)

    def ffn_in_bwd(name, a, w, after=None):
        return _matmul(name, a, w, (s, d), bf16, (2, mt, 1), pl.BlockSpec((tm, dff_), lambda j, i, k: (i, 0)),
                       pl.BlockSpec((dff_, dh2), lambda j, i, k: (0, j)), pl.BlockSpec((tm, dh2), lambda j, i, k: (i, j)), NN,
                       after=after)

    dh2b = ffn_in_bwd("d_ffn_up_in", duu, w_ut)
    gw_ut = wgrad_rows("gw_ffn_up", duu, h2, dff_)
    gw_gt = wgrad_rows("gw_ffn_gate", dgg, h2, dff_)
    dh2a = ffn_in_bwd("d_ffn_gate_in", dgg, w_gt, after=put_g("ffn", [gw_gt, gw_ut, gw_d]))
    sc_f_late = _after_tokens(sc_f, [sent("ffn", dh2a)])

    def b5(rows, fulls, outs, accs):
        da, db, x1v, dx2v, mov = [r[...].astype(f32) for r in rows]
        sc, g, gt = [r[...] for r in fulls]
        dh = da + db
        r, xh = _rms(x1v)
        accs[0][...] += _colsum(dh)
        accs[1][...] += _colsum(dh * (xh * g))
        dn = dh * (1.0 + sc)
        accs[2][...] += _colsum(dn * xh)
        dxh = dn * g
        dx1 = dx2v + r * (dxh - xh * jnp.mean(dxh * xh, axis=-1, keepdims=True))
        outs[0][...] = dx1
        accs[3][...] += _colsum(dx1 * mov)
        outs[1][...] = (dx1 * gt).astype(bf16)

    dx1, dmo, dsh_f, dsc_f, dg_ffn, dgt_m = _rowwise(
        "d_prenorm_ffn", b5, s, te, [(dh2a, d, 0), (dh2b, d, 0), (x1, d, 0), (dx2, d, 0), (mo, d, 0)], [sc_f_late, g_ffn, gt_m],
        [(d, f32), (d, bf16)], [(1, d)] * 4)
    dya, dyb, dga, dgb = _d_branch_merge(dmo, w_o, merge_factors, tm)
    gw_o = wgrad_rows("gw_mix_out", merged, dmo, d)

    def out_proj_bwd(name, dy, w):
        return _matmul(name, dy, w, (s, dc), bf16, (1, s // tm2, nsh), pl.BlockSpec((tm2, d // nsh), lambda j, i, k: (i, k)),
                       pl.BlockSpec((None, dc, d // nsh), lambda j, i, k: (k, 0, 0)), pl.BlockSpec((tm2, dc), lambda j, i, k: (i, 0)), NT)

    def out_proj_wgrad(name, a, dy):
        return _matmul(name, a, dy, (nsh, dc, d // nsh), bf16, (1, nsh, 1), pl.BlockSpec((s, dc), lambda i, j, k: (0, 0)),
                       pl.BlockSpec((s, d // nsh), lambda i, j, k: (0, j)), pl.BlockSpec((None, dc, d // nsh), lambda i, j, k: (j, 0, 0)), TN)

    dz_a = out_proj_bwd("d_conv_out_in", dya, w_co)
    dog = out_proj_bwd("d_hgrn_out_in", dyb, w_ho)
    gw_co = out_proj_wgrad("gw_conv_out", z_a, dya)
    gw_ho = out_proj_wgrad("gw_hgrn_out", og, dyb)
    conv_w_late = _after_tokens(conv_w, [put_g("mix", [gw_co, gw_ho, gw_o])])
    dab, dac, dax, dconv_w = _conv_bwd(proj, conv_w_late, dz_a, s, dc)
    lb_param_late = _after_tokens(lb_param, [sent("mix", dab)])
    dq, dfl, dvi, dgo, dlb, dgn = _hgrn_bwd(proj, lb_param_late, gnorm, o_saved, states, dog, s, dc, tb)
    dproj = jnp.concatenate([dab, dac, dax, dq, dfl, dvi, dgo, dga, dgb], axis=1)
    gw_in = _matmul("gw_proj", h, dproj, (nsh, d, win_sh), bf16, (nsh, d // 512, 1), pl.BlockSpec((s, 512), lambda j, i, k: (0, i)),
                    pl.BlockSpec((s, win_sh), lambda j, i, k: (0, j)), pl.BlockSpec((None, 512, win_sh), lambda j, i, k: (j, i, 0)), TN)
    dh = _matmul("d_proj_in", dproj, w_in, (s, d), bf16, (1, s // tm2, nsh), pl.BlockSpec((tm2, win_sh), lambda j, i, k: (i, k)),
                 pl.BlockSpec((None, d, win_sh), lambda j, i, k: (k, 0, 0)), pl.BlockSpec((tm2, d), lambda j, i, k: (i, 0)), NT,
                 after=put_g("in", [gw_in]))

    def b12(rows, fulls, outs, accs):
        dhv, xv, dx1v = [r[...].astype(f32) for r in rows]
        sc, g = fulls[0][...], fulls[1][...]
        r, xh = _rms(xv)
        accs[0][...] += _colsum(dhv)
        accs[1][...] += _colsum(dhv * (xh * g))
        dn = dhv * (1.0 + sc)
        accs[2][...] += _colsum(dn * xh)
        dxh = dn * g
        outs[0][...] = dx1v + r * (dxh - xh * jnp.mean(dxh * xh, axis=-1, keepdims=True))

    dx, dsh_m, dsc_m, dg_mix = _rowwise("d_prenorm_mix", b12, s, tm, [(dh, d, 0), (x, d, 0), (dx1, d, 0)], [sc_m, g_mix],
                                        [(d, f32)], [(1, d)] * 3)
    dmod = [dsh_m, dsc_m, dgt_m, dsh_f, dsc_f, dgt_f]
    small = dict(loss=loss_acc, g_mix=dg_mix, g_ffn=dg_ffn, g_fin=dg_fin, lb=dlb, gnorm=dgn, conv_w=dconv_w)
    return dx, dmod, small


def _ada_fwd(c_all, w_sh, b_sh):
    def body(c_ref, w_ref, b_ref, o_ref):
        cv = c_ref[...]
        ca = (cv * _sigmoid(cv)).astype(bf16)
        o_ref[...] = jnp.dot(ca, w_ref[...].astype(bf16), preferred_element_type=f32) + b_ref[...]

    return pl.pallas_call(body, name="ada_fwd", out_shape=SDS((c_all.shape[0], w_sh.shape[1]), f32),
                          compiler_params=pltpu.CompilerParams(vmem_limit_bytes=V7X_VMEM_LIMIT))(c_all, w_sh, b_sh)


def _ada_wgrad(c_all, dmod_sh):
    def body(c_ref, d_ref, o_ref):
        cv = c_ref[...]
        ca = (cv * _sigmoid(cv)).astype(bf16)
        o_ref[...] = lax.dot_general(ca, d_ref[...].astype(bf16), (TN, ((), ())), preferred_element_type=f32)

    return pl.pallas_call(body, name="ada_wgrad", out_shape=SDS((c_all.shape[1], dmod_sh.shape[1]), f32),
                          compiler_params=pltpu.CompilerParams(vmem_limit_bytes=V7X_VMEM_LIMIT))(c_all, dmod_sh)


def _lb_grad(lb_param, dlb):
    def body(p_ref, d_ref, o_ref):
        p = p_ref[...]
        lb = _sigmoid(p[0:1, :] - p[1:2, :])
        gl = d_ref[...] * lb * (1.0 - lb)
        o_ref[0:1, :] = gl
        o_ref[1:2, :] = -gl

    return pl.pallas_call(body, name="lb_grad", out_shape=SDS(lb_param.shape, f32))(lb_param, dlb)


def _sum_small(gathered):
    def body(g_ref, o_ref):
        acc = g_ref[0]
        for dd in range(1, NDEV):
            acc = acc + g_ref[dd]
        o_ref[...] = acc

    return pl.pallas_call(body, name="sum_small", out_shape=SDS(gathered.shape[1:], f32))(gathered)


def kernel(x, c, w_ada, b_ada, norm_mix_g, w_in, conv_w, lb_param, gnorm_g, w_conv_out, w_hgrn_out, w_o, norm_ffn_g, w_ffn_gate, w_ffn_up, w_ffn_down, norm_final_g, loss_target, m_w_ada, m_b_ada, m_norm_mix_g, m_w_in, m_conv_w, m_lb_param, m_gnorm_g, m_w_conv_out, m_w_hgrn_out, m_w_o, m_norm_ffn_g, m_w_ffn_gate, m_w_ffn_up, m_w_ffn_down, m_norm_final_g, v_w_ada, v_b_ada, v_norm_mix_g, v_w_in, v_conv_w, v_lb_param, v_gnorm_g, v_w_conv_out, v_w_hgrn_out, v_w_o, v_norm_ffn_g, v_w_ffn_gate, v_w_ffn_up, v_w_ffn_down, v_norm_final_g):
    assert lb_param.shape[0] == 2 and w_ada.shape[0] == 1
    s, d = x.shape[1], x.shape[2]
    me = 4 * lax.axis_index("x") + 2 * lax.axis_index("y") + lax.axis_index("c")
    ada_cols = w_ada.shape[2]

    me1 = me.astype(jnp.int32).reshape(1)
    placed_in = [_place_shard("place_in0", me1, w_in[0])[0]]
    c_all, cw_all = _all_gather("gather_cond", [c, conv_w[0]], after=placed_in)
    c_all = c_all.reshape(NDEV, d)
    conv_w_full = jnp.transpose(cw_all, (1, 0, 2)).reshape(conv_w.shape[1], -1)
    b_sh = lax.dynamic_slice_in_dim(b_ada, me * ada_cols, ada_cols, axis=1)
    mod_cols = _ada_fwd(c_all, w_ada[0], b_sh)
    mod_all, = _all_gather("gather_mod", [mod_cols])
    mod = lax.dynamic_index_in_dim(mod_all, me, axis=1, keepdims=False).reshape(N_MOD, 1, d)

    shard_groups = {"mix": [w_conv_out[0], w_hgrn_out[0], w_o[0]], "ffn": [w_ffn_gate[0].T, w_ffn_up[0].T],
                    "ffn_down": [w_ffn_down[0]]}
    own_slot = lambda frm, to: _flat(frm)
    gather_plan = lambda n: [(a, j, own_slot, own_slot) for a in range(n) for j in (1,) + ICI_RELATIONS]
    flat = lambda a: a.reshape(a.shape[0] * a.shape[1], a.shape[2])
    to8 = lambda a: a.reshape(NDEV, a.shape[0] // NDEV, a.shape[1])
    near, far = ICI_RELATIONS[:2], ICI_RELATIONS[2:]
    in_plan = lambda rels: [(0, j, own_slot, own_slot) for j in rels]
    ss_a, rs_a, _, lands, tok_a = _push_start("gather_start_in_sib", [], placed_in, in_plan((1,)), after=mod_all)
    ss_b, rs_b, _, lands_in, tok_b = _push_start("gather_start_in_near", [], lands, in_plan(near), after=tok_a)
    gathering, tokens, placed = {}, [tok_a, tok_b], {}
    for grp, sh in shard_groups.items():
        placed[grp] = []
        for i, a in enumerate(sh):
            buf, tok = _place_shard(f"place_{grp}{i}", me1, a, after=tokens[-1])
            placed[grp].append(buf)
            tokens.append(tok)

    pos = (lax.axis_index("x"), lax.axis_index("y"), lax.axis_index("c"))
    ids = lambda frm, rels: jnp.stack([_flat(_peer(frm, j)) for j in rels]).astype(jnp.int32)

    def project(h, tm):
        sib = _peer(pos, 1)
        fwd_of = lambda idxs: [(0, 1, fwd_slot(i), fwd_slot(i)) for i in idxs]
        _, lands = _push_wait("gather_wait_in_sib", ss_a, rs_a, [], lands_in, in_plan((1,)), h)
        proj = _proj_part("proj_local", h, lands[0], ids(pos, (0, 1)), None, tm)
        _, lands = _push_wait("gather_wait_in_near", ss_b, rs_b, [], lands, in_plan(near), proj)
        ss_c, rs_c, _, lands, tok = _push_start("gather_start_in_far", [], lands, in_plan(far))
        for grp in shard_groups:
            ss, rs, _, bufs, tok = _push_start("gather_start_" + grp, [], placed[grp], gather_plan(len(placed[grp])), after=tok)
            gathering[grp] = (ss, rs, bufs)
        ss1, rs1, _, lands, tok = _push_start("gather_fwd_in_near", [], lands, fwd_of((0, 1)), after=tok)
        proj = _proj_part("proj_near", h, lands[0], ids(pos, near), proj, tm, after=tok)
        _, lands = _push_wait("gather_fwd_wait_in_near", ss1, rs1, [], lands, fwd_of((0, 1)), proj)
        proj = _proj_part("proj_fwd_near", h, lands[0], ids(sib, near), proj, tm)
        _, lands = _push_wait("gather_wait_in_far", ss_c, rs_c, [], lands, in_plan(far), proj)
        ss2, rs2, _, lands, tok = _push_start("gather_fwd_in_far", [], lands, fwd_of((2,)))
        proj = _proj_part("proj_far", h, lands[0], ids(pos, far), proj, tm, after=tok)
        _, lands = _push_wait("gather_fwd_wait_in_far", ss2, rs2, [], lands, fwd_of((2,)), proj)
        proj = _proj_part("proj_fwd_far", h, lands[0], ids(sib, far), proj, tm)
        return proj, lands[0]

    fwd_slot = lambda i: (lambda frm, to: _flat(_peer(frm, ICI_RELATIONS[i])))
    fwd_plan = lambda n: [(a, 1, fwd_slot(i), fwd_slot(i)) for a in range(n) for i in range(len(ICI_RELATIONS))]
    forwarding = {}

    def prefetch(grp, after):
        ss, rs, lands = gathering[grp]
        _, lands = _push_wait("gather_wait_" + grp, ss, rs, [], lands, gather_plan(len(lands)), after)
        ss, rs, _, lands, tok = _push_start("gather_fwd_" + grp, [], lands, fwd_plan(len(lands)))
        forwarding[grp] = (ss, rs, lands)
        return tok

    def get_w(grp, after):
        ss, rs, lands = forwarding[grp]
        _, full = _push_wait("gather_fwd_wait_" + grp, ss, rs, [], lands, fwd_plan(len(lands)), after)
        return [f if i < 2 and grp == "mix" else flat(f) for i, f in enumerate(full)]

    core = lax.axis_index("c").astype(jnp.int32).reshape(1)
    chip = (2 * lax.axis_index("x") + lax.axis_index("y")).astype(jnp.int32).reshape(1)
    scatter_plan = lambda n: [(a, j, lambda frm, to: _chip(to), lambda frm, to: _chip(frm)) for a in range(n) for j in ICI_RELATIONS]
    scattering = {}

    swap_plan = lambda n: [(a, 1, functools.partial(lambda frm, to, q: 2 * q + to[2], q=q), functools.partial(lambda frm, to, q: q, q=q))
                           for a in range(n) for q in range(NDEV // 2)]
    swapping = {}

    def start_ici(grp, g8, recv):
        pairs = [_pair_sum(f"pair_sum_{grp}{i}", core, g, r, _row_tile(g.shape[1], 1024)) for i, (g, r) in enumerate(zip(g8, recv))]
        lands = [lax.empty(p.shape, p.dtype) for p in pairs]
        ss, rs, srcs, lands, tok = _push_start("scatter_start_" + grp, pairs, lands, scatter_plan(len(pairs)))
        scattering[grp] = (ss, rs, srcs, lands)
        return tok

    def put_g(grp, grads):
        g8 = [g if g.ndim == 3 else to8(g) for g in grads]
        if grp == "in":
            return start_ici(grp, g8, _sibling_swap("scatter_pair_" + grp, g8))
        lands = [lax.empty((NDEV // 2,) + g.shape[1:], g.dtype) for g in g8]
        ss, rs, srcs, lands, tok = _push_start("scatter_swap_" + grp, g8, lands, swap_plan(len(g8)))
        swapping[grp] = (ss, rs, srcs, lands)
        return tok

    def sent(grp, after):
        ss, rs, srcs, lands = swapping[grp]
        g8, recv = _push_wait("scatter_swapped_" + grp, ss, rs, srcs, lands, swap_plan(len(srcs)), after)
        return start_ici(grp, g8, recv)

    def reduced(grp, after, names):
        ss, rs, srcs, lands = scattering[grp]
        srcs, lands = _push_wait("scatter_wait_" + grp, ss, rs, srcs, lands, scatter_plan(len(srcs)), after)
        for i, (p, r, nm) in enumerate(zip(srcs, lands, names)):
            tr = nm in ("w_ffn_gate", "w_ffn_up")
            wmv = tuple(a[0].T if tr else a[0] for a in weights[nm])
            out = _chip_sum_adamw(f"chip_sum_{grp}{i}", chip, p, r, _row_tile(p.shape[1]), wmv)
            res[nm] = [(a.T if tr else a).reshape(weights[nm][0].shape) for a in out]

    dx, dmod, small = _local_step(x[0], loss_target[0], mod, _after_tokens(norm_mix_g, tokens), norm_ffn_g,
                                  norm_final_g.reshape(1, d), lb_param, gnorm_g, conv_w_full, project, get_w, prefetch, put_g, sent)

    pieces = [*dmod, small["g_mix"], small["g_ffn"], small["g_fin"], small["lb"], small["gnorm"], small["loss"],
              small["conv_w"].reshape(1, -1)]
    widths = [p.shape[1] for p in pieces]
    offs = np.concatenate([[0], np.cumsum(widths)])
    packed = jnp.concatenate(pieces, axis=1)
    gathered, = _all_gather("gather_small", [packed])
    summed = _sum_small(gathered)
    part = lambda i: summed[:, offs[i]:offs[i + 1]]
    g_b_ada = summed[:, :N_MOD * d]
    g_norm_mix, g_norm_ffn, g_norm_fin, g_lb_row, g_gnorm, loss_vec, g_convw_flat = [part(i) for i in range(N_MOD, N_MOD + 7)]
    loss = loss_vec[0, 0]
    dmod_all = gathered[:, 0, :N_MOD * d]
    g_w_ada = _ada_wgrad(c_all, lax.dynamic_slice_in_dim(dmod_all, me * ada_cols, ada_cols, axis=1))
    g_lb = _lb_grad(lb_param, g_lb_row)
    cw_cols = conv_w.shape[2]
    g_conv_w = lax.dynamic_slice_in_dim(g_convw_flat.reshape(conv_w.shape[1], -1), me * cw_cols, cw_cols, axis=1)

    grads = dict(w_ada=g_w_ada, b_ada=g_b_ada, norm_mix_g=g_norm_mix, conv_w=g_conv_w, lb_param=g_lb, gnorm_g=g_gnorm,
                 norm_ffn_g=g_norm_ffn, norm_final_g=g_norm_fin)
    weights = dict(w_ada=(w_ada, m_w_ada, v_w_ada), b_ada=(b_ada, m_b_ada, v_b_ada), norm_mix_g=(norm_mix_g, m_norm_mix_g, v_norm_mix_g),
                   w_in=(w_in, m_w_in, v_w_in), conv_w=(conv_w, m_conv_w, v_conv_w), lb_param=(lb_param, m_lb_param, v_lb_param),
                   gnorm_g=(gnorm_g, m_gnorm_g, v_gnorm_g), w_conv_out=(w_conv_out, m_w_conv_out, v_w_conv_out),
                   w_hgrn_out=(w_hgrn_out, m_w_hgrn_out, v_w_hgrn_out), w_o=(w_o, m_w_o, v_w_o),
                   norm_ffn_g=(norm_ffn_g, m_norm_ffn_g, v_norm_ffn_g), w_ffn_gate=(w_ffn_gate, m_w_ffn_gate, v_w_ffn_gate),
                   w_ffn_up=(w_ffn_up, m_w_ffn_up, v_w_ffn_up), w_ffn_down=(w_ffn_down, m_w_ffn_down, v_w_ffn_down),
                   norm_final_g=(norm_final_g, m_norm_final_g, v_norm_final_g))
    res = {}

    def update(nm):
        w, m, v = weights[nm]
        shape2 = (w.shape[-2], w.shape[-1]) if w.ndim >= 2 else (1, w.shape[0])
        g2 = grads[nm].reshape(shape2)
        dl, mn, vn = _adamw("adamw_" + nm, g2, w.reshape(shape2), m.reshape(shape2), v.reshape(shape2))
        res[nm] = [a.reshape(w.shape) for a in (g2, dl, mn, vn)]

    for nm in list(grads):
        update(nm)
    reduced("ffn", res["w_ada"][1], ("w_ffn_gate", "w_ffn_up", "w_ffn_down"))
    reduced("mix", res["w_ffn_down"][1], ("w_conv_out", "w_hgrn_out", "w_o"))
    reduced("in", res["w_o"][1], ("w_in",))
    outs = [[res[nm][i] for nm in weights] for i in range(4)]
    return (loss, dx.reshape(x.shape), *outs[0], *outs[1], *outs[2], *outs[3])
```

```python
import functools

import numpy as np
import jax
import jax.numpy as jnp
from jax import lax
from jax.experimental import pallas as pl
from jax.experimental.pallas import tpu as pltpu

f32, bf16 = jnp.float32, jnp.bfloat16
SDS = jax.ShapeDtypeStruct

EPS = 1e-6
HEADS, DK, CHUNK = 8, 128, 64
HEADS_PER_STEP = 4
N_MOD = 6
NDEV = 8
ADAM_LR, ADAM_B1, ADAM_B2, ADAM_EPS, ADAM_WD, ADAM_STEP = 0.001, 0.9, 0.999, 1e-08, 0.01, 10
LEVELS = (32, 16, 8, 4, 2, 1)
V7X_VMEM_LIMIT = 56 * 1024 * 1024
HBM = pl.BlockSpec(memory_space=pltpu.HBM)
MESH = pl.DeviceIdType.MESH


def _params(sem):
    return pltpu.CompilerParams(dimension_semantics=sem, vmem_limit_bytes=V7X_VMEM_LIMIT)


def _sigmoid(x):
    return jax.nn.sigmoid(x)


def _dsilu(x, s):
    return s * (1.0 + x * (1.0 - s))


def _mesh_pos():
    x, y, c = lax.axis_index("x"), lax.axis_index("y"), lax.axis_index("c")
    return x, y, c


def _peer(pos, j):
    x, y, c = pos
    return (1 - x if j & 4 else x, 1 - y if j & 2 else y, 1 - c if j & 1 else c)


def _flat(pos):
    return 4 * pos[0] + 2 * pos[1] + pos[2]


def _all_gather(name, arrs, after=()):
    n, ne = len(arrs), len(after)
    out_shapes = [SDS((NDEV,) + a.shape, a.dtype) for a in arrs]

    def body(*refs):
        ins, outs = refs[:n], refs[n + ne:2 * n + ne]
        send_sems, recv_sems, local_sems = refs[2 * n + ne:]
        pos = _mesh_pos()
        me = _flat(pos)

        def copy(a, j, frm, to_pos):
            k = a * (NDEV - 1) + j - 1
            return pltpu.make_async_remote_copy(src_ref=ins[a], dst_ref=outs[a].at[frm], send_sem=send_sems.at[k],
                                                recv_sem=recv_sems.at[k], device_id=to_pos, device_id_type=MESH)

        local = [pltpu.make_async_copy(ins[a], outs[a].at[me], local_sems.at[a]) for a in range(n)]
        for cp in local:
            cp.start()
        sends = [copy(a, j, me, _peer(pos, j)) for j in range(1, NDEV) for a in range(n)]
        for cp in sends:
            cp.start()
        for j in range(1, NDEV):
            for a in range(n):
                copy(a, j, _flat(_peer(pos, j)), pos).wait_recv()
        for cp in sends:
            cp.wait_send()
        for cp in local:
            cp.wait()

    return pl.pallas_call(
        body, name=name, out_shape=out_shapes, in_specs=[HBM] * n + [ANY] * ne, out_specs=[HBM] * n,
        scratch_shapes=[pltpu.SemaphoreType.DMA((n * (NDEV - 1),)), pltpu.SemaphoreType.DMA((n * (NDEV - 1),)),
                        pltpu.SemaphoreType.DMA((n,))],
    )(*arrs, *after)


SEM = pl.BlockSpec(memory_space=pltpu.SEMAPHORE)
ANY = pl.BlockSpec(memory_space=pl.ANY)
EFFECT = pltpu.SideEffectType.DATAFLOW_SIDE_EFFECTING
ICI_RELATIONS = (2, 4, 6)


def _chip(pos):
    return 2 * pos[0] + pos[1]


def _hbm(a):
    return pltpu.with_memory_space_constraint(a, pltpu.HBM)


def _plan_copy(plan_entry, k, pos, frm, to, src_refs, land_refs, send_sems, recv_sems):
    a, _, src_slot, dst_slot = plan_entry
    s = src_refs[a] if src_slot is None else src_refs[a].at[src_slot(frm, to)]
    return pltpu.make_async_remote_copy(src_ref=s, dst_ref=land_refs[a].at[dst_slot(frm, to)], send_sem=send_sems.at[k],
                                        recv_sem=recv_sems.at[k], device_id=to, device_id_type=MESH)


def _push_start(name, srcs, lands, plan, after=None):
    ns, nb, nk = len(srcs), len(srcs) + len(lands), len(plan)
    extra = [] if after is None else [after]

    def body(*refs):
        land_refs = refs[ns:nb]
        src_refs = refs[:ns] if ns else land_refs
        send_sems, recv_sems = refs[nb + len(extra)], refs[nb + len(extra) + 1]
        pos = _mesh_pos()
        for k, e in enumerate(plan):
            _plan_copy(e, k, pos, pos, _peer(pos, e[1]), src_refs, land_refs, send_sems, recv_sems).start()
        refs[-1][...] = jnp.zeros_like(refs[-1])

    outs = pl.pallas_call(
        body, name=name,
        out_shape=[pltpu.SemaphoreType.DMA((nk,)), pltpu.SemaphoreType.DMA((nk,))] + [pltpu.HBM(a.shape, a.dtype) for a in srcs + lands]
        + [SDS((8, 128), f32)],
        in_specs=[HBM] * nb + [ANY] * len(extra), out_specs=[SEM, SEM] + [HBM] * nb + [pl.BlockSpec(memory_space=pltpu.VMEM)],
        input_output_aliases={i: 2 + i for i in range(nb)},
        compiler_params=pltpu.CompilerParams(has_side_effects=EFFECT),
    )(*[_hbm(a) for a in srcs + lands], *extra)
    return outs[0], outs[1], list(outs[2:2 + ns]), list(outs[2 + ns:2 + nb]), outs[-1]


def _push_wait(name, send_sems, recv_sems, srcs, lands, plan, after):
    ns, nb = len(srcs), len(srcs) + len(lands)

    def body(*refs):
        land_refs = refs[ns:nb]
        src_refs = refs[:ns] if ns else land_refs
        ssem, rsem = refs[nb], refs[nb + 1]
        pos = _mesh_pos()
        for k, e in enumerate(plan):
            peer = _peer(pos, e[1])
            _plan_copy(e, k, pos, pos, peer, src_refs, land_refs, ssem, rsem).wait_send()
            _plan_copy(e, k, pos, peer, pos, src_refs, land_refs, ssem, rsem).wait_recv()

    outs = pl.pallas_call(
        body, name=name, out_shape=[pltpu.HBM(a.shape, a.dtype) for a in srcs + lands],
        in_specs=[HBM] * nb + [SEM, SEM, ANY], out_specs=[HBM] * nb,
        input_output_aliases={i: i for i in range(nb)},
        compiler_params=pltpu.CompilerParams(has_side_effects=EFFECT),
    )(*srcs, *lands, send_sems, recv_sems, after)
    return list(outs[:ns]), list(outs[ns:])


def _after_tokens(small, tokens):
    for t in tokens:
        if t is not None:
            small = small + t[0:1, 0:1].reshape((1,) * small.ndim)
    return small


def _place_shard(name, me, shard, after=None):
    r, c = shard.shape
    tr = _row_tile(r)
    extra = [] if after is None else [after]

    def body(me_ref, s_ref, *rest):
        rest[-2][...] = s_ref[...].astype(bf16)
        rest[-1][...] = jnp.zeros_like(rest[-1])

    return pl.pallas_call(
        body, name=name, out_shape=[SDS((NDEV, r, c), bf16), SDS((8, 128), f32)],
        grid_spec=pltpu.PrefetchScalarGridSpec(
            num_scalar_prefetch=1, grid=(r // tr,), in_specs=[pl.BlockSpec((tr, c), lambda i, me_ref: (i, 0))] + [ANY] * len(extra),
            out_specs=[pl.BlockSpec((None, tr, c), lambda i, me_ref: (me_ref[0], i, 0)),
                       pl.BlockSpec((8, 128), lambda i, me_ref: (0, 0))]),
        compiler_params=_params(("arbitrary",)),
    )(me, shard, *extra)


def _row_tile(r, most=256):
    return max(t for t in range(16, most + 1, 16) if r % t == 0)


def _sibling_swap(name, grads):
    n = len(grads)
    nchip = NDEV // 2

    def body(*refs):
        g_refs, out_refs = refs[:n], refs[n:2 * n]
        send_sems, recv_sems = refs[2 * n:]
        pos = _mesh_pos()
        sib = _peer(pos, 1)
        sends = []
        for a in range(n):
            for q in range(nchip):
                k = a * nchip + q
                sends.append(pltpu.make_async_remote_copy(src_ref=g_refs[a].at[2 * q + sib[2]], dst_ref=out_refs[a].at[q],
                                                          send_sem=send_sems.at[k], recv_sem=recv_sems.at[k], device_id=sib,
                                                          device_id_type=MESH))
        for cp in sends:
            cp.start()
        for cp in sends:
            cp.wait()

    return pl.pallas_call(
        body, name=name, out_shape=[SDS((nchip,) + a.shape[1:], a.dtype) for a in grads], in_specs=[HBM] * n, out_specs=[HBM] * n,
        scratch_shapes=[pltpu.SemaphoreType.DMA((n * nchip,)), pltpu.SemaphoreType.DMA((n * nchip,))],
    )(*grads)


def _pair_sum(name, core, grad, recv, tr):
    _, r, c = grad.shape
    nchip = NDEV // 2

    def body(core_ref, g_ref, r_ref, o_ref):
        o_ref[...] = (g_ref[...].astype(f32) + r_ref[...].astype(f32)).astype(o_ref.dtype)

    return pl.pallas_call(
        body, name=name, out_shape=SDS((nchip, r, c), grad.dtype),
        grid_spec=pltpu.PrefetchScalarGridSpec(
            num_scalar_prefetch=1, grid=(nchip, r // tr),
            in_specs=[pl.BlockSpec((None, tr, c), lambda q, i, core_ref: (2 * q + core_ref[0], i, 0)),
                      pl.BlockSpec((None, tr, c), lambda q, i, core_ref: (q, i, 0))],
            out_specs=pl.BlockSpec((None, tr, c), lambda q, i, core_ref: (q, i, 0))),
        compiler_params=_params(("parallel", "parallel")),
    )(core, grad, recv)


def _adamw_math(g, w, m, v):
    mn = ADAM_B1 * m + (1.0 - ADAM_B1) * g
    vn = ADAM_B2 * v + (1.0 - ADAM_B2) * jnp.square(g)
    m_hat = mn / (1.0 - ADAM_B1 ** ADAM_STEP)
    v_hat = vn / (1.0 - ADAM_B2 ** ADAM_STEP)
    return -ADAM_LR * (m_hat / (jnp.sqrt(v_hat) + ADAM_EPS) + ADAM_WD * w), mn, vn


def _chip_sum_adamw(name, chip, pairs, recv, tr, wmv):
    nchip, r, c = pairs.shape

    def body(chip_ref, p_ref, r_ref, w_ref, m_ref, v_ref, g_ref, d_ref, mo_ref, vo_ref):
        mine = chip_ref[0]
        own = p_ref[...].astype(f32)
        acc = jnp.zeros((tr, c), f32)
        for q in range(nchip):
            acc = acc + jnp.where(mine == q, own, r_ref[q].astype(f32))
        g_ref[...] = acc
        d_ref[...], mo_ref[...], vo_ref[...] = _adamw_math(acc, w_ref[...], m_ref[...], v_ref[...])

    blk = pl.BlockSpec((tr, c), lambda i, chip_ref: (i, 0))
    return pl.pallas_call(
        body, name=name, out_shape=[SDS((r, c), f32)] * 4,
        grid_spec=pltpu.PrefetchScalarGridSpec(
            num_scalar_prefetch=1, grid=(r // tr,),
            in_specs=[pl.BlockSpec((None, tr, c), lambda i, chip_ref: (chip_ref[0], i, 0)),
                      pl.BlockSpec((nchip, tr, c), lambda i, chip_ref: (0, i, 0))] + [blk] * 3,
            out_specs=[blk] * 4),
        compiler_params=_params(("parallel",)),
    )(chip, pairs, recv, *wmv)


def _matmul(name, a, b, out_shape, out_dtype, grid, a_spec, b_spec, o_spec, dims, after=None):
    ksteps = grid[2]
    acc_shape = tuple(d for d in o_spec.block_shape if d is not None)
    extra = [] if after is None else [after]

    def body(a_ref, b_ref, *rest):
        o_ref, acc = rest[len(extra)], rest[len(extra) + 1:]
        prod = lax.dot_general(a_ref[...], b_ref[...], (dims, ((), ())), preferred_element_type=f32)
        if ksteps == 1:
            o_ref[...] = prod.astype(o_ref.dtype)
        else:
            k = pl.program_id(2)

            @pl.when(k == 0)
            def _():
                acc[0][...] = prod

            @pl.when(k > 0)
            def _():
                acc[0][...] += prod

            @pl.when(k == ksteps - 1)
            def _():
                o_ref[...] = acc[0][...].astype(o_ref.dtype)

    return pl.pallas_call(
        body, name=name, grid=grid, in_specs=[a_spec, b_spec] + [ANY] * len(extra), out_specs=o_spec,
        out_shape=SDS(out_shape, out_dtype), scratch_shapes=[] if ksteps == 1 else [pltpu.VMEM(acc_shape, f32)],
        compiler_params=_params(("parallel", "parallel", "arbitrary")),
    )(a, b, *extra)


NN, NT, TN = ((1,), (0,)), ((1,), (1,)), ((0,), (0,))


def _concat_columns(name, pieces, tm):
    s = pieces[0].shape[0]
    widths = [p.shape[1] for p in pieces]
    offs = [sum(widths[:i]) for i in range(len(widths))]

    def body(*refs):
        o_ref = refs[-1]
        for r, o, w in zip(refs[:-1], offs, widths):
            o_ref[:, o:o + w] = r[...]

    return pl.pallas_call(
        body, name=name, grid=(s // tm,), in_specs=[pl.BlockSpec((tm, w), lambda i: (i, 0)) for w in widths],
        out_specs=pl.BlockSpec((tm, sum(widths)), lambda i: (i, 0)), out_shape=SDS((s, sum(widths)), pieces[0].dtype),
        compiler_params=_params(("parallel",)),
    )(*pieces)


def _d_proj_in(dproj, w_sh, tm, after):
    s = dproj.shape[0]
    nsh, d, n = w_sh.shape
    ksteps = nsh // 2
    extra = [] if after is None else [after]

    def body(a_ref, b_ref, *rest):
        o_ref, acc = rest[len(extra)], rest[len(extra) + 1]
        k = pl.program_id(1)
        a = a_ref[...]
        prod = lax.dot_general(a[:, :n], b_ref[0], (NT, ((), ())), preferred_element_type=f32)
        prod += lax.dot_general(a[:, n:], b_ref[1], (NT, ((), ())), preferred_element_type=f32)

        @pl.when(k == 0)
        def _():
            acc[...] = prod

        @pl.when(k > 0)
        def _():
            acc[...] += prod

        @pl.when(k == ksteps - 1)
        def _():
            o_ref[...] = acc[...].astype(o_ref.dtype)

    return pl.pallas_call(
        body, name="d_proj_in", grid=(s // tm, ksteps),
        in_specs=[pl.BlockSpec((tm, 2 * n), lambda i, k: (i, k)), pl.BlockSpec((2, d, n), lambda i, k: (k, 0, 0))] + [ANY] * len(extra),
        out_specs=pl.BlockSpec((tm, d), lambda i, k: (i, 0)), out_shape=SDS((s, d), bf16),
        scratch_shapes=[pltpu.VMEM((tm, d), f32)], compiler_params=_params(("parallel", "arbitrary")),
    )(dproj, w_sh, *extra)


def _proj_part(name, h, w_sh, ids, prev, tm, after=None):
    s, d = h.shape
    nsh, _, n = w_sh.shape
    extra = ([] if prev is None else [prev]) + ([] if after is None else [after])

    def body(ids_ref, a_ref, b_ref, *rest):
        rest[-1][...] = jnp.dot(a_ref[...], b_ref[...], preferred_element_type=f32)

    return pl.pallas_call(
        body, name=name, out_shape=SDS((s, nsh * n), f32),
        grid_spec=pltpu.PrefetchScalarGridSpec(
            num_scalar_prefetch=1, grid=(ids.shape[0], s // tm),
            in_specs=[pl.BlockSpec((tm, d), lambda j, i, ids_ref: (i, 0)),
                      pl.BlockSpec((None, d, n), lambda j, i, ids_ref: (ids_ref[j], 0, 0))] + [ANY] * len(extra),
            out_specs=pl.BlockSpec((tm, n), lambda j, i, ids_ref: (i, ids_ref[j]))),
        input_output_aliases={} if prev is None else {3: 0},
        compiler_params=_params(("parallel", "parallel")),
    )(ids, h, w_sh, *extra)


def _rowwise(name, fn, n_rows, tm, rows, fulls, row_outs, acc_outs=(), ncol=1):
    assert ncol == 1 or not acc_outs
    nr, nf, no, na = len(rows), len(fulls), len(row_outs), len(acc_outs)
    in_specs = [pl.BlockSpec((tm, w), functools.partial(lambda i, j, cb: (i, cb + j), cb=cb)) for (_, w, cb) in rows]
    in_specs += [pl.BlockSpec(a.shape, functools.partial(lambda i, j, nd: (0,) * nd, nd=a.ndim)) for a in fulls]
    out_shape = [SDS((n_rows, w), dt) for (w, dt) in row_outs] + [SDS(s, f32) for s in acc_outs]
    out_specs = [pl.BlockSpec((tm, w // ncol), lambda i, j: (i, j)) for (w, _) in row_outs]
    out_specs += [pl.BlockSpec(s, functools.partial(lambda i, j, nd: (0,) * nd, nd=len(s))) for s in acc_outs]

    def body(*refs):
        if na:
            @pl.when(pl.program_id(0) == 0)
            def _():
                for r in refs[nr + nf + no:]:
                    r[...] = jnp.zeros(r.shape, r.dtype)
        fn(refs[:nr], refs[nr:nr + nf], refs[nr + nf:nr + nf + no], refs[nr + nf + no:])

    return pl.pallas_call(
        body, name=name, grid=(n_rows // tm, ncol), in_specs=in_specs, out_specs=out_specs, out_shape=out_shape,
        compiler_params=_params(("arbitrary" if na else "parallel", "arbitrary" if na else "parallel")),
    )(*[r[0] for r in rows], *fulls)


def _rms(x):
    r = lax.rsqrt(jnp.mean(x * x, axis=-1, keepdims=True) + EPS)
    return r, x * r


def _colsum(v):
    return jnp.sum(v, axis=0, keepdims=True)


def _shift_down(u, row, k):
    return jnp.where(row >= k, pltpu.roll(u, k, 0), 0.0)


def _shift_up(u, row, k):
    n = u.shape[0]
    return jnp.where(row < n - k, pltpu.roll(u, n - k, 0), 0.0)


def _conv_fwd(proj, conv_w, s, dc):
    nb = dc // 128

    def body(ab_ref, ac_ref, ax_ref, w_ref, z_ref):
        u = ac_ref[...] * ax_ref[...]
        row = lax.broadcasted_iota(jnp.int32, u.shape, 0)
        w = w_ref[...]
        cv = w[0:1] * _shift_down(u, row, 2) + w[1:2] * _shift_down(u, row, 1) + w[2:3] * u
        z_ref[...] = (ab_ref[...] * cv).astype(z_ref.dtype)

    col = lambda off: pl.BlockSpec((s, 128), functools.partial(lambda j, off: (0, off + j), off=off))
    return pl.pallas_call(
        body, name="conv_fwd", grid=(nb,), in_specs=[col(0), col(nb), col(2 * nb), pl.BlockSpec((3, 128), lambda j: (0, j))],
        out_specs=pl.BlockSpec((s, 128), lambda j: (0, j)), out_shape=SDS((s, dc), bf16), compiler_params=_params(("parallel",)),
    )(proj, proj, proj, conv_w)


def _conv_bwd(proj, conv_w, dz, s, dc):
    nb = dc // 128

    def body(ab_ref, ac_ref, ax_ref, w_ref, dz_ref, dab_ref, dac_ref, dax_ref, dw_ref):
        ab, ac, ax, dzv = ab_ref[...], ac_ref[...], ax_ref[...], dz_ref[...].astype(f32)
        u = ac * ax
        row = lax.broadcasted_iota(jnp.int32, u.shape, 0)
        w = w_ref[...]
        u1, u2 = _shift_down(u, row, 1), _shift_down(u, row, 2)
        cv = w[0:1] * u2 + w[1:2] * u1 + w[2:3] * u
        dcv = dzv * ab
        dab_ref[...] = (dzv * cv).astype(dab_ref.dtype)
        du = w[2:3] * dcv + w[1:2] * _shift_up(dcv, row, 1) + w[0:1] * _shift_up(dcv, row, 2)
        dac_ref[...] = (du * ax).astype(dac_ref.dtype)
        dax_ref[...] = (du * ac).astype(dax_ref.dtype)
        dw_ref[0:1, :] = _colsum(dcv * u2)
        dw_ref[1:2, :] = _colsum(dcv * u1)
        dw_ref[2:3, :] = _colsum(dcv * u)

    col = lambda off: pl.BlockSpec((s, 128), functools.partial(lambda j, off: (0, off + j), off=off))
    blk = pl.BlockSpec((s, 128), lambda j: (0, j))
    return pl.pallas_call(
        body, name="conv_bwd", grid=(nb,),
        in_specs=[col(0), col(nb), col(2 * nb), pl.BlockSpec((3, 128), lambda j: (0, j)), blk],
        out_specs=[blk, blk, blk, pl.BlockSpec((3, 128), lambda j: (0, j))],
        out_shape=[SDS((s, dc), bf16)] * 3 + [SDS((3, dc), f32)], compiler_params=_params(("parallel",)),
    )(proj, proj, proj, conv_w, dz)


def _level_masks():
    t = np.arange(CHUNK)[:, None]
    s = np.arange(CHUNK)[None, :]
    m = np.stack([((t & h) != 0) & ((s & h) == 0) & (t // (2 * h) == s // (2 * h)) for h in LEVELS]).astype(np.float32)
    return jnp.asarray(m), jnp.asarray(m.transpose(0, 2, 1))


def _cumsum_rows(x, row):
    for sh in (1, 2, 4, 8, 16, 32):
        x = x + jnp.where(row >= sh, pltpu.roll(x, sh, 0), 0.0)
    return x


def _rev_cumsum_rows(x, row):
    n = x.shape[0]
    for sh in (1, 2, 4, 8, 16, 32):
        x = x + jnp.where(row < n - sh, pltpu.roll(x, n - sh, 0), 0.0)
    return x


def _chunk_terms(qp, fl, lb):
    row = lax.broadcasted_iota(jnp.int32, qp.shape, 0)
    sig = _sigmoid(fl)
    f = lb + (1.0 - lb) * sig
    k = 1.0 - f
    sq = _sigmoid(qp)
    qh = qp * sq
    b = _cumsum_rows(jnp.log(f), row)
    sub = lax.broadcasted_iota(jnp.int32, (CHUNK // 8, 8, DK), 1)
    b8 = b.reshape(CHUNK // 8, 8, DK)
    us, exs, ups = [], [], []
    for m in LEVELS:
        sb = 2 * m
        if sb >= 8:
            b3 = b.reshape(CHUNK // sb, sb, DK)
            bref = jnp.broadcast_to(b3[:, m - 1:m, :], b3.shape).reshape(CHUNK, DK)
        else:
            bref8 = None
            for j in range(8 // sb):
                cand = jnp.broadcast_to(b8[:, j * sb + m - 1:j * sb + m, :], b8.shape)
                bref8 = cand if bref8 is None else jnp.where(sub >= j * sb, cand, bref8)
            bref = bref8.reshape(CHUNK, DK)
        up = (row & m) != 0
        ex = jnp.exp(jnp.where(up, b - bref, bref - b))
        us.append((jnp.where(up, qh, k) * ex).astype(bf16))
        exs.append(ex)
        ups.append(up)
    blast = b[CHUNK - 1:CHUNK, :]
    eb, ebl = jnp.exp(b), jnp.exp(blast - b)
    return dict(sig=sig, f=f, k=k, sq=sq, qh=qh, u=jnp.stack(us), ex=exs, up=ups, eb=eb, ebl=ebl, qt=qh * eb, kt=k * ebl,
                el=jnp.exp(blast), row=row)


def _scores(t, mask):
    pl_ = jnp.einsum("ltk,lsk->lts", t["u"], t["u"], preferred_element_type=f32)
    p = jnp.sum(pl_ * mask, axis=0)
    r = lax.broadcasted_iota(jnp.int32, (CHUNK, CHUNK), 0)
    c = lax.broadcasted_iota(jnp.int32, (CHUNK, CHUNK), 1)
    diag = jnp.sum(t["qh"] * t["k"], axis=-1, keepdims=True)
    return p + jnp.where(r == c, diag, 0.0)


def _hgrn_fwd(proj, lb_param, gnorm, s, dv_total, tb):
    nchunk = tb // CHUNK
    masks, _ = _level_masks()
    q0, f0, v0, g0 = 3 * HEADS, 4 * HEADS, 5 * HEADS, 6 * HEADS

    def body(q_ref, f_ref, v_ref, g_ref, lb_ref, gn_ref, mask_ref, og_ref, o_ref, st_ref, state):
        @pl.when(pl.program_id(1) == 0)
        def _():
            state[...] = jnp.zeros_like(state)

        lbp = lb_ref[...]
        lb_all = _sigmoid(lbp[0:1, :] - lbp[1:2, :])
        mask = mask_ref[...]
        for i, hh in [(i, hh) for i in range(nchunk) for hh in range(HEADS_PER_STEP)]:
            rs, cs = pl.ds(i * CHUNK, CHUNK), pl.ds(hh * DK, DK)
            t = _chunk_terms(q_ref[rs, cs], f_ref[rs, cs], lb_all[:, hh * DK:(hh + 1) * DK])
            v = v_ref[rs, cs]
            vb = v.astype(bf16)
            st = state[hh]
            st_ref[i, hh] = st
            p = _scores(t, mask)
            o = jnp.dot(p.astype(bf16), vb, preferred_element_type=f32)
            o += lax.dot_general(t["qt"].astype(bf16), st.astype(bf16), (NT, ((), ())), preferred_element_type=f32)
            state[hh] = st * t["el"] + lax.dot_general(vb, t["kt"].astype(bf16), (TN, ((), ())), preferred_element_type=f32)
            o_ref[rs, cs] = o
            r, oh = _rms(o)
            g = g_ref[rs, cs]
            og_ref[rs, cs] = (oh * gn_ref[...] * (g * _sigmoid(g))).astype(og_ref.dtype)

    hp, wide = HEADS_PER_STEP, HEADS_PER_STEP * DK
    col = lambda off: pl.BlockSpec((tb, wide), functools.partial(lambda h, t, off: (t, off + h), off=off // hp))
    blk = pl.BlockSpec((tb, wide), lambda h, t: (t, h))
    return pl.pallas_call(
        body, name="hgrn_fwd", grid=(HEADS // hp, s // tb),
        in_specs=[col(q0), col(f0), col(v0), col(g0), pl.BlockSpec((2, wide), lambda h, t: (0, h)),
                  pl.BlockSpec((1, DK), lambda h, t: (0, 0)), pl.BlockSpec(masks.shape, lambda h, t: (0, 0, 0))],
        out_specs=[blk, blk, pl.BlockSpec((nchunk, hp, DK, DK), lambda h, t: (t, h, 0, 0))],
        out_shape=[SDS((s, dv_total), bf16), SDS((s, dv_total), f32), SDS((s // CHUNK, HEADS, DK, DK), f32)],
        scratch_shapes=[pltpu.VMEM((hp, DK, DK), f32)], compiler_params=_params(("parallel", "arbitrary")),
    )(proj, proj, proj, proj, lb_param, gnorm, masks)


def _hgrn_bwd(proj, lb_param, gnorm, o_saved, states, dog, s, dv_total, tb):
    nchunk = tb // CHUNK
    nt = s // tb
    nc_total = s // CHUNK
    masks, masks_t = _level_masks()
    q0, f0, v0, g0 = 3 * HEADS, 4 * HEADS, 5 * HEADS, 6 * HEADS

    def body(q_ref, f_ref, v_ref, g_ref, lb_ref, gn_ref, mask_ref, maskt_ref, o_ref, dog_ref, st_ref, stn_ref,
             dq_ref, df_ref, dv_ref, dg_ref, dlb_ref, dgn_ref, gstate):
        h_id, t_id = pl.program_id(0), pl.program_id(1)

        @pl.when(t_id == 0)
        def _():
            gstate[...] = jnp.zeros_like(gstate)
            dlb_ref[...] = jnp.zeros_like(dlb_ref)

        @pl.when((t_id == 0) & (h_id == 0))
        def _():
            dgn_ref[...] = jnp.zeros_like(dgn_ref)

        lbp = lb_ref[...]
        lb_all = _sigmoid(lbp[0:1, :] - lbp[1:2, :])
        mask, maskt = mask_ref[...], maskt_ref[...]
        gn = gn_ref[...]
        for i, hh in [(i, hh) for i in reversed(range(nchunk)) for hh in range(HEADS_PER_STEP)]:
            rs, cs = pl.ds(i * CHUNK, CHUNK), pl.ds(hh * DK, DK)
            lb = lb_all[:, hh * DK:(hh + 1) * DK]
            qp, fl, v, g = q_ref[rs, cs], f_ref[rs, cs], v_ref[rs, cs], g_ref[rs, cs]
            t = _chunk_terms(qp, fl, lb)
            vb = v.astype(bf16)
            st0 = st_ref[i, hh]
            st1 = st_ref[i + 1, hh] if i + 1 < nchunk else stn_ref[0, hh]
            gt = gstate[hh]
            o = o_ref[rs, cs]
            r, oh = _rms(o)
            sg = _sigmoid(g)
            dog_v = dog_ref[rs, cs].astype(f32)
            dg_ref[rs, cs] = (dog_v * (oh * gn) * _dsilu(g, sg)).astype(dg_ref.dtype)
            don = dog_v * (g * sg)
            dgn_ref[...] += _colsum(don * oh)
            doh = don * gn
            do = r * (doh - oh * jnp.mean(doh * oh, axis=-1, keepdims=True))
            dob = do.astype(bf16)
            d = lax.dot_general(dob, vb, (NT, ((), ())), preferred_element_type=f32)
            dt = lax.dot_general(vb, dob, (NT, ((), ())), preferred_element_type=f32)
            z = (mask * d[None] + maskt * dt[None]).astype(bf16)
            rr = jnp.einsum("lts,lsk->ltk", z, t["u"], preferred_element_type=f32)
            dq = jnp.zeros((CHUNK, DK), f32)
            dk = jnp.zeros((CHUNK, DK), f32)
            qdk = jnp.zeros((CHUNK, DK), f32)
            for li in range(len(LEVELS)):
                du = t["ex"][li] * rr[li]
                dq += jnp.where(t["up"][li], du, 0.0)
                dk += jnp.where(t["up"][li], 0.0, du)
                e = t["u"][li].astype(f32) * rr[li]
                qdk += jnp.where(t["up"][li], e, -e)
            dd = jnp.sum(do * v, axis=-1, keepdims=True)
            dq += dd * t["k"]
            dk += dd * t["qh"]
            gtb = gt.astype(bf16)
            ktb, qtb = t["kt"].astype(bf16), t["qt"].astype(bf16)
            dq_in = jnp.dot(dob, st0.astype(bf16), preferred_element_type=f32)
            dk_in = jnp.dot(vb, gtb, preferred_element_type=f32)
            dq += t["eb"] * dq_in
            dk += t["ebl"] * dk_in
            qdk += qtb.astype(f32) * dq_in - ktb.astype(f32) * dk_in
            p = _scores(t, mask)
            dvv = lax.dot_general(p.astype(bf16), dob, (TN, ((), ())), preferred_element_type=f32)
            dvv += lax.dot_general(ktb, gtb, (NT, ((), ())), preferred_element_type=f32)
            dv_ref[rs, cs] = dvv.astype(dv_ref.dtype)
            a_end = _colsum(gtb.astype(f32) * st1)
            dlf = _rev_cumsum_rows(qdk, t["row"]) + a_end
            dfv = dlf / t["f"] - dk
            df_ref[rs, cs] = (dfv * (1.0 - lb) * t["sig"] * (1.0 - t["sig"])).astype(df_ref.dtype)
            dlb_ref[:, cs] += _colsum(dfv * (1.0 - t["sig"]))
            dq_ref[rs, cs] = (dq * _dsilu(qp, t["sq"])).astype(dq_ref.dtype)
            gstate[hh] = gt * t["el"] + lax.dot_general(dob, qtb, (TN, ((), ())), preferred_element_type=f32)

    hp, wide = HEADS_PER_STEP, HEADS_PER_STEP * DK
    rev = lambda t: nt - 1 - t
    col = lambda off: pl.BlockSpec((tb, wide), functools.partial(lambda h, t, off: (rev(t), off + h), off=off // hp))
    blk = pl.BlockSpec((tb, wide), lambda h, t: (rev(t), h))
    nxt = lambda h, t: (jnp.minimum((rev(t) + 1) * nchunk, nc_total - 1), h, 0, 0)
    return pl.pallas_call(
        body, name="hgrn_bwd", grid=(HEADS // hp, nt),
        in_specs=[col(q0), col(f0), col(v0), col(g0), pl.BlockSpec((2, wide), lambda h, t: (0, h)),
                  pl.BlockSpec((1, DK), lambda h, t: (0, 0)), pl.BlockSpec(masks.shape, lambda h, t: (0, 0, 0)),
                  pl.BlockSpec(masks.shape, lambda h, t: (0, 0, 0)), blk, blk,
                  pl.BlockSpec((nchunk, hp, DK, DK), lambda h, t: (rev(t), h, 0, 0)),
                  pl.BlockSpec((1, hp, DK, DK), nxt)],
        out_specs=[blk, blk, blk, blk, pl.BlockSpec((1, wide), lambda h, t: (0, h)), pl.BlockSpec((1, DK), lambda h, t: (0, 0))],
        out_shape=[SDS((s, dv_total), bf16)] * 4 + [SDS((1, HEADS * DK), f32), SDS((1, DK), f32)],
        scratch_shapes=[pltpu.VMEM((hp, DK, DK), f32)], compiler_params=_params(("arbitrary", "arbitrary")),
    )(proj, proj, proj, proj, lb_param, gnorm, masks, masks_t, o_saved, dog, states, states)


def _adamw(name, g, w, m, v):
    r, c = w.shape
    tr = r
    for cand in (256, 128, 64, 32, 16, 8):
        if r % cand == 0 and r > cand:
            tr = cand
            break

    def body(g_ref, w_ref, m_ref, v_ref, d_ref, mo_ref, vo_ref):
        d_ref[...], mo_ref[...], vo_ref[...] = _adamw_math(g_ref[...], w_ref[...], m_ref[...], v_ref[...])

    blk = pl.BlockSpec((tr, c), lambda i: (i, 0))
    return pl.pallas_call(
        body, name=name, grid=(r // tr,), in_specs=[blk] * 4, out_specs=[blk] * 3, out_shape=[SDS((r, c), f32)] * 3,
        compiler_params=_params(("parallel",)),
    )(g, w, m, v)


def _ffn_in(h2, w_gt, w_ut, tm, ffb):
    s, d = h2.shape
    dff = w_gt.shape[0]

    def body(a_ref, wg_ref, wu_ref, dg_ref, du_ref, act_ref):
        a = a_ref[...]
        g = lax.dot_general(a, wg_ref[...], (NT, ((), ())), preferred_element_type=f32)
        u = lax.dot_general(a, wu_ref[...], (NT, ((), ())), preferred_element_type=f32)
        sg = _sigmoid(g)
        silu = g * sg
        dg_ref[...] = (u * _dsilu(g, sg)).astype(bf16)
        du_ref[...] = silu.astype(bf16)
        act_ref[...] = (silu * u).astype(bf16)

    w_spec = pl.BlockSpec((ffb, d), lambda j, i: (j, 0))
    o_spec = pl.BlockSpec((tm, ffb), lambda j, i: (i, j))
    return pl.pallas_call(
        body, name="ffn_in", grid=(dff // ffb, s // tm), in_specs=[pl.BlockSpec((tm, d), lambda j, i: (i, 0)), w_spec, w_spec],
        out_specs=[o_spec] * 3, out_shape=[SDS((s, dff), bf16)] * 3, compiler_params=_params(("parallel", "parallel")),
    )(h2, w_gt, w_ut)


def _ffn_down_bwd(dff_out, w_d, act_dg, act_du, tm, ffb):
    s, d = dff_out.shape
    dff = w_d.shape[0]

    def body(a_ref, w_ref, fg_ref, fu_ref, dg_ref, du_ref):
        da = lax.dot_general(a_ref[...], w_ref[...], (NT, ((), ())), preferred_element_type=f32)
        dg_ref[...] = (da * fg_ref[...].astype(f32)).astype(bf16)
        du_ref[...] = (da * fu_ref[...].astype(f32)).astype(bf16)

    t_spec = pl.BlockSpec((tm, ffb), lambda j, i: (i, j))
    return pl.pallas_call(
        body, name="d_ffn_down_in", grid=(dff // ffb, s // tm),
        in_specs=[pl.BlockSpec((tm, d), lambda j, i: (i, 0)), pl.BlockSpec((ffb, d), lambda j, i: (j, 0)), t_spec, t_spec],
        out_specs=[t_spec] * 2, out_shape=[SDS((s, dff), bf16)] * 2, compiler_params=_params(("parallel", "parallel")),
    )(dff_out, w_d, act_dg, act_du)


def _branch_merge(z_a, og, w_co, w_ho, proj, tm, gate_a0, gate_b0):
    s, dc = z_a.shape
    nsh, _, n = w_co.shape

    def body(za_ref, og_ref, wa_ref, wb_ref, ga_ref, gb_ref, m_ref, sa_ref, sb_ref, fa_ref, fb_ref):
        ya = jnp.dot(za_ref[...], wa_ref[...], preferred_element_type=f32)
        yb = jnp.dot(og_ref[...], wb_ref[...], preferred_element_type=f32)
        sa, sb_ = _sigmoid(ga_ref[...]), _sigmoid(gb_ref[...])
        m_ref[...] = (sa * ya + sb_ * yb).astype(bf16)
        sa_ref[...] = sa.astype(bf16)
        sb_ref[...] = sb_.astype(bf16)
        fa_ref[...] = (ya * sa * (1.0 - sa)).astype(bf16)
        fb_ref[...] = (yb * sb_ * (1.0 - sb_)).astype(bf16)

    a_spec = pl.BlockSpec((tm, dc), lambda i, j: (i, 0))
    w_spec = pl.BlockSpec((None, dc, n), lambda i, j: (j, 0, 0))
    gate = lambda c0: pl.BlockSpec((tm, n), functools.partial(lambda i, j, cb: (i, cb + j), cb=c0 // n))
    o_spec = pl.BlockSpec((tm, n), lambda i, j: (i, j))
    return pl.pallas_call(
        body, name="branch_merge", grid=(s // tm, nsh), in_specs=[a_spec, a_spec, w_spec, w_spec, gate(gate_a0), gate(gate_b0)],
        out_specs=[o_spec] * 5, out_shape=[SDS((s, nsh * n), bf16)] * 5, compiler_params=_params(("parallel", "parallel")),
    )(z_a, og, w_co, w_ho, proj, proj)


def _d_branch_merge(dmo, w_o, factors, tm):
    s, d = dmo.shape
    n = d // 2

    def body(a_ref, w_ref, sa_ref, sb_ref, fa_ref, fb_ref, dya_ref, dyb_ref, dga_ref, dgb_ref):
        dm = lax.dot_general(a_ref[...], w_ref[...], (NT, ((), ())), preferred_element_type=f32)
        for f_ref, o_ref in ((sa_ref, dya_ref), (sb_ref, dyb_ref), (fa_ref, dga_ref), (fb_ref, dgb_ref)):
            o_ref[...] = (dm * f_ref[...].astype(f32)).astype(bf16)

    t_spec = pl.BlockSpec((tm, n), lambda j, i: (i, j))
    return pl.pallas_call(
        body, name="d_branch_merge", grid=(d // n, s // tm),
        in_specs=[pl.BlockSpec((tm, d), lambda j, i: (i, 0)), pl.BlockSpec((n, d), lambda j, i: (j, 0))] + [t_spec] * 4,
        out_specs=[t_spec] * 4, out_shape=[SDS((s, d), bf16)] * 4, compiler_params=_params(("parallel", "parallel")),
    )(dmo, w_o, *factors)


def _local_step(x, tgt, mod, g_mix, g_ffn, g_fin, lb_param, gnorm, conv_w, project, get_w, prefetch, put_g, sent):
    s, d = x.shape
    dc = d // 2
    tm = min(512, s)
    tm2 = min(1024, s)
    te = min(256, s)
    tb = min(128, s)
    mt = s // tm
    nb = 512
    sh_m, sc_m, gt_m, sh_f, sc_f, gt_f = [mod[i] for i in range(N_MOD)]
    dh2 = d // 2

    def e1(rows, fulls, outs, accs):
        xv = rows[0][...]
        g, sc, sh = [r[...] for r in fulls]
        _, xh = _rms(xv)
        outs[0][...] = (xh * g * (1.0 + sc) + sh).astype(bf16)

    h, = _rowwise("prenorm_mix", e1, s, te, [(x, d, 0)], [g_mix, sc_m, sh_m], [(d, bf16)])
    proj, w_in = project(h, tm2)
    nsh, _, win_sh = w_in.shape
    z_a = _conv_fwd(proj, _after_tokens(conv_w, [prefetch("mix", proj)]), s, dc)
    og, o_saved, states = _hgrn_fwd(proj, lb_param, gnorm, s, dc, tb)
    w_co, w_ho, w_o = get_w("mix", og)

    gate_a0, gate_b0 = 7 * dh2, 9 * dh2
    merged, *merge_factors = _branch_merge(z_a, og, w_co, w_ho, proj, tm2, gate_a0, gate_b0)
    mo = _matmul("mix_out", merged, w_o, (s, d), bf16, (2, s // tm2, 1), pl.BlockSpec((tm2, d), lambda j, i, k: (i, 0)),
                 pl.BlockSpec((d, dh2), lambda j, i, k: (0, j)), pl.BlockSpec((tm2, dh2), lambda j, i, k: (i, j)), NN,
                 after=prefetch("ffn", merged))

    def e6(rows, fulls, outs, accs):
        xv, mov = rows[0][...], rows[1][...].astype(f32)
        gt, g, sc, sh = [r[...] for r in fulls]
        x1 = xv + gt * mov
        outs[0][...] = x1
        _, xh = _rms(x1)
        outs[1][...] = (xh * g * (1.0 + sc) + sh).astype(bf16)

    x1, h2 = _rowwise("prenorm_ffn", e6, s, tm, [(x, d, 0), (mo, d, 0)], [gt_m, g_ffn, sc_f, sh_f], [(d, f32), (d, bf16)])
    w_gt, w_ut = get_w("ffn", h2)
    dff_ = w_gt.shape[0]
    act_dg, act_du, act = _ffn_in(h2, w_gt, w_ut, tm2, nb)
    prefetch("ffn_down", act)
    w_d, = get_w("ffn_down", act)
    ff = _matmul("ffn_down", act, w_d, (s, d), bf16, (2, mt, 1), pl.BlockSpec((tm, dff_), lambda j, i, k: (i, 0)),
                 pl.BlockSpec((dff_, dh2), lambda j, i, k: (0, j)), pl.BlockSpec((tm, dh2), lambda j, i, k: (i, j)), NN)

    def e9(rows, fulls, outs, accs):
        x1v, ffv, tv = [r[...].astype(f32) for r in rows]
        gt, gf = fulls[0][...], fulls[1][...]
        x2 = x1v + gt * ffv
        r, xh = _rms(x2)
        err = xh * gf - tv
        accs[0][...] += 0.5 * jnp.sum(jnp.mean(err * err, axis=-1, keepdims=True), axis=0, keepdims=True)
        dy = err / d
        accs[1][...] += _colsum(dy * xh)
        dxh = dy * gf
        dx2 = r * (dxh - xh * jnp.mean(dxh * xh, axis=-1, keepdims=True))
        outs[0][...] = dx2
        outs[1][...] = (dx2 * gt).astype(bf16)
        accs[2][...] += _colsum(dx2 * ffv)

    dx2, dff, loss_acc, dg_fin, dgt_f = _rowwise("loss_head", e9, s, tm, [(x1, d, 0), (ff, d, 0), (tgt, d, 0)], [gt_f, g_fin],
                                                 [(d, f32), (d, bf16)], [(1, 128), (1, d), (1, d)])
    dgg, duu = _ffn_down_bwd(dff, w_d, act_dg, act_du, tm, dff_ // 4)

    def wgrad_rows(name, a, b, n_out):
        kb = 512
        return _matmul(name, a, b, (n_out, d), bf16, (n_out // kb, 2, 1), pl.BlockSpec((s, kb), lambda i, j, k: (0, i)),
                       pl.BlockSpec((s, dh2), lambda i, j, k: (0, j)), pl.BlockSpec((kb, dh2), lambda i, j, k: (i, j)), TN)

    gw_d = wgrad_rows("gw_ffn_down", act, dff, dff_)

    def ffn_in_bwd(name, a, w, after=None):
        return _matmul(name, a, w, (s, d), bf16, (2, mt, 1), pl.BlockSpec((tm, dff_), lambda j, i, k: (i, 0)),
                       pl.BlockSpec((dff_, dh2), lambda j, i, k: (0, j)), pl.BlockSpec((tm, dh2), lambda j, i, k: (i, j)), NN,
                       after=after)

    dh2b = ffn_in_bwd("d_ffn_up_in", duu, w_ut)
    gw_ut = wgrad_rows("gw_ffn_up", duu, h2, dff_)
    gw_gt = wgrad_rows("gw_ffn_gate", dgg, h2, dff_)
    dh2a = ffn_in_bwd("d_ffn_gate_in", dgg, w_gt, after=put_g("ffn", [gw_gt, gw_ut, gw_d]))
    sc_f_late = _after_tokens(sc_f, [sent("ffn", dh2a)])

    def b5(rows, fulls, outs, accs):
        da, db, x1v, dx2v, mov = [r[...].astype(f32) for r in rows]
        sc, g, gt = [r[...] for r in fulls]
        dh = da + db
        r, xh = _rms(x1v)
        accs[0][...] += _colsum(dh)
        accs[1][...] += _colsum(dh * (xh * g))
        dn = dh * (1.0 + sc)
        accs[2][...] += _colsum(dn * xh)
        dxh = dn * g
        dx1 = dx2v + r * (dxh - xh * jnp.mean(dxh * xh, axis=-1, keepdims=True))
        outs[0][...] = dx1
        accs[3][...] += _colsum(dx1 * mov)
        outs[1][...] = (dx1 * gt).astype(bf16)

    dx1, dmo, dsh_f, dsc_f, dg_ffn, dgt_m = _rowwise(
        "d_prenorm_ffn", b5, s, te, [(dh2a, d, 0), (dh2b, d, 0), (x1, d, 0), (dx2, d, 0), (mo, d, 0)], [sc_f_late, g_ffn, gt_m],
        [(d, f32), (d, bf16)], [(1, d)] * 4)
    dya, dyb, dga, dgb = _d_branch_merge(dmo, w_o, merge_factors, tm)
    gw_o = wgrad_rows("gw_mix_out", merged, dmo, d)

    def out_proj_bwd(name, dy, w):
        return _matmul(name, dy, w, (s, dc), bf16, (1, s // tm2, nsh), pl.BlockSpec((tm2, d // nsh), lambda j, i, k: (i, k)),
                       pl.BlockSpec((None, dc, d // nsh), lambda j, i, k: (k, 0, 0)), pl.BlockSpec((tm2, dc), lambda j, i, k: (i, 0)), NT)

    def out_proj_wgrad(name, a, dy):
        return _matmul(name, a, dy, (nsh, dc, d // nsh), bf16, (1, nsh, 1), pl.BlockSpec((s, dc), lambda i, j, k: (0, 0)),
                       pl.BlockSpec((s, d // nsh), lambda i, j, k: (0, j)), pl.BlockSpec((None, dc, d // nsh), lambda i, j, k: (j, 0, 0)), TN)

    dz_a = out_proj_bwd("d_conv_out_in", dya, w_co)
    dog = out_proj_bwd("d_hgrn_out_in", dyb, w_ho)
    gw_co = out_proj_wgrad("gw_conv_out", z_a, dya)
    gw_ho = out_proj_wgrad("gw_hgrn_out", og, dyb)
    conv_w_late = _after_tokens(conv_w, [put_g("mix", [gw_co, gw_ho, gw_o])])
    dab, dac, dax, dconv_w = _conv_bwd(proj, conv_w_late, dz_a, s, dc)
    lb_param_late = _after_tokens(lb_param, [sent("mix", dab)])
    dq, dfl, dvi, dgo, dlb, dgn = _hgrn_bwd(proj, lb_param_late, gnorm, o_saved, states, dog, s, dc, tb)
    dproj = _concat_columns("d_proj", [dab, dac, dax, dq, dfl, dvi, dgo, dga, dgb], te)
    gw_in = _matmul("gw_proj", h, dproj, (nsh, d, win_sh), bf16, (nsh, d // 512, 1), pl.BlockSpec((s, 512), lambda j, i, k: (0, i)),
                    pl.BlockSpec((s, win_sh), lambda j, i, k: (0, j)), pl.BlockSpec((None, 512, win_sh), lambda j, i, k: (j, i, 0)), TN)
    dh = _d_proj_in(dproj, w_in, tm, put_g("in", [gw_in]))

    def b12(rows, fulls, outs, accs):
        dhv, xv, dx1v = [r[...].astype(f32) for r in rows]
        sc, g = fulls[0][...], fulls[1][...]
        r, xh = _rms(xv)
        accs[0][...] += _colsum(dhv)
        accs[1][...] += _colsum(dhv * (xh * g))
        dn = dhv * (1.0 + sc)
        accs[2][...] += _colsum(dn * xh)
        dxh = dn * g
        outs[0][...] = dx1v + r * (dxh - xh * jnp.mean(dxh * xh, axis=-1, keepdims=True))

    dx, dsh_m, dsc_m, dg_mix = _rowwise("d_prenorm_mix", b12, s, tm, [(dh, d, 0), (x, d, 0), (dx1, d, 0)], [sc_m, g_mix],
                                        [(d, f32)], [(1, d)] * 3)
    dmod = [dsh_m, dsc_m, dgt_m, dsh_f, dsc_f, dgt_f]
    small = dict(loss=loss_acc, g_mix=dg_mix, g_ffn=dg_ffn, g_fin=dg_fin, lb=dlb, gnorm=dgn, conv_w=dconv_w)
    return dx, dmod, small


def _ada_fwd(c_all, w_sh, b_sh):
    def body(c_ref, w_ref, b_ref, o_ref):
        cv = c_ref[...]
        ca = (cv * _sigmoid(cv)).astype(bf16)
        o_ref[...] = jnp.dot(ca, w_ref[...].astype(bf16), preferred_element_type=f32) + b_ref[...]

    return pl.pallas_call(body, name="ada_fwd", out_shape=SDS((c_all.shape[0], w_sh.shape[1]), f32),
                          compiler_params=pltpu.CompilerParams(vmem_limit_bytes=V7X_VMEM_LIMIT))(c_all, w_sh, b_sh)


def _ada_wgrad(c_all, dmod_sh):
    def body(c_ref, d_ref, o_ref):
        cv = c_ref[...]
        ca = (cv * _sigmoid(cv)).astype(bf16)
        o_ref[...] = lax.dot_general(ca, d_ref[...].astype(bf16), (TN, ((), ())), preferred_element_type=f32)

    return pl.pallas_call(body, name="ada_wgrad", out_shape=SDS((c_all.shape[1], dmod_sh.shape[1]), f32),
                          compiler_params=pltpu.CompilerParams(vmem_limit_bytes=V7X_VMEM_LIMIT))(c_all, dmod_sh)


def _lb_grad(lb_param, dlb):
    def body(p_ref, d_ref, o_ref):
        p = p_ref[...]
        lb = _sigmoid(p[0:1, :] - p[1:2, :])
        gl = d_ref[...] * lb * (1.0 - lb)
        o_ref[0:1, :] = gl
        o_ref[1:2, :] = -gl

    return pl.pallas_call(body, name="lb_grad", out_shape=SDS(lb_param.shape, f32))(lb_param, dlb)


def _sum_small(gathered):
    def body(g_ref, o_ref):
        acc = g_ref[0]
        for dd in range(1, NDEV):
            acc = acc + g_ref[dd]
        o_ref[...] = acc

    return pl.pallas_call(body, name="sum_small", out_shape=SDS(gathered.shape[1:], f32))(gathered)


def kernel(x, c, w_ada, b_ada, norm_mix_g, w_in, conv_w, lb_param, gnorm_g, w_conv_out, w_hgrn_out, w_o, norm_ffn_g, w_ffn_gate, w_ffn_up, w_ffn_down, norm_final_g, loss_target, m_w_ada, m_b_ada, m_norm_mix_g, m_w_in, m_conv_w, m_lb_param, m_gnorm_g, m_w_conv_out, m_w_hgrn_out, m_w_o, m_norm_ffn_g, m_w_ffn_gate, m_w_ffn_up, m_w_ffn_down, m_norm_final_g, v_w_ada, v_b_ada, v_norm_mix_g, v_w_in, v_conv_w, v_lb_param, v_gnorm_g, v_w_conv_out, v_w_hgrn_out, v_w_o, v_norm_ffn_g, v_w_ffn_gate, v_w_ffn_up, v_w_ffn_down, v_norm_final_g):
    assert lb_param.shape[0] == 2 and w_ada.shape[0] == 1
    s, d = x.shape[1], x.shape[2]
    me = 4 * lax.axis_index("x") + 2 * lax.axis_index("y") + lax.axis_index("c")
    ada_cols = w_ada.shape[2]

    me1 = me.astype(jnp.int32).reshape(1)
    placed_in = [_place_shard("place_in0", me1, w_in[0])[0]]
    c_all, cw_all = _all_gather("gather_cond", [c, conv_w[0]], after=placed_in)
    c_all = c_all.reshape(NDEV, d)
    conv_w_full = jnp.transpose(cw_all, (1, 0, 2)).reshape(conv_w.shape[1], -1)
    b_sh = lax.dynamic_slice_in_dim(b_ada, me * ada_cols, ada_cols, axis=1)
    mod_cols = _ada_fwd(c_all, w_ada[0], b_sh)
    mod_all, = _all_gather("gather_mod", [mod_cols])
    mod = lax.dynamic_index_in_dim(mod_all, me, axis=1, keepdims=False).reshape(N_MOD, 1, d)

    shard_groups = {"mix": [w_conv_out[0], w_hgrn_out[0], w_o[0]], "ffn": [w_ffn_gate[0].T, w_ffn_up[0].T],
                    "ffn_down": [w_ffn_down[0]]}
    own_slot = lambda frm, to: _flat(frm)
    gather_plan = lambda n: [(a, j, own_slot, own_slot) for a in range(n) for j in (1,) + ICI_RELATIONS]
    flat = lambda a: a.reshape(a.shape[0] * a.shape[1], a.shape[2])
    to8 = lambda a: a.reshape(NDEV, a.shape[0] // NDEV, a.shape[1])
    near, far = ICI_RELATIONS[:2], ICI_RELATIONS[2:]
    in_plan = lambda rels: [(0, j, own_slot, own_slot) for j in rels]
    ss_a, rs_a, _, lands, tok_a = _push_start("gather_start_in_sib", [], placed_in, in_plan((1,)), after=mod_all)
    ss_b, rs_b, _, lands_in, tok_b = _push_start("gather_start_in_near", [], lands, in_plan(near), after=tok_a)
    gathering, tokens, placed = {}, [tok_a, tok_b], {}
    for grp, sh in shard_groups.items():
        placed[grp] = []
        for i, a in enumerate(sh):
            buf, tok = _place_shard(f"place_{grp}{i}", me1, a, after=tokens[-1])
            placed[grp].append(buf)
            tokens.append(tok)

    pos = (lax.axis_index("x"), lax.axis_index("y"), lax.axis_index("c"))
    ids = lambda frm, rels: jnp.stack([_flat(_peer(frm, j)) for j in rels]).astype(jnp.int32)

    def project(h, tm):
        sib = _peer(pos, 1)
        fwd_of = lambda idxs: [(0, 1, fwd_slot(i), fwd_slot(i)) for i in idxs]
        _, lands = _push_wait("gather_wait_in_sib", ss_a, rs_a, [], lands_in, in_plan((1,)), h)
        proj = _proj_part("proj_local", h, lands[0], ids(pos, (0, 1)), None, tm)
        _, lands = _push_wait("gather_wait_in_near", ss_b, rs_b, [], lands, in_plan(near), proj)
        ss_c, rs_c, _, lands, tok = _push_start("gather_start_in_far", [], lands, in_plan(far))
        for grp in shard_groups:
            ss, rs, _, bufs, tok = _push_start("gather_start_" + grp, [], placed[grp], gather_plan(len(placed[grp])), after=tok)
            gathering[grp] = (ss, rs, bufs)
        ss1, rs1, _, lands, tok = _push_start("gather_fwd_in_near", [], lands, fwd_of((0, 1)), after=tok)
        proj = _proj_part("proj_near", h, lands[0], ids(pos, near), proj, tm, after=tok)
        _, lands = _push_wait("gather_fwd_wait_in_near", ss1, rs1, [], lands, fwd_of((0, 1)), proj)
        proj = _proj_part("proj_fwd_near", h, lands[0], ids(sib, near), proj, tm)
        _, lands = _push_wait("gather_wait_in_far", ss_c, rs_c, [], lands, in_plan(far), proj)
        ss2, rs2, _, lands, tok = _push_start("gather_fwd_in_far", [], lands, fwd_of((2,)))
        proj = _proj_part("proj_far", h, lands[0], ids(pos, far), proj, tm, after=tok)
        _, lands = _push_wait("gather_fwd_wait_in_far", ss2, rs2, [], lands, fwd_of((2,)), proj)
        proj = _proj_part("proj_fwd_far", h, lands[0], ids(sib, far), proj, tm)
        return proj, lands[0]

    fwd_slot = lambda i: (lambda frm, to: _flat(_peer(frm, ICI_RELATIONS[i])))
    fwd_plan = lambda n: [(a, 1, fwd_slot(i), fwd_slot(i)) for a in range(n) for i in range(len(ICI_RELATIONS))]
    forwarding = {}

    def prefetch(grp, after):
        ss, rs, lands = gathering[grp]
        _, lands = _push_wait("gather_wait_" + grp, ss, rs, [], lands, gather_plan(len(lands)), after)
        ss, rs, _, lands, tok = _push_start("gather_fwd_" + grp, [], lands, fwd_plan(len(lands)))
        forwarding[grp] = (ss, rs, lands)
        return tok

    def get_w(grp, after):
        ss, rs, lands = forwarding[grp]
        _, full = _push_wait("gather_fwd_wait_" + grp, ss, rs, [], lands, fwd_plan(len(lands)), after)
        return [f if i < 2 and grp == "mix" else flat(f) for i, f in enumerate(full)]

    core = lax.axis_index("c").astype(jnp.int32).reshape(1)
    chip = (2 * lax.axis_index("x") + lax.axis_index("y")).astype(jnp.int32).reshape(1)
    scatter_plan = lambda n: [(a, j, lambda frm, to: _chip(to), lambda frm, to: _chip(frm)) for a in range(n) for j in ICI_RELATIONS]
    scattering = {}

    swap_plan = lambda n: [(a, 1, functools.partial(lambda frm, to, q: 2 * q + to[2], q=q), functools.partial(lambda frm, to, q: q, q=q))
                           for a in range(n) for q in range(NDEV // 2)]
    swapping = {}

    def start_ici(grp, g8, recv):
        pairs = [_pair_sum(f"pair_sum_{grp}{i}", core, g, r, _row_tile(g.shape[1], 1024)) for i, (g, r) in enumerate(zip(g8, recv))]
        lands = [lax.empty(p.shape, p.dtype) for p in pairs]
        ss, rs, srcs, lands, tok = _push_start("scatter_start_" + grp, pairs, lands, scatter_plan(len(pairs)))
        scattering[grp] = (ss, rs, srcs, lands)
        return tok

    def put_g(grp, grads):
        g8 = [g if g.ndim == 3 else to8(g) for g in grads]
        if grp == "in":
            return start_ici(grp, g8, _sibling_swap("scatter_pair_" + grp, g8))
        lands = [lax.empty((NDEV // 2,) + g.shape[1:], g.dtype) for g in g8]
        ss, rs, srcs, lands, tok = _push_start("scatter_swap_" + grp, g8, lands, swap_plan(len(g8)))
        swapping[grp] = (ss, rs, srcs, lands)
        return tok

    def sent(grp, after):
        ss, rs, srcs, lands = swapping[grp]
        g8, recv = _push_wait("scatter_swapped_" + grp, ss, rs, srcs, lands, swap_plan(len(srcs)), after)
        return start_ici(grp, g8, recv)

    def reduced(grp, after, names):
        ss, rs, srcs, lands = scattering[grp]
        srcs, lands = _push_wait("scatter_wait_" + grp, ss, rs, srcs, lands, scatter_plan(len(srcs)), after)
        for i, (p, r, nm) in enumerate(zip(srcs, lands, names)):
            tr = nm in ("w_ffn_gate", "w_ffn_up")
            wmv = tuple(a[0].T if tr else a[0] for a in weights[nm])
            out = _chip_sum_adamw(f"chip_sum_{grp}{i}", chip, p, r, _row_tile(p.shape[1]), wmv)
            res[nm] = [(a.T if tr else a).reshape(weights[nm][0].shape) for a in out]

    dx, dmod, small = _local_step(x[0], loss_target[0], mod, _after_tokens(norm_mix_g, tokens), norm_ffn_g,
                                  norm_final_g.reshape(1, d), lb_param, gnorm_g, conv_w_full, project, get_w, prefetch, put_g, sent)

    pieces = [*dmod, small["g_mix"], small["g_ffn"], small["g_fin"], small["lb"], small["gnorm"], small["loss"],
              small["conv_w"].reshape(1, -1)]
    widths = [p.shape[1] for p in pieces]
    offs = np.concatenate([[0], np.cumsum(widths)])
    packed = jnp.concatenate(pieces, axis=1)
    gathered, = _all_gather("gather_small", [packed])
    summed = _sum_small(gathered)
    part = lambda i: summed[:, offs[i]:offs[i + 1]]
    g_b_ada = summed[:, :N_MOD * d]
    g_norm_mix, g_norm_ffn, g_norm_fin, g_lb_row, g_gnorm, loss_vec, g_convw_flat = [part(i) for i in range(N_MOD, N_MOD + 7)]
    loss = loss_vec[0, 0]
    dmod_all = gathered[:, 0, :N_MOD * d]
    g_w_ada = _ada_wgrad(c_all, lax.dynamic_slice_in_dim(dmod_all, me * ada_cols, ada_cols, axis=1))
    g_lb = _lb_grad(lb_param, g_lb_row)
    cw_cols = conv_w.shape[2]
    g_conv_w = lax.dynamic_slice_in_dim(g_convw_flat.reshape(conv_w.shape[1], -1), me * cw_cols, cw_cols, axis=1)

    grads = dict(w_ada=g_w_ada, b_ada=g_b_ada, norm_mix_g=g_norm_mix, conv_w=g_conv_w, lb_param=g_lb, gnorm_g=g_gnorm,
                 norm_ffn_g=g_norm_ffn, norm_final_g=g_norm_fin)
    weights = dict(w_ada=(w_ada, m_w_ada, v_w_ada), b_ada=(b_ada, m_b_ada, v_b_ada), norm_mix_g=(norm_mix_g, m_norm_mix_g, v_norm_mix_g),
                   w_in=(w_in, m_w_in, v_w_in), conv_w=(conv_w, m_conv_w, v_conv_w), lb_param=(lb_param, m_lb_param, v_lb_param),
                   gnorm_g=(gnorm_g, m_gnorm_g, v_gnorm_g), w_conv_out=(w_conv_out, m_w_conv_out, v_w_conv_out),
                   w_hgrn_out=(w_hgrn_out, m_w_hgrn_out, v_w_hgrn_out), w_o=(w_o, m_w_o, v_w_o),
                   norm_ffn_g=(norm_ffn_g, m_norm_ffn_g, v_norm_ffn_g), w_ffn_gate=(w_ffn_gate, m_w_ffn_gate, v_w_ffn_gate),
                   w_ffn_up=(w_ffn_up, m_w_ffn_up, v_w_ffn_up), w_ffn_down=(w_ffn_down, m_w_ffn_down, v_w_ffn_down),
                   norm_final_g=(norm_final_g, m_norm_final_g, v_norm_final_g))
    res = {}

    def update(nm):
        w, m, v = weights[nm]
        shape2 = (w.shape[-2], w.shape[-1]) if w.ndim >= 2 else (1, w.shape[0])
        g2 = grads[nm].reshape(shape2)
        dl, mn, vn = _adamw("adamw_" + nm, g2, w.reshape(shape2), m.reshape(shape2), v.reshape(shape2))
        res[nm] = [a.reshape(w.shape) for a in (g2, dl, mn, vn)]

    for nm in list(grads):
        update(nm)
    reduced("ffn", res["w_ada"][1], ("w_ffn_gate", "w_ffn_up", "w_ffn_down"))
    reduced("mix", res["w_ffn_down"][1], ("w_conv_out", "w_hgrn_out", "w_o"))
    reduced("in", res["w_o"][1], ("w_in",))
    outs = [[res[nm][i] for nm in weights] for i in range(4)]
    return (loss, dx.reshape(x.shape), *outs[0], *outs[1], *outs[2], *outs[3])
```

```python
import functools

import numpy as np
import jax
import jax.numpy as jnp
from jax import lax
from jax.experimental import pallas as pl
from jax.experimental.pallas import tpu as pltpu

f32, bf16 = jnp.float32, jnp.bfloat16
SDS = jax.ShapeDtypeStruct

EPS = 1e-6
HEADS, DK, CHUNK = 8, 128, 64
HEADS_PER_STEP = 4
N_MOD = 6
NDEV = 8
ADAM_LR, ADAM_B1, ADAM_B2, ADAM_EPS, ADAM_WD, ADAM_STEP = 0.001, 0.9, 0.999, 1e-08, 0.01, 10
LEVELS = (32, 16, 8, 4, 2, 1)
V7X_VMEM_LIMIT = 56 * 1024 * 1024
HBM = pl.BlockSpec(memory_space=pltpu.HBM)
MESH = pl.DeviceIdType.MESH


def _params(sem):
    return pltpu.CompilerParams(dimension_semantics=sem, vmem_limit_bytes=V7X_VMEM_LIMIT)


def _sigmoid(x):
    return jax.nn.sigmoid(x)


def _dsilu(x, s):
    return s * (1.0 + x * (1.0 - s))


def _mesh_pos():
    x, y, c = lax.axis_index("x"), lax.axis_index("y"), lax.axis_index("c")
    return x, y, c


def _peer(pos, j):
    x, y, c = pos
    return (1 - x if j & 4 else x, 1 - y if j & 2 else y, 1 - c if j & 1 else c)


def _flat(pos):
    return 4 * pos[0] + 2 * pos[1] + pos[2]


def _all_gather(name, arrs, after=()):
    n, ne = len(arrs), len(after)
    out_shapes = [SDS((NDEV,) + a.shape, a.dtype) for a in arrs]

    def body(*refs):
        ins, outs = refs[:n], refs[n + ne:2 * n + ne]
        send_sems, recv_sems, local_sems = refs[2 * n + ne:]
        pos = _mesh_pos()
        me = _flat(pos)

        def copy(a, j, frm, to_pos):
            k = a * (NDEV - 1) + j - 1
            return pltpu.make_async_remote_copy(src_ref=ins[a], dst_ref=outs[a].at[frm], send_sem=send_sems.at[k],
                                                recv_sem=recv_sems.at[k], device_id=to_pos, device_id_type=MESH)

        local = [pltpu.make_async_copy(ins[a], outs[a].at[me], local_sems.at[a]) for a in range(n)]
        for cp in local:
            cp.start()
        sends = [copy(a, j, me, _peer(pos, j)) for j in range(1, NDEV) for a in range(n)]
        for cp in sends:
            cp.start()
        for j in range(1, NDEV):
            for a in range(n):
                copy(a, j, _flat(_peer(pos, j)), pos).wait_recv()
        for cp in sends:
            cp.wait_send()
        for cp in local:
            cp.wait()

    return pl.pallas_call(
        body, name=name, out_shape=out_shapes, in_specs=[HBM] * n + [ANY] * ne, out_specs=[HBM] * n,
        scratch_shapes=[pltpu.SemaphoreType.DMA((n * (NDEV - 1),)), pltpu.SemaphoreType.DMA((n * (NDEV - 1),)),
                        pltpu.SemaphoreType.DMA((n,))],
    )(*arrs, *after)


SEM = pl.BlockSpec(memory_space=pltpu.SEMAPHORE)
ANY = pl.BlockSpec(memory_space=pl.ANY)
EFFECT = pltpu.SideEffectType.DATAFLOW_SIDE_EFFECTING
ICI_RELATIONS = (2, 4, 6)


def _chip(pos):
    return 2 * pos[0] + pos[1]


def _hbm(a):
    return pltpu.with_memory_space_constraint(a, pltpu.HBM)


def _plan_copy(plan_entry, k, pos, frm, to, src_refs, land_refs, send_sems, recv_sems):
    a, _, src_slot, dst_slot = plan_entry
    s = src_refs[a] if src_slot is None else src_refs[a].at[src_slot(frm, to)]
    return pltpu.make_async_remote_copy(src_ref=s, dst_ref=land_refs[a].at[dst_slot(frm, to)], send_sem=send_sems.at[k],
                                        recv_sem=recv_sems.at[k], device_id=to, device_id_type=MESH)


def _push_start(name, srcs, lands, plan, after=None):
    ns, nb, nk = len(srcs), len(srcs) + len(lands), len(plan)
    extra = [] if after is None else [after]

    def body(*refs):
        land_refs = refs[ns:nb]
        src_refs = refs[:ns] if ns else land_refs
        send_sems, recv_sems = refs[nb + len(extra)], refs[nb + len(extra) + 1]
        pos = _mesh_pos()
        for k, e in enumerate(plan):
            _plan_copy(e, k, pos, pos, _peer(pos, e[1]), src_refs, land_refs, send_sems, recv_sems).start()
        refs[-1][...] = jnp.zeros_like(refs[-1])

    outs = pl.pallas_call(
        body, name=name,
        out_shape=[pltpu.SemaphoreType.DMA((nk,)), pltpu.SemaphoreType.DMA((nk,))] + [pltpu.HBM(a.shape, a.dtype) for a in srcs + lands]
        + [SDS((8, 128), f32)],
        in_specs=[HBM] * nb + [ANY] * len(extra), out_specs=[SEM, SEM] + [HBM] * nb + [pl.BlockSpec(memory_space=pltpu.VMEM)],
        input_output_aliases={i: 2 + i for i in range(nb)},
        compiler_params=pltpu.CompilerParams(has_side_effects=EFFECT),
    )(*[_hbm(a) for a in srcs + lands], *extra)
    return outs[0], outs[1], list(outs[2:2 + ns]), list(outs[2 + ns:2 + nb]), outs[-1]


def _push_wait(name, send_sems, recv_sems, srcs, lands, plan, after):
    ns, nb = len(srcs), len(srcs) + len(lands)

    def body(*refs):
        land_refs = refs[ns:nb]
        src_refs = refs[:ns] if ns else land_refs
        ssem, rsem = refs[nb], refs[nb + 1]
        pos = _mesh_pos()
        for k, e in enumerate(plan):
            peer = _peer(pos, e[1])
            _plan_copy(e, k, pos, pos, peer, src_refs, land_refs, ssem, rsem).wait_send()
            _plan_copy(e, k, pos, peer, pos, src_refs, land_refs, ssem, rsem).wait_recv()

    outs = pl.pallas_call(
        body, name=name, out_shape=[pltpu.HBM(a.shape, a.dtype) for a in srcs + lands],
        in_specs=[HBM] * nb + [SEM, SEM, ANY], out_specs=[HBM] * nb,
        input_output_aliases={i: i for i in range(nb)},
        compiler_params=pltpu.CompilerParams(has_side_effects=EFFECT),
    )(*srcs, *lands, send_sems, recv_sems, after)
    return list(outs[:ns]), list(outs[ns:])


def _after_tokens(small, tokens):
    for t in tokens:
        if t is not None:
            small = small + t[0:1, 0:1].reshape((1,) * small.ndim)
    return small


def _place_shard(name, me, shard, after=None):
    r, c = shard.shape
    tr = _row_tile(r)
    extra = [] if after is None else [after]

    def body(me_ref, s_ref, *rest):
        rest[-2][...] = s_ref[...].astype(bf16)
        rest[-1][...] = jnp.zeros_like(rest[-1])

    return pl.pallas_call(
        body, name=name, out_shape=[SDS((NDEV, r, c), bf16), SDS((8, 128), f32)],
        grid_spec=pltpu.PrefetchScalarGridSpec(
            num_scalar_prefetch=1, grid=(r // tr,), in_specs=[pl.BlockSpec((tr, c), lambda i, me_ref: (i, 0))] + [ANY] * len(extra),
            out_specs=[pl.BlockSpec((None, tr, c), lambda i, me_ref: (me_ref[0], i, 0)),
                       pl.BlockSpec((8, 128), lambda i, me_ref: (0, 0))]),
        compiler_params=_params(("arbitrary",)),
    )(me, shard, *extra)


def _row_tile(r, most=256):
    return max(t for t in range(16, most + 1, 16) if r % t == 0)


def _pair_sum(name, core, grad, recv, tr):
    _, r, c = grad.shape
    nchip = NDEV // 2

    def body(core_ref, g_ref, r_ref, o_ref):
        o_ref[...] = (g_ref[...].astype(f32) + r_ref[...].astype(f32)).astype(o_ref.dtype)

    return pl.pallas_call(
        body, name=name, out_shape=SDS((nchip, r, c), grad.dtype),
        grid_spec=pltpu.PrefetchScalarGridSpec(
            num_scalar_prefetch=1, grid=(nchip, r // tr),
            in_specs=[pl.BlockSpec((None, tr, c), lambda q, i, core_ref: (2 * q + core_ref[0], i, 0)),
                      pl.BlockSpec((None, tr, c), lambda q, i, core_ref: (q, i, 0))],
            out_specs=pl.BlockSpec((None, tr, c), lambda q, i, core_ref: (q, i, 0))),
        compiler_params=_params(("parallel", "parallel")),
    )(core, grad, recv)


def _adamw_math(g, w, m, v):
    mn = ADAM_B1 * m + (1.0 - ADAM_B1) * g
    vn = ADAM_B2 * v + (1.0 - ADAM_B2) * jnp.square(g)
    m_hat = mn / (1.0 - ADAM_B1 ** ADAM_STEP)
    v_hat = vn / (1.0 - ADAM_B2 ** ADAM_STEP)
    return -ADAM_LR * (m_hat / (jnp.sqrt(v_hat) + ADAM_EPS) + ADAM_WD * w), mn, vn


def _chip_sum_adamw(name, chip, pairs, recv, tr, wmv):
    nchip, r, c = pairs.shape

    def body(chip_ref, p_ref, r_ref, w_ref, m_ref, v_ref, g_ref, d_ref, mo_ref, vo_ref):
        mine = chip_ref[0]
        own = p_ref[...].astype(f32)
        acc = jnp.zeros((tr, c), f32)
        for q in range(nchip):
            acc = acc + jnp.where(mine == q, own, r_ref[q].astype(f32))
        g_ref[...] = acc
        d_ref[...], mo_ref[...], vo_ref[...] = _adamw_math(acc, w_ref[...], m_ref[...], v_ref[...])

    blk = pl.BlockSpec((tr, c), lambda i, chip_ref: (i, 0))
    return pl.pallas_call(
        body, name=name, out_shape=[SDS((r, c), f32)] * 4,
        grid_spec=pltpu.PrefetchScalarGridSpec(
            num_scalar_prefetch=1, grid=(r // tr,),
            in_specs=[pl.BlockSpec((None, tr, c), lambda i, chip_ref: (chip_ref[0], i, 0)),
                      pl.BlockSpec((nchip, tr, c), lambda i, chip_ref: (0, i, 0))] + [blk] * 3,
            out_specs=[blk] * 4),
        compiler_params=_params(("parallel",)),
    )(chip, pairs, recv, *wmv)


def _matmul(name, a, b, out_shape, out_dtype, grid, a_spec, b_spec, o_spec, dims, after=None):
    ksteps = grid[2]
    acc_shape = tuple(d for d in o_spec.block_shape if d is not None)
    extra = [] if after is None else [after]

    def body(a_ref, b_ref, *rest):
        o_ref, acc = rest[len(extra)], rest[len(extra) + 1:]
        prod = lax.dot_general(a_ref[...], b_ref[...], (dims, ((), ())), preferred_element_type=f32)
        if ksteps == 1:
            o_ref[...] = prod.astype(o_ref.dtype)
        else:
            k = pl.program_id(2)

            @pl.when(k == 0)
            def _():
                acc[0][...] = prod

            @pl.when(k > 0)
            def _():
                acc[0][...] += prod

            @pl.when(k == ksteps - 1)
            def _():
                o_ref[...] = acc[0][...].astype(o_ref.dtype)

    return pl.pallas_call(
        body, name=name, grid=grid, in_specs=[a_spec, b_spec] + [ANY] * len(extra), out_specs=o_spec,
        out_shape=SDS(out_shape, out_dtype), scratch_shapes=[] if ksteps == 1 else [pltpu.VMEM(acc_shape, f32)],
        compiler_params=_params(("parallel", "parallel", "arbitrary")),
    )(a, b, *extra)


NN, NT, TN = ((1,), (0,)), ((1,), (1,)), ((0,), (0,))


def _concat_columns(name, pieces, tm):
    s = pieces[0].shape[0]
    widths = [p.shape[1] for p in pieces]
    offs = [sum(widths[:i]) for i in range(len(widths))]

    def body(*refs):
        o_ref = refs[-1]
        for r, o, w in zip(refs[:-1], offs, widths):
            o_ref[:, o:o + w] = r[...]

    return pl.pallas_call(
        body, name=name, grid=(s // tm,), in_specs=[pl.BlockSpec((tm, w), lambda i: (i, 0)) for w in widths],
        out_specs=pl.BlockSpec((tm, sum(widths)), lambda i: (i, 0)), out_shape=SDS((s, sum(widths)), pieces[0].dtype),
        compiler_params=_params(("parallel",)),
    )(*pieces)


def _d_proj_in(dproj, w_sh, tm, after):
    s = dproj.shape[0]
    nsh, d, n = w_sh.shape
    ksteps = nsh // 2
    extra = [] if after is None else [after]

    def body(a_ref, b_ref, *rest):
        o_ref, acc = rest[len(extra)], rest[len(extra) + 1]
        k = pl.program_id(1)
        a = a_ref[...]
        prod = lax.dot_general(a[:, :n], b_ref[0], (NT, ((), ())), preferred_element_type=f32)
        prod += lax.dot_general(a[:, n:], b_ref[1], (NT, ((), ())), preferred_element_type=f32)

        @pl.when(k == 0)
        def _():
            acc[...] = prod

        @pl.when(k > 0)
        def _():
            acc[...] += prod

        @pl.when(k == ksteps - 1)
        def _():
            o_ref[...] = acc[...].astype(o_ref.dtype)

    return pl.pallas_call(
        body, name="d_proj_in", grid=(s // tm, ksteps),
        in_specs=[pl.BlockSpec((tm, 2 * n), lambda i, k: (i, k)), pl.BlockSpec((2, d, n), lambda i, k: (k, 0, 0))] + [ANY] * len(extra),
        out_specs=pl.BlockSpec((tm, d), lambda i, k: (i, 0)), out_shape=SDS((s, d), bf16),
        scratch_shapes=[pltpu.VMEM((tm, d), f32)], compiler_params=_params(("parallel", "arbitrary")),
    )(dproj, w_sh, *extra)


def _gw_part(name, h, dproj, n, ids, prev, after=None):
    s, d = h.shape
    kb = 512
    extra = ([] if prev is None else [prev]) + ([] if after is None else [after])

    def body(ids_ref, a_ref, b_ref, *rest):
        rest[-1][...] = lax.dot_general(a_ref[...], b_ref[...], (TN, ((), ())), preferred_element_type=f32).astype(bf16)

    return pl.pallas_call(
        body, name=name, out_shape=SDS((dproj.shape[1] // n, d, n), bf16),
        grid_spec=pltpu.PrefetchScalarGridSpec(
            num_scalar_prefetch=1, grid=(ids.shape[0], d // kb),
            in_specs=[pl.BlockSpec((s, kb), lambda j, i, ids_ref: (0, i)),
                      pl.BlockSpec((s, n), lambda j, i, ids_ref: (0, ids_ref[j]))] + [ANY] * len(extra),
            out_specs=pl.BlockSpec((None, kb, n), lambda j, i, ids_ref: (ids_ref[j], i, 0))),
        input_output_aliases={} if prev is None else {3: 0},
        compiler_params=_params(("parallel", "parallel")),
    )(ids, h, dproj, *extra)


def _proj_part(name, h, w_sh, ids, prev, tm, after=None):
    s, d = h.shape
    nsh, _, n = w_sh.shape
    extra = ([] if prev is None else [prev]) + ([] if after is None else [after])

    def body(ids_ref, a_ref, b_ref, *rest):
        rest[-1][...] = jnp.dot(a_ref[...], b_ref[...], preferred_element_type=f32)

    return pl.pallas_call(
        body, name=name, out_shape=SDS((s, nsh * n), f32),
        grid_spec=pltpu.PrefetchScalarGridSpec(
            num_scalar_prefetch=1, grid=(ids.shape[0], s // tm),
            in_specs=[pl.BlockSpec((tm, d), lambda j, i, ids_ref: (i, 0)),
                      pl.BlockSpec((None, d, n), lambda j, i, ids_ref: (ids_ref[j], 0, 0))] + [ANY] * len(extra),
            out_specs=pl.BlockSpec((tm, n), lambda j, i, ids_ref: (i, ids_ref[j]))),
        input_output_aliases={} if prev is None else {3: 0},
        compiler_params=_params(("parallel", "parallel")),
    )(ids, h, w_sh, *extra)


def _rowwise(name, fn, n_rows, tm, rows, fulls, row_outs, acc_outs=(), ncol=1):
    assert ncol == 1 or not acc_outs
    nr, nf, no, na = len(rows), len(fulls), len(row_outs), len(acc_outs)
    in_specs = [pl.BlockSpec((tm, w), functools.partial(lambda i, j, cb: (i, cb + j), cb=cb)) for (_, w, cb) in rows]
    in_specs += [pl.BlockSpec(a.shape, functools.partial(lambda i, j, nd: (0,) * nd, nd=a.ndim)) for a in fulls]
    out_shape = [SDS((n_rows, w), dt) for (w, dt) in row_outs] + [SDS(s, f32) for s in acc_outs]
    out_specs = [pl.BlockSpec((tm, w // ncol), lambda i, j: (i, j)) for (w, _) in row_outs]
    out_specs += [pl.BlockSpec(s, functools.partial(lambda i, j, nd: (0,) * nd, nd=len(s))) for s in acc_outs]

    def body(*refs):
        if na:
            @pl.when(pl.program_id(0) == 0)
            def _():
                for r in refs[nr + nf + no:]:
                    r[...] = jnp.zeros(r.shape, r.dtype)
        fn(refs[:nr], refs[nr:nr + nf], refs[nr + nf:nr + nf + no], refs[nr + nf + no:])

    return pl.pallas_call(
        body, name=name, grid=(n_rows // tm, ncol), in_specs=in_specs, out_specs=out_specs, out_shape=out_shape,
        compiler_params=_params(("arbitrary" if na else "parallel", "arbitrary" if na else "parallel")),
    )(*[r[0] for r in rows], *fulls)


def _rms(x):
    r = lax.rsqrt(jnp.mean(x * x, axis=-1, keepdims=True) + EPS)
    return r, x * r


def _colsum(v):
    return jnp.sum(v, axis=0, keepdims=True)


def _shift_down(u, row, k):
    return jnp.where(row >= k, pltpu.roll(u, k, 0), 0.0)


def _shift_up(u, row, k):
    n = u.shape[0]
    return jnp.where(row < n - k, pltpu.roll(u, n - k, 0), 0.0)


def _conv_fwd(proj, conv_w, s, dc):
    nb = dc // 128

    def body(ab_ref, ac_ref, ax_ref, w_ref, z_ref):
        u = ac_ref[...] * ax_ref[...]
        row = lax.broadcasted_iota(jnp.int32, u.shape, 0)
        w = w_ref[...]
        cv = w[0:1] * _shift_down(u, row, 2) + w[1:2] * _shift_down(u, row, 1) + w[2:3] * u
        z_ref[...] = (ab_ref[...] * cv).astype(z_ref.dtype)

    col = lambda off: pl.BlockSpec((s, 128), functools.partial(lambda j, off: (0, off + j), off=off))
    return pl.pallas_call(
        body, name="conv_fwd", grid=(nb,), in_specs=[col(0), col(nb), col(2 * nb), pl.BlockSpec((3, 128), lambda j: (0, j))],
        out_specs=pl.BlockSpec((s, 128), lambda j: (0, j)), out_shape=SDS((s, dc), bf16), compiler_params=_params(("parallel",)),
    )(proj, proj, proj, conv_w)


def _conv_bwd(proj, conv_w, dz, s, dc):
    nb = dc // 128

    def body(ab_ref, ac_ref, ax_ref, w_ref, dz_ref, dab_ref, dac_ref, dax_ref, dw_ref):
        ab, ac, ax, dzv = ab_ref[...], ac_ref[...], ax_ref[...], dz_ref[...].astype(f32)
        u = ac * ax
        row = lax.broadcasted_iota(jnp.int32, u.shape, 0)
        w = w_ref[...]
        u1, u2 = _shift_down(u, row, 1), _shift_down(u, row, 2)
        cv = w[0:1] * u2 + w[1:2] * u1 + w[2:3] * u
        dcv = dzv * ab
        dab_ref[...] = (dzv * cv).astype(dab_ref.dtype)
        du = w[2:3] * dcv + w[1:2] * _shift_up(dcv, row, 1) + w[0:1] * _shift_up(dcv, row, 2)
        dac_ref[...] = (du * ax).astype(dac_ref.dtype)
        dax_ref[...] = (du * ac).astype(dax_ref.dtype)
        dw_ref[0:1, :] = _colsum(dcv * u2)
        dw_ref[1:2, :] = _colsum(dcv * u1)
        dw_ref[2:3, :] = _colsum(dcv * u)

    col = lambda off: pl.BlockSpec((s, 128), functools.partial(lambda j, off: (0, off + j), off=off))
    blk = pl.BlockSpec((s, 128), lambda j: (0, j))
    return pl.pallas_call(
        body, name="conv_bwd", grid=(nb,),
        in_specs=[col(0), col(nb), col(2 * nb), pl.BlockSpec((3, 128), lambda j: (0, j)), blk],
        out_specs=[blk, blk, blk, pl.BlockSpec((3, 128), lambda j: (0, j))],
        out_shape=[SDS((s, dc), bf16)] * 3 + [SDS((3, dc), f32)], compiler_params=_params(("parallel",)),
    )(proj, proj, proj, conv_w, dz)


def _level_masks():
    t = np.arange(CHUNK)[:, None]
    s = np.arange(CHUNK)[None, :]
    m = np.stack([((t & h) != 0) & ((s & h) == 0) & (t // (2 * h) == s // (2 * h)) for h in LEVELS]).astype(np.float32)
    return jnp.asarray(m), jnp.asarray(m.transpose(0, 2, 1))


def _cumsum_rows(x, row):
    for sh in (1, 2, 4, 8, 16, 32):
        x = x + jnp.where(row >= sh, pltpu.roll(x, sh, 0), 0.0)
    return x


def _rev_cumsum_rows(x, row):
    n = x.shape[0]
    for sh in (1, 2, 4, 8, 16, 32):
        x = x + jnp.where(row < n - sh, pltpu.roll(x, n - sh, 0), 0.0)
    return x


def _chunk_terms(qp, fl, lb):
    row = lax.broadcasted_iota(jnp.int32, qp.shape, 0)
    sig = _sigmoid(fl)
    f = lb + (1.0 - lb) * sig
    k = 1.0 - f
    sq = _sigmoid(qp)
    qh = qp * sq
    b = _cumsum_rows(jnp.log(f), row)
    sub = lax.broadcasted_iota(jnp.int32, (CHUNK // 8, 8, DK), 1)
    b8 = b.reshape(CHUNK // 8, 8, DK)
    us, exs, ups = [], [], []
    for m in LEVELS:
        sb = 2 * m
        if sb >= 8:
            b3 = b.reshape(CHUNK // sb, sb, DK)
            bref = jnp.broadcast_to(b3[:, m - 1:m, :], b3.shape).reshape(CHUNK, DK)
        else:
            bref8 = None
            for j in range(8 // sb):
                cand = jnp.broadcast_to(b8[:, j * sb + m - 1:j * sb + m, :], b8.shape)
                bref8 = cand if bref8 is None else jnp.where(sub >= j * sb, cand, bref8)
            bref = bref8.reshape(CHUNK, DK)
        up = (row & m) != 0
        ex = jnp.exp(jnp.where(up, b - bref, bref - b))
        us.append((jnp.where(up, qh, k) * ex).astype(bf16))
        exs.append(ex)
        ups.append(up)
    blast = b[CHUNK - 1:CHUNK, :]
    eb, ebl = jnp.exp(b), jnp.exp(blast - b)
    return dict(sig=sig, f=f, k=k, sq=sq, qh=qh, u=jnp.stack(us), ex=exs, up=ups, eb=eb, ebl=ebl, qt=qh * eb, kt=k * ebl,
                el=jnp.exp(blast), row=row)


def _scores(t, mask):
    pl_ = jnp.einsum("ltk,lsk->lts", t["u"], t["u"], preferred_element_type=f32)
    p = jnp.sum(pl_ * mask, axis=0)
    r = lax.broadcasted_iota(jnp.int32, (CHUNK, CHUNK), 0)
    c = lax.broadcasted_iota(jnp.int32, (CHUNK, CHUNK), 1)
    diag = jnp.sum(t["qh"] * t["k"], axis=-1, keepdims=True)
    return p + jnp.where(r == c, diag, 0.0)


def _hgrn_fwd(proj, lb_param, gnorm, s, dv_total, tb):
    nchunk = tb // CHUNK
    masks, _ = _level_masks()
    q0, f0, v0, g0 = 3 * HEADS, 4 * HEADS, 5 * HEADS, 6 * HEADS

    def body(q_ref, f_ref, v_ref, g_ref, lb_ref, gn_ref, mask_ref, og_ref, o_ref, st_ref, state):
        @pl.when(pl.program_id(1) == 0)
        def _():
            state[...] = jnp.zeros_like(state)

        lbp = lb_ref[...]
        lb_all = _sigmoid(lbp[0:1, :] - lbp[1:2, :])
        mask = mask_ref[...]
        for i, hh in [(i, hh) for i in range(nchunk) for hh in range(HEADS_PER_STEP)]:
            rs, cs = pl.ds(i * CHUNK, CHUNK), pl.ds(hh * DK, DK)
            t = _chunk_terms(q_ref[rs, cs], f_ref[rs, cs], lb_all[:, hh * DK:(hh + 1) * DK])
            v = v_ref[rs, cs]
            vb = v.astype(bf16)
            st = state[hh]
            st_ref[i, hh] = st
            p = _scores(t, mask)
            o = jnp.dot(p.astype(bf16), vb, preferred_element_type=f32)
            o += lax.dot_general(t["qt"].astype(bf16), st.astype(bf16), (NT, ((), ())), preferred_element_type=f32)
            state[hh] = st * t["el"] + lax.dot_general(vb, t["kt"].astype(bf16), (TN, ((), ())), preferred_element_type=f32)
            o_ref[rs, cs] = o
            r, oh = _rms(o)
            g = g_ref[rs, cs]
            og_ref[rs, cs] = (oh * gn_ref[...] * (g * _sigmoid(g))).astype(og_ref.dtype)

    hp, wide = HEADS_PER_STEP, HEADS_PER_STEP * DK
    col = lambda off: pl.BlockSpec((tb, wide), functools.partial(lambda h, t, off: (t, off + h), off=off // hp))
    blk = pl.BlockSpec((tb, wide), lambda h, t: (t, h))
    return pl.pallas_call(
        body, name="hgrn_fwd", grid=(HEADS // hp, s // tb),
        in_specs=[col(q0), col(f0), col(v0), col(g0), pl.BlockSpec((2, wide), lambda h, t: (0, h)),
                  pl.BlockSpec((1, DK), lambda h, t: (0, 0)), pl.BlockSpec(masks.shape, lambda h, t: (0, 0, 0))],
        out_specs=[blk, blk, pl.BlockSpec((nchunk, hp, DK, DK), lambda h, t: (t, h, 0, 0))],
        out_shape=[SDS((s, dv_total), bf16), SDS((s, dv_total), f32), SDS((s // CHUNK, HEADS, DK, DK), f32)],
        scratch_shapes=[pltpu.VMEM((hp, DK, DK), f32)], compiler_params=_params(("parallel", "arbitrary")),
    )(proj, proj, proj, proj, lb_param, gnorm, masks)


def _hgrn_bwd(proj, lb_param, gnorm, o_saved, states, dog, s, dv_total, tb):
    nchunk = tb // CHUNK
    nt = s // tb
    nc_total = s // CHUNK
    masks, masks_t = _level_masks()
    q0, f0, v0, g0 = 3 * HEADS, 4 * HEADS, 5 * HEADS, 6 * HEADS

    def body(q_ref, f_ref, v_ref, g_ref, lb_ref, gn_ref, mask_ref, maskt_ref, o_ref, dog_ref, st_ref, stn_ref,
             dq_ref, df_ref, dv_ref, dg_ref, dlb_ref, dgn_ref, gstate):
        h_id, t_id = pl.program_id(0), pl.program_id(1)

        @pl.when(t_id == 0)
        def _():
            gstate[...] = jnp.zeros_like(gstate)
            dlb_ref[...] = jnp.zeros_like(dlb_ref)

        @pl.when((t_id == 0) & (h_id == 0))
        def _():
            dgn_ref[...] = jnp.zeros_like(dgn_ref)

        lbp = lb_ref[...]
        lb_all = _sigmoid(lbp[0:1, :] - lbp[1:2, :])
        mask, maskt = mask_ref[...], maskt_ref[...]
        gn = gn_ref[...]
        for i, hh in [(i, hh) for i in reversed(range(nchunk)) for hh in range(HEADS_PER_STEP)]:
            rs, cs = pl.ds(i * CHUNK, CHUNK), pl.ds(hh * DK, DK)
            lb = lb_all[:, hh * DK:(hh + 1) * DK]
            qp, fl, v, g = q_ref[rs, cs], f_ref[rs, cs], v_ref[rs, cs], g_ref[rs, cs]
            t = _chunk_terms(qp, fl, lb)
            vb = v.astype(bf16)
            st0 = st_ref[i, hh]
            st1 = st_ref[i + 1, hh] if i + 1 < nchunk else stn_ref[0, hh]
            gt = gstate[hh]
            o = o_ref[rs, cs]
            r, oh = _rms(o)
            sg = _sigmoid(g)
            dog_v = dog_ref[rs, cs].astype(f32)
            dg_ref[rs, cs] = (dog_v * (oh * gn) * _dsilu(g, sg)).astype(dg_ref.dtype)
            don = dog_v * (g * sg)
            dgn_ref[...] += _colsum(don * oh)
            doh = don * gn
            do = r * (doh - oh * jnp.mean(doh * oh, axis=-1, keepdims=True))
            dob = do.astype(bf16)
            d = lax.dot_general(dob, vb, (NT, ((), ())), preferred_element_type=f32)
            dt = lax.dot_general(vb, dob, (NT, ((), ())), preferred_element_type=f32)
            z = (mask * d[None] + maskt * dt[None]).astype(bf16)
            rr = jnp.einsum("lts,lsk->ltk", z, t["u"], preferred_element_type=f32)
            dq = jnp.zeros((CHUNK, DK), f32)
            dk = jnp.zeros((CHUNK, DK), f32)
            qdk = jnp.zeros((CHUNK, DK), f32)
            for li in range(len(LEVELS)):
                du = t["ex"][li] * rr[li]
                dq += jnp.where(t["up"][li], du, 0.0)
                dk += jnp.where(t["up"][li], 0.0, du)
                e = t["u"][li].astype(f32) * rr[li]
                qdk += jnp.where(t["up"][li], e, -e)
            dd = jnp.sum(do * v, axis=-1, keepdims=True)
            dq += dd * t["k"]
            dk += dd * t["qh"]
            gtb = gt.astype(bf16)
            ktb, qtb = t["kt"].astype(bf16), t["qt"].astype(bf16)
            dq_in = jnp.dot(dob, st0.astype(bf16), preferred_element_type=f32)
            dk_in = jnp.dot(vb, gtb, preferred_element_type=f32)
            dq += t["eb"] * dq_in
            dk += t["ebl"] * dk_in
            qdk += qtb.astype(f32) * dq_in - ktb.astype(f32) * dk_in
            p = _scores(t, mask)
            dvv = lax.dot_general(p.astype(bf16), dob, (TN, ((), ())), preferred_element_type=f32)
            dvv += lax.dot_general(ktb, gtb, (NT, ((), ())), preferred_element_type=f32)
            dv_ref[rs, cs] = dvv.astype(dv_ref.dtype)
            a_end = _colsum(gtb.astype(f32) * st1)
            dlf = _rev_cumsum_rows(qdk, t["row"]) + a_end
            dfv = dlf / t["f"] - dk
            df_ref[rs, cs] = (dfv * (1.0 - lb) * t["sig"] * (1.0 - t["sig"])).astype(df_ref.dtype)
            dlb_ref[:, cs] += _colsum(dfv * (1.0 - t["sig"]))
            dq_ref[rs, cs] = (dq * _dsilu(qp, t["sq"])).astype(dq_ref.dtype)
            gstate[hh] = gt * t["el"] + lax.dot_general(dob, qtb, (TN, ((), ())), preferred_element_type=f32)

    hp, wide = HEADS_PER_STEP, HEADS_PER_STEP * DK
    rev = lambda t: nt - 1 - t
    col = lambda off: pl.BlockSpec((tb, wide), functools.partial(lambda h, t, off: (rev(t), off + h), off=off // hp))
    blk = pl.BlockSpec((tb, wide), lambda h, t: (rev(t), h))
    nxt = lambda h, t: (jnp.minimum((rev(t) + 1) * nchunk, nc_total - 1), h, 0, 0)
    return pl.pallas_call(
        body, name="hgrn_bwd", grid=(HEADS // hp, nt),
        in_specs=[col(q0), col(f0), col(v0), col(g0), pl.BlockSpec((2, wide), lambda h, t: (0, h)),
                  pl.BlockSpec((1, DK), lambda h, t: (0, 0)), pl.BlockSpec(masks.shape, lambda h, t: (0, 0, 0)),
                  pl.BlockSpec(masks.shape, lambda h, t: (0, 0, 0)), blk, blk,
                  pl.BlockSpec((nchunk, hp, DK, DK), lambda h, t: (rev(t), h, 0, 0)),
                  pl.BlockSpec((1, hp, DK, DK), nxt)],
        out_specs=[blk, blk, blk, blk, pl.BlockSpec((1, wide), lambda h, t: (0, h)), pl.BlockSpec((1, DK), lambda h, t: (0, 0))],
        out_shape=[SDS((s, dv_total), bf16)] * 4 + [SDS((1, HEADS * DK), f32), SDS((1, DK), f32)],
        scratch_shapes=[pltpu.VMEM((hp, DK, DK), f32)], compiler_params=_params(("arbitrary", "arbitrary")),
    )(proj, proj, proj, proj, lb_param, gnorm, masks, masks_t, o_saved, dog, states, states)


def _adamw(name, g, w, m, v):
    r, c = w.shape
    tr = r
    for cand in (256, 128, 64, 32, 16, 8):
        if r % cand == 0 and r > cand:
            tr = cand
            break

    def body(g_ref, w_ref, m_ref, v_ref, d_ref, mo_ref, vo_ref):
        d_ref[...], mo_ref[...], vo_ref[...] = _adamw_math(g_ref[...], w_ref[...], m_ref[...], v_ref[...])

    blk = pl.BlockSpec((tr, c), lambda i: (i, 0))
    return pl.pallas_call(
        body, name=name, grid=(r // tr,), in_specs=[blk] * 4, out_specs=[blk] * 3, out_shape=[SDS((r, c), f32)] * 3,
        compiler_params=_params(("parallel",)),
    )(g, w, m, v)


def _ffn_in(h2, w_gt, w_ut, tm, ffb):
    s, d = h2.shape
    dff = w_gt.shape[0]

    def body(a_ref, wg_ref, wu_ref, dg_ref, du_ref, act_ref):
        a = a_ref[...]
        g = lax.dot_general(a, wg_ref[...], (NT, ((), ())), preferred_element_type=f32)
        u = lax.dot_general(a, wu_ref[...], (NT, ((), ())), preferred_element_type=f32)
        sg = _sigmoid(g)
        silu = g * sg
        dg_ref[...] = (u * _dsilu(g, sg)).astype(bf16)
        du_ref[...] = silu.astype(bf16)
        act_ref[...] = (silu * u).astype(bf16)

    w_spec = pl.BlockSpec((ffb, d), lambda j, i: (j, 0))
    o_spec = pl.BlockSpec((tm, ffb), lambda j, i: (i, j))
    return pl.pallas_call(
        body, name="ffn_in", grid=(dff // ffb, s // tm), in_specs=[pl.BlockSpec((tm, d), lambda j, i: (i, 0)), w_spec, w_spec],
        out_specs=[o_spec] * 3, out_shape=[SDS((s, dff), bf16)] * 3, compiler_params=_params(("parallel", "parallel")),
    )(h2, w_gt, w_ut)


def _ffn_down_bwd(dff_out, w_d, act_dg, act_du, tm, ffb):
    s, d = dff_out.shape
    dff = w_d.shape[0]

    def body(a_ref, w_ref, fg_ref, fu_ref, dg_ref, du_ref):
        da = lax.dot_general(a_ref[...], w_ref[...], (NT, ((), ())), preferred_element_type=f32)
        dg_ref[...] = (da * fg_ref[...].astype(f32)).astype(bf16)
        du_ref[...] = (da * fu_ref[...].astype(f32)).astype(bf16)

    t_spec = pl.BlockSpec((tm, ffb), lambda j, i: (i, j))
    return pl.pallas_call(
        body, name="d_ffn_down_in", grid=(dff // ffb, s // tm),
        in_specs=[pl.BlockSpec((tm, d), lambda j, i: (i, 0)), pl.BlockSpec((ffb, d), lambda j, i: (j, 0)), t_spec, t_spec],
        out_specs=[t_spec] * 2, out_shape=[SDS((s, dff), bf16)] * 2, compiler_params=_params(("parallel", "parallel")),
    )(dff_out, w_d, act_dg, act_du)


def _branch_merge(z_a, og, w_co, w_ho, proj, tm, gate_a0, gate_b0):
    s, dc = z_a.shape
    nsh, _, n = w_co.shape

    def body(za_ref, og_ref, wa_ref, wb_ref, ga_ref, gb_ref, m_ref, sa_ref, sb_ref, fa_ref, fb_ref):
        ya = jnp.dot(za_ref[...], wa_ref[...], preferred_element_type=f32)
        yb = jnp.dot(og_ref[...], wb_ref[...], preferred_element_type=f32)
        sa, sb_ = _sigmoid(ga_ref[...]), _sigmoid(gb_ref[...])
        m_ref[...] = (sa * ya + sb_ * yb).astype(bf16)
        sa_ref[...] = sa.astype(bf16)
        sb_ref[...] = sb_.astype(bf16)
        fa_ref[...] = (ya * sa * (1.0 - sa)).astype(bf16)
        fb_ref[...] = (yb * sb_ * (1.0 - sb_)).astype(bf16)

    a_spec = pl.BlockSpec((tm, dc), lambda i, j: (i, 0))
    w_spec = pl.BlockSpec((None, dc, n), lambda i, j: (j, 0, 0))
    gate = lambda c0: pl.BlockSpec((tm, n), functools.partial(lambda i, j, cb: (i, cb + j), cb=c0 // n))
    o_spec = pl.BlockSpec((tm, n), lambda i, j: (i, j))
    return pl.pallas_call(
        body, name="branch_merge", grid=(s // tm, nsh), in_specs=[a_spec, a_spec, w_spec, w_spec, gate(gate_a0), gate(gate_b0)],
        out_specs=[o_spec] * 5, out_shape=[SDS((s, nsh * n), bf16)] * 5, compiler_params=_params(("parallel", "parallel")),
    )(z_a, og, w_co, w_ho, proj, proj)


def _d_branch_merge(dmo, w_o, factors, tm):
    s, d = dmo.shape
    n = d // 2

    def body(a_ref, w_ref, sa_ref, sb_ref, fa_ref, fb_ref, dya_ref, dyb_ref, dga_ref, dgb_ref):
        dm = lax.dot_general(a_ref[...], w_ref[...], (NT, ((), ())), preferred_element_type=f32)
        for f_ref, o_ref in ((sa_ref, dya_ref), (sb_ref, dyb_ref), (fa_ref, dga_ref), (fb_ref, dgb_ref)):
            o_ref[...] = (dm * f_ref[...].astype(f32)).astype(bf16)

    t_spec = pl.BlockSpec((tm, n), lambda j, i: (i, j))
    return pl.pallas_call(
        body, name="d_branch_merge", grid=(d // n, s // tm),
        in_specs=[pl.BlockSpec((tm, d), lambda j, i: (i, 0)), pl.BlockSpec((n, d), lambda j, i: (j, 0))] + [t_spec] * 4,
        out_specs=[t_spec] * 4, out_shape=[SDS((s, d), bf16)] * 4, compiler_params=_params(("parallel", "parallel")),
    )(dmo, w_o, *factors)


def _local_step(x, tgt, mod, g_mix, g_ffn, g_fin, lb_param, gnorm, conv_w, project, grad_in, get_w, prefetch, put_g, sent):
    s, d = x.shape
    dc = d // 2
    tm = min(512, s)
    tm2 = min(1024, s)
    te = min(256, s)
    tb = min(128, s)
    mt = s // tm
    nb = 512
    sh_m, sc_m, gt_m, sh_f, sc_f, gt_f = [mod[i] for i in range(N_MOD)]
    dh2 = d // 2

    def e1(rows, fulls, outs, accs):
        xv = rows[0][...]
        g, sc, sh = [r[...] for r in fulls]
        _, xh = _rms(xv)
        outs[0][...] = (xh * g * (1.0 + sc) + sh).astype(bf16)

    h, = _rowwise("prenorm_mix", e1, s, te, [(x, d, 0)], [g_mix, sc_m, sh_m], [(d, bf16)])
    proj, w_in = project(h, tm2)
    nsh, _, win_sh = w_in.shape
    z_a = _conv_fwd(proj, _after_tokens(conv_w, [prefetch("mix", proj)]), s, dc)
    og, o_saved, states = _hgrn_fwd(proj, lb_param, gnorm, s, dc, tb)
    w_co, w_ho, w_o = get_w("mix", og)

    gate_a0, gate_b0 = 7 * dh2, 9 * dh2
    merged, *merge_factors = _branch_merge(z_a, og, w_co, w_ho, proj, tm2, gate_a0, gate_b0)
    mo = _matmul("mix_out", merged, w_o, (s, d), bf16, (2, s // tm2, 1), pl.BlockSpec((tm2, d), lambda j, i, k: (i, 0)),
                 pl.BlockSpec((d, dh2), lambda j, i, k: (0, j)), pl.BlockSpec((tm2, dh2), lambda j, i, k: (i, j)), NN,
                 after=prefetch("ffn", merged))

    def e6(rows, fulls, outs, accs):
        xv, mov = rows[0][...], rows[1][...].astype(f32)
        gt, g, sc, sh = [r[...] for r in fulls]
        x1 = xv + gt * mov
        outs[0][...] = x1
        _, xh = _rms(x1)
        outs[1][...] = (xh * g * (1.0 + sc) + sh).astype(bf16)

    x1, h2 = _rowwise("prenorm_ffn", e6, s, tm, [(x, d, 0), (mo, d, 0)], [gt_m, g_ffn, sc_f, sh_f], [(d, f32), (d, bf16)])
    w_gt, w_ut = get_w("ffn", h2)
    dff_ = w_gt.shape[0]
    act_dg, act_du, act = _ffn_in(h2, w_gt, w_ut, tm2, nb)
    prefetch("ffn_down", act)
    w_d, = get_w("ffn_down", act)
    ff = _matmul("ffn_down", act, w_d, (s, d), bf16, (2, mt, 1), pl.BlockSpec((tm, dff_), lambda j, i, k: (i, 0)),
                 pl.BlockSpec((dff_, dh2), lambda j, i, k: (0, j)), pl.BlockSpec((tm, dh2), lambda j, i, k: (i, j)), NN)

    def e9(rows, fulls, outs, accs):
        x1v, ffv, tv = [r[...].astype(f32) for r in rows]
        gt, gf = fulls[0][...], fulls[1][...]
        x2 = x1v + gt * ffv
        r, xh = _rms(x2)
        err = xh * gf - tv
        accs[0][...] += 0.5 * jnp.sum(jnp.mean(err * err, axis=-1, keepdims=True), axis=0, keepdims=True)
        dy = err / d
        accs[1][...] += _colsum(dy * xh)
        dxh = dy * gf
        dx2 = r * (dxh - xh * jnp.mean(dxh * xh, axis=-1, keepdims=True))
        outs[0][...] = dx2
        outs[1][...] = (dx2 * gt).astype(bf16)
        accs[2][...] += _colsum(dx2 * ffv)

    dx2, dff, loss_acc, dg_fin, dgt_f = _rowwise("loss_head", e9, s, tm, [(x1, d, 0), (ff, d, 0), (tgt, d, 0)], [gt_f, g_fin],
                                                 [(d, f32), (d, bf16)], [(1, 128), (1, d), (1, d)])
    dgg, duu = _ffn_down_bwd(dff, w_d, act_dg, act_du, tm, dff_ // 4)

    def wgrad_rows(name, a, b, n_out):
        kb = 512
        return _matmul(name, a, b, (n_out, d), bf16, (n_out // kb, 2, 1), pl.BlockSpec((s, kb), lambda i, j, k: (0, i)),
                       pl.BlockSpec((s, dh2), lambda i, j, k: (0, j)), pl.BlockSpec((kb, dh2), lambda i, j, k: (i, j)), TN)

    gw_d = wgrad_rows("gw_ffn_down", act, dff, dff_)

    def ffn_in_bwd(name, a, w, after=None):
        return _matmul(name, a, w, (s, d), bf16, (2, mt, 1), pl.BlockSpec((tm, dff_), lambda j, i, k: (i, 0)),
                       pl.BlockSpec((dff_, dh2), lambda j, i, k: (0, j)), pl.BlockSpec((tm, dh2), lambda j, i, k: (i, j)), NN,
                       after=after)

    dh2b = ffn_in_bwd("d_ffn_up_in", duu, w_ut)
    gw_ut = wgrad_rows("gw_ffn_up", duu, h2, dff_)
    gw_gt = wgrad_rows("gw_ffn_gate", dgg, h2, dff_)
    dh2a = ffn_in_bwd("d_ffn_gate_in", dgg, w_gt, after=put_g("ffn", [gw_gt, gw_ut, gw_d]))
    sc_f_late = _after_tokens(sc_f, [sent("ffn", dh2a)])

    def b5(rows, fulls, outs, accs):
        da, db, x1v, dx2v, mov = [r[...].astype(f32) for r in rows]
        sc, g, gt = [r[...] for r in fulls]
        dh = da + db
        r, xh = _rms(x1v)
        accs[0][...] += _colsum(dh)
        accs[1][...] += _colsum(dh * (xh * g))
        dn = dh * (1.0 + sc)
        accs[2][...] += _colsum(dn * xh)
        dxh = dn * g
        dx1 = dx2v + r * (dxh - xh * jnp.mean(dxh * xh, axis=-1, keepdims=True))
        outs[0][...] = dx1
        accs[3][...] += _colsum(dx1 * mov)
        outs[1][...] = (dx1 * gt).astype(bf16)

    dx1, dmo, dsh_f, dsc_f, dg_ffn, dgt_m = _rowwise(
        "d_prenorm_ffn", b5, s, te, [(dh2a, d, 0), (dh2b, d, 0), (x1, d, 0), (dx2, d, 0), (mo, d, 0)], [sc_f_late, g_ffn, gt_m],
        [(d, f32), (d, bf16)], [(1, d)] * 4)
    dya, dyb, dga, dgb = _d_branch_merge(dmo, w_o, merge_factors, tm)
    gw_o = wgrad_rows("gw_mix_out", merged, dmo, d)

    def out_proj_bwd(name, dy, w):
        return _matmul(name, dy, w, (s, dc), bf16, (1, s // tm2, nsh), pl.BlockSpec((tm2, d // nsh), lambda j, i, k: (i, k)),
                       pl.BlockSpec((None, dc, d // nsh), lambda j, i, k: (k, 0, 0)), pl.BlockSpec((tm2, dc), lambda j, i, k: (i, 0)), NT)

    def out_proj_wgrad(name, a, dy):
        return _matmul(name, a, dy, (nsh, dc, d // nsh), bf16, (1, nsh, 1), pl.BlockSpec((s, dc), lambda i, j, k: (0, 0)),
                       pl.BlockSpec((s, d // nsh), lambda i, j, k: (0, j)), pl.BlockSpec((None, dc, d // nsh), lambda i, j, k: (j, 0, 0)), TN)

    dz_a = out_proj_bwd("d_conv_out_in", dya, w_co)
    dog = out_proj_bwd("d_hgrn_out_in", dyb, w_ho)
    gw_co = out_proj_wgrad("gw_conv_out", z_a, dya)
    gw_ho = out_proj_wgrad("gw_hgrn_out", og, dyb)
    conv_w_late = _after_tokens(conv_w, [put_g("mix", [gw_co, gw_ho, gw_o])])
    dab, dac, dax, dconv_w = _conv_bwd(proj, conv_w_late, dz_a, s, dc)
    lb_param_late = _after_tokens(lb_param, [sent("mix", dab)])
    dq, dfl, dvi, dgo, dlb, dgn = _hgrn_bwd(proj, lb_param_late, gnorm, o_saved, states, dog, s, dc, tb)
    dproj = _concat_columns("d_proj", [dab, dac, dax, dq, dfl, dvi, dgo, dga, dgb], te)
    dh = _d_proj_in(dproj, w_in, tm, grad_in(h, dproj, win_sh))

    def b12(rows, fulls, outs, accs):
        dhv, xv, dx1v = [r[...].astype(f32) for r in rows]
        sc, g = fulls[0][...], fulls[1][...]
        r, xh = _rms(xv)
        accs[0][...] += _colsum(dhv)
        accs[1][...] += _colsum(dhv * (xh * g))
        dn = dhv * (1.0 + sc)
        accs[2][...] += _colsum(dn * xh)
        dxh = dn * g
        outs[0][...] = dx1v + r * (dxh - xh * jnp.mean(dxh * xh, axis=-1, keepdims=True))

    dx, dsh_m, dsc_m, dg_mix = _rowwise("d_prenorm_mix", b12, s, tm, [(dh, d, 0), (x, d, 0), (dx1, d, 0)], [sc_m, g_mix],
                                        [(d, f32)], [(1, d)] * 3)
    dmod = [dsh_m, dsc_m, dgt_m, dsh_f, dsc_f, dgt_f]
    small = dict(loss=loss_acc, g_mix=dg_mix, g_ffn=dg_ffn, g_fin=dg_fin, lb=dlb, gnorm=dgn, conv_w=dconv_w)
    return dx, dmod, small


def _ada_fwd(c_all, w_sh, b_sh):
    def body(c_ref, w_ref, b_ref, o_ref):
        cv = c_ref[...]
        ca = (cv * _sigmoid(cv)).astype(bf16)
        o_ref[...] = jnp.dot(ca, w_ref[...].astype(bf16), preferred_element_type=f32) + b_ref[...]

    return pl.pallas_call(body, name="ada_fwd", out_shape=SDS((c_all.shape[0], w_sh.shape[1]), f32),
                          compiler_params=pltpu.CompilerParams(vmem_limit_bytes=V7X_VMEM_LIMIT))(c_all, w_sh, b_sh)


def _ada_wgrad(c_all, dmod_sh):
    def body(c_ref, d_ref, o_ref):
        cv = c_ref[...]
        ca = (cv * _sigmoid(cv)).astype(bf16)
        o_ref[...] = lax.dot_general(ca, d_ref[...].astype(bf16), (TN, ((), ())), preferred_element_type=f32)

    return pl.pallas_call(body, name="ada_wgrad", out_shape=SDS((c_all.shape[1], dmod_sh.shape[1]), f32),
                          compiler_params=pltpu.CompilerParams(vmem_limit_bytes=V7X_VMEM_LIMIT))(c_all, dmod_sh)


def _lb_grad(lb_param, dlb):
    def body(p_ref, d_ref, o_ref):
        p = p_ref[...]
        lb = _sigmoid(p[0:1, :] - p[1:2, :])
        gl = d_ref[...] * lb * (1.0 - lb)
        o_ref[0:1, :] = gl
        o_ref[1:2, :] = -gl

    return pl.pallas_call(body, name="lb_grad", out_shape=SDS(lb_param.shape, f32))(lb_param, dlb)


def _sum_small(gathered):
    def body(g_ref, o_ref):
        acc = g_ref[0]
        for dd in range(1, NDEV):
            acc = acc + g_ref[dd]
        o_ref[...] = acc

    return pl.pallas_call(body, name="sum_small", out_shape=SDS(gathered.shape[1:], f32))(gathered)


def kernel(x, c, w_ada, b_ada, norm_mix_g, w_in, conv_w, lb_param, gnorm_g, w_conv_out, w_hgrn_out, w_o, norm_ffn_g, w_ffn_gate, w_ffn_up, w_ffn_down, norm_final_g, loss_target, m_w_ada, m_b_ada, m_norm_mix_g, m_w_in, m_conv_w, m_lb_param, m_gnorm_g, m_w_conv_out, m_w_hgrn_out, m_w_o, m_norm_ffn_g, m_w_ffn_gate, m_w_ffn_up, m_w_ffn_down, m_norm_final_g, v_w_ada, v_b_ada, v_norm_mix_g, v_w_in, v_conv_w, v_lb_param, v_gnorm_g, v_w_conv_out, v_w_hgrn_out, v_w_o, v_norm_ffn_g, v_w_ffn_gate, v_w_ffn_up, v_w_ffn_down, v_norm_final_g):
    assert lb_param.shape[0] == 2 and w_ada.shape[0] == 1
    s, d = x.shape[1], x.shape[2]
    me = 4 * lax.axis_index("x") + 2 * lax.axis_index("y") + lax.axis_index("c")
    ada_cols = w_ada.shape[2]

    me1 = me.astype(jnp.int32).reshape(1)
    placed_in = [_place_shard("place_in0", me1, w_in[0])[0]]
    c_all, cw_all = _all_gather("gather_cond", [c, conv_w[0]], after=placed_in)
    c_all = c_all.reshape(NDEV, d)
    conv_w_full = jnp.transpose(cw_all, (1, 0, 2)).reshape(conv_w.shape[1], -1)
    b_sh = lax.dynamic_slice_in_dim(b_ada, me * ada_cols, ada_cols, axis=1)
    mod_cols = _ada_fwd(c_all, w_ada[0], b_sh)
    mod_all, = _all_gather("gather_mod", [mod_cols])
    mod = lax.dynamic_index_in_dim(mod_all, me, axis=1, keepdims=False).reshape(N_MOD, 1, d)

    shard_groups = {"mix": [w_conv_out[0], w_hgrn_out[0], w_o[0]], "ffn": [w_ffn_gate[0].T, w_ffn_up[0].T],
                    "ffn_down": [w_ffn_down[0]]}
    own_slot = lambda frm, to: _flat(frm)
    gather_plan = lambda n: [(a, j, own_slot, own_slot) for a in range(n) for j in (1,) + ICI_RELATIONS]
    flat = lambda a: a.reshape(a.shape[0] * a.shape[1], a.shape[2])
    to8 = lambda a: a.reshape(NDEV, a.shape[0] // NDEV, a.shape[1])
    near, far = ICI_RELATIONS[:2], ICI_RELATIONS[2:]
    in_plan = lambda rels: [(0, j, own_slot, own_slot) for j in rels]
    ss_a, rs_a, _, lands, tok_a = _push_start("gather_start_in_sib", [], placed_in, in_plan((1,)), after=mod_all)
    ss_b, rs_b, _, lands_in, tok_b = _push_start("gather_start_in_near", [], lands, in_plan(near), after=tok_a)
    gathering, tokens, placed = {}, [tok_a, tok_b], {}
    for grp, sh in shard_groups.items():
        placed[grp] = []
        for i, a in enumerate(sh):
            buf, tok = _place_shard(f"place_{grp}{i}", me1, a, after=tokens[-1])
            placed[grp].append(buf)
            tokens.append(tok)

    pos = (lax.axis_index("x"), lax.axis_index("y"), lax.axis_index("c"))
    ids = lambda frm, rels: jnp.stack([_flat(_peer(frm, j)) for j in rels]).astype(jnp.int32)

    def project(h, tm):
        sib = _peer(pos, 1)
        fwd_of = lambda idxs: [(0, 1, fwd_slot(i), fwd_slot(i)) for i in idxs]
        _, lands = _push_wait("gather_wait_in_sib", ss_a, rs_a, [], lands_in, in_plan((1,)), h)
        proj = _proj_part("proj_local", h, lands[0], ids(pos, (0, 1)), None, tm)
        _, lands = _push_wait("gather_wait_in_near", ss_b, rs_b, [], lands, in_plan(near), proj)
        ss_c, rs_c, _, lands, tok = _push_start("gather_start_in_far", [], lands, in_plan(far))
        for grp in shard_groups:
            ss, rs, _, bufs, tok = _push_start("gather_start_" + grp, [], placed[grp], gather_plan(len(placed[grp])), after=tok)
            gathering[grp] = (ss, rs, bufs)
        ss1, rs1, _, lands, tok = _push_start("gather_fwd_in_near", [], lands, fwd_of((0, 1)), after=tok)
        proj = _proj_part("proj_near", h, lands[0], ids(pos, near), proj, tm, after=tok)
        _, lands = _push_wait("gather_fwd_wait_in_near", ss1, rs1, [], lands, fwd_of((0, 1)), proj)
        proj = _proj_part("proj_fwd_near", h, lands[0], ids(sib, near), proj, tm)
        _, lands = _push_wait("gather_wait_in_far", ss_c, rs_c, [], lands, in_plan(far), proj)
        ss2, rs2, _, lands, tok = _push_start("gather_fwd_in_far", [], lands, fwd_of((2,)))
        proj = _proj_part("proj_far", h, lands[0], ids(pos, far), proj, tm, after=tok)
        _, lands = _push_wait("gather_fwd_wait_in_far", ss2, rs2, [], lands, fwd_of((2,)), proj)
        proj = _proj_part("proj_fwd_far", h, lands[0], ids(sib, far), proj, tm)
        return proj, lands[0]

    fwd_slot = lambda i: (lambda frm, to: _flat(_peer(frm, ICI_RELATIONS[i])))
    fwd_plan = lambda n: [(a, 1, fwd_slot(i), fwd_slot(i)) for a in range(n) for i in range(len(ICI_RELATIONS))]
    forwarding = {}

    def prefetch(grp, after):
        ss, rs, lands = gathering[grp]
        _, lands = _push_wait("gather_wait_" + grp, ss, rs, [], lands, gather_plan(len(lands)), after)
        ss, rs, _, lands, tok = _push_start("gather_fwd_" + grp, [], lands, fwd_plan(len(lands)))
        forwarding[grp] = (ss, rs, lands)
        return tok

    def get_w(grp, after):
        ss, rs, lands = forwarding[grp]
        _, full = _push_wait("gather_fwd_wait_" + grp, ss, rs, [], lands, fwd_plan(len(lands)), after)
        return [f if i < 2 and grp == "mix" else flat(f) for i, f in enumerate(full)]

    core = lax.axis_index("c").astype(jnp.int32).reshape(1)
    chip = (2 * lax.axis_index("x") + lax.axis_index("y")).astype(jnp.int32).reshape(1)
    scatter_plan = lambda n: [(a, j, lambda frm, to: _chip(to), lambda frm, to: _chip(frm)) for a in range(n) for j in ICI_RELATIONS]
    scattering = {}

    swap_plan = lambda n: [(a, 1, functools.partial(lambda frm, to, q: 2 * q + to[2], q=q), functools.partial(lambda frm, to, q: q, q=q))
                           for a in range(n) for q in range(NDEV // 2)]
    swapping = {}

    def start_ici(grp, g8, recv):
        pairs = [_pair_sum(f"pair_sum_{grp}{i}", core, g, r, _row_tile(g.shape[1], 1024)) for i, (g, r) in enumerate(zip(g8, recv))]
        lands = [lax.empty(p.shape, p.dtype) for p in pairs]
        ss, rs, srcs, lands, tok = _push_start("scatter_start_" + grp, pairs, lands, scatter_plan(len(pairs)))
        scattering[grp] = (ss, rs, srcs, lands)
        return tok

    def grad_in(h, dproj, n):
        halves = [jnp.stack([2 * q + cc for q in range(NDEV // 2)]).astype(jnp.int32) for cc in (1 - pos[2], pos[2])]
        gw = _gw_part("gw_proj_sibling", h, dproj, n, halves[0], None)
        lands = [lax.empty((NDEV // 2,) + gw.shape[1:], gw.dtype)]
        ss, rs, srcs, lands, tok = _push_start("scatter_swap_in", [gw], lands, swap_plan(1))
        gw = _gw_part("gw_proj_own", h, dproj, n, halves[1], srcs[0], after=tok)
        g8, recv = _push_wait("scatter_swapped_in", ss, rs, [gw], lands, swap_plan(1), gw)
        return start_ici("in", g8, recv)

    def put_g(grp, grads):
        g8 = [g if g.ndim == 3 else to8(g) for g in grads]
        lands = [lax.empty((NDEV // 2,) + g.shape[1:], g.dtype) for g in g8]
        ss, rs, srcs, lands, tok = _push_start("scatter_swap_" + grp, g8, lands, swap_plan(len(g8)))
        swapping[grp] = (ss, rs, srcs, lands)
        return tok

    def sent(grp, after):
        ss, rs, srcs, lands = swapping[grp]
        g8, recv = _push_wait("scatter_swapped_" + grp, ss, rs, srcs, lands, swap_plan(len(srcs)), after)
        return start_ici(grp, g8, recv)

    def reduced(grp, after, names):
        ss, rs, srcs, lands = scattering[grp]
        srcs, lands = _push_wait("scatter_wait_" + grp, ss, rs, srcs, lands, scatter_plan(len(srcs)), after)
        for i, (p, r, nm) in enumerate(zip(srcs, lands, names)):
            tr = nm in ("w_ffn_gate", "w_ffn_up")
            wmv = tuple(a[0].T if tr else a[0] for a in weights[nm])
            out = _chip_sum_adamw(f"chip_sum_{grp}{i}", chip, p, r, _row_tile(p.shape[1]), wmv)
            res[nm] = [(a.T if tr else a).reshape(weights[nm][0].shape) for a in out]

    dx, dmod, small = _local_step(x[0], loss_target[0], mod, _after_tokens(norm_mix_g, tokens), norm_ffn_g,
                                  norm_final_g.reshape(1, d), lb_param, gnorm_g, conv_w_full, project, grad_in, get_w, prefetch, put_g, sent)

    pieces = [*dmod, small["g_mix"], small["g_ffn"], small["g_fin"], small["lb"], small["gnorm"], small["loss"],
              small["conv_w"].reshape(1, -1)]
    widths = [p.shape[1] for p in pieces]
    offs = np.concatenate([[0], np.cumsum(widths)])
    packed = jnp.concatenate(pieces, axis=1)
    gathered, = _all_gather("gather_small", [packed])
    summed = _sum_small(gathered)
    part = lambda i: summed[:, offs[i]:offs[i + 1]]
    g_b_ada = summed[:, :N_MOD * d]
    g_norm_mix, g_norm_ffn, g_norm_fin, g_lb_row, g_gnorm, loss_vec, g_convw_flat = [part(i) for i in range(N_MOD, N_MOD + 7)]
    loss = loss_vec[0, 0]
    dmod_all = gathered[:, 0, :N_MOD * d]
    g_w_ada = _ada_wgrad(c_all, lax.dynamic_slice_in_dim(dmod_all, me * ada_cols, ada_cols, axis=1))
    g_lb = _lb_grad(lb_param, g_lb_row)
    cw_cols = conv_w.shape[2]
    g_conv_w = lax.dynamic_slice_in_dim(g_convw_flat.reshape(conv_w.shape[1], -1), me * cw_cols, cw_cols, axis=1)

    grads = dict(w_ada=g_w_ada, b_ada=g_b_ada, norm_mix_g=g_norm_mix, conv_w=g_conv_w, lb_param=g_lb, gnorm_g=g_gnorm,
                 norm_ffn_g=g_norm_ffn, norm_final_g=g_norm_fin)
    weights = dict(w_ada=(w_ada, m_w_ada, v_w_ada), b_ada=(b_ada, m_b_ada, v_b_ada), norm_mix_g=(norm_mix_g, m_norm_mix_g, v_norm_mix_g),
                   w_in=(w_in, m_w_in, v_w_in), conv_w=(conv_w, m_conv_w, v_conv_w), lb_param=(lb_param, m_lb_param, v_lb_param),
                   gnorm_g=(gnorm_g, m_gnorm_g, v_gnorm_g), w_conv_out=(w_conv_out, m_w_conv_out, v_w_conv_out),
                   w_hgrn_out=(w_hgrn_out, m_w_hgrn_out, v_w_hgrn_out), w_o=(w_o, m_w_o, v_w_o),
                   norm_ffn_g=(norm_ffn_g, m_norm_ffn_g, v_norm_ffn_g), w_ffn_gate=(w_ffn_gate, m_w_ffn_gate, v_w_ffn_gate),
                   w_ffn_up=(w_ffn_up, m_w_ffn_up, v_w_ffn_up), w_ffn_down=(w_ffn_down, m_w_ffn_down, v_w_ffn_down),
                   norm_final_g=(norm_final_g, m_norm_final_g, v_norm_final_g))
    res = {}

    def update(nm):
        w, m, v = weights[nm]
        shape2 = (w.shape[-2], w.shape[-1]) if w.ndim >= 2 else (1, w.shape[0])
        g2 = grads[nm].reshape(shape2)
        dl, mn, vn = _adamw("adamw_" + nm, g2, w.reshape(shape2), m.reshape(shape2), v.reshape(shape2))
        res[nm] = [a.reshape(w.shape) for a in (g2, dl, mn, vn)]

    for nm in list(grads):
        update(nm)
    reduced("ffn", res["w_ada"][1], ("w_ffn_gate", "w_ffn_up", "w_ffn_down"))
    reduced("mix", res["w_ffn_down"][1], ("w_conv_out", "w_hgrn_out", "w_o"))
    reduced("in", res["w_o"][1], ("w_in",))
    outs = [[res[nm][i] for nm in weights] for i in range(4)]
    return (loss, dx.reshape(x.shape), *outs[0], *outs[1], *outs[2], *outs[3])
```

```python
import functools

import numpy as np
import jax
import jax.numpy as jnp
from jax import lax
from jax.experimental import pallas as pl
from jax.experimental.pallas import tpu as pltpu

f32, bf16 = jnp.float32, jnp.bfloat16
SDS = jax.ShapeDtypeStruct

EPS = 1e-6
HEADS, DK, CHUNK = 8, 128, 64
HEADS_PER_STEP = 4
N_MOD = 6
NDEV = 8
ADAM_LR, ADAM_B1, ADAM_B2, ADAM_EPS, ADAM_WD, ADAM_STEP = 0.001, 0.9, 0.999, 1e-08, 0.01, 10
LEVELS = (32, 16, 8, 4, 2, 1)
V7X_VMEM_LIMIT = 56 * 1024 * 1024
HBM = pl.BlockSpec(memory_space=pltpu.HBM)
MESH = pl.DeviceIdType.MESH


def _params(sem):
    return pltpu.CompilerParams(dimension_semantics=sem, vmem_limit_bytes=V7X_VMEM_LIMIT)


def _sigmoid(x):
    return jax.nn.sigmoid(x)


def _dsilu(x, s):
    return s * (1.0 + x * (1.0 - s))


def _mesh_pos():
    x, y, c = lax.axis_index("x"), lax.axis_index("y"), lax.axis_index("c")
    return x, y, c


def _peer(pos, j):
    x, y, c = pos
    return (1 - x if j & 4 else x, 1 - y if j & 2 else y, 1 - c if j & 1 else c)


def _flat(pos):
    return 4 * pos[0] + 2 * pos[1] + pos[2]


def _all_gather(name, arrs, after=()):
    n, ne = len(arrs), len(after)
    out_shapes = [SDS((NDEV,) + a.shape, a.dtype) for a in arrs]

    def body(*refs):
        ins, outs = refs[:n], refs[n + ne:2 * n + ne]
        send_sems, recv_sems, local_sems = refs[2 * n + ne:]
        pos = _mesh_pos()
        me = _flat(pos)

        def copy(a, j, frm, to_pos):
            k = a * (NDEV - 1) + j - 1
            return pltpu.make_async_remote_copy(src_ref=ins[a], dst_ref=outs[a].at[frm], send_sem=send_sems.at[k],
                                                recv_sem=recv_sems.at[k], device_id=to_pos, device_id_type=MESH)

        local = [pltpu.make_async_copy(ins[a], outs[a].at[me], local_sems.at[a]) for a in range(n)]
        for cp in local:
            cp.start()
        sends = [copy(a, j, me, _peer(pos, j)) for j in range(1, NDEV) for a in range(n)]
        for cp in sends:
            cp.start()
        for j in range(1, NDEV):
            for a in range(n):
                copy(a, j, _flat(_peer(pos, j)), pos).wait_recv()
        for cp in sends:
            cp.wait_send()
        for cp in local:
            cp.wait()

    return pl.pallas_call(
        body, name=name, out_shape=out_shapes, in_specs=[HBM] * n + [ANY] * ne, out_specs=[HBM] * n,
        scratch_shapes=[pltpu.SemaphoreType.DMA((n * (NDEV - 1),)), pltpu.SemaphoreType.DMA((n * (NDEV - 1),)),
                        pltpu.SemaphoreType.DMA((n,))],
    )(*arrs, *after)


SEM = pl.BlockSpec(memory_space=pltpu.SEMAPHORE)
ANY = pl.BlockSpec(memory_space=pl.ANY)
EFFECT = pltpu.SideEffectType.DATAFLOW_SIDE_EFFECTING
ICI_RELATIONS = (2, 4, 6)


def _chip(pos):
    return 2 * pos[0] + pos[1]


def _hbm(a):
    return pltpu.with_memory_space_constraint(a, pltpu.HBM)


def _plan_copy(plan_entry, k, pos, frm, to, src_refs, land_refs, send_sems, recv_sems):
    a, _, src_slot, dst_slot = plan_entry
    s = src_refs[a] if src_slot is None else src_refs[a].at[src_slot(frm, to)]
    return pltpu.make_async_remote_copy(src_ref=s, dst_ref=land_refs[a].at[dst_slot(frm, to)], send_sem=send_sems.at[k],
                                        recv_sem=recv_sems.at[k], device_id=to, device_id_type=MESH)


def _push_start(name, srcs, lands, plan, after=None):
    ns, nb, nk = len(srcs), len(srcs) + len(lands), len(plan)
    extra = [] if after is None else [after]

    def body(*refs):
        land_refs = refs[ns:nb]
        src_refs = refs[:ns] if ns else land_refs
        send_sems, recv_sems = refs[nb + len(extra)], refs[nb + len(extra) + 1]
        pos = _mesh_pos()
        for k, e in enumerate(plan):
            _plan_copy(e, k, pos, pos, _peer(pos, e[1]), src_refs, land_refs, send_sems, recv_sems).start()
        refs[-1][...] = jnp.zeros_like(refs[-1])

    outs = pl.pallas_call(
        body, name=name,
        out_shape=[pltpu.SemaphoreType.DMA((nk,)), pltpu.SemaphoreType.DMA((nk,))] + [pltpu.HBM(a.shape, a.dtype) for a in srcs + lands]
        + [SDS((8, 128), f32)],
        in_specs=[HBM] * nb + [ANY] * len(extra), out_specs=[SEM, SEM] + [HBM] * nb + [pl.BlockSpec(memory_space=pltpu.VMEM)],
        input_output_aliases={i: 2 + i for i in range(nb)},
        compiler_params=pltpu.CompilerParams(has_side_effects=EFFECT),
    )(*[_hbm(a) for a in srcs + lands], *extra)
    return outs[0], outs[1], list(outs[2:2 + ns]), list(outs[2 + ns:2 + nb]), outs[-1]


def _push_wait(name, send_sems, recv_sems, srcs, lands, plan, after):
    ns, nb = len(srcs), len(srcs) + len(lands)

    def body(*refs):
        land_refs = refs[ns:nb]
        src_refs = refs[:ns] if ns else land_refs
        ssem, rsem = refs[nb], refs[nb + 1]
        pos = _mesh_pos()
        for k, e in enumerate(plan):
            peer = _peer(pos, e[1])
            _plan_copy(e, k, pos, pos, peer, src_refs, land_refs, ssem, rsem).wait_send()
            _plan_copy(e, k, pos, peer, pos, src_refs, land_refs, ssem, rsem).wait_recv()

    outs = pl.pallas_call(
        body, name=name, out_shape=[pltpu.HBM(a.shape, a.dtype) for a in srcs + lands],
        in_specs=[HBM] * nb + [SEM, SEM, ANY], out_specs=[HBM] * nb,
        input_output_aliases={i: i for i in range(nb)},
        compiler_params=pltpu.CompilerParams(has_side_effects=EFFECT),
    )(*srcs, *lands, send_sems, recv_sems, after)
    return list(outs[:ns]), list(outs[ns:])


def _after_tokens(small, tokens):
    for t in tokens:
        if t is not None:
            small = small + t[0:1, 0:1].reshape((1,) * small.ndim)
    return small


def _place_shard(name, me, shard, after=None):
    r, c = shard.shape
    tr = _row_tile(r)
    extra = [] if after is None else [after]

    def body(me_ref, s_ref, *rest):
        rest[-2][...] = s_ref[...].astype(bf16)
        rest[-1][...] = jnp.zeros_like(rest[-1])

    return pl.pallas_call(
        body, name=name, out_shape=[SDS((NDEV, r, c), bf16), SDS((8, 128), f32)],
        grid_spec=pltpu.PrefetchScalarGridSpec(
            num_scalar_prefetch=1, grid=(r // tr,), in_specs=[pl.BlockSpec((tr, c), lambda i, me_ref: (i, 0))] + [ANY] * len(extra),
            out_specs=[pl.BlockSpec((None, tr, c), lambda i, me_ref: (me_ref[0], i, 0)),
                       pl.BlockSpec((8, 128), lambda i, me_ref: (0, 0))]),
        compiler_params=_params(("arbitrary",)),
    )(me, shard, *extra)


def _row_tile(r, most=256):
    return max(t for t in range(16, most + 1, 16) if r % t == 0)


def _pair_sum(name, core, grad, recv, tr):
    _, r, c = grad.shape
    nchip = NDEV // 2

    def body(core_ref, g_ref, r_ref, o_ref):
        o_ref[...] = (g_ref[...].astype(f32) + r_ref[...].astype(f32)).astype(o_ref.dtype)

    return pl.pallas_call(
        body, name=name, out_shape=SDS((nchip, r, c), grad.dtype),
        grid_spec=pltpu.PrefetchScalarGridSpec(
            num_scalar_prefetch=1, grid=(nchip, r // tr),
            in_specs=[pl.BlockSpec((None, tr, c), lambda q, i, core_ref: (2 * q + core_ref[0], i, 0)),
                      pl.BlockSpec((None, tr, c), lambda q, i, core_ref: (q, i, 0))],
            out_specs=pl.BlockSpec((None, tr, c), lambda q, i, core_ref: (q, i, 0))),
        compiler_params=_params(("parallel", "parallel")),
    )(core, grad, recv)


def _adamw_math(g, w, m, v):
    mn = ADAM_B1 * m + (1.0 - ADAM_B1) * g
    vn = ADAM_B2 * v + (1.0 - ADAM_B2) * jnp.square(g)
    m_hat = mn / (1.0 - ADAM_B1 ** ADAM_STEP)
    v_hat = vn / (1.0 - ADAM_B2 ** ADAM_STEP)
    return -ADAM_LR * (m_hat / (jnp.sqrt(v_hat) + ADAM_EPS) + ADAM_WD * w), mn, vn


def _chip_sum_adamw(name, chip, pairs, recv, tr, wmv):
    nchip, r, c = pairs.shape

    def body(chip_ref, p_ref, r_ref, w_ref, m_ref, v_ref, g_ref, d_ref, mo_ref, vo_ref):
        mine = chip_ref[0]
        own = p_ref[...].astype(f32)
        acc = jnp.zeros((tr, c), f32)
        for q in range(nchip):
            acc = acc + jnp.where(mine == q, own, r_ref[q].astype(f32))
        g_ref[...] = acc
        d_ref[...], mo_ref[...], vo_ref[...] = _adamw_math(acc, w_ref[...], m_ref[...], v_ref[...])

    blk = pl.BlockSpec((tr, c), lambda i, chip_ref: (i, 0))
    return pl.pallas_call(
        body, name=name, out_shape=[SDS((r, c), f32)] * 4,
        grid_spec=pltpu.PrefetchScalarGridSpec(
            num_scalar_prefetch=1, grid=(r // tr,),
            in_specs=[pl.BlockSpec((None, tr, c), lambda i, chip_ref: (chip_ref[0], i, 0)),
                      pl.BlockSpec((nchip, tr, c), lambda i, chip_ref: (0, i, 0))] + [blk] * 3,
            out_specs=[blk] * 4),
        compiler_params=_params(("parallel",)),
    )(chip, pairs, recv, *wmv)


def _matmul(name, a, b, out_shape, out_dtype, grid, a_spec, b_spec, o_spec, dims, after=None):
    ksteps = grid[2]
    acc_shape = tuple(d for d in o_spec.block_shape if d is not None)
    extra = [] if after is None else [after]

    def body(a_ref, b_ref, *rest):
        o_ref, acc = rest[len(extra)], rest[len(extra) + 1:]
        prod = lax.dot_general(a_ref[...], b_ref[...], (dims, ((), ())), preferred_element_type=f32)
        if ksteps == 1:
            o_ref[...] = prod.astype(o_ref.dtype)
        else:
            k = pl.program_id(2)

            @pl.when(k == 0)
            def _():
                acc[0][...] = prod

            @pl.when(k > 0)
            def _():
                acc[0][...] += prod

            @pl.when(k == ksteps - 1)
            def _():
                o_ref[...] = acc[0][...].astype(o_ref.dtype)

    return pl.pallas_call(
        body, name=name, grid=grid, in_specs=[a_spec, b_spec] + [ANY] * len(extra), out_specs=o_spec,
        out_shape=SDS(out_shape, out_dtype), scratch_shapes=[] if ksteps == 1 else [pltpu.VMEM(acc_shape, f32)],
        compiler_params=_params(("parallel", "parallel", "arbitrary")),
    )(a, b, *extra)


NN, NT, TN = ((1,), (0,)), ((1,), (1,)), ((0,), (0,))


def _concat_columns(name, pieces, tm):
    s = pieces[0].shape[0]
    widths = [p.shape[1] for p in pieces]
    offs = [sum(widths[:i]) for i in range(len(widths))]

    def body(*refs):
        o_ref = refs[-1]
        for r, o, w in zip(refs[:-1], offs, widths):
            o_ref[:, o:o + w] = r[...]

    return pl.pallas_call(
        body, name=name, grid=(s // tm,), in_specs=[pl.BlockSpec((tm, w), lambda i: (i, 0)) for w in widths],
        out_specs=pl.BlockSpec((tm, sum(widths)), lambda i: (i, 0)), out_shape=SDS((s, sum(widths)), pieces[0].dtype),
        compiler_params=_params(("parallel",)),
    )(*pieces)


def _d_proj_in(dproj, w_sh, tm, after):
    s = dproj.shape[0]
    nsh, d, n = w_sh.shape
    ksteps = nsh // 2
    extra = [] if after is None else [after]

    def body(a_ref, b_ref, *rest):
        o_ref, acc = rest[len(extra)], rest[len(extra) + 1]
        k = pl.program_id(1)
        a = a_ref[...]
        prod = lax.dot_general(a[:, :n], b_ref[0], (NT, ((), ())), preferred_element_type=f32)
        prod += lax.dot_general(a[:, n:], b_ref[1], (NT, ((), ())), preferred_element_type=f32)

        @pl.when(k == 0)
        def _():
            acc[...] = prod

        @pl.when(k > 0)
        def _():
            acc[...] += prod

        @pl.when(k == ksteps - 1)
        def _():
            o_ref[...] = acc[...].astype(o_ref.dtype)

    return pl.pallas_call(
        body, name="d_proj_in", grid=(s // tm, ksteps),
        in_specs=[pl.BlockSpec((tm, 2 * n), lambda i, k: (i, k)), pl.BlockSpec((2, d, n), lambda i, k: (k, 0, 0))] + [ANY] * len(extra),
        out_specs=pl.BlockSpec((tm, d), lambda i, k: (i, 0)), out_shape=SDS((s, d), bf16),
        scratch_shapes=[pltpu.VMEM((tm, d), f32)], compiler_params=_params(("parallel", "arbitrary")),
    )(dproj, w_sh, *extra)


def _gw_part(name, h, dproj, n, ids, prev, after=None):
    s, d = h.shape
    kb = 512
    extra = ([] if prev is None else [prev]) + ([] if after is None else [after])

    def body(ids_ref, a_ref, b_ref, *rest):
        rest[-1][...] = lax.dot_general(a_ref[...], b_ref[...], (TN, ((), ())), preferred_element_type=f32).astype(bf16)

    return pl.pallas_call(
        body, name=name, out_shape=SDS((dproj.shape[1] // n, d, n), bf16),
        grid_spec=pltpu.PrefetchScalarGridSpec(
            num_scalar_prefetch=1, grid=(ids.shape[0], d // kb),
            in_specs=[pl.BlockSpec((s, kb), lambda j, i, ids_ref: (0, i)),
                      pl.BlockSpec((s, n), lambda j, i, ids_ref: (0, ids_ref[j]))] + [ANY] * len(extra),
            out_specs=pl.BlockSpec((None, kb, n), lambda j, i, ids_ref: (ids_ref[j], i, 0))),
        input_output_aliases={} if prev is None else {3: 0},
        compiler_params=_params(("parallel", "parallel")),
    )(ids, h, dproj, *extra)


def _proj_part(name, h, w_sh, ids, prev, tm, after=None):
    s, d = h.shape
    nsh, _, n = w_sh.shape
    extra = ([] if prev is None else [prev]) + ([] if after is None else [after])

    def body(ids_ref, a_ref, b_ref, *rest):
        rest[-1][...] = jnp.dot(a_ref[...], b_ref[...], preferred_element_type=f32)

    return pl.pallas_call(
        body, name=name, out_shape=SDS((s, nsh * n), f32),
        grid_spec=pltpu.PrefetchScalarGridSpec(
            num_scalar_prefetch=1, grid=(ids.shape[0], s // tm),
            in_specs=[pl.BlockSpec((tm, d), lambda j, i, ids_ref: (i, 0)),
                      pl.BlockSpec((None, d, n), lambda j, i, ids_ref: (ids_ref[j], 0, 0))] + [ANY] * len(extra),
            out_specs=pl.BlockSpec((tm, n), lambda j, i, ids_ref: (i, ids_ref[j]))),
        input_output_aliases={} if prev is None else {3: 0},
        compiler_params=_params(("parallel", "parallel")),
    )(ids, h, w_sh, *extra)


def _rowwise(name, fn, n_rows, tm, rows, fulls, row_outs, acc_outs=(), ncol=1):
    assert ncol == 1 or not acc_outs
    nr, nf, no, na = len(rows), len(fulls), len(row_outs), len(acc_outs)
    in_specs = [pl.BlockSpec((tm, w), functools.partial(lambda i, j, cb: (i, cb + j), cb=cb)) for (_, w, cb) in rows]
    in_specs += [pl.BlockSpec(a.shape, functools.partial(lambda i, j, nd: (0,) * nd, nd=a.ndim)) for a in fulls]
    out_shape = [SDS((n_rows, w), dt) for (w, dt) in row_outs] + [SDS(s, f32) for s in acc_outs]
    out_specs = [pl.BlockSpec((tm, w // ncol), lambda i, j: (i, j)) for (w, _) in row_outs]
    out_specs += [pl.BlockSpec(s, functools.partial(lambda i, j, nd: (0,) * nd, nd=len(s))) for s in acc_outs]

    def body(*refs):
        if na:
            @pl.when(pl.program_id(0) == 0)
            def _():
                for r in refs[nr + nf + no:]:
                    r[...] = jnp.zeros(r.shape, r.dtype)
        fn(refs[:nr], refs[nr:nr + nf], refs[nr + nf:nr + nf + no], refs[nr + nf + no:])

    return pl.pallas_call(
        body, name=name, grid=(n_rows // tm, ncol), in_specs=in_specs, out_specs=out_specs, out_shape=out_shape,
        compiler_params=_params(("arbitrary" if na else "parallel", "arbitrary" if na else "parallel")),
    )(*[r[0] for r in rows], *fulls)


def _rms(x):
    r = lax.rsqrt(jnp.mean(x * x, axis=-1, keepdims=True) + EPS)
    return r, x * r


def _colsum(v):
    return jnp.sum(v, axis=0, keepdims=True)


def _shift_down(u, row, k):
    return jnp.where(row >= k, pltpu.roll(u, k, 0), 0.0)


def _shift_up(u, row, k):
    n = u.shape[0]
    return jnp.where(row < n - k, pltpu.roll(u, n - k, 0), 0.0)


def _conv_fwd(proj, conv_w, s, dc):
    nb = dc // 128

    def body(ab_ref, ac_ref, ax_ref, w_ref, z_ref):
        u = ac_ref[...] * ax_ref[...]
        row = lax.broadcasted_iota(jnp.int32, u.shape, 0)
        w = w_ref[...]
        cv = w[0:1] * _shift_down(u, row, 2) + w[1:2] * _shift_down(u, row, 1) + w[2:3] * u
        z_ref[...] = (ab_ref[...] * cv).astype(z_ref.dtype)

    col = lambda off: pl.BlockSpec((s, 128), functools.partial(lambda j, off: (0, off + j), off=off))
    return pl.pallas_call(
        body, name="conv_fwd", grid=(nb,), in_specs=[col(0), col(nb), col(2 * nb), pl.BlockSpec((3, 128), lambda j: (0, j))],
        out_specs=pl.BlockSpec((s, 128), lambda j: (0, j)), out_shape=SDS((s, dc), bf16), compiler_params=_params(("parallel",)),
    )(proj, proj, proj, conv_w)


def _conv_bwd(proj, conv_w, dz, s, dc):
    nb = dc // 128

    def body(ab_ref, ac_ref, ax_ref, w_ref, dz_ref, dab_ref, dac_ref, dax_ref, dw_ref):
        ab, ac, ax, dzv = ab_ref[...], ac_ref[...], ax_ref[...], dz_ref[...].astype(f32)
        u = ac * ax
        row = lax.broadcasted_iota(jnp.int32, u.shape, 0)
        w = w_ref[...]
        u1, u2 = _shift_down(u, row, 1), _shift_down(u, row, 2)
        cv = w[0:1] * u2 + w[1:2] * u1 + w[2:3] * u
        dcv = dzv * ab
        dab_ref[...] = (dzv * cv).astype(dab_ref.dtype)
        du = w[2:3] * dcv + w[1:2] * _shift_up(dcv, row, 1) + w[0:1] * _shift_up(dcv, row, 2)
        dac_ref[...] = (du * ax).astype(dac_ref.dtype)
        dax_ref[...] = (du * ac).astype(dax_ref.dtype)
        dw_ref[0:1, :] = _colsum(dcv * u2)
        dw_ref[1:2, :] = _colsum(dcv * u1)
        dw_ref[2:3, :] = _colsum(dcv * u)

    col = lambda off: pl.BlockSpec((s, 128), functools.partial(lambda j, off: (0, off + j), off=off))
    blk = pl.BlockSpec((s, 128), lambda j: (0, j))
    return pl.pallas_call(
        body, name="conv_bwd", grid=(nb,),
        in_specs=[col(0), col(nb), col(2 * nb), pl.BlockSpec((3, 128), lambda j: (0, j)), blk],
        out_specs=[blk, blk, blk, pl.BlockSpec((3, 128), lambda j: (0, j))],
        out_shape=[SDS((s, dc), bf16)] * 3 + [SDS((3, dc), f32)], compiler_params=_params(("parallel",)),
    )(proj, proj, proj, conv_w, dz)


def _level_masks():
    t = np.arange(CHUNK)[:, None]
    s = np.arange(CHUNK)[None, :]
    m = np.stack([((t & h) != 0) & ((s & h) == 0) & (t // (2 * h) == s // (2 * h)) for h in LEVELS]).astype(np.float32)
    return jnp.asarray(m), jnp.asarray(m.transpose(0, 2, 1))


def _cumsum_rows(x, row):
    for sh in (1, 2, 4, 8, 16, 32):
        x = x + jnp.where(row >= sh, pltpu.roll(x, sh, 0), 0.0)
    return x


def _rev_cumsum_rows(x, row):
    n = x.shape[0]
    for sh in (1, 2, 4, 8, 16, 32):
        x = x + jnp.where(row < n - sh, pltpu.roll(x, n - sh, 0), 0.0)
    return x


def _chunk_terms(qp, fl, lb):
    row = lax.broadcasted_iota(jnp.int32, qp.shape, 0)
    sig = _sigmoid(fl)
    f = lb + (1.0 - lb) * sig
    k = 1.0 - f
    sq = _sigmoid(qp)
    qh = qp * sq
    b = _cumsum_rows(jnp.log(f), row)
    sub = lax.broadcasted_iota(jnp.int32, (CHUNK // 8, 8, DK), 1)
    b8 = b.reshape(CHUNK // 8, 8, DK)
    us, exs, ups = [], [], []
    for m in LEVELS:
        sb = 2 * m
        if sb >= 8:
            b3 = b.reshape(CHUNK // sb, sb, DK)
            bref = jnp.broadcast_to(b3[:, m - 1:m, :], b3.shape).reshape(CHUNK, DK)
        else:
            bref8 = None
            for j in range(8 // sb):
                cand = jnp.broadcast_to(b8[:, j * sb + m - 1:j * sb + m, :], b8.shape)
                bref8 = cand if bref8 is None else jnp.where(sub >= j * sb, cand, bref8)
            bref = bref8.reshape(CHUNK, DK)
        up = (row & m) != 0
        ex = jnp.exp(jnp.where(up, b - bref, bref - b))
        us.append((jnp.where(up, qh, k) * ex).astype(bf16))
        exs.append(ex)
        ups.append(up)
    blast = b[CHUNK - 1:CHUNK, :]
    eb, ebl = jnp.exp(b), jnp.exp(blast - b)
    return dict(sig=sig, f=f, k=k, sq=sq, qh=qh, u=jnp.stack(us), ex=exs, up=ups, eb=eb, ebl=ebl, qt=qh * eb, kt=k * ebl,
                el=jnp.exp(blast), row=row)


def _scores(t, mask):
    pl_ = jnp.einsum("ltk,lsk->lts", t["u"], t["u"], preferred_element_type=f32)
    p = jnp.sum(pl_ * mask, axis=0)
    r = lax.broadcasted_iota(jnp.int32, (CHUNK, CHUNK), 0)
    c = lax.broadcasted_iota(jnp.int32, (CHUNK, CHUNK), 1)
    diag = jnp.sum(t["qh"] * t["k"], axis=-1, keepdims=True)
    return p + jnp.where(r == c, diag, 0.0)


def _hgrn_fwd(proj, lb_param, gnorm, s, dv_total, tb):
    nchunk = tb // CHUNK
    masks, _ = _level_masks()
    q0, f0, v0, g0 = 3 * HEADS, 4 * HEADS, 5 * HEADS, 6 * HEADS

    def body(q_ref, f_ref, v_ref, g_ref, lb_ref, gn_ref, mask_ref, og_ref, o_ref, st_ref, state):
        @pl.when(pl.program_id(1) == 0)
        def _():
            state[...] = jnp.zeros_like(state)

        lbp = lb_ref[...]
        lb_all = _sigmoid(lbp[0:1, :] - lbp[1:2, :])
        mask = mask_ref[...]
        for i, hh in [(i, hh) for i in range(nchunk) for hh in range(HEADS_PER_STEP)]:
            rs, cs = pl.ds(i * CHUNK, CHUNK), pl.ds(hh * DK, DK)
            t = _chunk_terms(q_ref[rs, cs], f_ref[rs, cs], lb_all[:, hh * DK:(hh + 1) * DK])
            v = v_ref[rs, cs]
            vb = v.astype(bf16)
            st = state[hh]
            st_ref[i, hh] = st
            p = _scores(t, mask)
            o = jnp.dot(p.astype(bf16), vb, preferred_element_type=f32)
            o += lax.dot_general(t["qt"].astype(bf16), st.astype(bf16), (NT, ((), ())), preferred_element_type=f32)
            state[hh] = st * t["el"] + lax.dot_general(vb, t["kt"].astype(bf16), (TN, ((), ())), preferred_element_type=f32)
            o_ref[rs, cs] = o
            r, oh = _rms(o)
            g = g_ref[rs, cs]
            og_ref[rs, cs] = (oh * gn_ref[...] * (g * _sigmoid(g))).astype(og_ref.dtype)

    hp, wide = HEADS_PER_STEP, HEADS_PER_STEP * DK
    col = lambda off: pl.BlockSpec((tb, wide), functools.partial(lambda h, t, off: (t, off + h), off=off // hp))
    blk = pl.BlockSpec((tb, wide), lambda h, t: (t, h))
    return pl.pallas_call(
        body, name="hgrn_fwd", grid=(HEADS // hp, s // tb),
        in_specs=[col(q0), col(f0), col(v0), col(g0), pl.BlockSpec((2, wide), lambda h, t: (0, h)),
                  pl.BlockSpec((1, DK), lambda h, t: (0, 0)), pl.BlockSpec(masks.shape, lambda h, t: (0, 0, 0))],
        out_specs=[blk, blk, pl.BlockSpec((nchunk, hp, DK, DK), lambda h, t: (t, h, 0, 0))],
        out_shape=[SDS((s, dv_total), bf16), SDS((s, dv_total), f32), SDS((s // CHUNK, HEADS, DK, DK), f32)],
        scratch_shapes=[pltpu.VMEM((hp, DK, DK), f32)], compiler_params=_params(("parallel", "arbitrary")),
    )(proj, proj, proj, proj, lb_param, gnorm, masks)


def _hgrn_bwd(proj, lb_param, gnorm, o_saved, states, dog, s, dv_total, tb):
    nchunk = tb // CHUNK
    nt = s // tb
    nc_total = s // CHUNK
    masks, masks_t = _level_masks()
    q0, f0, v0, g0 = 3 * HEADS, 4 * HEADS, 5 * HEADS, 6 * HEADS

    def body(q_ref, f_ref, v_ref, g_ref, lb_ref, gn_ref, mask_ref, maskt_ref, o_ref, dog_ref, st_ref, stn_ref,
             dq_ref, df_ref, dv_ref, dg_ref, dlb_ref, dgn_ref, gstate):
        h_id, t_id = pl.program_id(0), pl.program_id(1)

        @pl.when(t_id == 0)
        def _():
            gstate[...] = jnp.zeros_like(gstate)
            dlb_ref[...] = jnp.zeros_like(dlb_ref)

        @pl.when((t_id == 0) & (h_id == 0))
        def _():
            dgn_ref[...] = jnp.zeros_like(dgn_ref)

        lbp = lb_ref[...]
        lb_all = _sigmoid(lbp[0:1, :] - lbp[1:2, :])
        mask, maskt = mask_ref[...], maskt_ref[...]
        gn = gn_ref[...]
        for i, hh in [(i, hh) for i in reversed(range(nchunk)) for hh in range(HEADS_PER_STEP)]:
            rs, cs = pl.ds(i * CHUNK, CHUNK), pl.ds(hh * DK, DK)
            lb = lb_all[:, hh * DK:(hh + 1) * DK]
            qp, fl, v, g = q_ref[rs, cs], f_ref[rs, cs], v_ref[rs, cs], g_ref[rs, cs]
            t = _chunk_terms(qp, fl, lb)
            vb = v.astype(bf16)
            st0 = st_ref[i, hh]
            st1 = st_ref[i + 1, hh] if i + 1 < nchunk else stn_ref[0, hh]
            gt = gstate[hh]
            o = o_ref[rs, cs]
            r, oh = _rms(o)
            sg = _sigmoid(g)
            dog_v = dog_ref[rs, cs].astype(f32)
            dg_ref[rs, cs] = (dog_v * (oh * gn) * _dsilu(g, sg)).astype(dg_ref.dtype)
            don = dog_v * (g * sg)
            dgn_ref[...] += _colsum(don * oh)
            doh = don * gn
            do = r * (doh - oh * jnp.mean(doh * oh, axis=-1, keepdims=True))
            dob = do.astype(bf16)
            d = lax.dot_general(dob, vb, (NT, ((), ())), preferred_element_type=f32)
            dt = lax.dot_general(vb, dob, (NT, ((), ())), preferred_element_type=f32)
            z = (mask * d[None] + maskt * dt[None]).astype(bf16)
            rr = jnp.einsum("lts,lsk->ltk", z, t["u"], preferred_element_type=f32)
            dq = jnp.zeros((CHUNK, DK), f32)
            dk = jnp.zeros((CHUNK, DK), f32)
            qdk = jnp.zeros((CHUNK, DK), f32)
            for li in range(len(LEVELS)):
                du = t["ex"][li] * rr[li]
                dq += jnp.where(t["up"][li], du, 0.0)
                dk += jnp.where(t["up"][li], 0.0, du)
                e = t["u"][li].astype(f32) * rr[li]
                qdk += jnp.where(t["up"][li], e, -e)
            dd = jnp.sum(do * v, axis=-1, keepdims=True)
            dq += dd * t["k"]
            dk += dd * t["qh"]
            gtb = gt.astype(bf16)
            ktb, qtb = t["kt"].astype(bf16), t["qt"].astype(bf16)
            dq_in = jnp.dot(dob, st0.astype(bf16), preferred_element_type=f32)
            dk_in = jnp.dot(vb, gtb, preferred_element_type=f32)
            dq += t["eb"] * dq_in
            dk += t["ebl"] * dk_in
            qdk += qtb.astype(f32) * dq_in - ktb.astype(f32) * dk_in
            p = _scores(t, mask)
            dvv = lax.dot_general(p.astype(bf16), dob, (TN, ((), ())), preferred_element_type=f32)
            dvv += lax.dot_general(ktb, gtb, (NT, ((), ())), preferred_element_type=f32)
            dv_ref[rs, cs] = dvv.astype(dv_ref.dtype)
            a_end = _colsum(gtb.astype(f32) * st1)
            dlf = _rev_cumsum_rows(qdk, t["row"]) + a_end
            dfv = dlf / t["f"] - dk
            df_ref[rs, cs] = (dfv * (1.0 - lb) * t["sig"] * (1.0 - t["sig"])).astype(df_ref.dtype)
            dlb_ref[:, cs] += _colsum(dfv * (1.0 - t["sig"]))
            dq_ref[rs, cs] = (dq * _dsilu(qp, t["sq"])).astype(dq_ref.dtype)
            gstate[hh] = gt * t["el"] + lax.dot_general(dob, qtb, (TN, ((), ())), preferred_element_type=f32)

    hp, wide = HEADS_PER_STEP, HEADS_PER_STEP * DK
    rev = lambda t: nt - 1 - t
    col = lambda off: pl.BlockSpec((tb, wide), functools.partial(lambda h, t, off: (rev(t), off + h), off=off // hp))
    blk = pl.BlockSpec((tb, wide), lambda h, t: (rev(t), h))
    nxt = lambda h, t: (jnp.minimum((rev(t) + 1) * nchunk, nc_total - 1), h, 0, 0)
    return pl.pallas_call(
        body, name="hgrn_bwd", grid=(HEADS // hp, nt),
        in_specs=[col(q0), col(f0), col(v0), col(g0), pl.BlockSpec((2, wide), lambda h, t: (0, h)),
                  pl.BlockSpec((1, DK), lambda h, t: (0, 0)), pl.BlockSpec(masks.shape, lambda h, t: (0, 0, 0)),
                  pl.BlockSpec(masks.shape, lambda h, t: (0, 0, 0)), blk, blk,
                  pl.BlockSpec((nchunk, hp, DK, DK), lambda h, t: (rev(t), h, 0, 0)),
                  pl.BlockSpec((1, hp, DK, DK), nxt)],
        out_specs=[blk, blk, blk, blk, pl.BlockSpec((1, wide), lambda h, t: (0, h)), pl.BlockSpec((1, DK), lambda h, t: (0, 0))],
        out_shape=[SDS((s, dv_total), bf16)] * 4 + [SDS((1, HEADS * DK), f32), SDS((1, DK), f32)],
        scratch_shapes=[pltpu.VMEM((hp, DK, DK), f32)], compiler_params=_params(("arbitrary", "arbitrary")),
    )(proj, proj, proj, proj, lb_param, gnorm, masks, masks_t, o_saved, dog, states, states)


def _adamw(name, g, w, m, v):
    r, c = w.shape
    tr = r
    for cand in (256, 128, 64, 32, 16, 8):
        if r % cand == 0 and r > cand:
            tr = cand
            break

    def body(g_ref, w_ref, m_ref, v_ref, d_ref, mo_ref, vo_ref):
        d_ref[...], mo_ref[...], vo_ref[...] = _adamw_math(g_ref[...], w_ref[...], m_ref[...], v_ref[...])

    blk = pl.BlockSpec((tr, c), lambda i: (i, 0))
    return pl.pallas_call(
        body, name=name, grid=(r // tr,), in_specs=[blk] * 4, out_specs=[blk] * 3, out_shape=[SDS((r, c), f32)] * 3,
        compiler_params=_params(("parallel",)),
    )(g, w, m, v)


def _ffn_in(h2, w_gt, w_ut, tm, ffb):
    s, d = h2.shape
    dff = w_gt.shape[0]

    def body(a_ref, wg_ref, wu_ref, dg_ref, du_ref, act_ref):
        a = a_ref[...]
        g = lax.dot_general(a, wg_ref[...], (NT, ((), ())), preferred_element_type=f32)
        u = lax.dot_general(a, wu_ref[...], (NT, ((), ())), preferred_element_type=f32)
        sg = _sigmoid(g)
        silu = g * sg
        dg_ref[...] = (u * _dsilu(g, sg)).astype(bf16)
        du_ref[...] = silu.astype(bf16)
        act_ref[...] = (silu * u).astype(bf16)

    w_spec = pl.BlockSpec((ffb, d), lambda j, i: (j, 0))
    o_spec = pl.BlockSpec((tm, ffb), lambda j, i: (i, j))
    return pl.pallas_call(
        body, name="ffn_in", grid=(dff // ffb, s // tm), in_specs=[pl.BlockSpec((tm, d), lambda j, i: (i, 0)), w_spec, w_spec],
        out_specs=[o_spec] * 3, out_shape=[SDS((s, dff), bf16)] * 3, compiler_params=_params(("parallel", "parallel")),
    )(h2, w_gt, w_ut)


def _ffn_down_bwd(dff_out, w_d, act_dg, act_du, tm, ffb):
    s, d = dff_out.shape
    dff = w_d.shape[0]

    def body(a_ref, w_ref, fg_ref, fu_ref, dg_ref, du_ref):
        da = lax.dot_general(a_ref[...], w_ref[...], (NT, ((), ())), preferred_element_type=f32)
        dg_ref[...] = (da * fg_ref[...].astype(f32)).astype(bf16)
        du_ref[...] = (da * fu_ref[...].astype(f32)).astype(bf16)

    t_spec = pl.BlockSpec((tm, ffb), lambda j, i: (i, j))
    return pl.pallas_call(
        body, name="d_ffn_down_in", grid=(dff // ffb, s // tm),
        in_specs=[pl.BlockSpec((tm, d), lambda j, i: (i, 0)), pl.BlockSpec((ffb, d), lambda j, i: (j, 0)), t_spec, t_spec],
        out_specs=[t_spec] * 2, out_shape=[SDS((s, dff), bf16)] * 2, compiler_params=_params(("parallel", "parallel")),
    )(dff_out, w_d, act_dg, act_du)


def _branch_merge(z_a, og, w_co, w_ho, proj, tm, gate_a0, gate_b0):
    s, dc = z_a.shape
    nsh, _, n = w_co.shape

    def body(za_ref, og_ref, wa_ref, wb_ref, ga_ref, gb_ref, m_ref, sa_ref, sb_ref, fa_ref, fb_ref):
        ya = jnp.dot(za_ref[...], wa_ref[...], preferred_element_type=f32)
        yb = jnp.dot(og_ref[...], wb_ref[...], preferred_element_type=f32)
        sa, sb_ = _sigmoid(ga_ref[...]), _sigmoid(gb_ref[...])
        m_ref[...] = (sa * ya + sb_ * yb).astype(bf16)
        sa_ref[...] = sa.astype(bf16)
        sb_ref[...] = sb_.astype(bf16)
        fa_ref[...] = (ya * sa * (1.0 - sa)).astype(bf16)
        fb_ref[...] = (yb * sb_ * (1.0 - sb_)).astype(bf16)

    a_spec = pl.BlockSpec((tm, dc), lambda i, j: (i, 0))
    w_spec = pl.BlockSpec((None, dc, n), lambda i, j: (j, 0, 0))
    gate = lambda c0: pl.BlockSpec((tm, n), functools.partial(lambda i, j, cb: (i, cb + j), cb=c0 // n))
    o_spec = pl.BlockSpec((tm, n), lambda i, j: (i, j))
    return pl.pallas_call(
        body, name="branch_merge", grid=(s // tm, nsh), in_specs=[a_spec, a_spec, w_spec, w_spec, gate(gate_a0), gate(gate_b0)],
        out_specs=[o_spec] * 5, out_shape=[SDS((s, nsh * n), bf16)] * 5, compiler_params=_params(("parallel", "parallel")),
    )(z_a, og, w_co, w_ho, proj, proj)


def _d_branch_merge(dmo, w_o, factors, tm):
    s, d = dmo.shape
    n = d // 2

    def body(a_ref, w_ref, sa_ref, sb_ref, fa_ref, fb_ref, dya_ref, dyb_ref, dga_ref, dgb_ref):
        dm = lax.dot_general(a_ref[...], w_ref[...], (NT, ((), ())), preferred_element_type=f32)
        for f_ref, o_ref in ((sa_ref, dya_ref), (sb_ref, dyb_ref), (fa_ref, dga_ref), (fb_ref, dgb_ref)):
            o_ref[...] = (dm * f_ref[...].astype(f32)).astype(bf16)

    t_spec = pl.BlockSpec((tm, n), lambda j, i: (i, j))
    return pl.pallas_call(
        body, name="d_branch_merge", grid=(d // n, s // tm),
        in_specs=[pl.BlockSpec((tm, d), lambda j, i: (i, 0)), pl.BlockSpec((n, d), lambda j, i: (j, 0))] + [t_spec] * 4,
        out_specs=[t_spec] * 4, out_shape=[SDS((s, d), bf16)] * 4, compiler_params=_params(("parallel", "parallel")),
    )(dmo, w_o, *factors)


def _local_step(x, tgt, mod, g_mix, g_ffn, g_fin, lb_param, gnorm, conv_w, project, grad_in, get_w, prefetch, put_g, sent):
    s, d = x.shape
    dc = d // 2
    tm = min(512, s)
    tm2 = min(1024, s)
    te = min(256, s)
    tb = min(128, s)
    mt = s // tm
    nb = 512
    sh_m, sc_m, gt_m, sh_f, sc_f, gt_f = [mod[i] for i in range(N_MOD)]
    dh2 = d // 2

    def e1(rows, fulls, outs, accs):
        xv = rows[0][...]
        g, sc, sh = [r[...] for r in fulls]
        _, xh = _rms(xv)
        outs[0][...] = (xh * g * (1.0 + sc) + sh).astype(bf16)

    h, = _rowwise("prenorm_mix", e1, s, te, [(x, d, 0)], [g_mix, sc_m, sh_m], [(d, bf16)])
    proj, w_in = project(h, tm2)
    nsh, _, win_sh = w_in.shape
    z_a = _conv_fwd(proj, _after_tokens(conv_w, [prefetch("mix", proj)]), s, dc)
    og, o_saved, states = _hgrn_fwd(proj, lb_param, gnorm, s, dc, tb)
    w_co, w_ho, w_o = get_w("mix", og)

    gate_a0, gate_b0 = 7 * dh2, 9 * dh2
    merged, *merge_factors = _branch_merge(z_a, og, w_co, w_ho, proj, tm2, gate_a0, gate_b0)
    mo = _matmul("mix_out", merged, w_o, (s, d), bf16, (2, s // tm2, 1), pl.BlockSpec((tm2, d), lambda j, i, k: (i, 0)),
                 pl.BlockSpec((d, dh2), lambda j, i, k: (0, j)), pl.BlockSpec((tm2, dh2), lambda j, i, k: (i, j)), NN,
                 after=prefetch("ffn", merged))

    def e6(rows, fulls, outs, accs):
        xv, mov = rows[0][...], rows[1][...].astype(f32)
        gt, g, sc, sh = [r[...] for r in fulls]
        x1 = xv + gt * mov
        outs[0][...] = x1
        _, xh = _rms(x1)
        outs[1][...] = (xh * g * (1.0 + sc) + sh).astype(bf16)

    x1, h2 = _rowwise("prenorm_ffn", e6, s, tm, [(x, d, 0), (mo, d, 0)], [gt_m, g_ffn, sc_f, sh_f], [(d, f32), (d, bf16)])
    w_gt, w_ut = get_w("ffn", h2)
    dff_ = w_gt.shape[0]
    act_dg, act_du, act = _ffn_in(h2, w_gt, w_ut, tm2, nb)
    prefetch("ffn_down", act)
    w_d, = get_w("ffn_down", act)
    ff = _matmul("ffn_down", act, w_d, (s, d), bf16, (2, mt, 1), pl.BlockSpec((tm, dff_), lambda j, i, k: (i, 0)),
                 pl.BlockSpec((dff_, dh2), lambda j, i, k: (0, j)), pl.BlockSpec((tm, dh2), lambda j, i, k: (i, j)), NN)

    def e9(rows, fulls, outs, accs):
        x1v, ffv, tv = [r[...].astype(f32) for r in rows]
        gt, gf = fulls[0][...], fulls[1][...]
        x2 = x1v + gt * ffv
        r, xh = _rms(x2)
        err = xh * gf - tv
        accs[0][...] += 0.5 * jnp.sum(jnp.mean(err * err, axis=-1, keepdims=True), axis=0, keepdims=True)
        dy = err / d
        accs[1][...] += _colsum(dy * xh)
        dxh = dy * gf
        dx2 = r * (dxh - xh * jnp.mean(dxh * xh, axis=-1, keepdims=True))
        outs[0][...] = dx2
        outs[1][...] = (dx2 * gt).astype(bf16)
        accs[2][...] += _colsum(dx2 * ffv)

    dx2, dff, loss_acc, dg_fin, dgt_f = _rowwise("loss_head", e9, s, tm, [(x1, d, 0), (ff, d, 0), (tgt, d, 0)], [gt_f, g_fin],
                                                 [(d, f32), (d, bf16)], [(1, 128), (1, d), (1, d)])
    dgg, duu = _ffn_down_bwd(dff, w_d, act_dg, act_du, tm, dff_ // 4)

    def wgrad_rows(name, a, b, n_out):
        kb = 512
        return _matmul(name, a, b, (n_out, d), bf16, (n_out // kb, 2, 1), pl.BlockSpec((s, kb), lambda i, j, k: (0, i)),
                       pl.BlockSpec((s, dh2), lambda i, j, k: (0, j)), pl.BlockSpec((kb, dh2), lambda i, j, k: (i, j)), TN)

    gw_d = wgrad_rows("gw_ffn_down", act, dff, dff_)

    def ffn_in_bwd(name, a, w, after=None):
        return _matmul(name, a, w, (s, d), bf16, (2, mt, 1), pl.BlockSpec((tm, dff_), lambda j, i, k: (i, 0)),
                       pl.BlockSpec((dff_, dh2), lambda j, i, k: (0, j)), pl.BlockSpec((tm, dh2), lambda j, i, k: (i, j)), NN,
                       after=after)

    dh2b = ffn_in_bwd("d_ffn_up_in", duu, w_ut)
    gw_ut = wgrad_rows("gw_ffn_up", duu, h2, dff_)
    gw_gt = wgrad_rows("gw_ffn_gate", dgg, h2, dff_)
    dh2a = ffn_in_bwd("d_ffn_gate_in", dgg, w_gt, after=put_g("ffn", [gw_gt, gw_ut, gw_d]))
    sc_f_late = _after_tokens(sc_f, [sent("ffn", dh2a)])

    def b5(rows, fulls, outs, accs):
        da, db, x1v, dx2v, mov = [r[...].astype(f32) for r in rows]
        sc, g, gt = [r[...] for r in fulls]
        dh = da + db
        r, xh = _rms(x1v)
        accs[0][...] += _colsum(dh)
        accs[1][...] += _colsum(dh * (xh * g))
        dn = dh * (1.0 + sc)
        accs[2][...] += _colsum(dn * xh)
        dxh = dn * g
        dx1 = dx2v + r * (dxh - xh * jnp.mean(dxh * xh, axis=-1, keepdims=True))
        outs[0][...] = dx1
        accs[3][...] += _colsum(dx1 * mov)
        outs[1][...] = (dx1 * gt).astype(bf16)

    dx1, dmo, dsh_f, dsc_f, dg_ffn, dgt_m = _rowwise(
        "d_prenorm_ffn", b5, s, te, [(dh2a, d, 0), (dh2b, d, 0), (x1, d, 0), (dx2, d, 0), (mo, d, 0)], [sc_f_late, g_ffn, gt_m],
        [(d, f32), (d, bf16)], [(1, d)] * 4)
    dya, dyb, dga, dgb = _d_branch_merge(dmo, w_o, merge_factors, tm)
    gw_o = wgrad_rows("gw_mix_out", merged, dmo, d)

    def out_proj_bwd(name, dy, w):
        return _matmul(name, dy, w, (s, dc), bf16, (1, s // tm2, nsh), pl.BlockSpec((tm2, d // nsh), lambda j, i, k: (i, k)),
                       pl.BlockSpec((None, dc, d // nsh), lambda j, i, k: (k, 0, 0)), pl.BlockSpec((tm2, dc), lambda j, i, k: (i, 0)), NT)

    def out_proj_wgrad(name, a, dy):
        return _matmul(name, a, dy, (nsh, dc, d // nsh), bf16, (1, nsh, 1), pl.BlockSpec((s, dc), lambda i, j, k: (0, 0)),
                       pl.BlockSpec((s, d // nsh), lambda i, j, k: (0, j)), pl.BlockSpec((None, dc, d // nsh), lambda i, j, k: (j, 0, 0)), TN)

    dz_a = out_proj_bwd("d_conv_out_in", dya, w_co)
    dog = out_proj_bwd("d_hgrn_out_in", dyb, w_ho)
    gw_co = out_proj_wgrad("gw_conv_out", z_a, dya)
    gw_ho = out_proj_wgrad("gw_hgrn_out", og, dyb)
    conv_w_late = _after_tokens(conv_w, [put_g("mix", [gw_co, gw_ho, gw_o])])
    dab, dac, dax, dconv_w = _conv_bwd(proj, conv_w_late, dz_a, s, dc)
    lb_param_late = _after_tokens(lb_param, [sent("mix", dab)])
    dq, dfl, dvi, dgo, dlb, dgn = _hgrn_bwd(proj, lb_param_late, gnorm, o_saved, states, dog, s, dc, tb)
    dproj = _concat_columns("d_proj", [dab, dac, dax, dq, dfl, dvi, dgo, dga, dgb], te)
    dh = _d_proj_in(dproj, w_in, tm, grad_in(h, dproj, win_sh))

    def b12(rows, fulls, outs, accs):
        dhv, xv, dx1v = [r[...].astype(f32) for r in rows]
        sc, g = fulls[0][...], fulls[1][...]
        r, xh = _rms(xv)
        accs[0][...] += _colsum(dhv)
        accs[1][...] += _colsum(dhv * (xh * g))
        dn = dhv * (1.0 + sc)
        accs[2][...] += _colsum(dn * xh)
        dxh = dn * g
        outs[0][...] = dx1v + r * (dxh - xh * jnp.mean(dxh * xh, axis=-1, keepdims=True))

    dx, dsh_m, dsc_m, dg_mix = _rowwise("d_prenorm_mix", b12, s, tm, [(dh, d, 0), (x, d, 0), (dx1, d, 0)], [sc_m, g_mix],
                                        [(d, f32)], [(1, d)] * 3)
    dmod = [dsh_m, dsc_m, dgt_m, dsh_f, dsc_f, dgt_f]
    small = dict(loss=loss_acc, g_mix=dg_mix, g_ffn=dg_ffn, g_fin=dg_fin, lb=dlb, gnorm=dgn, conv_w=dconv_w)
    return dx, dmod, small


def _ada_fwd(c_all, w_sh, b_sh):
    def body(c_ref, w_ref, b_ref, o_ref):
        cv = c_ref[...]
        ca = (cv * _sigmoid(cv)).astype(bf16)
        o_ref[...] = jnp.dot(ca, w_ref[...].astype(bf16), preferred_element_type=f32) + b_ref[...]

    return pl.pallas_call(body, name="ada_fwd", out_shape=SDS((c_all.shape[0], w_sh.shape[1]), f32),
                          compiler_params=pltpu.CompilerParams(vmem_limit_bytes=V7X_VMEM_LIMIT))(c_all, w_sh, b_sh)


def _ada_wgrad(c_all, dmod_sh):
    def body(c_ref, d_ref, o_ref):
        cv = c_ref[...]
        ca = (cv * _sigmoid(cv)).astype(bf16)
        o_ref[...] = lax.dot_general(ca, d_ref[...].astype(bf16), (TN, ((), ())), preferred_element_type=f32)

    return pl.pallas_call(body, name="ada_wgrad", out_shape=SDS((c_all.shape[1], dmod_sh.shape[1]), f32),
                          compiler_params=pltpu.CompilerParams(vmem_limit_bytes=V7X_VMEM_LIMIT))(c_all, dmod_sh)


def _lb_grad(lb_param, dlb):
    def body(p_ref, d_ref, o_ref):
        p = p_ref[...]
        lb = _sigmoid(p[0:1, :] - p[1:2, :])
        gl = d_ref[...] * lb * (1.0 - lb)
        o_ref[0:1, :] = gl
        o_ref[1:2, :] = -gl

    return pl.pallas_call(body, name="lb_grad", out_shape=SDS(lb_param.shape, f32))(lb_param, dlb)


def _sum_small(gathered):
    def body(g_ref, o_ref):
        acc = g_ref[0]
        for dd in range(1, NDEV):
            acc = acc + g_ref[dd]
        o_ref[...] = acc

    return pl.pallas_call(body, name="sum_small", out_shape=SDS(gathered.shape[1:], f32))(gathered)


def kernel(x, c, w_ada, b_ada, norm_mix_g, w_in, conv_w, lb_param, gnorm_g, w_conv_out, w_hgrn_out, w_o, norm_ffn_g, w_ffn_gate, w_ffn_up, w_ffn_down, norm_final_g, loss_target, m_w_ada, m_b_ada, m_norm_mix_g, m_w_in, m_conv_w, m_lb_param, m_gnorm_g, m_w_conv_out, m_w_hgrn_out, m_w_o, m_norm_ffn_g, m_w_ffn_gate, m_w_ffn_up, m_w_ffn_down, m_norm_final_g, v_w_ada, v_b_ada, v_norm_mix_g, v_w_in, v_conv_w, v_lb_param, v_gnorm_g, v_w_conv_out, v_w_hgrn_out, v_w_o, v_norm_ffn_g, v_w_ffn_gate, v_w_ffn_up, v_w_ffn_down, v_norm_final_g):
    assert lb_param.shape[0] == 2 and w_ada.shape[0] == 1
    s, d = x.shape[1], x.shape[2]
    me = 4 * lax.axis_index("x") + 2 * lax.axis_index("y") + lax.axis_index("c")
    ada_cols = w_ada.shape[2]

    me1 = me.astype(jnp.int32).reshape(1)
    placed_in = [_place_shard("place_in0", me1, w_in[0])[0]]
    c_all, cw_all = _all_gather("gather_cond", [c, conv_w[0]], after=placed_in)
    c_all = c_all.reshape(NDEV, d)
    conv_w_full = jnp.transpose(cw_all, (1, 0, 2)).reshape(conv_w.shape[1], -1)
    b_sh = lax.dynamic_slice_in_dim(b_ada, me * ada_cols, ada_cols, axis=1)
    mod_cols = _ada_fwd(c_all, w_ada[0], b_sh)
    mod_all, = _all_gather("gather_mod", [mod_cols])
    mod = lax.dynamic_index_in_dim(mod_all, me, axis=1, keepdims=False).reshape(N_MOD, 1, d)

    shard_groups = {"mix": [w_conv_out[0], w_hgrn_out[0], w_o[0]], "ffn": [w_ffn_gate[0].T, w_ffn_up[0].T],
                    "ffn_down": [w_ffn_down[0]]}
    own_slot = lambda frm, to: _flat(frm)
    gather_plan = lambda n: [(a, j, own_slot, own_slot) for a in range(n) for j in (1,) + ICI_RELATIONS]
    flat = lambda a: a.reshape(a.shape[0] * a.shape[1], a.shape[2])
    to8 = lambda a: a.reshape(NDEV, a.shape[0] // NDEV, a.shape[1])
    near, far = ICI_RELATIONS[:2], ICI_RELATIONS[2:]
    in_plan = lambda rels: [(0, j, own_slot, own_slot) for j in rels]
    ss_a, rs_a, _, lands, tok_a = _push_start("gather_start_in_sib", [], placed_in, in_plan((1,)), after=mod_all)
    ss_b, rs_b, _, lands_in, tok_b = _push_start("gather_start_in_near", [], lands, in_plan(near), after=tok_a)
    gathering, tokens, placed = {}, [tok_a, tok_b], {}
    for grp, sh in shard_groups.items():
        placed[grp] = []
        for i, a in enumerate(sh):
            buf, tok = _place_shard(f"place_{grp}{i}", me1, a, after=tokens[-1])
            placed[grp].append(buf)
            tokens.append(tok)

    pos = (lax.axis_index("x"), lax.axis_index("y"), lax.axis_index("c"))
    ids = lambda frm, rels: jnp.stack([_flat(_peer(frm, j)) for j in rels]).astype(jnp.int32)

    def project(h, tm):
        sib = _peer(pos, 1)
        fwd_of = lambda idxs: [(0, 1, fwd_slot(i), fwd_slot(i)) for i in idxs]
        _, lands = _push_wait("gather_wait_in_sib", ss_a, rs_a, [], lands_in, in_plan((1,)), h)
        proj = _proj_part("proj_local", h, lands[0], ids(pos, (0, 1)), None, tm)
        _, lands = _push_wait("gather_wait_in_near", ss_b, rs_b, [], lands, in_plan(near), proj)
        ss_c, rs_c, _, lands, tok = _push_start("gather_start_in_far", [], lands, in_plan(far))
        for grp in shard_groups:
            ss, rs, _, bufs, tok = _push_start("gather_start_" + grp, [], placed[grp], gather_plan(len(placed[grp])), after=tok)
            gathering[grp] = (ss, rs, bufs)
        ss1, rs1, _, lands, tok = _push_start("gather_fwd_in_near", [], lands, fwd_of((0, 1)), after=tok)
        proj = _proj_part("proj_near", h, lands[0], ids(pos, near), proj, tm, after=tok)
        _, lands = _push_wait("gather_fwd_wait_in_near", ss1, rs1, [], lands, fwd_of((0, 1)), proj)
        proj = _proj_part("proj_fwd_near", h, lands[0], ids(sib, near), proj, tm)
        _, lands = _push_wait("gather_wait_in_far", ss_c, rs_c, [], lands, in_plan(far), proj)
        ss2, rs2, _, lands, tok = _push_start("gather_fwd_in_far", [], lands, fwd_of((2,)))
        proj = _proj_part("proj_far", h, lands[0], ids(pos, far), proj, tm, after=tok)
        _, lands = _push_wait("gather_fwd_wait_in_far", ss2, rs2, [], lands, fwd_of((2,)), proj)
        proj = _proj_part("proj_fwd_far", h, lands[0], ids(sib, far), proj, tm)
        return proj, lands[0]

    fwd_slot = lambda i: (lambda frm, to: _flat(_peer(frm, ICI_RELATIONS[i])))
    fwd_plan = lambda n: [(a, 1, fwd_slot(i), fwd_slot(i)) for a in range(n) for i in range(len(ICI_RELATIONS))]
    forwarding = {}

    def prefetch(grp, after):
        ss, rs, lands = gathering[grp]
        _, lands = _push_wait("gather_wait_" + grp, ss, rs, [], lands, gather_plan(len(lands)), after)
        ss, rs, _, lands, tok = _push_start("gather_fwd_" + grp, [], lands, fwd_plan(len(lands)))
        forwarding[grp] = (ss, rs, lands)
        return tok

    def get_w(grp, after):
        ss, rs, lands = forwarding[grp]
        _, full = _push_wait("gather_fwd_wait_" + grp, ss, rs, [], lands, fwd_plan(len(lands)), after)
        return [f if i < 2 and grp == "mix" else flat(f) for i, f in enumerate(full)]

    core = lax.axis_index("c").astype(jnp.int32).reshape(1)
    chip = (2 * lax.axis_index("x") + lax.axis_index("y")).astype(jnp.int32).reshape(1)
    scatter_plan = lambda n: [(a, j, lambda frm, to: _chip(to), lambda frm, to: _chip(frm)) for a in range(n) for j in ICI_RELATIONS]
    scattering = {}

    swap_plan = lambda n: [(a, 1, functools.partial(lambda frm, to, q: 2 * q + to[2], q=q), functools.partial(lambda frm, to, q: q, q=q))
                           for a in range(n) for q in range(NDEV // 2)]
    swapping = {}

    def start_ici(grp, g8, recv):
        pairs = [_pair_sum(f"pair_sum_{grp}{i}", core, g, r, _row_tile(g.shape[1], 1024)) for i, (g, r) in enumerate(zip(g8, recv))]
        lands = [lax.empty(p.shape, p.dtype) for p in pairs]
        ss, rs, srcs, lands, tok = _push_start("scatter_start_" + grp, pairs, lands, scatter_plan(len(pairs)))
        scattering[grp] = (ss, rs, srcs, lands)
        return tok

    def grad_in(h, dproj, n):
        halves = [jnp.stack([2 * q + cc for q in range(NDEV // 2)]).astype(jnp.int32) for cc in (1 - pos[2], pos[2])]
        gw = _gw_part("gw_proj_sibling", h, dproj, n, halves[0], None)
        lands = [lax.empty((NDEV // 2,) + gw.shape[1:], gw.dtype)]
        ss, rs, srcs, lands, tok = _push_start("scatter_swap_in", [gw], lands, swap_plan(1))
        gw = _gw_part("gw_proj_own", h, dproj, n, halves[1], srcs[0], after=tok)
        g8, recv = _push_wait("scatter_swapped_in", ss, rs, [gw], lands, swap_plan(1), tok)
        return start_ici("in", g8, recv)

    def put_g(grp, grads):
        g8 = [g if g.ndim == 3 else to8(g) for g in grads]
        lands = [lax.empty((NDEV // 2,) + g.shape[1:], g.dtype) for g in g8]
        ss, rs, srcs, lands, tok = _push_start("scatter_swap_" + grp, g8, lands, swap_plan(len(g8)))
        swapping[grp] = (ss, rs, srcs, lands)
        return tok

    def sent(grp, after):
        ss, rs, srcs, lands = swapping[grp]
        g8, recv = _push_wait("scatter_swapped_" + grp, ss, rs, srcs, lands, swap_plan(len(srcs)), after)
        return start_ici(grp, g8, recv)

    def reduced(grp, after, names):
        ss, rs, srcs, lands = scattering[grp]
        srcs, lands = _push_wait("scatter_wait_" + grp, ss, rs, srcs, lands, scatter_plan(len(srcs)), after)
        for i, (p, r, nm) in enumerate(zip(srcs, lands, names)):
            tr = nm in ("w_ffn_gate", "w_ffn_up")
            wmv = tuple(a[0].T if tr else a[0] for a in weights[nm])
            out = _chip_sum_adamw(f"chip_sum_{grp}{i}", chip, p, r, _row_tile(p.shape[1]), wmv)
            res[nm] = [(a.T if tr else a).reshape(weights[nm][0].shape) for a in out]

    dx, dmod, small = _local_step(x[0], loss_target[0], mod, _after_tokens(norm_mix_g, tokens), norm_ffn_g,
                                  norm_final_g.reshape(1, d), lb_param, gnorm_g, conv_w_full, project, grad_in, get_w, prefetch, put_g, sent)

    pieces = [*dmod, small["g_mix"], small["g_ffn"], small["g_fin"], small["lb"], small["gnorm"], small["loss"],
              small["conv_w"].reshape(1, -1)]
    widths = [p.shape[1] for p in pieces]
    offs = np.concatenate([[0], np.cumsum(widths)])
    packed = jnp.concatenate(pieces, axis=1)
    gathered, = _all_gather("gather_small", [packed])
    summed = _sum_small(gathered)
    part = lambda i: summed[:, offs[i]:offs[i + 1]]
    g_b_ada = summed[:, :N_MOD * d]
    g_norm_mix, g_norm_ffn, g_norm_fin, g_lb_row, g_gnorm, loss_vec, g_convw_flat = [part(i) for i in range(N_MOD, N_MOD + 7)]
    loss = loss_vec[0, 0]
    dmod_all = gathered[:, 0, :N_MOD * d]
    g_w_ada = _ada_wgrad(c_all, lax.dynamic_slice_in_dim(dmod_all, me * ada_cols, ada_cols, axis=1))
    g_lb = _lb_grad(lb_param, g_lb_row)
    cw_cols = conv_w.shape[2]
    g_conv_w = lax.dynamic_slice_in_dim(g_convw_flat.reshape(conv_w.shape[1], -1), me * cw_cols, cw_cols, axis=1)

    grads = dict(w_ada=g_w_ada, b_ada=g_b_ada, norm_mix_g=g_norm_mix, conv_w=g_conv_w, lb_param=g_lb, gnorm_g=g_gnorm,
                 norm_ffn_g=g_norm_ffn, norm_final_g=g_norm_fin)
    weights = dict(w_ada=(w_ada, m_w_ada, v_w_ada), b_ada=(b_ada, m_b_ada, v_b_ada), norm_mix_g=(norm_mix_g, m_norm_mix_g, v_norm_mix_g),
                   w_in=(w_in, m_w_in, v_w_in), conv_w=(conv_w, m_conv_w, v_conv_w), lb_param=(lb_param, m_lb_param, v_lb_param),
                   gnorm_g=(gnorm_g, m_gnorm_g, v_gnorm_g), w_conv_out=(w_conv_out, m_w_conv_out, v_w_conv_out),
                   w_hgrn_out=(w_hgrn_out, m_w_hgrn_out, v_w_hgrn_out), w_o=(w_o, m_w_o, v_w_o),
                   norm_ffn_g=(norm_ffn_g, m_norm_ffn_g, v_norm_ffn_g), w_ffn_gate=(w_ffn_gate, m_w_ffn_gate, v_w_ffn_gate),
                   w_ffn_up=(w_ffn_up, m_w_ffn_up, v_w_ffn_up), w_ffn_down=(w_ffn_down, m_w_ffn_down, v_w_ffn_down),
                   norm_final_g=(norm_final_g, m_norm_final_g, v_norm_final_g))
    res = {}

    def update(nm):
        w, m, v = weights[nm]
        shape2 = (w.shape[-2], w.shape[-1]) if w.ndim >= 2 else (1, w.shape[0])
        g2 = grads[nm].reshape(shape2)
        dl, mn, vn = _adamw("adamw_" + nm, g2, w.reshape(shape2), m.reshape(shape2), v.reshape(shape2))
        res[nm] = [a.reshape(w.shape) for a in (g2, dl, mn, vn)]

    for nm in list(grads):
        update(nm)
    reduced("ffn", res["w_ada"][1], ("w_ffn_gate", "w_ffn_up", "w_ffn_down"))
    reduced("mix", res["w_ffn_down"][1], ("w_conv_out", "w_hgrn_out", "w_o"))
    reduced("in", res["w_o"][1], ("w_in",))
    outs = [[res[nm][i] for nm in weights] for i in range(4)]
    return (loss, dx.reshape(x.shape), *outs[0], *outs[1], *outs[2], *outs[3])
```

```python
import functools

import numpy as np
import jax
import jax.numpy as jnp
from jax import lax
from jax.experimental import pallas as pl
from jax.experimental.pallas import tpu as pltpu

f32, bf16 = jnp.float32, jnp.bfloat16
SDS = jax.ShapeDtypeStruct

EPS = 1e-6
HEADS, DK, CHUNK = 8, 128, 64
HEADS_PER_STEP = 8
N_MOD = 6
NDEV = 8
ADAM_LR, ADAM_B1, ADAM_B2, ADAM_EPS, ADAM_WD, ADAM_STEP = 0.001, 0.9, 0.999, 1e-08, 0.01, 10
LEVELS = (32, 16, 8, 4, 2, 1)
V7X_VMEM_LIMIT = 56 * 1024 * 1024
HBM = pl.BlockSpec(memory_space=pltpu.HBM)
MESH = pl.DeviceIdType.MESH


def _params(sem):
    return pltpu.CompilerParams(dimension_semantics=sem, vmem_limit_bytes=V7X_VMEM_LIMIT)


def _sigmoid(x):
    return jax.nn.sigmoid(x)


def _dsilu(x, s):
    return s * (1.0 + x * (1.0 - s))


def _mesh_pos():
    x, y, c = lax.axis_index("x"), lax.axis_index("y"), lax.axis_index("c")
    return x, y, c


def _peer(pos, j):
    x, y, c = pos
    return (1 - x if j & 4 else x, 1 - y if j & 2 else y, 1 - c if j & 1 else c)


def _flat(pos):
    return 4 * pos[0] + 2 * pos[1] + pos[2]


def _all_gather(name, arrs, after=()):
    n, ne = len(arrs), len(after)
    out_shapes = [SDS((NDEV,) + a.shape, a.dtype) for a in arrs]

    def body(*refs):
        ins, outs = refs[:n], refs[n + ne:2 * n + ne]
        send_sems, recv_sems, local_sems = refs[2 * n + ne:]
        pos = _mesh_pos()
        me = _flat(pos)

        def copy(a, j, frm, to_pos):
            k = a * (NDEV - 1) + j - 1
            return pltpu.make_async_remote_copy(src_ref=ins[a], dst_ref=outs[a].at[frm], send_sem=send_sems.at[k],
                                                recv_sem=recv_sems.at[k], device_id=to_pos, device_id_type=MESH)

        local = [pltpu.make_async_copy(ins[a], outs[a].at[me], local_sems.at[a]) for a in range(n)]
        for cp in local:
            cp.start()
        sends = [copy(a, j, me, _peer(pos, j)) for j in range(1, NDEV) for a in range(n)]
        for cp in sends:
            cp.start()
        for j in range(1, NDEV):
            for a in range(n):
                copy(a, j, _flat(_peer(pos, j)), pos).wait_recv()
        for cp in sends:
            cp.wait_send()
        for cp in local:
            cp.wait()

    return pl.pallas_call(
        body, name=name, out_shape=out_shapes, in_specs=[HBM] * n + [ANY] * ne, out_specs=[HBM] * n,
        scratch_shapes=[pltpu.SemaphoreType.DMA((n * (NDEV - 1),)), pltpu.SemaphoreType.DMA((n * (NDEV - 1),)),
                        pltpu.SemaphoreType.DMA((n,))],
    )(*arrs, *after)


SEM = pl.BlockSpec(memory_space=pltpu.SEMAPHORE)
ANY = pl.BlockSpec(memory_space=pl.ANY)
EFFECT = pltpu.SideEffectType.DATAFLOW_SIDE_EFFECTING
ICI_RELATIONS = (2, 4, 6)


def _chip(pos):
    return 2 * pos[0] + pos[1]


def _hbm(a):
    return pltpu.with_memory_space_constraint(a, pltpu.HBM)


def _plan_copy(plan_entry, k, pos, frm, to, src_refs, land_refs, send_sems, recv_sems):
    a, _, src_slot, dst_slot = plan_entry
    s = src_refs[a] if src_slot is None else src_refs[a].at[src_slot(frm, to)]
    return pltpu.make_async_remote_copy(src_ref=s, dst_ref=land_refs[a].at[dst_slot(frm, to)], send_sem=send_sems.at[k],
                                        recv_sem=recv_sems.at[k], device_id=to, device_id_type=MESH)


def _push_start(name, srcs, lands, plan, after=None):
    ns, nb, nk = len(srcs), len(srcs) + len(lands), len(plan)
    extra = [] if after is None else [after]

    def body(*refs):
        land_refs = refs[ns:nb]
        src_refs = refs[:ns] if ns else land_refs
        send_sems, recv_sems = refs[nb + len(extra)], refs[nb + len(extra) + 1]
        pos = _mesh_pos()
        for k, e in enumerate(plan):
            _plan_copy(e, k, pos, pos, _peer(pos, e[1]), src_refs, land_refs, send_sems, recv_sems).start()
        refs[-1][...] = jnp.zeros_like(refs[-1])

    outs = pl.pallas_call(
        body, name=name,
        out_shape=[pltpu.SemaphoreType.DMA((nk,)), pltpu.SemaphoreType.DMA((nk,))] + [pltpu.HBM(a.shape, a.dtype) for a in srcs + lands]
        + [SDS((8, 128), f32)],
        in_specs=[HBM] * nb + [ANY] * len(extra), out_specs=[SEM, SEM] + [HBM] * nb + [pl.BlockSpec(memory_space=pltpu.VMEM)],
        input_output_aliases={i: 2 + i for i in range(nb)},
        compiler_params=pltpu.CompilerParams(has_side_effects=EFFECT),
    )(*[_hbm(a) for a in srcs + lands], *extra)
    return outs[0], outs[1], list(outs[2:2 + ns]), list(outs[2 + ns:2 + nb]), outs[-1]


def _push_wait(name, send_sems, recv_sems, srcs, lands, plan, after):
    ns, nb = len(srcs), len(srcs) + len(lands)

    def body(*refs):
        land_refs = refs[ns:nb]
        src_refs = refs[:ns] if ns else land_refs
        ssem, rsem = refs[nb], refs[nb + 1]
        pos = _mesh_pos()
        for k, e in enumerate(plan):
            peer = _peer(pos, e[1])
            _plan_copy(e, k, pos, pos, peer, src_refs, land_refs, ssem, rsem).wait_send()
            _plan_copy(e, k, pos, peer, pos, src_refs, land_refs, ssem, rsem).wait_recv()

    outs = pl.pallas_call(
        body, name=name, out_shape=[pltpu.HBM(a.shape, a.dtype) for a in srcs + lands],
        in_specs=[HBM] * nb + [SEM, SEM, ANY], out_specs=[HBM] * nb,
        input_output_aliases={i: i for i in range(nb)},
        compiler_params=pltpu.CompilerParams(has_side_effects=EFFECT),
    )(*srcs, *lands, send_sems, recv_sems, after)
    return list(outs[:ns]), list(outs[ns:])


def _after_tokens(small, tokens):
    for t in tokens:
        if t is not None:
            small = small + t[0:1, 0:1].reshape((1,) * small.ndim)
    return small


def _place_shard(name, me, shard, after=None):
    r, c = shard.shape
    tr = _row_tile(r)
    extra = [] if after is None else [after]

    def body(me_ref, s_ref, *rest):
        rest[-2][...] = s_ref[...].astype(bf16)
        rest[-1][...] = jnp.zeros_like(rest[-1])

    return pl.pallas_call(
        body, name=name, out_shape=[SDS((NDEV, r, c), bf16), SDS((8, 128), f32)],
        grid_spec=pltpu.PrefetchScalarGridSpec(
            num_scalar_prefetch=1, grid=(r // tr,), in_specs=[pl.BlockSpec((tr, c), lambda i, me_ref: (i, 0))] + [ANY] * len(extra),
            out_specs=[pl.BlockSpec((None, tr, c), lambda i, me_ref: (me_ref[0], i, 0)),
                       pl.BlockSpec((8, 128), lambda i, me_ref: (0, 0))]),
        compiler_params=_params(("arbitrary",)),
    )(me, shard, *extra)


def _row_tile(r, most=256):
    return max(t for t in range(16, most + 1, 16) if r % t == 0)


def _pair_sum(name, core, grad, recv, tr):
    _, r, c = grad.shape
    nchip = NDEV // 2

    def body(core_ref, g_ref, r_ref, o_ref):
        o_ref[...] = (g_ref[...].astype(f32) + r_ref[...].astype(f32)).astype(o_ref.dtype)

    return pl.pallas_call(
        body, name=name, out_shape=SDS((nchip, r, c), grad.dtype),
        grid_spec=pltpu.PrefetchScalarGridSpec(
            num_scalar_prefetch=1, grid=(nchip, r // tr),
            in_specs=[pl.BlockSpec((None, tr, c), lambda q, i, core_ref: (2 * q + core_ref[0], i, 0)),
                      pl.BlockSpec((None, tr, c), lambda q, i, core_ref: (q, i, 0))],
            out_specs=pl.BlockSpec((None, tr, c), lambda q, i, core_ref: (q, i, 0))),
        compiler_params=_params(("parallel", "parallel")),
    )(core, grad, recv)


def _adamw_math(g, w, m, v):
    mn = ADAM_B1 * m + (1.0 - ADAM_B1) * g
    vn = ADAM_B2 * v + (1.0 - ADAM_B2) * jnp.square(g)
    m_hat = mn / (1.0 - ADAM_B1 ** ADAM_STEP)
    v_hat = vn / (1.0 - ADAM_B2 ** ADAM_STEP)
    return -ADAM_LR * (m_hat / (jnp.sqrt(v_hat) + ADAM_EPS) + ADAM_WD * w), mn, vn


def _chip_sum_adamw(name, chip, pairs, recv, tr, wmv):
    nchip, r, c = pairs.shape

    def body(chip_ref, p_ref, r_ref, w_ref, m_ref, v_ref, g_ref, d_ref, mo_ref, vo_ref):
        mine = chip_ref[0]
        own = p_ref[...].astype(f32)
        acc = jnp.zeros((tr, c), f32)
        for q in range(nchip):
            acc = acc + jnp.where(mine == q, own, r_ref[q].astype(f32))
        g_ref[...] = acc
        d_ref[...], mo_ref[...], vo_ref[...] = _adamw_math(acc, w_ref[...], m_ref[...], v_ref[...])

    blk = pl.BlockSpec((tr, c), lambda i, chip_ref: (i, 0))
    return pl.pallas_call(
        body, name=name, out_shape=[SDS((r, c), f32)] * 4,
        grid_spec=pltpu.PrefetchScalarGridSpec(
            num_scalar_prefetch=1, grid=(r // tr,),
            in_specs=[pl.BlockSpec((None, tr, c), lambda i, chip_ref: (chip_ref[0], i, 0)),
                      pl.BlockSpec((nchip, tr, c), lambda i, chip_ref: (0, i, 0))] + [blk] * 3,
            out_specs=[blk] * 4),
        compiler_params=_params(("parallel",)),
    )(chip, pairs, recv, *wmv)


def _matmul(name, a, b, out_shape, out_dtype, grid, a_spec, b_spec, o_spec, dims, after=None):
    ksteps = grid[2]
    acc_shape = tuple(d for d in o_spec.block_shape if d is not None)
    extra = [] if after is None else [after]

    def body(a_ref, b_ref, *rest):
        o_ref, acc = rest[len(extra)], rest[len(extra) + 1:]
        prod = lax.dot_general(a_ref[...], b_ref[...], (dims, ((), ())), preferred_element_type=f32)
        if ksteps == 1:
            o_ref[...] = prod.astype(o_ref.dtype)
        else:
            k = pl.program_id(2)

            @pl.when(k == 0)
            def _():
                acc[0][...] = prod

            @pl.when(k > 0)
            def _():
                acc[0][...] += prod

            @pl.when(k == ksteps - 1)
            def _():
                o_ref[...] = acc[0][...].astype(o_ref.dtype)

    return pl.pallas_call(
        body, name=name, grid=grid, in_specs=[a_spec, b_spec] + [ANY] * len(extra), out_specs=o_spec,
        out_shape=SDS(out_shape, out_dtype), scratch_shapes=[] if ksteps == 1 else [pltpu.VMEM(acc_shape, f32)],
        compiler_params=_params(("parallel", "parallel", "arbitrary")),
    )(a, b, *extra)


NN, NT, TN = ((1,), (0,)), ((1,), (1,)), ((0,), (0,))


def _concat_columns(name, pieces, tm):
    s = pieces[0].shape[0]
    widths = [p.shape[1] for p in pieces]
    offs = [sum(widths[:i]) for i in range(len(widths))]

    def body(*refs):
        o_ref = refs[-1]
        for r, o, w in zip(refs[:-1], offs, widths):
            o_ref[:, o:o + w] = r[...]

    return pl.pallas_call(
        body, name=name, grid=(s // tm,), in_specs=[pl.BlockSpec((tm, w), lambda i: (i, 0)) for w in widths],
        out_specs=pl.BlockSpec((tm, sum(widths)), lambda i: (i, 0)), out_shape=SDS((s, sum(widths)), pieces[0].dtype),
        compiler_params=_params(("parallel",)),
    )(*pieces)


def _d_proj_in(dproj, w_sh, tm, after):
    s = dproj.shape[0]
    nsh, d, n = w_sh.shape
    ksteps = nsh // 2
    extra = [] if after is None else [after]

    def body(a_ref, b_ref, *rest):
        o_ref, acc = rest[len(extra)], rest[len(extra) + 1]
        k = pl.program_id(1)
        a = a_ref[...]
        prod = lax.dot_general(a[:, :n], b_ref[0], (NT, ((), ())), preferred_element_type=f32)
        prod += lax.dot_general(a[:, n:], b_ref[1], (NT, ((), ())), preferred_element_type=f32)

        @pl.when(k == 0)
        def _():
            acc[...] = prod

        @pl.when(k > 0)
        def _():
            acc[...] += prod

        @pl.when(k == ksteps - 1)
        def _():
            o_ref[...] = acc[...].astype(o_ref.dtype)

    return pl.pallas_call(
        body, name="d_proj_in", grid=(s // tm, ksteps),
        in_specs=[pl.BlockSpec((tm, 2 * n), lambda i, k: (i, k)), pl.BlockSpec((2, d, n), lambda i, k: (k, 0, 0))] + [ANY] * len(extra),
        out_specs=pl.BlockSpec((tm, d), lambda i, k: (i, 0)), out_shape=SDS((s, d), bf16),
        scratch_shapes=[pltpu.VMEM((tm, d), f32)], compiler_params=_params(("parallel", "arbitrary")),
    )(dproj, w_sh, *extra)


def _gw_part(name, h, dproj, n, ids, prev, after=None):
    s, d = h.shape
    kb = 512
    extra = ([] if prev is None else [prev]) + ([] if after is None else [after])

    def body(ids_ref, a_ref, b_ref, *rest):
        rest[-1][...] = lax.dot_general(a_ref[...], b_ref[...], (TN, ((), ())), preferred_element_type=f32).astype(bf16)

    return pl.pallas_call(
        body, name=name, out_shape=SDS((dproj.shape[1] // n, d, n), bf16),
        grid_spec=pltpu.PrefetchScalarGridSpec(
            num_scalar_prefetch=1, grid=(ids.shape[0], d // kb),
            in_specs=[pl.BlockSpec((s, kb), lambda j, i, ids_ref: (0, i)),
                      pl.BlockSpec((s, n), lambda j, i, ids_ref: (0, ids_ref[j]))] + [ANY] * len(extra),
            out_specs=pl.BlockSpec((None, kb, n), lambda j, i, ids_ref: (ids_ref[j], i, 0))),
        input_output_aliases={} if prev is None else {3: 0},
        compiler_params=_params(("parallel", "parallel")),
    )(ids, h, dproj, *extra)


def _proj_part(name, h, w_sh, ids, prev, tm, after=None):
    s, d = h.shape
    nsh, _, n = w_sh.shape
    extra = ([] if prev is None else [prev]) + ([] if after is None else [after])

    def body(ids_ref, a_ref, b_ref, *rest):
        rest[-1][...] = jnp.dot(a_ref[...], b_ref[...], preferred_element_type=f32)

    return pl.pallas_call(
        body, name=name, out_shape=SDS((s, nsh * n), f32),
        grid_spec=pltpu.PrefetchScalarGridSpec(
            num_scalar_prefetch=1, grid=(ids.shape[0], s // tm),
            in_specs=[pl.BlockSpec((tm, d), lambda j, i, ids_ref: (i, 0)),
                      pl.BlockSpec((None, d, n), lambda j, i, ids_ref: (ids_ref[j], 0, 0))] + [ANY] * len(extra),
            out_specs=pl.BlockSpec((tm, n), lambda j, i, ids_ref: (i, ids_ref[j]))),
        input_output_aliases={} if prev is None else {3: 0},
        compiler_params=_params(("parallel", "parallel")),
    )(ids, h, w_sh, *extra)


def _rowwise(name, fn, n_rows, tm, rows, fulls, row_outs, acc_outs=(), ncol=1):
    assert ncol == 1 or not acc_outs
    nr, nf, no, na = len(rows), len(fulls), len(row_outs), len(acc_outs)
    in_specs = [pl.BlockSpec((tm, w), functools.partial(lambda i, j, cb: (i, cb + j), cb=cb)) for (_, w, cb) in rows]
    in_specs += [pl.BlockSpec(a.shape, functools.partial(lambda i, j, nd: (0,) * nd, nd=a.ndim)) for a in fulls]
    out_shape = [SDS((n_rows, w), dt) for (w, dt) in row_outs] + [SDS(s, f32) for s in acc_outs]
    out_specs = [pl.BlockSpec((tm, w // ncol), lambda i, j: (i, j)) for (w, _) in row_outs]
    out_specs += [pl.BlockSpec(s, functools.partial(lambda i, j, nd: (0,) * nd, nd=len(s))) for s in acc_outs]

    def body(*refs):
        if na:
            @pl.when(pl.program_id(0) == 0)
            def _():
                for r in refs[nr + nf + no:]:
                    r[...] = jnp.zeros(r.shape, r.dtype)
        fn(refs[:nr], refs[nr:nr + nf], refs[nr + nf:nr + nf + no], refs[nr + nf + no:])

    return pl.pallas_call(
        body, name=name, grid=(n_rows // tm, ncol), in_specs=in_specs, out_specs=out_specs, out_shape=out_shape,
        compiler_params=_params(("arbitrary" if na else "parallel", "arbitrary" if na else "parallel")),
    )(*[r[0] for r in rows], *fulls)


def _rms(x):
    r = lax.rsqrt(jnp.mean(x * x, axis=-1, keepdims=True) + EPS)
    return r, x * r


def _colsum(v):
    return jnp.sum(v, axis=0, keepdims=True)


def _shift_down(u, row, k):
    return jnp.where(row >= k, pltpu.roll(u, k, 0), 0.0)


def _shift_up(u, row, k):
    n = u.shape[0]
    return jnp.where(row < n - k, pltpu.roll(u, n - k, 0), 0.0)


def _conv_fwd(proj, conv_w, s, dc):
    nb = dc // 128

    def body(ab_ref, ac_ref, ax_ref, w_ref, z_ref):
        u = ac_ref[...] * ax_ref[...]
        row = lax.broadcasted_iota(jnp.int32, u.shape, 0)
        w = w_ref[...]
        cv = w[0:1] * _shift_down(u, row, 2) + w[1:2] * _shift_down(u, row, 1) + w[2:3] * u
        z_ref[...] = (ab_ref[...] * cv).astype(z_ref.dtype)

    col = lambda off: pl.BlockSpec((s, 128), functools.partial(lambda j, off: (0, off + j), off=off))
    return pl.pallas_call(
        body, name="conv_fwd", grid=(nb,), in_specs=[col(0), col(nb), col(2 * nb), pl.BlockSpec((3, 128), lambda j: (0, j))],
        out_specs=pl.BlockSpec((s, 128), lambda j: (0, j)), out_shape=SDS((s, dc), bf16), compiler_params=_params(("parallel",)),
    )(proj, proj, proj, conv_w)


def _conv_bwd(proj, conv_w, dz, s, dc):
    nb = dc // 128

    def body(ab_ref, ac_ref, ax_ref, w_ref, dz_ref, dab_ref, dac_ref, dax_ref, dw_ref):
        ab, ac, ax, dzv = ab_ref[...], ac_ref[...], ax_ref[...], dz_ref[...].astype(f32)
        u = ac * ax
        row = lax.broadcasted_iota(jnp.int32, u.shape, 0)
        w = w_ref[...]
        u1, u2 = _shift_down(u, row, 1), _shift_down(u, row, 2)
        cv = w[0:1] * u2 + w[1:2] * u1 + w[2:3] * u
        dcv = dzv * ab
        dab_ref[...] = (dzv * cv).astype(dab_ref.dtype)
        du = w[2:3] * dcv + w[1:2] * _shift_up(dcv, row, 1) + w[0:1] * _shift_up(dcv, row, 2)
        dac_ref[...] = (du * ax).astype(dac_ref.dtype)
        dax_ref[...] = (du * ac).astype(dax_ref.dtype)
        dw_ref[0:1, :] = _colsum(dcv * u2)
        dw_ref[1:2, :] = _colsum(dcv * u1)
        dw_ref[2:3, :] = _colsum(dcv * u)

    col = lambda off: pl.BlockSpec((s, 128), functools.partial(lambda j, off: (0, off + j), off=off))
    blk = pl.BlockSpec((s, 128), lambda j: (0, j))
    return pl.pallas_call(
        body, name="conv_bwd", grid=(nb,),
        in_specs=[col(0), col(nb), col(2 * nb), pl.BlockSpec((3, 128), lambda j: (0, j)), blk],
        out_specs=[blk, blk, blk, pl.BlockSpec((3, 128), lambda j: (0, j))],
        out_shape=[SDS((s, dc), bf16)] * 3 + [SDS((3, dc), f32)], compiler_params=_params(("parallel",)),
    )(proj, proj, proj, conv_w, dz)


def _level_masks():
    t = np.arange(CHUNK)[:, None]
    s = np.arange(CHUNK)[None, :]
    m = np.stack([((t & h) != 0) & ((s & h) == 0) & (t // (2 * h) == s // (2 * h)) for h in LEVELS]).astype(np.float32)
    return jnp.asarray(m), jnp.asarray(m.transpose(0, 2, 1))


def _cumsum_rows(x, row):
    for sh in (1, 2, 4, 8, 16, 32):
        x = x + jnp.where(row >= sh, pltpu.roll(x, sh, 0), 0.0)
    return x


def _rev_cumsum_rows(x, row):
    n = x.shape[0]
    for sh in (1, 2, 4, 8, 16, 32):
        x = x + jnp.where(row < n - sh, pltpu.roll(x, n - sh, 0), 0.0)
    return x


def _chunk_terms(qp, fl, lb):
    row = lax.broadcasted_iota(jnp.int32, qp.shape, 0)
    sig = _sigmoid(fl)
    f = lb + (1.0 - lb) * sig
    k = 1.0 - f
    sq = _sigmoid(qp)
    qh = qp * sq
    b = _cumsum_rows(jnp.log(f), row)
    sub = lax.broadcasted_iota(jnp.int32, (CHUNK // 8, 8, DK), 1)
    b8 = b.reshape(CHUNK // 8, 8, DK)
    us, exs, ups = [], [], []
    for m in LEVELS:
        sb = 2 * m
        if sb >= 8:
            b3 = b.reshape(CHUNK // sb, sb, DK)
            bref = jnp.broadcast_to(b3[:, m - 1:m, :], b3.shape).reshape(CHUNK, DK)
        else:
            bref8 = None
            for j in range(8 // sb):
                cand = jnp.broadcast_to(b8[:, j * sb + m - 1:j * sb + m, :], b8.shape)
                bref8 = cand if bref8 is None else jnp.where(sub >= j * sb, cand, bref8)
            bref = bref8.reshape(CHUNK, DK)
        up = (row & m) != 0
        ex = jnp.exp(jnp.where(up, b - bref, bref - b))
        us.append((jnp.where(up, qh, k) * ex).astype(bf16))
        exs.append(ex)
        ups.append(up)
    blast = b[CHUNK - 1:CHUNK, :]
    eb, ebl = jnp.exp(b), jnp.exp(blast - b)
    return dict(sig=sig, f=f, k=k, sq=sq, qh=qh, u=jnp.stack(us), ex=exs, up=ups, eb=eb, ebl=ebl, qt=qh * eb, kt=k * ebl,
                el=jnp.exp(blast), row=row)


def _scores(t, mask):
    pl_ = jnp.einsum("ltk,lsk->lts", t["u"], t["u"], preferred_element_type=f32)
    p = jnp.sum(pl_ * mask, axis=0)
    r = lax.broadcasted_iota(jnp.int32, (CHUNK, CHUNK), 0)
    c = lax.broadcasted_iota(jnp.int32, (CHUNK, CHUNK), 1)
    diag = jnp.sum(t["qh"] * t["k"], axis=-1, keepdims=True)
    return p + jnp.where(r == c, diag, 0.0)


def _hgrn_fwd(proj, lb_param, gnorm, s, dv_total, tb):
    nchunk = tb // CHUNK
    masks, _ = _level_masks()
    q0, f0, v0, g0 = 3 * HEADS, 4 * HEADS, 5 * HEADS, 6 * HEADS

    def body(q_ref, f_ref, v_ref, g_ref, lb_ref, gn_ref, mask_ref, og_ref, o_ref, st_ref, state):
        @pl.when(pl.program_id(1) == 0)
        def _():
            state[...] = jnp.zeros_like(state)

        lbp = lb_ref[...]
        lb_all = _sigmoid(lbp[0:1, :] - lbp[1:2, :])
        mask = mask_ref[...]
        for i, hh in [(i, hh) for i in range(nchunk) for hh in range(HEADS_PER_STEP)]:
            rs, cs = pl.ds(i * CHUNK, CHUNK), pl.ds(hh * DK, DK)
            t = _chunk_terms(q_ref[rs, cs], f_ref[rs, cs], lb_all[:, hh * DK:(hh + 1) * DK])
            v = v_ref[rs, cs]
            vb = v.astype(bf16)
            st = state[hh]
            st_ref[i, hh] = st
            p = _scores(t, mask)
            o = jnp.dot(p.astype(bf16), vb, preferred_element_type=f32)
            o += lax.dot_general(t["qt"].astype(bf16), st.astype(bf16), (NT, ((), ())), preferred_element_type=f32)
            state[hh] = st * t["el"] + lax.dot_general(vb, t["kt"].astype(bf16), (TN, ((), ())), preferred_element_type=f32)
            o_ref[rs, cs] = o
            r, oh = _rms(o)
            g = g_ref[rs, cs]
            og_ref[rs, cs] = (oh * gn_ref[...] * (g * _sigmoid(g))).astype(og_ref.dtype)

    hp, wide = HEADS_PER_STEP, HEADS_PER_STEP * DK
    col = lambda off: pl.BlockSpec((tb, wide), functools.partial(lambda h, t, off: (t, off + h), off=off // hp))
    blk = pl.BlockSpec((tb, wide), lambda h, t: (t, h))
    return pl.pallas_call(
        body, name="hgrn_fwd", grid=(HEADS // hp, s // tb),
        in_specs=[col(q0), col(f0), col(v0), col(g0), pl.BlockSpec((2, wide), lambda h, t: (0, h)),
                  pl.BlockSpec((1, DK), lambda h, t: (0, 0)), pl.BlockSpec(masks.shape, lambda h, t: (0, 0, 0))],
        out_specs=[blk, blk, pl.BlockSpec((nchunk, hp, DK, DK), lambda h, t: (t, h, 0, 0))],
        out_shape=[SDS((s, dv_total), bf16), SDS((s, dv_total), f32), SDS((s // CHUNK, HEADS, DK, DK), f32)],
        scratch_shapes=[pltpu.VMEM((hp, DK, DK), f32)], compiler_params=_params(("parallel", "arbitrary")),
    )(proj, proj, proj, proj, lb_param, gnorm, masks)


def _hgrn_bwd(proj, lb_param, gnorm, o_saved, states, dog, s, dv_total, tb):
    nchunk = tb // CHUNK
    nt = s // tb
    nc_total = s // CHUNK
    masks, masks_t = _level_masks()
    q0, f0, v0, g0 = 3 * HEADS, 4 * HEADS, 5 * HEADS, 6 * HEADS

    def body(q_ref, f_ref, v_ref, g_ref, lb_ref, gn_ref, mask_ref, maskt_ref, o_ref, dog_ref, st_ref, stn_ref,
             dq_ref, df_ref, dv_ref, dg_ref, dlb_ref, dgn_ref, gstate):
        h_id, t_id = pl.program_id(0), pl.program_id(1)

        @pl.when(t_id == 0)
        def _():
            gstate[...] = jnp.zeros_like(gstate)
            dlb_ref[...] = jnp.zeros_like(dlb_ref)

        @pl.when((t_id == 0) & (h_id == 0))
        def _():
            dgn_ref[...] = jnp.zeros_like(dgn_ref)

        lbp = lb_ref[...]
        lb_all = _sigmoid(lbp[0:1, :] - lbp[1:2, :])
        mask, maskt = mask_ref[...], maskt_ref[...]
        gn = gn_ref[...]
        for i, hh in [(i, hh) for i in reversed(range(nchunk)) for hh in range(HEADS_PER_STEP)]:
            rs, cs = pl.ds(i * CHUNK, CHUNK), pl.ds(hh * DK, DK)
            lb = lb_all[:, hh * DK:(hh + 1) * DK]
            qp, fl, v, g = q_ref[rs, cs], f_ref[rs, cs], v_ref[rs, cs], g_ref[rs, cs]
            t = _chunk_terms(qp, fl, lb)
            vb = v.astype(bf16)
            st0 = st_ref[i, hh]
            st1 = st_ref[i + 1, hh] if i + 1 < nchunk else stn_ref[0, hh]
            gt = gstate[hh]
            o = o_ref[rs, cs]
            r, oh = _rms(o)
            sg = _sigmoid(g)
            dog_v = dog_ref[rs, cs].astype(f32)
            dg_ref[rs, cs] = (dog_v * (oh * gn) * _dsilu(g, sg)).astype(dg_ref.dtype)
            don = dog_v * (g * sg)
            dgn_ref[...] += _colsum(don * oh)
            doh = don * gn
            do = r * (doh - oh * jnp.mean(doh * oh, axis=-1, keepdims=True))
            dob = do.astype(bf16)
            d = lax.dot_general(dob, vb, (NT, ((), ())), preferred_element_type=f32)
            dt = lax.dot_general(vb, dob, (NT, ((), ())), preferred_element_type=f32)
            z = (mask * d[None] + maskt * dt[None]).astype(bf16)
            rr = jnp.einsum("lts,lsk->ltk", z, t["u"], preferred_element_type=f32)
            dq = jnp.zeros((CHUNK, DK), f32)
            dk = jnp.zeros((CHUNK, DK), f32)
            qdk = jnp.zeros((CHUNK, DK), f32)
            for li in range(len(LEVELS)):
                du = t["ex"][li] * rr[li]
                dq += jnp.where(t["up"][li], du, 0.0)
                dk += jnp.where(t["up"][li], 0.0, du)
                e = t["u"][li].astype(f32) * rr[li]
                qdk += jnp.where(t["up"][li], e, -e)
            dd = jnp.sum(do * v, axis=-1, keepdims=True)
            dq += dd * t["k"]
            dk += dd * t["qh"]
            gtb = gt.astype(bf16)
            ktb, qtb = t["kt"].astype(bf16), t["qt"].astype(bf16)
            dq_in = jnp.dot(dob, st0.astype(bf16), preferred_element_type=f32)
            dk_in = jnp.dot(vb, gtb, preferred_element_type=f32)
            dq += t["eb"] * dq_in
            dk += t["ebl"] * dk_in
            qdk += qtb.astype(f32) * dq_in - ktb.astype(f32) * dk_in
            p = _scores(t, mask)
            dvv = lax.dot_general(p.astype(bf16), dob, (TN, ((), ())), preferred_element_type=f32)
            dvv += lax.dot_general(ktb, gtb, (NT, ((), ())), preferred_element_type=f32)
            dv_ref[rs, cs] = dvv.astype(dv_ref.dtype)
            a_end = _colsum(gtb.astype(f32) * st1)
            dlf = _rev_cumsum_rows(qdk, t["row"]) + a_end
            dfv = dlf / t["f"] - dk
            df_ref[rs, cs] = (dfv * (1.0 - lb) * t["sig"] * (1.0 - t["sig"])).astype(df_ref.dtype)
            dlb_ref[:, cs] += _colsum(dfv * (1.0 - t["sig"]))
            dq_ref[rs, cs] = (dq * _dsilu(qp, t["sq"])).astype(dq_ref.dtype)
            gstate[hh] = gt * t["el"] + lax.dot_general(dob, qtb, (TN, ((), ())), preferred_element_type=f32)

    hp, wide = HEADS_PER_STEP, HEADS_PER_STEP * DK
    rev = lambda t: nt - 1 - t
    col = lambda off: pl.BlockSpec((tb, wide), functools.partial(lambda h, t, off: (rev(t), off + h), off=off // hp))
    blk = pl.BlockSpec((tb, wide), lambda h, t: (rev(t), h))
    nxt = lambda h, t: (jnp.minimum((rev(t) + 1) * nchunk, nc_total - 1), h, 0, 0)
    return pl.pallas_call(
        body, name="hgrn_bwd", grid=(HEADS // hp, nt),
        in_specs=[col(q0), col(f0), col(v0), col(g0), pl.BlockSpec((2, wide), lambda h, t: (0, h)),
                  pl.BlockSpec((1, DK), lambda h, t: (0, 0)), pl.BlockSpec(masks.shape, lambda h, t: (0, 0, 0)),
                  pl.BlockSpec(masks.shape, lambda h, t: (0, 0, 0)), blk, blk,
                  pl.BlockSpec((nchunk, hp, DK, DK), lambda h, t: (rev(t), h, 0, 0)),
                  pl.BlockSpec((1, hp, DK, DK), nxt)],
        out_specs=[blk, blk, blk, blk, pl.BlockSpec((1, wide), lambda h, t: (0, h)), pl.BlockSpec((1, DK), lambda h, t: (0, 0))],
        out_shape=[SDS((s, dv_total), bf16)] * 4 + [SDS((1, HEADS * DK), f32), SDS((1, DK), f32)],
        scratch_shapes=[pltpu.VMEM((hp, DK, DK), f32)], compiler_params=_params(("arbitrary", "arbitrary")),
    )(proj, proj, proj, proj, lb_param, gnorm, masks, masks_t, o_saved, dog, states, states)


def _adamw(name, g, w, m, v):
    r, c = w.shape
    tr = r
    for cand in (256, 128, 64, 32, 16, 8):
        if r % cand == 0 and r > cand:
            tr = cand
            break

    def body(g_ref, w_ref, m_ref, v_ref, d_ref, mo_ref, vo_ref):
        d_ref[...], mo_ref[...], vo_ref[...] = _adamw_math(g_ref[...], w_ref[...], m_ref[...], v_ref[...])

    blk = pl.BlockSpec((tr, c), lambda i: (i, 0))
    return pl.pallas_call(
        body, name=name, grid=(r // tr,), in_specs=[blk] * 4, out_specs=[blk] * 3, out_shape=[SDS((r, c), f32)] * 3,
        compiler_params=_params(("parallel",)),
    )(g, w, m, v)


def _ffn_in(h2, w_gt, w_ut, tm, ffb):
    s, d = h2.shape
    dff = w_gt.shape[0]

    def body(a_ref, wg_ref, wu_ref, dg_ref, du_ref, act_ref):
        a = a_ref[...]
        g = lax.dot_general(a, wg_ref[...], (NT, ((), ())), preferred_element_type=f32)
        u = lax.dot_general(a, wu_ref[...], (NT, ((), ())), preferred_element_type=f32)
        sg = _sigmoid(g)
        silu = g * sg
        dg_ref[...] = (u * _dsilu(g, sg)).astype(bf16)
        du_ref[...] = silu.astype(bf16)
        act_ref[...] = (silu * u).astype(bf16)

    w_spec = pl.BlockSpec((ffb, d), lambda j, i: (j, 0))
    o_spec = pl.BlockSpec((tm, ffb), lambda j, i: (i, j))
    return pl.pallas_call(
        body, name="ffn_in", grid=(dff // ffb, s // tm), in_specs=[pl.BlockSpec((tm, d), lambda j, i: (i, 0)), w_spec, w_spec],
        out_specs=[o_spec] * 3, out_shape=[SDS((s, dff), bf16)] * 3, compiler_params=_params(("parallel", "parallel")),
    )(h2, w_gt, w_ut)


def _ffn_down_bwd(dff_out, w_d, act_dg, act_du, tm, ffb):
    s, d = dff_out.shape
    dff = w_d.shape[0]

    def body(a_ref, w_ref, fg_ref, fu_ref, dg_ref, du_ref):
        da = lax.dot_general(a_ref[...], w_ref[...], (NT, ((), ())), preferred_element_type=f32)
        dg_ref[...] = (da * fg_ref[...].astype(f32)).astype(bf16)
        du_ref[...] = (da * fu_ref[...].astype(f32)).astype(bf16)

    t_spec = pl.BlockSpec((tm, ffb), lambda j, i: (i, j))
    return pl.pallas_call(
        body, name="d_ffn_down_in", grid=(dff // ffb, s // tm),
        in_specs=[pl.BlockSpec((tm, d), lambda j, i: (i, 0)), pl.BlockSpec((ffb, d), lambda j, i: (j, 0)), t_spec, t_spec],
        out_specs=[t_spec] * 2, out_shape=[SDS((s, dff), bf16)] * 2, compiler_params=_params(("parallel", "parallel")),
    )(dff_out, w_d, act_dg, act_du)


def _branch_merge(z_a, og, w_co, w_ho, proj, tm, gate_a0, gate_b0):
    s, dc = z_a.shape
    nsh, _, n = w_co.shape

    def body(za_ref, og_ref, wa_ref, wb_ref, ga_ref, gb_ref, m_ref, sa_ref, sb_ref, fa_ref, fb_ref):
        ya = jnp.dot(za_ref[...], wa_ref[...], preferred_element_type=f32)
        yb = jnp.dot(og_ref[...], wb_ref[...], preferred_element_type=f32)
        sa, sb_ = _sigmoid(ga_ref[...]), _sigmoid(gb_ref[...])
        m_ref[...] = (sa * ya + sb_ * yb).astype(bf16)
        sa_ref[...] = sa.astype(bf16)
        sb_ref[...] = sb_.astype(bf16)
        fa_ref[...] = (ya * sa * (1.0 - sa)).astype(bf16)
        fb_ref[...] = (yb * sb_ * (1.0 - sb_)).astype(bf16)

    a_spec = pl.BlockSpec((tm, dc), lambda i, j: (i, 0))
    w_spec = pl.BlockSpec((None, dc, n), lambda i, j: (j, 0, 0))
    gate = lambda c0: pl.BlockSpec((tm, n), functools.partial(lambda i, j, cb: (i, cb + j), cb=c0 // n))
    o_spec = pl.BlockSpec((tm, n), lambda i, j: (i, j))
    return pl.pallas_call(
        body, name="branch_merge", grid=(s // tm, nsh), in_specs=[a_spec, a_spec, w_spec, w_spec, gate(gate_a0), gate(gate_b0)],
        out_specs=[o_spec] * 5, out_shape=[SDS((s, nsh * n), bf16)] * 5, compiler_params=_params(("parallel", "parallel")),
    )(z_a, og, w_co, w_ho, proj, proj)


def _d_branch_merge(dmo, w_o, factors, tm):
    s, d = dmo.shape
    n = d // 2

    def body(a_ref, w_ref, sa_ref, sb_ref, fa_ref, fb_ref, dya_ref, dyb_ref, dga_ref, dgb_ref):
        dm = lax.dot_general(a_ref[...], w_ref[...], (NT, ((), ())), preferred_element_type=f32)
        for f_ref, o_ref in ((sa_ref, dya_ref), (sb_ref, dyb_ref), (fa_ref, dga_ref), (fb_ref, dgb_ref)):
            o_ref[...] = (dm * f_ref[...].astype(f32)).astype(bf16)

    t_spec = pl.BlockSpec((tm, n), lambda j, i: (i, j))
    return pl.pallas_call(
        body, name="d_branch_merge", grid=(d // n, s // tm),
        in_specs=[pl.BlockSpec((tm, d), lambda j, i: (i, 0)), pl.BlockSpec((n, d), lambda j, i: (j, 0))] + [t_spec] * 4,
        out_specs=[t_spec] * 4, out_shape=[SDS((s, d), bf16)] * 4, compiler_params=_params(("parallel", "parallel")),
    )(dmo, w_o, *factors)


def _local_step(x, tgt, mod, g_mix, g_ffn, g_fin, lb_param, gnorm, conv_w, project, grad_in, get_w, prefetch, put_g, sent):
    s, d = x.shape
    dc = d // 2
    tm = min(512, s)
    tm2 = min(1024, s)
    te = min(256, s)
    tb = min(128, s)
    mt = s // tm
    nb = 512
    sh_m, sc_m, gt_m, sh_f, sc_f, gt_f = [mod[i] for i in range(N_MOD)]
    dh2 = d // 2

    def e1(rows, fulls, outs, accs):
        xv = rows[0][...]
        g, sc, sh = [r[...] for r in fulls]
        _, xh = _rms(xv)
        outs[0][...] = (xh * g * (1.0 + sc) + sh).astype(bf16)

    h, = _rowwise("prenorm_mix", e1, s, te, [(x, d, 0)], [g_mix, sc_m, sh_m], [(d, bf16)])
    proj, w_in = project(h, tm2)
    nsh, _, win_sh = w_in.shape
    z_a = _conv_fwd(proj, _after_tokens(conv_w, [prefetch("mix", proj)]), s, dc)
    og, o_saved, states = _hgrn_fwd(proj, lb_param, gnorm, s, dc, tb)
    w_co, w_ho, w_o = get_w("mix", og)

    gate_a0, gate_b0 = 7 * dh2, 9 * dh2
    merged, *merge_factors = _branch_merge(z_a, og, w_co, w_ho, proj, tm2, gate_a0, gate_b0)
    mo = _matmul("mix_out", merged, w_o, (s, d), bf16, (2, s // tm2, 1), pl.BlockSpec((tm2, d), lambda j, i, k: (i, 0)),
                 pl.BlockSpec((d, dh2), lambda j, i, k: (0, j)), pl.BlockSpec((tm2, dh2), lambda j, i, k: (i, j)), NN,
                 after=prefetch("ffn", merged))

    def e6(rows, fulls, outs, accs):
        xv, mov = rows[0][...], rows[1][...].astype(f32)
        gt, g, sc, sh = [r[...] for r in fulls]
        x1 = xv + gt * mov
        outs[0][...] = x1
        _, xh = _rms(x1)
        outs[1][...] = (xh * g * (1.0 + sc) + sh).astype(bf16)

    x1, h2 = _rowwise("prenorm_ffn", e6, s, tm, [(x, d, 0), (mo, d, 0)], [gt_m, g_ffn, sc_f, sh_f], [(d, f32), (d, bf16)])
    w_gt, w_ut = get_w("ffn", h2)
    dff_ = w_gt.shape[0]
    act_dg, act_du, act = _ffn_in(h2, w_gt, w_ut, tm2, nb)
    prefetch("ffn_down", act)
    w_d, = get_w("ffn_down", act)
    ff = _matmul("ffn_down", act, w_d, (s, d), bf16, (2, mt, 1), pl.BlockSpec((tm, dff_), lambda j, i, k: (i, 0)),
                 pl.BlockSpec((dff_, dh2), lambda j, i, k: (0, j)), pl.BlockSpec((tm, dh2), lambda j, i, k: (i, j)), NN)

    def e9(rows, fulls, outs, accs):
        x1v, ffv, tv = [r[...].astype(f32) for r in rows]
        gt, gf = fulls[0][...], fulls[1][...]
        x2 = x1v + gt * ffv
        r, xh = _rms(x2)
        err = xh * gf - tv
        accs[0][...] += 0.5 * jnp.sum(jnp.mean(err * err, axis=-1, keepdims=True), axis=0, keepdims=True)
        dy = err / d
        accs[1][...] += _colsum(dy * xh)
        dxh = dy * gf
        dx2 = r * (dxh - xh * jnp.mean(dxh * xh, axis=-1, keepdims=True))
        outs[0][...] = dx2
        outs[1][...] = (dx2 * gt).astype(bf16)
        accs[2][...] += _colsum(dx2 * ffv)

    dx2, dff, loss_acc, dg_fin, dgt_f = _rowwise("loss_head", e9, s, tm, [(x1, d, 0), (ff, d, 0), (tgt, d, 0)], [gt_f, g_fin],
                                                 [(d, f32), (d, bf16)], [(1, 128), (1, d), (1, d)])
    dgg, duu = _ffn_down_bwd(dff, w_d, act_dg, act_du, tm, dff_ // 4)

    def wgrad_rows(name, a, b, n_out):
        kb = 512
        return _matmul(name, a, b, (n_out, d), bf16, (n_out // kb, 2, 1), pl.BlockSpec((s, kb), lambda i, j, k: (0, i)),
                       pl.BlockSpec((s, dh2), lambda i, j, k: (0, j)), pl.BlockSpec((kb, dh2), lambda i, j, k: (i, j)), TN)

    gw_d = wgrad_rows("gw_ffn_down", act, dff, dff_)

    def ffn_in_bwd(name, a, w, after=None):
        return _matmul(name, a, w, (s, d), bf16, (2, mt, 1), pl.BlockSpec((tm, dff_), lambda j, i, k: (i, 0)),
                       pl.BlockSpec((dff_, dh2), lambda j, i, k: (0, j)), pl.BlockSpec((tm, dh2), lambda j, i, k: (i, j)), NN,
                       after=after)

    dh2b = ffn_in_bwd("d_ffn_up_in", duu, w_ut)
    gw_ut = wgrad_rows("gw_ffn_up", duu, h2, dff_)
    gw_gt = wgrad_rows("gw_ffn_gate", dgg, h2, dff_)
    dh2a = ffn_in_bwd("d_ffn_gate_in", dgg, w_gt, after=put_g("ffn", [gw_gt, gw_ut, gw_d]))
    sc_f_late = _after_tokens(sc_f, [sent("ffn", dh2a)])

    def b5(rows, fulls, outs, accs):
        da, db, x1v, dx2v, mov = [r[...].astype(f32) for r in rows]
        sc, g, gt = [r[...] for r in fulls]
        dh = da + db
        r, xh = _rms(x1v)
        accs[0][...] += _colsum(dh)
        accs[1][...] += _colsum(dh * (xh * g))
        dn = dh * (1.0 + sc)
        accs[2][...] += _colsum(dn * xh)
        dxh = dn * g
        dx1 = dx2v + r * (dxh - xh * jnp.mean(dxh * xh, axis=-1, keepdims=True))
        outs[0][...] = dx1
        accs[3][...] += _colsum(dx1 * mov)
        outs[1][...] = (dx1 * gt).astype(bf16)

    dx1, dmo, dsh_f, dsc_f, dg_ffn, dgt_m = _rowwise(
        "d_prenorm_ffn", b5, s, te, [(dh2a, d, 0), (dh2b, d, 0), (x1, d, 0), (dx2, d, 0), (mo, d, 0)], [sc_f_late, g_ffn, gt_m],
        [(d, f32), (d, bf16)], [(1, d)] * 4)
    dya, dyb, dga, dgb = _d_branch_merge(dmo, w_o, merge_factors, tm)
    gw_o = wgrad_rows("gw_mix_out", merged, dmo, d)

    def out_proj_bwd(name, dy, w):
        return _matmul(name, dy, w, (s, dc), bf16, (1, s // tm2, nsh), pl.BlockSpec((tm2, d // nsh), lambda j, i, k: (i, k)),
                       pl.BlockSpec((None, dc, d // nsh), lambda j, i, k: (k, 0, 0)), pl.BlockSpec((tm2, dc), lambda j, i, k: (i, 0)), NT)

    def out_proj_wgrad(name, a, dy):
        return _matmul(name, a, dy, (nsh, dc, d // nsh), bf16, (1, nsh, 1), pl.BlockSpec((s, dc), lambda i, j, k: (0, 0)),
                       pl.BlockSpec((s, d // nsh), lambda i, j, k: (0, j)), pl.BlockSpec((None, dc, d // nsh), lambda i, j, k: (j, 0, 0)), TN)

    dz_a = out_proj_bwd("d_conv_out_in", dya, w_co)
    dog = out_proj_bwd("d_hgrn_out_in", dyb, w_ho)
    gw_co = out_proj_wgrad("gw_conv_out", z_a, dya)
    gw_ho = out_proj_wgrad("gw_hgrn_out", og, dyb)
    conv_w_late = _after_tokens(conv_w, [put_g("mix", [gw_co, gw_ho, gw_o])])
    dab, dac, dax, dconv_w = _conv_bwd(proj, conv_w_late, dz_a, s, dc)
    lb_param_late = _after_tokens(lb_param, [sent("mix", dab)])
    dq, dfl, dvi, dgo, dlb, dgn = _hgrn_bwd(proj, lb_param_late, gnorm, o_saved, states, dog, s, dc, tb)
    dproj = _concat_columns("d_proj", [dab, dac, dax, dq, dfl, dvi, dgo, dga, dgb], te)
    dh = _d_proj_in(dproj, w_in, tm, grad_in(h, dproj, win_sh))

    def b12(rows, fulls, outs, accs):
        dhv, xv, dx1v = [r[...].astype(f32) for r in rows]
        sc, g = fulls[0][...], fulls[1][...]
        r, xh = _rms(xv)
        accs[0][...] += _colsum(dhv)
        accs[1][...] += _colsum(dhv * (xh * g))
        dn = dhv * (1.0 + sc)
        accs[2][...] += _colsum(dn * xh)
        dxh = dn * g
        outs[0][...] = dx1v + r * (dxh - xh * jnp.mean(dxh * xh, axis=-1, keepdims=True))

    dx, dsh_m, dsc_m, dg_mix = _rowwise("d_prenorm_mix", b12, s, tm, [(dh, d, 0), (x, d, 0), (dx1, d, 0)], [sc_m, g_mix],
                                        [(d, f32)], [(1, d)] * 3)
    dmod = [dsh_m, dsc_m, dgt_m, dsh_f, dsc_f, dgt_f]
    small = dict(loss=loss_acc, g_mix=dg_mix, g_ffn=dg_ffn, g_fin=dg_fin, lb=dlb, gnorm=dgn, conv_w=dconv_w)
    return dx, dmod, small


def _ada_fwd(c_all, w_sh, b_sh):
    def body(c_ref, w_ref, b_ref, o_ref):
        cv = c_ref[...]
        ca = (cv * _sigmoid(cv)).astype(bf16)
        o_ref[...] = jnp.dot(ca, w_ref[...].astype(bf16), preferred_element_type=f32) + b_ref[...]

    return pl.pallas_call(body, name="ada_fwd", out_shape=SDS((c_all.shape[0], w_sh.shape[1]), f32),
                          compiler_params=pltpu.CompilerParams(vmem_limit_bytes=V7X_VMEM_LIMIT))(c_all, w_sh, b_sh)


def _ada_wgrad(c_all, dmod_sh):
    def body(c_ref, d_ref, o_ref):
        cv = c_ref[...]
        ca = (cv * _sigmoid(cv)).astype(bf16)
        o_ref[...] = lax.dot_general(ca, d_ref[...].astype(bf16), (TN, ((), ())), preferred_element_type=f32)

    return pl.pallas_call(body, name="ada_wgrad", out_shape=SDS((c_all.shape[1], dmod_sh.shape[1]), f32),
                          compiler_params=pltpu.CompilerParams(vmem_limit_bytes=V7X_VMEM_LIMIT))(c_all, dmod_sh)


def _lb_grad(lb_param, dlb):
    def body(p_ref, d_ref, o_ref):
        p = p_ref[...]
        lb = _sigmoid(p[0:1, :] - p[1:2, :])
        gl = d_ref[...] * lb * (1.0 - lb)
        o_ref[0:1, :] = gl
        o_ref[1:2, :] = -gl

    return pl.pallas_call(body, name="lb_grad", out_shape=SDS(lb_param.shape, f32))(lb_param, dlb)


def _sum_small(gathered):
    def body(g_ref, o_ref):
        acc = g_ref[0]
        for dd in range(1, NDEV):
            acc = acc + g_ref[dd]
        o_ref[...] = acc

    return pl.pallas_call(body, name="sum_small", out_shape=SDS(gathered.shape[1:], f32))(gathered)


def kernel(x, c, w_ada, b_ada, norm_mix_g, w_in, conv_w, lb_param, gnorm_g, w_conv_out, w_hgrn_out, w_o, norm_ffn_g, w_ffn_gate, w_ffn_up, w_ffn_down, norm_final_g, loss_target, m_w_ada, m_b_ada, m_norm_mix_g, m_w_in, m_conv_w, m_lb_param, m_gnorm_g, m_w_conv_out, m_w_hgrn_out, m_w_o, m_norm_ffn_g, m_w_ffn_gate, m_w_ffn_up, m_w_ffn_down, m_norm_final_g, v_w_ada, v_b_ada, v_norm_mix_g, v_w_in, v_conv_w, v_lb_param, v_gnorm_g, v_w_conv_out, v_w_hgrn_out, v_w_o, v_norm_ffn_g, v_w_ffn_gate, v_w_ffn_up, v_w_ffn_down, v_norm_final_g):
    assert lb_param.shape[0] == 2 and w_ada.shape[0] == 1
    s, d = x.shape[1], x.shape[2]
    me = 4 * lax.axis_index("x") + 2 * lax.axis_index("y") + lax.axis_index("c")
    ada_cols = w_ada.shape[2]

    me1 = me.astype(jnp.int32).reshape(1)
    placed_in = [_place_shard("place_in0", me1, w_in[0])[0]]
    c_all, cw_all = _all_gather("gather_cond", [c, conv_w[0]], after=placed_in)
    c_all = c_all.reshape(NDEV, d)
    conv_w_full = jnp.transpose(cw_all, (1, 0, 2)).reshape(conv_w.shape[1], -1)
    b_sh = lax.dynamic_slice_in_dim(b_ada, me * ada_cols, ada_cols, axis=1)
    mod_cols = _ada_fwd(c_all, w_ada[0], b_sh)
    mod_all, = _all_gather("gather_mod", [mod_cols])
    mod = lax.dynamic_index_in_dim(mod_all, me, axis=1, keepdims=False).reshape(N_MOD, 1, d)

    shard_groups = {"mix": [w_conv_out[0], w_hgrn_out[0], w_o[0]], "ffn": [w_ffn_gate[0].T, w_ffn_up[0].T],
                    "ffn_down": [w_ffn_down[0]]}
    own_slot = lambda frm, to: _flat(frm)
    gather_plan = lambda n: [(a, j, own_slot, own_slot) for a in range(n) for j in (1,) + ICI_RELATIONS]
    flat = lambda a: a.reshape(a.shape[0] * a.shape[1], a.shape[2])
    to8 = lambda a: a.reshape(NDEV, a.shape[0] // NDEV, a.shape[1])
    near, far = ICI_RELATIONS[:2], ICI_RELATIONS[2:]
    in_plan = lambda rels: [(0, j, own_slot, own_slot) for j in rels]
    ss_a, rs_a, _, lands, tok_a = _push_start("gather_start_in_sib", [], placed_in, in_plan((1,)), after=mod_all)
    ss_b, rs_b, _, lands_in, tok_b = _push_start("gather_start_in_near", [], lands, in_plan(near), after=tok_a)
    gathering, tokens, placed = {}, [tok_a, tok_b], {}
    for grp, sh in shard_groups.items():
        placed[grp] = []
        for i, a in enumerate(sh):
            buf, tok = _place_shard(f"place_{grp}{i}", me1, a, after=tokens[-1])
            placed[grp].append(buf)
            tokens.append(tok)

    pos = (lax.axis_index("x"), lax.axis_index("y"), lax.axis_index("c"))
    ids = lambda frm, rels: jnp.stack([_flat(_peer(frm, j)) for j in rels]).astype(jnp.int32)

    def project(h, tm):
        sib = _peer(pos, 1)
        fwd_of = lambda idxs: [(0, 1, fwd_slot(i), fwd_slot(i)) for i in idxs]
        _, lands = _push_wait("gather_wait_in_sib", ss_a, rs_a, [], lands_in, in_plan((1,)), h)
        proj = _proj_part("proj_local", h, lands[0], ids(pos, (0, 1)), None, tm)
        _, lands = _push_wait("gather_wait_in_near", ss_b, rs_b, [], lands, in_plan(near), proj)
        ss_c, rs_c, _, lands, tok = _push_start("gather_start_in_far", [], lands, in_plan(far))
        for grp in shard_groups:
            ss, rs, _, bufs, tok = _push_start("gather_start_" + grp, [], placed[grp], gather_plan(len(placed[grp])), after=tok)
            gathering[grp] = (ss, rs, bufs)
        ss1, rs1, _, lands, tok = _push_start("gather_fwd_in_near", [], lands, fwd_of((0, 1)), after=tok)
        proj = _proj_part("proj_near", h, lands[0], ids(pos, near), proj, tm, after=tok)
        _, lands = _push_wait("gather_fwd_wait_in_near", ss1, rs1, [], lands, fwd_of((0, 1)), proj)
        proj = _proj_part("proj_fwd_near", h, lands[0], ids(sib, near), proj, tm)
        _, lands = _push_wait("gather_wait_in_far", ss_c, rs_c, [], lands, in_plan(far), proj)
        ss2, rs2, _, lands, tok = _push_start("gather_fwd_in_far", [], lands, fwd_of((2,)))
        proj = _proj_part("proj_far", h, lands[0], ids(pos, far), proj, tm, after=tok)
        _, lands = _push_wait("gather_fwd_wait_in_far", ss2, rs2, [], lands, fwd_of((2,)), proj)
        proj = _proj_part("proj_fwd_far", h, lands[0], ids(sib, far), proj, tm)
        return proj, lands[0]

    fwd_slot = lambda i: (lambda frm, to: _flat(_peer(frm, ICI_RELATIONS[i])))
    fwd_plan = lambda n: [(a, 1, fwd_slot(i), fwd_slot(i)) for a in range(n) for i in range(len(ICI_RELATIONS))]
    forwarding = {}

    def prefetch(grp, after):
        ss, rs, lands = gathering[grp]
        _, lands = _push_wait("gather_wait_" + grp, ss, rs, [], lands, gather_plan(len(lands)), after)
        ss, rs, _, lands, tok = _push_start("gather_fwd_" + grp, [], lands, fwd_plan(len(lands)))
        forwarding[grp] = (ss, rs, lands)
        return tok

    def get_w(grp, after):
        ss, rs, lands = forwarding[grp]
        _, full = _push_wait("gather_fwd_wait_" + grp, ss, rs, [], lands, fwd_plan(len(lands)), after)
        return [f if i < 2 and grp == "mix" else flat(f) for i, f in enumerate(full)]

    core = lax.axis_index("c").astype(jnp.int32).reshape(1)
    chip = (2 * lax.axis_index("x") + lax.axis_index("y")).astype(jnp.int32).reshape(1)
    scatter_plan = lambda n: [(a, j, lambda frm, to: _chip(to), lambda frm, to: _chip(frm)) for a in range(n) for j in ICI_RELATIONS]
    scattering = {}

    swap_plan = lambda n: [(a, 1, functools.partial(lambda frm, to, q: 2 * q + to[2], q=q), functools.partial(lambda frm, to, q: q, q=q))
                           for a in range(n) for q in range(NDEV // 2)]
    swapping = {}

    def start_ici(grp, g8, recv):
        pairs = [_pair_sum(f"pair_sum_{grp}{i}", core, g, r, _row_tile(g.shape[1], 1024)) for i, (g, r) in enumerate(zip(g8, recv))]
        lands = [lax.empty(p.shape, p.dtype) for p in pairs]
        ss, rs, srcs, lands, tok = _push_start("scatter_start_" + grp, pairs, lands, scatter_plan(len(pairs)))
        scattering[grp] = (ss, rs, srcs, lands)
        return tok

    def grad_in(h, dproj, n):
        halves = [jnp.stack([2 * q + cc for q in range(NDEV // 2)]).astype(jnp.int32) for cc in (1 - pos[2], pos[2])]
        gw = _gw_part("gw_proj_sibling", h, dproj, n, halves[0], None)
        lands = [lax.empty((NDEV // 2,) + gw.shape[1:], gw.dtype)]
        ss, rs, srcs, lands, tok = _push_start("scatter_swap_in", [gw], lands, swap_plan(1))
        gw = _gw_part("gw_proj_own", h, dproj, n, halves[1], srcs[0], after=tok)
        g8, recv = _push_wait("scatter_swapped_in", ss, rs, [gw], lands, swap_plan(1), tok)
        return start_ici("in", g8, recv)

    def put_g(grp, grads):
        g8 = [g if g.ndim == 3 else to8(g) for g in grads]
        lands = [lax.empty((NDEV // 2,) + g.shape[1:], g.dtype) for g in g8]
        ss, rs, srcs, lands, tok = _push_start("scatter_swap_" + grp, g8, lands, swap_plan(len(g8)))
        swapping[grp] = (ss, rs, srcs, lands)
        return tok

    def sent(grp, after):
        ss, rs, srcs, lands = swapping[grp]
        g8, recv = _push_wait("scatter_swapped_" + grp, ss, rs, srcs, lands, swap_plan(len(srcs)), after)
        return start_ici(grp, g8, recv)

    def reduced(grp, after, names):
        ss, rs, srcs, lands = scattering[grp]
        srcs, lands = _push_wait("scatter_wait_" + grp, ss, rs, srcs, lands, scatter_plan(len(srcs)), after)
        for i, (p, r, nm) in enumerate(zip(srcs, lands, names)):
            tr = nm in ("w_ffn_gate", "w_ffn_up")
            wmv = tuple(a[0].T if tr else a[0] for a in weights[nm])
            out = _chip_sum_adamw(f"chip_sum_{grp}{i}", chip, p, r, _row_tile(p.shape[1]), wmv)
            res[nm] = [(a.T if tr else a).reshape(weights[nm][0].shape) for a in out]

    dx, dmod, small = _local_step(x[0], loss_target[0], mod, _after_tokens(norm_mix_g, tokens), norm_ffn_g,
                                  norm_final_g.reshape(1, d), lb_param, gnorm_g, conv_w_full, project, grad_in, get_w, prefetch, put_g, sent)

    pieces = [*dmod, small["g_mix"], small["g_ffn"], small["g_fin"], small["lb"], small["gnorm"], small["loss"],
              small["conv_w"].reshape(1, -1)]
    widths = [p.shape[1] for p in pieces]
    offs = np.concatenate([[0], np.cumsum(widths)])
    packed = jnp.concatenate(pieces, axis=1)
    gathered, = _all_gather("gather_small", [packed])
    summed = _sum_small(gathered)
    part = lambda i: summed[:, offs[i]:offs[i + 1]]
    g_b_ada = summed[:, :N_MOD * d]
    g_norm_mix, g_norm_ffn, g_norm_fin, g_lb_row, g_gnorm, loss_vec, g_convw_flat = [part(i) for i in range(N_MOD, N_MOD + 7)]
    loss = loss_vec[0, 0]
    dmod_all = gathered[:, 0, :N_MOD * d]
    g_w_ada = _ada_wgrad(c_all, lax.dynamic_slice_in_dim(dmod_all, me * ada_cols, ada_cols, axis=1))
    g_lb = _lb_grad(lb_param, g_lb_row)
    cw_cols = conv_w.shape[2]
    g_conv_w = lax.dynamic_slice_in_dim(g_convw_flat.reshape(conv_w.shape[1], -1), me * cw_cols, cw_cols, axis=1)

    grads = dict(w_ada=g_w_ada, b_ada=g_b_ada, norm_mix_g=g_norm_mix, conv_w=g_conv_w, lb_param=g_lb, gnorm_g=g_gnorm,
                 norm_ffn_g=g_norm_ffn, norm_final_g=g_norm_fin)
    weights = dict(w_ada=(w_ada, m_w_ada, v_w_ada), b_ada=(b_ada, m_b_ada, v_b_ada), norm_mix_g=(norm_mix_g, m_norm_mix_g, v_norm_mix_g),
                   w_in=(w_in, m_w_in, v_w_in), conv_w=(conv_w, m_conv_w, v_conv_w), lb_param=(lb_param, m_lb_param, v_lb_param),
                   gnorm_g=(gnorm_g, m_gnorm_g, v_gnorm_g), w_conv_out=(w_conv_out, m_w_conv_out, v_w_conv_out),
                   w_hgrn_out=(w_hgrn_out, m_w_hgrn_out, v_w_hgrn_out), w_o=(w_o, m_w_o, v_w_o),
                   norm_ffn_g=(norm_ffn_g, m_norm_ffn_g, v_norm_ffn_g), w_ffn_gate=(w_ffn_gate, m_w_ffn_gate, v_w_ffn_gate),
                   w_ffn_up=(w_ffn_up, m_w_ffn_up, v_w_ffn_up), w_ffn_down=(w_ffn_down, m_w_ffn_down, v_w_ffn_down),
                   norm_final_g=(norm_final_g, m_norm_final_g, v_norm_final_g))
    res = {}

    def update(nm):
        w, m, v = weights[nm]
        shape2 = (w.shape[-2], w.shape[-1]) if w.ndim >= 2 else (1, w.shape[0])
        g2 = grads[nm].reshape(shape2)
        dl, mn, vn = _adamw("adamw_" + nm, g2, w.reshape(shape2), m.reshape(shape2), v.reshape(shape2))
        res[nm] = [a.reshape(w.shape) for a in (g2, dl, mn, vn)]

    for nm in list(grads):
        update(nm)
    reduced("ffn", res["w_ada"][1], ("w_ffn_gate", "w_ffn_up", "w_ffn_down"))
    reduced("mix", res["w_ffn_down"][1], ("w_conv_out", "w_hgrn_out", "w_o"))
    reduced("in", res["w_o"][1], ("w_in",))
    outs = [[res[nm][i] for nm in weights] for i in range(4)]
    return (loss, dx.reshape(x.shape), *outs[0], *outs[1], *outs[2], *outs[3])
```

```python
import functools

import numpy as np
import jax
import jax.numpy as jnp
from jax import lax
from jax.experimental import pallas as pl
from jax.experimental.pallas import tpu as pltpu

f32, bf16 = jnp.float32, jnp.bfloat16
SDS = jax.ShapeDtypeStruct

EPS = 1e-6
HEADS, DK, CHUNK = 8, 128, 64
HEADS_PER_STEP = 8
N_MOD = 6
NDEV = 8
ADAM_LR, ADAM_B1, ADAM_B2, ADAM_EPS, ADAM_WD, ADAM_STEP = 0.001, 0.9, 0.999, 1e-08, 0.01, 10
LEVELS = (32, 16, 8, 4, 2, 1)
V7X_VMEM_LIMIT = 56 * 1024 * 1024
HBM = pl.BlockSpec(memory_space=pltpu.HBM)
MESH = pl.DeviceIdType.MESH


def _params(sem):
    return pltpu.CompilerParams(dimension_semantics=sem, vmem_limit_bytes=V7X_VMEM_LIMIT)


def _sigmoid(x):
    return jax.nn.sigmoid(x)


def _dsilu(x, s):
    return s * (1.0 + x * (1.0 - s))


def _mesh_pos():
    x, y, c = lax.axis_index("x"), lax.axis_index("y"), lax.axis_index("c")
    return x, y, c


def _peer(pos, j):
    x, y, c = pos
    return (1 - x if j & 4 else x, 1 - y if j & 2 else y, 1 - c if j & 1 else c)


def _flat(pos):
    return 4 * pos[0] + 2 * pos[1] + pos[2]


def _all_gather(name, arrs, after=()):
    n, ne = len(arrs), len(after)
    out_shapes = [SDS((NDEV,) + a.shape, a.dtype) for a in arrs]

    def body(*refs):
        ins, outs = refs[:n], refs[n + ne:2 * n + ne]
        send_sems, recv_sems, local_sems = refs[2 * n + ne:]
        pos = _mesh_pos()
        me = _flat(pos)

        def copy(a, j, frm, to_pos):
            k = a * (NDEV - 1) + j - 1
            return pltpu.make_async_remote_copy(src_ref=ins[a], dst_ref=outs[a].at[frm], send_sem=send_sems.at[k],
                                                recv_sem=recv_sems.at[k], device_id=to_pos, device_id_type=MESH)

        local = [pltpu.make_async_copy(ins[a], outs[a].at[me], local_sems.at[a]) for a in range(n)]
        for cp in local:
            cp.start()
        sends = [copy(a, j, me, _peer(pos, j)) for j in range(1, NDEV) for a in range(n)]
        for cp in sends:
            cp.start()
        for j in range(1, NDEV):
            for a in range(n):
                copy(a, j, _flat(_peer(pos, j)), pos).wait_recv()
        for cp in sends:
            cp.wait_send()
        for cp in local:
            cp.wait()

    return pl.pallas_call(
        body, name=name, out_shape=out_shapes, in_specs=[HBM] * n + [ANY] * ne, out_specs=[HBM] * n,
        scratch_shapes=[pltpu.SemaphoreType.DMA((n * (NDEV - 1),)), pltpu.SemaphoreType.DMA((n * (NDEV - 1),)),
                        pltpu.SemaphoreType.DMA((n,))],
    )(*arrs, *after)


SEM = pl.BlockSpec(memory_space=pltpu.SEMAPHORE)
ANY = pl.BlockSpec(memory_space=pl.ANY)
EFFECT = pltpu.SideEffectType.DATAFLOW_SIDE_EFFECTING
ICI_RELATIONS = (2, 4, 6)


def _chip(pos):
    return 2 * pos[0] + pos[1]


def _hbm(a):
    return pltpu.with_memory_space_constraint(a, pltpu.HBM)


def _plan_copy(plan_entry, k, pos, frm, to, src_refs, land_refs, send_sems, recv_sems):
    a, _, src_slot, dst_slot = plan_entry
    s = src_refs[a] if src_slot is None else src_refs[a].at[src_slot(frm, to)]
    return pltpu.make_async_remote_copy(src_ref=s, dst_ref=land_refs[a].at[dst_slot(frm, to)], send_sem=send_sems.at[k],
                                        recv_sem=recv_sems.at[k], device_id=to, device_id_type=MESH)


def _push_start(name, srcs, lands, plan, after=None):
    ns, nb, nk = len(srcs), len(srcs) + len(lands), len(plan)
    extra = [] if after is None else [after]

    def body(*refs):
        land_refs = refs[ns:nb]
        src_refs = refs[:ns] if ns else land_refs
        send_sems, recv_sems = refs[nb + len(extra)], refs[nb + len(extra) + 1]
        pos = _mesh_pos()
        for k, e in enumerate(plan):
            _plan_copy(e, k, pos, pos, _peer(pos, e[1]), src_refs, land_refs, send_sems, recv_sems).start()
        refs[-1][...] = jnp.zeros_like(refs[-1])

    outs = pl.pallas_call(
        body, name=name,
        out_shape=[pltpu.SemaphoreType.DMA((nk,)), pltpu.SemaphoreType.DMA((nk,))] + [pltpu.HBM(a.shape, a.dtype) for a in srcs + lands]
        + [SDS((8, 128), f32)],
        in_specs=[HBM] * nb + [ANY] * len(extra), out_specs=[SEM, SEM] + [HBM] * nb + [pl.BlockSpec(memory_space=pltpu.VMEM)],
        input_output_aliases={i: 2 + i for i in range(nb)},
        compiler_params=pltpu.CompilerParams(has_side_effects=EFFECT),
    )(*[_hbm(a) for a in srcs + lands], *extra)
    return outs[0], outs[1], list(outs[2:2 + ns]), list(outs[2 + ns:2 + nb]), outs[-1]


def _push_wait(name, send_sems, recv_sems, srcs, lands, plan, after):
    ns, nb = len(srcs), len(srcs) + len(lands)

    def body(*refs):
        land_refs = refs[ns:nb]
        src_refs = refs[:ns] if ns else land_refs
        ssem, rsem = refs[nb], refs[nb + 1]
        pos = _mesh_pos()
        for k, e in enumerate(plan):
            peer = _peer(pos, e[1])
            _plan_copy(e, k, pos, pos, peer, src_refs, land_refs, ssem, rsem).wait_send()
            _plan_copy(e, k, pos, peer, pos, src_refs, land_refs, ssem, rsem).wait_recv()

    outs = pl.pallas_call(
        body, name=name, out_shape=[pltpu.HBM(a.shape, a.dtype) for a in srcs + lands],
        in_specs=[HBM] * nb + [SEM, SEM, ANY], out_specs=[HBM] * nb,
        input_output_aliases={i: i for i in range(nb)},
        compiler_params=pltpu.CompilerParams(has_side_effects=EFFECT),
    )(*srcs, *lands, send_sems, recv_sems, after)
    return list(outs[:ns]), list(outs[ns:])


def _after_tokens(small, tokens):
    for t in tokens:
        if t is not None:
            small = small + t[0:1, 0:1].reshape((1,) * small.ndim)
    return small


def _place_shard(name, me, shard, after=None):
    r, c = shard.shape
    tr = _row_tile(r)
    extra = [] if after is None else [after]

    def body(me_ref, s_ref, *rest):
        rest[-2][...] = s_ref[...].astype(bf16)
        rest[-1][...] = jnp.zeros_like(rest[-1])

    return pl.pallas_call(
        body, name=name, out_shape=[SDS((NDEV, r, c), bf16), SDS((8, 128), f32)],
        grid_spec=pltpu.PrefetchScalarGridSpec(
            num_scalar_prefetch=1, grid=(r // tr,), in_specs=[pl.BlockSpec((tr, c), lambda i, me_ref: (i, 0))] + [ANY] * len(extra),
            out_specs=[pl.BlockSpec((None, tr, c), lambda i, me_ref: (me_ref[0], i, 0)),
                       pl.BlockSpec((8, 128), lambda i, me_ref: (0, 0))]),
        compiler_params=_params(("arbitrary",)),
    )(me, shard, *extra)


def _row_tile(r, most=256):
    return max(t for t in range(16, most + 1, 16) if r % t == 0)


def _pair_sum(name, core, grad, recv, tr):
    _, r, c = grad.shape
    nchip = NDEV // 2

    def body(core_ref, g_ref, r_ref, o_ref):
        o_ref[...] = (g_ref[...].astype(f32) + r_ref[...].astype(f32)).astype(o_ref.dtype)

    return pl.pallas_call(
        body, name=name, out_shape=SDS((nchip, r, c), grad.dtype),
        grid_spec=pltpu.PrefetchScalarGridSpec(
            num_scalar_prefetch=1, grid=(nchip, r // tr),
            in_specs=[pl.BlockSpec((None, tr, c), lambda q, i, core_ref: (2 * q + core_ref[0], i, 0)),
                      pl.BlockSpec((None, tr, c), lambda q, i, core_ref: (q, i, 0))],
            out_specs=pl.BlockSpec((None, tr, c), lambda q, i, core_ref: (q, i, 0))),
        compiler_params=_params(("parallel", "parallel")),
    )(core, grad, recv)


def _adamw_math(g, w, m, v):
    mn = ADAM_B1 * m + (1.0 - ADAM_B1) * g
    vn = ADAM_B2 * v + (1.0 - ADAM_B2) * jnp.square(g)
    m_hat = mn / (1.0 - ADAM_B1 ** ADAM_STEP)
    v_hat = vn / (1.0 - ADAM_B2 ** ADAM_STEP)
    return -ADAM_LR * (m_hat / (jnp.sqrt(v_hat) + ADAM_EPS) + ADAM_WD * w), mn, vn


def _chip_sum_adamw(name, chip, pairs, recv, tr, wmv):
    nchip, r, c = pairs.shape

    def body(chip_ref, p_ref, r_ref, w_ref, m_ref, v_ref, g_ref, d_ref, mo_ref, vo_ref):
        mine = chip_ref[0]
        own = p_ref[...].astype(f32)
        acc = jnp.zeros((tr, c), f32)
        for q in range(nchip):
            acc = acc + jnp.where(mine == q, own, r_ref[q].astype(f32))
        g_ref[...] = acc
        d_ref[...], mo_ref[...], vo_ref[...] = _adamw_math(acc, w_ref[...], m_ref[...], v_ref[...])

    blk = pl.BlockSpec((tr, c), lambda i, chip_ref: (i, 0))
    return pl.pallas_call(
        body, name=name, out_shape=[SDS((r, c), f32)] * 4,
        grid_spec=pltpu.PrefetchScalarGridSpec(
            num_scalar_prefetch=1, grid=(r // tr,),
            in_specs=[pl.BlockSpec((None, tr, c), lambda i, chip_ref: (chip_ref[0], i, 0)),
                      pl.BlockSpec((nchip, tr, c), lambda i, chip_ref: (0, i, 0))] + [blk] * 3,
            out_specs=[blk] * 4),
        compiler_params=_params(("parallel",)),
    )(chip, pairs, recv, *wmv)


def _matmul(name, a, b, out_shape, out_dtype, grid, a_spec, b_spec, o_spec, dims, after=None):
    ksteps = grid[2]
    acc_shape = tuple(d for d in o_spec.block_shape if d is not None)
    extra = [] if after is None else [after]

    def body(a_ref, b_ref, *rest):
        o_ref, acc = rest[len(extra)], rest[len(extra) + 1:]
        prod = lax.dot_general(a_ref[...], b_ref[...], (dims, ((), ())), preferred_element_type=f32)
        if ksteps == 1:
            o_ref[...] = prod.astype(o_ref.dtype)
        else:
            k = pl.program_id(2)

            @pl.when(k == 0)
            def _():
                acc[0][...] = prod

            @pl.when(k > 0)
            def _():
                acc[0][...] += prod

            @pl.when(k == ksteps - 1)
            def _():
                o_ref[...] = acc[0][...].astype(o_ref.dtype)

    return pl.pallas_call(
        body, name=name, grid=grid, in_specs=[a_spec, b_spec] + [ANY] * len(extra), out_specs=o_spec,
        out_shape=SDS(out_shape, out_dtype), scratch_shapes=[] if ksteps == 1 else [pltpu.VMEM(acc_shape, f32)],
        compiler_params=_params(("parallel", "parallel", "arbitrary")),
    )(a, b, *extra)


NN, NT, TN = ((1,), (0,)), ((1,), (1,)), ((0,), (0,))


def _concat_columns(name, pieces, tm):
    s = pieces[0].shape[0]
    widths = [p.shape[1] for p in pieces]
    offs = [sum(widths[:i]) for i in range(len(widths))]

    def body(*refs):
        o_ref = refs[-1]
        for r, o, w in zip(refs[:-1], offs, widths):
            o_ref[:, o:o + w] = r[...]

    return pl.pallas_call(
        body, name=name, grid=(s // tm,), in_specs=[pl.BlockSpec((tm, w), lambda i: (i, 0)) for w in widths],
        out_specs=pl.BlockSpec((tm, sum(widths)), lambda i: (i, 0)), out_shape=SDS((s, sum(widths)), pieces[0].dtype),
        compiler_params=_params(("parallel",)),
    )(*pieces)


def _d_proj_in(dproj, w_sh, tm, after):
    s = dproj.shape[0]
    nsh, d, n = w_sh.shape
    ksteps = nsh // 2
    extra = [] if after is None else [after]

    def body(a_ref, b_ref, *rest):
        o_ref, acc = rest[len(extra)], rest[len(extra) + 1]
        k = pl.program_id(1)
        a = a_ref[...]
        prod = lax.dot_general(a[:, :n], b_ref[0], (NT, ((), ())), preferred_element_type=f32)
        prod += lax.dot_general(a[:, n:], b_ref[1], (NT, ((), ())), preferred_element_type=f32)

        @pl.when(k == 0)
        def _():
            acc[...] = prod

        @pl.when(k > 0)
        def _():
            acc[...] += prod

        @pl.when(k == ksteps - 1)
        def _():
            o_ref[...] = acc[...].astype(o_ref.dtype)

    return pl.pallas_call(
        body, name="d_proj_in", grid=(s // tm, ksteps),
        in_specs=[pl.BlockSpec((tm, 2 * n), lambda i, k: (i, k)), pl.BlockSpec((2, d, n), lambda i, k: (k, 0, 0))] + [ANY] * len(extra),
        out_specs=pl.BlockSpec((tm, d), lambda i, k: (i, 0)), out_shape=SDS((s, d), bf16),
        scratch_shapes=[pltpu.VMEM((tm, d), f32)], compiler_params=_params(("parallel", "arbitrary")),
    )(dproj, w_sh, *extra)


def _gw_part(name, h, dproj, n, ids, prev, after=None):
    s, d = h.shape
    kb = 512
    extra = ([] if prev is None else [prev]) + ([] if after is None else [after])

    def body(ids_ref, a_ref, b_ref, *rest):
        rest[-1][...] = lax.dot_general(a_ref[...], b_ref[...], (TN, ((), ())), preferred_element_type=f32).astype(bf16)

    return pl.pallas_call(
        body, name=name, out_shape=SDS((dproj.shape[1] // n, d, n), bf16),
        grid_spec=pltpu.PrefetchScalarGridSpec(
            num_scalar_prefetch=1, grid=(ids.shape[0], d // kb),
            in_specs=[pl.BlockSpec((s, kb), lambda j, i, ids_ref: (0, i)),
                      pl.BlockSpec((s, n), lambda j, i, ids_ref: (0, ids_ref[j]))] + [ANY] * len(extra),
            out_specs=pl.BlockSpec((None, kb, n), lambda j, i, ids_ref: (ids_ref[j], i, 0))),
        input_output_aliases={} if prev is None else {3: 0},
        compiler_params=_params(("parallel", "parallel")),
    )(ids, h, dproj, *extra)


def _proj_part(name, h, w_sh, ids, prev, tm, after=None):
    s, d = h.shape
    nsh, _, n = w_sh.shape
    extra = ([] if prev is None else [prev]) + ([] if after is None else [after])

    def body(ids_ref, a_ref, b_ref, *rest):
        rest[-1][...] = jnp.dot(a_ref[...], b_ref[...], preferred_element_type=f32)

    return pl.pallas_call(
        body, name=name, out_shape=SDS((s, nsh * n), f32),
        grid_spec=pltpu.PrefetchScalarGridSpec(
            num_scalar_prefetch=1, grid=(ids.shape[0], s // tm),
            in_specs=[pl.BlockSpec((tm, d), lambda j, i, ids_ref: (i, 0)),
                      pl.BlockSpec((None, d, n), lambda j, i, ids_ref: (ids_ref[j], 0, 0))] + [ANY] * len(extra),
            out_specs=pl.BlockSpec((tm, n), lambda j, i, ids_ref: (i, ids_ref[j]))),
        input_output_aliases={} if prev is None else {3: 0},
        compiler_params=_params(("parallel", "parallel")),
    )(ids, h, w_sh, *extra)


def _rowwise(name, fn, n_rows, tm, rows, fulls, row_outs, acc_outs=(), ncol=1):
    assert ncol == 1 or not acc_outs
    nr, nf, no, na = len(rows), len(fulls), len(row_outs), len(acc_outs)
    in_specs = [pl.BlockSpec((tm, w), functools.partial(lambda i, j, cb: (i, cb + j), cb=cb)) for (_, w, cb) in rows]
    in_specs += [pl.BlockSpec(a.shape, functools.partial(lambda i, j, nd: (0,) * nd, nd=a.ndim)) for a in fulls]
    out_shape = [SDS((n_rows, w), dt) for (w, dt) in row_outs] + [SDS(s, f32) for s in acc_outs]
    out_specs = [pl.BlockSpec((tm, w // ncol), lambda i, j: (i, j)) for (w, _) in row_outs]
    out_specs += [pl.BlockSpec(s, functools.partial(lambda i, j, nd: (0,) * nd, nd=len(s))) for s in acc_outs]

    def body(*refs):
        if na:
            @pl.when(pl.program_id(0) == 0)
            def _():
                for r in refs[nr + nf + no:]:
                    r[...] = jnp.zeros(r.shape, r.dtype)
        fn(refs[:nr], refs[nr:nr + nf], refs[nr + nf:nr + nf + no], refs[nr + nf + no:])

    return pl.pallas_call(
        body, name=name, grid=(n_rows // tm, ncol), in_specs=in_specs, out_specs=out_specs, out_shape=out_shape,
        compiler_params=_params(("arbitrary" if na else "parallel", "arbitrary" if na else "parallel")),
    )(*[r[0] for r in rows], *fulls)


def _rms(x):
    r = lax.rsqrt(jnp.mean(x * x, axis=-1, keepdims=True) + EPS)
    return r, x * r


def _colsum(v):
    return jnp.sum(v, axis=0, keepdims=True)


def _shift_down(u, row, k):
    return jnp.where(row >= k, pltpu.roll(u, k, 0), 0.0)


def _shift_up(u, row, k):
    n = u.shape[0]
    return jnp.where(row < n - k, pltpu.roll(u, n - k, 0), 0.0)


def _conv_fwd(proj, conv_w, s, dc):
    nb = dc // 128

    def body(ab_ref, ac_ref, ax_ref, w_ref, z_ref):
        u = ac_ref[...] * ax_ref[...]
        row = lax.broadcasted_iota(jnp.int32, u.shape, 0)
        w = w_ref[...]
        cv = w[0:1] * _shift_down(u, row, 2) + w[1:2] * _shift_down(u, row, 1) + w[2:3] * u
        z_ref[...] = (ab_ref[...] * cv).astype(z_ref.dtype)

    col = lambda off: pl.BlockSpec((s, 128), functools.partial(lambda j, off: (0, off + j), off=off))
    return pl.pallas_call(
        body, name="conv_fwd", grid=(nb,), in_specs=[col(0), col(nb), col(2 * nb), pl.BlockSpec((3, 128), lambda j: (0, j))],
        out_specs=pl.BlockSpec((s, 128), lambda j: (0, j)), out_shape=SDS((s, dc), bf16), compiler_params=_params(("parallel",)),
    )(proj, proj, proj, conv_w)


def _conv_bwd(proj, conv_w, dz, s, dc):
    nb = dc // 128

    def body(ab_ref, ac_ref, ax_ref, w_ref, dz_ref, dab_ref, dac_ref, dax_ref, dw_ref):
        ab, ac, ax, dzv = ab_ref[...], ac_ref[...], ax_ref[...], dz_ref[...].astype(f32)
        u = ac * ax
        row = lax.broadcasted_iota(jnp.int32, u.shape, 0)
        w = w_ref[...]
        u1, u2 = _shift_down(u, row, 1), _shift_down(u, row, 2)
        cv = w[0:1] * u2 + w[1:2] * u1 + w[2:3] * u
        dcv = dzv * ab
        dab_ref[...] = (dzv * cv).astype(dab_ref.dtype)
        du = w[2:3] * dcv + w[1:2] * _shift_up(dcv, row, 1) + w[0:1] * _shift_up(dcv, row, 2)
        dac_ref[...] = (du * ax).astype(dac_ref.dtype)
        dax_ref[...] = (du * ac).astype(dax_ref.dtype)
        dw_ref[0:1, :] = _colsum(dcv * u2)
        dw_ref[1:2, :] = _colsum(dcv * u1)
        dw_ref[2:3, :] = _colsum(dcv * u)

    col = lambda off: pl.BlockSpec((s, 128), functools.partial(lambda j, off: (0, off + j), off=off))
    blk = pl.BlockSpec((s, 128), lambda j: (0, j))
    return pl.pallas_call(
        body, name="conv_bwd", grid=(nb,),
        in_specs=[col(0), col(nb), col(2 * nb), pl.BlockSpec((3, 128), lambda j: (0, j)), blk],
        out_specs=[blk, blk, blk, pl.BlockSpec((3, 128), lambda j: (0, j))],
        out_shape=[SDS((s, dc), bf16)] * 3 + [SDS((3, dc), f32)], compiler_params=_params(("parallel",)),
    )(proj, proj, proj, conv_w, dz)


def _level_masks():
    t = np.arange(CHUNK)[:, None]
    s = np.arange(CHUNK)[None, :]
    m = np.stack([((t & h) != 0) & ((s & h) == 0) & (t // (2 * h) == s // (2 * h)) for h in LEVELS]).astype(np.float32)
    m = np.concatenate([m, (t >= s).astype(np.float32)[None]])
    return jnp.asarray(m), jnp.asarray(m.transpose(0, 2, 1))


def _running_sum(tri, x):
    return jnp.dot(tri, x, preferred_element_type=f32, precision=lax.Precision.HIGHEST)


def _cumsum_rows(x, row):
    for sh in (1, 2, 4, 8, 16, 32):
        x = x + jnp.where(row >= sh, pltpu.roll(x, sh, 0), 0.0)
    return x


def _rev_cumsum_rows(x, row):
    n = x.shape[0]
    for sh in (1, 2, 4, 8, 16, 32):
        x = x + jnp.where(row < n - sh, pltpu.roll(x, n - sh, 0), 0.0)
    return x


def _chunk_terms(qp, fl, lb, tril):
    row = lax.broadcasted_iota(jnp.int32, qp.shape, 0)
    sig = _sigmoid(fl)
    f = lb + (1.0 - lb) * sig
    k = 1.0 - f
    sq = _sigmoid(qp)
    qh = qp * sq
    b = _running_sum(tril, jnp.log(f))
    sub = lax.broadcasted_iota(jnp.int32, (CHUNK // 8, 8, DK), 1)
    b8 = b.reshape(CHUNK // 8, 8, DK)
    us, exs, ups = [], [], []
    for m in LEVELS:
        sb = 2 * m
        if sb >= 8:
            b3 = b.reshape(CHUNK // sb, sb, DK)
            bref = jnp.broadcast_to(b3[:, m - 1:m, :], b3.shape).reshape(CHUNK, DK)
        else:
            bref8 = None
            for j in range(8 // sb):
                cand = jnp.broadcast_to(b8[:, j * sb + m - 1:j * sb + m, :], b8.shape)
                bref8 = cand if bref8 is None else jnp.where(sub >= j * sb, cand, bref8)
            bref = bref8.reshape(CHUNK, DK)
        up = (row & m) != 0
        ex = jnp.exp(jnp.where(up, b - bref, bref - b))
        us.append((jnp.where(up, qh, k) * ex).astype(bf16))
        exs.append(ex)
        ups.append(up)
    blast = b[CHUNK - 1:CHUNK, :]
    eb, ebl = jnp.exp(b), jnp.exp(blast - b)
    return dict(sig=sig, f=f, k=k, sq=sq, qh=qh, u=jnp.stack(us), ex=exs, up=ups, eb=eb, ebl=ebl, qt=qh * eb, kt=k * ebl,
                el=jnp.exp(blast), row=row)


def _scores(t, mask):
    pl_ = jnp.einsum("ltk,lsk->lts", t["u"], t["u"], preferred_element_type=f32)
    p = jnp.sum(pl_ * mask, axis=0)
    r = lax.broadcasted_iota(jnp.int32, (CHUNK, CHUNK), 0)
    c = lax.broadcasted_iota(jnp.int32, (CHUNK, CHUNK), 1)
    diag = jnp.sum(t["qh"] * t["k"], axis=-1, keepdims=True)
    return p + jnp.where(r == c, diag, 0.0)


def _hgrn_fwd(proj, lb_param, gnorm, s, dv_total, tb):
    nchunk = tb // CHUNK
    masks, _ = _level_masks()
    q0, f0, v0, g0 = 3 * HEADS, 4 * HEADS, 5 * HEADS, 6 * HEADS

    def body(q_ref, f_ref, v_ref, g_ref, lb_ref, gn_ref, mask_ref, og_ref, o_ref, st_ref, state):
        @pl.when(pl.program_id(1) == 0)
        def _():
            state[...] = jnp.zeros_like(state)

        lbp = lb_ref[...]
        lb_all = _sigmoid(lbp[0:1, :] - lbp[1:2, :])
        mask, tril = mask_ref[0:len(LEVELS)], mask_ref[len(LEVELS)]
        for i, hh in [(i, hh) for i in range(nchunk) for hh in range(HEADS_PER_STEP)]:
            rs, cs = pl.ds(i * CHUNK, CHUNK), pl.ds(hh * DK, DK)
            t = _chunk_terms(q_ref[rs, cs], f_ref[rs, cs], lb_all[:, hh * DK:(hh + 1) * DK], tril)
            v = v_ref[rs, cs]
            vb = v.astype(bf16)
            st = state[hh]
            st_ref[i, hh] = st
            p = _scores(t, mask)
            o = jnp.dot(p.astype(bf16), vb, preferred_element_type=f32)
            o += lax.dot_general(t["qt"].astype(bf16), st.astype(bf16), (NT, ((), ())), preferred_element_type=f32)
            state[hh] = st * t["el"] + lax.dot_general(vb, t["kt"].astype(bf16), (TN, ((), ())), preferred_element_type=f32)
            o_ref[rs, cs] = o
            r, oh = _rms(o)
            g = g_ref[rs, cs]
            og_ref[rs, cs] = (oh * gn_ref[...] * (g * _sigmoid(g))).astype(og_ref.dtype)

    hp, wide = HEADS_PER_STEP, HEADS_PER_STEP * DK
    col = lambda off: pl.BlockSpec((tb, wide), functools.partial(lambda h, t, off: (t, off + h), off=off // hp))
    blk = pl.BlockSpec((tb, wide), lambda h, t: (t, h))
    return pl.pallas_call(
        body, name="hgrn_fwd", grid=(HEADS // hp, s // tb),
        in_specs=[col(q0), col(f0), col(v0), col(g0), pl.BlockSpec((2, wide), lambda h, t: (0, h)),
                  pl.BlockSpec((1, DK), lambda h, t: (0, 0)), pl.BlockSpec(masks.shape, lambda h, t: (0, 0, 0))],
        out_specs=[blk, blk, pl.BlockSpec((nchunk, hp, DK, DK), lambda h, t: (t, h, 0, 0))],
        out_shape=[SDS((s, dv_total), bf16), SDS((s, dv_total), f32), SDS((s // CHUNK, HEADS, DK, DK), f32)],
        scratch_shapes=[pltpu.VMEM((hp, DK, DK), f32)], compiler_params=_params(("parallel", "arbitrary")),
    )(proj, proj, proj, proj, lb_param, gnorm, masks)


def _hgrn_bwd(proj, lb_param, gnorm, o_saved, states, dog, s, dv_total, tb):
    nchunk = tb // CHUNK
    nt = s // tb
    nc_total = s // CHUNK
    masks, masks_t = _level_masks()
    q0, f0, v0, g0 = 3 * HEADS, 4 * HEADS, 5 * HEADS, 6 * HEADS

    def body(q_ref, f_ref, v_ref, g_ref, lb_ref, gn_ref, mask_ref, maskt_ref, o_ref, dog_ref, st_ref, stn_ref,
             dq_ref, df_ref, dv_ref, dg_ref, dlb_ref, dgn_ref, gstate):
        h_id, t_id = pl.program_id(0), pl.program_id(1)

        @pl.when(t_id == 0)
        def _():
            gstate[...] = jnp.zeros_like(gstate)
            dlb_ref[...] = jnp.zeros_like(dlb_ref)

        @pl.when((t_id == 0) & (h_id == 0))
        def _():
            dgn_ref[...] = jnp.zeros_like(dgn_ref)

        lbp = lb_ref[...]
        lb_all = _sigmoid(lbp[0:1, :] - lbp[1:2, :])
        mask, tril = mask_ref[0:len(LEVELS)], mask_ref[len(LEVELS)]
        maskt, triu = maskt_ref[0:len(LEVELS)], maskt_ref[len(LEVELS)]
        gn = gn_ref[...]
        for i, hh in [(i, hh) for i in reversed(range(nchunk)) for hh in range(HEADS_PER_STEP)]:
            rs, cs = pl.ds(i * CHUNK, CHUNK), pl.ds(hh * DK, DK)
            lb = lb_all[:, hh * DK:(hh + 1) * DK]
            qp, fl, v, g = q_ref[rs, cs], f_ref[rs, cs], v_ref[rs, cs], g_ref[rs, cs]
            t = _chunk_terms(qp, fl, lb, tril)
            vb = v.astype(bf16)
            st0 = st_ref[i, hh]
            st1 = st_ref[i + 1, hh] if i + 1 < nchunk else stn_ref[0, hh]
            gt = gstate[hh]
            o = o_ref[rs, cs]
            r, oh = _rms(o)
            sg = _sigmoid(g)
            dog_v = dog_ref[rs, cs].astype(f32)
            dg_ref[rs, cs] = (dog_v * (oh * gn) * _dsilu(g, sg)).astype(dg_ref.dtype)
            don = dog_v * (g * sg)
            dgn_ref[...] += _colsum(don * oh)
            doh = don * gn
            do = r * (doh - oh * jnp.mean(doh * oh, axis=-1, keepdims=True))
            dob = do.astype(bf16)
            d = lax.dot_general(dob, vb, (NT, ((), ())), preferred_element_type=f32)
            dt = lax.dot_general(vb, dob, (NT, ((), ())), preferred_element_type=f32)
            z = (mask * d[None] + maskt * dt[None]).astype(bf16)
            rr = jnp.einsum("lts,lsk->ltk", z, t["u"], preferred_element_type=f32)
            dq = jnp.zeros((CHUNK, DK), f32)
            dk = jnp.zeros((CHUNK, DK), f32)
            qdk = jnp.zeros((CHUNK, DK), f32)
            for li in range(len(LEVELS)):
                du = t["ex"][li] * rr[li]
                dq += jnp.where(t["up"][li], du, 0.0)
                dk += jnp.where(t["up"][li], 0.0, du)
                e = t["u"][li].astype(f32) * rr[li]
                qdk += jnp.where(t["up"][li], e, -e)
            dd = jnp.sum(do * v, axis=-1, keepdims=True)
            dq += dd * t["k"]
            dk += dd * t["qh"]
            gtb = gt.astype(bf16)
            ktb, qtb = t["kt"].astype(bf16), t["qt"].astype(bf16)
            dq_in = jnp.dot(dob, st0.astype(bf16), preferred_element_type=f32)
            dk_in = jnp.dot(vb, gtb, preferred_element_type=f32)
            dq += t["eb"] * dq_in
            dk += t["ebl"] * dk_in
            qdk += qtb.astype(f32) * dq_in - ktb.astype(f32) * dk_in
            p = _scores(t, mask)
            dvv = lax.dot_general(p.astype(bf16), dob, (TN, ((), ())), preferred_element_type=f32)
            dvv += lax.dot_general(ktb, gtb, (NT, ((), ())), preferred_element_type=f32)
            dv_ref[rs, cs] = dvv.astype(dv_ref.dtype)
            a_end = _colsum(gtb.astype(f32) * st1)
            dlf = _running_sum(triu, qdk) + a_end
            dfv = dlf / t["f"] - dk
            df_ref[rs, cs] = (dfv * (1.0 - lb) * t["sig"] * (1.0 - t["sig"])).astype(df_ref.dtype)
            dlb_ref[:, cs] += _colsum(dfv * (1.0 - t["sig"]))
            dq_ref[rs, cs] = (dq * _dsilu(qp, t["sq"])).astype(dq_ref.dtype)
            gstate[hh] = gt * t["el"] + lax.dot_general(dob, qtb, (TN, ((), ())), preferred_element_type=f32)

    hp, wide = HEADS_PER_STEP, HEADS_PER_STEP * DK
    rev = lambda t: nt - 1 - t
    col = lambda off: pl.BlockSpec((tb, wide), functools.partial(lambda h, t, off: (rev(t), off + h), off=off // hp))
    blk = pl.BlockSpec((tb, wide), lambda h, t: (rev(t), h))
    nxt = lambda h, t: (jnp.minimum((rev(t) + 1) * nchunk, nc_total - 1), h, 0, 0)
    return pl.pallas_call(
        body, name="hgrn_bwd", grid=(HEADS // hp, nt),
        in_specs=[col(q0), col(f0), col(v0), col(g0), pl.BlockSpec((2, wide), lambda h, t: (0, h)),
                  pl.BlockSpec((1, DK), lambda h, t: (0, 0)), pl.BlockSpec(masks.shape, lambda h, t: (0, 0, 0)),
                  pl.BlockSpec(masks.shape, lambda h, t: (0, 0, 0)), blk, blk,
                  pl.BlockSpec((nchunk, hp, DK, DK), lambda h, t: (rev(t), h, 0, 0)),
                  pl.BlockSpec((1, hp, DK, DK), nxt)],
        out_specs=[blk, blk, blk, blk, pl.BlockSpec((1, wide), lambda h, t: (0, h)), pl.BlockSpec((1, DK), lambda h, t: (0, 0))],
        out_shape=[SDS((s, dv_total), bf16)] * 4 + [SDS((1, HEADS * DK), f32), SDS((1, DK), f32)],
        scratch_shapes=[pltpu.VMEM((hp, DK, DK), f32)], compiler_params=_params(("arbitrary", "arbitrary")),
    )(proj, proj, proj, proj, lb_param, gnorm, masks, masks_t, o_saved, dog, states, states)


def _adamw(name, g, w, m, v):
    r, c = w.shape
    tr = r
    for cand in (256, 128, 64, 32, 16, 8):
        if r % cand == 0 and r > cand:
            tr = cand
            break

    def body(g_ref, w_ref, m_ref, v_ref, d_ref, mo_ref, vo_ref):
        d_ref[...], mo_ref[...], vo_ref[...] = _adamw_math(g_ref[...], w_ref[...], m_ref[...], v_ref[...])

    blk = pl.BlockSpec((tr, c), lambda i: (i, 0))
    return pl.pallas_call(
        body, name=name, grid=(r // tr,), in_specs=[blk] * 4, out_specs=[blk] * 3, out_shape=[SDS((r, c), f32)] * 3,
        compiler_params=_params(("parallel",)),
    )(g, w, m, v)


def _ffn_in(h2, w_gt, w_ut, tm, ffb):
    s, d = h2.shape
    dff = w_gt.shape[0]

    def body(a_ref, wg_ref, wu_ref, dg_ref, du_ref, act_ref):
        a = a_ref[...]
        g = lax.dot_general(a, wg_ref[...], (NT, ((), ())), preferred_element_type=f32)
        u = lax.dot_general(a, wu_ref[...], (NT, ((), ())), preferred_element_type=f32)
        sg = _sigmoid(g)
        silu = g * sg
        dg_ref[...] = (u * _dsilu(g, sg)).astype(bf16)
        du_ref[...] = silu.astype(bf16)
        act_ref[...] = (silu * u).astype(bf16)

    w_spec = pl.BlockSpec((ffb, d), lambda j, i: (j, 0))
    o_spec = pl.BlockSpec((tm, ffb), lambda j, i: (i, j))
    return pl.pallas_call(
        body, name="ffn_in", grid=(dff // ffb, s // tm), in_specs=[pl.BlockSpec((tm, d), lambda j, i: (i, 0)), w_spec, w_spec],
        out_specs=[o_spec] * 3, out_shape=[SDS((s, dff), bf16)] * 3, compiler_params=_params(("parallel", "parallel")),
    )(h2, w_gt, w_ut)


def _ffn_down_bwd(dff_out, w_d, act_dg, act_du, tm, ffb):
    s, d = dff_out.shape
    dff = w_d.shape[0]

    def body(a_ref, w_ref, fg_ref, fu_ref, dg_ref, du_ref):
        da = lax.dot_general(a_ref[...], w_ref[...], (NT, ((), ())), preferred_element_type=f32)
        dg_ref[...] = (da * fg_ref[...].astype(f32)).astype(bf16)
        du_ref[...] = (da * fu_ref[...].astype(f32)).astype(bf16)

    t_spec = pl.BlockSpec((tm, ffb), lambda j, i: (i, j))
    return pl.pallas_call(
        body, name="d_ffn_down_in", grid=(dff // ffb, s // tm),
        in_specs=[pl.BlockSpec((tm, d), lambda j, i: (i, 0)), pl.BlockSpec((ffb, d), lambda j, i: (j, 0)), t_spec, t_spec],
        out_specs=[t_spec] * 2, out_shape=[SDS((s, dff), bf16)] * 2, compiler_params=_params(("parallel", "parallel")),
    )(dff_out, w_d, act_dg, act_du)


def _branch_merge(z_a, og, w_co, w_ho, proj, tm, gate_a0, gate_b0):
    s, dc = z_a.shape
    nsh, _, n = w_co.shape

    def body(za_ref, og_ref, wa_ref, wb_ref, ga_ref, gb_ref, m_ref, sa_ref, sb_ref, fa_ref, fb_ref):
        ya = jnp.dot(za_ref[...], wa_ref[...], preferred_element_type=f32)
        yb = jnp.dot(og_ref[...], wb_ref[...], preferred_element_type=f32)
        sa, sb_ = _sigmoid(ga_ref[...]), _sigmoid(gb_ref[...])
        m_ref[...] = (sa * ya + sb_ * yb).astype(bf16)
        sa_ref[...] = sa.astype(bf16)
        sb_ref[...] = sb_.astype(bf16)
        fa_ref[...] = (ya * sa * (1.0 - sa)).astype(bf16)
        fb_ref[...] = (yb * sb_ * (1.0 - sb_)).astype(bf16)

    a_spec = pl.BlockSpec((tm, dc), lambda i, j: (i, 0))
    w_spec = pl.BlockSpec((None, dc, n), lambda i, j: (j, 0, 0))
    gate = lambda c0: pl.BlockSpec((tm, n), functools.partial(lambda i, j, cb: (i, cb + j), cb=c0 // n))
    o_spec = pl.BlockSpec((tm, n), lambda i, j: (i, j))
    return pl.pallas_call(
        body, name="branch_merge", grid=(s // tm, nsh), in_specs=[a_spec, a_spec, w_spec, w_spec, gate(gate_a0), gate(gate_b0)],
        out_specs=[o_spec] * 5, out_shape=[SDS((s, nsh * n), bf16)] * 5, compiler_params=_params(("parallel", "parallel")),
    )(z_a, og, w_co, w_ho, proj, proj)


def _d_branch_merge(dmo, w_o, factors, tm):
    s, d = dmo.shape
    n = d // 2

    def body(a_ref, w_ref, sa_ref, sb_ref, fa_ref, fb_ref, dya_ref, dyb_ref, dga_ref, dgb_ref):
        dm = lax.dot_general(a_ref[...], w_ref[...], (NT, ((), ())), preferred_element_type=f32)
        for f_ref, o_ref in ((sa_ref, dya_ref), (sb_ref, dyb_ref), (fa_ref, dga_ref), (fb_ref, dgb_ref)):
            o_ref[...] = (dm * f_ref[...].astype(f32)).astype(bf16)

    t_spec = pl.BlockSpec((tm, n), lambda j, i: (i, j))
    return pl.pallas_call(
        body, name="d_branch_merge", grid=(d // n, s // tm),
        in_specs=[pl.BlockSpec((tm, d), lambda j, i: (i, 0)), pl.BlockSpec((n, d), lambda j, i: (j, 0))] + [t_spec] * 4,
        out_specs=[t_spec] * 4, out_shape=[SDS((s, d), bf16)] * 4, compiler_params=_params(("parallel", "parallel")),
    )(dmo, w_o, *factors)


def _local_step(x, tgt, mod, g_mix, g_ffn, g_fin, lb_param, gnorm, conv_w, project, grad_in, get_w, prefetch, put_g, sent):
    s, d = x.shape
    dc = d // 2
    tm = min(512, s)
    tm2 = min(1024, s)
    te = min(256, s)
    tb = min(128, s)
    mt = s // tm
    nb = 512
    sh_m, sc_m, gt_m, sh_f, sc_f, gt_f = [mod[i] for i in range(N_MOD)]
    dh2 = d // 2

    def e1(rows, fulls, outs, accs):
        xv = rows[0][...]
        g, sc, sh = [r[...] for r in fulls]
        _, xh = _rms(xv)
        outs[0][...] = (xh * g * (1.0 + sc) + sh).astype(bf16)

    h, = _rowwise("prenorm_mix", e1, s, te, [(x, d, 0)], [g_mix, sc_m, sh_m], [(d, bf16)])
    proj, w_in = project(h, tm2)
    nsh, _, win_sh = w_in.shape
    z_a = _conv_fwd(proj, _after_tokens(conv_w, [prefetch("mix", proj)]), s, dc)
    og, o_saved, states = _hgrn_fwd(proj, lb_param, gnorm, s, dc, tb)
    w_co, w_ho, w_o = get_w("mix", og)

    gate_a0, gate_b0 = 7 * dh2, 9 * dh2
    merged, *merge_factors = _branch_merge(z_a, og, w_co, w_ho, proj, tm2, gate_a0, gate_b0)
    mo = _matmul("mix_out", merged, w_o, (s, d), bf16, (2, s // tm2, 1), pl.BlockSpec((tm2, d), lambda j, i, k: (i, 0)),
                 pl.BlockSpec((d, dh2), lambda j, i, k: (0, j)), pl.BlockSpec((tm2, dh2), lambda j, i, k: (i, j)), NN,
                 after=prefetch("ffn", merged))

    def e6(rows, fulls, outs, accs):
        xv, mov = rows[0][...], rows[1][...].astype(f32)
        gt, g, sc, sh = [r[...] for r in fulls]
        x1 = xv + gt * mov
        outs[0][...] = x1
        _, xh = _rms(x1)
        outs[1][...] = (xh * g * (1.0 + sc) + sh).astype(bf16)

    x1, h2 = _rowwise("prenorm_ffn", e6, s, tm, [(x, d, 0), (mo, d, 0)], [gt_m, g_ffn, sc_f, sh_f], [(d, f32), (d, bf16)])
    w_gt, w_ut = get_w("ffn", h2)
    dff_ = w_gt.shape[0]
    act_dg, act_du, act = _ffn_in(h2, w_gt, w_ut, tm2, nb)
    prefetch("ffn_down", act)
    w_d, = get_w("ffn_down", act)
    ff = _matmul("ffn_down", act, w_d, (s, d), bf16, (2, mt, 1), pl.BlockSpec((tm, dff_), lambda j, i, k: (i, 0)),
                 pl.BlockSpec((dff_, dh2), lambda j, i, k: (0, j)), pl.BlockSpec((tm, dh2), lambda j, i, k: (i, j)), NN)

    def e9(rows, fulls, outs, accs):
        x1v, ffv, tv = [r[...].astype(f32) for r in rows]
        gt, gf = fulls[0][...], fulls[1][...]
        x2 = x1v + gt * ffv
        r, xh = _rms(x2)
        err = xh * gf - tv
        accs[0][...] += 0.5 * jnp.sum(jnp.mean(err * err, axis=-1, keepdims=True), axis=0, keepdims=True)
        dy = err / d
        accs[1][...] += _colsum(dy * xh)
        dxh = dy * gf
        dx2 = r * (dxh - xh * jnp.mean(dxh * xh, axis=-1, keepdims=True))
        outs[0][...] = dx2
        outs[1][...] = (dx2 * gt).astype(bf16)
        accs[2][...] += _colsum(dx2 * ffv)

    dx2, dff, loss_acc, dg_fin, dgt_f = _rowwise("loss_head", e9, s, tm, [(x1, d, 0), (ff, d, 0), (tgt, d, 0)], [gt_f, g_fin],
                                                 [(d, f32), (d, bf16)], [(1, 128), (1, d), (1, d)])
    dgg, duu = _ffn_down_bwd(dff, w_d, act_dg, act_du, tm, dff_ // 4)

    def wgrad_rows(name, a, b, n_out):
        kb = 512
        return _matmul(name, a, b, (n_out, d), bf16, (n_out // kb, 2, 1), pl.BlockSpec((s, kb), lambda i, j, k: (0, i)),
                       pl.BlockSpec((s, dh2), lambda i, j, k: (0, j)), pl.BlockSpec((kb, dh2), lambda i, j, k: (i, j)), TN)

    gw_d = wgrad_rows("gw_ffn_down", act, dff, dff_)

    def ffn_in_bwd(name, a, w, after=None):
        return _matmul(name, a, w, (s, d), bf16, (2, mt, 1), pl.BlockSpec((tm, dff_), lambda j, i, k: (i, 0)),
                       pl.BlockSpec((dff_, dh2), lambda j, i, k: (0, j)), pl.BlockSpec((tm, dh2), lambda j, i, k: (i, j)), NN,
                       after=after)

    dh2b = ffn_in_bwd("d_ffn_up_in", duu, w_ut)
    gw_ut = wgrad_rows("gw_ffn_up", duu, h2, dff_)
    gw_gt = wgrad_rows("gw_ffn_gate", dgg, h2, dff_)
    dh2a = ffn_in_bwd("d_ffn_gate_in", dgg, w_gt, after=put_g("ffn", [gw_gt, gw_ut, gw_d]))
    sc_f_late = _after_tokens(sc_f, [sent("ffn", dh2a)])

    def b5(rows, fulls, outs, accs):
        da, db, x1v, dx2v, mov = [r[...].astype(f32) for r in rows]
        sc, g, gt = [r[...] for r in fulls]
        dh = da + db
        r, xh = _rms(x1v)
        accs[0][...] += _colsum(dh)
        accs[1][...] += _colsum(dh * (xh * g))
        dn = dh * (1.0 + sc)
        accs[2][...] += _colsum(dn * xh)
        dxh = dn * g
        dx1 = dx2v + r * (dxh - xh * jnp.mean(dxh * xh, axis=-1, keepdims=True))
        outs[0][...] = dx1
        accs[3][...] += _colsum(dx1 * mov)
        outs[1][...] = (dx1 * gt).astype(bf16)

    dx1, dmo, dsh_f, dsc_f, dg_ffn, dgt_m = _rowwise(
        "d_prenorm_ffn", b5, s, te, [(dh2a, d, 0), (dh2b, d, 0), (x1, d, 0), (dx2, d, 0), (mo, d, 0)], [sc_f_late, g_ffn, gt_m],
        [(d, f32), (d, bf16)], [(1, d)] * 4)
    dya, dyb, dga, dgb = _d_branch_merge(dmo, w_o, merge_factors, tm)
    gw_o = wgrad_rows("gw_mix_out", merged, dmo, d)

    def out_proj_bwd(name, dy, w):
        return _matmul(name, dy, w, (s, dc), bf16, (1, s // tm2, nsh), pl.BlockSpec((tm2, d // nsh), lambda j, i, k: (i, k)),
                       pl.BlockSpec((None, dc, d // nsh), lambda j, i, k: (k, 0, 0)), pl.BlockSpec((tm2, dc), lambda j, i, k: (i, 0)), NT)

    def out_proj_wgrad(name, a, dy):
        return _matmul(name, a, dy, (nsh, dc, d // nsh), bf16, (1, nsh, 1), pl.BlockSpec((s, dc), lambda i, j, k: (0, 0)),
                       pl.BlockSpec((s, d // nsh), lambda i, j, k: (0, j)), pl.BlockSpec((None, dc, d // nsh), lambda i, j, k: (j, 0, 0)), TN)

    dz_a = out_proj_bwd("d_conv_out_in", dya, w_co)
    dog = out_proj_bwd("d_hgrn_out_in", dyb, w_ho)
    gw_co = out_proj_wgrad("gw_conv_out", z_a, dya)
    gw_ho = out_proj_wgrad("gw_hgrn_out", og, dyb)
    conv_w_late = _after_tokens(conv_w, [put_g("mix", [gw_co, gw_ho, gw_o])])
    dab, dac, dax, dconv_w = _conv_bwd(proj, conv_w_late, dz_a, s, dc)
    lb_param_late = _after_tokens(lb_param, [sent("mix", dab)])
    dq, dfl, dvi, dgo, dlb, dgn = _hgrn_bwd(proj, lb_param_late, gnorm, o_saved, states, dog, s, dc, tb)
    dproj = _concat_columns("d_proj", [dab, dac, dax, dq, dfl, dvi, dgo, dga, dgb], te)
    dh = _d_proj_in(dproj, w_in, tm, grad_in(h, dproj, win_sh))

    def b12(rows, fulls, outs, accs):
        dhv, xv, dx1v = [r[...].astype(f32) for r in rows]
        sc, g = fulls[0][...], fulls[1][...]
        r, xh = _rms(xv)
        accs[0][...] += _colsum(dhv)
        accs[1][...] += _colsum(dhv * (xh * g))
        dn = dhv * (1.0 + sc)
        accs[2][...] += _colsum(dn * xh)
        dxh = dn * g
        outs[0][...] = dx1v + r * (dxh - xh * jnp.mean(dxh * xh, axis=-1, keepdims=True))

    dx, dsh_m, dsc_m, dg_mix = _rowwise("d_prenorm_mix", b12, s, tm, [(dh, d, 0), (x, d, 0), (dx1, d, 0)], [sc_m, g_mix],
                                        [(d, f32)], [(1, d)] * 3)
    dmod = [dsh_m, dsc_m, dgt_m, dsh_f, dsc_f, dgt_f]
    small = dict(loss=loss_acc, g_mix=dg_mix, g_ffn=dg_ffn, g_fin=dg_fin, lb=dlb, gnorm=dgn, conv_w=dconv_w)
    return dx, dmod, small


def _ada_fwd(c_all, w_sh, b_sh):
    def body(c_ref, w_ref, b_ref, o_ref):
        cv = c_ref[...]
        ca = (cv * _sigmoid(cv)).astype(bf16)
        o_ref[...] = jnp.dot(ca, w_ref[...].astype(bf16), preferred_element_type=f32) + b_ref[...]

    return pl.pallas_call(body, name="ada_fwd", out_shape=SDS((c_all.shape[0], w_sh.shape[1]), f32),
                          compiler_params=pltpu.CompilerParams(vmem_limit_bytes=V7X_VMEM_LIMIT))(c_all, w_sh, b_sh)


def _ada_wgrad(c_all, dmod_sh):
    def body(c_ref, d_ref, o_ref):
        cv = c_ref[...]
        ca = (cv * _sigmoid(cv)).astype(bf16)
        o_ref[...] = lax.dot_general(ca, d_ref[...].astype(bf16), (TN, ((), ())), preferred_element_type=f32)

    return pl.pallas_call(body, name="ada_wgrad", out_shape=SDS((c_all.shape[1], dmod_sh.shape[1]), f32),
                          compiler_params=pltpu.CompilerParams(vmem_limit_bytes=V7X_VMEM_LIMIT))(c_all, dmod_sh)


def _lb_grad(lb_param, dlb):
    def body(p_ref, d_ref, o_ref):
        p = p_ref[...]
        lb = _sigmoid(p[0:1, :] - p[1:2, :])
        gl = d_ref[...] * lb * (1.0 - lb)
        o_ref[0:1, :] = gl
        o_ref[1:2, :] = -gl

    return pl.pallas_call(body, name="lb_grad", out_shape=SDS(lb_param.shape, f32))(lb_param, dlb)


def _sum_small(gathered):
    def body(g_ref, o_ref):
        acc = g_ref[0]
        for dd in range(1, NDEV):
            acc = acc + g_ref[dd]
        o_ref[...] = acc

    return pl.pallas_call(body, name="sum_small", out_shape=SDS(gathered.shape[1:], f32))(gathered)


def kernel(x, c, w_ada, b_ada, norm_mix_g, w_in, conv_w, lb_param, gnorm_g, w_conv_out, w_hgrn_out, w_o, norm_ffn_g, w_ffn_gate, w_ffn_up, w_ffn_down, norm_final_g, loss_target, m_w_ada, m_b_ada, m_norm_mix_g, m_w_in, m_conv_w, m_lb_param, m_gnorm_g, m_w_conv_out, m_w_hgrn_out, m_w_o, m_norm_ffn_g, m_w_ffn_gate, m_w_ffn_up, m_w_ffn_down, m_norm_final_g, v_w_ada, v_b_ada, v_norm_mix_g, v_w_in, v_conv_w, v_lb_param, v_gnorm_g, v_w_conv_out, v_w_hgrn_out, v_w_o, v_norm_ffn_g, v_w_ffn_gate, v_w_ffn_up, v_w_ffn_down, v_norm_final_g):
    assert lb_param.shape[0] == 2 and w_ada.shape[0] == 1
    s, d = x.shape[1], x.shape[2]
    me = 4 * lax.axis_index("x") + 2 * lax.axis_index("y") + lax.axis_index("c")
    ada_cols = w_ada.shape[2]

    me1 = me.astype(jnp.int32).reshape(1)
    placed_in = [_place_shard("place_in0", me1, w_in[0])[0]]
    c_all, cw_all = _all_gather("gather_cond", [c, conv_w[0]], after=placed_in)
    c_all = c_all.reshape(NDEV, d)
    conv_w_full = jnp.transpose(cw_all, (1, 0, 2)).reshape(conv_w.shape[1], -1)
    b_sh = lax.dynamic_slice_in_dim(b_ada, me * ada_cols, ada_cols, axis=1)
    mod_cols = _ada_fwd(c_all, w_ada[0], b_sh)
    mod_all, = _all_gather("gather_mod", [mod_cols])
    mod = lax.dynamic_index_in_dim(mod_all, me, axis=1, keepdims=False).reshape(N_MOD, 1, d)

    shard_groups = {"mix": [w_conv_out[0], w_hgrn_out[0], w_o[0]], "ffn": [w_ffn_gate[0].T, w_ffn_up[0].T],
                    "ffn_down": [w_ffn_down[0]]}
    own_slot = lambda frm, to: _flat(frm)
    gather_plan = lambda n: [(a, j, own_slot, own_slot) for a in range(n) for j in (1,) + ICI_RELATIONS]
    flat = lambda a: a.reshape(a.shape[0] * a.shape[1], a.shape[2])
    to8 = lambda a: a.reshape(NDEV, a.shape[0] // NDEV, a.shape[1])
    near, far = ICI_RELATIONS[:2], ICI_RELATIONS[2:]
    in_plan = lambda rels: [(0, j, own_slot, own_slot) for j in rels]
    ss_a, rs_a, _, lands, tok_a = _push_start("gather_start_in_sib", [], placed_in, in_plan((1,)), after=mod_all)
    ss_b, rs_b, _, lands_in, tok_b = _push_start("gather_start_in_near", [], lands, in_plan(near), after=tok_a)
    gathering, tokens, placed = {}, [tok_a, tok_b], {}
    for grp, sh in shard_groups.items():
        placed[grp] = []
        for i, a in enumerate(sh):
            buf, tok = _place_shard(f"place_{grp}{i}", me1, a, after=tokens[-1])
            placed[grp].append(buf)
            tokens.append(tok)

    pos = (lax.axis_index("x"), lax.axis_index("y"), lax.axis_index("c"))
    ids = lambda frm, rels: jnp.stack([_flat(_peer(frm, j)) for j in rels]).astype(jnp.int32)

    def project(h, tm):
        sib = _peer(pos, 1)
        fwd_of = lambda idxs: [(0, 1, fwd_slot(i), fwd_slot(i)) for i in idxs]
        _, lands = _push_wait("gather_wait_in_sib", ss_a, rs_a, [], lands_in, in_plan((1,)), h)
        proj = _proj_part("proj_local", h, lands[0], ids(pos, (0, 1)), None, tm)
        _, lands = _push_wait("gather_wait_in_near", ss_b, rs_b, [], lands, in_plan(near), proj)
        ss_c, rs_c, _, lands, tok = _push_start("gather_start_in_far", [], lands, in_plan(far))
        for grp in shard_groups:
            ss, rs, _, bufs, tok = _push_start("gather_start_" + grp, [], placed[grp], gather_plan(len(placed[grp])), after=tok)
            gathering[grp] = (ss, rs, bufs)
        ss1, rs1, _, lands, tok = _push_start("gather_fwd_in_near", [], lands, fwd_of((0, 1)), after=tok)
        proj = _proj_part("proj_near", h, lands[0], ids(pos, near), proj, tm, after=tok)
        _, lands = _push_wait("gather_fwd_wait_in_near", ss1, rs1, [], lands, fwd_of((0, 1)), proj)
        proj = _proj_part("proj_fwd_near", h, lands[0], ids(sib, near), proj, tm)
        _, lands = _push_wait("gather_wait_in_far", ss_c, rs_c, [], lands, in_plan(far), proj)
        ss2, rs2, _, lands, tok = _push_start("gather_fwd_in_far", [], lands, fwd_of((2,)))
        proj = _proj_part("proj_far", h, lands[0], ids(pos, far), proj, tm, after=tok)
        _, lands = _push_wait("gather_fwd_wait_in_far", ss2, rs2, [], lands, fwd_of((2,)), proj)
        proj = _proj_part("proj_fwd_far", h, lands[0], ids(sib, far), proj, tm)
        return proj, lands[0]

    fwd_slot = lambda i: (lambda frm, to: _flat(_peer(frm, ICI_RELATIONS[i])))
    fwd_plan = lambda n: [(a, 1, fwd_slot(i), fwd_slot(i)) for a in range(n) for i in range(len(ICI_RELATIONS))]
    forwarding = {}

    def prefetch(grp, after):
        ss, rs, lands = gathering[grp]
        _, lands = _push_wait("gather_wait_" + grp, ss, rs, [], lands, gather_plan(len(lands)), after)
        ss, rs, _, lands, tok = _push_start("gather_fwd_" + grp, [], lands, fwd_plan(len(lands)))
        forwarding[grp] = (ss, rs, lands)
        return tok

    def get_w(grp, after):
        ss, rs, lands = forwarding[grp]
        _, full = _push_wait("gather_fwd_wait_" + grp, ss, rs, [], lands, fwd_plan(len(lands)), after)
        return [f if i < 2 and grp == "mix" else flat(f) for i, f in enumerate(full)]

    core = lax.axis_index("c").astype(jnp.int32).reshape(1)
    chip = (2 * lax.axis_index("x") + lax.axis_index("y")).astype(jnp.int32).reshape(1)
    scatter_plan = lambda n: [(a, j, lambda frm, to: _chip(to), lambda frm, to: _chip(frm)) for a in range(n) for j in ICI_RELATIONS]
    scattering = {}

    swap_plan = lambda n: [(a, 1, functools.partial(lambda frm, to, q: 2 * q + to[2], q=q), functools.partial(lambda frm, to, q: q, q=q))
                           for a in range(n) for q in range(NDEV // 2)]
    swapping = {}

    def start_ici(grp, g8, recv):
        pairs = [_pair_sum(f"pair_sum_{grp}{i}", core, g, r, _row_tile(g.shape[1], 1024)) for i, (g, r) in enumerate(zip(g8, recv))]
        lands = [lax.empty(p.shape, p.dtype) for p in pairs]
        ss, rs, srcs, lands, tok = _push_start("scatter_start_" + grp, pairs, lands, scatter_plan(len(pairs)))
        scattering[grp] = (ss, rs, srcs, lands)
        return tok

    def grad_in(h, dproj, n):
        halves = [jnp.stack([2 * q + cc for q in range(NDEV // 2)]).astype(jnp.int32) for cc in (1 - pos[2], pos[2])]
        gw = _gw_part("gw_proj_sibling", h, dproj, n, halves[0], None)
        lands = [lax.empty((NDEV // 2,) + gw.shape[1:], gw.dtype)]
        ss, rs, srcs, lands, tok = _push_start("scatter_swap_in", [gw], lands, swap_plan(1))
        gw = _gw_part("gw_proj_own", h, dproj, n, halves[1], srcs[0], after=tok)
        g8, recv = _push_wait("scatter_swapped_in", ss, rs, [gw], lands, swap_plan(1), tok)
        return start_ici("in", g8, recv)

    def put_g(grp, grads):
        g8 = [g if g.ndim == 3 else to8(g) for g in grads]
        lands = [lax.empty((NDEV // 2,) + g.shape[1:], g.dtype) for g in g8]
        ss, rs, srcs, lands, tok = _push_start("scatter_swap_" + grp, g8, lands, swap_plan(len(g8)))
        swapping[grp] = (ss, rs, srcs, lands)
        return tok

    def sent(grp, after):
        ss, rs, srcs, lands = swapping[grp]
        g8, recv = _push_wait("scatter_swapped_" + grp, ss, rs, srcs, lands, swap_plan(len(srcs)), after)
        return start_ici(grp, g8, recv)

    def reduced(grp, after, names):
        ss, rs, srcs, lands = scattering[grp]
        srcs, lands = _push_wait("scatter_wait_" + grp, ss, rs, srcs, lands, scatter_plan(len(srcs)), after)
        for i, (p, r, nm) in enumerate(zip(srcs, lands, names)):
            tr = nm in ("w_ffn_gate", "w_ffn_up")
            wmv = tuple(a[0].T if tr else a[0] for a in weights[nm])
            out = _chip_sum_adamw(f"chip_sum_{grp}{i}", chip, p, r, _row_tile(p.shape[1]), wmv)
            res[nm] = [(a.T if tr else a).reshape(weights[nm][0].shape) for a in out]

    dx, dmod, small = _local_step(x[0], loss_target[0], mod, _after_tokens(norm_mix_g, tokens), norm_ffn_g,
                                  norm_final_g.reshape(1, d), lb_param, gnorm_g, conv_w_full, project, grad_in, get_w, prefetch, put_g, sent)

    pieces = [*dmod, small["g_mix"], small["g_ffn"], small["g_fin"], small["lb"], small["gnorm"], small["loss"],
              small["conv_w"].reshape(1, -1)]
    widths = [p.shape[1] for p in pieces]
    offs = np.concatenate([[0], np.cumsum(widths)])
    packed = jnp.concatenate(pieces, axis=1)
    gathered, = _all_gather("gather_small", [packed])
    summed = _sum_small(gathered)
    part = lambda i: summed[:, offs[i]:offs[i + 1]]
    g_b_ada = summed[:, :N_MOD * d]
    g_norm_mix, g_norm_ffn, g_norm_fin, g_lb_row, g_gnorm, loss_vec, g_convw_flat = [part(i) for i in range(N_MOD, N_MOD + 7)]
    loss = loss_vec[0, 0]
    dmod_all = gathered[:, 0, :N_MOD * d]
    g_w_ada = _ada_wgrad(c_all, lax.dynamic_slice_in_dim(dmod_all, me * ada_cols, ada_cols, axis=1))
    g_lb = _lb_grad(lb_param, g_lb_row)
    cw_cols = conv_w.shape[2]
    g_conv_w = lax.dynamic_slice_in_dim(g_convw_flat.reshape(conv_w.shape[1], -1), me * cw_cols, cw_cols, axis=1)

    grads = dict(w_ada=g_w_ada, b_ada=g_b_ada, norm_mix_g=g_norm_mix, conv_w=g_conv_w, lb_param=g_lb, gnorm_g=g_gnorm,
                 norm_ffn_g=g_norm_ffn, norm_final_g=g_norm_fin)
    weights = dict(w_ada=(w_ada, m_w_ada, v_w_ada), b_ada=(b_ada, m_b_ada, v_b_ada), norm_mix_g=(norm_mix_g, m_norm_mix_g, v_norm_mix_g),
                   w_in=(w_in, m_w_in, v_w_in), conv_w=(conv_w, m_conv_w, v_conv_w), lb_param=(lb_param, m_lb_param, v_lb_param),
                   gnorm_g=(gnorm_g, m_gnorm_g, v_gnorm_g), w_conv_out=(w_conv_out, m_w_conv_out, v_w_conv_out),
                   w_hgrn_out=(w_hgrn_out, m_w_hgrn_out, v_w_hgrn_out), w_o=(w_o, m_w_o, v_w_o),
                   norm_ffn_g=(norm_ffn_g, m_norm_ffn_g, v_norm_ffn_g), w_ffn_gate=(w_ffn_gate, m_w_ffn_gate, v_w_ffn_gate),
                   w_ffn_up=(w_ffn_up, m_w_ffn_up, v_w_ffn_up), w_ffn_down=(w_ffn_down, m_w_ffn_down, v_w_ffn_down),
                   norm_final_g=(norm_final_g, m_norm_final_g, v_norm_final_g))
    res = {}

    def update(nm):
        w, m, v = weights[nm]
        shape2 = (w.shape[-2], w.shape[-1]) if w.ndim >= 2 else (1, w.shape[0])
        g2 = grads[nm].reshape(shape2)
        dl, mn, vn = _adamw("adamw_" + nm, g2, w.reshape(shape2), m.reshape(shape2), v.reshape(shape2))
        res[nm] = [a.reshape(w.shape) for a in (g2, dl, mn, vn)]

    for nm in list(grads):
        update(nm)
    reduced("ffn", res["w_ada"][1], ("w_ffn_gate", "w_ffn_up", "w_ffn_down"))
    reduced("mix", res["w_ffn_down"][1], ("w_conv_out", "w_hgrn_out", "w_o"))
    reduced("in", res["w_o"][1], ("w_in",))
    outs = [[res[nm][i] for nm in weights] for i in range(4)]
    return (loss, dx.reshape(x.shape), *outs[0], *outs[1], *outs[2], *outs[3])
```
